```python
import jax, jax.numpy as jnp
from jax import lax
import numpy as np

D_MODEL = 1024
BATCH = 8
SEQ = 8192
DEPTH = 4

HEAD_DIM = 64
A_HEADS = 8
A_KV_HEADS = 2
B_HEADS = 8
B_KV_HEADS = 2
WINDOW = 128
BLOCK = 128
ROPE_THETA = 10000.0
GRID_W = 64
SGU_WIDTH = D_MODEL
SGU_GROUPS = 8
SGU_CHUNK = 128
D_FF = 4 * D_MODEL
EPS = 1e-6
N_ATT_LAYERS = (DEPTH + 1) // 2
N_SGU_LAYERS = DEPTH // 2

A_Q = A_HEADS * HEAD_DIM
A_KV = A_KV_HEADS * HEAD_DIM
B_Q = B_HEADS * HEAD_DIM
B_KV = B_KV_HEADS * HEAD_DIM
ATT_IN = A_Q + 2 * A_KV + B_Q + 2 * B_KV
ATT_OUT_IN = A_Q + B_Q

kernel_name = "hybrid_window_grid_attn_sgu_encoder"


def _rmsnorm(x, g):
    xf = x.astype(jnp.float32)
    y = xf * lax.rsqrt(jnp.mean(xf * xf, axis=-1, keepdims=True) + EPS)
    return (y * g.astype(jnp.float32)).astype(x.dtype)


def _layernorm(x, g, b):
    xf = x.astype(jnp.float32)
    mu = jnp.mean(xf, axis=-1, keepdims=True)
    var = jnp.mean(jnp.square(xf - mu), axis=-1, keepdims=True)
    y = (xf - mu) * lax.rsqrt(var + EPS)
    return (y * g.astype(jnp.float32) + b.astype(jnp.float32)).astype(x.dtype)


def _rope_angles(pos, dim):
    freqs = ROPE_THETA ** (-jnp.arange(0, dim, 2, dtype=jnp.float32) / dim)
    ang = pos.astype(jnp.float32)[:, None] * freqs[None, :]
    return jnp.cos(ang), jnp.sin(ang)


def _apply_rope(x, cos, sin):
    xf = x.astype(jnp.float32)
    half = xf.shape[-1] // 2
    x1, x2 = xf[..., :half], xf[..., half:]
    c, s = cos[None, :, None, :], sin[None, :, None, :]
    return jnp.concatenate([x1 * c - x2 * s, x2 * c + x1 * s], axis=-1).astype(x.dtype)


def _apply_axial_rope(x, cos_r, sin_r, cos_c, sin_c):
    half = x.shape[-1] // 2
    return jnp.concatenate([_apply_rope(x[..., :half], cos_r, sin_r),
                            _apply_rope(x[..., half:], cos_c, sin_c)], axis=-1)


def _window_attention(q, k, v, sink):
    bsz, s_len = q.shape[0], q.shape[1]
    nb = s_len // BLOCK
    g = A_HEADS // A_KV_HEADS
    scale = HEAD_DIM ** -0.5
    qb = q.reshape(bsz, nb, BLOCK, A_KV_HEADS, g, HEAD_DIM).astype(jnp.float32)
    pad = ((0, 0), (BLOCK, BLOCK), (0, 0), (0, 0))
    kp = jnp.pad(k, pad).reshape(bsz, nb + 2, BLOCK, A_KV_HEADS, HEAD_DIM)
    vp = jnp.pad(v, pad).reshape(bsz, nb + 2, BLOCK, A_KV_HEADS, HEAD_DIM)
    kband = jnp.concatenate([kp[:, :-2], kp[:, 1:-1], kp[:, 2:]], axis=2).astype(jnp.float32)
    vband = jnp.concatenate([vp[:, :-2], vp[:, 1:-1], vp[:, 2:]], axis=2).astype(jnp.float32)
    s = jnp.einsum('bnqhgd,bnjhd->bnhgqj', qb, kband) * scale
    qi = jnp.arange(BLOCK)
    kj = jnp.arange(3 * BLOCK)
    rel = kj[None, :] - BLOCK - qi[:, None]
    kpos = jnp.arange(nb)[:, None] * BLOCK - BLOCK + kj[None, :]
    mask = (jnp.abs(rel) <= WINDOW)[None, :, :] & ((kpos >= 0) & (kpos < s_len))[:, None, :]
    s = jnp.where(mask[None, :, None, None, :, :], s, -1e30)
    sink_b = sink.astype(jnp.float32).reshape(A_KV_HEADS, g)[None, None, :, :, None, None]
    m = jnp.maximum(jnp.max(s, axis=-1, keepdims=True), sink_b)
    p = jnp.exp(s - m)
    denom = jnp.sum(p, axis=-1, keepdims=True) + jnp.exp(sink_b - m)
    o = jnp.einsum('bnhgqj,bnjhd->bnqhgd', p / denom, vband)
    return o.reshape(bsz, s_len, A_Q).astype(q.dtype)


def _grid_attention(q, k, v):
    bsz, s_len = q.shape[0], q.shape[1]
    nb = s_len // BLOCK
    g = B_HEADS // B_KV_HEADS
    scale = HEAD_DIM ** -0.5
    qb = q.reshape(bsz, nb, BLOCK, B_KV_HEADS, g, HEAD_DIM).transpose(1, 0, 2, 3, 4, 5)
    kf = k.astype(jnp.float32)
    vf = v.astype(jnp.float32)

    def one_block(qblk):
        s = jnp.einsum('bqhgd,bkhd->bhgqk', qblk.astype(jnp.float32), kf) * scale
        p = jax.nn.softmax(s, axis=-1)
        return jnp.einsum('bhgqk,bkhd->bqhgd', p, vf)

    o = lax.map(one_block, qb)
    return o.transpose(1, 0, 2, 3, 4, 5).reshape(bsz, s_len, B_Q).astype(q.dtype)


def _attention_layer(x, norm_g, w_in, sink, qn_g, kn_g, w_out,
                     cos1, sin1, cos_r, sin_r, cos_c, sin_c):
    bsz, s_len = x.shape[0], x.shape[1]
    h = _rmsnorm(x, norm_g)
    proj = h @ w_in
    offs = [A_Q, A_Q + A_KV, A_Q + 2 * A_KV, A_Q + 2 * A_KV + B_Q, A_Q + 2 * A_KV + B_Q + B_KV]
    qa, ka, va, qb, kb, vb = jnp.split(proj, offs, axis=-1)
    qa = qa.reshape(bsz, s_len, A_HEADS, HEAD_DIM)
    ka = ka.reshape(bsz, s_len, A_KV_HEADS, HEAD_DIM)
    va = va.reshape(bsz, s_len, A_KV_HEADS, HEAD_DIM)
    qb = qb.reshape(bsz, s_len, B_HEADS, HEAD_DIM)
    kb = kb.reshape(bsz, s_len, B_KV_HEADS, HEAD_DIM)
    vb = vb.reshape(bsz, s_len, B_KV_HEADS, HEAD_DIM)
    qa = _apply_rope(qa, cos1, sin1)
    ka = _apply_rope(ka, cos1, sin1)
    oa = _window_attention(qa, ka, va, sink)
    qb = _apply_axial_rope(_rmsnorm(qb, qn_g), cos_r, sin_r, cos_c, sin_c)
    kb = _apply_axial_rope(_rmsnorm(kb, kn_g), cos_r, sin_r, cos_c, sin_c)
    ob = _grid_attention(qb, kb, vb)
    return x + jnp.concatenate([oa, ob], axis=-1) @ w_out


def _sgu_layer(x, norm_g, w_in, ln_g, ln_b, w_s, b_s, w_out):
    bsz, s_len = x.shape[0], x.shape[1]
    nc = s_len // SGU_CHUNK
    dg = SGU_WIDTH // SGU_GROUPS
    h = _rmsnorm(x, norm_g)
    z = jax.nn.gelu(h @ w_in)
    u, v = jnp.split(z, 2, axis=-1)
    v = _layernorm(v, ln_g, ln_b)
    vb = v.reshape(bsz, nc, SGU_CHUNK, SGU_GROUPS, dg)
    mixed = jnp.einsum('gpq,bnqgd->bnpgd', w_s, vb) + b_s.T[None, None, :, :, None]
    y = u * mixed.reshape(bsz, s_len, SGU_WIDTH)
    return x + y @ w_out


def _mlp(x, norm_g, w1, w2):
    h = _rmsnorm(x, norm_g)
    return x + jnp.square(jax.nn.relu(h @ w1)) @ w2


def _fwd_setup_inputs(seed: int = 0) -> dict:
    key = jax.random.key(seed)
    ks = jax.random.split(key, 20)
    f32 = jnp.float32
    nrm = lambda k, shape, s: jax.random.normal(k, shape, f32) * s
    return {
        "x": jax.random.normal(ks[0], (BATCH, SEQ, D_MODEL), f32),
        "att_norm": 1.0 + nrm(ks[1], (N_ATT_LAYERS, D_MODEL), 0.02),
        "att_w_in": nrm(ks[2], (N_ATT_LAYERS, D_MODEL, ATT_IN), D_MODEL ** -0.5),
        "att_sink": nrm(ks[3], (N_ATT_LAYERS, A_HEADS), 0.5),
        "att_qnorm": 1.0 + nrm(ks[4], (N_ATT_LAYERS, HEAD_DIM), 0.02),
        "att_knorm": 1.0 + nrm(ks[5], (N_ATT_LAYERS, HEAD_DIM), 0.02),
        "att_w_out": nrm(ks[6], (N_ATT_LAYERS, ATT_OUT_IN, D_MODEL), ATT_OUT_IN ** -0.5),
        "sgu_norm": 1.0 + nrm(ks[7], (N_SGU_LAYERS, D_MODEL), 0.02),
        "sgu_w_in": nrm(ks[8], (N_SGU_LAYERS, D_MODEL, 2 * SGU_WIDTH), D_MODEL ** -0.5),
        "sgu_ln_g": 1.0 + nrm(ks[9], (N_SGU_LAYERS, SGU_WIDTH), 0.02),
        "sgu_ln_b": nrm(ks[10], (N_SGU_LAYERS, SGU_WIDTH), 0.02),
        "sgu_w_s": nrm(ks[11], (N_SGU_LAYERS, SGU_GROUPS, SGU_CHUNK, SGU_CHUNK), SGU_CHUNK ** -0.5),
        "sgu_b_s": 1.0 + nrm(ks[12], (N_SGU_LAYERS, SGU_GROUPS, SGU_CHUNK), 0.1),
        "sgu_w_out": nrm(ks[13], (N_SGU_LAYERS, SGU_WIDTH, D_MODEL), SGU_WIDTH ** -0.5),
        "mlp_norm": 1.0 + nrm(ks[14], (DEPTH, D_MODEL), 0.02),
        "mlp_w1": nrm(ks[15], (DEPTH, D_MODEL, D_FF), D_MODEL ** -0.5),
        "mlp_w2": nrm(ks[16], (DEPTH, D_FF, D_MODEL), D_FF ** -0.5),
        "final_norm": 1.0 + nrm(ks[17], (D_MODEL,), 0.02),
    }


def _fwd_reference(x, att_norm, att_w_in, att_sink, att_qnorm, att_knorm, att_w_out,
              sgu_norm, sgu_w_in, sgu_ln_g, sgu_ln_b, sgu_w_s, sgu_b_s, sgu_w_out,
              mlp_norm, mlp_w1, mlp_w2, final_norm):
    s_len = x.shape[1]
    pos = jnp.arange(s_len)
    rows = s_len // GRID_W
    row_idx = jnp.repeat(jnp.arange(rows), GRID_W)
    col_idx = jnp.tile(jnp.arange(GRID_W), rows)
    cos1, sin1 = _rope_angles(pos, HEAD_DIM)
    cos_r, sin_r = _rope_angles(row_idx, HEAD_DIM // 2)
    cos_c, sin_c = _rope_angles(col_idx, HEAD_DIM // 2)
    h = x
    for layer in range(DEPTH):
        i = layer // 2
        if layer % 2 == 0:
            h = _attention_layer(h, att_norm[i], att_w_in[i], att_sink[i], att_qnorm[i],
                                 att_knorm[i], att_w_out[i],
                                 cos1, sin1, cos_r, sin_r, cos_c, sin_c)
        else:
            h = _sgu_layer(h, sgu_norm[i], sgu_w_in[i], sgu_ln_g[i], sgu_ln_b[i],
                           sgu_w_s[i], sgu_b_s[i], sgu_w_out[i])
        h = _mlp(h, mlp_norm[layer], mlp_w1[layer], mlp_w2[layer])
    return _rmsnorm(h, final_norm)


import jax as _jax
import jax.numpy as _jnp

TWIN_FORMAT = 'train_step'
FWD_PARAMS = ['x', 'att_norm', 'att_w_in', 'att_sink', 'att_qnorm', 'att_knorm', 'att_w_out', 'sgu_norm', 'sgu_w_in', 'sgu_ln_g', 'sgu_ln_b', 'sgu_w_s', 'sgu_b_s', 'sgu_w_out', 'mlp_norm', 'mlp_w1', 'mlp_w2', 'final_norm']
TWIN_WEIGHTS = ['att_norm', 'att_w_in', 'att_sink', 'att_qnorm', 'att_knorm', 'att_w_out', 'sgu_norm', 'sgu_w_in', 'sgu_ln_g', 'sgu_ln_b', 'sgu_w_s', 'sgu_b_s', 'sgu_w_out', 'mlp_norm', 'mlp_w1', 'mlp_w2', 'final_norm']
TWIN_DIFF_INPUT = 'x'
TWIN_INPUTS = ['x', 'att_norm', 'att_w_in', 'att_sink', 'att_qnorm', 'att_knorm', 'att_w_out', 'sgu_norm', 'sgu_w_in', 'sgu_ln_g', 'sgu_ln_b', 'sgu_w_s', 'sgu_b_s', 'sgu_w_out', 'mlp_norm', 'mlp_w1', 'mlp_w2', 'final_norm', 'loss_target', 'm_att_norm', 'm_att_w_in', 'm_att_sink', 'm_att_qnorm', 'm_att_knorm', 'm_att_w_out', 'm_sgu_norm', 'm_sgu_w_in', 'm_sgu_ln_g', 'm_sgu_ln_b', 'm_sgu_w_s', 'm_sgu_b_s', 'm_sgu_w_out', 'm_mlp_norm', 'm_mlp_w1', 'm_mlp_w2', 'm_final_norm', 'v_att_norm', 'v_att_w_in', 'v_att_sink', 'v_att_qnorm', 'v_att_knorm', 'v_att_w_out', 'v_sgu_norm', 'v_sgu_w_in', 'v_sgu_ln_g', 'v_sgu_ln_b', 'v_sgu_w_s', 'v_sgu_b_s', 'v_sgu_w_out', 'v_mlp_norm', 'v_mlp_w1', 'v_mlp_w2', 'v_final_norm']
TWIN_OUTPUTS = ['loss', 'grad_x', 'grad_att_norm', 'grad_att_w_in', 'grad_att_sink', 'grad_att_qnorm', 'grad_att_knorm', 'grad_att_w_out', 'grad_sgu_norm', 'grad_sgu_w_in', 'grad_sgu_ln_g', 'grad_sgu_ln_b', 'grad_sgu_w_s', 'grad_sgu_b_s', 'grad_sgu_w_out', 'grad_mlp_norm', 'grad_mlp_w1', 'grad_mlp_w2', 'grad_final_norm', 'delta_att_norm', 'delta_att_w_in', 'delta_att_sink', 'delta_att_qnorm', 'delta_att_knorm', 'delta_att_w_out', 'delta_sgu_norm', 'delta_sgu_w_in', 'delta_sgu_ln_g', 'delta_sgu_ln_b', 'delta_sgu_w_s', 'delta_sgu_b_s', 'delta_sgu_w_out', 'delta_mlp_norm', 'delta_mlp_w1', 'delta_mlp_w2', 'delta_final_norm', 'new_m_att_norm', 'new_m_att_w_in', 'new_m_att_sink', 'new_m_att_qnorm', 'new_m_att_knorm', 'new_m_att_w_out', 'new_m_sgu_norm', 'new_m_sgu_w_in', 'new_m_sgu_ln_g', 'new_m_sgu_ln_b', 'new_m_sgu_w_s', 'new_m_sgu_b_s', 'new_m_sgu_w_out', 'new_m_mlp_norm', 'new_m_mlp_w1', 'new_m_mlp_w2', 'new_m_final_norm', 'new_v_att_norm', 'new_v_att_w_in', 'new_v_att_sink', 'new_v_att_qnorm', 'new_v_att_knorm', 'new_v_att_w_out', 'new_v_sgu_norm', 'new_v_sgu_w_in', 'new_v_sgu_ln_g', 'new_v_sgu_ln_b', 'new_v_sgu_w_s', 'new_v_sgu_b_s', 'new_v_sgu_w_out', 'new_v_mlp_norm', 'new_v_mlp_w1', 'new_v_mlp_w2', 'new_v_final_norm']
TWIN_LEAF_KINDS = {'loss': 'loss', 'grad_x': 'grad_x', 'grad_att_norm': 'grad_w', 'grad_att_w_in': 'grad_w', 'grad_att_sink': 'grad_w', 'grad_att_qnorm': 'grad_w', 'grad_att_knorm': 'grad_w', 'grad_att_w_out': 'grad_w', 'grad_sgu_norm': 'grad_w', 'grad_sgu_w_in': 'grad_w', 'grad_sgu_ln_g': 'grad_w', 'grad_sgu_ln_b': 'grad_w', 'grad_sgu_w_s': 'grad_w', 'grad_sgu_b_s': 'grad_w', 'grad_sgu_w_out': 'grad_w', 'grad_mlp_norm': 'grad_w', 'grad_mlp_w1': 'grad_w', 'grad_mlp_w2': 'grad_w', 'grad_final_norm': 'grad_w', 'delta_att_norm': 'delta_w', 'delta_att_w_in': 'delta_w', 'delta_att_sink': 'delta_w', 'delta_att_qnorm': 'delta_w', 'delta_att_knorm': 'delta_w', 'delta_att_w_out': 'delta_w', 'delta_sgu_norm': 'delta_w', 'delta_sgu_w_in': 'delta_w', 'delta_sgu_ln_g': 'delta_w', 'delta_sgu_ln_b': 'delta_w', 'delta_sgu_w_s': 'delta_w', 'delta_sgu_b_s': 'delta_w', 'delta_sgu_w_out': 'delta_w', 'delta_mlp_norm': 'delta_w', 'delta_mlp_w1': 'delta_w', 'delta_mlp_w2': 'delta_w', 'delta_final_norm': 'delta_w', 'new_m_att_norm': 'new_m', 'new_m_att_w_in': 'new_m', 'new_m_att_sink': 'new_m', 'new_m_att_qnorm': 'new_m', 'new_m_att_knorm': 'new_m', 'new_m_att_w_out': 'new_m', 'new_m_sgu_norm': 'new_m', 'new_m_sgu_w_in': 'new_m', 'new_m_sgu_ln_g': 'new_m', 'new_m_sgu_ln_b': 'new_m', 'new_m_sgu_w_s': 'new_m', 'new_m_sgu_b_s': 'new_m', 'new_m_sgu_w_out': 'new_m', 'new_m_mlp_norm': 'new_m', 'new_m_mlp_w1': 'new_m', 'new_m_mlp_w2': 'new_m', 'new_m_final_norm': 'new_m', 'new_v_att_norm': 'new_v', 'new_v_att_w_in': 'new_v', 'new_v_att_sink': 'new_v', 'new_v_att_qnorm': 'new_v', 'new_v_att_knorm': 'new_v', 'new_v_att_w_out': 'new_v', 'new_v_sgu_norm': 'new_v', 'new_v_sgu_w_in': 'new_v', 'new_v_sgu_ln_g': 'new_v', 'new_v_sgu_ln_b': 'new_v', 'new_v_sgu_w_s': 'new_v', 'new_v_sgu_b_s': 'new_v', 'new_v_sgu_w_out': 'new_v', 'new_v_mlp_norm': 'new_v', 'new_v_mlp_w1': 'new_v', 'new_v_mlp_w2': 'new_v', 'new_v_final_norm': 'new_v'}


def _forward(args):
    return _fwd_reference(*[args[k] for k in FWD_PARAMS])


def _output_shape():
    def fwd():
        inp = _fwd_setup_inputs(0)
        return _fwd_reference(*[inp[k] for k in FWD_PARAMS])
    out = _jax.eval_shape(fwd)
    return out.shape, out.dtype

N_MICROBATCH = 1
ADAM_LR = 0.001
ADAM_B1 = 0.9
ADAM_B2 = 0.999
ADAM_EPS = 1e-08
ADAM_WD = 0.01
ADAM_STEP = 10
PER_EXAMPLE_BATCH_AXIS = {'x': 0, 'loss_target': 0}
SHARED_INPUTS = []
_WEIGHT_DTYPES = {'att_norm': _jnp.float32, 'att_w_in': _jnp.float32, 'att_sink': _jnp.float32, 'att_qnorm': _jnp.float32, 'att_knorm': _jnp.float32, 'att_w_out': _jnp.float32, 'sgu_norm': _jnp.float32, 'sgu_w_in': _jnp.float32, 'sgu_ln_g': _jnp.float32, 'sgu_ln_b': _jnp.float32, 'sgu_w_s': _jnp.float32, 'sgu_b_s': _jnp.float32, 'sgu_w_out': _jnp.float32, 'mlp_norm': _jnp.float32, 'mlp_w1': _jnp.float32, 'mlp_w2': _jnp.float32, 'final_norm': _jnp.float32}
MOMENT_SCALE = {'att_norm': 6.242348e-02, 'att_w_in': 5.267342e-02, 'att_sink': 2.375089e-03, 'att_qnorm': 7.816698e-02, 'att_knorm': 7.285478e-02, 'att_w_out': 7.630065e-02, 'sgu_norm': 1.807983e-01, 'sgu_w_in': 1.161994e-01, 'sgu_ln_g': 8.510725e-02, 'sgu_ln_b': 8.252403e-02, 'sgu_w_s': 8.484940e-02, 'sgu_b_s': 8.402920e-02, 'sgu_w_out': 1.537174e-01, 'mlp_norm': 2.045353e-01, 'mlp_w1': 1.017697e-01, 'mlp_w2': 2.280654e-01, 'final_norm': 6.648483e+01}


def _to_microbatches(a, axis):
    t = _jnp.moveaxis(a, axis, 0)
    t = t.reshape((N_MICROBATCH, t.shape[0] // N_MICROBATCH) + t.shape[1:])
    return _jnp.moveaxis(t, 1, axis + 1)


def setup_inputs(seed: int = 0) -> dict:
    inp = _fwd_setup_inputs(seed)
    key = _jax.random.fold_in(_jax.random.key(seed), 7919)
    shape, _ = _output_shape()
    out = dict(inp)
    out["loss_target"] = _jax.random.normal(_jax.random.fold_in(key, 0), shape, _jnp.float32)
    for i, name in enumerate(TWIN_WEIGHTS):
        w = inp[name].astype(_jnp.float32)
        if MOMENT_SCALE is None:
            s = _jnp.sqrt(_jnp.mean(_jnp.square(w)) + 1e-30)
        else:
            s = MOMENT_SCALE[name]
        km, kv = _jax.random.split(_jax.random.fold_in(key, i + 1))
        out[name] = w
        out["m_" + name] = s * _jax.random.normal(km, w.shape, _jnp.float32)
        out["v_" + name] = (s * s) * _jax.random.uniform(kv, w.shape, _jnp.float32, 0.5, 1.5)
    if N_MICROBATCH > 1:
        for name, axis in PER_EXAMPLE_BATCH_AXIS.items():
            out[name] = _to_microbatches(out[name], axis)
    return {'x': out['x'], 'att_norm': out['att_norm'], 'att_w_in': out['att_w_in'], 'att_sink': out['att_sink'], 'att_qnorm': out['att_qnorm'], 'att_knorm': out['att_knorm'], 'att_w_out': out['att_w_out'], 'sgu_norm': out['sgu_norm'], 'sgu_w_in': out['sgu_w_in'], 'sgu_ln_g': out['sgu_ln_g'], 'sgu_ln_b': out['sgu_ln_b'], 'sgu_w_s': out['sgu_w_s'], 'sgu_b_s': out['sgu_b_s'], 'sgu_w_out': out['sgu_w_out'], 'mlp_norm': out['mlp_norm'], 'mlp_w1': out['mlp_w1'], 'mlp_w2': out['mlp_w2'], 'final_norm': out['final_norm'], 'loss_target': out['loss_target'], 'm_att_norm': out['m_att_norm'], 'm_att_w_in': out['m_att_w_in'], 'm_att_sink': out['m_att_sink'], 'm_att_qnorm': out['m_att_qnorm'], 'm_att_knorm': out['m_att_knorm'], 'm_att_w_out': out['m_att_w_out'], 'm_sgu_norm': out['m_sgu_norm'], 'm_sgu_w_in': out['m_sgu_w_in'], 'm_sgu_ln_g': out['m_sgu_ln_g'], 'm_sgu_ln_b': out['m_sgu_ln_b'], 'm_sgu_w_s': out['m_sgu_w_s'], 'm_sgu_b_s': out['m_sgu_b_s'], 'm_sgu_w_out': out['m_sgu_w_out'], 'm_mlp_norm': out['m_mlp_norm'], 'm_mlp_w1': out['m_mlp_w1'], 'm_mlp_w2': out['m_mlp_w2'], 'm_final_norm': out['m_final_norm'], 'v_att_norm': out['v_att_norm'], 'v_att_w_in': out['v_att_w_in'], 'v_att_sink': out['v_att_sink'], 'v_att_qnorm': out['v_att_qnorm'], 'v_att_knorm': out['v_att_knorm'], 'v_att_w_out': out['v_att_w_out'], 'v_sgu_norm': out['v_sgu_norm'], 'v_sgu_w_in': out['v_sgu_w_in'], 'v_sgu_ln_g': out['v_sgu_ln_g'], 'v_sgu_ln_b': out['v_sgu_ln_b'], 'v_sgu_w_s': out['v_sgu_w_s'], 'v_sgu_b_s': out['v_sgu_b_s'], 'v_sgu_w_out': out['v_sgu_w_out'], 'v_mlp_norm': out['v_mlp_norm'], 'v_mlp_w1': out['v_mlp_w1'], 'v_mlp_w2': out['v_mlp_w2'], 'v_final_norm': out['v_final_norm']}


def _loss(weights, diff, rest, loss_target):
    with _jax.named_scope("forward"):
        args = {**rest, TWIN_DIFF_INPUT: diff, **{k: w.astype(_WEIGHT_DTYPES[k]) for k, w in weights.items()}}
        y = _forward(args)
    with _jax.named_scope("loss_head"):
        err = _jnp.square(y.astype(_jnp.float32) - loss_target)
        return 0.5 * _jnp.sum(_jnp.mean(err, axis=-1)) if err.ndim else 0.5 * err


def _adamw(w, g, m, v):
    m = ADAM_B1 * m + (1.0 - ADAM_B1) * g
    v = ADAM_B2 * v + (1.0 - ADAM_B2) * _jnp.square(g)
    m_hat = m / (1.0 - ADAM_B1 ** ADAM_STEP)
    v_hat = v / (1.0 - ADAM_B2 ** ADAM_STEP)
    delta = -ADAM_LR * (m_hat / (_jnp.sqrt(v_hat) + ADAM_EPS) + ADAM_WD * w)
    return delta, m, v


def reference(x, att_norm, att_w_in, att_sink, att_qnorm, att_knorm, att_w_out, sgu_norm, sgu_w_in, sgu_ln_g, sgu_ln_b, sgu_w_s, sgu_b_s, sgu_w_out, mlp_norm, mlp_w1, mlp_w2, final_norm, loss_target, m_att_norm, m_att_w_in, m_att_sink, m_att_qnorm, m_att_knorm, m_att_w_out, m_sgu_norm, m_sgu_w_in, m_sgu_ln_g, m_sgu_ln_b, m_sgu_w_s, m_sgu_b_s, m_sgu_w_out, m_mlp_norm, m_mlp_w1, m_mlp_w2, m_final_norm, v_att_norm, v_att_w_in, v_att_sink, v_att_qnorm, v_att_knorm, v_att_w_out, v_sgu_norm, v_sgu_w_in, v_sgu_ln_g, v_sgu_ln_b, v_sgu_w_s, v_sgu_b_s, v_sgu_w_out, v_mlp_norm, v_mlp_w1, v_mlp_w2, v_final_norm):
    given = dict(x=x, att_norm=att_norm, att_w_in=att_w_in, att_sink=att_sink, att_qnorm=att_qnorm, att_knorm=att_knorm, att_w_out=att_w_out, sgu_norm=sgu_norm, sgu_w_in=sgu_w_in, sgu_ln_g=sgu_ln_g, sgu_ln_b=sgu_ln_b, sgu_w_s=sgu_w_s, sgu_b_s=sgu_b_s, sgu_w_out=sgu_w_out, mlp_norm=mlp_norm, mlp_w1=mlp_w1, mlp_w2=mlp_w2, final_norm=final_norm, loss_target=loss_target, m_att_norm=m_att_norm, m_att_w_in=m_att_w_in, m_att_sink=m_att_sink, m_att_qnorm=m_att_qnorm, m_att_knorm=m_att_knorm, m_att_w_out=m_att_w_out, m_sgu_norm=m_sgu_norm, m_sgu_w_in=m_sgu_w_in, m_sgu_ln_g=m_sgu_ln_g, m_sgu_ln_b=m_sgu_ln_b, m_sgu_w_s=m_sgu_w_s, m_sgu_b_s=m_sgu_b_s, m_sgu_w_out=m_sgu_w_out, m_mlp_norm=m_mlp_norm, m_mlp_w1=m_mlp_w1, m_mlp_w2=m_mlp_w2, m_final_norm=m_final_norm, v_att_norm=v_att_norm, v_att_w_in=v_att_w_in, v_att_sink=v_att_sink, v_att_qnorm=v_att_qnorm, v_att_knorm=v_att_knorm, v_att_w_out=v_att_w_out, v_sgu_norm=v_sgu_norm, v_sgu_w_in=v_sgu_w_in, v_sgu_ln_g=v_sgu_ln_g, v_sgu_ln_b=v_sgu_ln_b, v_sgu_w_s=v_sgu_w_s, v_sgu_b_s=v_sgu_b_s, v_sgu_w_out=v_sgu_w_out, v_mlp_norm=v_mlp_norm, v_mlp_w1=v_mlp_w1, v_mlp_w2=v_mlp_w2, v_final_norm=v_final_norm)
    weights = {n: given[n] for n in TWIN_WEIGHTS}
    shared = {n: given[n] for n in SHARED_INPUTS}
    per_example = {n: given[n] for n in ['x']}
    grad_fn = _jax.value_and_grad(_loss, argnums=(0, 1))

    def one_microbatch(ex, loss_target):
        ex = dict(ex)
        diff = ex.pop(TWIN_DIFF_INPUT)
        return grad_fn(weights, diff, {**shared, **ex}, loss_target)

    if N_MICROBATCH == 1:
        loss, (grad_w, grad_x) = one_microbatch(per_example, given["loss_target"])
    else:
        def body(carry, xs):
            loss_sum, grad_sum = carry
            l_k, (gw_k, gx_k) = one_microbatch(xs[0], xs[1])
            with _jax.named_scope("update"):
                return (loss_sum + l_k, _jax.tree.map(_jnp.add, grad_sum, gw_k)), gx_k

        init = (_jnp.zeros((), _jnp.float32), _jax.tree.map(_jnp.zeros_like, weights))
        (loss, grad_w), grad_x = _jax.lax.scan(body, init, (per_example, given["loss_target"]))
    with _jax.named_scope("update"):
        delta_w, new_m, new_v = {}, {}, {}
        for n in TWIN_WEIGHTS:
            delta_w[n], new_m[n], new_v[n] = _adamw(weights[n], grad_w[n], given["m_" + n], given["v_" + n])
    return (loss, grad_x, *[grad_w[n] for n in TWIN_WEIGHTS], *[delta_w[n] for n in TWIN_WEIGHTS],
            *[new_m[n] for n in TWIN_WEIGHTS], *[new_v[n] for n in TWIN_WEIGHTS])
```

```python
import functools
import math

import jax
import jax.numpy as jnp
from jax import lax
from jax.experimental import pallas as pl
from jax.experimental.pallas import tpu as pltpu

F32 = jnp.float32
BF16 = jnp.bfloat16

HEAD_DIM = 64
A_HEADS = 8
A_KV_HEADS = 2
B_HEADS = 8
B_KV_HEADS = 2
WINDOW = 128
BLOCK = 128
ROPE_THETA = 10000.0
GRID_W = 64
SGU_GROUPS = 8
SGU_CHUNK = 128
EPS = 1e-6
SCALE = HEAD_DIM ** -0.5
NEG = -1e30

A_Q = A_HEADS * HEAD_DIM
A_KV = A_KV_HEADS * HEAD_DIM
B_Q = B_HEADS * HEAD_DIM
B_KV = B_KV_HEADS * HEAD_DIM
OFF_QA, OFF_KA, OFF_VA = 0, A_Q, A_Q + A_KV
OFF_QB = A_Q + 2 * A_KV
OFF_KB = OFF_QB + B_Q
OFF_VB = OFF_KB + B_KV
ATT_IN = OFF_VB + B_KV

ADAM_LR = 0.001
ADAM_B1 = 0.9
ADAM_B2 = 0.999
ADAM_EPS = 1e-08
ADAM_WD = 0.01
ADAM_STEP = 10

N_DEV = 8
LANES = 128
V7X_VMEM_LIMIT = 56 * 1024 * 1024
FLAT_COLS = 1024


def _cparams(sem, vmem=V7X_VMEM_LIMIT):
    return pltpu.CompilerParams(dimension_semantics=sem, vmem_limit_bytes=vmem)


def _dot_nn(a, b):
    return lax.dot_general(a, b, (((1,), (0,)), ((), ())), preferred_element_type=F32)


def _dot_nt(a, b):
    return lax.dot_general(a, b, (((1,), (1,)), ((), ())), preferred_element_type=F32)


def _dot_tn(a, b):
    return lax.dot_general(a, b, (((0,), (0,)), ((), ())), preferred_element_type=F32)


def _bf(x):
    return x if x.dtype == BF16 else x.astype(BF16)


def _lane(shape):
    return lax.broadcasted_iota(jnp.int32, shape, len(shape) - 1)


def _seg_matrix(rows_lo, rows_hi):
    r = lax.broadcasted_iota(jnp.int32, (LANES, LANES), 0)
    return jnp.where((r >= rows_lo) & (r < rows_hi), 1.0, 0.0).astype(BF16)


def _group_matrix(width):
    r = lax.broadcasted_iota(jnp.int32, (LANES, LANES), 0)
    c = lax.broadcasted_iota(jnp.int32, (LANES, LANES), 1)
    return jnp.where((r // width) == (c // width), 1.0, 0.0).astype(BF16)


def _dot_f32_by_ones(s, ones_bf16):
    hi = s.astype(BF16)
    lo = (s - hi.astype(F32)).astype(BF16)
    return _dot_nn(hi, ones_bf16) + _dot_nn(lo, ones_bf16)


def _swap_halves(x, width):
    half = width // 2
    first = (_lane(x.shape) % width) < half
    return jnp.where(first, pltpu.roll(x, LANES - half, 1), pltpu.roll(x, half, 1))


def rmsnorm_fwd(x, g, name):
    t, d = x.shape
    tm = min(t, 512)

    def body(x_ref, g_ref, h_ref):
        xf = x_ref[...]
        r = lax.rsqrt(jnp.mean(xf * xf, axis=-1, keepdims=True) + EPS)
        h_ref[...] = (xf * r * g_ref[...]).astype(BF16)

    return pl.pallas_call(
        body, name=name, grid=(t // tm,),
        in_specs=[pl.BlockSpec((tm, d), lambda i: (i, 0)), pl.BlockSpec((1, d), lambda i: (0, 0))],
        out_specs=pl.BlockSpec((tm, d), lambda i: (i, 0)),
        out_shape=jax.ShapeDtypeStruct((t, d), BF16),
        compiler_params=_cparams(("parallel",)),
    )(x, g.reshape(1, d))


def _fit(n, want):
    t = min(n, want)
    while n % t:
        t //= 2
    return t


def matmul(a, b, mode, name, out_dtypes, epilogue=None, extras=(), tm=512, tn=1024, tk=512):
    if mode == "nn":
        (m, k), (_, n) = a.shape, b.shape
    elif mode == "nt":
        (m, k), (n, _) = a.shape, b.shape
    else:
        (k, m), (_, n) = a.shape, b.shape
    tm, tn, tk = _fit(m, tm), _fit(n, tn), _fit(k, tk)
    nk = k // tk
    n_ex, n_out = len(extras), len(out_dtypes)
    if epilogue is None:
        epilogue = lambda acc: (acc,)

    def body(*refs):
        a_ref, b_ref = refs[0], refs[1]
        ex_refs = refs[2:2 + n_ex]
        out_refs = refs[2 + n_ex:2 + n_ex + n_out]
        acc_ref = refs[2 + n_ex + n_out]
        kk = pl.program_id(2)
        av, bv = _bf(a_ref[...]), _bf(b_ref[...])
        if mode == "nn":
            part = _dot_nn(av, bv)
        elif mode == "nt":
            part = _dot_nt(av, bv)
        else:
            part = _dot_tn(av, bv)

        @pl.when(kk == 0)
        def _():
            acc_ref[...] = part

        @pl.when(kk > 0)
        def _():
            acc_ref[...] += part

        @pl.when(kk == nk - 1)
        def _():
            outs = epilogue(acc_ref[...], *[r[...] for r in ex_refs])
            for r, o in zip(out_refs, outs):
                r[...] = o.astype(r.dtype)

    if mode == "nn":
        a_spec = pl.BlockSpec((tm, tk), lambda i, j, kk: (i, kk))
        b_spec = pl.BlockSpec((tk, tn), lambda i, j, kk: (kk, j))
    elif mode == "nt":
        a_spec = pl.BlockSpec((tm, tk), lambda i, j, kk: (i, kk))
        b_spec = pl.BlockSpec((tn, tk), lambda i, j, kk: (j, kk))
    else:
        a_spec = pl.BlockSpec((tk, tm), lambda i, j, kk: (kk, i))
        b_spec = pl.BlockSpec((tk, tn), lambda i, j, kk: (kk, j))
    mn_spec = pl.BlockSpec((tm, tn), lambda i, j, kk: (i, j))
    outs = pl.pallas_call(
        body, name=name, grid=(m // tm, n // tn, nk),
        in_specs=[a_spec, b_spec] + [mn_spec] * n_ex,
        out_specs=[mn_spec] * n_out,
        out_shape=[jax.ShapeDtypeStruct((m, n), dt) for dt in out_dtypes],
        scratch_shapes=[pltpu.VMEM((tm, tn), F32)],
        compiler_params=_cparams(("parallel", "parallel", "arbitrary")),
    )(a, b, *extras)
    return outs


def matmul_nt_normbwd(dz, w, x, g, dres, name, tm=512, tk=512):
    m, k = dz.shape
    d = w.shape[0]
    tm, tk = _fit(m, tm), _fit(k, tk)
    nk = k // tk

    def body(dz_ref, w_ref, x_ref, g_ref, dres_ref, dx_ref, dg_ref, acc_ref):
        i, kk = pl.program_id(0), pl.program_id(1)
        part = _dot_nt(_bf(dz_ref[...]), _bf(w_ref[...]))

        @pl.when(kk == 0)
        def _():
            acc_ref[...] = part

        @pl.when(kk > 0)
        def _():
            acc_ref[...] += part

        @pl.when((i == 0) & (kk == 0))
        def _():
            dg_ref[...] = jnp.zeros_like(dg_ref)

        @pl.when(kk == nk - 1)
        def _():
            dh = acc_ref[...]
            xf = x_ref[...]
            r = lax.rsqrt(jnp.mean(xf * xf, axis=-1, keepdims=True) + EPS)
            xhat = xf * r
            dg_ref[...] += jnp.sum(dh * xhat, axis=0, keepdims=True)
            dxh = dh * g_ref[...]
            dx = r * (dxh - xhat * jnp.mean(dxh * xhat, axis=-1, keepdims=True))
            dx_ref[...] = dres_ref[...] + dx

    row = pl.BlockSpec((tm, d), lambda i, kk: (i, 0))
    vec = pl.BlockSpec((1, d), lambda i, kk: (0, 0))
    return pl.pallas_call(
        body, name=name, grid=(m // tm, nk),
        in_specs=[pl.BlockSpec((tm, tk), lambda i, kk: (i, kk)), pl.BlockSpec((d, tk), lambda i, kk: (0, kk)), row, vec, row],
        out_specs=[row, vec],
        out_shape=[jax.ShapeDtypeStruct((m, d), F32), jax.ShapeDtypeStruct((1, d), F32)],
        scratch_shapes=[pltpu.VMEM((tm, d), F32)],
        compiler_params=_cparams(("arbitrary", "arbitrary")),
    )(dz, w, x, g.reshape(1, d), dres)


def _rope_tables(t):
    pos = jnp.arange(t)

    def angles(p, dim):
        freqs = ROPE_THETA ** (-jnp.arange(0, dim, 2, dtype=F32) / dim)
        return p.astype(F32)[:, None] * freqs[None, :]

    a1 = angles(pos, HEAD_DIM)
    cos_a = jnp.concatenate([jnp.cos(a1), jnp.cos(a1)], axis=-1)
    sin_a = jnp.concatenate([-jnp.sin(a1), jnp.sin(a1)], axis=-1)
    ar = angles(pos // GRID_W, HEAD_DIM // 2)
    ac = angles(pos % GRID_W, HEAD_DIM // 2)
    cos_b = jnp.concatenate([jnp.cos(ar), jnp.cos(ar), jnp.cos(ac), jnp.cos(ac)], axis=-1)
    sin_b = jnp.concatenate([-jnp.sin(ar), jnp.sin(ar), -jnp.sin(ac), jnp.sin(ac)], axis=-1)
    two = lambda z: jnp.concatenate([z, z], axis=-1)
    return two(cos_a), two(sin_a), two(cos_b), two(sin_b)


def _headnorm(xs, gmat):
    return lax.rsqrt(_dot_f32_by_ones(xs * xs, gmat) * (1.0 / HEAD_DIM) + EPS)


def qkv_post_fwd(proj, tables, qn_g, kn_g, name):
    t = proj.shape[0]
    tm = min(t, 256)
    cos_a, sin_a, cos_b, sin_b = tables
    g2 = lambda g: jnp.concatenate([g, g]).reshape(1, LANES)

    def body(p_ref, ca_ref, sa_ref, cb_ref, sb_ref, qg_ref, kg_ref, qa_ref, ka_ref, va_ref, qb_ref, kb_ref, vb_ref):
        ca, sa, cb, sb = ca_ref[...], sa_ref[...], cb_ref[...], sb_ref[...]
        gmat = _group_matrix(HEAD_DIM)

        def rope_a(xs):
            return xs * ca + _swap_halves(xs, HEAD_DIM) * sa

        def norm_rope_b(xs, g):
            y = xs * _headnorm(xs, gmat) * g
            return y * cb + _swap_halves(y, HEAD_DIM // 2) * sb

        for c in range(A_Q // LANES):
            qa_ref[:, c * LANES:(c + 1) * LANES] = rope_a(p_ref[:, OFF_QA + c * LANES:OFF_QA + (c + 1) * LANES]).astype(BF16)
        ka_ref[...] = rope_a(p_ref[:, OFF_KA:OFF_KA + LANES]).astype(BF16)
        va_ref[...] = p_ref[:, OFF_VA:OFF_VA + LANES].astype(BF16)
        for c in range(B_Q // LANES):
            qb_ref[:, c * LANES:(c + 1) * LANES] = norm_rope_b(
                p_ref[:, OFF_QB + c * LANES:OFF_QB + (c + 1) * LANES], qg_ref[...]).astype(BF16)
        kb_ref[...] = norm_rope_b(p_ref[:, OFF_KB:OFF_KB + LANES], kg_ref[...]).astype(BF16)
        vb_ref[...] = p_ref[:, OFF_VB:OFF_VB + LANES].astype(BF16)

    tab = pl.BlockSpec((tm, LANES), lambda i: (i, 0))
    vec = pl.BlockSpec((1, LANES), lambda i: (0, 0))
    wide = pl.BlockSpec((tm, A_Q), lambda i: (i, 0))
    return pl.pallas_call(
        body, name=name, grid=(t // tm,),
        in_specs=[pl.BlockSpec((tm, ATT_IN), lambda i: (i, 0)), tab, tab, tab, tab, vec, vec],
        out_specs=[wide, tab, tab, wide, tab, tab],
        out_shape=[jax.ShapeDtypeStruct((t, A_Q), BF16), jax.ShapeDtypeStruct((t, LANES), BF16),
                   jax.ShapeDtypeStruct((t, LANES), BF16), jax.ShapeDtypeStruct((t, B_Q), BF16),
                   jax.ShapeDtypeStruct((t, LANES), BF16), jax.ShapeDtypeStruct((t, LANES), BF16)],
        compiler_params=_cparams(("parallel",)),
    )(proj, cos_a, sin_a, cos_b, sin_b, g2(qn_g), g2(kn_g))


def qkv_post_bwd(proj, tables, qn_g, kn_g, dqa, dka, dva, dqb, dkb, dvb, name):
    t = proj.shape[0]
    tm = min(t, 256)
    cos_a, sin_a, cos_b, sin_b = tables
    g2 = lambda g: jnp.concatenate([g, g]).reshape(1, LANES)

    def body(p_ref, ca_ref, sa_ref, cb_ref, sb_ref, qg_ref, kg_ref, dqa_ref, dka_ref, dva_ref, dqb_ref, dkb_ref, dvb_ref,
             dp_ref, dqg_ref, dkg_ref):
        ca, sa, cb, sb = ca_ref[...], sa_ref[...], cb_ref[...], sb_ref[...]
        gmat = _group_matrix(HEAD_DIM)

        @pl.when(pl.program_id(0) == 0)
        def _():
            dqg_ref[...] = jnp.zeros_like(dqg_ref)
            dkg_ref[...] = jnp.zeros_like(dkg_ref)

        def rope_a_bwd(dy):
            return dy * ca + _swap_halves(dy * sa, HEAD_DIM)

        def norm_rope_b_bwd(dout, xs, g):
            dy = dout * cb + _swap_halves(dout * sb, HEAD_DIM // 2)
            r = _headnorm(xs, gmat)
            xhat = xs * r
            dxh = dy * g
            mean = _dot_f32_by_ones(dxh * xhat, gmat) * (1.0 / HEAD_DIM)
            return r * (dxh - xhat * mean), jnp.sum(dy * xhat, axis=0, keepdims=True)

        for c in range(A_Q // LANES):
            sl = slice(c * LANES, (c + 1) * LANES)
            dp_ref[:, OFF_QA + c * LANES:OFF_QA + (c + 1) * LANES] = rope_a_bwd(dqa_ref[:, sl].astype(F32)).astype(BF16)
        dp_ref[:, OFF_KA:OFF_KA + LANES] = rope_a_bwd(dka_ref[0] + dka_ref[1]).astype(BF16)
        dp_ref[:, OFF_VA:OFF_VA + LANES] = (dva_ref[0] + dva_ref[1]).astype(BF16)
        dqg = jnp.zeros((1, LANES), F32)
        for c in range(B_Q // LANES):
            sl = slice(c * LANES, (c + 1) * LANES)
            dx, dg = norm_rope_b_bwd(dqb_ref[:, sl].astype(F32), p_ref[:, OFF_QB + c * LANES:OFF_QB + (c + 1) * LANES], qg_ref[...])
            dp_ref[:, OFF_QB + c * LANES:OFF_QB + (c + 1) * LANES] = dx.astype(BF16)
            dqg = dqg + dg
        dqg_ref[...] += dqg
        dx, dg = norm_rope_b_bwd(dkb_ref[0] + dkb_ref[1], p_ref[:, OFF_KB:OFF_KB + LANES], kg_ref[...])
        dp_ref[:, OFF_KB:OFF_KB + LANES] = dx.astype(BF16)
        dkg_ref[...] += dg
        dp_ref[:, OFF_VB:OFF_VB + LANES] = (dvb_ref[0] + dvb_ref[1]).astype(BF16)

        @pl.when(pl.program_id(0) == t // tm - 1)
        def _():
            dqg_ref[...] = dqg_ref[...] + pltpu.roll(dqg_ref[...], HEAD_DIM, 1)
            dkg_ref[...] = dkg_ref[...] + pltpu.roll(dkg_ref[...], HEAD_DIM, 1)

    tab = pl.BlockSpec((tm, LANES), lambda i: (i, 0))
    vec = pl.BlockSpec((1, LANES), lambda i: (0, 0))
    wide = pl.BlockSpec((tm, A_Q), lambda i: (i, 0))
    slab = pl.BlockSpec((2, tm, LANES), lambda i: (0, i, 0))
    full = pl.BlockSpec((tm, ATT_IN), lambda i: (i, 0))
    return pl.pallas_call(
        body, name=name, grid=(t // tm,),
        in_specs=[full, tab, tab, tab, tab, vec, vec, wide, slab, slab, wide, slab, slab],
        out_specs=[full, vec, vec],
        out_shape=[jax.ShapeDtypeStruct((t, ATT_IN), BF16), jax.ShapeDtypeStruct((1, LANES), F32),
                   jax.ShapeDtypeStruct((1, LANES), F32)],
        compiler_params=_cparams(("arbitrary",)),
    )(proj, cos_a, sin_a, cos_b, sin_b, g2(qn_g), g2(kn_g), dqa, dka, dva, dqb, dkb, dvb)


def _head_to_half(xs, head_half, kv_half):
    low = _lane(xs.shape) < HEAD_DIM
    kept = jnp.where(low if head_half == 0 else jnp.logical_not(low), xs, 0.0)
    return jnp.where(kv_half == head_half, kept, pltpu.roll(kept, HEAD_DIM, 1))


def _halves_to_heads(r0, r1, kv_half):
    low = _lane(r0.shape) < HEAD_DIM
    a = jnp.where(kv_half == 0, r0, pltpu.roll(r0, HEAD_DIM, 1))
    b = jnp.where(kv_half == 1, r1, pltpu.roll(r1, HEAD_DIM, 1))
    return jnp.where(low, a, b)


def attn_delta(o, do, name):
    t, w = o.shape
    tm = min(t, 512)
    n_heads = w // HEAD_DIM

    def body(o_ref, do_ref, d_ref):
        lo, hi = _seg_matrix(0, HEAD_DIM), _seg_matrix(HEAD_DIM, LANES)
        for c in range(w // LANES):
            sl = slice(c * LANES, (c + 1) * LANES)
            s = o_ref[:, sl].astype(F32) * do_ref[:, sl].astype(F32)
            d_ref[2 * c] = _dot_f32_by_ones(s, lo)
            d_ref[2 * c + 1] = _dot_f32_by_ones(s, hi)

    blk = pl.BlockSpec((tm, w), lambda i: (i, 0))
    return pl.pallas_call(
        body, name=name, grid=(t // tm,),
        in_specs=[blk, blk],
        out_specs=pl.BlockSpec((n_heads, tm, LANES), lambda i: (0, i, 0)),
        out_shape=jax.ShapeDtypeStruct((n_heads, t, LANES), F32),
        compiler_params=_cparams(("parallel",)),
    )(o, do)


def _band_mask(n, t, rows_rep):
    qi = lax.broadcasted_iota(jnp.int32, (BLOCK, 3 * BLOCK), 0)
    kj = lax.broadcasted_iota(jnp.int32, (BLOCK, 3 * BLOCK), 1)
    rel = kj - BLOCK - qi
    kpos = n * BLOCK - BLOCK + kj
    ok = (jnp.abs(rel) <= WINDOW) & (kpos >= 0) & (kpos < t)
    return jnp.concatenate([ok] * rows_rep, axis=0)


def _band_specs(t, pos_of):
    nb = t // BLOCK
    prev = pl.BlockSpec((BLOCK, LANES), lambda *g: (jnp.maximum(pos_of(*g) - 1, 0), 0))
    cur = pl.BlockSpec((BLOCK, LANES), lambda *g: (pos_of(*g), 0))
    nxt = pl.BlockSpec((BLOCK, LANES), lambda *g: (jnp.minimum(pos_of(*g) + 1, nb - 1), 0))
    return [prev, cur, nxt]


def window_attn_fwd(q, k, v, sink, name):
    t = q.shape[0]
    nb = t // BLOCK

    def body(sink_ref, q_ref, kp_ref, kc_ref, kn_ref, vp_ref, vc_ref, vn_ref, o_ref, lse_ref):
        j, n = pl.program_id(0), pl.program_id(1)
        kvh = j // 2
        qf = q_ref[...].astype(F32) * SCALE
        qs = jnp.concatenate([_head_to_half(qf, 0, kvh), _head_to_half(qf, 1, kvh)], axis=0).astype(BF16)
        kband = jnp.concatenate([kp_ref[...], kc_ref[...], kn_ref[...]], axis=0)
        vband = jnp.concatenate([vp_ref[...], vc_ref[...], vn_ref[...]], axis=0)
        s = jnp.where(_band_mask(n, t, 2), _dot_nt(qs, kband), NEG)
        row = lax.broadcasted_iota(jnp.int32, (2 * BLOCK, 1), 0)
        sk = jnp.where(row < BLOCK, sink_ref[2 * j], sink_ref[2 * j + 1])
        m = jnp.maximum(jnp.max(s, axis=-1, keepdims=True), sk)
        p = jnp.exp(s - m)
        denom = jnp.sum(p, axis=-1, keepdims=True) + jnp.exp(sk - m)
        o = _dot_nn(p.astype(BF16), vband) / denom
        o_ref[...] = _halves_to_heads(o[:BLOCK], o[BLOCK:], kvh).astype(BF16)
        lse = jnp.broadcast_to(m + jnp.log(denom), (2 * BLOCK, LANES))
        lse_ref[0] = lse[:BLOCK]
        lse_ref[1] = lse[BLOCK:]

    band = _band_specs(t, lambda j, n: n)
    qspec = pl.BlockSpec((BLOCK, LANES), lambda j, n: (n, j))
    return pl.pallas_call(
        body, name=name, grid=(A_HEADS // 2, nb),
        in_specs=[pl.BlockSpec(memory_space=pltpu.SMEM), qspec] + band + band,
        out_specs=[qspec, pl.BlockSpec((2, BLOCK, LANES), lambda j, n: (j, n, 0))],
        out_shape=[jax.ShapeDtypeStruct((t, A_Q), BF16), jax.ShapeDtypeStruct((A_HEADS, t, LANES), F32)],
        compiler_params=_cparams(("parallel", "parallel")),
    )(sink, q, k, k, k, v, v, v)


def window_attn_bwd(q, k, v, sink, do, lse, delta, name):
    t = q.shape[0]
    nb = t // BLOCK
    grp = A_HEADS // A_KV_HEADS
    gw = grp * HEAD_DIM

    def body(sink_ref, q_ref, do_ref, kp_ref, kc_ref, kn_ref, vp_ref, vc_ref, vn_ref, lse_ref, dl_ref,
             dq_ref, dk_ref, dv_ref, ds_ref):
        kvh, n = pl.program_id(0), pl.program_id(1)

        @pl.when(n == 0)
        def _():
            dk_ref[...] = jnp.zeros_like(dk_ref)
            dv_ref[...] = jnp.zeros_like(dv_ref)
            ds_ref[...] = jnp.zeros_like(ds_ref)

        qparts, doparts = [], []
        for hh in range(grp):
            sl = slice((hh // 2) * LANES, (hh // 2 + 1) * LANES)
            qparts.append(_head_to_half(q_ref[:, sl].astype(F32) * SCALE, hh % 2, kvh))
            doparts.append(_head_to_half(do_ref[:, sl].astype(F32), hh % 2, kvh))
        qs = jnp.concatenate(qparts, axis=0).astype(BF16)
        dos = jnp.concatenate(doparts, axis=0).astype(BF16)
        lse_b = jnp.concatenate([lse_ref[hh] for hh in range(grp)], axis=0)
        dl_b = jnp.concatenate([dl_ref[hh] for hh in range(grp)], axis=0)
        kband = jnp.concatenate([kp_ref[...], kc_ref[...], kn_ref[...]], axis=0)
        vband = jnp.concatenate([vp_ref[...], vc_ref[...], vn_ref[...]], axis=0)
        s = jnp.where(_band_mask(n, t, grp), _dot_nt(qs, kband), NEG)
        p = jnp.exp(s - lse_b[:, :1])
        dp = _dot_nt(dos, vband)
        dsc = (p * (dp - dl_b[:, :1])).astype(BF16)
        dvb = _dot_tn(p.astype(BF16), dos)
        dkb = _dot_tn(dsc, qs)
        dq = _dot_nn(dsc, kband) * SCALE
        for c in range(grp // 2):
            dq_ref[:, c * LANES:(c + 1) * LANES] = _halves_to_heads(
                dq[2 * c * BLOCK:(2 * c + 1) * BLOCK], dq[(2 * c + 1) * BLOCK:(2 * c + 2) * BLOCK], kvh).astype(dq_ref.dtype)
        blocks = [jnp.maximum(n - 1, 0), n, jnp.minimum(n + 1, nb - 1)]
        for b3, blk in enumerate(blocks):
            rows = pl.ds(pl.multiple_of(blk * BLOCK, BLOCK), BLOCK)
            dk_ref[0, rows, :] += dkb[b3 * BLOCK:(b3 + 1) * BLOCK]
            dv_ref[0, rows, :] += dvb[b3 * BLOCK:(b3 + 1) * BLOCK]
        rid = lax.broadcasted_iota(jnp.int32, (8, LANES), 0)
        upd = jnp.zeros((8, LANES), F32)
        for hh in range(grp):
            sk = sink_ref[kvh * grp + hh]
            rs = slice(hh * BLOCK, (hh + 1) * BLOCK)
            tot = jnp.sum(jnp.exp(sk - lse_b[rs]) * dl_b[rs], axis=0, keepdims=True)
            upd = upd + jnp.where(rid == hh, -tot, 0.0)
        ds_ref[0] += upd

    band = _band_specs(t, lambda kvh, n: n)
    qspec = pl.BlockSpec((BLOCK, gw), lambda kvh, n: (n, kvh))
    stat = pl.BlockSpec((grp, BLOCK, LANES), lambda kvh, n: (kvh, n, 0))
    slab = pl.BlockSpec((1, t, LANES), lambda kvh, n: (kvh, 0, 0))
    return pl.pallas_call(
        body, name=name, grid=(A_KV_HEADS, nb),
        in_specs=[pl.BlockSpec(memory_space=pltpu.SMEM), qspec, qspec] + band + band + [stat, stat],
        out_specs=[qspec, slab, slab, pl.BlockSpec((1, 8, LANES), lambda kvh, n: (kvh, 0, 0))],
        out_shape=[jax.ShapeDtypeStruct((t, A_Q), BF16), jax.ShapeDtypeStruct((A_KV_HEADS, t, LANES), F32),
                   jax.ShapeDtypeStruct((A_KV_HEADS, t, LANES), F32), jax.ShapeDtypeStruct((A_KV_HEADS, 8, LANES), F32)],
        compiler_params=_cparams(("arbitrary", "arbitrary")),
    )(sink, q, do, k, k, k, v, v, v, lse, delta)


def flash_attn_fwd(q, k, v, name, tq=256, tk=512):
    t = q.shape[0]
    tq, tk = _fit(t, tq), _fit(t, tk)
    nk = t // tk

    def body(q_ref, k_ref, v_ref, o_ref, lse_ref, qs_ref, m_ref, l_ref, acc_ref):
        j, kk = pl.program_id(0), pl.program_id(2)
        kvh = j // 2

        @pl.when(kk == 0)
        def _():
            qf = q_ref[...].astype(F32) * SCALE
            qs_ref[:tq] = _head_to_half(qf, 0, kvh).astype(BF16)
            qs_ref[tq:] = _head_to_half(qf, 1, kvh).astype(BF16)
            m_ref[...] = jnp.full_like(m_ref, NEG)
            l_ref[...] = jnp.zeros_like(l_ref)
            acc_ref[...] = jnp.zeros_like(acc_ref)

        s = _dot_nt(qs_ref[...], k_ref[...])
        m_prev = m_ref[...]
        m_new = jnp.maximum(m_prev, jnp.max(s, axis=-1, keepdims=True))
        alpha = jnp.exp(m_prev - m_new)
        p = jnp.exp(s - m_new[:, :1])
        l_ref[...] = alpha * l_ref[...] + jnp.sum(p, axis=-1, keepdims=True)
        acc_ref[...] = alpha * acc_ref[...] + _dot_nn(p.astype(BF16), v_ref[...])
        m_ref[...] = m_new

        @pl.when(kk == nk - 1)
        def _():
            o = acc_ref[...] / l_ref[...]
            o_ref[...] = _halves_to_heads(o[:tq], o[tq:], kvh).astype(BF16)
            lse = m_ref[...] + jnp.log(l_ref[...])
            lse_ref[0] = lse[:tq]
            lse_ref[1] = lse[tq:]

    qspec = pl.BlockSpec((tq, LANES), lambda j, i, kk: (i, j))
    kspec = pl.BlockSpec((tk, LANES), lambda j, i, kk: (kk, 0))
    return pl.pallas_call(
        body, name=name, grid=(B_HEADS // 2, t // tq, nk),
        in_specs=[qspec, kspec, kspec],
        out_specs=[qspec, pl.BlockSpec((2, tq, LANES), lambda j, i, kk: (j, i, 0))],
        out_shape=[jax.ShapeDtypeStruct((t, B_Q), BF16), jax.ShapeDtypeStruct((B_HEADS, t, LANES), F32)],
        scratch_shapes=[pltpu.VMEM((2 * tq, LANES), BF16), pltpu.VMEM((2 * tq, LANES), F32),
                        pltpu.VMEM((2 * tq, LANES), F32), pltpu.VMEM((2 * tq, LANES), F32)],
        compiler_params=_cparams(("parallel", "parallel", "arbitrary")),
    )(q, k, v)


def flash_attn_bwd(q, k, v, do, lse, delta, name, tq=256, tk=512):
    t = q.shape[0]
    tq, tk = _fit(t, tq), _fit(t, tk)
    nk = t // tk
    grp = B_HEADS // B_KV_HEADS
    gw = grp * HEAD_DIM

    def body(q_ref, do_ref, k_ref, v_ref, lse_ref, dl_ref, dq_ref, dk_ref, dv_ref, qs_ref, dos_ref, lse_s, dl_s, dq_s):
        kvh, i, kk = pl.program_id(0), pl.program_id(1), pl.program_id(2)

        @pl.when((i == 0) & (kk == 0))
        def _():
            dk_ref[...] = jnp.zeros_like(dk_ref)
            dv_ref[...] = jnp.zeros_like(dv_ref)

        @pl.when(kk == 0)
        def _():
            for hh in range(grp):
                sl = slice((hh // 2) * LANES, (hh // 2 + 1) * LANES)
                rs = slice(hh * tq, (hh + 1) * tq)
                qs_ref[rs] = _head_to_half(q_ref[:, sl].astype(F32) * SCALE, hh % 2, kvh).astype(BF16)
                dos_ref[rs] = _head_to_half(do_ref[:, sl].astype(F32), hh % 2, kvh).astype(BF16)
                lse_s[rs] = lse_ref[hh]
                dl_s[rs] = dl_ref[hh]
            dq_s[...] = jnp.zeros_like(dq_s)

        qs, dos, kb, vb = qs_ref[...], dos_ref[...], k_ref[...], v_ref[...]
        p = jnp.exp(_dot_nt(qs, kb) - lse_s[:, :1])
        dp = _dot_nt(dos, vb)
        dsc = (p * (dp - dl_s[:, :1])).astype(BF16)
        rows = pl.ds(pl.multiple_of(kk * tk, tk), tk)
        dv_ref[0, rows, :] += _dot_tn(p.astype(BF16), dos)
        dk_ref[0, rows, :] += _dot_tn(dsc, qs)
        dq_s[...] += _dot_nn(dsc, kb)

        @pl.when(kk == nk - 1)
        def _():
            for c in range(grp // 2):
                dq_ref[:, c * LANES:(c + 1) * LANES] = (_halves_to_heads(
                    dq_s[2 * c * tq:(2 * c + 1) * tq], dq_s[(2 * c + 1) * tq:(2 * c + 2) * tq], kvh) * SCALE).astype(dq_ref.dtype)

    qspec = pl.BlockSpec((tq, gw), lambda kvh, i, kk: (i, kvh))
    kspec = pl.BlockSpec((tk, LANES), lambda kvh, i, kk: (kk, 0))
    stat = pl.BlockSpec((grp, tq, LANES), lambda kvh, i, kk: (kvh, i, 0))
    slab = pl.BlockSpec((1, t, LANES), lambda kvh, i, kk: (kvh, 0, 0))
    return pl.pallas_call(
        body, name=name, grid=(B_KV_HEADS, t // tq, nk),
        in_specs=[qspec, qspec, kspec, kspec, stat, stat],
        out_specs=[qspec, slab, slab],
        out_shape=[jax.ShapeDtypeStruct((t, B_Q), BF16), jax.ShapeDtypeStruct((B_KV_HEADS, t, LANES), F32),
                   jax.ShapeDtypeStruct((B_KV_HEADS, t, LANES), F32)],
        scratch_shapes=[pltpu.VMEM((grp * tq, LANES), BF16), pltpu.VMEM((grp * tq, LANES), BF16),
                        pltpu.VMEM((grp * tq, LANES), F32), pltpu.VMEM((grp * tq, LANES), F32),
                        pltpu.VMEM((grp * tq, LANES), F32)],
        compiler_params=_cparams(("arbitrary", "arbitrary", "arbitrary")),
    )(q, do, k, v, lse, delta)


_GELU_C = math.sqrt(2.0 / math.pi)
_GELU_A = 0.044715


def _gelu(x):
    return 0.5 * x * (1.0 + jnp.tanh(_GELU_C * (x + _GELU_A * x * x * x)))


def _gelu_grad(x):
    th = jnp.tanh(_GELU_C * (x + _GELU_A * x * x * x))
    return 0.5 * (1.0 + th) + 0.5 * x * (1.0 - th * th) * _GELU_C * (1.0 + 3.0 * _GELU_A * x * x)


def _layernorm_stats(vf):
    mu = jnp.mean(vf, axis=-1, keepdims=True)
    vc = vf - mu
    r = lax.rsqrt(jnp.mean(vc * vc, axis=-1, keepdims=True) + EPS)
    return vc * r, r


def sgu_mix_fwd(z, ln_g, ln_b, w_s, b_rows, name):
    t, w2 = z.shape
    w = w2 // 2
    dg = w // SGU_GROUPS

    def body(u_ref, v_ref, g_ref, b_ref, ws_ref, bb_ref, y_ref):
        vhat, _ = _layernorm_stats(v_ref[...].astype(F32))
        vn = (vhat * g_ref[...] + b_ref[...]).astype(BF16)
        for g in range(SGU_GROUPS):
            sl = slice(g * dg, (g + 1) * dg)
            mixed = _dot_nn(ws_ref[g], vn[:, sl]) + bb_ref[g]
            y_ref[:, sl] = (u_ref[:, sl].astype(F32) * mixed).astype(BF16)

    vec = pl.BlockSpec((1, w), lambda n: (0, 0))
    whole = pl.BlockSpec((SGU_GROUPS, SGU_CHUNK, SGU_CHUNK), lambda n: (0, 0, 0))
    return pl.pallas_call(
        body, name=name, grid=(t // SGU_CHUNK,),
        in_specs=[pl.BlockSpec((SGU_CHUNK, w), lambda n: (n, 0)), pl.BlockSpec((SGU_CHUNK, w), lambda n: (n, 1)),
                  vec, vec, whole, whole],
        out_specs=pl.BlockSpec((SGU_CHUNK, w), lambda n: (n, 0)),
        out_shape=jax.ShapeDtypeStruct((t, w), BF16),
        compiler_params=_cparams(("parallel",)),
    )(z, z, ln_g.reshape(1, w), ln_b.reshape(1, w), w_s, b_rows)


def sgu_mix_bwd(z, apre, dy, ln_g, ln_b, w_s, b_rows, name):
    t, w2 = z.shape
    w = w2 // 2
    dg = w // SGU_GROUPS

    def body(u_ref, v_ref, au_ref, av_ref, dy_ref, g_ref, b_ref, ws_ref, bb_ref, da_ref, dlg_ref, dlb_ref, dws_ref, dbs_ref):
        @pl.when(pl.program_id(0) == 0)
        def _():
            dlg_ref[...] = jnp.zeros_like(dlg_ref)
            dlb_ref[...] = jnp.zeros_like(dlb_ref)
            dws_ref[...] = jnp.zeros_like(dws_ref)
            dbs_ref[...] = jnp.zeros_like(dbs_ref)

        vhat, r = _layernorm_stats(v_ref[...].astype(F32))
        gam = g_ref[...]
        vn = (vhat * gam + b_ref[...]).astype(BF16)
        ones8 = jnp.ones((8, dg), BF16)
        rid = lax.broadcasted_iota(jnp.int32, (8, SGU_CHUNK), 0)
        dbs = jnp.zeros((8, SGU_CHUNK), F32)
        dvn_parts = []
        for g in range(SGU_GROUPS):
            sl = slice(g * dg, (g + 1) * dg)
            dyg = dy_ref[:, sl].astype(F32)
            mixed = _dot_nn(ws_ref[g], vn[:, sl]) + bb_ref[g]
            da_ref[:, sl] = (dyg * mixed * _gelu_grad(au_ref[:, sl].astype(F32))).astype(BF16)
            dmix = dyg * u_ref[:, sl].astype(F32)
            dm_hi = dmix.astype(BF16)
            dm_lo = (dmix - dm_hi.astype(F32)).astype(BF16)
            dws_ref[g] += _dot_nt(dm_hi, vn[:, sl])
            dbs = dbs + jnp.where(rid == g, _dot_nt(ones8, dm_hi) + _dot_nt(ones8, dm_lo), 0.0)
            dvn_parts.append(_dot_tn(ws_ref[g], dm_hi))
        dbs_ref[...] += dbs
        dvn = jnp.concatenate(dvn_parts, axis=1)
        dlg_ref[...] += jnp.sum(dvn * vhat, axis=0, keepdims=True)
        dlb_ref[...] += jnp.sum(dvn, axis=0, keepdims=True)
        dvh = dvn * gam
        dv = r * (dvh - jnp.mean(dvh, axis=-1, keepdims=True) - vhat * jnp.mean(dvh * vhat, axis=-1, keepdims=True))
        da_ref[:, w:] = (dv * _gelu_grad(av_ref[...].astype(F32))).astype(BF16)

    vec = pl.BlockSpec((1, w), lambda n: (0, 0))
    whole = pl.BlockSpec((SGU_GROUPS, SGU_CHUNK, SGU_CHUNK), lambda n: (0, 0, 0))
    left = pl.BlockSpec((SGU_CHUNK, w), lambda n: (n, 0))
    right = pl.BlockSpec((SGU_CHUNK, w), lambda n: (n, 1))
    return pl.pallas_call(
        body, name=name, grid=(t // SGU_CHUNK,),
        in_specs=[left, right, left, right, left, vec, vec, whole, whole],
        out_specs=[pl.BlockSpec((SGU_CHUNK, w2), lambda n: (n, 0)), vec, vec, whole,
                   pl.BlockSpec((SGU_GROUPS, SGU_CHUNK), lambda n: (0, 0))],
        out_shape=[jax.ShapeDtypeStruct((t, w2), BF16), jax.ShapeDtypeStruct((1, w), F32), jax.ShapeDtypeStruct((1, w), F32),
                   jax.ShapeDtypeStruct((SGU_GROUPS, SGU_CHUNK, SGU_CHUNK), F32),
                   jax.ShapeDtypeStruct((SGU_GROUPS, SGU_CHUNK), F32)],
        compiler_params=_cparams(("arbitrary",)),
    )(z, z, apre, apre, dy, ln_g.reshape(1, w), ln_b.reshape(1, w), w_s, b_rows)


def loss_head(h, g, target, name):
    t, d = h.shape
    tm = min(t, 512)

    def body(h_ref, g_ref, t_ref, loss_ref, dh_ref, dg_ref):
        @pl.when(pl.program_id(0) == 0)
        def _():
            loss_ref[...] = jnp.zeros_like(loss_ref)
            dg_ref[...] = jnp.zeros_like(dg_ref)

        xf = h_ref[...]
        r = lax.rsqrt(jnp.mean(xf * xf, axis=-1, keepdims=True) + EPS)
        xhat = xf * r
        err = xhat * g_ref[...] - t_ref[...]
        per_tok = jnp.mean(err * err, axis=-1, keepdims=True)
        loss_ref[...] += 0.5 * jnp.sum(per_tok, axis=0, keepdims=True)
        dy = err * (1.0 / d)
        dg_ref[...] += jnp.sum(dy * xhat, axis=0, keepdims=True)
        dxh = dy * g_ref[...]
        dh_ref[...] = r * (dxh - xhat * jnp.mean(dxh * xhat, axis=-1, keepdims=True))

    row = pl.BlockSpec((tm, d), lambda i: (i, 0))
    vec = pl.BlockSpec((1, d), lambda i: (0, 0))
    return pl.pallas_call(
        body, name=name, grid=(t // tm,),
        in_specs=[row, vec, row],
        out_specs=[pl.BlockSpec((1, LANES), lambda i: (0, 0)), row, vec],
        out_shape=[jax.ShapeDtypeStruct((1, LANES), F32), jax.ShapeDtypeStruct((t, d), F32), jax.ShapeDtypeStruct((1, d), F32)],
        compiler_params=_cparams(("arbitrary",)),
    )(h, g.reshape(1, d), target)


def adamw(parts, w, m, v, name, rows=256):
    n, r, c = parts.shape
    tr = _fit(r, rows)
    bc1 = 1.0 - ADAM_B1 ** ADAM_STEP
    bc2 = 1.0 - ADAM_B2 ** ADAM_STEP

    def body(p_ref, w_ref, m_ref, v_ref, g_ref, d_ref, nm_ref, nv_ref):
        g = p_ref[0].astype(F32)
        for j in range(1, n):
            g = g + p_ref[j].astype(F32)
        nm = ADAM_B1 * m_ref[...] + (1.0 - ADAM_B1) * g
        nv = ADAM_B2 * v_ref[...] + (1.0 - ADAM_B2) * (g * g)
        g_ref[...] = g
        nm_ref[...] = nm
        nv_ref[...] = nv
        d_ref[...] = -ADAM_LR * ((nm / bc1) / (jnp.sqrt(nv / bc2) + ADAM_EPS) + ADAM_WD * w_ref[...])

    blk = pl.BlockSpec((tr, c), lambda i: (i, 0))
    return pl.pallas_call(
        body, name=name, grid=(r // tr,),
        in_specs=[pl.BlockSpec((n, tr, c), lambda i: (0, i, 0)), blk, blk, blk],
        out_specs=[blk] * 4,
        out_shape=[jax.ShapeDtypeStruct((r, c), F32)] * 4,
        compiler_params=_cparams(("parallel",)),
    )(parts, w, m, v)


_ANY = pl.BlockSpec(memory_space=pl.ANY)


def _mesh_pos():
    return lax.axis_index("x"), lax.axis_index("y"), lax.axis_index("c")


def all_gather(arrs, name):
    n_arr = len(arrs)

    def body(*refs):
        in_refs, out_refs = refs[:n_arr], refs[n_arr:2 * n_arr]
        send_sems, recv_sems, local_sems = refs[2 * n_arr:]
        x, y, c = _mesh_pos()
        me, sibling = (x, y, c), (x, y, 1 - c)
        chips = [(1 - x, y), (x, 1 - y), (1 - x, 1 - y)]

        def copy(a, k, block, to, from_input=False):
            slot = out_refs[a].at[4 * block[0] + 2 * block[1] + block[2]]
            return pltpu.make_async_remote_copy(
                src_ref=in_refs[a] if from_input else slot, dst_ref=slot,
                send_sem=send_sems.at[7 * a + k], recv_sem=recv_sems.at[7 * a + k],
                device_id=to, device_id_type=pl.DeviceIdType.MESH)

        mine = [pltpu.make_async_copy(in_refs[a], out_refs[a].at[4 * x + 2 * y + c], local_sems.at[a]) for a in range(n_arr)]
        for cp in mine:
            cp.start()
        first = []
        for a in range(n_arr):
            first.append(copy(a, 0, me, sibling, True))
            first += [copy(a, 1 + j, me, (*chip, c), True) for j, chip in enumerate(chips)]
        for cp in first:
            cp.start()
        passed = []
        for a in range(n_arr):
            for j, chip in enumerate(chips):
                copy(a, 1 + j, (*chip, c), me).wait_recv()
                fwd = copy(a, 4 + j, (*chip, c), sibling)
                fwd.start()
                passed.append(fwd)
        for a in range(n_arr):
            copy(a, 0, sibling, me).wait_recv()
            for j, chip in enumerate(chips):
                copy(a, 4 + j, (*chip, 1 - c), me).wait_recv()
        for cp in first + passed:
            cp.wait_send()
        for cp in mine:
            cp.wait()

    return pl.pallas_call(
        body, name=name,
        in_specs=[_ANY] * n_arr, out_specs=[_ANY] * n_arr,
        out_shape=[jax.ShapeDtypeStruct((N_DEV,) + a.shape, a.dtype) for a in arrs],
        scratch_shapes=[pltpu.SemaphoreType.DMA((7 * n_arr,)), pltpu.SemaphoreType.DMA((7 * n_arr,)),
                        pltpu.SemaphoreType.DMA((n_arr,))],
    )(*arrs)


def exchange_partials(scatter, bcast, name):
    def body(sc_ref, bc_ref, sc_out, bc_out, send_sems, recv_sems, local_sems):
        x, y, c = _mesh_pos()
        me = 4 * x + 2 * y + c

        def peer(k):
            px = 1 - x if k & 4 else x
            py = 1 - y if k & 2 else y
            pc = 1 - c if k & 1 else c
            return px, py, pc

        def copies(k):
            px, py, pc = peer(k)
            pid = 4 * px + 2 * py + pc
            kw = dict(device_id=(px, py, pc), device_id_type=pl.DeviceIdType.MESH)
            big = pltpu.make_async_remote_copy(src_ref=sc_ref.at[pid], dst_ref=sc_out.at[me],
                                               send_sem=send_sems.at[2 * (k - 1)], recv_sem=recv_sems.at[2 * (k - 1)], **kw)
            small = pltpu.make_async_remote_copy(src_ref=bc_ref, dst_ref=bc_out.at[me],
                                                 send_sem=send_sems.at[2 * (k - 1) + 1], recv_sem=recv_sems.at[2 * (k - 1) + 1], **kw)
            return big, small

        def arrivals(k):
            px, py, pc = peer(k)
            pid = 4 * px + 2 * py + pc
            kw = dict(device_id=(px, py, pc), device_id_type=pl.DeviceIdType.MESH)
            big = pltpu.make_async_remote_copy(src_ref=sc_ref.at[pid], dst_ref=sc_out.at[pid],
                                               send_sem=send_sems.at[2 * (k - 1)], recv_sem=recv_sems.at[2 * (k - 1)], **kw)
            small = pltpu.make_async_remote_copy(src_ref=bc_ref, dst_ref=bc_out.at[pid],
                                                 send_sem=send_sems.at[2 * (k - 1) + 1], recv_sem=recv_sems.at[2 * (k - 1) + 1], **kw)
            return big, small

        own = [pltpu.make_async_copy(sc_ref.at[me], sc_out.at[me], local_sems.at[0]),
               pltpu.make_async_copy(bc_ref, bc_out.at[me], local_sems.at[1])]
        for cp in own:
            cp.start()
        sent = []
        for k in range(1, N_DEV):
            for cp in copies(k):
                cp.start()
                sent.append(cp)
        for k in range(1, N_DEV):
            for cp in arrivals(k):
                cp.wait_recv()
        for cp in sent:
            cp.wait_send()
        for cp in own:
            cp.wait()

    return pl.pallas_call(
        body, name=name,
        in_specs=[_ANY, _ANY], out_specs=[_ANY, _ANY],
        out_shape=[jax.ShapeDtypeStruct(scatter.shape, scatter.dtype),
                   jax.ShapeDtypeStruct((N_DEV,) + bcast.shape, bcast.dtype)],
        scratch_shapes=[pltpu.SemaphoreType.DMA((14,)), pltpu.SemaphoreType.DMA((14,)), pltpu.SemaphoreType.DMA((2,))],
    )(scatter, bcast)


def attention_fwd(x, norm_g, w_in, sink, qn_g, kn_g, w_out, tables, tag):
    h = rmsnorm_fwd(x, norm_g, f"{tag}_norm")
    (proj,) = matmul(h, w_in, "nn", f"{tag}_proj", [F32], tn=ATT_IN, tk=1024)
    qa, ka, va, qb, kb, vb = qkv_post_fwd(proj, tables, qn_g, kn_g, f"{tag}_qkv")
    oa, lse_a = window_attn_fwd(qa, ka, va, sink, f"{tag}_win")
    ob, lse_b = flash_attn_fwd(qb, kb, vb, f"{tag}_flash")
    cat = jnp.concatenate([oa, ob], axis=1)
    (y,) = matmul(cat, w_out, "nn", f"{tag}_out", [F32], epilogue=lambda acc, r: (r + acc,), extras=(x,), tk=1024)
    saved = (x, h, proj, qa, ka, va, qb, kb, vb, oa, ob, cat, lse_a, lse_b)
    return y, saved


def attention_bwd(dy, saved, norm_g, w_in, sink, qn_g, kn_g, w_out, tables, tag):
    x, h, proj, qa, ka, va, qb, kb, vb, oa, ob, cat, lse_a, lse_b = saved
    (dcat,) = matmul(dy, w_out, "nt", f"{tag}_dcat", [BF16], tk=1024)
    (dw_out,) = matmul(cat, dy, "tn", f"{tag}_dwout", [BF16], tm=1024)
    doa, dob = dcat[:, :A_Q], dcat[:, A_Q:]
    dqa, dka, dva, dsink = window_attn_bwd(qa, ka, va, sink, doa, lse_a, attn_delta(oa, doa, f"{tag}_delta_a"), f"{tag}_dwin")
    dqb, dkb, dvb = flash_attn_bwd(qb, kb, vb, dob, lse_b, attn_delta(ob, dob, f"{tag}_delta_b"), f"{tag}_dflash")
    dproj, dqg, dkg = qkv_post_bwd(proj, tables, qn_g, kn_g, dqa, dka, dva, dqb, dkb, dvb, f"{tag}_dqkv")
    (dw_in,) = matmul(h, dproj, "tn", f"{tag}_dwin_w", [BF16], tm=1024, tn=ATT_IN)
    dx, dg = matmul_nt_normbwd(dproj, w_in, x, norm_g, dy, f"{tag}_dx")
    grp = A_HEADS // A_KV_HEADS
    small = dict(norm=dg[0], sink=dsink[:, :grp, 0].reshape(A_HEADS), qnorm=dqg[0, :HEAD_DIM], knorm=dkg[0, :HEAD_DIM])
    return dx, dw_in, dw_out, small


def sgu_fwd(x, norm_g, w_in, ln_g, ln_b, w_s, b_rows, w_out, tag):
    h = rmsnorm_fwd(x, norm_g, f"{tag}_norm")
    apre, z = matmul(h, w_in, "nn", f"{tag}_in", [BF16, BF16], epilogue=lambda acc: (acc, _gelu(acc)), tk=1024)
    y = sgu_mix_fwd(z, ln_g, ln_b, w_s, b_rows, f"{tag}_mix")
    (out,) = matmul(y, w_out, "nn", f"{tag}_out", [F32], epilogue=lambda acc, r: (r + acc,), extras=(x,), tk=1024)
    return out, (x, h, apre, z, y)


def sgu_bwd(dout, saved, norm_g, w_in, ln_g, ln_b, w_s, b_rows, w_out, tag):
    x, h, apre, z, y = saved
    (dy,) = matmul(dout, w_out, "nt", f"{tag}_dy", [BF16], tk=1024)
    (dw_out,) = matmul(y, dout, "tn", f"{tag}_dwout", [BF16], tm=1024)
    dapre, dlg, dlb, dws, dbs = sgu_mix_bwd(z, apre, dy, ln_g, ln_b, w_s, b_rows, f"{tag}_dmix")
    (dw_in,) = matmul(h, dapre, "tn", f"{tag}_dwin", [BF16], tm=1024)
    dx, dg = matmul_nt_normbwd(dapre, w_in, x, norm_g, dout, f"{tag}_dx")
    small = dict(norm=dg[0], ln_g=dlg[0], ln_b=dlb[0], w_s=dws, b_s=dbs)
    return dx, dw_in, dw_out, small


def mlp_fwd(x, norm_g, w1, w2, tag):
    h = rmsnorm_fwd(x, norm_g, f"{tag}_norm")

    def relu2(acc):
        r = jnp.maximum(acc, 0.0)
        return r, r * r

    r, s = matmul(h, w1, "nn", f"{tag}_up", [BF16, BF16], epilogue=relu2, tk=1024)
    (y,) = matmul(s, w2, "nn", f"{tag}_down", [F32], epilogue=lambda acc, res: (res + acc,), extras=(x,))
    return y, (x, h, r, s)


def mlp_bwd(dy, saved, norm_g, w1, w2, tag):
    x, h, r, s = saved
    (da,) = matmul(dy, w2, "nt", f"{tag}_da", [BF16], epilogue=lambda acc, rr: (acc * (2.0 * rr.astype(F32)),),
                   extras=(r,), tk=1024)
    (dw2,) = matmul(s, dy, "tn", f"{tag}_dw2", [BF16], tm=1024)
    (dw1,) = matmul(h, da, "tn", f"{tag}_dw1", [BF16], tm=1024)
    dx, dg = matmul_nt_normbwd(da, w1, x, norm_g, dy, f"{tag}_dx")
    return dx, dw1, dw2, dg[0]


def _col_full(f, l, k, n):
    return f.reshape(N_DEV, l, k, n // N_DEV).transpose(1, 2, 0, 3).reshape(l, k, n)


def _row_full(f, l, k, n):
    return f.reshape(N_DEV, l, k // N_DEV, n).transpose(1, 0, 2, 3).reshape(l, k, n)


def _col_flat(g):
    l, k, n = g.shape
    return g.reshape(l, k, N_DEV, n // N_DEV).transpose(2, 0, 1, 3).reshape(N_DEV, -1)


def _row_flat(g):
    l, k, n = g.shape
    return g.reshape(l, N_DEV, k // N_DEV, n).transpose(1, 0, 2, 3).reshape(N_DEV, -1)


def _pad_rows(flat, axis, rows_mult):
    n = flat.shape[axis]
    per = rows_mult * FLAT_COLS
    total = -(-n // per) * per
    pad = [(0, 0)] * flat.ndim
    pad[axis] = (0, total - n)
    out = jnp.pad(flat, pad)
    return out.reshape(out.shape[:axis] + (total // FLAT_COLS, FLAT_COLS))


BIG = ("att_w_in", "att_w_out", "sgu_w_in", "sgu_w_out", "mlp_w1", "mlp_w2", "sgu_norm", "sgu_ln_g", "sgu_ln_b")
COL_SHARDED = ("att_w_in", "sgu_w_in", "mlp_w1", "sgu_norm", "sgu_ln_g", "sgu_ln_b")
SMALL = ("att_norm", "att_sink", "att_qnorm", "att_knorm", "sgu_w_s", "sgu_b_s", "mlp_norm", "final_norm")
ORDER = ("att_norm", "att_w_in", "att_sink", "att_qnorm", "att_knorm", "att_w_out", "sgu_norm", "sgu_w_in", "sgu_ln_g",
         "sgu_ln_b", "sgu_w_s", "sgu_b_s", "sgu_w_out", "mlp_norm", "mlp_w1", "mlp_w2", "final_norm")
BIG_ROWS_MULT = 256
SMALL_ROWS_MULT = 8


def _flat_local(blocks, names, rows_mult, dtype):
    return _pad_rows(jnp.concatenate([blocks[n].reshape(-1).astype(dtype) for n in names]), 0, rows_mult)


def _split_local(flat, like, names):
    out, off = {}, 0
    f = flat.reshape(-1)
    for n in names:
        size = like[n].size
        out[n] = f[off:off + size].reshape(like[n].shape)
        off += size
    return out


def kernel(x, att_norm, att_w_in, att_sink, att_qnorm, att_knorm, att_w_out, sgu_norm, sgu_w_in, sgu_ln_g, sgu_ln_b, sgu_w_s, sgu_b_s, sgu_w_out, mlp_norm, mlp_w1, mlp_w2, final_norm, loss_target, m_att_norm, m_att_w_in, m_att_sink, m_att_qnorm, m_att_knorm, m_att_w_out, m_sgu_norm, m_sgu_w_in, m_sgu_ln_g, m_sgu_ln_b, m_sgu_w_s, m_sgu_b_s, m_sgu_w_out, m_mlp_norm, m_mlp_w1, m_mlp_w2, m_final_norm, v_att_norm, v_att_w_in, v_att_sink, v_att_qnorm, v_att_knorm, v_att_w_out, v_sgu_norm, v_sgu_w_in, v_sgu_ln_g, v_sgu_ln_b, v_sgu_w_s, v_sgu_b_s, v_sgu_w_out, v_mlp_norm, v_mlp_w1, v_mlp_w2, v_final_norm):
    w = dict(att_norm=att_norm, att_w_in=att_w_in, att_sink=att_sink, att_qnorm=att_qnorm, att_knorm=att_knorm,
             att_w_out=att_w_out, sgu_norm=sgu_norm, sgu_w_in=sgu_w_in, sgu_ln_g=sgu_ln_g, sgu_ln_b=sgu_ln_b, sgu_w_s=sgu_w_s,
             sgu_b_s=sgu_b_s, sgu_w_out=sgu_w_out, mlp_norm=mlp_norm, mlp_w1=mlp_w1, mlp_w2=mlp_w2, final_norm=final_norm)
    m = dict(att_norm=m_att_norm, att_w_in=m_att_w_in, att_sink=m_att_sink, att_qnorm=m_att_qnorm, att_knorm=m_att_knorm,
             att_w_out=m_att_w_out, sgu_norm=m_sgu_norm, sgu_w_in=m_sgu_w_in, sgu_ln_g=m_sgu_ln_g, sgu_ln_b=m_sgu_ln_b,
             sgu_w_s=m_sgu_w_s, sgu_b_s=m_sgu_b_s, sgu_w_out=m_sgu_w_out, mlp_norm=m_mlp_norm, mlp_w1=m_mlp_w1, mlp_w2=m_mlp_w2,
             final_norm=m_final_norm)
    v = dict(att_norm=v_att_norm, att_w_in=v_att_w_in, att_sink=v_att_sink, att_qnorm=v_att_qnorm, att_knorm=v_att_knorm,
             att_w_out=v_att_w_out, sgu_norm=v_sgu_norm, sgu_w_in=v_sgu_w_in, sgu_ln_g=v_sgu_ln_g, sgu_ln_b=v_sgu_ln_b,
             sgu_w_s=v_sgu_w_s, sgu_b_s=v_sgu_b_s, sgu_w_out=v_sgu_w_out, mlp_norm=v_mlp_norm, mlp_w1=v_mlp_w1, mlp_w2=v_mlp_w2,
             final_norm=v_final_norm)
    loss, grad_x, g, d, nm, nv = train_step(x[0], loss_target[0], w, m, v)
    return (loss, grad_x[None], *[g[n] for n in ORDER], *[d[n] for n in ORDER], *[nm[n] for n in ORDER], *[nv[n] for n in ORDER])


def train_step(x, target, w, m, v):
    t, d_model = x.shape
    n_att, n_sgu, depth = w["att_w_in"].shape[0], w["sgu_w_in"].shape[0], w["mlp_w1"].shape[0]
    d_ff = w["mlp_w1"].shape[2] * N_DEV
    sgu_w = w["sgu_w_out"].shape[1] * N_DEV
    att_o = w["att_w_out"].shape[1] * N_DEV

    big_local = _flat_local(w, BIG, BIG_ROWS_MULT, BF16)
    vec_local = jnp.pad(jnp.concatenate([w[n].reshape(-1) for n in ("sgu_norm", "sgu_ln_g", "sgu_ln_b")]).reshape(-1, LANES),
                        ((0, 2), (0, 0)))
    big_all, vec_all = all_gather([big_local, vec_local], "gather_weights")
    flat = big_all.reshape(N_DEV, -1)
    full, off = {}, 0
    shapes = dict(att_w_in=(n_att, d_model, ATT_IN), att_w_out=(n_att, att_o, d_model), sgu_w_in=(n_sgu, d_model, 2 * sgu_w),
                  sgu_w_out=(n_sgu, sgu_w, d_model), mlp_w1=(depth, d_model, d_ff), mlp_w2=(depth, d_ff, d_model))
    for n in BIG[:6]:
        size = w[n].size
        full[n] = (_col_full if n in COL_SHARDED else _row_full)(flat[:, off:off + size], *shapes[n])
        off += size
    vecs = vec_all[:, :3 * n_sgu].reshape(N_DEV, 3, n_sgu, LANES).transpose(1, 2, 0, 3).reshape(3, n_sgu, N_DEV * LANES)
    sgu_norm_f, sgu_ln_g_f, sgu_ln_b_f = vecs[0], vecs[1], vecs[2]
    w_s_bf = w["sgu_w_s"].astype(BF16)
    b_rows = jnp.broadcast_to(w["sgu_b_s"][:, :, :, None], w["sgu_b_s"].shape + (LANES,))
    tables = _rope_tables(t)

    saved, h = [], x
    for layer in range(depth):
        i = layer // 2
        if layer % 2 == 0:
            h, sv = attention_fwd(h, w["att_norm"][i], full["att_w_in"][i], w["att_sink"][i], w["att_qnorm"][i],
                                  w["att_knorm"][i], full["att_w_out"][i], tables, f"att{i}")
        else:
            h, sv = sgu_fwd(h, sgu_norm_f[i], full["sgu_w_in"][i], sgu_ln_g_f[i], sgu_ln_b_f[i], w_s_bf[i], b_rows[i],
                            full["sgu_w_out"][i], f"sgu{i}")
        h, sm = mlp_fwd(h, w["mlp_norm"][layer], full["mlp_w1"][layer], full["mlp_w2"][layer], f"mlp{layer}")
        saved.append((sv, sm))
    loss_row, dh, dgf = loss_head(h, w["final_norm"], target, "loss_head")

    gb = {n: [None] * w[n].shape[0] for n in BIG[:6]}
    gs = dict(att_norm=[None] * n_att, att_sink=[None] * n_att, att_qnorm=[None] * n_att, att_knorm=[None] * n_att,
              sgu_norm=[None] * n_sgu, sgu_ln_g=[None] * n_sgu, sgu_ln_b=[None] * n_sgu, sgu_w_s=[None] * n_sgu,
              sgu_b_s=[None] * n_sgu, mlp_norm=[None] * depth)
    for layer in reversed(range(depth)):
        i = layer // 2
        sv, sm = saved[layer]
        dh, gb["mlp_w1"][layer], gb["mlp_w2"][layer], gs["mlp_norm"][layer] = mlp_bwd(
            dh, sm, w["mlp_norm"][layer], full["mlp_w1"][layer], full["mlp_w2"][layer], f"mlp{layer}")
        if layer % 2 == 0:
            dh, gb["att_w_in"][i], gb["att_w_out"][i], sm_g = attention_bwd(
                dh, sv, w["att_norm"][i], full["att_w_in"][i], w["att_sink"][i], w["att_qnorm"][i], w["att_knorm"][i],
                full["att_w_out"][i], tables, f"att{i}")
            gs["att_norm"][i], gs["att_sink"][i] = sm_g["norm"], sm_g["sink"]
            gs["att_qnorm"][i], gs["att_knorm"][i] = sm_g["qnorm"], sm_g["knorm"]
        else:
            dh, gb["sgu_w_in"][i], gb["sgu_w_out"][i], sm_g = sgu_bwd(
                dh, sv, sgu_norm_f[i], full["sgu_w_in"][i], sgu_ln_g_f[i], sgu_ln_b_f[i], w_s_bf[i], b_rows[i],
                full["sgu_w_out"][i], f"sgu{i}")
            gs["sgu_norm"][i], gs["sgu_ln_g"][i], gs["sgu_ln_b"][i] = sm_g["norm"], sm_g["ln_g"], sm_g["ln_b"]
            gs["sgu_w_s"][i], gs["sgu_b_s"][i] = sm_g["w_s"], sm_g["b_s"]
    grad_x = dh

    parts = []
    for n in BIG[:6]:
        stacked = jnp.stack(gb[n])
        parts.append(_col_flat(stacked) if n in COL_SHARDED else _row_flat(stacked))
    for n in ("sgu_norm", "sgu_ln_g", "sgu_ln_b"):
        parts.append(_col_flat(jnp.stack(gs[n])[:, None, :]).astype(BF16))
    scatter = _pad_rows(jnp.concatenate(parts, axis=1), 1, BIG_ROWS_MULT)
    small_g = dict(att_norm=jnp.stack(gs["att_norm"]), att_sink=jnp.stack(gs["att_sink"]), att_qnorm=jnp.stack(gs["att_qnorm"]),
                   att_knorm=jnp.stack(gs["att_knorm"]), sgu_w_s=jnp.stack(gs["sgu_w_s"]), sgu_b_s=jnp.stack(gs["sgu_b_s"]),
                   mlp_norm=jnp.stack(gs["mlp_norm"]), final_norm=dgf[0])
    bcast = _flat_local(small_g, SMALL, SMALL_ROWS_MULT, F32)
    big_parts, small_parts = exchange_partials(scatter, bcast, "exchange_grads")
    outs = {}
    for names, got, mult, tag in ((BIG, big_parts, BIG_ROWS_MULT, "adamw_sharded"), (SMALL, small_parts, SMALL_ROWS_MULT, "adamw_replicated")):
        res = adamw(got, _flat_local(w, names, mult, F32), _flat_local(m, names, mult, F32), _flat_local(v, names, mult, F32), tag)
        for kind, flat_out in zip(("g", "d", "m", "v"), res):
            outs.setdefault(kind, {}).update(_split_local(flat_out, w, names))
    loss = lax.psum(loss_row[0, 0], ("x", "y", "c"))
    return loss, grad_x, outs["g"], outs["d"], outs["m"], outs["v"]
```

```python
import functools
import math

import jax
import jax.numpy as jnp
from jax import lax
from jax.experimental import pallas as pl
from jax.experimental.pallas import tpu as pltpu

F32 = jnp.float32
BF16 = jnp.bfloat16

HEAD_DIM = 64
A_HEADS = 8
A_KV_HEADS = 2
B_HEADS = 8
B_KV_HEADS = 2
WINDOW = 128
BLOCK = 128
ROPE_THETA = 10000.0
GRID_W = 64
SGU_GROUPS = 8
SGU_CHUNK = 128
EPS = 1e-6
SCALE = HEAD_DIM ** -0.5
NEG = -1e30
LOG2E = math.log2(math.e)
LN2 = math.log(2.0)

A_Q = A_HEADS * HEAD_DIM
A_KV = A_KV_HEADS * HEAD_DIM
B_Q = B_HEADS * HEAD_DIM
B_KV = B_KV_HEADS * HEAD_DIM
OFF_QA, OFF_KA, OFF_VA = 0, A_Q, A_Q + A_KV
OFF_QB = A_Q + 2 * A_KV
OFF_KB = OFF_QB + B_Q
OFF_VB = OFF_KB + B_KV
ATT_IN = OFF_VB + B_KV

ADAM_LR = 0.001
ADAM_B1 = 0.9
ADAM_B2 = 0.999
ADAM_EPS = 1e-08
ADAM_WD = 0.01
ADAM_STEP = 10

N_DEV = 8
LANES = 128
V7X_VMEM_LIMIT = 56 * 1024 * 1024
FLAT_COLS = 1024


def _cparams(sem, vmem=V7X_VMEM_LIMIT):
    return pltpu.CompilerParams(dimension_semantics=sem, vmem_limit_bytes=vmem)


def _dot_nn(a, b):
    return lax.dot_general(a, b, (((1,), (0,)), ((), ())), preferred_element_type=F32)


def _dot_nt(a, b):
    return lax.dot_general(a, b, (((1,), (1,)), ((), ())), preferred_element_type=F32)


def _dot_tn(a, b):
    return lax.dot_general(a, b, (((0,), (0,)), ((), ())), preferred_element_type=F32)


def _bf(x):
    return x if x.dtype == BF16 else x.astype(BF16)


def _lane(shape):
    return lax.broadcasted_iota(jnp.int32, shape, len(shape) - 1)


def _seg_matrix(rows_lo, rows_hi):
    r = lax.broadcasted_iota(jnp.int32, (LANES, LANES), 0)
    return jnp.where((r >= rows_lo) & (r < rows_hi), 1.0, 0.0).astype(BF16)


def _group_matrix(width):
    r = lax.broadcasted_iota(jnp.int32, (LANES, LANES), 0)
    c = lax.broadcasted_iota(jnp.int32, (LANES, LANES), 1)
    return jnp.where((r // width) == (c // width), 1.0, 0.0).astype(BF16)


def _dot_f32_by_ones(s, ones_bf16):
    hi = s.astype(BF16)
    lo = (s - hi.astype(F32)).astype(BF16)
    return _dot_nn(hi, ones_bf16) + _dot_nn(lo, ones_bf16)


def _swap_halves(x, width):
    half = width // 2
    first = (_lane(x.shape) % width) < half
    return jnp.where(first, pltpu.roll(x, LANES - half, 1), pltpu.roll(x, half, 1))


def rmsnorm_fwd(x, g, name):
    t, d = x.shape
    tm = min(t, 512)

    def body(x_ref, g_ref, h_ref):
        xf = x_ref[...]
        r = lax.rsqrt(jnp.mean(xf * xf, axis=-1, keepdims=True) + EPS)
        h_ref[...] = (xf * r * g_ref[...]).astype(BF16)

    return pl.pallas_call(
        body, name=name, grid=(t // tm,),
        in_specs=[pl.BlockSpec((tm, d), lambda i: (i, 0)), pl.BlockSpec((1, d), lambda i: (0, 0))],
        out_specs=pl.BlockSpec((tm, d), lambda i: (i, 0)),
        out_shape=jax.ShapeDtypeStruct((t, d), BF16),
        compiler_params=_cparams(("parallel",)),
    )(x, g.reshape(1, d))


def _fit(n, want):
    t = min(n, want)
    while n % t:
        t //= 2
    return t


def matmul(a, b, mode, name, out_dtypes, epilogue=None, extras=(), a_fn=None, tm=1024, tn=1024, tk=1024):
    if mode == "nn":
        (m, k), (_, n) = a.shape, b.shape
    elif mode == "nt":
        (m, k), (n, _) = a.shape, b.shape
    else:
        (k, m), (_, n) = a.shape, b.shape
    tm, tn, tk = _fit(m, tm), _fit(n, tn), _fit(k, tk)
    nk = k // tk
    n_ex, n_out = len(extras), len(out_dtypes)
    if epilogue is None:
        epilogue = lambda acc: (acc,)

    def body(*refs):
        a_ref, b_ref = refs[0], refs[1]
        ex_refs = refs[2:2 + n_ex]
        out_refs = refs[2 + n_ex:2 + n_ex + n_out]
        acc_ref = refs[2 + n_ex + n_out]
        kk = pl.program_id(2)
        av, bv = _bf(a_ref[...]), _bf(b_ref[...])
        if a_fn is not None:
            av = a_fn(av)
        if mode == "nn":
            part = _dot_nn(av, bv)
        elif mode == "nt":
            part = _dot_nt(av, bv)
        else:
            part = _dot_tn(av, bv)

        @pl.when(kk == 0)
        def _():
            acc_ref[...] = part

        @pl.when(kk > 0)
        def _():
            acc_ref[...] += part

        @pl.when(kk == nk - 1)
        def _():
            outs = epilogue(acc_ref[...], *[r[...] for r in ex_refs])
            for r, o in zip(out_refs, outs):
                r[...] = o.astype(r.dtype)

    if mode == "nn":
        a_spec = pl.BlockSpec((tm, tk), lambda i, j, kk: (i, kk))
        b_spec = pl.BlockSpec((tk, tn), lambda i, j, kk: (kk, j))
    elif mode == "nt":
        a_spec = pl.BlockSpec((tm, tk), lambda i, j, kk: (i, kk))
        b_spec = pl.BlockSpec((tn, tk), lambda i, j, kk: (j, kk))
    else:
        a_spec = pl.BlockSpec((tk, tm), lambda i, j, kk: (kk, i))
        b_spec = pl.BlockSpec((tk, tn), lambda i, j, kk: (kk, j))
    mn_spec = pl.BlockSpec((tm, tn), lambda i, j, kk: (i, j))
    outs = pl.pallas_call(
        body, name=name, grid=(m // tm, n // tn, nk),
        in_specs=[a_spec, b_spec] + [mn_spec] * n_ex,
        out_specs=[mn_spec] * n_out,
        out_shape=[jax.ShapeDtypeStruct((m, n), dt) for dt in out_dtypes],
        scratch_shapes=[pltpu.VMEM((tm, tn), F32)],
        compiler_params=_cparams(("parallel", "parallel", "arbitrary")),
    )(a, b, *extras)
    return outs


def matmul_nt_normbwd(dz, w, x, g, dres, name, tm=1024, tk=1024):
    m, k = dz.shape
    d = w.shape[0]
    tm, tk = _fit(m, tm), _fit(k, tk)
    nk = k // tk

    def body(dz_ref, w_ref, x_ref, g_ref, dres_ref, dx_ref, dxb_ref, dg_ref, acc_ref):
        i, kk = pl.program_id(0), pl.program_id(1)
        part = _dot_nt(_bf(dz_ref[...]), _bf(w_ref[...]))

        @pl.when(kk == 0)
        def _():
            acc_ref[...] = part

        @pl.when(kk > 0)
        def _():
            acc_ref[...] += part

        @pl.when((i == 0) & (kk == 0))
        def _():
            dg_ref[...] = jnp.zeros_like(dg_ref)

        @pl.when(kk == nk - 1)
        def _():
            dh = acc_ref[...]
            xf = x_ref[...]
            r = lax.rsqrt(jnp.mean(xf * xf, axis=-1, keepdims=True) + EPS)
            xhat = xf * r
            dg_ref[...] += jnp.sum(dh * xhat, axis=0, keepdims=True)
            dxh = dh * g_ref[...]
            dx = r * (dxh - xhat * jnp.mean(dxh * xhat, axis=-1, keepdims=True))
            out = dres_ref[...] + dx
            dx_ref[...] = out
            dxb_ref[...] = out.astype(BF16)

    row = pl.BlockSpec((tm, d), lambda i, kk: (i, 0))
    vec = pl.BlockSpec((1, d), lambda i, kk: (0, 0))
    return pl.pallas_call(
        body, name=name, grid=(m // tm, nk),
        in_specs=[pl.BlockSpec((tm, tk), lambda i, kk: (i, kk)), pl.BlockSpec((d, tk), lambda i, kk: (0, kk)), row, vec, row],
        out_specs=[row, row, vec],
        out_shape=[jax.ShapeDtypeStruct((m, d), F32), jax.ShapeDtypeStruct((m, d), BF16), jax.ShapeDtypeStruct((1, d), F32)],
        scratch_shapes=[pltpu.VMEM((tm, d), F32)],
        compiler_params=_cparams(("arbitrary", "arbitrary")),
    )(dz, w, x, g.reshape(1, d), dres)


def _rope_tables(t):
    pos = jnp.arange(t)

    def angles(p, dim):
        freqs = ROPE_THETA ** (-jnp.arange(0, dim, 2, dtype=F32) / dim)
        return p.astype(F32)[:, None] * freqs[None, :]

    a1 = angles(pos, HEAD_DIM)
    cos_a = jnp.concatenate([jnp.cos(a1), jnp.cos(a1)], axis=-1)
    sin_a = jnp.concatenate([-jnp.sin(a1), jnp.sin(a1)], axis=-1)
    ar = angles(pos // GRID_W, HEAD_DIM // 2)
    ac = angles(pos % GRID_W, HEAD_DIM // 2)
    cos_b = jnp.concatenate([jnp.cos(ar), jnp.cos(ar), jnp.cos(ac), jnp.cos(ac)], axis=-1)
    sin_b = jnp.concatenate([-jnp.sin(ar), jnp.sin(ar), -jnp.sin(ac), jnp.sin(ac)], axis=-1)
    two = lambda z: jnp.concatenate([z, z], axis=-1)
    return two(cos_a), two(sin_a), two(cos_b), two(sin_b)


def _headnorm(xs, gmat):
    return lax.rsqrt(_dot_f32_by_ones(xs * xs, gmat) * (1.0 / HEAD_DIM) + EPS)


def qkv_post_fwd(proj, tables, qn_g, kn_g, name):
    t = proj.shape[0]
    tm = min(t, 256)
    cos_a, sin_a, cos_b, sin_b = tables
    g2 = lambda g: jnp.concatenate([g, g]).reshape(1, LANES)

    def body(p_ref, ca_ref, sa_ref, cb_ref, sb_ref, qg_ref, kg_ref, qa_ref, ka_ref, va_ref, qb_ref, kb_ref, vb_ref):
        ca, sa, cb, sb = ca_ref[...], sa_ref[...], cb_ref[...], sb_ref[...]
        gmat = _group_matrix(HEAD_DIM)

        def rope_a(xs):
            return xs * ca + _swap_halves(xs, HEAD_DIM) * sa

        def norm_rope_b(xs, g):
            y = xs * _headnorm(xs, gmat) * g
            return y * cb + _swap_halves(y, HEAD_DIM // 2) * sb

        for c in range(A_Q // LANES):
            qa_ref[:, c * LANES:(c + 1) * LANES] = rope_a(p_ref[:, OFF_QA + c * LANES:OFF_QA + (c + 1) * LANES]).astype(BF16)
        ka_ref[...] = rope_a(p_ref[:, OFF_KA:OFF_KA + LANES]).astype(BF16)
        va_ref[...] = p_ref[:, OFF_VA:OFF_VA + LANES].astype(BF16)
        for c in range(B_Q // LANES):
            qb_ref[:, c * LANES:(c + 1) * LANES] = norm_rope_b(
                p_ref[:, OFF_QB + c * LANES:OFF_QB + (c + 1) * LANES], qg_ref[...]).astype(BF16)
        kb_ref[...] = norm_rope_b(p_ref[:, OFF_KB:OFF_KB + LANES], kg_ref[...]).astype(BF16)
        vb_ref[...] = p_ref[:, OFF_VB:OFF_VB + LANES].astype(BF16)

    tab = pl.BlockSpec((tm, LANES), lambda i: (i, 0))
    vec = pl.BlockSpec((1, LANES), lambda i: (0, 0))
    wide = pl.BlockSpec((tm, A_Q), lambda i: (i, 0))
    return pl.pallas_call(
        body, name=name, grid=(t // tm,),
        in_specs=[pl.BlockSpec((tm, ATT_IN), lambda i: (i, 0)), tab, tab, tab, tab, vec, vec],
        out_specs=[wide, tab, tab, wide, tab, tab],
        out_shape=[jax.ShapeDtypeStruct((t, A_Q), BF16), jax.ShapeDtypeStruct((t, LANES), BF16),
                   jax.ShapeDtypeStruct((t, LANES), BF16), jax.ShapeDtypeStruct((t, B_Q), BF16),
                   jax.ShapeDtypeStruct((t, LANES), BF16), jax.ShapeDtypeStruct((t, LANES), BF16)],
        compiler_params=_cparams(("parallel",)),
    )(proj, cos_a, sin_a, cos_b, sin_b, g2(qn_g), g2(kn_g))


def qkv_post_bwd(proj, tables, qn_g, kn_g, dqa, dka, dva, dqb, dkb, dvb, name):
    t = proj.shape[0]
    tm = min(t, 256)
    cos_a, sin_a, cos_b, sin_b = tables
    g2 = lambda g: jnp.concatenate([g, g]).reshape(1, LANES)

    def body(p_ref, ca_ref, sa_ref, cb_ref, sb_ref, qg_ref, kg_ref, dqa_ref, dka_ref, dva_ref, dqb_ref, dkb_ref, dvb_ref,
             dp_ref, dqg_ref, dkg_ref):
        ca, sa, cb, sb = ca_ref[...], sa_ref[...], cb_ref[...], sb_ref[...]
        gmat = _group_matrix(HEAD_DIM)

        @pl.when(pl.program_id(0) == 0)
        def _():
            dqg_ref[...] = jnp.zeros_like(dqg_ref)
            dkg_ref[...] = jnp.zeros_like(dkg_ref)

        def rope_a_bwd(dy):
            return dy * ca + _swap_halves(dy * sa, HEAD_DIM)

        def norm_rope_b_bwd(dout, xs, g):
            dy = dout * cb + _swap_halves(dout * sb, HEAD_DIM // 2)
            r = _headnorm(xs, gmat)
            xhat = xs * r
            dxh = dy * g
            mean = _dot_f32_by_ones(dxh * xhat, gmat) * (1.0 / HEAD_DIM)
            return r * (dxh - xhat * mean), jnp.sum(dy * xhat, axis=0, keepdims=True)

        for c in range(A_Q // LANES):
            sl = slice(c * LANES, (c + 1) * LANES)
            dp_ref[:, OFF_QA + c * LANES:OFF_QA + (c + 1) * LANES] = rope_a_bwd(dqa_ref[:, sl].astype(F32)).astype(BF16)
        dp_ref[:, OFF_KA:OFF_KA + LANES] = rope_a_bwd(dka_ref[0] + dka_ref[1]).astype(BF16)
        dp_ref[:, OFF_VA:OFF_VA + LANES] = (dva_ref[0] + dva_ref[1]).astype(BF16)
        dqg = jnp.zeros((1, LANES), F32)
        for c in range(B_Q // LANES):
            sl = slice(c * LANES, (c + 1) * LANES)
            dx, dg = norm_rope_b_bwd(dqb_ref[:, sl].astype(F32), p_ref[:, OFF_QB + c * LANES:OFF_QB + (c + 1) * LANES], qg_ref[...])
            dp_ref[:, OFF_QB + c * LANES:OFF_QB + (c + 1) * LANES] = dx.astype(BF16)
            dqg = dqg + dg
        dqg_ref[...] += dqg
        dx, dg = norm_rope_b_bwd(dkb_ref[0] + dkb_ref[1], p_ref[:, OFF_KB:OFF_KB + LANES], kg_ref[...])
        dp_ref[:, OFF_KB:OFF_KB + LANES] = dx.astype(BF16)
        dkg_ref[...] += dg
        dp_ref[:, OFF_VB:OFF_VB + LANES] = (dvb_ref[0] + dvb_ref[1]).astype(BF16)

        @pl.when(pl.program_id(0) == t // tm - 1)
        def _():
            dqg_ref[...] = dqg_ref[...] + pltpu.roll(dqg_ref[...], HEAD_DIM, 1)
            dkg_ref[...] = dkg_ref[...] + pltpu.roll(dkg_ref[...], HEAD_DIM, 1)

    tab = pl.BlockSpec((tm, LANES), lambda i: (i, 0))
    vec = pl.BlockSpec((1, LANES), lambda i: (0, 0))
    wide = pl.BlockSpec((tm, A_Q), lambda i: (i, 0))
    slab = pl.BlockSpec((2, tm, LANES), lambda i: (0, i, 0))
    full = pl.BlockSpec((tm, ATT_IN), lambda i: (i, 0))
    return pl.pallas_call(
        body, name=name, grid=(t // tm,),
        in_specs=[full, tab, tab, tab, tab, vec, vec, wide, slab, slab, wide, slab, slab],
        out_specs=[full, vec, vec],
        out_shape=[jax.ShapeDtypeStruct((t, ATT_IN), BF16), jax.ShapeDtypeStruct((1, LANES), F32),
                   jax.ShapeDtypeStruct((1, LANES), F32)],
        compiler_params=_cparams(("arbitrary",)),
    )(proj, cos_a, sin_a, cos_b, sin_b, g2(qn_g), g2(kn_g), dqa, dka, dva, dqb, dkb, dvb)


def _head_to_half(xs, head_half, kv_half):
    low = _lane(xs.shape) < HEAD_DIM
    kept = jnp.where(low if head_half == 0 else jnp.logical_not(low), xs, 0.0)
    return jnp.where(kv_half == head_half, kept, pltpu.roll(kept, HEAD_DIM, 1))


def _halves_to_heads(r0, r1, kv_half):
    low = _lane(r0.shape) < HEAD_DIM
    a = jnp.where(kv_half == 0, r0, pltpu.roll(r0, HEAD_DIM, 1))
    b = jnp.where(kv_half == 1, r1, pltpu.roll(r1, HEAD_DIM, 1))
    return jnp.where(low, a, b)


def attn_delta(o, do, name):
    t, w = o.shape
    tm = min(t, 512)
    n_heads = w // HEAD_DIM

    def body(o_ref, do_ref, d_ref):
        lo, hi = _seg_matrix(0, HEAD_DIM), _seg_matrix(HEAD_DIM, LANES)
        for c in range(w // LANES):
            sl = slice(c * LANES, (c + 1) * LANES)
            s = o_ref[:, sl].astype(F32) * do_ref[:, sl].astype(F32)
            d_ref[2 * c] = _dot_f32_by_ones(s, lo)
            d_ref[2 * c + 1] = _dot_f32_by_ones(s, hi)

    blk = pl.BlockSpec((tm, w), lambda i: (i, 0))
    return pl.pallas_call(
        body, name=name, grid=(t // tm,),
        in_specs=[blk, blk],
        out_specs=pl.BlockSpec((n_heads, tm, LANES), lambda i: (0, i, 0)),
        out_shape=jax.ShapeDtypeStruct((n_heads, t, LANES), F32),
        compiler_params=_cparams(("parallel",)),
    )(o, do)


def _band_mask(n, t, rows_rep):
    qi = lax.broadcasted_iota(jnp.int32, (BLOCK, 3 * BLOCK), 0)
    kj = lax.broadcasted_iota(jnp.int32, (BLOCK, 3 * BLOCK), 1)
    rel = kj - BLOCK - qi
    kpos = n * BLOCK - BLOCK + kj
    ok = (jnp.abs(rel) <= WINDOW) & (kpos >= 0) & (kpos < t)
    return jnp.concatenate([ok] * rows_rep, axis=0)


def _band_specs(t, pos_of):
    nb = t // BLOCK
    prev = pl.BlockSpec((BLOCK, LANES), lambda *g: (jnp.maximum(pos_of(*g) - 1, 0), 0))
    cur = pl.BlockSpec((BLOCK, LANES), lambda *g: (pos_of(*g), 0))
    nxt = pl.BlockSpec((BLOCK, LANES), lambda *g: (jnp.minimum(pos_of(*g) + 1, nb - 1), 0))
    return [prev, cur, nxt]


def window_attn_fwd(q, k, v, sink, name):
    t = q.shape[0]
    nb = t // BLOCK

    def body(sink_ref, q_ref, kp_ref, kc_ref, kn_ref, vp_ref, vc_ref, vn_ref, o_ref, lse_ref):
        j, n = pl.program_id(0), pl.program_id(1)
        kvh = j // 2
        qf = q_ref[...].astype(F32) * SCALE
        qs = jnp.concatenate([_head_to_half(qf, 0, kvh), _head_to_half(qf, 1, kvh)], axis=0).astype(BF16)
        kband = jnp.concatenate([kp_ref[...], kc_ref[...], kn_ref[...]], axis=0)
        vband = jnp.concatenate([vp_ref[...], vc_ref[...], vn_ref[...]], axis=0)
        s = jnp.where(_band_mask(n, t, 2), _dot_nt(qs, kband), NEG)
        row = lax.broadcasted_iota(jnp.int32, (2 * BLOCK, 1), 0)
        sk = jnp.where(row < BLOCK, sink_ref[2 * j], sink_ref[2 * j + 1])
        m = jnp.maximum(jnp.max(s, axis=-1, keepdims=True), sk)
        p = jnp.exp(s - m)
        denom = jnp.sum(p, axis=-1, keepdims=True) + jnp.exp(sk - m)
        o = _dot_nn(p.astype(BF16), vband) / denom
        o_ref[...] = _halves_to_heads(o[:BLOCK], o[BLOCK:], kvh).astype(BF16)
        lse = jnp.broadcast_to(m + jnp.log(denom), (2 * BLOCK, LANES))
        lse_ref[0] = lse[:BLOCK]
        lse_ref[1] = lse[BLOCK:]

    band = _band_specs(t, lambda j, n: n)
    qspec = pl.BlockSpec((BLOCK, LANES), lambda j, n: (n, j))
    return pl.pallas_call(
        body, name=name, grid=(A_HEADS // 2, nb),
        in_specs=[pl.BlockSpec(memory_space=pltpu.SMEM), qspec] + band + band,
        out_specs=[qspec, pl.BlockSpec((2, BLOCK, LANES), lambda j, n: (j, n, 0))],
        out_shape=[jax.ShapeDtypeStruct((t, A_Q + B_Q), BF16), jax.ShapeDtypeStruct((A_HEADS, t, LANES), F32)],
        compiler_params=_cparams(("parallel", "parallel")),
    )(sink, q, k, k, k, v, v, v)


def window_attn_bwd(q, k, v, sink, do, lse, delta, name):
    t = q.shape[0]
    nb = t // BLOCK
    grp = A_HEADS // A_KV_HEADS
    gw = grp * HEAD_DIM

    def body(sink_ref, q_ref, do_ref, kp_ref, kc_ref, kn_ref, vp_ref, vc_ref, vn_ref, lse_ref, dl_ref,
             dq_ref, dk_ref, dv_ref, ds_ref):
        kvh, n = pl.program_id(0), pl.program_id(1)

        @pl.when(n == 0)
        def _():
            dk_ref[...] = jnp.zeros_like(dk_ref)
            dv_ref[...] = jnp.zeros_like(dv_ref)
            ds_ref[...] = jnp.zeros_like(ds_ref)

        qparts, doparts = [], []
        for hh in range(grp):
            sl = slice((hh // 2) * LANES, (hh // 2 + 1) * LANES)
            qparts.append(_head_to_half(q_ref[:, sl].astype(F32) * SCALE, hh % 2, kvh))
            doparts.append(_head_to_half(do_ref[:, sl].astype(F32), hh % 2, kvh))
        qs = jnp.concatenate(qparts, axis=0).astype(BF16)
        dos = jnp.concatenate(doparts, axis=0).astype(BF16)
        lse_b = jnp.concatenate([lse_ref[hh] for hh in range(grp)], axis=0)
        dl_b = jnp.concatenate([dl_ref[hh] for hh in range(grp)], axis=0)
        kband = jnp.concatenate([kp_ref[...], kc_ref[...], kn_ref[...]], axis=0)
        vband = jnp.concatenate([vp_ref[...], vc_ref[...], vn_ref[...]], axis=0)
        s = jnp.where(_band_mask(n, t, grp), _dot_nt(qs, kband), NEG)
        p = jnp.exp(s - lse_b[:, :1])
        dp = _dot_nt(dos, vband)
        dsc = (p * (dp - dl_b[:, :1])).astype(BF16)
        dvb = _dot_tn(p.astype(BF16), dos)
        dkb = _dot_tn(dsc, qs)
        dq = _dot_nn(dsc, kband) * SCALE
        for c in range(grp // 2):
            dq_ref[:, c * LANES:(c + 1) * LANES] = _halves_to_heads(
                dq[2 * c * BLOCK:(2 * c + 1) * BLOCK], dq[(2 * c + 1) * BLOCK:(2 * c + 2) * BLOCK], kvh).astype(dq_ref.dtype)
        blocks = [jnp.maximum(n - 1, 0), n, jnp.minimum(n + 1, nb - 1)]
        for b3, blk in enumerate(blocks):
            rows = pl.ds(pl.multiple_of(blk * BLOCK, BLOCK), BLOCK)
            dk_ref[0, rows, :] += dkb[b3 * BLOCK:(b3 + 1) * BLOCK]
            dv_ref[0, rows, :] += dvb[b3 * BLOCK:(b3 + 1) * BLOCK]
        rid = lax.broadcasted_iota(jnp.int32, (8, LANES), 0)
        upd = jnp.zeros((8, LANES), F32)
        for hh in range(grp):
            sk = sink_ref[kvh * grp + hh]
            rs = slice(hh * BLOCK, (hh + 1) * BLOCK)
            tot = jnp.sum(jnp.exp(sk - lse_b[rs]) * dl_b[rs], axis=0, keepdims=True)
            upd = upd + jnp.where(rid == hh, -tot, 0.0)
        ds_ref[0] += upd

    band = _band_specs(t, lambda kvh, n: n)
    qspec = pl.BlockSpec((BLOCK, gw), lambda kvh, n: (n, kvh))
    stat = pl.BlockSpec((grp, BLOCK, LANES), lambda kvh, n: (kvh, n, 0))
    slab = pl.BlockSpec((1, t, LANES), lambda kvh, n: (kvh, 0, 0))
    return pl.pallas_call(
        body, name=name, grid=(A_KV_HEADS, nb),
        in_specs=[pl.BlockSpec(memory_space=pltpu.SMEM), qspec, qspec] + band + band + [stat, stat],
        out_specs=[qspec, slab, slab, pl.BlockSpec((1, 8, LANES), lambda kvh, n: (kvh, 0, 0))],
        out_shape=[jax.ShapeDtypeStruct((t, A_Q), BF16), jax.ShapeDtypeStruct((A_KV_HEADS, t, LANES), F32),
                   jax.ShapeDtypeStruct((A_KV_HEADS, t, LANES), F32), jax.ShapeDtypeStruct((A_KV_HEADS, 8, LANES), F32)],
        compiler_params=_cparams(("arbitrary", "arbitrary")),
    )(sink, q, do, k, k, k, v, v, v, lse, delta)


def flash_attn_fwd(q, k, v, cat, name, tq=256, tk=2048):
    t = q.shape[0]
    tq, tk = _fit(t, tq), _fit(t, tk)
    nk = t // tk

    def body(q_ref, k_ref, v_ref, cat_ref, o_ref, lse_ref, m_ref, acc_ref):
        del cat_ref
        kvh = pl.program_id(0) // 2
        qf = q_ref[...].astype(F32) * (SCALE * LOG2E)
        qs = jnp.concatenate([_head_to_half(qf, 0, kvh), _head_to_half(qf, 1, kvh)], axis=0).astype(BF16)
        m_ref[...] = jnp.full_like(m_ref, NEG)
        acc_ref[...] = jnp.zeros_like(acc_ref)
        mine = (_lane((tk, LANES)) < HEAD_DIM) == (kvh == 0)

        def chunk(c, carry):
            rows = pl.ds(pl.multiple_of(c * tk, tk), tk)
            s = _dot_nt(qs, k_ref[rows, :])
            vb = jnp.where(mine, v_ref[rows, :], jnp.ones((), BF16))
            m_prev = m_ref[...]
            m_new = jnp.maximum(m_prev, jnp.max(s, axis=-1, keepdims=True))
            p = jnp.exp2(s - m_new[:, :1]).astype(BF16)
            acc_ref[...] = jnp.exp2(m_prev - m_new) * acc_ref[...] + _dot_nn(p, vb)
            m_ref[...] = m_new
            return carry

        lax.fori_loop(0, nk, chunk, 0)
        acc = acc_ref[...]
        other = pltpu.roll(acc, HEAD_DIM, 1)
        o = acc / other
        o_ref[...] = _halves_to_heads(o[:tq], o[tq:], kvh).astype(BF16)
        in_mine = (_lane(acc.shape) < HEAD_DIM) == (kvh == 0)
        lse = m_ref[...] + jnp.log2(jnp.where(in_mine, other, acc))
        lse_ref[0] = lse[:tq]
        lse_ref[1] = lse[tq:]

    qspec = pl.BlockSpec((tq, LANES), lambda j, i: (i, j))
    whole = pl.BlockSpec((t, LANES), lambda j, i: (0, 0))
    return pl.pallas_call(
        body, name=name, grid=(B_HEADS // 2, t // tq),
        in_specs=[qspec, whole, whole, _ANY],
        out_specs=[pl.BlockSpec((tq, LANES), lambda j, i: (i, A_Q // LANES + j)),
                   pl.BlockSpec((2, tq, LANES), lambda j, i: (j, i, 0))],
        out_shape=[jax.ShapeDtypeStruct(cat.shape, BF16), jax.ShapeDtypeStruct((B_HEADS, t, LANES), F32)],
        scratch_shapes=[pltpu.VMEM((2 * tq, LANES), F32), pltpu.VMEM((2 * tq, LANES), F32)],
        input_output_aliases={3: 0},
        compiler_params=_cparams(("parallel", "parallel")),
    )(q, k, v, cat)


def flash_attn_bwd(q, k, v, do, lse, delta, name, tq=256, tk=512):
    t = q.shape[0]
    tq, tk = _fit(t, tq), _fit(t, tk)
    nk = t // tk
    grp = B_HEADS // B_KV_HEADS
    gw = grp * HEAD_DIM

    def body(q_ref, do_ref, k_ref, v_ref, lse_ref, dl_ref, dq_ref, dk_ref, dv_ref, dq_s):
        kvh, i = pl.program_id(0), pl.program_id(1)

        @pl.when(i == 0)
        def _():
            dk_ref[...] = jnp.zeros_like(dk_ref)
            dv_ref[...] = jnp.zeros_like(dv_ref)

        qparts, doparts = [], []
        for hh in range(grp):
            sl = slice((hh // 2) * LANES, (hh // 2 + 1) * LANES)
            qparts.append(_head_to_half(q_ref[:, sl].astype(F32) * (SCALE * LOG2E), hh % 2, kvh))
            doparts.append(_head_to_half(do_ref[:, sl].astype(F32), hh % 2, kvh))
        qs = jnp.concatenate(qparts, axis=0).astype(BF16)
        dos = jnp.concatenate(doparts, axis=0).astype(BF16)
        lse = jnp.concatenate([lse_ref[hh][:, :1] for hh in range(grp)], axis=0)
        dl = jnp.concatenate([dl_ref[hh][:, :1] for hh in range(grp)], axis=0)
        dq_s[...] = jnp.zeros_like(dq_s)

        def chunk(c, carry):
            rows = pl.ds(pl.multiple_of(c * tk, tk), tk)
            kb, vb = k_ref[rows, :], v_ref[rows, :]
            p = jnp.exp2(_dot_nt(qs, kb) - lse)
            dp = _dot_nt(dos, vb)
            dsc = (p * (dp - dl)).astype(BF16)
            dv_ref[0, rows, :] += _dot_tn(p.astype(BF16), dos)
            dk_ref[0, rows, :] += _dot_tn(dsc, qs) * LN2
            dq_s[...] += _dot_nn(dsc, kb)
            return carry

        lax.fori_loop(0, nk, chunk, 0)
        for c in range(grp // 2):
            dq_ref[:, c * LANES:(c + 1) * LANES] = (_halves_to_heads(
                dq_s[2 * c * tq:(2 * c + 1) * tq], dq_s[(2 * c + 1) * tq:(2 * c + 2) * tq], kvh) * SCALE).astype(dq_ref.dtype)

    qspec = pl.BlockSpec((tq, gw), lambda kvh, i: (i, kvh))
    dospec = pl.BlockSpec((tq, gw), lambda kvh, i: (i, A_Q // gw + kvh))
    whole = pl.BlockSpec((t, LANES), lambda kvh, i: (0, 0))
    stat = pl.BlockSpec((grp, tq, LANES), lambda kvh, i: (kvh, i, 0))
    dlstat = pl.BlockSpec((grp, tq, LANES), lambda kvh, i: (A_HEADS // grp + kvh, i, 0))
    slab = pl.BlockSpec((1, t, LANES), lambda kvh, i: (kvh, 0, 0))
    return pl.pallas_call(
        body, name=name, grid=(B_KV_HEADS, t // tq),
        in_specs=[qspec, dospec, whole, whole, stat, dlstat],
        out_specs=[qspec, slab, slab],
        out_shape=[jax.ShapeDtypeStruct((t, B_Q), BF16), jax.ShapeDtypeStruct((B_KV_HEADS, t, LANES), F32),
                   jax.ShapeDtypeStruct((B_KV_HEADS, t, LANES), F32)],
        scratch_shapes=[pltpu.VMEM((grp * tq, LANES), F32)],
        compiler_params=_cparams(("arbitrary", "arbitrary")),
    )(q, do, k, v, lse, delta)


_GELU_C = math.sqrt(2.0 / math.pi)
_GELU_A = 0.044715


def _gelu(x):
    return 0.5 * x * (1.0 + jnp.tanh(_GELU_C * (x + _GELU_A * x * x * x)))


def _gelu_grad(x):
    th = jnp.tanh(_GELU_C * (x + _GELU_A * x * x * x))
    return 0.5 * (1.0 + th) + 0.5 * x * (1.0 - th * th) * _GELU_C * (1.0 + 3.0 * _GELU_A * x * x)


def _layernorm_stats(vf):
    mu = jnp.mean(vf, axis=-1, keepdims=True)
    vc = vf - mu
    r = lax.rsqrt(jnp.mean(vc * vc, axis=-1, keepdims=True) + EPS)
    return vc * r, r


def sgu_mix_fwd(z, ln_g, ln_b, w_s, b_rows, name):
    t, w2 = z.shape
    w = w2 // 2
    dg = w // SGU_GROUPS

    def body(u_ref, v_ref, g_ref, b_ref, ws_ref, bb_ref, y_ref):
        vhat, _ = _layernorm_stats(v_ref[...].astype(F32))
        vn = (vhat * g_ref[...] + b_ref[...]).astype(BF16)
        for g in range(SGU_GROUPS):
            sl = slice(g * dg, (g + 1) * dg)
            mixed = _dot_nn(ws_ref[g], vn[:, sl]) + bb_ref[g]
            y_ref[:, sl] = (u_ref[:, sl].astype(F32) * mixed).astype(BF16)

    vec = pl.BlockSpec((1, w), lambda n: (0, 0))
    whole = pl.BlockSpec((SGU_GROUPS, SGU_CHUNK, SGU_CHUNK), lambda n: (0, 0, 0))
    return pl.pallas_call(
        body, name=name, grid=(t // SGU_CHUNK,),
        in_specs=[pl.BlockSpec((SGU_CHUNK, w), lambda n: (n, 0)), pl.BlockSpec((SGU_CHUNK, w), lambda n: (n, 1)),
                  vec, vec, whole, whole],
        out_specs=pl.BlockSpec((SGU_CHUNK, w), lambda n: (n, 0)),
        out_shape=jax.ShapeDtypeStruct((t, w), BF16),
        compiler_params=_cparams(("parallel",)),
    )(z, z, ln_g.reshape(1, w), ln_b.reshape(1, w), w_s, b_rows)


def sgu_mix_bwd(z, apre, dy, ln_g, ln_b, w_s, b_rows, name):
    t, w2 = z.shape
    w = w2 // 2
    dg = w // SGU_GROUPS

    def body(u_ref, v_ref, au_ref, av_ref, dy_ref, g_ref, b_ref, ws_ref, bb_ref, da_ref, dlg_ref, dlb_ref, dws_ref, dbs_ref):
        @pl.when(pl.program_id(0) == 0)
        def _():
            dlg_ref[...] = jnp.zeros_like(dlg_ref)
            dlb_ref[...] = jnp.zeros_like(dlb_ref)
            dws_ref[...] = jnp.zeros_like(dws_ref)
            dbs_ref[...] = jnp.zeros_like(dbs_ref)

        vhat, r = _layernorm_stats(v_ref[...].astype(F32))
        gam = g_ref[...]
        vn = (vhat * gam + b_ref[...]).astype(BF16)
        ones8 = jnp.ones((8, dg), BF16)
        rid = lax.broadcasted_iota(jnp.int32, (8, SGU_CHUNK), 0)
        dbs = jnp.zeros((8, SGU_CHUNK), F32)
        dvn_parts = []
        for g in range(SGU_GROUPS):
            sl = slice(g * dg, (g + 1) * dg)
            dyg = dy_ref[:, sl].astype(F32)
            mixed = _dot_nn(ws_ref[g], vn[:, sl]) + bb_ref[g]
            da_ref[:, sl] = (dyg * mixed * _gelu_grad(au_ref[:, sl].astype(F32))).astype(BF16)
            dmix = dyg * u_ref[:, sl].astype(F32)
            dm_hi = dmix.astype(BF16)
            dm_lo = (dmix - dm_hi.astype(F32)).astype(BF16)
            dws_ref[g] += _dot_nt(dm_hi, vn[:, sl])
            dbs = dbs + jnp.where(rid == g, _dot_nt(ones8, dm_hi) + _dot_nt(ones8, dm_lo), 0.0)
            dvn_parts.append(_dot_tn(ws_ref[g], dm_hi))
        dbs_ref[...] += dbs
        dvn = jnp.concatenate(dvn_parts, axis=1)
        dlg_ref[...] += jnp.sum(dvn * vhat, axis=0, keepdims=True)
        dlb_ref[...] += jnp.sum(dvn, axis=0, keepdims=True)
        dvh = dvn * gam
        dv = r * (dvh - jnp.mean(dvh, axis=-1, keepdims=True) - vhat * jnp.mean(dvh * vhat, axis=-1, keepdims=True))
        da_ref[:, w:] = (dv * _gelu_grad(av_ref[...].astype(F32))).astype(BF16)

    vec = pl.BlockSpec((1, w), lambda n: (0, 0))
    whole = pl.BlockSpec((SGU_GROUPS, SGU_CHUNK, SGU_CHUNK), lambda n: (0, 0, 0))
    left = pl.BlockSpec((SGU_CHUNK, w), lambda n: (n, 0))
    right = pl.BlockSpec((SGU_CHUNK, w), lambda n: (n, 1))
    return pl.pallas_call(
        body, name=name, grid=(t // SGU_CHUNK,),
        in_specs=[left, right, left, right, left, vec, vec, whole, whole],
        out_specs=[pl.BlockSpec((SGU_CHUNK, w2), lambda n: (n, 0)), vec, vec, whole,
                   pl.BlockSpec((SGU_GROUPS, SGU_CHUNK), lambda n: (0, 0))],
        out_shape=[jax.ShapeDtypeStruct((t, w2), BF16), jax.ShapeDtypeStruct((1, w), F32), jax.ShapeDtypeStruct((1, w), F32),
                   jax.ShapeDtypeStruct((SGU_GROUPS, SGU_CHUNK, SGU_CHUNK), F32),
                   jax.ShapeDtypeStruct((SGU_GROUPS, SGU_CHUNK), F32)],
        compiler_params=_cparams(("arbitrary",)),
    )(z, z, apre, apre, dy, ln_g.reshape(1, w), ln_b.reshape(1, w), w_s, b_rows)


def loss_head(h, g, target, name):
    t, d = h.shape
    tm = min(t, 512)

    def body(h_ref, g_ref, t_ref, loss_ref, dh_ref, dhb_ref, dg_ref):
        @pl.when(pl.program_id(0) == 0)
        def _():
            loss_ref[...] = jnp.zeros_like(loss_ref)
            dg_ref[...] = jnp.zeros_like(dg_ref)

        xf = h_ref[...]
        r = lax.rsqrt(jnp.mean(xf * xf, axis=-1, keepdims=True) + EPS)
        xhat = xf * r
        err = xhat * g_ref[...] - t_ref[...]
        per_tok = jnp.mean(err * err, axis=-1, keepdims=True)
        loss_ref[...] += 0.5 * jnp.sum(per_tok, axis=0, keepdims=True)
        dy = err * (1.0 / d)
        dg_ref[...] += jnp.sum(dy * xhat, axis=0, keepdims=True)
        dxh = dy * g_ref[...]
        dh = r * (dxh - xhat * jnp.mean(dxh * xhat, axis=-1, keepdims=True))
        dh_ref[...] = dh
        dhb_ref[...] = dh.astype(BF16)

    row = pl.BlockSpec((tm, d), lambda i: (i, 0))
    vec = pl.BlockSpec((1, d), lambda i: (0, 0))
    return pl.pallas_call(
        body, name=name, grid=(t // tm,),
        in_specs=[row, vec, row],
        out_specs=[pl.BlockSpec((1, LANES), lambda i: (0, 0)), row, row, vec],
        out_shape=[jax.ShapeDtypeStruct((1, LANES), F32), jax.ShapeDtypeStruct((t, d), F32), jax.ShapeDtypeStruct((t, d), BF16),
                   jax.ShapeDtypeStruct((1, d), F32)],
        compiler_params=_cparams(("arbitrary",)),
    )(h, g.reshape(1, d), target)


def adamw(parts, w, m, v, name, rows=256):
    n, r, c = parts.shape
    tr = _fit(r, rows)
    bc1 = 1.0 - ADAM_B1 ** ADAM_STEP
    bc2 = 1.0 - ADAM_B2 ** ADAM_STEP

    def body(p_ref, w_ref, m_ref, v_ref, g_ref, d_ref, nm_ref, nv_ref):
        g = p_ref[0].astype(F32)
        for j in range(1, n):
            g = g + p_ref[j].astype(F32)
        nm = ADAM_B1 * m_ref[...] + (1.0 - ADAM_B1) * g
        nv = ADAM_B2 * v_ref[...] + (1.0 - ADAM_B2) * (g * g)
        g_ref[...] = g
        nm_ref[...] = nm
        nv_ref[...] = nv
        d_ref[...] = -ADAM_LR * ((nm / bc1) / (jnp.sqrt(nv / bc2) + ADAM_EPS) + ADAM_WD * w_ref[...])

    blk = pl.BlockSpec((tr, c), lambda i: (i, 0))
    return pl.pallas_call(
        body, name=name, grid=(r // tr,),
        in_specs=[pl.BlockSpec((n, tr, c), lambda i: (0, i, 0)), blk, blk, blk],
        out_specs=[blk] * 4,
        out_shape=[jax.ShapeDtypeStruct((r, c), F32)] * 4,
        compiler_params=_cparams(("parallel",)),
    )(parts, w, m, v)


_ANY = pl.BlockSpec(memory_space=pl.ANY)


def _mesh_pos():
    return lax.axis_index("x"), lax.axis_index("y"), lax.axis_index("c")


def all_gather(arrs, name):
    n_arr = len(arrs)

    def body(*refs):
        in_refs, out_refs = refs[:n_arr], refs[n_arr:2 * n_arr]
        send_sems, recv_sems, local_sems = refs[2 * n_arr:]
        x, y, c = _mesh_pos()
        me, sibling = (x, y, c), (x, y, 1 - c)
        chips = [(1 - x, y), (x, 1 - y), (1 - x, 1 - y)]

        def copy(a, k, block, to, from_input=False):
            slot = out_refs[a].at[4 * block[0] + 2 * block[1] + block[2]]
            return pltpu.make_async_remote_copy(
                src_ref=in_refs[a] if from_input else slot, dst_ref=slot,
                send_sem=send_sems.at[7 * a + k], recv_sem=recv_sems.at[7 * a + k],
                device_id=to, device_id_type=pl.DeviceIdType.MESH)

        mine = [pltpu.make_async_copy(in_refs[a], out_refs[a].at[4 * x + 2 * y + c], local_sems.at[a]) for a in range(n_arr)]
        for cp in mine:
            cp.start()
        first = []
        for a in range(n_arr):
            first.append(copy(a, 0, me, sibling, True))
            first += [copy(a, 1 + j, me, (*chip, c), True) for j, chip in enumerate(chips)]
        for cp in first:
            cp.start()
        passed = []
        for a in range(n_arr):
            for j, chip in enumerate(chips):
                copy(a, 1 + j, (*chip, c), me).wait_recv()
                fwd = copy(a, 4 + j, (*chip, c), sibling)
                fwd.start()
                passed.append(fwd)
        for a in range(n_arr):
            copy(a, 0, sibling, me).wait_recv()
            for j, chip in enumerate(chips):
                copy(a, 4 + j, (*chip, 1 - c), me).wait_recv()
        for cp in first + passed:
            cp.wait_send()
        for cp in mine:
            cp.wait()

    return pl.pallas_call(
        body, name=name,
        in_specs=[_ANY] * n_arr, out_specs=[_ANY] * n_arr,
        out_shape=[jax.ShapeDtypeStruct((N_DEV,) + a.shape, a.dtype) for a in arrs],
        scratch_shapes=[pltpu.SemaphoreType.DMA((7 * n_arr,)), pltpu.SemaphoreType.DMA((7 * n_arr,)),
                        pltpu.SemaphoreType.DMA((n_arr,))],
    )(*arrs)


def exchange_partials(scatter, bcast, name):
    def body(sc_ref, bc_ref, sc_out, bc_out, send_sems, recv_sems, local_sems):
        x, y, c = _mesh_pos()
        me = 4 * x + 2 * y + c

        def peer(k):
            px = 1 - x if k & 4 else x
            py = 1 - y if k & 2 else y
            pc = 1 - c if k & 1 else c
            return px, py, pc

        def copies(k):
            px, py, pc = peer(k)
            pid = 4 * px + 2 * py + pc
            kw = dict(device_id=(px, py, pc), device_id_type=pl.DeviceIdType.MESH)
            big = pltpu.make_async_remote_copy(src_ref=sc_ref.at[pid], dst_ref=sc_out.at[me],
                                               send_sem=send_sems.at[2 * (k - 1)], recv_sem=recv_sems.at[2 * (k - 1)], **kw)
            small = pltpu.make_async_remote_copy(src_ref=bc_ref, dst_ref=bc_out.at[me],
                                                 send_sem=send_sems.at[2 * (k - 1) + 1], recv_sem=recv_sems.at[2 * (k - 1) + 1], **kw)
            return big, small

        def arrivals(k):
            px, py, pc = peer(k)
            pid = 4 * px + 2 * py + pc
            kw = dict(device_id=(px, py, pc), device_id_type=pl.DeviceIdType.MESH)
            big = pltpu.make_async_remote_copy(src_ref=sc_ref.at[pid], dst_ref=sc_out.at[pid],
                                               send_sem=send_sems.at[2 * (k - 1)], recv_sem=recv_sems.at[2 * (k - 1)], **kw)
            small = pltpu.make_async_remote_copy(src_ref=bc_ref, dst_ref=bc_out.at[pid],
                                                 send_sem=send_sems.at[2 * (k - 1) + 1], recv_sem=recv_sems.at[2 * (k - 1) + 1], **kw)
            return big, small

        own = [pltpu.make_async_copy(sc_ref.at[me], sc_out.at[me], local_sems.at[0]),
               pltpu.make_async_copy(bc_ref, bc_out.at[me], local_sems.at[1])]
        for cp in own:
            cp.start()
        sent = []
        for k in range(1, N_DEV):
            for cp in copies(k):
                cp.start()
                sent.append(cp)
        for k in range(1, N_DEV):
            for cp in arrivals(k):
                cp.wait_recv()
        for cp in sent:
            cp.wait_send()
        for cp in own:
            cp.wait()

    return pl.pallas_call(
        body, name=name,
        in_specs=[_ANY, _ANY], out_specs=[_ANY, _ANY],
        out_shape=[jax.ShapeDtypeStruct(scatter.shape, scatter.dtype),
                   jax.ShapeDtypeStruct((N_DEV,) + bcast.shape, bcast.dtype)],
        scratch_shapes=[pltpu.SemaphoreType.DMA((14,)), pltpu.SemaphoreType.DMA((14,)), pltpu.SemaphoreType.DMA((2,))],
    )(scatter, bcast)


def attention_fwd(x, norm_g, w_in, sink, qn_g, kn_g, w_out, tables, tag):
    h = rmsnorm_fwd(x, norm_g, f"{tag}_norm")
    (proj,) = matmul(h, w_in, "nn", f"{tag}_proj", [F32], tn=ATT_IN)
    qa, ka, va, qb, kb, vb = qkv_post_fwd(proj, tables, qn_g, kn_g, f"{tag}_qkv")
    cat, lse_a = window_attn_fwd(qa, ka, va, sink, f"{tag}_win")
    cat, lse_b = flash_attn_fwd(qb, kb, vb, cat, f"{tag}_flash")
    (y,) = matmul(cat, w_out, "nn", f"{tag}_out", [F32], epilogue=lambda acc, r: (r + acc,), extras=(x,))
    saved = (x, h, proj, qa, ka, va, qb, kb, vb, cat, lse_a, lse_b)
    return y, saved


def attention_bwd(dy, dyb, saved, norm_g, w_in, sink, qn_g, kn_g, w_out, tables, tag):
    x, h, proj, qa, ka, va, qb, kb, vb, cat, lse_a, lse_b = saved
    (dcat,) = matmul(dyb, w_out, "nt", f"{tag}_dcat", [BF16])
    (dw_out,) = matmul(cat, dyb, "tn", f"{tag}_dwout", [BF16], tk=2048)
    delta = attn_delta(cat, dcat, f"{tag}_delta")
    dqa, dka, dva, dsink = window_attn_bwd(qa, ka, va, sink, dcat, lse_a, delta, f"{tag}_dwin")
    dqb, dkb, dvb = flash_attn_bwd(qb, kb, vb, dcat, lse_b, delta, f"{tag}_dflash")
    dproj, dqg, dkg = qkv_post_bwd(proj, tables, qn_g, kn_g, dqa, dka, dva, dqb, dkb, dvb, f"{tag}_dqkv")
    (dw_in,) = matmul(h, dproj, "tn", f"{tag}_dwin_w", [BF16], tn=ATT_IN // 2, tk=2048)
    dx, dxb, dg = matmul_nt_normbwd(dproj, w_in, x, norm_g, dy, f"{tag}_dx", tk=ATT_IN // 2)
    grp = A_HEADS // A_KV_HEADS
    small = dict(norm=dg[0], sink=dsink[:, :grp, 0].reshape(A_HEADS), qnorm=dqg[0, :HEAD_DIM], knorm=dkg[0, :HEAD_DIM])
    return dx, dxb, dw_in, dw_out, small


def sgu_fwd(x, norm_g, w_in, ln_g, ln_b, w_s, b_rows, w_out, tag):
    h = rmsnorm_fwd(x, norm_g, f"{tag}_norm")
    apre, z = matmul(h, w_in, "nn", f"{tag}_in", [BF16, BF16], epilogue=lambda acc: (acc, _gelu(acc)))
    y = sgu_mix_fwd(z, ln_g, ln_b, w_s, b_rows, f"{tag}_mix")
    (out,) = matmul(y, w_out, "nn", f"{tag}_out", [F32], epilogue=lambda acc, r: (r + acc,), extras=(x,))
    return out, (x, h, apre, z, y)


def sgu_bwd(dout, doutb, saved, norm_g, w_in, ln_g, ln_b, w_s, b_rows, w_out, tag):
    x, h, apre, z, y = saved
    (dy,) = matmul(doutb, w_out, "nt", f"{tag}_dy", [BF16])
    (dw_out,) = matmul(y, doutb, "tn", f"{tag}_dwout", [BF16], tk=2048)
    dapre, dlg, dlb, dws, dbs = sgu_mix_bwd(z, apre, dy, ln_g, ln_b, w_s, b_rows, f"{tag}_dmix")
    (dw_in,) = matmul(h, dapre, "tn", f"{tag}_dwin", [BF16], tk=2048)
    dx, dxb, dg = matmul_nt_normbwd(dapre, w_in, x, norm_g, dout, f"{tag}_dx")
    small = dict(norm=dg[0], ln_g=dlg[0], ln_b=dlb[0], w_s=dws, b_s=dbs)
    return dx, dxb, dw_in, dw_out, small


def _square(r):
    return r * r


def mlp_fwd(x, norm_g, w1, w2, tag):
    h = rmsnorm_fwd(x, norm_g, f"{tag}_norm")
    (r,) = matmul(h, w1, "nn", f"{tag}_up", [BF16], epilogue=lambda acc: (jnp.maximum(acc, 0.0),))
    (y,) = matmul(r, w2, "nn", f"{tag}_down", [F32], epilogue=lambda acc, res: (res + acc,), extras=(x,), a_fn=_square, tk=2048)
    return y, (x, h, r)


def mlp_bwd(dy, dyb, saved, norm_g, w1, w2, tag):
    x, h, r = saved
    (da,) = matmul(dyb, w2, "nt", f"{tag}_da", [BF16], epilogue=lambda acc, rr: (acc * (2.0 * rr.astype(F32)),), extras=(r,))
    (dw2,) = matmul(r, dyb, "tn", f"{tag}_dw2", [BF16], a_fn=_square, tk=2048)
    (dw1,) = matmul(h, da, "tn", f"{tag}_dw1", [BF16], tk=2048)
    dx, dxb, dg = matmul_nt_normbwd(da, w1, x, norm_g, dy, f"{tag}_dx")
    return dx, dxb, dw1, dw2, dg[0]


def _col_full(f, l, k, n):
    return f.reshape(N_DEV, l, k, n // N_DEV).transpose(1, 2, 0, 3).reshape(l, k, n)


def _row_full(f, l, k, n):
    return f.reshape(N_DEV, l, k // N_DEV, n).transpose(1, 0, 2, 3).reshape(l, k, n)


def _col_flat(g):
    l, k, n = g.shape
    return g.reshape(l, k, N_DEV, n // N_DEV).transpose(2, 0, 1, 3).reshape(N_DEV, -1)


def _row_flat(g):
    l, k, n = g.shape
    return g.reshape(l, N_DEV, k // N_DEV, n).transpose(1, 0, 2, 3).reshape(N_DEV, -1)


def _pad_rows(flat, axis, rows_mult):
    n = flat.shape[axis]
    per = rows_mult * FLAT_COLS
    total = -(-n // per) * per
    pad = [(0, 0)] * flat.ndim
    pad[axis] = (0, total - n)
    out = jnp.pad(flat, pad)
    return out.reshape(out.shape[:axis] + (total // FLAT_COLS, FLAT_COLS))


BIG = ("att_w_in", "att_w_out", "sgu_w_in", "sgu_w_out", "mlp_w1", "mlp_w2", "sgu_norm", "sgu_ln_g", "sgu_ln_b")
COL_SHARDED = ("att_w_in", "sgu_w_in", "mlp_w1", "sgu_norm", "sgu_ln_g", "sgu_ln_b")
SMALL = ("att_norm", "att_sink", "att_qnorm", "att_knorm", "sgu_w_s", "sgu_b_s", "mlp_norm", "final_norm")
ORDER = ("att_norm", "att_w_in", "att_sink", "att_qnorm", "att_knorm", "att_w_out", "sgu_norm", "sgu_w_in", "sgu_ln_g",
         "sgu_ln_b", "sgu_w_s", "sgu_b_s", "sgu_w_out", "mlp_norm", "mlp_w1", "mlp_w2", "final_norm")
BIG_ROWS_MULT = 256
SMALL_ROWS_MULT = 8


def _flat_local(blocks, names, rows_mult, dtype):
    return _pad_rows(jnp.concatenate([blocks[n].reshape(-1).astype(dtype) for n in names]), 0, rows_mult)


def _split_local(flat, like, names):
    out, off = {}, 0
    f = flat.reshape(-1)
    for n in names:
        size = like[n].size
        out[n] = f[off:off + size].reshape(like[n].shape)
        off += size
    return out


def kernel(x, att_norm, att_w_in, att_sink, att_qnorm, att_knorm, att_w_out, sgu_norm, sgu_w_in, sgu_ln_g, sgu_ln_b, sgu_w_s, sgu_b_s, sgu_w_out, mlp_norm, mlp_w1, mlp_w2, final_norm, loss_target, m_att_norm, m_att_w_in, m_att_sink, m_att_qnorm, m_att_knorm, m_att_w_out, m_sgu_norm, m_sgu_w_in, m_sgu_ln_g, m_sgu_ln_b, m_sgu_w_s, m_sgu_b_s, m_sgu_w_out, m_mlp_norm, m_mlp_w1, m_mlp_w2, m_final_norm, v_att_norm, v_att_w_in, v_att_sink, v_att_qnorm, v_att_knorm, v_att_w_out, v_sgu_norm, v_sgu_w_in, v_sgu_ln_g, v_sgu_ln_b, v_sgu_w_s, v_sgu_b_s, v_sgu_w_out, v_mlp_norm, v_mlp_w1, v_mlp_w2, v_final_norm):
    w = dict(att_norm=att_norm, att_w_in=att_w_in, att_sink=att_sink, att_qnorm=att_qnorm, att_knorm=att_knorm,
             att_w_out=att_w_out, sgu_norm=sgu_norm, sgu_w_in=sgu_w_in, sgu_ln_g=sgu_ln_g, sgu_ln_b=sgu_ln_b, sgu_w_s=sgu_w_s,
             sgu_b_s=sgu_b_s, sgu_w_out=sgu_w_out, mlp_norm=mlp_norm, mlp_w1=mlp_w1, mlp_w2=mlp_w2, final_norm=final_norm)
    m = dict(att_norm=m_att_norm, att_w_in=m_att_w_in, att_sink=m_att_sink, att_qnorm=m_att_qnorm, att_knorm=m_att_knorm,
             att_w_out=m_att_w_out, sgu_norm=m_sgu_norm, sgu_w_in=m_sgu_w_in, sgu_ln_g=m_sgu_ln_g, sgu_ln_b=m_sgu_ln_b,
             sgu_w_s=m_sgu_w_s, sgu_b_s=m_sgu_b_s, sgu_w_out=m_sgu_w_out, mlp_norm=m_mlp_norm, mlp_w1=m_mlp_w1, mlp_w2=m_mlp_w2,
             final_norm=m_final_norm)
    v = dict(att_norm=v_att_norm, att_w_in=v_att_w_in, att_sink=v_att_sink, att_qnorm=v_att_qnorm, att_knorm=v_att_knorm,
             att_w_out=v_att_w_out, sgu_norm=v_sgu_norm, sgu_w_in=v_sgu_w_in, sgu_ln_g=v_sgu_ln_g, sgu_ln_b=v_sgu_ln_b,
             sgu_w_s=v_sgu_w_s, sgu_b_s=v_sgu_b_s, sgu_w_out=v_sgu_w_out, mlp_norm=v_mlp_norm, mlp_w1=v_mlp_w1, mlp_w2=v_mlp_w2,
             final_norm=v_final_norm)
    loss, grad_x, g, d, nm, nv = train_step(x[0], loss_target[0], w, m, v)
    return (loss, grad_x[None], *[g[n] for n in ORDER], *[d[n] for n in ORDER], *[nm[n] for n in ORDER], *[nv[n] for n in ORDER])


def train_step(x, target, w, m, v):
    t, d_model = x.shape
    n_att, n_sgu, depth = w["att_w_in"].shape[0], w["sgu_w_in"].shape[0], w["mlp_w1"].shape[0]
    d_ff = w["mlp_w1"].shape[2] * N_DEV
    sgu_w = w["sgu_w_out"].shape[1] * N_DEV
    att_o = w["att_w_out"].shape[1] * N_DEV

    big_local = _flat_local(w, BIG, BIG_ROWS_MULT, BF16)
    vec_local = jnp.pad(jnp.concatenate([w[n].reshape(-1) for n in ("sgu_norm", "sgu_ln_g", "sgu_ln_b")]).reshape(-1, LANES),
                        ((0, 2), (0, 0)))
    big_all, vec_all = all_gather([big_local, vec_local], "gather_weights")
    flat = big_all.reshape(N_DEV, -1)
    full, off = {}, 0
    shapes = dict(att_w_in=(n_att, d_model, ATT_IN), att_w_out=(n_att, att_o, d_model), sgu_w_in=(n_sgu, d_model, 2 * sgu_w),
                  sgu_w_out=(n_sgu, sgu_w, d_model), mlp_w1=(depth, d_model, d_ff), mlp_w2=(depth, d_ff, d_model))
    for n in BIG[:6]:
        size = w[n].size
        full[n] = (_col_full if n in COL_SHARDED else _row_full)(flat[:, off:off + size], *shapes[n])
        off += size
    vecs = vec_all[:, :3 * n_sgu].reshape(N_DEV, 3, n_sgu, LANES).transpose(1, 2, 0, 3).reshape(3, n_sgu, N_DEV * LANES)
    sgu_norm_f, sgu_ln_g_f, sgu_ln_b_f = vecs[0], vecs[1], vecs[2]
    w_s_bf = w["sgu_w_s"].astype(BF16)
    b_rows = jnp.broadcast_to(w["sgu_b_s"][:, :, :, None], w["sgu_b_s"].shape + (LANES,))
    tables = _rope_tables(t)

    saved, h = [], x
    for layer in range(depth):
        i = layer // 2
        if layer % 2 == 0:
            h, sv = attention_fwd(h, w["att_norm"][i], full["att_w_in"][i], w["att_sink"][i], w["att_qnorm"][i],
                                  w["att_knorm"][i], full["att_w_out"][i], tables, f"att{i}")
        else:
            h, sv = sgu_fwd(h, sgu_norm_f[i], full["sgu_w_in"][i], sgu_ln_g_f[i], sgu_ln_b_f[i], w_s_bf[i], b_rows[i],
                            full["sgu_w_out"][i], f"sgu{i}")
        h, sm = mlp_fwd(h, w["mlp_norm"][layer], full["mlp_w1"][layer], full["mlp_w2"][layer], f"mlp{layer}")
        saved.append((sv, sm))
    loss_row, dh, dhb, dgf = loss_head(h, w["final_norm"], target, "loss_head")

    gb = {n: [None] * w[n].shape[0] for n in BIG[:6]}
    gs = dict(att_norm=[None] * n_att, att_sink=[None] * n_att, att_qnorm=[None] * n_att, att_knorm=[None] * n_att,
              sgu_norm=[None] * n_sgu, sgu_ln_g=[None] * n_sgu, sgu_ln_b=[None] * n_sgu, sgu_w_s=[None] * n_sgu,
              sgu_b_s=[None] * n_sgu, mlp_norm=[None] * depth)
    for layer in reversed(range(depth)):
        i = layer // 2
        sv, sm = saved[layer]
        dh, dhb, gb["mlp_w1"][layer], gb["mlp_w2"][layer], gs["mlp_norm"][layer] = mlp_bwd(
            dh, dhb, sm, w["mlp_norm"][layer], full["mlp_w1"][layer], full["mlp_w2"][layer], f"mlp{layer}")
        if layer % 2 == 0:
            dh, dhb, gb["att_w_in"][i], gb["att_w_out"][i], sm_g = attention_bwd(
                dh, dhb, sv, w["att_norm"][i], full["att_w_in"][i], w["att_sink"][i], w["att_qnorm"][i], w["att_knorm"][i],
                full["att_w_out"][i], tables, f"att{i}")
            gs["att_norm"][i], gs["att_sink"][i] = sm_g["norm"], sm_g["sink"]
            gs["att_qnorm"][i], gs["att_knorm"][i] = sm_g["qnorm"], sm_g["knorm"]
        else:
            dh, dhb, gb["sgu_w_in"][i], gb["sgu_w_out"][i], sm_g = sgu_bwd(
                dh, dhb, sv, sgu_norm_f[i], full["sgu_w_in"][i], sgu_ln_g_f[i], sgu_ln_b_f[i], w_s_bf[i], b_rows[i],
                full["sgu_w_out"][i], f"sgu{i}")
            gs["sgu_norm"][i], gs["sgu_ln_g"][i], gs["sgu_ln_b"][i] = sm_g["norm"], sm_g["ln_g"], sm_g["ln_b"]
            gs["sgu_w_s"][i], gs["sgu_b_s"][i] = sm_g["w_s"], sm_g["b_s"]
    grad_x = dh

    parts = []
    for n in BIG[:6]:
        stacked = jnp.stack(gb[n])
        parts.append(_col_flat(stacked) if n in COL_SHARDED else _row_flat(stacked))
    for n in ("sgu_norm", "sgu_ln_g", "sgu_ln_b"):
        parts.append(_col_flat(jnp.stack(gs[n])[:, None, :]).astype(BF16))
    scatter = _pad_rows(jnp.concatenate(parts, axis=1), 1, BIG_ROWS_MULT)
    small_g = dict(att_norm=jnp.stack(gs["att_norm"]), att_sink=jnp.stack(gs["att_sink"]), att_qnorm=jnp.stack(gs["att_qnorm"]),
                   att_knorm=jnp.stack(gs["att_knorm"]), sgu_w_s=jnp.stack(gs["sgu_w_s"]), sgu_b_s=jnp.stack(gs["sgu_b_s"]),
                   mlp_norm=jnp.stack(gs["mlp_norm"]), final_norm=dgf[0])
    bcast = _flat_local(small_g, SMALL, SMALL_ROWS_MULT, F32)
    big_parts, small_parts = exchange_partials(scatter, bcast, "exchange_grads")
    outs = {}
    for names, got, mult, tag in ((BIG, big_parts, BIG_ROWS_MULT, "adamw_sharded"), (SMALL, small_parts, SMALL_ROWS_MULT, "adamw_replicated")):
        res = adamw(got, _flat_local(w, names, mult, F32), _flat_local(m, names, mult, F32), _flat_local(v, names, mult, F32), tag)
        for kind, flat_out in zip(("g", "d", "m", "v"), res):
            outs.setdefault(kind, {}).update(_split_local(flat_out, w, names))
    loss = lax.psum(loss_row[0, 0], ("x", "y", "c"))
    return loss, grad_x, outs["g"], outs["d"], outs["m"], outs["v"]
```

```python
import functools
import math

import jax
import jax.numpy as jnp
from jax import lax
from jax.experimental import pallas as pl
from jax.experimental.pallas import tpu as pltpu

F32 = jnp.float32
BF16 = jnp.bfloat16

HEAD_DIM = 64
A_HEADS = 8
A_KV_HEADS = 2
B_HEADS = 8
B_KV_HEADS = 2
WINDOW = 128
BLOCK = 128
ROPE_THETA = 10000.0
GRID_W = 64
SGU_GROUPS = 8
SGU_CHUNK = 128
EPS = 1e-6
SCALE = HEAD_DIM ** -0.5
NEG = -1e30
LOG2E = math.log2(math.e)
LN2 = math.log(2.0)

A_Q = A_HEADS * HEAD_DIM
A_KV = A_KV_HEADS * HEAD_DIM
B_Q = B_HEADS * HEAD_DIM
B_KV = B_KV_HEADS * HEAD_DIM
OFF_QA, OFF_KA, OFF_VA = 0, A_Q, A_Q + A_KV
OFF_QB = A_Q + 2 * A_KV
OFF_KB = OFF_QB + B_Q
OFF_VB = OFF_KB + B_KV
ATT_IN = OFF_VB + B_KV

ADAM_LR = 0.001
ADAM_B1 = 0.9
ADAM_B2 = 0.999
ADAM_EPS = 1e-08
ADAM_WD = 0.01
ADAM_STEP = 10

N_DEV = 8
LANES = 128
V7X_VMEM_LIMIT = 56 * 1024 * 1024
FLAT_COLS = 1024


def _cparams(sem, vmem=V7X_VMEM_LIMIT):
    return pltpu.CompilerParams(dimension_semantics=sem, vmem_limit_bytes=vmem)


def _dot_nn(a, b):
    return lax.dot_general(a, b, (((1,), (0,)), ((), ())), preferred_element_type=F32)


def _dot_nt(a, b):
    return lax.dot_general(a, b, (((1,), (1,)), ((), ())), preferred_element_type=F32)


def _dot_tn(a, b):
    return lax.dot_general(a, b, (((0,), (0,)), ((), ())), preferred_element_type=F32)


def _bf(x):
    return x if x.dtype == BF16 else x.astype(BF16)


def _lane(shape):
    return lax.broadcasted_iota(jnp.int32, shape, len(shape) - 1)


def _seg_matrix(rows_lo, rows_hi):
    r = lax.broadcasted_iota(jnp.int32, (LANES, LANES), 0)
    return jnp.where((r >= rows_lo) & (r < rows_hi), 1.0, 0.0).astype(BF16)


def _group_matrix(width):
    r = lax.broadcasted_iota(jnp.int32, (LANES, LANES), 0)
    c = lax.broadcasted_iota(jnp.int32, (LANES, LANES), 1)
    return jnp.where((r // width) == (c // width), 1.0, 0.0).astype(BF16)


def _dot_f32_by_ones(s, ones_bf16):
    hi = s.astype(BF16)
    lo = (s - hi.astype(F32)).astype(BF16)
    return _dot_nn(hi, ones_bf16) + _dot_nn(lo, ones_bf16)


def _swap_halves(x, width):
    half = width // 2
    first = (_lane(x.shape) % width) < half
    return jnp.where(first, pltpu.roll(x, LANES - half, 1), pltpu.roll(x, half, 1))


def rmsnorm_fwd(x, g, name):
    t, d = x.shape
    tm = min(t, 512)

    def body(x_ref, g_ref, h_ref):
        xf = x_ref[...]
        r = lax.rsqrt(jnp.mean(xf * xf, axis=-1, keepdims=True) + EPS)
        h_ref[...] = (xf * r * g_ref[...]).astype(BF16)

    return pl.pallas_call(
        body, name=name, grid=(t // tm,),
        in_specs=[pl.BlockSpec((tm, d), lambda i: (i, 0)), pl.BlockSpec((1, d), lambda i: (0, 0))],
        out_specs=pl.BlockSpec((tm, d), lambda i: (i, 0)),
        out_shape=jax.ShapeDtypeStruct((t, d), BF16),
        compiler_params=_cparams(("parallel",)),
    )(x, g.reshape(1, d))


def _fit(n, want):
    t = min(n, want)
    while n % t:
        t //= 2
    return t


class Gathered:
    def __init__(self, arr, kind, layer):
        self.arr, self.kind, self.layer = arr, kind, layer
        _, _, self.rows, self.cols = arr.shape
        self.shape = (N_DEV * self.rows, self.cols) if kind == "row" else (self.rows, N_DEV * self.cols)


def _b_operand(b, mode, tn, tk, idx):
    dot = {"nn": _dot_nn, "nt": _dot_nt, "tn": _dot_tn}[mode]
    if not isinstance(b, Gathered):
        if mode == "nt":
            spec = pl.BlockSpec((tn, tk), lambda *g: idx(*g))
        else:
            spec = pl.BlockSpec((tk, tn), lambda *g: idx(*g)[::-1])
        return b, spec, lambda av, ref: dot(av, _bf(ref[...]))
    lay, rows, cols = b.layer, b.rows, b.cols
    if mode == "nn" and b.kind == "col":
        assert tn == cols
        spec = pl.BlockSpec((None, None, tk, cols), lambda *g: (idx(*g)[0], lay, idx(*g)[1], 0))
        return b.arr, spec, lambda av, ref: _dot_nn(av, ref[...])
    if mode == "nn" and b.kind == "row":
        s = tk // rows
        assert s * rows == tk
        spec = pl.BlockSpec((s, None, rows, tn), lambda *g: (idx(*g)[1], lay, 0, idx(*g)[0]))
        return b.arr, spec, lambda av, ref: _dot_nn(av, ref[...].reshape(s * rows, tn))
    if mode == "nt" and b.kind == "row":
        s = tn // rows
        assert s * rows == tn
        spec = pl.BlockSpec((s, None, rows, tk), lambda *g: (idx(*g)[0], lay, 0, idx(*g)[1]))
        return b.arr, spec, lambda av, ref: _dot_nt(av, ref[...].reshape(s * rows, tk))
    if mode == "nt" and b.kind == "col":
        s = tk // cols
        assert s * cols == tk
        spec = pl.BlockSpec((s, None, tn, cols), lambda *g: (idx(*g)[1], lay, idx(*g)[0], 0))

        def prod(av, ref):
            tot = _dot_nt(av[:, :cols], ref[0])
            for c in range(1, s):
                tot = tot + _dot_nt(av[:, c * cols:(c + 1) * cols], ref[c])
            return tot

        return b.arr, spec, prod
    raise NotImplementedError((mode, b.kind))


def matmul(a, b, mode, name, out_dtypes, epilogue=None, extras=(), a_fn=None, out_shards=False, tm=1024, tn=1024, tk=1024):
    (m, k) = a.shape[::-1] if mode == "tn" else a.shape
    n = b.shape[0] if mode == "nt" else b.shape[1]
    if out_shards:
        tn = n // N_DEV
    if isinstance(b, Gathered) and mode == "nn" and b.kind == "col":
        tn = b.cols
    tm, tn, tk = _fit(m, tm), _fit(n, tn), _fit(k, tk)
    nk = k // tk
    n_ex, n_out = len(extras), len(out_dtypes)
    if epilogue is None:
        epilogue = lambda acc: (acc,)
    b_arr, b_spec, prod = _b_operand(b, mode, tn, tk, lambda i, j, kk: (j, kk))

    def body(*refs):
        a_ref, b_ref = refs[0], refs[1]
        ex_refs = refs[2:2 + n_ex]
        out_refs = refs[2 + n_ex:2 + n_ex + n_out]
        acc_ref = refs[2 + n_ex + n_out]
        kk = pl.program_id(2)
        av = _bf(a_ref[...])
        if a_fn is not None:
            av = a_fn(av)
        part = prod(av, b_ref)

        @pl.when(kk == 0)
        def _():
            acc_ref[...] = part

        @pl.when(kk > 0)
        def _():
            acc_ref[...] += part

        @pl.when(kk == nk - 1)
        def _():
            outs = epilogue(acc_ref[...], *[r[...] for r in ex_refs])
            for r, o in zip(out_refs, outs):
                r[...] = o.astype(r.dtype)

    if mode == "tn":
        a_spec = pl.BlockSpec((tk, tm), lambda i, j, kk: (kk, i))
    else:
        a_spec = pl.BlockSpec((tm, tk), lambda i, j, kk: (i, kk))
    mn_spec = pl.BlockSpec((tm, tn), lambda i, j, kk: (i, j))
    if out_shards:
        out_spec = pl.BlockSpec((None, tm, tn), lambda i, j, kk: (j, i, 0))
        out_shape = [jax.ShapeDtypeStruct((N_DEV, m, tn), dt) for dt in out_dtypes]
    else:
        out_spec = mn_spec
        out_shape = [jax.ShapeDtypeStruct((m, n), dt) for dt in out_dtypes]
    outs = pl.pallas_call(
        body, name=name, grid=(m // tm, n // tn, nk),
        in_specs=[a_spec, b_spec] + [mn_spec] * n_ex,
        out_specs=[out_spec] * n_out,
        out_shape=out_shape,
        scratch_shapes=[pltpu.VMEM((tm, tn), F32)],
        compiler_params=_cparams(("parallel", "parallel", "arbitrary")),
    )(a, b_arr, *extras)
    return outs


def matmul_nt_normbwd(dz, w, x, g, dres, name, tm=1024, tk=1024):
    m, k = dz.shape
    d = w.shape[0]
    tm, tk = _fit(m, tm), _fit(k, tk)
    nk = k // tk
    w_arr, w_spec, prod = _b_operand(w, "nt", d, tk, lambda i, kk: (0, kk))

    def body(dz_ref, w_ref, x_ref, g_ref, dres_ref, dx_ref, dxb_ref, dg_ref, acc_ref):
        i, kk = pl.program_id(0), pl.program_id(1)
        part = prod(_bf(dz_ref[...]), w_ref)

        @pl.when(kk == 0)
        def _():
            acc_ref[...] = part

        @pl.when(kk > 0)
        def _():
            acc_ref[...] += part

        @pl.when((i == 0) & (kk == 0))
        def _():
            dg_ref[...] = jnp.zeros_like(dg_ref)

        @pl.when(kk == nk - 1)
        def _():
            dh = acc_ref[...]
            xf = x_ref[...]
            r = lax.rsqrt(jnp.mean(xf * xf, axis=-1, keepdims=True) + EPS)
            xhat = xf * r
            dg_ref[...] += jnp.sum(dh * xhat, axis=0, keepdims=True)
            dxh = dh * g_ref[...]
            dx = r * (dxh - xhat * jnp.mean(dxh * xhat, axis=-1, keepdims=True))
            out = dres_ref[...] + dx
            dx_ref[...] = out
            dxb_ref[...] = out.astype(BF16)

    row = pl.BlockSpec((tm, d), lambda i, kk: (i, 0))
    vec = pl.BlockSpec((1, d), lambda i, kk: (0, 0))
    return pl.pallas_call(
        body, name=name, grid=(m // tm, nk),
        in_specs=[pl.BlockSpec((tm, tk), lambda i, kk: (i, kk)), w_spec, row, vec, row],
        out_specs=[row, row, vec],
        out_shape=[jax.ShapeDtypeStruct((m, d), F32), jax.ShapeDtypeStruct((m, d), BF16), jax.ShapeDtypeStruct((1, d), F32)],
        scratch_shapes=[pltpu.VMEM((tm, d), F32)],
        compiler_params=_cparams(("arbitrary", "arbitrary")),
    )(dz, w_arr, x, g.reshape(1, d), dres)


def _rope_tables(t):
    pos = jnp.arange(t)

    def angles(p, dim):
        freqs = ROPE_THETA ** (-jnp.arange(0, dim, 2, dtype=F32) / dim)
        return p.astype(F32)[:, None] * freqs[None, :]

    a1 = angles(pos, HEAD_DIM)
    cos_a = jnp.concatenate([jnp.cos(a1), jnp.cos(a1)], axis=-1)
    sin_a = jnp.concatenate([-jnp.sin(a1), jnp.sin(a1)], axis=-1)
    ar = angles(pos // GRID_W, HEAD_DIM // 2)
    ac = angles(pos % GRID_W, HEAD_DIM // 2)
    cos_b = jnp.concatenate([jnp.cos(ar), jnp.cos(ar), jnp.cos(ac), jnp.cos(ac)], axis=-1)
    sin_b = jnp.concatenate([-jnp.sin(ar), jnp.sin(ar), -jnp.sin(ac), jnp.sin(ac)], axis=-1)
    two = lambda z: jnp.concatenate([z, z], axis=-1)
    return two(cos_a), two(sin_a), two(cos_b), two(sin_b)


def _headnorm(xs, gmat):
    return lax.rsqrt(_dot_f32_by_ones(xs * xs, gmat) * (1.0 / HEAD_DIM) + EPS)


def qkv_post_fwd(proj, tables, qn_g, kn_g, name):
    t = proj.shape[0]
    tm = min(t, 256)
    cos_a, sin_a, cos_b, sin_b = tables
    g2 = lambda g: jnp.concatenate([g, g]).reshape(1, LANES)

    def body(p_ref, ca_ref, sa_ref, cb_ref, sb_ref, qg_ref, kg_ref, qa_ref, ka_ref, va_ref, qb_ref, kb_ref, vb_ref):
        ca, sa, cb, sb = ca_ref[...], sa_ref[...], cb_ref[...], sb_ref[...]
        gmat = _group_matrix(HEAD_DIM)

        def rope_a(xs):
            return xs * ca + _swap_halves(xs, HEAD_DIM) * sa

        def norm_rope_b(xs, g):
            y = xs * _headnorm(xs, gmat) * g
            return y * cb + _swap_halves(y, HEAD_DIM // 2) * sb

        for c in range(A_Q // LANES):
            qa_ref[:, c * LANES:(c + 1) * LANES] = rope_a(p_ref[:, OFF_QA + c * LANES:OFF_QA + (c + 1) * LANES]).astype(BF16)
        ka_ref[...] = rope_a(p_ref[:, OFF_KA:OFF_KA + LANES]).astype(BF16)
        va_ref[...] = p_ref[:, OFF_VA:OFF_VA + LANES].astype(BF16)
        for c in range(B_Q // LANES):
            qb_ref[:, c * LANES:(c + 1) * LANES] = norm_rope_b(
                p_ref[:, OFF_QB + c * LANES:OFF_QB + (c + 1) * LANES], qg_ref[...]).astype(BF16)
        kb_ref[...] = norm_rope_b(p_ref[:, OFF_KB:OFF_KB + LANES], kg_ref[...]).astype(BF16)
        vb_ref[...] = p_ref[:, OFF_VB:OFF_VB + LANES].astype(BF16)

    tab = pl.BlockSpec((tm, LANES), lambda i: (i, 0))
    vec = pl.BlockSpec((1, LANES), lambda i: (0, 0))
    wide = pl.BlockSpec((tm, A_Q), lambda i: (i, 0))
    return pl.pallas_call(
        body, name=name, grid=(t // tm,),
        in_specs=[pl.BlockSpec((tm, ATT_IN), lambda i: (i, 0)), tab, tab, tab, tab, vec, vec],
        out_specs=[wide, tab, tab, wide, tab, tab],
        out_shape=[jax.ShapeDtypeStruct((t, A_Q), BF16), jax.ShapeDtypeStruct((t, LANES), BF16),
                   jax.ShapeDtypeStruct((t, LANES), BF16), jax.ShapeDtypeStruct((t, B_Q), BF16),
                   jax.ShapeDtypeStruct((t, LANES), BF16), jax.ShapeDtypeStruct((t, LANES), BF16)],
        compiler_params=_cparams(("parallel",)),
    )(proj, cos_a, sin_a, cos_b, sin_b, g2(qn_g), g2(kn_g))


def qkv_post_bwd(proj, tables, qn_g, kn_g, dqa, dka, dva, dqb, dkb, dvb, name):
    t = proj.shape[0]
    tm = min(t, 256)
    cos_a, sin_a, cos_b, sin_b = tables
    g2 = lambda g: jnp.concatenate([g, g]).reshape(1, LANES)

    def body(p_ref, ca_ref, sa_ref, cb_ref, sb_ref, qg_ref, kg_ref, dqa_ref, dka_ref, dva_ref, dqb_ref, dkb_ref, dvb_ref,
             dp_ref, dqg_ref, dkg_ref):
        ca, sa, cb, sb = ca_ref[...], sa_ref[...], cb_ref[...], sb_ref[...]
        gmat = _group_matrix(HEAD_DIM)

        @pl.when(pl.program_id(0) == 0)
        def _():
            dqg_ref[...] = jnp.zeros_like(dqg_ref)
            dkg_ref[...] = jnp.zeros_like(dkg_ref)

        def rope_a_bwd(dy):
            return dy * ca + _swap_halves(dy * sa, HEAD_DIM)

        def norm_rope_b_bwd(dout, xs, g):
            dy = dout * cb + _swap_halves(dout * sb, HEAD_DIM // 2)
            r = _headnorm(xs, gmat)
            xhat = xs * r
            dxh = dy * g
            mean = _dot_f32_by_ones(dxh * xhat, gmat) * (1.0 / HEAD_DIM)
            return r * (dxh - xhat * mean), jnp.sum(dy * xhat, axis=0, keepdims=True)

        for c in range(A_Q // LANES):
            sl = slice(c * LANES, (c + 1) * LANES)
            dp_ref[:, OFF_QA + c * LANES:OFF_QA + (c + 1) * LANES] = rope_a_bwd(dqa_ref[:, sl].astype(F32)).astype(BF16)
        dp_ref[:, OFF_KA:OFF_KA + LANES] = rope_a_bwd(dka_ref[0] + dka_ref[1]).astype(BF16)
        dp_ref[:, OFF_VA:OFF_VA + LANES] = (dva_ref[0] + dva_ref[1]).astype(BF16)
        dqg = jnp.zeros((1, LANES), F32)
        for c in range(B_Q // LANES):
            sl = slice(c * LANES, (c + 1) * LANES)
            dx, dg = norm_rope_b_bwd(dqb_ref[:, sl].astype(F32), p_ref[:, OFF_QB + c * LANES:OFF_QB + (c + 1) * LANES], qg_ref[...])
            dp_ref[:, OFF_QB + c * LANES:OFF_QB + (c + 1) * LANES] = dx.astype(BF16)
            dqg = dqg + dg
        dqg_ref[...] += dqg
        dx, dg = norm_rope_b_bwd(dkb_ref[0] + dkb_ref[1], p_ref[:, OFF_KB:OFF_KB + LANES], kg_ref[...])
        dp_ref[:, OFF_KB:OFF_KB + LANES] = dx.astype(BF16)
        dkg_ref[...] += dg
        dp_ref[:, OFF_VB:OFF_VB + LANES] = (dvb_ref[0] + dvb_ref[1]).astype(BF16)

        @pl.when(pl.program_id(0) == t // tm - 1)
        def _():
            dqg_ref[...] = dqg_ref[...] + pltpu.roll(dqg_ref[...], HEAD_DIM, 1)
            dkg_ref[...] = dkg_ref[...] + pltpu.roll(dkg_ref[...], HEAD_DIM, 1)

    tab = pl.BlockSpec((tm, LANES), lambda i: (i, 0))
    vec = pl.BlockSpec((1, LANES), lambda i: (0, 0))
    wide = pl.BlockSpec((tm, A_Q), lambda i: (i, 0))
    slab = pl.BlockSpec((2, tm, LANES), lambda i: (0, i, 0))
    full = pl.BlockSpec((tm, ATT_IN), lambda i: (i, 0))
    return pl.pallas_call(
        body, name=name, grid=(t // tm,),
        in_specs=[full, tab, tab, tab, tab, vec, vec, wide, slab, slab, wide, slab, slab],
        out_specs=[full, vec, vec],
        out_shape=[jax.ShapeDtypeStruct((t, ATT_IN), BF16), jax.ShapeDtypeStruct((1, LANES), F32),
                   jax.ShapeDtypeStruct((1, LANES), F32)],
        compiler_params=_cparams(("arbitrary",)),
    )(proj, cos_a, sin_a, cos_b, sin_b, g2(qn_g), g2(kn_g), dqa, dka, dva, dqb, dkb, dvb)


def _head_to_half(xs, head_half, kv_half):
    low = _lane(xs.shape) < HEAD_DIM
    kept = jnp.where(low if head_half == 0 else jnp.logical_not(low), xs, 0.0)
    return jnp.where(kv_half == head_half, kept, pltpu.roll(kept, HEAD_DIM, 1))


def _halves_to_heads(r0, r1, kv_half):
    low = _lane(r0.shape) < HEAD_DIM
    a = jnp.where(kv_half == 0, r0, pltpu.roll(r0, HEAD_DIM, 1))
    b = jnp.where(kv_half == 1, r1, pltpu.roll(r1, HEAD_DIM, 1))
    return jnp.where(low, a, b)


def attn_delta(o, do, name):
    t, w = o.shape
    tm = min(t, 512)
    n_heads = w // HEAD_DIM

    def body(o_ref, do_ref, d_ref):
        lo, hi = _seg_matrix(0, HEAD_DIM), _seg_matrix(HEAD_DIM, LANES)
        for c in range(w // LANES):
            sl = slice(c * LANES, (c + 1) * LANES)
            s = o_ref[:, sl].astype(F32) * do_ref[:, sl].astype(F32)
            d_ref[2 * c] = _dot_f32_by_ones(s, lo)
            d_ref[2 * c + 1] = _dot_f32_by_ones(s, hi)

    blk = pl.BlockSpec((tm, w), lambda i: (i, 0))
    return pl.pallas_call(
        body, name=name, grid=(t // tm,),
        in_specs=[blk, blk],
        out_specs=pl.BlockSpec((n_heads, tm, LANES), lambda i: (0, i, 0)),
        out_shape=jax.ShapeDtypeStruct((n_heads, t, LANES), F32),
        compiler_params=_cparams(("parallel",)),
    )(o, do)


def _band_mask(n, t, rows_rep):
    qi = lax.broadcasted_iota(jnp.int32, (BLOCK, 3 * BLOCK), 0)
    kj = lax.broadcasted_iota(jnp.int32, (BLOCK, 3 * BLOCK), 1)
    rel = kj - BLOCK - qi
    kpos = n * BLOCK - BLOCK + kj
    ok = (jnp.abs(rel) <= WINDOW) & (kpos >= 0) & (kpos < t)
    return jnp.concatenate([ok] * rows_rep, axis=0)


def _band_specs(t, pos_of):
    nb = t // BLOCK
    prev = pl.BlockSpec((BLOCK, LANES), lambda *g: (jnp.maximum(pos_of(*g) - 1, 0), 0))
    cur = pl.BlockSpec((BLOCK, LANES), lambda *g: (pos_of(*g), 0))
    nxt = pl.BlockSpec((BLOCK, LANES), lambda *g: (jnp.minimum(pos_of(*g) + 1, nb - 1), 0))
    return [prev, cur, nxt]


def window_attn_fwd(q, k, v, sink, name):
    t = q.shape[0]
    nb = t // BLOCK

    def body(sink_ref, q_ref, kp_ref, kc_ref, kn_ref, vp_ref, vc_ref, vn_ref, o_ref, lse_ref):
        j, n = pl.program_id(0), pl.program_id(1)
        kvh = j // 2
        qf = q_ref[...].astype(F32) * SCALE
        qs = jnp.concatenate([_head_to_half(qf, 0, kvh), _head_to_half(qf, 1, kvh)], axis=0).astype(BF16)
        kband = jnp.concatenate([kp_ref[...], kc_ref[...], kn_ref[...]], axis=0)
        vband = jnp.concatenate([vp_ref[...], vc_ref[...], vn_ref[...]], axis=0)
        s = jnp.where(_band_mask(n, t, 2), _dot_nt(qs, kband), NEG)
        row = lax.broadcasted_iota(jnp.int32, (2 * BLOCK, 1), 0)
        sk = jnp.where(row < BLOCK, sink_ref[2 * j], sink_ref[2 * j + 1])
        m = jnp.maximum(jnp.max(s, axis=-1, keepdims=True), sk)
        p = jnp.exp(s - m)
        denom = jnp.sum(p, axis=-1, keepdims=True) + jnp.exp(sk - m)
        o = _dot_nn(p.astype(BF16), vband) / denom
        o_ref[...] = _halves_to_heads(o[:BLOCK], o[BLOCK:], kvh).astype(BF16)
        lse = jnp.broadcast_to(m + jnp.log(denom), (2 * BLOCK, LANES))
        lse_ref[0] = lse[:BLOCK]
        lse_ref[1] = lse[BLOCK:]

    band = _band_specs(t, lambda j, n: n)
    qspec = pl.BlockSpec((BLOCK, LANES), lambda j, n: (n, j))
    return pl.pallas_call(
        body, name=name, grid=(A_HEADS // 2, nb),
        in_specs=[pl.BlockSpec(memory_space=pltpu.SMEM), qspec] + band + band,
        out_specs=[qspec, pl.BlockSpec((2, BLOCK, LANES), lambda j, n: (j, n, 0))],
        out_shape=[jax.ShapeDtypeStruct((t, A_Q + B_Q), BF16), jax.ShapeDtypeStruct((A_HEADS, t, LANES), F32)],
        compiler_params=_cparams(("parallel", "parallel")),
    )(sink, q, k, k, k, v, v, v)


def window_attn_bwd(q, k, v, sink, do, lse, delta, name):
    t = q.shape[0]
    nb = t // BLOCK
    grp = A_HEADS // A_KV_HEADS
    gw = grp * HEAD_DIM

    def body(sink_ref, q_ref, do_ref, kp_ref, kc_ref, kn_ref, vp_ref, vc_ref, vn_ref, lse_ref, dl_ref,
             dq_ref, dk_ref, dv_ref, ds_ref):
        kvh, n = pl.program_id(0), pl.program_id(1)

        @pl.when(n == 0)
        def _():
            dk_ref[...] = jnp.zeros_like(dk_ref)
            dv_ref[...] = jnp.zeros_like(dv_ref)
            ds_ref[...] = jnp.zeros_like(ds_ref)

        qparts, doparts = [], []
        for hh in range(grp):
            sl = slice((hh // 2) * LANES, (hh // 2 + 1) * LANES)
            qparts.append(_head_to_half(q_ref[:, sl].astype(F32) * SCALE, hh % 2, kvh))
            doparts.append(_head_to_half(do_ref[:, sl].astype(F32), hh % 2, kvh))
        qs = jnp.concatenate(qparts, axis=0).astype(BF16)
        dos = jnp.concatenate(doparts, axis=0).astype(BF16)
        lse_b = jnp.concatenate([lse_ref[hh] for hh in range(grp)], axis=0)
        dl_b = jnp.concatenate([dl_ref[hh] for hh in range(grp)], axis=0)
        kband = jnp.concatenate([kp_ref[...], kc_ref[...], kn_ref[...]], axis=0)
        vband = jnp.concatenate([vp_ref[...], vc_ref[...], vn_ref[...]], axis=0)
        s = jnp.where(_band_mask(n, t, grp), _dot_nt(qs, kband), NEG)
        p = jnp.exp(s - lse_b[:, :1])
        dp = _dot_nt(dos, vband)
        dsc = (p * (dp - dl_b[:, :1])).astype(BF16)
        dvb = _dot_tn(p.astype(BF16), dos)
        dkb = _dot_tn(dsc, qs)
        dq = _dot_nn(dsc, kband) * SCALE
        for c in range(grp // 2):
            dq_ref[:, c * LANES:(c + 1) * LANES] = _halves_to_heads(
                dq[2 * c * BLOCK:(2 * c + 1) * BLOCK], dq[(2 * c + 1) * BLOCK:(2 * c + 2) * BLOCK], kvh).astype(dq_ref.dtype)
        blocks = [jnp.maximum(n - 1, 0), n, jnp.minimum(n + 1, nb - 1)]
        for b3, blk in enumerate(blocks):
            rows = pl.ds(pl.multiple_of(blk * BLOCK, BLOCK), BLOCK)
            dk_ref[0, rows, :] += dkb[b3 * BLOCK:(b3 + 1) * BLOCK]
            dv_ref[0, rows, :] += dvb[b3 * BLOCK:(b3 + 1) * BLOCK]
        rid = lax.broadcasted_iota(jnp.int32, (8, LANES), 0)
        upd = jnp.zeros((8, LANES), F32)
        for hh in range(grp):
            sk = sink_ref[kvh * grp + hh]
            rs = slice(hh * BLOCK, (hh + 1) * BLOCK)
            tot = jnp.sum(jnp.exp(sk - lse_b[rs]) * dl_b[rs], axis=0, keepdims=True)
            upd = upd + jnp.where(rid == hh, -tot, 0.0)
        ds_ref[0] += upd

    band = _band_specs(t, lambda kvh, n: n)
    qspec = pl.BlockSpec((BLOCK, gw), lambda kvh, n: (n, kvh))
    stat = pl.BlockSpec((grp, BLOCK, LANES), lambda kvh, n: (kvh, n, 0))
    slab = pl.BlockSpec((1, t, LANES), lambda kvh, n: (kvh, 0, 0))
    return pl.pallas_call(
        body, name=name, grid=(A_KV_HEADS, nb),
        in_specs=[pl.BlockSpec(memory_space=pltpu.SMEM), qspec, qspec] + band + band + [stat, stat],
        out_specs=[qspec, slab, slab, pl.BlockSpec((1, 8, LANES), lambda kvh, n: (kvh, 0, 0))],
        out_shape=[jax.ShapeDtypeStruct((t, A_Q), BF16), jax.ShapeDtypeStruct((A_KV_HEADS, t, LANES), F32),
                   jax.ShapeDtypeStruct((A_KV_HEADS, t, LANES), F32), jax.ShapeDtypeStruct((A_KV_HEADS, 8, LANES), F32)],
        compiler_params=_cparams(("arbitrary", "arbitrary")),
    )(sink, q, do, k, k, k, v, v, v, lse, delta)


def flash_attn_fwd(q, k, v, cat, name, exchange=None, tq=256, tk=2048):
    t = q.shape[0]
    tq, tk = _fit(t, tq), _fit(t, tk)
    nk = t // tk

    def body(q_ref, k_ref, v_ref, cat_ref, o_ref, lse_ref, m_ref, acc_ref):
        del cat_ref
        kvh = pl.program_id(0) // 2
        qf = q_ref[...].astype(F32) * (SCALE * LOG2E)
        qs = jnp.concatenate([_head_to_half(qf, 0, kvh), _head_to_half(qf, 1, kvh)], axis=0).astype(BF16)
        m_ref[...] = jnp.full_like(m_ref, NEG)
        acc_ref[...] = jnp.zeros_like(acc_ref)
        mine = (_lane((tk, LANES)) < HEAD_DIM) == (kvh == 0)

        def chunk(c, carry):
            rows = pl.ds(pl.multiple_of(c * tk, tk), tk)
            s = _dot_nt(qs, k_ref[rows, :])
            vb = jnp.where(mine, v_ref[rows, :], jnp.ones((), BF16))
            m_prev = m_ref[...]
            m_new = jnp.maximum(m_prev, jnp.max(s, axis=-1, keepdims=True))
            p = jnp.exp2(s - m_new[:, :1]).astype(BF16)
            acc_ref[...] = jnp.exp2(m_prev - m_new) * acc_ref[...] + _dot_nn(p, vb)
            m_ref[...] = m_new
            return carry

        lax.fori_loop(0, nk, chunk, 0)
        acc = acc_ref[...]
        other = pltpu.roll(acc, HEAD_DIM, 1)
        o = acc / other
        o_ref[...] = _halves_to_heads(o[:tq], o[tq:], kvh).astype(BF16)
        in_mine = (_lane(acc.shape) < HEAD_DIM) == (kvh == 0)
        lse = m_ref[...] + jnp.log2(jnp.where(in_mine, other, acc))
        lse_ref[0] = lse[:tq]
        lse_ref[1] = lse[tq:]

    qspec = pl.BlockSpec((tq, LANES), lambda j, i: (i, j))
    whole = pl.BlockSpec((t, LANES), lambda j, i: (0, 0))
    nj, ni = B_HEADS // 2, t // tq
    steps = lambda: ((pl.program_id(0) == 0) & (pl.program_id(1) == 0), (pl.program_id(0) == nj - 1) & (pl.program_id(1) == ni - 1))
    body, x_in, x_out, x_shapes, x_scratch = carried(body, exchange, 4, 2, steps)
    return pl.pallas_call(
        body, name=name, grid=(nj, ni),
        in_specs=[qspec, whole, whole, _ANY] + x_in,
        out_specs=[pl.BlockSpec((tq, LANES), lambda j, i: (i, A_Q // LANES + j)),
                   pl.BlockSpec((2, tq, LANES), lambda j, i: (j, i, 0))] + x_out,
        out_shape=[jax.ShapeDtypeStruct(cat.shape, BF16), jax.ShapeDtypeStruct((B_HEADS, t, LANES), F32)] + x_shapes,
        scratch_shapes=[pltpu.VMEM((2 * tq, LANES), F32), pltpu.VMEM((2 * tq, LANES), F32)] + x_scratch,
        input_output_aliases={3: 0},
        compiler_params=_cparams(("arbitrary", "arbitrary")),
    )(q, k, v, cat, *(exchange.arrays if exchange else ()))


def flash_attn_bwd(q, k, v, do, lse, delta, name, exchange=None, tq=256, tk=512):
    t = q.shape[0]
    tq, tk = _fit(t, tq), _fit(t, tk)
    nk = t // tk
    grp = B_HEADS // B_KV_HEADS
    gw = grp * HEAD_DIM

    def body(q_ref, do_ref, k_ref, v_ref, lse_ref, dl_ref, dq_ref, dk_ref, dv_ref, dq_s):
        kvh, i = pl.program_id(0), pl.program_id(1)

        @pl.when(i == 0)
        def _():
            dk_ref[...] = jnp.zeros_like(dk_ref)
            dv_ref[...] = jnp.zeros_like(dv_ref)

        qparts, doparts = [], []
        for hh in range(grp):
            sl = slice((hh // 2) * LANES, (hh // 2 + 1) * LANES)
            qparts.append(_head_to_half(q_ref[:, sl].astype(F32) * (SCALE * LOG2E), hh % 2, kvh))
            doparts.append(_head_to_half(do_ref[:, sl].astype(F32), hh % 2, kvh))
        qs = jnp.concatenate(qparts, axis=0).astype(BF16)
        dos = jnp.concatenate(doparts, axis=0).astype(BF16)
        lse = jnp.concatenate([lse_ref[hh][:, :1] for hh in range(grp)], axis=0)
        dl = jnp.concatenate([dl_ref[hh][:, :1] for hh in range(grp)], axis=0)
        dq_s[...] = jnp.zeros_like(dq_s)

        def chunk(c, carry):
            rows = pl.ds(pl.multiple_of(c * tk, tk), tk)
            kb, vb = k_ref[rows, :], v_ref[rows, :]
            p = jnp.exp2(_dot_nt(qs, kb) - lse)
            dp = _dot_nt(dos, vb)
            dsc = (p * (dp - dl)).astype(BF16)
            dv_ref[0, rows, :] += _dot_tn(p.astype(BF16), dos)
            dk_ref[0, rows, :] += _dot_tn(dsc, qs) * LN2
            dq_s[...] += _dot_nn(dsc, kb)
            return carry

        lax.fori_loop(0, nk, chunk, 0)
        for c in range(grp // 2):
            dq_ref[:, c * LANES:(c + 1) * LANES] = (_halves_to_heads(
                dq_s[2 * c * tq:(2 * c + 1) * tq], dq_s[(2 * c + 1) * tq:(2 * c + 2) * tq], kvh) * SCALE).astype(dq_ref.dtype)

    qspec = pl.BlockSpec((tq, gw), lambda kvh, i: (i, kvh))
    dospec = pl.BlockSpec((tq, gw), lambda kvh, i: (i, A_Q // gw + kvh))
    whole = pl.BlockSpec((t, LANES), lambda kvh, i: (0, 0))
    stat = pl.BlockSpec((grp, tq, LANES), lambda kvh, i: (kvh, i, 0))
    dlstat = pl.BlockSpec((grp, tq, LANES), lambda kvh, i: (A_HEADS // grp + kvh, i, 0))
    slab = pl.BlockSpec((1, t, LANES), lambda kvh, i: (kvh, 0, 0))
    ni = t // tq
    steps = lambda: ((pl.program_id(0) == 0) & (pl.program_id(1) == 0),
                     (pl.program_id(0) == B_KV_HEADS - 1) & (pl.program_id(1) == ni - 1))
    body, x_in, x_out, x_shapes, x_scratch = carried(body, exchange, 6, 3, steps)
    return pl.pallas_call(
        body, name=name, grid=(B_KV_HEADS, ni),
        in_specs=[qspec, dospec, whole, whole, stat, dlstat] + x_in,
        out_specs=[qspec, slab, slab] + x_out,
        out_shape=[jax.ShapeDtypeStruct((t, B_Q), BF16), jax.ShapeDtypeStruct((B_KV_HEADS, t, LANES), F32),
                   jax.ShapeDtypeStruct((B_KV_HEADS, t, LANES), F32)] + x_shapes,
        scratch_shapes=[pltpu.VMEM((grp * tq, LANES), F32)] + x_scratch,
        compiler_params=_cparams(("arbitrary", "arbitrary")),
    )(q, do, k, v, lse, delta, *(exchange.arrays if exchange else ()))


_GELU_C = math.sqrt(2.0 / math.pi)
_GELU_A = 0.044715


def _gelu(x):
    return 0.5 * x * (1.0 + jnp.tanh(_GELU_C * (x + _GELU_A * x * x * x)))


def _gelu_grad(x):
    th = jnp.tanh(_GELU_C * (x + _GELU_A * x * x * x))
    return 0.5 * (1.0 + th) + 0.5 * x * (1.0 - th * th) * _GELU_C * (1.0 + 3.0 * _GELU_A * x * x)


def _layernorm_stats(vf):
    mu = jnp.mean(vf, axis=-1, keepdims=True)
    vc = vf - mu
    r = lax.rsqrt(jnp.mean(vc * vc, axis=-1, keepdims=True) + EPS)
    return vc * r, r


def sgu_mix_fwd(z, ln_g, ln_b, w_s, b_rows, name):
    t, w2 = z.shape
    w = w2 // 2
    dg = w // SGU_GROUPS

    def body(u_ref, v_ref, g_ref, b_ref, ws_ref, bb_ref, y_ref):
        vhat, _ = _layernorm_stats(v_ref[...].astype(F32))
        vn = (vhat * g_ref[...] + b_ref[...]).astype(BF16)
        for g in range(SGU_GROUPS):
            sl = slice(g * dg, (g + 1) * dg)
            mixed = _dot_nn(ws_ref[g], vn[:, sl]) + bb_ref[g]
            y_ref[:, sl] = (u_ref[:, sl].astype(F32) * mixed).astype(BF16)

    vec = pl.BlockSpec((1, w), lambda n: (0, 0))
    whole = pl.BlockSpec((SGU_GROUPS, SGU_CHUNK, SGU_CHUNK), lambda n: (0, 0, 0))
    return pl.pallas_call(
        body, name=name, grid=(t // SGU_CHUNK,),
        in_specs=[pl.BlockSpec((SGU_CHUNK, w), lambda n: (n, 0)), pl.BlockSpec((SGU_CHUNK, w), lambda n: (n, 1)),
                  vec, vec, whole, whole],
        out_specs=pl.BlockSpec((SGU_CHUNK, w), lambda n: (n, 0)),
        out_shape=jax.ShapeDtypeStruct((t, w), BF16),
        compiler_params=_cparams(("parallel",)),
    )(z, z, ln_g.reshape(1, w), ln_b.reshape(1, w), w_s, b_rows)


def sgu_mix_bwd(z, apre, dy, ln_g, ln_b, w_s, b_rows, name):
    t, w2 = z.shape
    w = w2 // 2
    dg = w // SGU_GROUPS

    def body(u_ref, v_ref, au_ref, av_ref, dy_ref, g_ref, b_ref, ws_ref, bb_ref, da_ref, dlg_ref, dlb_ref, dws_ref, dbs_ref):
        @pl.when(pl.program_id(0) == 0)
        def _():
            dlg_ref[...] = jnp.zeros_like(dlg_ref)
            dlb_ref[...] = jnp.zeros_like(dlb_ref)
            dws_ref[...] = jnp.zeros_like(dws_ref)
            dbs_ref[...] = jnp.zeros_like(dbs_ref)

        vhat, r = _layernorm_stats(v_ref[...].astype(F32))
        gam = g_ref[...]
        vn = (vhat * gam + b_ref[...]).astype(BF16)
        ones8 = jnp.ones((8, dg), BF16)
        rid = lax.broadcasted_iota(jnp.int32, (8, SGU_CHUNK), 0)
        dbs = jnp.zeros((8, SGU_CHUNK), F32)
        dvn_parts = []
        for g in range(SGU_GROUPS):
            sl = slice(g * dg, (g + 1) * dg)
            dyg = dy_ref[:, sl].astype(F32)
            mixed = _dot_nn(ws_ref[g], vn[:, sl]) + bb_ref[g]
            da_ref[:, sl] = (dyg * mixed * _gelu_grad(au_ref[:, sl].astype(F32))).astype(BF16)
            dmix = dyg * u_ref[:, sl].astype(F32)
            dm_hi = dmix.astype(BF16)
            dm_lo = (dmix - dm_hi.astype(F32)).astype(BF16)
            dws_ref[g] += _dot_nt(dm_hi, vn[:, sl])
            dbs = dbs + jnp.where(rid == g, _dot_nt(ones8, dm_hi) + _dot_nt(ones8, dm_lo), 0.0)
            dvn_parts.append(_dot_tn(ws_ref[g], dm_hi))
        dbs_ref[...] += dbs
        dvn = jnp.concatenate(dvn_parts, axis=1)
        dlg_ref[...] += jnp.sum(dvn * vhat, axis=0, keepdims=True)
        dlb_ref[...] += jnp.sum(dvn, axis=0, keepdims=True)
        dvh = dvn * gam
        dv = r * (dvh - jnp.mean(dvh, axis=-1, keepdims=True) - vhat * jnp.mean(dvh * vhat, axis=-1, keepdims=True))
        da_ref[:, w:] = (dv * _gelu_grad(av_ref[...].astype(F32))).astype(BF16)

    vec = pl.BlockSpec((1, w), lambda n: (0, 0))
    whole = pl.BlockSpec((SGU_GROUPS, SGU_CHUNK, SGU_CHUNK), lambda n: (0, 0, 0))
    left = pl.BlockSpec((SGU_CHUNK, w), lambda n: (n, 0))
    right = pl.BlockSpec((SGU_CHUNK, w), lambda n: (n, 1))
    return pl.pallas_call(
        body, name=name, grid=(t // SGU_CHUNK,),
        in_specs=[left, right, left, right, left, vec, vec, whole, whole],
        out_specs=[pl.BlockSpec((SGU_CHUNK, w2), lambda n: (n, 0)), vec, vec, whole,
                   pl.BlockSpec((SGU_GROUPS, SGU_CHUNK), lambda n: (0, 0))],
        out_shape=[jax.ShapeDtypeStruct((t, w2), BF16), jax.ShapeDtypeStruct((1, w), F32), jax.ShapeDtypeStruct((1, w), F32),
                   jax.ShapeDtypeStruct((SGU_GROUPS, SGU_CHUNK, SGU_CHUNK), F32),
                   jax.ShapeDtypeStruct((SGU_GROUPS, SGU_CHUNK), F32)],
        compiler_params=_cparams(("arbitrary",)),
    )(z, z, apre, apre, dy, ln_g.reshape(1, w), ln_b.reshape(1, w), w_s, b_rows)


def loss_head(h, g, target, name):
    t, d = h.shape
    tm = min(t, 512)

    def body(h_ref, g_ref, t_ref, loss_ref, dh_ref, dhb_ref, dg_ref):
        @pl.when(pl.program_id(0) == 0)
        def _():
            loss_ref[...] = jnp.zeros_like(loss_ref)
            dg_ref[...] = jnp.zeros_like(dg_ref)

        xf = h_ref[...]
        r = lax.rsqrt(jnp.mean(xf * xf, axis=-1, keepdims=True) + EPS)
        xhat = xf * r
        err = xhat * g_ref[...] - t_ref[...]
        per_tok = jnp.mean(err * err, axis=-1, keepdims=True)
        loss_ref[...] += 0.5 * jnp.sum(per_tok, axis=0, keepdims=True)
        dy = err * (1.0 / d)
        dg_ref[...] += jnp.sum(dy * xhat, axis=0, keepdims=True)
        dxh = dy * g_ref[...]
        dh = r * (dxh - xhat * jnp.mean(dxh * xhat, axis=-1, keepdims=True))
        dh_ref[...] = dh
        dhb_ref[...] = dh.astype(BF16)

    row = pl.BlockSpec((tm, d), lambda i: (i, 0))
    vec = pl.BlockSpec((1, d), lambda i: (0, 0))
    return pl.pallas_call(
        body, name=name, grid=(t // tm,),
        in_specs=[row, vec, row],
        out_specs=[pl.BlockSpec((1, LANES), lambda i: (0, 0)), row, row, vec],
        out_shape=[jax.ShapeDtypeStruct((1, LANES), F32), jax.ShapeDtypeStruct((t, d), F32), jax.ShapeDtypeStruct((t, d), BF16),
                   jax.ShapeDtypeStruct((1, d), F32)],
        compiler_params=_cparams(("arbitrary",)),
    )(h, g.reshape(1, d), target)


ADAMW_BLOCK_BYTES = 1 << 20


def adamw(parts, w, m, v, name):
    n_layers, r, c = w.shape
    row_bytes = n_layers * c * 4
    if r * row_bytes <= 2 * ADAMW_BLOCK_BYTES:
        tr = r
    else:
        tr = _fit(r, 1 << int(math.log2(max(8, ADAMW_BLOCK_BYTES // row_bytes))))
    bc1 = 1.0 - ADAM_B1 ** ADAM_STEP
    bc2 = 1.0 - ADAM_B2 ** ADAM_STEP

    def body(*refs):
        p_refs = refs[:n_layers]
        w_ref, m_ref, v_ref, g_ref, d_ref, nm_ref, nv_ref = refs[n_layers:]
        for l in range(n_layers):
            g = p_refs[l][0].astype(F32)
            for j in range(1, N_DEV):
                g = g + p_refs[l][j].astype(F32)
            nm = ADAM_B1 * m_ref[l] + (1.0 - ADAM_B1) * g
            nv = ADAM_B2 * v_ref[l] + (1.0 - ADAM_B2) * (g * g)
            g_ref[l] = g
            nm_ref[l] = nm
            nv_ref[l] = nv
            d_ref[l] = -ADAM_LR * ((nm / bc1) / (jnp.sqrt(nv / bc2) + ADAM_EPS) + ADAM_WD * w_ref[l])

    blk = pl.BlockSpec((n_layers, tr, c), lambda i: (0, i, 0))
    return pl.pallas_call(
        body, name=name, grid=(r // tr,),
        in_specs=[pl.BlockSpec((N_DEV, tr, c), lambda i: (0, i, 0))] * n_layers + [blk, blk, blk],
        out_specs=[blk] * 4,
        out_shape=[jax.ShapeDtypeStruct((n_layers, r, c), F32)] * 4,
        compiler_params=_cparams(("parallel",)),
    )(*parts, w, m, v)


_ANY = pl.BlockSpec(memory_space=pl.ANY)


def _mesh_pos():
    return lax.axis_index("x"), lax.axis_index("y"), lax.axis_index("c")


class Exchange:
    def __init__(self, gathers=(), scatters=()):
        self.items = [("gather", a) for a in gathers] + [("scatter", a) for a in scatters]
        self.arrays = [a for _, a in self.items]
        self.n = len(self.items)

    def out_shapes(self):
        return [jax.ShapeDtypeStruct(((N_DEV,) + a.shape) if kind == "gather" else a.shape, a.dtype) for kind, a in self.items]

    def scratch(self):
        return [pltpu.SemaphoreType.DMA((7 * self.n,)), pltpu.SemaphoreType.DMA((7 * self.n,)), pltpu.SemaphoreType.DMA((self.n,))]

    def _copies(self, in_refs, out_refs, send_sems, recv_sems, local_sems):
        x, y, c = _mesh_pos()
        me = 4 * x + 2 * y + c
        local, sends, arrivals = [], [], []
        for t, (kind, _) in enumerate(self.items):
            src_of = (lambda slot, r=in_refs[t]: r) if kind == "gather" else (lambda slot, r=in_refs[t]: r.at[slot])
            local.append(pltpu.make_async_copy(src_of(me), out_refs[t].at[me], local_sems.at[t]))
            for k in range(1, N_DEV):
                px = 1 - x if k & 4 else x
                py = 1 - y if k & 2 else y
                pc = 1 - c if k & 1 else c
                pid = 4 * px + 2 * py + pc
                kw = dict(send_sem=send_sems.at[7 * t + k - 1], recv_sem=recv_sems.at[7 * t + k - 1],
                          device_id=(px, py, pc), device_id_type=pl.DeviceIdType.MESH)
                sends.append(pltpu.make_async_remote_copy(src_ref=src_of(pid), dst_ref=out_refs[t].at[me], **kw))
                arrivals.append(pltpu.make_async_remote_copy(src_ref=src_of(pid), dst_ref=out_refs[t].at[pid], **kw))
        return local, sends, arrivals

    def start(self, *refs):
        local, sends, _ = self._copies(*refs)
        for cp in local + sends:
            cp.start()

    def wait(self, *refs):
        local, sends, arrivals = self._copies(*refs)
        for cp in arrivals:
            cp.wait_recv()
        for cp in sends:
            cp.wait_send()
        for cp in local:
            cp.wait()


def carried(body, exchange, n_in, n_out, first_last):
    if exchange is None:
        return body, [], [], [], []
    nx = exchange.n

    def wrapped(*refs):
        ins, xin = refs[:n_in], refs[n_in:n_in + nx]
        outs, xout = refs[n_in + nx:n_in + nx + n_out], refs[n_in + nx + n_out:n_in + 2 * nx + n_out]
        scratch, sems = refs[n_in + 2 * nx + n_out:-3], refs[-3:]
        first, last = first_last()

        @pl.when(first)
        def _():
            exchange.start(xin, xout, *sems)

        body(*ins, *outs, *scratch)

        @pl.when(last)
        def _():
            exchange.wait(xin, xout, *sems)

    return wrapped, [_ANY] * nx, [_ANY] * nx, exchange.out_shapes(), exchange.scratch()


def exchange_only(exchange, name):
    def body(*refs):
        xin, xout, sems = refs[:exchange.n], refs[exchange.n:2 * exchange.n], refs[-3:]
        exchange.start(xin, xout, *sems)
        exchange.wait(xin, xout, *sems)

    return pl.pallas_call(
        body, name=name, in_specs=[_ANY] * exchange.n, out_specs=[_ANY] * exchange.n,
        out_shape=exchange.out_shapes(), scratch_shapes=exchange.scratch(),
    )(*exchange.arrays)


def attention_fwd(x, norm_g, w_in, sink, qn_g, kn_g, w_out, tables, tag, exchange=None):
    h = rmsnorm_fwd(x, norm_g, f"{tag}_norm")
    (proj,) = matmul(h, w_in, "nn", f"{tag}_proj", [F32], tn=ATT_IN)
    qa, ka, va, qb, kb, vb = qkv_post_fwd(proj, tables, qn_g, kn_g, f"{tag}_qkv")
    cat, lse_a = window_attn_fwd(qa, ka, va, sink, f"{tag}_win")
    cat, lse_b, *arrived = flash_attn_fwd(qb, kb, vb, cat, f"{tag}_flash", exchange)
    (y,) = matmul(cat, w_out, "nn", f"{tag}_out", [F32], epilogue=lambda acc, r: (r + acc,), extras=(x,))
    saved = (x, h, proj, qa, ka, va, qb, kb, vb, cat, lse_a, lse_b)
    return y, saved, arrived


def attention_bwd(dy, dyb, saved, norm_g, w_in, sink, qn_g, kn_g, w_out, tables, tag, exchange_with=None):
    x, h, proj, qa, ka, va, qb, kb, vb, cat, lse_a, lse_b = saved
    (dcat,) = matmul(dyb, w_out, "nt", f"{tag}_dcat", [BF16])
    (dw_out,) = matmul(cat, dyb, "tn", f"{tag}_dwout", [BF16], tk=2048)
    delta = attn_delta(cat, dcat, f"{tag}_delta")
    dqa, dka, dva, dsink = window_attn_bwd(qa, ka, va, sink, dcat, lse_a, delta, f"{tag}_dwin")
    exchange = exchange_with(dw_out) if exchange_with else None
    dqb, dkb, dvb, *arrived = flash_attn_bwd(qb, kb, vb, dcat, lse_b, delta, f"{tag}_dflash", exchange)
    dproj, dqg, dkg = qkv_post_bwd(proj, tables, qn_g, kn_g, dqa, dka, dva, dqb, dkb, dvb, f"{tag}_dqkv")
    (dw_in,) = matmul(h, dproj, "tn", f"{tag}_dwin_w", [BF16], tn=ATT_IN // 2, tk=2048)
    dx, dxb, dg = matmul_nt_normbwd(dproj, w_in, x, norm_g, dy, f"{tag}_dx", tk=ATT_IN // 2)
    grp = A_HEADS // A_KV_HEADS
    small = dict(norm=dg[0], sink=dsink[:, :grp, 0].reshape(A_HEADS), qnorm=dqg[0, :HEAD_DIM], knorm=dkg[0, :HEAD_DIM])
    return dx, dxb, dw_in, dw_out, small, arrived


def sgu_fwd(x, norm_g, w_in, ln_g, ln_b, w_s, b_rows, w_out, tag):
    h = rmsnorm_fwd(x, norm_g, f"{tag}_norm")
    apre, z = matmul(h, w_in, "nn", f"{tag}_in", [BF16, BF16], epilogue=lambda acc: (acc, _gelu(acc)))
    y = sgu_mix_fwd(z, ln_g, ln_b, w_s, b_rows, f"{tag}_mix")
    (out,) = matmul(y, w_out, "nn", f"{tag}_out", [F32], epilogue=lambda acc, r: (r + acc,), extras=(x,))
    return out, (x, h, apre, z, y)


def sgu_bwd(dout, doutb, saved, norm_g, w_in, ln_g, ln_b, w_s, b_rows, w_out, tag):
    x, h, apre, z, y = saved
    (dy,) = matmul(doutb, w_out, "nt", f"{tag}_dy", [BF16])
    (dw_out,) = matmul(y, doutb, "tn", f"{tag}_dwout", [BF16], tk=2048)
    dapre, dlg, dlb, dws, dbs = sgu_mix_bwd(z, apre, dy, ln_g, ln_b, w_s, b_rows, f"{tag}_dmix")
    (dw_in,) = matmul(h, dapre, "tn", f"{tag}_dwin", [BF16], out_shards=True, tk=2048)
    dx, dxb, dg = matmul_nt_normbwd(dapre, w_in, x, norm_g, dout, f"{tag}_dx")
    small = dict(norm=dg[0], ln_g=dlg[0], ln_b=dlb[0], w_s=dws, b_s=dbs)
    return dx, dxb, dw_in, dw_out, small


def _square(r):
    return r * r


def mlp_fwd(x, norm_g, w1, w2, tag):
    h = rmsnorm_fwd(x, norm_g, f"{tag}_norm")
    (r,) = matmul(h, w1, "nn", f"{tag}_up", [BF16], epilogue=lambda acc: (jnp.maximum(acc, 0.0),))
    (y,) = matmul(r, w2, "nn", f"{tag}_down", [F32], epilogue=lambda acc, res: (res + acc,), extras=(x,), a_fn=_square, tk=2048)
    return y, (x, h, r)


def mlp_bwd(dy, dyb, saved, norm_g, w1, w2, tag):
    x, h, r = saved
    (da,) = matmul(dyb, w2, "nt", f"{tag}_da", [BF16], epilogue=lambda acc, rr: (acc * (2.0 * rr.astype(F32)),), extras=(r,))
    (dw2,) = matmul(r, dyb, "tn", f"{tag}_dw2", [BF16], a_fn=_square, tk=2048)
    (dw1,) = matmul(h, da, "tn", f"{tag}_dw1", [BF16], out_shards=True, tk=2048)
    dx, dxb, dg = matmul_nt_normbwd(da, w1, x, norm_g, dy, f"{tag}_dx")
    return dx, dxb, dw1, dw2, dg[0]


ORDER = ("att_norm", "att_w_in", "att_sink", "att_qnorm", "att_knorm", "att_w_out", "sgu_norm", "sgu_w_in", "sgu_ln_g",
         "sgu_ln_b", "sgu_w_s", "sgu_b_s", "sgu_w_out", "mlp_norm", "mlp_w1", "mlp_w2", "final_norm")
SHARDED = ("att_w_in", "att_w_out", "sgu_w_in", "sgu_w_out", "mlp_w1", "mlp_w2")
SGU_VECS = ("sgu_norm", "sgu_ln_g", "sgu_ln_b")
SMALL_EARLY = ("sgu_w_s", "sgu_b_s", "mlp_norm", "final_norm", "loss")
SMALL_LATE = ("att_norm", "att_sink", "att_qnorm", "att_knorm")
SMALL_ROWS_MULT = 8


def _flat(blocks, names):
    flat = jnp.concatenate([blocks[n].reshape(-1).astype(F32) for n in names])
    per = SMALL_ROWS_MULT * FLAT_COLS
    total = -(-flat.shape[0] // per) * per
    return jnp.pad(flat, (0, total - flat.shape[0])).reshape(1, total // FLAT_COLS, FLAT_COLS)


def _unflat(flat, like, names):
    out, off = {}, 0
    f = flat.reshape(-1)
    for n in names:
        size = like[n].size
        out[n] = f[off:off + size].reshape(like[n].shape)
        off += size
    return out


def kernel(x, att_norm, att_w_in, att_sink, att_qnorm, att_knorm, att_w_out, sgu_norm, sgu_w_in, sgu_ln_g, sgu_ln_b, sgu_w_s, sgu_b_s, sgu_w_out, mlp_norm, mlp_w1, mlp_w2, final_norm, loss_target, m_att_norm, m_att_w_in, m_att_sink, m_att_qnorm, m_att_knorm, m_att_w_out, m_sgu_norm, m_sgu_w_in, m_sgu_ln_g, m_sgu_ln_b, m_sgu_w_s, m_sgu_b_s, m_sgu_w_out, m_mlp_norm, m_mlp_w1, m_mlp_w2, m_final_norm, v_att_norm, v_att_w_in, v_att_sink, v_att_qnorm, v_att_knorm, v_att_w_out, v_sgu_norm, v_sgu_w_in, v_sgu_ln_g, v_sgu_ln_b, v_sgu_w_s, v_sgu_b_s, v_sgu_w_out, v_mlp_norm, v_mlp_w1, v_mlp_w2, v_final_norm):
    w = dict(att_norm=att_norm, att_w_in=att_w_in, att_sink=att_sink, att_qnorm=att_qnorm, att_knorm=att_knorm,
             att_w_out=att_w_out, sgu_norm=sgu_norm, sgu_w_in=sgu_w_in, sgu_ln_g=sgu_ln_g, sgu_ln_b=sgu_ln_b, sgu_w_s=sgu_w_s,
             sgu_b_s=sgu_b_s, sgu_w_out=sgu_w_out, mlp_norm=mlp_norm, mlp_w1=mlp_w1, mlp_w2=mlp_w2, final_norm=final_norm)
    m = dict(att_norm=m_att_norm, att_w_in=m_att_w_in, att_sink=m_att_sink, att_qnorm=m_att_qnorm, att_knorm=m_att_knorm,
             att_w_out=m_att_w_out, sgu_norm=m_sgu_norm, sgu_w_in=m_sgu_w_in, sgu_ln_g=m_sgu_ln_g, sgu_ln_b=m_sgu_ln_b,
             sgu_w_s=m_sgu_w_s, sgu_b_s=m_sgu_b_s, sgu_w_out=m_sgu_w_out, mlp_norm=m_mlp_norm, mlp_w1=m_mlp_w1, mlp_w2=m_mlp_w2,
             final_norm=m_final_norm)
    v = dict(att_norm=v_att_norm, att_w_in=v_att_w_in, att_sink=v_att_sink, att_qnorm=v_att_qnorm, att_knorm=v_att_knorm,
             att_w_out=v_att_w_out, sgu_norm=v_sgu_norm, sgu_w_in=v_sgu_w_in, sgu_ln_g=v_sgu_ln_g, sgu_ln_b=v_sgu_ln_b,
             sgu_w_s=v_sgu_w_s, sgu_b_s=v_sgu_b_s, sgu_w_out=v_sgu_w_out, mlp_norm=v_mlp_norm, mlp_w1=v_mlp_w1, mlp_w2=v_mlp_w2,
             final_norm=v_final_norm)
    loss, grad_x, g, d, nm, nv = train_step(x[0], loss_target[0], w, m, v)
    return (loss, grad_x[None], *[g[n] for n in ORDER], *[d[n] for n in ORDER], *[nm[n] for n in ORDER], *[nv[n] for n in ORDER])


def train_step(x, target, w, m, v):
    t, d_model = x.shape
    n_att, n_sgu, depth = w["att_w_in"].shape[0], w["sgu_w_in"].shape[0], w["mlp_w1"].shape[0]
    bf = lambda n: w[n].astype(BF16)

    vec_local = jnp.stack([w[n] for n in SGU_VECS], axis=1)
    g_att_in, g_att_out, g_vec = exchange_only(Exchange(gathers=[bf("att_w_in"), bf("att_w_out"), vec_local]), "gather_attention")
    att_w_in = g_att_in.transpose(1, 2, 0, 3).reshape(n_att, d_model, ATT_IN)
    vecs = g_vec.transpose(1, 2, 0, 3).reshape(n_sgu, len(SGU_VECS), -1)
    rest = Exchange(gathers=[bf("sgu_w_in"), bf("sgu_w_out"), bf("mlp_w1"), bf("mlp_w2")])
    w_s_bf = w["sgu_w_s"].astype(BF16)
    b_rows = jnp.broadcast_to(w["sgu_b_s"][:, :, :, None], w["sgu_b_s"].shape + (LANES,))
    tables = _rope_tables(t)

    saved, h = [], x
    for layer in range(depth):
        i = layer // 2
        if layer % 2 == 0:
            h, sv, arrived = attention_fwd(h, w["att_norm"][i], att_w_in[i], w["att_sink"][i], w["att_qnorm"][i], w["att_knorm"][i],
                                           Gathered(g_att_out, "row", i), tables, f"att{i}", rest if layer == 0 else None)
            if layer == 0:
                g_sgu_in, g_sgu_out, g_w1, g_w2 = arrived
        else:
            h, sv = sgu_fwd(h, vecs[i, 0], Gathered(g_sgu_in, "col", i), vecs[i, 1], vecs[i, 2], w_s_bf[i], b_rows[i],
                            Gathered(g_sgu_out, "row", i), f"sgu{i}")
        h, sm = mlp_fwd(h, w["mlp_norm"][layer], Gathered(g_w1, "col", layer), Gathered(g_w2, "row", layer), f"mlp{layer}")
        saved.append((sv, sm))
    loss_row, dh, dhb, dgf = loss_head(h, w["final_norm"], target, "loss_head")

    queue, recv = [], {}
    gs = dict(att_norm=[None] * n_att, att_sink=[None] * n_att, att_qnorm=[None] * n_att, att_knorm=[None] * n_att,
              sgu_w_s=[None] * n_sgu, sgu_b_s=[None] * n_sgu, mlp_norm=[None] * depth)

    def row_slabs(g):
        return g.reshape(N_DEV, g.shape[0] // N_DEV, g.shape[1])

    def col_slabs(g):
        return g.reshape(g.shape[0], N_DEV, g.shape[1] // N_DEV).transpose(1, 0, 2)

    def take_queue(gathers=()):
        items = list(queue)
        queue.clear()
        keys = [k for k, _ in gathers] + [k for k, _ in items]
        return Exchange(gathers=[a for _, a in gathers], scatters=[a for _, a in items]), keys

    def small_early():
        blocks = dict(sgu_w_s=jnp.stack(gs["sgu_w_s"]), sgu_b_s=jnp.stack(gs["sgu_b_s"]), mlp_norm=jnp.stack(gs["mlp_norm"]),
                      final_norm=dgf[0], loss=loss_row[0, :1])
        return _flat(blocks, SMALL_EARLY)[0]

    for layer in reversed(range(depth)):
        i = layer // 2
        sv, sm = saved[layer]
        dh, dhb, dw1, dw2, gs["mlp_norm"][layer] = mlp_bwd(
            dh, dhb, sm, w["mlp_norm"][layer], Gathered(g_w1, "col", layer), Gathered(g_w2, "row", layer), f"mlp{layer}")
        queue += [(("mlp_w1", layer), dw1), (("mlp_w2", layer), row_slabs(dw2))]
        if layer % 2 == 0:
            keys = []

            def exchange_with(dw_out, i=i, layer=layer, keys=keys):
                if layer == 0:
                    queue.append((("att_w_out", i), row_slabs(dw_out)))
                ex, got = take_queue([("small_early", small_early())] if layer == 0 else ())
                keys += got
                return ex

            dh, dhb, dw_in, dw_out, sm_g, arrived = attention_bwd(
                dh, dhb, sv, w["att_norm"][i], att_w_in[i], w["att_sink"][i], w["att_qnorm"][i], w["att_knorm"][i],
                Gathered(g_att_out, "row", i), tables, f"att{i}", exchange_with)
            recv.update(zip(keys, arrived))
            queue.append((("att_w_in", i), col_slabs(dw_in)))
            if layer != 0:
                queue.append((("att_w_out", i), row_slabs(dw_out)))
            gs["att_norm"][i], gs["att_sink"][i] = sm_g["norm"], sm_g["sink"]
            gs["att_qnorm"][i], gs["att_knorm"][i] = sm_g["qnorm"], sm_g["knorm"]
        else:
            dh, dhb, dw_in, dw_out, sm_g = sgu_bwd(
                dh, dhb, sv, vecs[i, 0], Gathered(g_sgu_in, "col", i), vecs[i, 1], vecs[i, 2], w_s_bf[i], b_rows[i],
                Gathered(g_sgu_out, "row", i), f"sgu{i}")
            dvec = jnp.stack([sm_g["norm"], sm_g["ln_g"], sm_g["ln_b"]])
            queue += [(("sgu_w_in", i), dw_in), (("sgu_w_out", i), row_slabs(dw_out)), (("sgu_vecs", i), col_slabs(dvec))]
            gs["sgu_w_s"][i], gs["sgu_b_s"][i] = sm_g["w_s"], sm_g["b_s"]
    grad_x = dh
    late = dict(att_norm=jnp.stack(gs["att_norm"]), att_sink=jnp.stack(gs["att_sink"]), att_qnorm=jnp.stack(gs["att_qnorm"]),
                att_knorm=jnp.stack(gs["att_knorm"]))
    last, keys = take_queue([("small_late", _flat(late, SMALL_LATE)[0])])
    recv.update(zip(keys, exchange_only(last, "exchange_last")))

    outs = [{}, {}, {}, {}]
    for n in SHARDED:
        res = adamw([recv[(n, l)] for l in range(w[n].shape[0])], w[n], m[n], v[n], f"adamw_{n}")
        for o, r in zip(outs, res):
            o[n] = r
    stack_vecs = lambda src: jnp.stack([src[n] for n in SGU_VECS], axis=1)
    res = adamw([recv[("sgu_vecs", i)] for i in range(n_sgu)], stack_vecs(w), stack_vecs(m), stack_vecs(v), "adamw_sgu_vecs")
    for o, r in zip(outs, res):
        o.update({n: r[:, k] for k, n in enumerate(SGU_VECS)})
    zero = {"loss": jnp.zeros((1,), F32)}
    for names, key in ((SMALL_EARLY, "small_early"), (SMALL_LATE, "small_late")):
        res = adamw([recv[key]], _flat({**w, **zero}, names), _flat({**m, **zero}, names), _flat({**v, **zero}, names), f"adamw_{key}")
        for o, r in zip(outs, res):
            o.update(_unflat(r, {**w, **zero}, names))
    loss = outs[0]["loss"][0]
    return loss, grad_x, *outs
```

```python
import functools
import math

import jax
import jax.numpy as jnp
from jax import lax
from jax.experimental import pallas as pl
from jax.experimental.pallas import tpu as pltpu

F32 = jnp.float32
BF16 = jnp.bfloat16

HEAD_DIM = 64
A_HEADS = 8
A_KV_HEADS = 2
B_HEADS = 8
B_KV_HEADS = 2
WINDOW = 128
BLOCK = 128
ROPE_THETA = 10000.0
GRID_W = 64
SGU_GROUPS = 8
SGU_CHUNK = 128
EPS = 1e-6
SCALE = HEAD_DIM ** -0.5
NEG = -1e30
LOG2E = math.log2(math.e)
LN2 = math.log(2.0)

A_Q = A_HEADS * HEAD_DIM
A_KV = A_KV_HEADS * HEAD_DIM
B_Q = B_HEADS * HEAD_DIM
B_KV = B_KV_HEADS * HEAD_DIM
OFF_QA, OFF_KA, OFF_VA = 0, A_Q, A_Q + A_KV
OFF_QB = A_Q + 2 * A_KV
OFF_KB = OFF_QB + B_Q
OFF_VB = OFF_KB + B_KV
ATT_IN = OFF_VB + B_KV

ADAM_LR = 0.001
ADAM_B1 = 0.9
ADAM_B2 = 0.999
ADAM_EPS = 1e-08
ADAM_WD = 0.01
ADAM_STEP = 10

N_DEV = 8
LANES = 128
V7X_VMEM_LIMIT = 56 * 1024 * 1024
FLAT_COLS = 1024


def _cparams(sem, vmem=V7X_VMEM_LIMIT):
    return pltpu.CompilerParams(dimension_semantics=sem, vmem_limit_bytes=vmem)


def _dot_nn(a, b):
    return lax.dot_general(a, b, (((1,), (0,)), ((), ())), preferred_element_type=F32)


def _dot_nt(a, b):
    return lax.dot_general(a, b, (((1,), (1,)), ((), ())), preferred_element_type=F32)


def _dot_tn(a, b):
    return lax.dot_general(a, b, (((0,), (0,)), ((), ())), preferred_element_type=F32)


def _bf(x):
    return x if x.dtype == BF16 else x.astype(BF16)


def _lane(shape):
    return lax.broadcasted_iota(jnp.int32, shape, len(shape) - 1)


def _seg_matrix(rows_lo, rows_hi):
    r = lax.broadcasted_iota(jnp.int32, (LANES, LANES), 0)
    return jnp.where((r >= rows_lo) & (r < rows_hi), 1.0, 0.0).astype(BF16)


def _group_matrix(width):
    r = lax.broadcasted_iota(jnp.int32, (LANES, LANES), 0)
    c = lax.broadcasted_iota(jnp.int32, (LANES, LANES), 1)
    return jnp.where((r // width) == (c // width), 1.0, 0.0).astype(BF16)


def _dot_f32_by_ones(s, ones_bf16):
    hi = s.astype(BF16)
    lo = (s - hi.astype(F32)).astype(BF16)
    return _dot_nn(hi, ones_bf16) + _dot_nn(lo, ones_bf16)


def _swap_halves(x, width):
    half = width // 2
    first = (_lane(x.shape) % width) < half
    return jnp.where(first, pltpu.roll(x, LANES - half, 1), pltpu.roll(x, half, 1))


def rmsnorm_fwd(x, g, name):
    t, d = x.shape
    tm = min(t, 512)

    def body(x_ref, g_ref, h_ref):
        xf = x_ref[...]
        r = lax.rsqrt(jnp.mean(xf * xf, axis=-1, keepdims=True) + EPS)
        h_ref[...] = (xf * r * g_ref[...]).astype(BF16)

    return pl.pallas_call(
        body, name=name, grid=(t // tm,),
        in_specs=[pl.BlockSpec((tm, d), lambda i: (i, 0)), pl.BlockSpec((1, d), lambda i: (0, 0))],
        out_specs=pl.BlockSpec((tm, d), lambda i: (i, 0)),
        out_shape=jax.ShapeDtypeStruct((t, d), BF16),
        compiler_params=_cparams(("parallel",)),
    )(x, g.reshape(1, d))


def _fit(n, want):
    t = min(n, want)
    while n % t:
        t //= 2
    return t


class Gathered:
    def __init__(self, arr, kind, layer):
        self.arr, self.kind, self.layer = arr, kind, layer
        _, _, self.rows, self.cols = arr.shape
        self.shape = (N_DEV * self.rows, self.cols) if kind == "row" else (self.rows, N_DEV * self.cols)


def _b_operand(b, mode, tn, tk, idx):
    dot = {"nn": _dot_nn, "nt": _dot_nt, "tn": _dot_tn}[mode]
    if not isinstance(b, Gathered):
        if mode == "nt":
            spec = pl.BlockSpec((tn, tk), lambda *g: idx(*g))
        else:
            spec = pl.BlockSpec((tk, tn), lambda *g: idx(*g)[::-1])
        return b, spec, lambda av, ref: dot(av, _bf(ref[...]))
    lay, rows, cols = b.layer, b.rows, b.cols
    if mode == "nn" and b.kind == "col":
        s = tn // cols
        assert s * cols == tn
        spec = pl.BlockSpec((s, None, tk, cols), lambda *g: (idx(*g)[0], lay, idx(*g)[1], 0))
        return b.arr, spec, lambda av, ref: jnp.concatenate([_dot_nn(av, ref[c]) for c in range(s)], axis=1)
    if mode == "nn" and b.kind == "row":
        s = tk // rows
        assert s * rows == tk
        spec = pl.BlockSpec((s, None, rows, tn), lambda *g: (idx(*g)[1], lay, 0, idx(*g)[0]))
        return b.arr, spec, lambda av, ref: _dot_nn(av, ref[...].reshape(s * rows, tn))
    if mode == "nt" and b.kind == "row":
        s = tn // rows
        assert s * rows == tn
        spec = pl.BlockSpec((s, None, rows, tk), lambda *g: (idx(*g)[0], lay, 0, idx(*g)[1]))
        return b.arr, spec, lambda av, ref: _dot_nt(av, ref[...].reshape(s * rows, tk))
    if mode == "nt" and b.kind == "col":
        s = tk // cols
        assert s * cols == tk
        spec = pl.BlockSpec((s, None, tn, cols), lambda *g: (idx(*g)[1], lay, idx(*g)[0], 0))

        def prod(av, ref):
            tot = _dot_nt(av[:, :cols], ref[0])
            for c in range(1, s):
                tot = tot + _dot_nt(av[:, c * cols:(c + 1) * cols], ref[c])
            return tot

        return b.arr, spec, prod
    raise NotImplementedError((mode, b.kind))


def matmul(a, b, mode, name, out_dtypes, epilogue=None, extras=(), a_fn=None, out_shards=False, tm=1024, tn=1024, tk=1024):
    (m, k) = a.shape[::-1] if mode == "tn" else a.shape
    n = b.shape[0] if mode == "nt" else b.shape[1]
    if out_shards:
        tn = n // N_DEV
    tm, tn, tk = _fit(m, tm), _fit(n, tn), _fit(k, tk)
    nk = k // tk
    n_ex, n_out = len(extras), len(out_dtypes)
    if epilogue is None:
        epilogue = lambda acc: (acc,)
    b_arr, b_spec, prod = _b_operand(b, mode, tn, tk, lambda i, j, kk: (j, kk))

    def body(*refs):
        a_ref, b_ref = refs[0], refs[1]
        ex_refs = refs[2:2 + n_ex]
        out_refs = refs[2 + n_ex:2 + n_ex + n_out]
        acc_ref = refs[2 + n_ex + n_out] if nk > 1 else None
        kk = pl.program_id(2)
        av = _bf(a_ref[...])
        if a_fn is not None:
            av = a_fn(av)
        part = prod(av, b_ref)

        def finish(acc):
            outs = epilogue(acc, *[r[...] for r in ex_refs])
            for r, o in zip(out_refs, outs):
                r[...] = o.astype(r.dtype)

        if nk == 1:
            finish(part)
            return

        @pl.when(kk == 0)
        def _():
            acc_ref[...] = part

        @pl.when(kk > 0)
        def _():
            acc_ref[...] += part

        @pl.when(kk == nk - 1)
        def _():
            finish(acc_ref[...])

    if mode == "tn":
        a_spec = pl.BlockSpec((tk, tm), lambda i, j, kk: (kk, i))
    else:
        a_spec = pl.BlockSpec((tm, tk), lambda i, j, kk: (i, kk))
    mn_spec = pl.BlockSpec((tm, tn), lambda i, j, kk: (i, j))
    if out_shards:
        out_spec = pl.BlockSpec((None, tm, tn), lambda i, j, kk: (j, i, 0))
        out_shape = [jax.ShapeDtypeStruct((N_DEV, m, tn), dt) for dt in out_dtypes]
    else:
        out_spec = mn_spec
        out_shape = [jax.ShapeDtypeStruct((m, n), dt) for dt in out_dtypes]
    outs = pl.pallas_call(
        body, name=name, grid=(m // tm, n // tn, nk),
        in_specs=[a_spec, b_spec] + [mn_spec] * n_ex,
        out_specs=[out_spec] * n_out,
        out_shape=out_shape,
        scratch_shapes=[pltpu.VMEM((tm, tn), F32)] if nk > 1 else [],
        compiler_params=_cparams(("parallel", "parallel", "arbitrary")),
    )(a, b_arr, *extras)
    return outs


def matmul_nt_normbwd(dz, w, x, g, dres, name, tm=512):
    m, k = dz.shape
    d = w.shape[0]
    tm = _fit(m, tm)
    w_arr, w_spec, prod = _b_operand(w, "nt", d, k, lambda i: (0, 0))

    def body(dz_ref, w_ref, x_ref, g_ref, dres_ref, dx_ref, dxb_ref, dg_ref):
        @pl.when(pl.program_id(0) == 0)
        def _():
            dg_ref[...] = jnp.zeros_like(dg_ref)

        dh = prod(_bf(dz_ref[...]), w_ref)
        xf = x_ref[...]
        r = lax.rsqrt(jnp.mean(xf * xf, axis=-1, keepdims=True) + EPS)
        xhat = xf * r
        dg_ref[...] += jnp.sum(dh * xhat, axis=0, keepdims=True)
        dxh = dh * g_ref[...]
        dx = r * (dxh - xhat * jnp.mean(dxh * xhat, axis=-1, keepdims=True))
        out = dres_ref[...] + dx
        dx_ref[...] = out
        dxb_ref[...] = out.astype(BF16)

    row = pl.BlockSpec((tm, d), lambda i: (i, 0))
    vec = pl.BlockSpec((1, d), lambda i: (0, 0))
    return pl.pallas_call(
        body, name=name, grid=(m // tm,),
        in_specs=[pl.BlockSpec((tm, k), lambda i: (i, 0)), w_spec, row, vec, row],
        out_specs=[row, row, vec],
        out_shape=[jax.ShapeDtypeStruct((m, d), F32), jax.ShapeDtypeStruct((m, d), BF16), jax.ShapeDtypeStruct((1, d), F32)],
        compiler_params=_cparams(("arbitrary",)),
    )(dz, w_arr, x, g.reshape(1, d), dres)


def _rope_tables(t):
    pos = jnp.arange(t)

    def angles(p, dim):
        freqs = ROPE_THETA ** (-jnp.arange(0, dim, 2, dtype=F32) / dim)
        return p.astype(F32)[:, None] * freqs[None, :]

    a1 = angles(pos, HEAD_DIM)
    cos_a = jnp.concatenate([jnp.cos(a1), jnp.cos(a1)], axis=-1)
    sin_a = jnp.concatenate([-jnp.sin(a1), jnp.sin(a1)], axis=-1)
    ar = angles(pos // GRID_W, HEAD_DIM // 2)
    ac = angles(pos % GRID_W, HEAD_DIM // 2)
    cos_b = jnp.concatenate([jnp.cos(ar), jnp.cos(ar), jnp.cos(ac), jnp.cos(ac)], axis=-1)
    sin_b = jnp.concatenate([-jnp.sin(ar), jnp.sin(ar), -jnp.sin(ac), jnp.sin(ac)], axis=-1)
    two = lambda z: jnp.concatenate([z, z], axis=-1)
    return two(cos_a), two(sin_a), two(cos_b), two(sin_b)


def _headnorm(xs, gmat):
    return lax.rsqrt(_dot_f32_by_ones(xs * xs, gmat) * (1.0 / HEAD_DIM) + EPS)


def qkv_post_fwd(proj, tables, qn_g, kn_g, name):
    t = proj.shape[0]
    tm = min(t, 256)
    cos_a, sin_a, cos_b, sin_b = tables
    g2 = lambda g: jnp.concatenate([g, g]).reshape(1, LANES)

    def body(p_ref, ca_ref, sa_ref, cb_ref, sb_ref, qg_ref, kg_ref, qa_ref, ka_ref, va_ref, qb_ref, kb_ref, vb_ref):
        ca, sa, cb, sb = ca_ref[...], sa_ref[...], cb_ref[...], sb_ref[...]
        gmat = _group_matrix(HEAD_DIM)

        def rope_a(xs):
            return xs * ca + _swap_halves(xs, HEAD_DIM) * sa

        def norm_rope_b(xs, g):
            y = xs * _headnorm(xs, gmat) * g
            return y * cb + _swap_halves(y, HEAD_DIM // 2) * sb

        for c in range(A_Q // LANES):
            qa_ref[:, c * LANES:(c + 1) * LANES] = rope_a(p_ref[:, OFF_QA + c * LANES:OFF_QA + (c + 1) * LANES]).astype(BF16)
        ka_ref[...] = rope_a(p_ref[:, OFF_KA:OFF_KA + LANES]).astype(BF16)
        va_ref[...] = p_ref[:, OFF_VA:OFF_VA + LANES].astype(BF16)
        for c in range(B_Q // LANES):
            qb_ref[:, c * LANES:(c + 1) * LANES] = norm_rope_b(
                p_ref[:, OFF_QB + c * LANES:OFF_QB + (c + 1) * LANES], qg_ref[...]).astype(BF16)
        kb_ref[...] = norm_rope_b(p_ref[:, OFF_KB:OFF_KB + LANES], kg_ref[...]).astype(BF16)
        vb_ref[...] = p_ref[:, OFF_VB:OFF_VB + LANES].astype(BF16)

    tab = pl.BlockSpec((tm, LANES), lambda i: (i, 0))
    vec = pl.BlockSpec((1, LANES), lambda i: (0, 0))
    wide = pl.BlockSpec((tm, A_Q), lambda i: (i, 0))
    return pl.pallas_call(
        body, name=name, grid=(t // tm,),
        in_specs=[pl.BlockSpec((tm, ATT_IN), lambda i: (i, 0)), tab, tab, tab, tab, vec, vec],
        out_specs=[wide, tab, tab, wide, tab, tab],
        out_shape=[jax.ShapeDtypeStruct((t, A_Q), BF16), jax.ShapeDtypeStruct((t, LANES), BF16),
                   jax.ShapeDtypeStruct((t, LANES), BF16), jax.ShapeDtypeStruct((t, B_Q), BF16),
                   jax.ShapeDtypeStruct((t, LANES), BF16), jax.ShapeDtypeStruct((t, LANES), BF16)],
        compiler_params=_cparams(("parallel",)),
    )(proj, cos_a, sin_a, cos_b, sin_b, g2(qn_g), g2(kn_g))


def qkv_post_bwd(proj, tables, qn_g, kn_g, dqa, dka, dva, dqb, dkb, dvb, name):
    t = proj.shape[0]
    tm = min(t, 256)
    cos_a, sin_a, cos_b, sin_b = tables
    g2 = lambda g: jnp.concatenate([g, g]).reshape(1, LANES)

    def body(p_ref, ca_ref, sa_ref, cb_ref, sb_ref, qg_ref, kg_ref, dqa_ref, dka_ref, dva_ref, dqb_ref, dkb_ref, dvb_ref,
             dp_ref, dqg_ref, dkg_ref):
        ca, sa, cb, sb = ca_ref[...], sa_ref[...], cb_ref[...], sb_ref[...]
        gmat = _group_matrix(HEAD_DIM)

        @pl.when(pl.program_id(0) == 0)
        def _():
            dqg_ref[...] = jnp.zeros_like(dqg_ref)
            dkg_ref[...] = jnp.zeros_like(dkg_ref)

        def rope_a_bwd(dy):
            return dy * ca + _swap_halves(dy * sa, HEAD_DIM)

        def norm_rope_b_bwd(dout, xs, g):
            dy = dout * cb + _swap_halves(dout * sb, HEAD_DIM // 2)
            r = _headnorm(xs, gmat)
            xhat = xs * r
            dxh = dy * g
            mean = _dot_f32_by_ones(dxh * xhat, gmat) * (1.0 / HEAD_DIM)
            return r * (dxh - xhat * mean), jnp.sum(dy * xhat, axis=0, keepdims=True)

        for c in range(A_Q // LANES):
            sl = slice(c * LANES, (c + 1) * LANES)
            dp_ref[:, OFF_QA + c * LANES:OFF_QA + (c + 1) * LANES] = rope_a_bwd(dqa_ref[:, sl].astype(F32)).astype(BF16)
        dp_ref[:, OFF_KA:OFF_KA + LANES] = rope_a_bwd(dka_ref[0] + dka_ref[1]).astype(BF16)
        dp_ref[:, OFF_VA:OFF_VA + LANES] = (dva_ref[0] + dva_ref[1]).astype(BF16)
        dqg = jnp.zeros((1, LANES), F32)
        for c in range(B_Q // LANES):
            sl = slice(c * LANES, (c + 1) * LANES)
            dx, dg = norm_rope_b_bwd(dqb_ref[:, sl].astype(F32), p_ref[:, OFF_QB + c * LANES:OFF_QB + (c + 1) * LANES], qg_ref[...])
            dp_ref[:, OFF_QB + c * LANES:OFF_QB + (c + 1) * LANES] = dx.astype(BF16)
            dqg = dqg + dg
        dqg_ref[...] += dqg
        dx, dg = norm_rope_b_bwd(dkb_ref[0] + dkb_ref[1], p_ref[:, OFF_KB:OFF_KB + LANES], kg_ref[...])
        dp_ref[:, OFF_KB:OFF_KB + LANES] = dx.astype(BF16)
        dkg_ref[...] += dg
        dp_ref[:, OFF_VB:OFF_VB + LANES] = (dvb_ref[0] + dvb_ref[1]).astype(BF16)

        @pl.when(pl.program_id(0) == t // tm - 1)
        def _():
            dqg_ref[...] = dqg_ref[...] + pltpu.roll(dqg_ref[...], HEAD_DIM, 1)
            dkg_ref[...] = dkg_ref[...] + pltpu.roll(dkg_ref[...], HEAD_DIM, 1)

    tab = pl.BlockSpec((tm, LANES), lambda i: (i, 0))
    vec = pl.BlockSpec((1, LANES), lambda i: (0, 0))
    wide = pl.BlockSpec((tm, A_Q), lambda i: (i, 0))
    slab = pl.BlockSpec((2, tm, LANES), lambda i: (0, i, 0))
    full = pl.BlockSpec((tm, ATT_IN), lambda i: (i, 0))
    return pl.pallas_call(
        body, name=name, grid=(t // tm,),
        in_specs=[full, tab, tab, tab, tab, vec, vec, wide, slab, slab, wide, slab, slab],
        out_specs=[full, vec, vec],
        out_shape=[jax.ShapeDtypeStruct((t, ATT_IN), BF16), jax.ShapeDtypeStruct((1, LANES), F32),
                   jax.ShapeDtypeStruct((1, LANES), F32)],
        compiler_params=_cparams(("arbitrary",)),
    )(proj, cos_a, sin_a, cos_b, sin_b, g2(qn_g), g2(kn_g), dqa, dka, dva, dqb, dkb, dvb)


def _head_to_half(xs, head_half, kv_half):
    low = _lane(xs.shape) < HEAD_DIM
    kept = jnp.where(low if head_half == 0 else jnp.logical_not(low), xs, 0.0)
    return jnp.where(kv_half == head_half, kept, pltpu.roll(kept, HEAD_DIM, 1))


def _halves_to_heads(r0, r1, kv_half):
    low = _lane(r0.shape) < HEAD_DIM
    a = jnp.where(kv_half == 0, r0, pltpu.roll(r0, HEAD_DIM, 1))
    b = jnp.where(kv_half == 1, r1, pltpu.roll(r1, HEAD_DIM, 1))
    return jnp.where(low, a, b)


def attn_delta(o, do, name):
    t, w = o.shape
    tm = min(t, 512)
    n_heads = w // HEAD_DIM

    def body(o_ref, do_ref, d_ref):
        lo, hi = _seg_matrix(0, HEAD_DIM), _seg_matrix(HEAD_DIM, LANES)
        for c in range(w // LANES):
            sl = slice(c * LANES, (c + 1) * LANES)
            s = o_ref[:, sl].astype(F32) * do_ref[:, sl].astype(F32)
            d_ref[2 * c] = _dot_f32_by_ones(s, lo)
            d_ref[2 * c + 1] = _dot_f32_by_ones(s, hi)

    blk = pl.BlockSpec((tm, w), lambda i: (i, 0))
    return pl.pallas_call(
        body, name=name, grid=(t // tm,),
        in_specs=[blk, blk],
        out_specs=pl.BlockSpec((n_heads, tm, LANES), lambda i: (0, i, 0)),
        out_shape=jax.ShapeDtypeStruct((n_heads, t, LANES), F32),
        compiler_params=_cparams(("parallel",)),
    )(o, do)


BAND = 3 * BLOCK


def _band(n, t, rows_rep):
    start = pl.multiple_of(jnp.clip((n - 1) * BLOCK, 0, t - BAND), BLOCK)
    qi = lax.broadcasted_iota(jnp.int32, (BLOCK, BAND), 0)
    kj = lax.broadcasted_iota(jnp.int32, (BLOCK, BAND), 1)
    ok = jnp.abs((start + kj) - (n * BLOCK + qi)) <= WINDOW
    return start, jnp.concatenate([ok] * rows_rep, axis=0)


def window_attn_fwd(q, k, v, sink, name, blocks_per_step=8):
    t = q.shape[0]
    assert t >= BAND
    nq = _fit(t // BLOCK, blocks_per_step)
    tq = nq * BLOCK

    def body(sink_ref, q_ref, k_ref, v_ref, o_ref, lse_ref):
        j, n0 = pl.program_id(0), pl.program_id(1)
        kvh = j // 2
        row = lax.broadcasted_iota(jnp.int32, (2 * BLOCK, 1), 0)
        sk = jnp.where(row < BLOCK, sink_ref[2 * j], sink_ref[2 * j + 1])
        for u in range(nq):
            rows = slice(u * BLOCK, (u + 1) * BLOCK)
            start, ok = _band(n0 * nq + u, t, 2)
            qf = q_ref[rows, :].astype(F32) * SCALE
            qs = jnp.concatenate([_head_to_half(qf, 0, kvh), _head_to_half(qf, 1, kvh)], axis=0).astype(BF16)
            s = jnp.where(ok, _dot_nt(qs, k_ref[pl.ds(start, BAND), :]), NEG)
            m = jnp.maximum(jnp.max(s, axis=-1, keepdims=True), sk)
            p = jnp.exp(s - m)
            denom = jnp.sum(p, axis=-1, keepdims=True) + jnp.exp(sk - m)
            o = _dot_nn(p.astype(BF16), v_ref[pl.ds(start, BAND), :]) / denom
            o_ref[rows, :] = _halves_to_heads(o[:BLOCK], o[BLOCK:], kvh).astype(BF16)
            lse = jnp.broadcast_to(m + jnp.log(denom), (2 * BLOCK, LANES))
            lse_ref[0, rows, :] = lse[:BLOCK]
            lse_ref[1, rows, :] = lse[BLOCK:]

    qspec = pl.BlockSpec((tq, LANES), lambda j, n: (n, j))
    whole = pl.BlockSpec((t, LANES), lambda j, n: (0, 0))
    return pl.pallas_call(
        body, name=name, grid=(A_HEADS // 2, t // tq),
        in_specs=[pl.BlockSpec(memory_space=pltpu.SMEM), qspec, whole, whole],
        out_specs=[qspec, pl.BlockSpec((2, tq, LANES), lambda j, n: (j, n, 0))],
        out_shape=[jax.ShapeDtypeStruct((t, A_Q + B_Q), BF16), jax.ShapeDtypeStruct((A_HEADS, t, LANES), F32)],
        compiler_params=_cparams(("parallel", "parallel")),
    )(sink, q, k, v)


def window_attn_bwd(q, k, v, sink, do, lse, delta, name, blocks_per_step=4):
    t = q.shape[0]
    assert t >= BAND
    nq = _fit(t // BLOCK, blocks_per_step)
    tq = nq * BLOCK
    grp = A_HEADS // A_KV_HEADS
    gw = grp * HEAD_DIM

    def body(sink_ref, q_ref, do_ref, k_ref, v_ref, lse_ref, dl_ref, dq_ref, dk_ref, dv_ref, ds_ref):
        kvh, n0 = pl.program_id(0), pl.program_id(1)

        @pl.when(n0 == 0)
        def _():
            dk_ref[...] = jnp.zeros_like(dk_ref)
            dv_ref[...] = jnp.zeros_like(dv_ref)
            ds_ref[...] = jnp.zeros_like(ds_ref)

        rid = lax.broadcasted_iota(jnp.int32, (8, LANES), 0)
        upd = jnp.zeros((8, LANES), F32)
        for u in range(nq):
            rows = slice(u * BLOCK, (u + 1) * BLOCK)
            start, ok = _band(n0 * nq + u, t, grp)
            band = pl.ds(start, BAND)
            qparts, doparts = [], []
            for hh in range(grp):
                sl = slice((hh // 2) * LANES, (hh // 2 + 1) * LANES)
                qparts.append(_head_to_half(q_ref[rows, sl].astype(F32) * SCALE, hh % 2, kvh))
                doparts.append(_head_to_half(do_ref[rows, sl].astype(F32), hh % 2, kvh))
            qs = jnp.concatenate(qparts, axis=0).astype(BF16)
            dos = jnp.concatenate(doparts, axis=0).astype(BF16)
            lse_b = jnp.concatenate([lse_ref[hh, rows, :] for hh in range(grp)], axis=0)
            dl_b = jnp.concatenate([dl_ref[hh, rows, :] for hh in range(grp)], axis=0)
            kband, vband = k_ref[band, :], v_ref[band, :]
            s = jnp.where(ok, _dot_nt(qs, kband), NEG)
            p = jnp.exp(s - lse_b[:, :1])
            dp = _dot_nt(dos, vband)
            dsc = (p * (dp - dl_b[:, :1])).astype(BF16)
            dv_ref[0, band, :] += _dot_tn(p.astype(BF16), dos)
            dk_ref[0, band, :] += _dot_tn(dsc, qs)
            dq = _dot_nn(dsc, kband) * SCALE
            for c in range(grp // 2):
                dq_ref[rows, c * LANES:(c + 1) * LANES] = _halves_to_heads(
                    dq[2 * c * BLOCK:(2 * c + 1) * BLOCK], dq[(2 * c + 1) * BLOCK:(2 * c + 2) * BLOCK], kvh).astype(dq_ref.dtype)
            for hh in range(grp):
                rs = slice(hh * BLOCK, (hh + 1) * BLOCK)
                tot = jnp.sum(jnp.exp(sink_ref[kvh * grp + hh] - lse_b[rs]) * dl_b[rs], axis=0, keepdims=True)
                upd = upd + jnp.where(rid == hh, -tot, 0.0)
        ds_ref[0] += upd

    qspec = pl.BlockSpec((tq, gw), lambda kvh, n: (n, kvh))
    whole = pl.BlockSpec((t, LANES), lambda kvh, n: (0, 0))
    stat = pl.BlockSpec((grp, tq, LANES), lambda kvh, n: (kvh, n, 0))
    slab = pl.BlockSpec((1, t, LANES), lambda kvh, n: (kvh, 0, 0))
    return pl.pallas_call(
        body, name=name, grid=(A_KV_HEADS, t // tq),
        in_specs=[pl.BlockSpec(memory_space=pltpu.SMEM), qspec, qspec, whole, whole, stat, stat],
        out_specs=[qspec, slab, slab, pl.BlockSpec((1, 8, LANES), lambda kvh, n: (kvh, 0, 0))],
        out_shape=[jax.ShapeDtypeStruct((t, A_Q), BF16), jax.ShapeDtypeStruct((A_KV_HEADS, t, LANES), F32),
                   jax.ShapeDtypeStruct((A_KV_HEADS, t, LANES), F32), jax.ShapeDtypeStruct((A_KV_HEADS, 8, LANES), F32)],
        compiler_params=_cparams(("arbitrary", "arbitrary")),
    )(sink, q, do, k, v, lse, delta)


def flash_attn_fwd(q, k, v, cat, name, exchange=None, tq=256, tk=2048):
    t = q.shape[0]
    tq, tk = _fit(t, tq), _fit(t, tk)
    nk = t // tk

    def body(q_ref, k_ref, v_ref, cat_ref, o_ref, lse_ref):
        del cat_ref
        kvh = pl.program_id(0) // 2
        qf = q_ref[...].astype(F32) * (SCALE * LOG2E)
        qs = jnp.concatenate([_head_to_half(qf, 0, kvh), _head_to_half(qf, 1, kvh)], axis=0).astype(BF16)
        mine = (_lane((tk, LANES)) < HEAD_DIM) == (kvh == 0)

        def scores(c):
            return _dot_nt(qs, k_ref[c * tk:(c + 1) * tk, :])

        s = scores(0)
        m = jnp.full((2 * tq, 1), NEG, F32)
        acc = jnp.zeros((2 * tq, LANES), F32)
        for c in range(nk):
            s_next = scores(c + 1) if c + 1 < nk else None
            vb = jnp.where(mine, v_ref[c * tk:(c + 1) * tk, :], jnp.ones((), BF16))
            m_new = jnp.maximum(m, jnp.max(s, axis=-1, keepdims=True))
            p = jnp.exp2(s - m_new).astype(BF16)
            acc = jnp.exp2(m - m_new) * acc + _dot_nn(p, vb)
            m, s = m_new, s_next
        other = pltpu.roll(acc, HEAD_DIM, 1)
        o = acc / other
        o_ref[...] = _halves_to_heads(o[:tq], o[tq:], kvh).astype(BF16)
        in_mine = (_lane(acc.shape) < HEAD_DIM) == (kvh == 0)
        lse = jnp.broadcast_to(m, acc.shape) + jnp.log2(jnp.where(in_mine, other, acc))
        lse_ref[0] = lse[:tq]
        lse_ref[1] = lse[tq:]

    qspec = pl.BlockSpec((tq, LANES), lambda j, i: (i, j))
    whole = pl.BlockSpec((t, LANES), lambda j, i: (0, 0))
    nj, ni = B_HEADS // 2, t // tq
    steps = lambda: ((pl.program_id(0) == 0) & (pl.program_id(1) == 0), (pl.program_id(0) == nj - 1) & (pl.program_id(1) == ni - 1))
    body, x_in, x_out, x_shapes, x_scratch = carried(body, exchange, 4, 2, steps)
    return pl.pallas_call(
        body, name=name, grid=(nj, ni),
        in_specs=[qspec, whole, whole, _ANY] + x_in,
        out_specs=[pl.BlockSpec((tq, LANES), lambda j, i: (i, A_Q // LANES + j)),
                   pl.BlockSpec((2, tq, LANES), lambda j, i: (j, i, 0))] + x_out,
        out_shape=[jax.ShapeDtypeStruct(cat.shape, BF16), jax.ShapeDtypeStruct((B_HEADS, t, LANES), F32)] + x_shapes,
        scratch_shapes=x_scratch,
        input_output_aliases={3: 0},
        compiler_params=_cparams(("arbitrary", "arbitrary")),
    )(q, k, v, cat, *(exchange.arrays if exchange else ()))


def flash_attn_bwd(q, k, v, do, lse, delta, name, exchange=None, tq=256, tk=512):
    t = q.shape[0]
    tq, tk = _fit(t, tq), _fit(t, tk)
    nk = t // tk
    grp = B_HEADS // B_KV_HEADS
    gw = grp * HEAD_DIM

    def body(q_ref, do_ref, k_ref, v_ref, lse_ref, dl_ref, dq_ref, dk_ref, dv_ref, dq_s):
        kvh, i = pl.program_id(0), pl.program_id(1)

        @pl.when(i == 0)
        def _():
            dk_ref[...] = jnp.zeros_like(dk_ref)
            dv_ref[...] = jnp.zeros_like(dv_ref)

        qparts, doparts = [], []
        for hh in range(grp):
            sl = slice((hh // 2) * LANES, (hh // 2 + 1) * LANES)
            qparts.append(_head_to_half(q_ref[:, sl].astype(F32) * (SCALE * LOG2E), hh % 2, kvh))
            doparts.append(_head_to_half(do_ref[:, sl].astype(F32), hh % 2, kvh))
        qs = jnp.concatenate(qparts, axis=0).astype(BF16)
        dos = jnp.concatenate(doparts, axis=0).astype(BF16)
        lse = jnp.concatenate([lse_ref[hh][:, :1] for hh in range(grp)], axis=0)
        dl = jnp.concatenate([dl_ref[hh][:, :1] for hh in range(grp)], axis=0)
        dq_s[...] = jnp.zeros_like(dq_s)

        def chunk(c, carry):
            rows = pl.ds(pl.multiple_of(c * tk, tk), tk)
            kb, vb = k_ref[rows, :], v_ref[rows, :]
            p = jnp.exp2(_dot_nt(qs, kb) - lse)
            dp = _dot_nt(dos, vb)
            dsc = (p * (dp - dl)).astype(BF16)
            dv_ref[0, rows, :] += _dot_tn(p.astype(BF16), dos)
            dk_ref[0, rows, :] += _dot_tn(dsc, qs) * LN2
            dq_s[...] += _dot_nn(dsc, kb)
            return carry

        lax.fori_loop(0, nk, chunk, 0)
        for c in range(grp // 2):
            dq_ref[:, c * LANES:(c + 1) * LANES] = (_halves_to_heads(
                dq_s[2 * c * tq:(2 * c + 1) * tq], dq_s[(2 * c + 1) * tq:(2 * c + 2) * tq], kvh) * SCALE).astype(dq_ref.dtype)

    qspec = pl.BlockSpec((tq, gw), lambda kvh, i: (i, kvh))
    dospec = pl.BlockSpec((tq, gw), lambda kvh, i: (i, A_Q // gw + kvh))
    whole = pl.BlockSpec((t, LANES), lambda kvh, i: (0, 0))
    stat = pl.BlockSpec((grp, tq, LANES), lambda kvh, i: (kvh, i, 0))
    dlstat = pl.BlockSpec((grp, tq, LANES), lambda kvh, i: (A_HEADS // grp + kvh, i, 0))
    slab = pl.BlockSpec((1, t, LANES), lambda kvh, i: (kvh, 0, 0))
    ni = t // tq
    steps = lambda: ((pl.program_id(0) == 0) & (pl.program_id(1) == 0),
                     (pl.program_id(0) == B_KV_HEADS - 1) & (pl.program_id(1) == ni - 1))
    body, x_in, x_out, x_shapes, x_scratch = carried(body, exchange, 6, 3, steps)
    return pl.pallas_call(
        body, name=name, grid=(B_KV_HEADS, ni),
        in_specs=[qspec, dospec, whole, whole, stat, dlstat] + x_in,
        out_specs=[qspec, slab, slab] + x_out,
        out_shape=[jax.ShapeDtypeStruct((t, B_Q), BF16), jax.ShapeDtypeStruct((B_KV_HEADS, t, LANES), F32),
                   jax.ShapeDtypeStruct((B_KV_HEADS, t, LANES), F32)] + x_shapes,
        scratch_shapes=[pltpu.VMEM((grp * tq, LANES), F32)] + x_scratch,
        compiler_params=_cparams(("arbitrary", "arbitrary")),
    )(q, do, k, v, lse, delta, *(exchange.arrays if exchange else ()))


_GELU_C = math.sqrt(2.0 / math.pi)
_GELU_A = 0.044715


def _gelu(x):
    return 0.5 * x * (1.0 + jnp.tanh(_GELU_C * (x + _GELU_A * x * x * x)))


def _gelu_grad(x):
    th = jnp.tanh(_GELU_C * (x + _GELU_A * x * x * x))
    return 0.5 * (1.0 + th) + 0.5 * x * (1.0 - th * th) * _GELU_C * (1.0 + 3.0 * _GELU_A * x * x)


def _layernorm_stats(vf):
    mu = jnp.mean(vf, axis=-1, keepdims=True)
    vc = vf - mu
    r = lax.rsqrt(jnp.mean(vc * vc, axis=-1, keepdims=True) + EPS)
    return vc * r, r


def sgu_mix_fwd(z, ln_g, ln_b, w_s, b_rows, name):
    t, w2 = z.shape
    w = w2 // 2
    dg = w // SGU_GROUPS

    def body(u_ref, v_ref, g_ref, b_ref, ws_ref, bb_ref, y_ref):
        vhat, _ = _layernorm_stats(v_ref[...].astype(F32))
        vn = (vhat * g_ref[...] + b_ref[...]).astype(BF16)
        for g in range(SGU_GROUPS):
            sl = slice(g * dg, (g + 1) * dg)
            mixed = _dot_nn(ws_ref[g], vn[:, sl]) + bb_ref[g]
            y_ref[:, sl] = (u_ref[:, sl].astype(F32) * mixed).astype(BF16)

    vec = pl.BlockSpec((1, w), lambda n: (0, 0))
    whole = pl.BlockSpec((SGU_GROUPS, SGU_CHUNK, SGU_CHUNK), lambda n: (0, 0, 0))
    return pl.pallas_call(
        body, name=name, grid=(t // SGU_CHUNK,),
        in_specs=[pl.BlockSpec((SGU_CHUNK, w), lambda n: (n, 0)), pl.BlockSpec((SGU_CHUNK, w), lambda n: (n, 1)),
                  vec, vec, whole, whole],
        out_specs=pl.BlockSpec((SGU_CHUNK, w), lambda n: (n, 0)),
        out_shape=jax.ShapeDtypeStruct((t, w), BF16),
        compiler_params=_cparams(("parallel",)),
    )(z, z, ln_g.reshape(1, w), ln_b.reshape(1, w), w_s, b_rows)


def sgu_mix_bwd(z, apre, dy, ln_g, ln_b, w_s, b_rows, name):
    t, w2 = z.shape
    w = w2 // 2
    dg = w // SGU_GROUPS

    def body(u_ref, v_ref, au_ref, av_ref, dy_ref, g_ref, b_ref, ws_ref, bb_ref, da_ref, dlg_ref, dlb_ref, dws_ref, dbs_ref):
        @pl.when(pl.program_id(0) == 0)
        def _():
            dlg_ref[...] = jnp.zeros_like(dlg_ref)
            dlb_ref[...] = jnp.zeros_like(dlb_ref)
            dws_ref[...] = jnp.zeros_like(dws_ref)
            dbs_ref[...] = jnp.zeros_like(dbs_ref)

        vhat, r = _layernorm_stats(v_ref[...].astype(F32))
        gam = g_ref[...]
        vn = (vhat * gam + b_ref[...]).astype(BF16)
        ones8 = jnp.ones((8, dg), BF16)
        rid = lax.broadcasted_iota(jnp.int32, (8, SGU_CHUNK), 0)
        dbs = jnp.zeros((8, SGU_CHUNK), F32)
        dvn_parts = []
        for g in range(SGU_GROUPS):
            sl = slice(g * dg, (g + 1) * dg)
            dyg = dy_ref[:, sl].astype(F32)
            mixed = _dot_nn(ws_ref[g], vn[:, sl]) + bb_ref[g]
            da_ref[:, sl] = (dyg * mixed * _gelu_grad(au_ref[:, sl].astype(F32))).astype(BF16)
            dmix = dyg * u_ref[:, sl].astype(F32)
            dm_hi = dmix.astype(BF16)
            dm_lo = (dmix - dm_hi.astype(F32)).astype(BF16)
            dws_ref[g] += _dot_nt(dm_hi, vn[:, sl])
            dbs = dbs + jnp.where(rid == g, _dot_nt(ones8, dm_hi) + _dot_nt(ones8, dm_lo), 0.0)
            dvn_parts.append(_dot_tn(ws_ref[g], dm_hi))
        dbs_ref[...] += dbs
        dvn = jnp.concatenate(dvn_parts, axis=1)
        dlg_ref[...] += jnp.sum(dvn * vhat, axis=0, keepdims=True)
        dlb_ref[...] += jnp.sum(dvn, axis=0, keepdims=True)
        dvh = dvn * gam
        dv = r * (dvh - jnp.mean(dvh, axis=-1, keepdims=True) - vhat * jnp.mean(dvh * vhat, axis=-1, keepdims=True))
        da_ref[:, w:] = (dv * _gelu_grad(av_ref[...].astype(F32))).astype(BF16)

    vec = pl.BlockSpec((1, w), lambda n: (0, 0))
    whole = pl.BlockSpec((SGU_GROUPS, SGU_CHUNK, SGU_CHUNK), lambda n: (0, 0, 0))
    left = pl.BlockSpec((SGU_CHUNK, w), lambda n: (n, 0))
    right = pl.BlockSpec((SGU_CHUNK, w), lambda n: (n, 1))
    return pl.pallas_call(
        body, name=name, grid=(t // SGU_CHUNK,),
        in_specs=[left, right, left, right, left, vec, vec, whole, whole],
        out_specs=[pl.BlockSpec((SGU_CHUNK, w2), lambda n: (n, 0)), vec, vec, whole,
                   pl.BlockSpec((SGU_GROUPS, SGU_CHUNK), lambda n: (0, 0))],
        out_shape=[jax.ShapeDtypeStruct((t, w2), BF16), jax.ShapeDtypeStruct((1, w), F32), jax.ShapeDtypeStruct((1, w), F32),
                   jax.ShapeDtypeStruct((SGU_GROUPS, SGU_CHUNK, SGU_CHUNK), F32),
                   jax.ShapeDtypeStruct((SGU_GROUPS, SGU_CHUNK), F32)],
        compiler_params=_cparams(("arbitrary",)),
    )(z, z, apre, apre, dy, ln_g.reshape(1, w), ln_b.reshape(1, w), w_s, b_rows)


def loss_head(h, g, target, name):
    t, d = h.shape
    tm = min(t, 512)

    def body(h_ref, g_ref, t_ref, loss_ref, dh_ref, dhb_ref, dg_ref):
        @pl.when(pl.program_id(0) == 0)
        def _():
            loss_ref[...] = jnp.zeros_like(loss_ref)
            dg_ref[...] = jnp.zeros_like(dg_ref)

        xf = h_ref[...]
        r = lax.rsqrt(jnp.mean(xf * xf, axis=-1, keepdims=True) + EPS)
        xhat = xf * r
        err = xhat * g_ref[...] - t_ref[...]
        per_tok = jnp.mean(err * err, axis=-1, keepdims=True)
        loss_ref[...] += 0.5 * jnp.sum(per_tok, axis=0, keepdims=True)
        dy = err * (1.0 / d)
        dg_ref[...] += jnp.sum(dy * xhat, axis=0, keepdims=True)
        dxh = dy * g_ref[...]
        dh = r * (dxh - xhat * jnp.mean(dxh * xhat, axis=-1, keepdims=True))
        dh_ref[...] = dh
        dhb_ref[...] = dh.astype(BF16)

    row = pl.BlockSpec((tm, d), lambda i: (i, 0))
    vec = pl.BlockSpec((1, d), lambda i: (0, 0))
    return pl.pallas_call(
        body, name=name, grid=(t // tm,),
        in_specs=[row, vec, row],
        out_specs=[pl.BlockSpec((1, LANES), lambda i: (0, 0)), row, row, vec],
        out_shape=[jax.ShapeDtypeStruct((1, LANES), F32), jax.ShapeDtypeStruct((t, d), F32), jax.ShapeDtypeStruct((t, d), BF16),
                   jax.ShapeDtypeStruct((1, d), F32)],
        compiler_params=_cparams(("arbitrary",)),
    )(h, g.reshape(1, d), target)


ADAMW_BLOCK_BYTES = 1 << 20


def adamw(parts, w, m, v, name):
    n_layers, r, c = w.shape
    row_bytes = n_layers * c * 4
    if r * row_bytes <= 2 * ADAMW_BLOCK_BYTES:
        tr = r
    else:
        tr = _fit(r, 1 << int(math.log2(max(8, ADAMW_BLOCK_BYTES // row_bytes))))
    bc1 = 1.0 - ADAM_B1 ** ADAM_STEP
    bc2 = 1.0 - ADAM_B2 ** ADAM_STEP

    def body(*refs):
        p_refs = refs[:n_layers]
        w_ref, m_ref, v_ref, g_ref, d_ref, nm_ref, nv_ref = refs[n_layers:]
        for l in range(n_layers):
            g = p_refs[l][0].astype(F32)
            for j in range(1, N_DEV):
                g = g + p_refs[l][j].astype(F32)
            nm = ADAM_B1 * m_ref[l] + (1.0 - ADAM_B1) * g
            nv = ADAM_B2 * v_ref[l] + (1.0 - ADAM_B2) * (g * g)
            g_ref[l] = g
            nm_ref[l] = nm
            nv_ref[l] = nv
            d_ref[l] = -ADAM_LR * ((nm / bc1) / (jnp.sqrt(nv / bc2) + ADAM_EPS) + ADAM_WD * w_ref[l])

    blk = pl.BlockSpec((n_layers, tr, c), lambda i: (0, i, 0))
    return pl.pallas_call(
        body, name=name, grid=(r // tr,),
        in_specs=[pl.BlockSpec((N_DEV, tr, c), lambda i: (0, i, 0))] * n_layers + [blk, blk, blk],
        out_specs=[blk] * 4,
        out_shape=[jax.ShapeDtypeStruct((n_layers, r, c), F32)] * 4,
        compiler_params=_cparams(("parallel",)),
    )(*parts, w, m, v)


_ANY = pl.BlockSpec(memory_space=pl.ANY)


def _mesh_pos():
    return lax.axis_index("x"), lax.axis_index("y"), lax.axis_index("c")


class Exchange:
    def __init__(self, gathers=(), scatters=()):
        self.items = [("gather", a) for a in gathers] + [("scatter", a) for a in scatters]
        self.arrays = [a for _, a in self.items]
        self.n = len(self.items)

    def out_shapes(self):
        return [jax.ShapeDtypeStruct(((N_DEV,) + a.shape) if kind == "gather" else a.shape, a.dtype) for kind, a in self.items]

    def scratch(self):
        return [pltpu.SemaphoreType.DMA((7 * self.n,)), pltpu.SemaphoreType.DMA((7 * self.n,)), pltpu.SemaphoreType.DMA((self.n,))]

    def _copies(self, in_refs, out_refs, send_sems, recv_sems, local_sems):
        x, y, c = _mesh_pos()
        me = 4 * x + 2 * y + c
        local, sends, arrivals = [], [], []
        for t, (kind, _) in enumerate(self.items):
            src_of = (lambda slot, r=in_refs[t]: r) if kind == "gather" else (lambda slot, r=in_refs[t]: r.at[slot])
            local.append(pltpu.make_async_copy(src_of(me), out_refs[t].at[me], local_sems.at[t]))
            for k in range(1, N_DEV):
                px = 1 - x if k & 4 else x
                py = 1 - y if k & 2 else y
                pc = 1 - c if k & 1 else c
                pid = 4 * px + 2 * py + pc
                kw = dict(send_sem=send_sems.at[7 * t + k - 1], recv_sem=recv_sems.at[7 * t + k - 1],
                          device_id=(px, py, pc), device_id_type=pl.DeviceIdType.MESH)
                sends.append(pltpu.make_async_remote_copy(src_ref=src_of(pid), dst_ref=out_refs[t].at[me], **kw))
                arrivals.append(pltpu.make_async_remote_copy(src_ref=src_of(pid), dst_ref=out_refs[t].at[pid], **kw))
        return local, sends, arrivals

    def start(self, *refs):
        local, sends, _ = self._copies(*refs)
        for cp in local + sends:
            cp.start()

    def wait(self, *refs):
        local, sends, arrivals = self._copies(*refs)
        for cp in arrivals:
            cp.wait_recv()
        for cp in sends:
            cp.wait_send()
        for cp in local:
            cp.wait()


def carried(body, exchange, n_in, n_out, first_last):
    if exchange is None:
        return body, [], [], [], []
    nx = exchange.n

    def wrapped(*refs):
        ins, xin = refs[:n_in], refs[n_in:n_in + nx]
        outs, xout = refs[n_in + nx:n_in + nx + n_out], refs[n_in + nx + n_out:n_in + 2 * nx + n_out]
        scratch, sems = refs[n_in + 2 * nx + n_out:-3], refs[-3:]
        first, last = first_last()

        @pl.when(first)
        def _():
            exchange.start(xin, xout, *sems)

        body(*ins, *outs, *scratch)

        @pl.when(last)
        def _():
            exchange.wait(xin, xout, *sems)

    return wrapped, [_ANY] * nx, [_ANY] * nx, exchange.out_shapes(), exchange.scratch()


def exchange_only(exchange, name):
    def body(*refs):
        xin, xout, sems = refs[:exchange.n], refs[exchange.n:2 * exchange.n], refs[-3:]
        exchange.start(xin, xout, *sems)
        exchange.wait(xin, xout, *sems)

    return pl.pallas_call(
        body, name=name, in_specs=[_ANY] * exchange.n, out_specs=[_ANY] * exchange.n,
        out_shape=exchange.out_shapes(), scratch_shapes=exchange.scratch(),
    )(*exchange.arrays)


def attention_fwd(x, norm_g, w_in, sink, qn_g, kn_g, w_out, tables, tag, exchange=None):
    h = rmsnorm_fwd(x, norm_g, f"{tag}_norm")
    (proj,) = matmul(h, w_in, "nn", f"{tag}_proj", [F32], tn=ATT_IN)
    qa, ka, va, qb, kb, vb = qkv_post_fwd(proj, tables, qn_g, kn_g, f"{tag}_qkv")
    cat, lse_a = window_attn_fwd(qa, ka, va, sink, f"{tag}_win")
    cat, lse_b, *arrived = flash_attn_fwd(qb, kb, vb, cat, f"{tag}_flash", exchange)
    (y,) = matmul(cat, w_out, "nn", f"{tag}_out", [F32], epilogue=lambda acc, r: (r + acc,), extras=(x,))
    saved = (x, h, proj, qa, ka, va, qb, kb, vb, cat, lse_a, lse_b)
    return y, saved, arrived


def attention_bwd(dy, dyb, saved, norm_g, w_in, sink, qn_g, kn_g, w_out, tables, tag, exchange_with=None):
    x, h, proj, qa, ka, va, qb, kb, vb, cat, lse_a, lse_b = saved
    (dcat,) = matmul(dyb, w_out, "nt", f"{tag}_dcat", [BF16])
    (dw_out,) = matmul(cat, dyb, "tn", f"{tag}_dwout", [BF16], tk=2048)
    delta = attn_delta(cat, dcat, f"{tag}_delta")
    dqa, dka, dva, dsink = window_attn_bwd(qa, ka, va, sink, dcat, lse_a, delta, f"{tag}_dwin")
    exchange = exchange_with(dw_out) if exchange_with else None
    dqb, dkb, dvb, *arrived = flash_attn_bwd(qb, kb, vb, dcat, lse_b, delta, f"{tag}_dflash", exchange)
    dproj, dqg, dkg = qkv_post_bwd(proj, tables, qn_g, kn_g, dqa, dka, dva, dqb, dkb, dvb, f"{tag}_dqkv")
    (dw_in,) = matmul(h, dproj, "tn", f"{tag}_dwin_w", [BF16], tn=ATT_IN // 2, tk=2048)
    dx, dxb, dg = matmul_nt_normbwd(dproj, w_in, x, norm_g, dy, f"{tag}_dx")
    grp = A_HEADS // A_KV_HEADS
    small = dict(norm=dg[0], sink=dsink[:, :grp, 0].reshape(A_HEADS), qnorm=dqg[0, :HEAD_DIM], knorm=dkg[0, :HEAD_DIM])
    return dx, dxb, dw_in, dw_out, small, arrived


def sgu_fwd(x, norm_g, w_in, ln_g, ln_b, w_s, b_rows, w_out, tag):
    h = rmsnorm_fwd(x, norm_g, f"{tag}_norm")
    apre, z = matmul(h, w_in, "nn", f"{tag}_in", [BF16, BF16], epilogue=lambda acc: (acc, _gelu(acc)))
    y = sgu_mix_fwd(z, ln_g, ln_b, w_s, b_rows, f"{tag}_mix")
    (out,) = matmul(y, w_out, "nn", f"{tag}_out", [F32], epilogue=lambda acc, r: (r + acc,), extras=(x,))
    return out, (x, h, apre, z, y)


def sgu_bwd(dout, doutb, saved, norm_g, w_in, ln_g, ln_b, w_s, b_rows, w_out, tag):
    x, h, apre, z, y = saved
    (dy,) = matmul(doutb, w_out, "nt", f"{tag}_dy", [BF16])
    (dw_out,) = matmul(y, doutb, "tn", f"{tag}_dwout", [BF16], tk=2048)
    dapre, dlg, dlb, dws, dbs = sgu_mix_bwd(z, apre, dy, ln_g, ln_b, w_s, b_rows, f"{tag}_dmix")
    (dw_in,) = matmul(h, dapre, "tn", f"{tag}_dwin", [BF16], out_shards=True, tk=4096)
    dx, dxb, dg = matmul_nt_normbwd(dapre, w_in, x, norm_g, dout, f"{tag}_dx")
    small = dict(norm=dg[0], ln_g=dlg[0], ln_b=dlb[0], w_s=dws, b_s=dbs)
    return dx, dxb, dw_in, dw_out, small


def _square(r):
    return r * r


def mlp_fwd(x, norm_g, w1, w2, tag):
    h = rmsnorm_fwd(x, norm_g, f"{tag}_norm")
    (r,) = matmul(h, w1, "nn", f"{tag}_up", [BF16], epilogue=lambda acc: (jnp.maximum(acc, 0.0),), tm=2048)
    (y,) = matmul(r, w2, "nn", f"{tag}_down", [F32], epilogue=lambda acc, res: (res + acc,), extras=(x,), a_fn=_square, tk=2048)
    return y, (x, h, r)


def mlp_bwd(dy, dyb, saved, norm_g, w1, w2, tag):
    x, h, r = saved
    (da,) = matmul(dyb, w2, "nt", f"{tag}_da", [BF16], epilogue=lambda acc, rr: (acc * (2.0 * rr.astype(F32)),), extras=(r,),
                   tm=2048)
    (dw2,) = matmul(r, dyb, "tn", f"{tag}_dw2", [BF16], a_fn=_square, tk=2048)
    (dw1,) = matmul(h, da, "tn", f"{tag}_dw1", [BF16], out_shards=True, tk=4096)
    dx, dxb, dg = matmul_nt_normbwd(da, w1, x, norm_g, dy, f"{tag}_dx")
    return dx, dxb, dw1, dw2, dg[0]


ORDER = ("att_norm", "att_w_in", "att_sink", "att_qnorm", "att_knorm", "att_w_out", "sgu_norm", "sgu_w_in", "sgu_ln_g",
         "sgu_ln_b", "sgu_w_s", "sgu_b_s", "sgu_w_out", "mlp_norm", "mlp_w1", "mlp_w2", "final_norm")
SHARDED = ("att_w_in", "att_w_out", "sgu_w_in", "sgu_w_out", "mlp_w1", "mlp_w2")
SGU_VECS = ("sgu_norm", "sgu_ln_g", "sgu_ln_b")
SMALL_EARLY = ("sgu_w_s", "sgu_b_s", "mlp_norm", "final_norm", "loss")
SMALL_LATE = ("att_norm", "att_sink", "att_qnorm", "att_knorm")
SMALL_ROWS_MULT = 8


def _flat(blocks, names):
    flat = jnp.concatenate([blocks[n].reshape(-1).astype(F32) for n in names])
    per = SMALL_ROWS_MULT * FLAT_COLS
    total = -(-flat.shape[0] // per) * per
    return jnp.pad(flat, (0, total - flat.shape[0])).reshape(1, total // FLAT_COLS, FLAT_COLS)


def _unflat(flat, like, names):
    out, off = {}, 0
    f = flat.reshape(-1)
    for n in names:
        size = like[n].size
        out[n] = f[off:off + size].reshape(like[n].shape)
        off += size
    return out


def kernel(x, att_norm, att_w_in, att_sink, att_qnorm, att_knorm, att_w_out, sgu_norm, sgu_w_in, sgu_ln_g, sgu_ln_b, sgu_w_s, sgu_b_s, sgu_w_out, mlp_norm, mlp_w1, mlp_w2, final_norm, loss_target, m_att_norm, m_att_w_in, m_att_sink, m_att_qnorm, m_att_knorm, m_att_w_out, m_sgu_norm, m_sgu_w_in, m_sgu_ln_g, m_sgu_ln_b, m_sgu_w_s, m_sgu_b_s, m_sgu_w_out, m_mlp_norm, m_mlp_w1, m_mlp_w2, m_final_norm, v_att_norm, v_att_w_in, v_att_sink, v_att_qnorm, v_att_knorm, v_att_w_out, v_sgu_norm, v_sgu_w_in, v_sgu_ln_g, v_sgu_ln_b, v_sgu_w_s, v_sgu_b_s, v_sgu_w_out, v_mlp_norm, v_mlp_w1, v_mlp_w2, v_final_norm):
    w = dict(att_norm=att_norm, att_w_in=att_w_in, att_sink=att_sink, att_qnorm=att_qnorm, att_knorm=att_knorm,
             att_w_out=att_w_out, sgu_norm=sgu_norm, sgu_w_in=sgu_w_in, sgu_ln_g=sgu_ln_g, sgu_ln_b=sgu_ln_b, sgu_w_s=sgu_w_s,
             sgu_b_s=sgu_b_s, sgu_w_out=sgu_w_out, mlp_norm=mlp_norm, mlp_w1=mlp_w1, mlp_w2=mlp_w2, final_norm=final_norm)
    m = dict(att_norm=m_att_norm, att_w_in=m_att_w_in, att_sink=m_att_sink, att_qnorm=m_att_qnorm, att_knorm=m_att_knorm,
             att_w_out=m_att_w_out, sgu_norm=m_sgu_norm, sgu_w_in=m_sgu_w_in, sgu_ln_g=m_sgu_ln_g, sgu_ln_b=m_sgu_ln_b,
             sgu_w_s=m_sgu_w_s, sgu_b_s=m_sgu_b_s, sgu_w_out=m_sgu_w_out, mlp_norm=m_mlp_norm, mlp_w1=m_mlp_w1, mlp_w2=m_mlp_w2,
             final_norm=m_final_norm)
    v = dict(att_norm=v_att_norm, att_w_in=v_att_w_in, att_sink=v_att_sink, att_qnorm=v_att_qnorm, att_knorm=v_att_knorm,
             att_w_out=v_att_w_out, sgu_norm=v_sgu_norm, sgu_w_in=v_sgu_w_in, sgu_ln_g=v_sgu_ln_g, sgu_ln_b=v_sgu_ln_b,
             sgu_w_s=v_sgu_w_s, sgu_b_s=v_sgu_b_s, sgu_w_out=v_sgu_w_out, mlp_norm=v_mlp_norm, mlp_w1=v_mlp_w1, mlp_w2=v_mlp_w2,
             final_norm=v_final_norm)
    loss, grad_x, g, d, nm, nv = train_step(x[0], loss_target[0], w, m, v)
    return (loss, grad_x[None], *[g[n] for n in ORDER], *[d[n] for n in ORDER], *[nm[n] for n in ORDER], *[nv[n] for n in ORDER])


def train_step(x, target, w, m, v):
    t, d_model = x.shape
    n_att, n_sgu, depth = w["att_w_in"].shape[0], w["sgu_w_in"].shape[0], w["mlp_w1"].shape[0]
    bf = lambda n: w[n].astype(BF16)

    vec_local = jnp.stack([w[n] for n in SGU_VECS], axis=1)
    g_att_in, g_att_out, g_vec = exchange_only(Exchange(gathers=[bf("att_w_in"), bf("att_w_out"), vec_local]), "gather_attention")
    att_w_in = g_att_in.transpose(1, 2, 0, 3).reshape(n_att, d_model, ATT_IN)
    vecs = g_vec.transpose(1, 2, 0, 3).reshape(n_sgu, len(SGU_VECS), -1)
    rest = Exchange(gathers=[bf("sgu_w_in"), bf("sgu_w_out"), bf("mlp_w1"), bf("mlp_w2")])
    w_s_bf = w["sgu_w_s"].astype(BF16)
    b_rows = jnp.broadcast_to(w["sgu_b_s"][:, :, :, None], w["sgu_b_s"].shape + (LANES,))
    tables = _rope_tables(t)

    saved, h = [], x
    for layer in range(depth):
        i = layer // 2
        if layer % 2 == 0:
            h, sv, arrived = attention_fwd(h, w["att_norm"][i], att_w_in[i], w["att_sink"][i], w["att_qnorm"][i], w["att_knorm"][i],
                                           Gathered(g_att_out, "row", i), tables, f"att{i}", rest if layer == 0 else None)
            if layer == 0:
                g_sgu_in, g_sgu_out, g_w1, g_w2 = arrived
        else:
            h, sv = sgu_fwd(h, vecs[i, 0], Gathered(g_sgu_in, "col", i), vecs[i, 1], vecs[i, 2], w_s_bf[i], b_rows[i],
                            Gathered(g_sgu_out, "row", i), f"sgu{i}")
        h, sm = mlp_fwd(h, w["mlp_norm"][layer], Gathered(g_w1, "col", layer), Gathered(g_w2, "row", layer), f"mlp{layer}")
        saved.append((sv, sm))
    loss_row, dh, dhb, dgf = loss_head(h, w["final_norm"], target, "loss_head")

    queue, recv = [], {}
    gs = dict(att_norm=[None] * n_att, att_sink=[None] * n_att, att_qnorm=[None] * n_att, att_knorm=[None] * n_att,
              sgu_w_s=[None] * n_sgu, sgu_b_s=[None] * n_sgu, mlp_norm=[None] * depth)

    def row_slabs(g):
        return g.reshape(N_DEV, g.shape[0] // N_DEV, g.shape[1])

    def col_slabs(g):
        return g.reshape(g.shape[0], N_DEV, g.shape[1] // N_DEV).transpose(1, 0, 2)

    def take_queue(gathers=()):
        items = list(queue)
        queue.clear()
        keys = [k for k, _ in gathers] + [k for k, _ in items]
        return Exchange(gathers=[a for _, a in gathers], scatters=[a for _, a in items]), keys

    def small_early():
        blocks = dict(sgu_w_s=jnp.stack(gs["sgu_w_s"]), sgu_b_s=jnp.stack(gs["sgu_b_s"]), mlp_norm=jnp.stack(gs["mlp_norm"]),
                      final_norm=dgf[0], loss=loss_row[0, :1])
        return _flat(blocks, SMALL_EARLY)[0]

    for layer in reversed(range(depth)):
        i = layer // 2
        sv, sm = saved[layer]
        dh, dhb, dw1, dw2, gs["mlp_norm"][layer] = mlp_bwd(
            dh, dhb, sm, w["mlp_norm"][layer], Gathered(g_w1, "col", layer), Gathered(g_w2, "row", layer), f"mlp{layer}")
        queue += [(("mlp_w1", layer), dw1), (("mlp_w2", layer), row_slabs(dw2))]
        if layer % 2 == 0:
            keys = []

            def exchange_with(dw_out, i=i, layer=layer, keys=keys):
                if layer == 0:
                    queue.append((("att_w_out", i), row_slabs(dw_out)))
                ex, got = take_queue([("small_early", small_early())] if layer == 0 else ())
                keys += got
                return ex

            dh, dhb, dw_in, dw_out, sm_g, arrived = attention_bwd(
                dh, dhb, sv, w["att_norm"][i], att_w_in[i], w["att_sink"][i], w["att_qnorm"][i], w["att_knorm"][i],
                Gathered(g_att_out, "row", i), tables, f"att{i}", exchange_with)
            recv.update(zip(keys, arrived))
            queue.append((("att_w_in", i), col_slabs(dw_in)))
            if layer != 0:
                queue.append((("att_w_out", i), row_slabs(dw_out)))
            gs["att_norm"][i], gs["att_sink"][i] = sm_g["norm"], sm_g["sink"]
            gs["att_qnorm"][i], gs["att_knorm"][i] = sm_g["qnorm"], sm_g["knorm"]
        else:
            dh, dhb, dw_in, dw_out, sm_g = sgu_bwd(
                dh, dhb, sv, vecs[i, 0], Gathered(g_sgu_in, "col", i), vecs[i, 1], vecs[i, 2], w_s_bf[i], b_rows[i],
                Gathered(g_sgu_out, "row", i), f"sgu{i}")
            dvec = jnp.stack([sm_g["norm"], sm_g["ln_g"], sm_g["ln_b"]])
            queue += [(("sgu_w_in", i), dw_in), (("sgu_w_out", i), row_slabs(dw_out)), (("sgu_vecs", i), col_slabs(dvec))]
            gs["sgu_w_s"][i], gs["sgu_b_s"][i] = sm_g["w_s"], sm_g["b_s"]
    grad_x = dh
    late = dict(att_norm=jnp.stack(gs["att_norm"]), att_sink=jnp.stack(gs["att_sink"]), att_qnorm=jnp.stack(gs["att_qnorm"]),
                att_knorm=jnp.stack(gs["att_knorm"]))
    last, keys = take_queue([("small_late", _flat(late, SMALL_LATE)[0])])
    recv.update(zip(keys, exchange_only(last, "exchange_last")))

    outs = [{}, {}, {}, {}]
    for n in SHARDED:
        res = adamw([recv[(n, l)] for l in range(w[n].shape[0])], w[n], m[n], v[n], f"adamw_{n}")
        for o, r in zip(outs, res):
            o[n] = r
    stack_vecs = lambda src: jnp.stack([src[n] for n in SGU_VECS], axis=1)
    res = adamw([recv[("sgu_vecs", i)] for i in range(n_sgu)], stack_vecs(w), stack_vecs(m), stack_vecs(v), "adamw_sgu_vecs")
    for o, r in zip(outs, res):
        o.update({n: r[:, k] for k, n in enumerate(SGU_VECS)})
    zero = {"loss": jnp.zeros((1,), F32)}
    for names, key in ((SMALL_EARLY, "small_early"), (SMALL_LATE, "small_late")):
        res = adamw([recv[key]], _flat({**w, **zero}, names), _flat({**m, **zero}, names), _flat({**v, **zero}, names), f"adamw_{key}")
        for o, r in zip(outs, res):
            o.update(_unflat(r, {**w, **zero}, names))
    loss = outs[0]["loss"][0]
    return loss, grad_x, *outs
```

```python
import functools
import math

import jax
import jax.numpy as jnp
from jax import lax
from jax.experimental import pallas as pl
from jax.experimental.pallas import tpu as pltpu

F32 = jnp.float32
BF16 = jnp.bfloat16

HEAD_DIM = 64
A_HEADS = 8
A_KV_HEADS = 2
B_HEADS = 8
B_KV_HEADS = 2
WINDOW = 128
BLOCK = 128
ROPE_THETA = 10000.0
GRID_W = 64
SGU_GROUPS = 8
SGU_CHUNK = 128
EPS = 1e-6
SCALE = HEAD_DIM ** -0.5
NEG = -1e30
LOG2E = math.log2(math.e)
LN2 = math.log(2.0)

A_Q = A_HEADS * HEAD_DIM
A_KV = A_KV_HEADS * HEAD_DIM
B_Q = B_HEADS * HEAD_DIM
B_KV = B_KV_HEADS * HEAD_DIM
OFF_QA, OFF_KA, OFF_VA = 0, A_Q, A_Q + A_KV
OFF_QB = A_Q + 2 * A_KV
OFF_KB = OFF_QB + B_Q
OFF_VB = OFF_KB + B_KV
ATT_IN = OFF_VB + B_KV

ADAM_LR = 0.001
ADAM_B1 = 0.9
ADAM_B2 = 0.999
ADAM_EPS = 1e-08
ADAM_WD = 0.01
ADAM_STEP = 10

N_DEV = 8
LANES = 128
V7X_VMEM_LIMIT = 56 * 1024 * 1024
FLAT_COLS = 1024


def _cparams(sem, vmem=V7X_VMEM_LIMIT):
    return pltpu.CompilerParams(dimension_semantics=sem, vmem_limit_bytes=vmem)


def _dot_nn(a, b):
    return lax.dot_general(a, b, (((1,), (0,)), ((), ())), preferred_element_type=F32)


def _dot_nt(a, b):
    return lax.dot_general(a, b, (((1,), (1,)), ((), ())), preferred_element_type=F32)


def _dot_tn(a, b):
    return lax.dot_general(a, b, (((0,), (0,)), ((), ())), preferred_element_type=F32)


def _bf(x):
    return x if x.dtype == BF16 else x.astype(BF16)


def _lane(shape):
    return lax.broadcasted_iota(jnp.int32, shape, len(shape) - 1)


def _seg_matrix(rows_lo, rows_hi):
    r = lax.broadcasted_iota(jnp.int32, (LANES, LANES), 0)
    return jnp.where((r >= rows_lo) & (r < rows_hi), 1.0, 0.0).astype(BF16)


def _group_matrix(width):
    r = lax.broadcasted_iota(jnp.int32, (LANES, LANES), 0)
    c = lax.broadcasted_iota(jnp.int32, (LANES, LANES), 1)
    return jnp.where((r // width) == (c // width), 1.0, 0.0).astype(BF16)


def _dot_f32_by_ones(s, ones_bf16):
    hi = s.astype(BF16)
    lo = (s - hi.astype(F32)).astype(BF16)
    return _dot_nn(hi, ones_bf16) + _dot_nn(lo, ones_bf16)


def _swap_halves(x, width):
    half = width // 2
    first = (_lane(x.shape) % width) < half
    return jnp.where(first, pltpu.roll(x, LANES - half, 1), pltpu.roll(x, half, 1))


def rmsnorm_fwd(x, g, name):
    t, d = x.shape
    tm = min(t, 512)

    def body(x_ref, g_ref, h_ref):
        xf = x_ref[...]
        r = lax.rsqrt(jnp.mean(xf * xf, axis=-1, keepdims=True) + EPS)
        h_ref[...] = (xf * r * g_ref[...]).astype(BF16)

    return pl.pallas_call(
        body, name=name, grid=(t // tm,),
        in_specs=[pl.BlockSpec((tm, d), lambda i: (i, 0)), pl.BlockSpec((1, d), lambda i: (0, 0))],
        out_specs=pl.BlockSpec((tm, d), lambda i: (i, 0)),
        out_shape=jax.ShapeDtypeStruct((t, d), BF16),
        compiler_params=_cparams(("parallel",)),
    )(x, g.reshape(1, d))


def _fit(n, want):
    t = min(n, want)
    while n % t:
        t //= 2
    return t


class Gathered:
    def __init__(self, arr, kind, layer):
        self.arr, self.kind, self.layer = arr, kind, layer
        _, _, self.rows, self.cols = arr.shape
        self.shape = (N_DEV * self.rows, self.cols) if kind == "row" else (self.rows, N_DEV * self.cols)


def _b_operand(b, mode, tn, tk, idx):
    dot = {"nn": _dot_nn, "nt": _dot_nt, "tn": _dot_tn}[mode]
    if not isinstance(b, Gathered):
        if mode == "nt":
            spec = pl.BlockSpec((tn, tk), lambda *g: idx(*g))
        else:
            spec = pl.BlockSpec((tk, tn), lambda *g: idx(*g)[::-1])
        return b, spec, lambda av, ref: dot(av, _bf(ref[...]))
    lay, rows, cols = b.layer, b.rows, b.cols
    if mode == "nn" and b.kind == "col":
        s = tn // cols
        assert s * cols == tn
        spec = pl.BlockSpec((s, None, tk, cols), lambda *g: (idx(*g)[0], lay, idx(*g)[1], 0))
        return b.arr, spec, lambda av, ref: jnp.concatenate([_dot_nn(av, ref[c]) for c in range(s)], axis=1)
    if mode == "nn" and b.kind == "row":
        s = tk // rows
        assert s * rows == tk
        spec = pl.BlockSpec((s, None, rows, tn), lambda *g: (idx(*g)[1], lay, 0, idx(*g)[0]))
        return b.arr, spec, lambda av, ref: _dot_nn(av, ref[...].reshape(s * rows, tn))
    if mode == "nt" and b.kind == "row":
        s = tn // rows
        assert s * rows == tn
        spec = pl.BlockSpec((s, None, rows, tk), lambda *g: (idx(*g)[0], lay, 0, idx(*g)[1]))
        return b.arr, spec, lambda av, ref: _dot_nt(av, ref[...].reshape(s * rows, tk))
    if mode == "nt" and b.kind == "col":
        s = tk // cols
        assert s * cols == tk
        spec = pl.BlockSpec((s, None, tn, cols), lambda *g: (idx(*g)[1], lay, idx(*g)[0], 0))

        def prod(av, ref):
            tot = _dot_nt(av[:, :cols], ref[0])
            for c in range(1, s):
                tot = tot + _dot_nt(av[:, c * cols:(c + 1) * cols], ref[c])
            return tot

        return b.arr, spec, prod
    raise NotImplementedError((mode, b.kind))


def matmul(a, b, mode, name, out_dtypes, epilogue=None, extras=(), a_fn=None, out_shards=False, tm=1024, tn=1024, tk=1024):
    (m, k) = a.shape[::-1] if mode == "tn" else a.shape
    n = b.shape[0] if mode == "nt" else b.shape[1]
    if out_shards:
        tn = n // N_DEV
    tm, tn, tk = _fit(m, tm), _fit(n, tn), _fit(k, tk)
    nk = k // tk
    n_ex, n_out = len(extras), len(out_dtypes)
    if epilogue is None:
        epilogue = lambda acc: (acc,)
    b_arr, b_spec, prod = _b_operand(b, mode, tn, tk, lambda i, j, kk: (j, kk))

    def body(*refs):
        a_ref, b_ref = refs[0], refs[1]
        ex_refs = refs[2:2 + n_ex]
        out_refs = refs[2 + n_ex:2 + n_ex + n_out]
        acc_ref = refs[2 + n_ex + n_out] if nk > 1 else None
        kk = pl.program_id(2)
        av = _bf(a_ref[...])
        if a_fn is not None:
            av = a_fn(av)
        part = prod(av, b_ref)

        def finish(acc):
            outs = epilogue(acc, *[r[...] for r in ex_refs])
            for r, o in zip(out_refs, outs):
                r[...] = o.astype(r.dtype)

        if nk == 1:
            finish(part)
            return

        @pl.when(kk == 0)
        def _():
            acc_ref[...] = part

        @pl.when(kk > 0)
        def _():
            acc_ref[...] += part

        @pl.when(kk == nk - 1)
        def _():
            finish(acc_ref[...])

    if mode == "tn":
        a_spec = pl.BlockSpec((tk, tm), lambda i, j, kk: (kk, i))
    else:
        a_spec = pl.BlockSpec((tm, tk), lambda i, j, kk: (i, kk))
    mn_spec = pl.BlockSpec((tm, tn), lambda i, j, kk: (i, j))
    row_spec = pl.BlockSpec((1, tn), lambda i, j, kk: (0, j))
    if out_shards:
        out_spec = pl.BlockSpec((None, tm, tn), lambda i, j, kk: (j, i, 0))
        out_shape = [jax.ShapeDtypeStruct((N_DEV, m, tn), dt) for dt in out_dtypes]
    else:
        out_spec = mn_spec
        out_shape = [jax.ShapeDtypeStruct((m, n), dt) for dt in out_dtypes]
    outs = pl.pallas_call(
        body, name=name, grid=(m // tm, n // tn, nk),
        in_specs=[a_spec, b_spec] + [row_spec if e.shape[0] == 1 else mn_spec for e in extras],
        out_specs=[out_spec] * n_out,
        out_shape=out_shape,
        scratch_shapes=[pltpu.VMEM((tm, tn), F32)] if nk > 1 else [],
        compiler_params=_cparams(("parallel", "parallel", "arbitrary")),
    )(a, b_arr, *extras)
    return outs


def matmul_nt_normbwd(dz, w, x, g, dres, name, tm=512):
    m, k = dz.shape
    d = w.shape[0]
    tm = _fit(m, tm)
    w_arr, w_spec, prod = _b_operand(w, "nt", d, k, lambda i: (0, 0))

    def body(dz_ref, w_ref, x_ref, g_ref, dres_ref, dx_ref, dxb_ref, dg_ref):
        @pl.when(pl.program_id(0) == 0)
        def _():
            dg_ref[...] = jnp.zeros_like(dg_ref)

        dh = prod(_bf(dz_ref[...]), w_ref)
        xf = x_ref[...]
        r = lax.rsqrt(jnp.mean(xf * xf, axis=-1, keepdims=True) + EPS)
        xhat = xf * r
        dg_ref[...] += jnp.sum(dh * xhat, axis=0, keepdims=True)
        dxh = dh * g_ref[...]
        dx = r * (dxh - xhat * jnp.mean(dxh * xhat, axis=-1, keepdims=True))
        out = dres_ref[...] + dx
        dx_ref[...] = out
        dxb_ref[...] = out.astype(BF16)

    row = pl.BlockSpec((tm, d), lambda i: (i, 0))
    vec = pl.BlockSpec((1, d), lambda i: (0, 0))
    return pl.pallas_call(
        body, name=name, grid=(m // tm,),
        in_specs=[pl.BlockSpec((tm, k), lambda i: (i, 0)), w_spec, row, vec, row],
        out_specs=[row, row, vec],
        out_shape=[jax.ShapeDtypeStruct((m, d), F32), jax.ShapeDtypeStruct((m, d), BF16), jax.ShapeDtypeStruct((1, d), F32)],
        compiler_params=_cparams(("arbitrary",)),
    )(dz, w_arr, x, g.reshape(1, d), dres)


def _rope_tables(t):
    pos = jnp.arange(t)

    def angles(p, dim):
        freqs = ROPE_THETA ** (-jnp.arange(0, dim, 2, dtype=F32) / dim)
        return p.astype(F32)[:, None] * freqs[None, :]

    a1 = angles(pos, HEAD_DIM)
    cos_a = jnp.concatenate([jnp.cos(a1), jnp.cos(a1)], axis=-1)
    sin_a = jnp.concatenate([-jnp.sin(a1), jnp.sin(a1)], axis=-1)
    ar = angles(pos // GRID_W, HEAD_DIM // 2)
    ac = angles(pos % GRID_W, HEAD_DIM // 2)
    cos_b = jnp.concatenate([jnp.cos(ar), jnp.cos(ar), jnp.cos(ac), jnp.cos(ac)], axis=-1)
    sin_b = jnp.concatenate([-jnp.sin(ar), jnp.sin(ar), -jnp.sin(ac), jnp.sin(ac)], axis=-1)
    two = lambda z: jnp.concatenate([z, z], axis=-1)
    return two(cos_a), two(sin_a), two(cos_b), two(sin_b)


def _headnorm(xs, gmat):
    return lax.rsqrt(_dot_f32_by_ones(xs * xs, gmat) * (1.0 / HEAD_DIM) + EPS)


def qkv_post_fwd(proj, tables, qn_g, kn_g, name):
    t = proj.shape[0]
    tm = min(t, 256)
    cos_a, sin_a, cos_b, sin_b = tables
    g2 = lambda g: jnp.concatenate([g, g]).reshape(1, LANES)

    def body(p_ref, ca_ref, sa_ref, cb_ref, sb_ref, qg_ref, kg_ref, qa_ref, ka_ref, va_ref, qb_ref, kb_ref, vb_ref):
        ca, sa, cb, sb = ca_ref[...], sa_ref[...], cb_ref[...], sb_ref[...]
        gmat = _group_matrix(HEAD_DIM)

        def rope_a(xs):
            return xs * ca + _swap_halves(xs, HEAD_DIM) * sa

        def norm_rope_b(xs, g):
            y = xs * _headnorm(xs, gmat) * g
            return y * cb + _swap_halves(y, HEAD_DIM // 2) * sb

        for c in range(A_Q // LANES):
            qa_ref[:, c * LANES:(c + 1) * LANES] = rope_a(p_ref[:, OFF_QA + c * LANES:OFF_QA + (c + 1) * LANES]).astype(BF16)
        ka_ref[...] = rope_a(p_ref[:, OFF_KA:OFF_KA + LANES]).astype(BF16)
        va_ref[...] = p_ref[:, OFF_VA:OFF_VA + LANES].astype(BF16)
        for c in range(B_Q // LANES):
            qb_ref[:, c * LANES:(c + 1) * LANES] = norm_rope_b(
                p_ref[:, OFF_QB + c * LANES:OFF_QB + (c + 1) * LANES], qg_ref[...]).astype(BF16)
        kb_ref[...] = norm_rope_b(p_ref[:, OFF_KB:OFF_KB + LANES], kg_ref[...]).astype(BF16)
        vb_ref[...] = p_ref[:, OFF_VB:OFF_VB + LANES].astype(BF16)

    tab = pl.BlockSpec((tm, LANES), lambda i: (i, 0))
    vec = pl.BlockSpec((1, LANES), lambda i: (0, 0))
    wide = pl.BlockSpec((tm, A_Q), lambda i: (i, 0))
    return pl.pallas_call(
        body, name=name, grid=(t // tm,),
        in_specs=[pl.BlockSpec((tm, ATT_IN), lambda i: (i, 0)), tab, tab, tab, tab, vec, vec],
        out_specs=[wide, tab, tab, wide, tab, tab],
        out_shape=[jax.ShapeDtypeStruct((t, A_Q), BF16), jax.ShapeDtypeStruct((t, LANES), BF16),
                   jax.ShapeDtypeStruct((t, LANES), BF16), jax.ShapeDtypeStruct((t, B_Q), BF16),
                   jax.ShapeDtypeStruct((t, LANES), BF16), jax.ShapeDtypeStruct((t, LANES), BF16)],
        compiler_params=_cparams(("parallel",)),
    )(proj, cos_a, sin_a, cos_b, sin_b, g2(qn_g), g2(kn_g))


def qkv_post_bwd(proj, tables, qn_g, kn_g, dqa, dka, dva, dqb, dkb, dvb, name):
    t = proj.shape[0]
    tm = min(t, 256)
    cos_a, sin_a, cos_b, sin_b = tables
    g2 = lambda g: jnp.concatenate([g, g]).reshape(1, LANES)

    def body(p_ref, ca_ref, sa_ref, cb_ref, sb_ref, qg_ref, kg_ref, dqa_ref, dka_ref, dva_ref, dqb_ref, dkb_ref, dvb_ref,
             dp_ref, dqg_ref, dkg_ref):
        ca, sa, cb, sb = ca_ref[...], sa_ref[...], cb_ref[...], sb_ref[...]
        gmat = _group_matrix(HEAD_DIM)

        @pl.when(pl.program_id(0) == 0)
        def _():
            dqg_ref[...] = jnp.zeros_like(dqg_ref)
            dkg_ref[...] = jnp.zeros_like(dkg_ref)

        def rope_a_bwd(dy):
            return dy * ca + _swap_halves(dy * sa, HEAD_DIM)

        def norm_rope_b_bwd(dout, xs, g):
            dy = dout * cb + _swap_halves(dout * sb, HEAD_DIM // 2)
            r = _headnorm(xs, gmat)
            xhat = xs * r
            dxh = dy * g
            mean = _dot_f32_by_ones(dxh * xhat, gmat) * (1.0 / HEAD_DIM)
            return r * (dxh - xhat * mean), jnp.sum(dy * xhat, axis=0, keepdims=True)

        for c in range(A_Q // LANES):
            sl = slice(c * LANES, (c + 1) * LANES)
            dp_ref[:, OFF_QA + c * LANES:OFF_QA + (c + 1) * LANES] = rope_a_bwd(dqa_ref[:, sl].astype(F32)).astype(BF16)
        dp_ref[:, OFF_KA:OFF_KA + LANES] = rope_a_bwd(dka_ref[0] + dka_ref[1]).astype(BF16)
        dp_ref[:, OFF_VA:OFF_VA + LANES] = (dva_ref[0] + dva_ref[1]).astype(BF16)
        dqg = jnp.zeros((1, LANES), F32)
        for c in range(B_Q // LANES):
            sl = slice(c * LANES, (c + 1) * LANES)
            dx, dg = norm_rope_b_bwd(dqb_ref[:, sl].astype(F32), p_ref[:, OFF_QB + c * LANES:OFF_QB + (c + 1) * LANES], qg_ref[...])
            dp_ref[:, OFF_QB + c * LANES:OFF_QB + (c + 1) * LANES] = dx.astype(BF16)
            dqg = dqg + dg
        dqg_ref[...] += dqg
        dx, dg = norm_rope_b_bwd((dkb_ref[0] + dkb_ref[1]).T, p_ref[:, OFF_KB:OFF_KB + LANES], kg_ref[...])
        dp_ref[:, OFF_KB:OFF_KB + LANES] = dx.astype(BF16)
        dkg_ref[...] += dg
        dp_ref[:, OFF_VB:OFF_VB + LANES] = (dvb_ref[0] + dvb_ref[1]).T.astype(BF16)

        @pl.when(pl.program_id(0) == t // tm - 1)
        def _():
            dqg_ref[...] = dqg_ref[...] + pltpu.roll(dqg_ref[...], HEAD_DIM, 1)
            dkg_ref[...] = dkg_ref[...] + pltpu.roll(dkg_ref[...], HEAD_DIM, 1)

    tab = pl.BlockSpec((tm, LANES), lambda i: (i, 0))
    vec = pl.BlockSpec((1, LANES), lambda i: (0, 0))
    wide = pl.BlockSpec((tm, A_Q), lambda i: (i, 0))
    slab = pl.BlockSpec((2, tm, LANES), lambda i: (0, i, 0))
    per_chunk = dkb.shape[3] // tm
    slab_t = pl.BlockSpec((2, None, LANES, tm), lambda i: (0, i // per_chunk, 0, i % per_chunk))
    full = pl.BlockSpec((tm, ATT_IN), lambda i: (i, 0))
    return pl.pallas_call(
        body, name=name, grid=(t // tm,),
        in_specs=[full, tab, tab, tab, tab, vec, vec, wide, slab, slab, wide, slab_t, slab_t],
        out_specs=[full, vec, vec],
        out_shape=[jax.ShapeDtypeStruct((t, ATT_IN), BF16), jax.ShapeDtypeStruct((1, LANES), F32),
                   jax.ShapeDtypeStruct((1, LANES), F32)],
        compiler_params=_cparams(("arbitrary",)),
    )(proj, cos_a, sin_a, cos_b, sin_b, g2(qn_g), g2(kn_g), dqa, dka, dva, dqb, dkb, dvb)


def _head_to_half(xs, head_half, kv_half):
    low = _lane(xs.shape) < HEAD_DIM
    kept = jnp.where(low if head_half == 0 else jnp.logical_not(low), xs, 0.0)
    return jnp.where(kv_half == head_half, kept, pltpu.roll(kept, HEAD_DIM, 1))


def _halves_to_heads(r0, r1, kv_half):
    low = _lane(r0.shape) < HEAD_DIM
    a = jnp.where(kv_half == 0, r0, pltpu.roll(r0, HEAD_DIM, 1))
    b = jnp.where(kv_half == 1, r1, pltpu.roll(r1, HEAD_DIM, 1))
    return jnp.where(low, a, b)


def attn_delta(o, do, name):
    t, w = o.shape
    tm = min(t, 512)
    n_heads = w // HEAD_DIM

    def body(o_ref, do_ref, d_ref):
        lo, hi = _seg_matrix(0, HEAD_DIM), _seg_matrix(HEAD_DIM, LANES)
        for c in range(w // LANES):
            sl = slice(c * LANES, (c + 1) * LANES)
            s = o_ref[:, sl].astype(F32) * do_ref[:, sl].astype(F32)
            d_ref[2 * c] = _dot_f32_by_ones(s, lo)
            d_ref[2 * c + 1] = _dot_f32_by_ones(s, hi)

    blk = pl.BlockSpec((tm, w), lambda i: (i, 0))
    return pl.pallas_call(
        body, name=name, grid=(t // tm,),
        in_specs=[blk, blk],
        out_specs=pl.BlockSpec((n_heads, tm, LANES), lambda i: (0, i, 0)),
        out_shape=jax.ShapeDtypeStruct((n_heads, t, LANES), F32),
        compiler_params=_cparams(("parallel",)),
    )(o, do)


BAND = 3 * BLOCK


def _band(n, t, rows_rep):
    start = pl.multiple_of(jnp.clip((n - 1) * BLOCK, 0, t - BAND), BLOCK)
    qi = lax.broadcasted_iota(jnp.int32, (BLOCK, BAND), 0)
    kj = lax.broadcasted_iota(jnp.int32, (BLOCK, BAND), 1)
    ok = jnp.abs((start + kj) - (n * BLOCK + qi)) <= WINDOW
    return start, jnp.concatenate([ok] * rows_rep, axis=0)


def window_attn_fwd(q, k, v, sink, name, blocks_per_step=8):
    t = q.shape[0]
    assert t >= BAND
    nq = _fit(t // BLOCK, blocks_per_step)
    tq = nq * BLOCK

    def body(sink_ref, q_ref, k_ref, v_ref, o_ref, lse_ref):
        j, n0 = pl.program_id(0), pl.program_id(1)
        kvh = j // 2
        row = lax.broadcasted_iota(jnp.int32, (2 * BLOCK, 1), 0)
        sk = jnp.where(row < BLOCK, sink_ref[2 * j], sink_ref[2 * j + 1])
        for u in range(nq):
            rows = slice(u * BLOCK, (u + 1) * BLOCK)
            start, ok = _band(n0 * nq + u, t, 2)
            qf = q_ref[rows, :].astype(F32) * SCALE
            qs = jnp.concatenate([_head_to_half(qf, 0, kvh), _head_to_half(qf, 1, kvh)], axis=0).astype(BF16)
            s = jnp.where(ok, _dot_nt(qs, k_ref[pl.ds(start, BAND), :]), NEG)
            m = jnp.maximum(jnp.max(s, axis=-1, keepdims=True), sk)
            p = jnp.exp(s - m)
            denom = jnp.sum(p, axis=-1, keepdims=True) + jnp.exp(sk - m)
            o = _dot_nn(p.astype(BF16), v_ref[pl.ds(start, BAND), :]) / denom
            o_ref[rows, :] = _halves_to_heads(o[:BLOCK], o[BLOCK:], kvh).astype(BF16)
            lse = jnp.broadcast_to(m + jnp.log(denom), (2 * BLOCK, LANES))
            lse_ref[0, rows, :] = lse[:BLOCK]
            lse_ref[1, rows, :] = lse[BLOCK:]

    qspec = pl.BlockSpec((tq, LANES), lambda j, n: (n, j))
    whole = pl.BlockSpec((t, LANES), lambda j, n: (0, 0))
    return pl.pallas_call(
        body, name=name, grid=(A_HEADS // 2, t // tq),
        in_specs=[pl.BlockSpec(memory_space=pltpu.SMEM), qspec, whole, whole],
        out_specs=[qspec, pl.BlockSpec((2, tq, LANES), lambda j, n: (j, n, 0))],
        out_shape=[jax.ShapeDtypeStruct((t, A_Q + B_Q), BF16), jax.ShapeDtypeStruct((A_HEADS, t, LANES), F32)],
        compiler_params=_cparams(("parallel", "parallel")),
    )(sink, q, k, v)


def window_attn_bwd(q, k, v, sink, do, lse, delta, name, blocks_per_step=4):
    t = q.shape[0]
    assert t >= BAND
    nq = _fit(t // BLOCK, blocks_per_step)
    tq = nq * BLOCK
    grp = A_HEADS // A_KV_HEADS
    gw = grp * HEAD_DIM

    def body(sink_ref, q_ref, do_ref, k_ref, v_ref, lse_ref, dl_ref, dq_ref, dk_ref, dv_ref, ds_ref):
        kvh, n0 = pl.program_id(0), pl.program_id(1)

        @pl.when(n0 == 0)
        def _():
            dk_ref[...] = jnp.zeros_like(dk_ref)
            dv_ref[...] = jnp.zeros_like(dv_ref)
            ds_ref[...] = jnp.zeros_like(ds_ref)

        rid = lax.broadcasted_iota(jnp.int32, (8, LANES), 0)
        upd = jnp.zeros((8, LANES), F32)
        for u in range(nq):
            rows = slice(u * BLOCK, (u + 1) * BLOCK)
            start, ok = _band(n0 * nq + u, t, grp)
            band = pl.ds(start, BAND)
            qparts, doparts = [], []
            for hh in range(grp):
                sl = slice((hh // 2) * LANES, (hh // 2 + 1) * LANES)
                qparts.append(_head_to_half(q_ref[rows, sl].astype(F32) * SCALE, hh % 2, kvh))
                doparts.append(_head_to_half(do_ref[rows, sl].astype(F32), hh % 2, kvh))
            qs = jnp.concatenate(qparts, axis=0).astype(BF16)
            dos = jnp.concatenate(doparts, axis=0).astype(BF16)
            lse_b = jnp.concatenate([lse_ref[hh, rows, :] for hh in range(grp)], axis=0)
            dl_b = jnp.concatenate([dl_ref[hh, rows, :] for hh in range(grp)], axis=0)
            kband, vband = k_ref[band, :], v_ref[band, :]
            s = jnp.where(ok, _dot_nt(qs, kband), NEG)
            p = jnp.exp(s - lse_b[:, :1])
            dp = _dot_nt(dos, vband)
            dsc = (p * (dp - dl_b[:, :1])).astype(BF16)
            dv_ref[0, band, :] += _dot_tn(p.astype(BF16), dos)
            dk_ref[0, band, :] += _dot_tn(dsc, qs)
            dq = _dot_nn(dsc, kband) * SCALE
            for c in range(grp // 2):
                dq_ref[rows, c * LANES:(c + 1) * LANES] = _halves_to_heads(
                    dq[2 * c * BLOCK:(2 * c + 1) * BLOCK], dq[(2 * c + 1) * BLOCK:(2 * c + 2) * BLOCK], kvh).astype(dq_ref.dtype)
            for hh in range(grp):
                rs = slice(hh * BLOCK, (hh + 1) * BLOCK)
                tot = jnp.sum(jnp.exp(sink_ref[kvh * grp + hh] - lse_b[rs]) * dl_b[rs], axis=0, keepdims=True)
                upd = upd + jnp.where(rid == hh, -tot, 0.0)
        ds_ref[0] += upd

    qspec = pl.BlockSpec((tq, gw), lambda kvh, n: (n, kvh))
    whole = pl.BlockSpec((t, LANES), lambda kvh, n: (0, 0))
    stat = pl.BlockSpec((grp, tq, LANES), lambda kvh, n: (kvh, n, 0))
    slab = pl.BlockSpec((1, t, LANES), lambda kvh, n: (kvh, 0, 0))
    return pl.pallas_call(
        body, name=name, grid=(A_KV_HEADS, t // tq),
        in_specs=[pl.BlockSpec(memory_space=pltpu.SMEM), qspec, qspec, whole, whole, stat, stat],
        out_specs=[qspec, slab, slab, pl.BlockSpec((1, 8, LANES), lambda kvh, n: (kvh, 0, 0))],
        out_shape=[jax.ShapeDtypeStruct((t, A_Q), BF16), jax.ShapeDtypeStruct((A_KV_HEADS, t, LANES), F32),
                   jax.ShapeDtypeStruct((A_KV_HEADS, t, LANES), F32), jax.ShapeDtypeStruct((A_KV_HEADS, 8, LANES), F32)],
        compiler_params=_cparams(("arbitrary", "arbitrary")),
    )(sink, q, do, k, v, lse, delta)


def flash_attn_fwd(q, k, v, cat, name, exchange=None, tq=256, tk=2048):
    t = q.shape[0]
    tq, tk = _fit(t, tq), _fit(t, tk)
    nk = t // tk

    def body(q_ref, k_ref, v_ref, cat_ref, o_ref, lse_ref):
        del cat_ref
        kvh = pl.program_id(0) // 2
        qf = q_ref[...].astype(F32) * (SCALE * LOG2E)
        qs = jnp.concatenate([_head_to_half(qf, 0, kvh), _head_to_half(qf, 1, kvh)], axis=0).astype(BF16)
        mine = (_lane((tk, LANES)) < HEAD_DIM) == (kvh == 0)

        def scores(c):
            return _dot_nt(qs, k_ref[c * tk:(c + 1) * tk, :])

        s = scores(0)
        m = jnp.full((2 * tq, 1), NEG, F32)
        acc = jnp.zeros((2 * tq, LANES), F32)
        for c in range(nk):
            s_next = scores(c + 1) if c + 1 < nk else None
            vb = jnp.where(mine, v_ref[c * tk:(c + 1) * tk, :], jnp.ones((), BF16))
            m_new = jnp.maximum(m, jnp.max(s, axis=-1, keepdims=True))
            p = jnp.exp2(s - m_new).astype(BF16)
            acc = jnp.exp2(m - m_new) * acc + _dot_nn(p, vb)
            m, s = m_new, s_next
        other = pltpu.roll(acc, HEAD_DIM, 1)
        o = acc / other
        o_ref[...] = _halves_to_heads(o[:tq], o[tq:], kvh).astype(BF16)
        in_mine = (_lane(acc.shape) < HEAD_DIM) == (kvh == 0)
        lse = jnp.broadcast_to(m, acc.shape) + jnp.log2(jnp.where(in_mine, other, acc))
        lse_ref[0] = lse[:tq]
        lse_ref[1] = lse[tq:]

    qspec = pl.BlockSpec((tq, LANES), lambda j, i: (i, j))
    whole = pl.BlockSpec((t, LANES), lambda j, i: (0, 0))
    nj, ni = B_HEADS // 2, t // tq
    steps = lambda: ((pl.program_id(0) == 0) & (pl.program_id(1) == 0), (pl.program_id(0) == nj - 1) & (pl.program_id(1) == ni - 1))
    body, x_in, x_out, x_shapes, x_scratch = carried(body, exchange, 4, 2, steps)
    return pl.pallas_call(
        body, name=name, grid=(nj, ni),
        in_specs=[qspec, whole, whole, _ANY] + x_in,
        out_specs=[pl.BlockSpec((tq, LANES), lambda j, i: (i, A_Q // LANES + j)),
                   pl.BlockSpec((2, tq, LANES), lambda j, i: (j, i, 0))] + x_out,
        out_shape=[jax.ShapeDtypeStruct(cat.shape, BF16), jax.ShapeDtypeStruct((B_HEADS, t, LANES), F32)] + x_shapes,
        scratch_shapes=x_scratch,
        input_output_aliases={3: 0},
        compiler_params=_cparams(("arbitrary", "arbitrary")),
    )(q, k, v, cat, *(exchange.arrays if exchange else ()))


def flash_attn_bwd(q, k, v, do, lse, delta, name, exchange=None, tq=256, tk=512):
    t = q.shape[0]
    tq, tk = _fit(t, tq), _fit(t, tk)
    nk = t // tk
    grp = B_HEADS // B_KV_HEADS
    gw = grp * HEAD_DIM

    def body(q_ref, do_ref, k_ref, v_ref, lse_ref, dl_ref, dq_ref, dk_ref, dv_ref, dq_s):
        kvh, i = pl.program_id(0), pl.program_id(1)

        @pl.when(i == 0)
        def _():
            dk_ref[...] = jnp.zeros_like(dk_ref)
            dv_ref[...] = jnp.zeros_like(dv_ref)

        qparts, doparts = [], []
        for hh in range(grp):
            sl = slice((hh // 2) * LANES, (hh // 2 + 1) * LANES)
            qparts.append(_head_to_half(q_ref[:, sl].astype(F32) * (SCALE * LOG2E), hh % 2, kvh))
            doparts.append(_head_to_half(do_ref[:, sl].astype(F32), hh % 2, kvh))
        qf, dof = jnp.concatenate(qparts, axis=0), jnp.concatenate(doparts, axis=0)
        qs, dos = qf.astype(BF16), dof.astype(BF16)
        qs_t, dos_t = qf.T.astype(BF16), dof.T.astype(BF16)
        lse = jnp.concatenate([lse_ref[hh][:, :1] for hh in range(grp)], axis=0)
        dl = jnp.concatenate([dl_ref[hh][:, :1] for hh in range(grp)], axis=0)
        dq_s[...] = jnp.zeros_like(dq_s)

        def chunk(c, carry):
            rows = pl.ds(pl.multiple_of(c * tk, tk), tk)
            kb, vb = k_ref[rows, :], v_ref[rows, :]
            p = jnp.exp2(_dot_nt(qs, kb) - lse)
            dp = _dot_nt(dos, vb)
            dsc = (p * (dp - dl)).astype(BF16)
            dv_ref[0, c] += _dot_nn(dos_t, p.astype(BF16))
            dk_ref[0, c] += _dot_nn(qs_t, dsc) * LN2
            dq_s[...] += _dot_nn(dsc, kb)
            return carry

        lax.fori_loop(0, nk, chunk, 0)
        for c in range(grp // 2):
            dq_ref[:, c * LANES:(c + 1) * LANES] = (_halves_to_heads(
                dq_s[2 * c * tq:(2 * c + 1) * tq], dq_s[(2 * c + 1) * tq:(2 * c + 2) * tq], kvh) * SCALE).astype(dq_ref.dtype)

    qspec = pl.BlockSpec((tq, gw), lambda kvh, i: (i, kvh))
    dospec = pl.BlockSpec((tq, gw), lambda kvh, i: (i, A_Q // gw + kvh))
    whole = pl.BlockSpec((t, LANES), lambda kvh, i: (0, 0))
    stat = pl.BlockSpec((grp, tq, LANES), lambda kvh, i: (kvh, i, 0))
    dlstat = pl.BlockSpec((grp, tq, LANES), lambda kvh, i: (A_HEADS // grp + kvh, i, 0))
    slab = pl.BlockSpec((1, nk, LANES, tk), lambda kvh, i: (kvh, 0, 0, 0))
    ni = t // tq
    steps = lambda: ((pl.program_id(0) == 0) & (pl.program_id(1) == 0),
                     (pl.program_id(0) == B_KV_HEADS - 1) & (pl.program_id(1) == ni - 1))
    body, x_in, x_out, x_shapes, x_scratch = carried(body, exchange, 6, 3, steps)
    return pl.pallas_call(
        body, name=name, grid=(B_KV_HEADS, ni),
        in_specs=[qspec, dospec, whole, whole, stat, dlstat] + x_in,
        out_specs=[qspec, slab, slab] + x_out,
        out_shape=[jax.ShapeDtypeStruct((t, B_Q), BF16), jax.ShapeDtypeStruct((B_KV_HEADS, nk, LANES, tk), F32),
                   jax.ShapeDtypeStruct((B_KV_HEADS, nk, LANES, tk), F32)] + x_shapes,
        scratch_shapes=[pltpu.VMEM((grp * tq, LANES), F32)] + x_scratch,
        compiler_params=_cparams(("arbitrary", "arbitrary")),
    )(q, do, k, v, lse, delta, *(exchange.arrays if exchange else ()))


_GELU_C = math.sqrt(2.0 / math.pi)
_GELU_A = 0.044715


def _gelu(x):
    return 0.5 * x * (1.0 + jnp.tanh(_GELU_C * (x + _GELU_A * x * x * x)))


def _gelu_grad(x):
    th = jnp.tanh(_GELU_C * (x + _GELU_A * x * x * x))
    return 0.5 * (1.0 + th) + 0.5 * x * (1.0 - th * th) * _GELU_C * (1.0 + 3.0 * _GELU_A * x * x)


def _layernorm_stats(vf):
    mu = jnp.mean(vf, axis=-1, keepdims=True)
    vc = vf - mu
    r = lax.rsqrt(jnp.mean(vc * vc, axis=-1, keepdims=True) + EPS)
    return vc * r, r


def sgu_mix_fwd(z, ln_g, ln_b, w_s, b_rows, name):
    t, w2 = z.shape
    w = w2 // 2
    dg = w // SGU_GROUPS

    def body(u_ref, v_ref, g_ref, b_ref, ws_ref, bb_ref, y_ref):
        vhat, _ = _layernorm_stats(v_ref[...].astype(F32))
        vn = (vhat * g_ref[...] + b_ref[...]).astype(BF16)
        for g in range(SGU_GROUPS):
            sl = slice(g * dg, (g + 1) * dg)
            mixed = _dot_nn(ws_ref[g], vn[:, sl]) + bb_ref[g]
            y_ref[:, sl] = (u_ref[:, sl].astype(F32) * mixed).astype(BF16)

    vec = pl.BlockSpec((1, w), lambda n: (0, 0))
    whole = pl.BlockSpec((SGU_GROUPS, SGU_CHUNK, SGU_CHUNK), lambda n: (0, 0, 0))
    return pl.pallas_call(
        body, name=name, grid=(t // SGU_CHUNK,),
        in_specs=[pl.BlockSpec((SGU_CHUNK, w), lambda n: (n, 0)), pl.BlockSpec((SGU_CHUNK, w), lambda n: (n, 1)),
                  vec, vec, whole, whole],
        out_specs=pl.BlockSpec((SGU_CHUNK, w), lambda n: (n, 0)),
        out_shape=jax.ShapeDtypeStruct((t, w), BF16),
        compiler_params=_cparams(("parallel",)),
    )(z, z, ln_g.reshape(1, w), ln_b.reshape(1, w), w_s, b_rows)


def sgu_mix_bwd(z, apre, dy, ln_g, ln_b, w_s, b_rows, name):
    t, w2 = z.shape
    w = w2 // 2
    dg = w // SGU_GROUPS

    def body(u_ref, v_ref, au_ref, av_ref, dy_ref, g_ref, b_ref, ws_ref, bb_ref, da_ref, dlg_ref, dlb_ref, dws_ref, dbs_ref):
        @pl.when(pl.program_id(0) == 0)
        def _():
            dlg_ref[...] = jnp.zeros_like(dlg_ref)
            dlb_ref[...] = jnp.zeros_like(dlb_ref)
            dws_ref[...] = jnp.zeros_like(dws_ref)
            dbs_ref[...] = jnp.zeros_like(dbs_ref)

        vhat, r = _layernorm_stats(v_ref[...].astype(F32))
        gam = g_ref[...]
        vn = (vhat * gam + b_ref[...]).astype(BF16)
        ones8 = jnp.ones((8, dg), BF16)
        rid = lax.broadcasted_iota(jnp.int32, (8, SGU_CHUNK), 0)
        dbs = jnp.zeros((8, SGU_CHUNK), F32)
        dvn_parts = []
        for g in range(SGU_GROUPS):
            sl = slice(g * dg, (g + 1) * dg)
            dyg = dy_ref[:, sl].astype(F32)
            mixed = _dot_nn(ws_ref[g], vn[:, sl]) + bb_ref[g]
            da_ref[:, sl] = (dyg * mixed * _gelu_grad(au_ref[:, sl].astype(F32))).astype(BF16)
            dmix = dyg * u_ref[:, sl].astype(F32)
            dm_hi = dmix.astype(BF16)
            dm_lo = (dmix - dm_hi.astype(F32)).astype(BF16)
            dws_ref[g] += _dot_nt(dm_hi, vn[:, sl])
            dbs = dbs + jnp.where(rid == g, _dot_nt(ones8, dm_hi) + _dot_nt(ones8, dm_lo), 0.0)
            dvn_parts.append(_dot_tn(ws_ref[g], dm_hi))
        dbs_ref[...] += dbs
        dvn = jnp.concatenate(dvn_parts, axis=1)
        dlg_ref[...] += jnp.sum(dvn * vhat, axis=0, keepdims=True)
        dlb_ref[...] += jnp.sum(dvn, axis=0, keepdims=True)
        dvh = dvn * gam
        dv = r * (dvh - jnp.mean(dvh, axis=-1, keepdims=True) - vhat * jnp.mean(dvh * vhat, axis=-1, keepdims=True))
        da_ref[:, w:] = (dv * _gelu_grad(av_ref[...].astype(F32))).astype(BF16)

    vec = pl.BlockSpec((1, w), lambda n: (0, 0))
    whole = pl.BlockSpec((SGU_GROUPS, SGU_CHUNK, SGU_CHUNK), lambda n: (0, 0, 0))
    left = pl.BlockSpec((SGU_CHUNK, w), lambda n: (n, 0))
    right = pl.BlockSpec((SGU_CHUNK, w), lambda n: (n, 1))
    return pl.pallas_call(
        body, name=name, grid=(t // SGU_CHUNK,),
        in_specs=[left, right, left, right, left, vec, vec, whole, whole],
        out_specs=[pl.BlockSpec((SGU_CHUNK, w2), lambda n: (n, 0)), vec, vec, whole,
                   pl.BlockSpec((SGU_GROUPS, SGU_CHUNK), lambda n: (0, 0))],
        out_shape=[jax.ShapeDtypeStruct((t, w2), BF16), jax.ShapeDtypeStruct((1, w), F32), jax.ShapeDtypeStruct((1, w), F32),
                   jax.ShapeDtypeStruct((SGU_GROUPS, SGU_CHUNK, SGU_CHUNK), F32),
                   jax.ShapeDtypeStruct((SGU_GROUPS, SGU_CHUNK), F32)],
        compiler_params=_cparams(("arbitrary",)),
    )(z, z, apre, apre, dy, ln_g.reshape(1, w), ln_b.reshape(1, w), w_s, b_rows)


def loss_head(h, g, target, name):
    t, d = h.shape
    tm = min(t, 512)

    def body(h_ref, g_ref, t_ref, loss_ref, dh_ref, dhb_ref, dg_ref):
        @pl.when(pl.program_id(0) == 0)
        def _():
            loss_ref[...] = jnp.zeros_like(loss_ref)
            dg_ref[...] = jnp.zeros_like(dg_ref)

        xf = h_ref[...]
        r = lax.rsqrt(jnp.mean(xf * xf, axis=-1, keepdims=True) + EPS)
        xhat = xf * r
        err = xhat * g_ref[...] - t_ref[...]
        per_tok = jnp.mean(err * err, axis=-1, keepdims=True)
        loss_ref[...] += 0.5 * jnp.sum(per_tok, axis=0, keepdims=True)
        dy = err * (1.0 / d)
        dg_ref[...] += jnp.sum(dy * xhat, axis=0, keepdims=True)
        dxh = dy * g_ref[...]
        dh = r * (dxh - xhat * jnp.mean(dxh * xhat, axis=-1, keepdims=True))
        dh_ref[...] = dh
        dhb_ref[...] = dh.astype(BF16)

    row = pl.BlockSpec((tm, d), lambda i: (i, 0))
    vec = pl.BlockSpec((1, d), lambda i: (0, 0))
    return pl.pallas_call(
        body, name=name, grid=(t // tm,),
        in_specs=[row, vec, row],
        out_specs=[pl.BlockSpec((1, LANES), lambda i: (0, 0)), row, row, vec],
        out_shape=[jax.ShapeDtypeStruct((1, LANES), F32), jax.ShapeDtypeStruct((t, d), F32), jax.ShapeDtypeStruct((t, d), BF16),
                   jax.ShapeDtypeStruct((1, d), F32)],
        compiler_params=_cparams(("arbitrary",)),
    )(h, g.reshape(1, d), target)


ADAMW_BLOCK_BYTES = 1 << 20


def adamw(parts, w, m, v, name):
    n_layers, r, c = w.shape
    row_bytes = n_layers * c * 4
    if r * row_bytes <= 2 * ADAMW_BLOCK_BYTES:
        tr = r
    else:
        tr = _fit(r, 1 << int(math.log2(max(8, ADAMW_BLOCK_BYTES // row_bytes))))
    bc1 = 1.0 - ADAM_B1 ** ADAM_STEP
    bc2 = 1.0 - ADAM_B2 ** ADAM_STEP

    def body(*refs):
        p_refs = refs[:n_layers]
        w_ref, m_ref, v_ref, g_ref, d_ref, nm_ref, nv_ref = refs[n_layers:]
        for l in range(n_layers):
            g = p_refs[l][0].astype(F32)
            for j in range(1, N_DEV):
                g = g + p_refs[l][j].astype(F32)
            nm = ADAM_B1 * m_ref[l] + (1.0 - ADAM_B1) * g
            nv = ADAM_B2 * v_ref[l] + (1.0 - ADAM_B2) * (g * g)
            g_ref[l] = g
            nm_ref[l] = nm
            nv_ref[l] = nv
            d_ref[l] = -ADAM_LR * ((nm / bc1) / (jnp.sqrt(nv / bc2) + ADAM_EPS) + ADAM_WD * w_ref[l])

    blk = pl.BlockSpec((n_layers, tr, c), lambda i: (0, i, 0))
    return pl.pallas_call(
        body, name=name, grid=(r // tr,),
        in_specs=[pl.BlockSpec((N_DEV, tr, c), lambda i: (0, i, 0))] * n_layers + [blk, blk, blk],
        out_specs=[blk] * 4,
        out_shape=[jax.ShapeDtypeStruct((n_layers, r, c), F32)] * 4,
        compiler_params=_cparams(("parallel",)),
    )(*parts, w, m, v)


_ANY = pl.BlockSpec(memory_space=pl.ANY)


def _mesh_pos():
    return lax.axis_index("x"), lax.axis_index("y"), lax.axis_index("c")


class Exchange:
    def __init__(self, gathers=(), scatters=()):
        self.items = [("gather", a) for a in gathers] + [("scatter", a) for a in scatters]
        self.arrays = [a for _, a in self.items]
        self.n = len(self.items)

    def out_shapes(self):
        return [jax.ShapeDtypeStruct(((N_DEV,) + a.shape) if kind == "gather" else a.shape, a.dtype) for kind, a in self.items]

    def scratch(self):
        return [pltpu.SemaphoreType.DMA((7 * self.n,)), pltpu.SemaphoreType.DMA((7 * self.n,)), pltpu.SemaphoreType.DMA((self.n,))]

    def _copies(self, in_refs, out_refs, send_sems, recv_sems, local_sems):
        x, y, c = _mesh_pos()
        me = 4 * x + 2 * y + c
        local, sends, arrivals = [], [], []
        for t, (kind, _) in enumerate(self.items):
            src_of = (lambda slot, r=in_refs[t]: r) if kind == "gather" else (lambda slot, r=in_refs[t]: r.at[slot])
            local.append(pltpu.make_async_copy(src_of(me), out_refs[t].at[me], local_sems.at[t]))
            for k in range(1, N_DEV):
                px = 1 - x if k & 4 else x
                py = 1 - y if k & 2 else y
                pc = 1 - c if k & 1 else c
                pid = 4 * px + 2 * py + pc
                kw = dict(send_sem=send_sems.at[7 * t + k - 1], recv_sem=recv_sems.at[7 * t + k - 1],
                          device_id=(px, py, pc), device_id_type=pl.DeviceIdType.MESH)
                sends.append(pltpu.make_async_remote_copy(src_ref=src_of(pid), dst_ref=out_refs[t].at[me], **kw))
                arrivals.append(pltpu.make_async_remote_copy(src_ref=src_of(pid), dst_ref=out_refs[t].at[pid], **kw))
        return local, sends, arrivals

    def start(self, *refs):
        local, sends, _ = self._copies(*refs)
        for cp in local + sends:
            cp.start()

    def wait(self, *refs):
        local, sends, arrivals = self._copies(*refs)
        for cp in arrivals:
            cp.wait_recv()
        for cp in sends:
            cp.wait_send()
        for cp in local:
            cp.wait()


def carried(body, exchange, n_in, n_out, first_last):
    if exchange is None:
        return body, [], [], [], []
    nx = exchange.n

    def wrapped(*refs):
        ins, xin = refs[:n_in], refs[n_in:n_in + nx]
        outs, xout = refs[n_in + nx:n_in + nx + n_out], refs[n_in + nx + n_out:n_in + 2 * nx + n_out]
        scratch, sems = refs[n_in + 2 * nx + n_out:-3], refs[-3:]
        first, last = first_last()

        @pl.when(first)
        def _():
            exchange.start(xin, xout, *sems)

        body(*ins, *outs, *scratch)

        @pl.when(last)
        def _():
            exchange.wait(xin, xout, *sems)

    return wrapped, [_ANY] * nx, [_ANY] * nx, exchange.out_shapes(), exchange.scratch()


def exchange_only(exchange, name):
    def body(*refs):
        xin, xout, sems = refs[:exchange.n], refs[exchange.n:2 * exchange.n], refs[-3:]
        exchange.start(xin, xout, *sems)
        exchange.wait(xin, xout, *sems)

    return pl.pallas_call(
        body, name=name, in_specs=[_ANY] * exchange.n, out_specs=[_ANY] * exchange.n,
        out_shape=exchange.out_shapes(), scratch_shapes=exchange.scratch(),
    )(*exchange.arrays)


def _residual_out(a, w_out, x, next_g, name, **tiles):
    if next_g is None:
        (y,) = matmul(a, w_out, "nn", name, [F32], epilogue=lambda acc, r: (r + acc,), extras=(x,), **tiles)
        return y, None

    def add_and_norm(acc, r, g):
        y = r + acc
        return y, y * lax.rsqrt(jnp.mean(y * y, axis=-1, keepdims=True) + EPS) * g

    assert w_out.shape[1] <= tiles.get("tn", 1024)
    return matmul(a, w_out, "nn", name, [F32, BF16], epilogue=add_and_norm, extras=(x, next_g.reshape(1, -1)), **tiles)


def attention_fwd(x, h, w_in, sink, qn_g, kn_g, w_out, tables, next_g, tag, exchange=None):
    (proj,) = matmul(h, w_in, "nn", f"{tag}_proj", [F32], tn=ATT_IN)
    qa, ka, va, qb, kb, vb = qkv_post_fwd(proj, tables, qn_g, kn_g, f"{tag}_qkv")
    cat, lse_a = window_attn_fwd(qa, ka, va, sink, f"{tag}_win")
    cat, lse_b, *arrived = flash_attn_fwd(qb, kb, vb, cat, f"{tag}_flash", exchange)
    y, h_next = _residual_out(cat, w_out, x, next_g, f"{tag}_out")
    saved = (x, h, proj, qa, ka, va, qb, kb, vb, cat, lse_a, lse_b)
    return y, h_next, saved, arrived


def attention_bwd(dy, dyb, saved, norm_g, w_in, sink, qn_g, kn_g, w_out, tables, tag, exchange_with=None):
    x, h, proj, qa, ka, va, qb, kb, vb, cat, lse_a, lse_b = saved
    (dcat,) = matmul(dyb, w_out, "nt", f"{tag}_dcat", [BF16])
    (dw_out,) = matmul(cat, dyb, "tn", f"{tag}_dwout", [BF16], tk=2048)
    delta = attn_delta(cat, dcat, f"{tag}_delta")
    dqa, dka, dva, dsink = window_attn_bwd(qa, ka, va, sink, dcat, lse_a, delta, f"{tag}_dwin")
    exchange = exchange_with(dw_out) if exchange_with else None
    dqb, dkb, dvb, *arrived = flash_attn_bwd(qb, kb, vb, dcat, lse_b, delta, f"{tag}_dflash", exchange)
    dproj, dqg, dkg = qkv_post_bwd(proj, tables, qn_g, kn_g, dqa, dka, dva, dqb, dkb, dvb, f"{tag}_dqkv")
    (dw_in,) = matmul(h, dproj, "tn", f"{tag}_dwin_w", [BF16], tn=ATT_IN // 2, tk=2048)
    dx, dxb, dg = matmul_nt_normbwd(dproj, w_in, x, norm_g, dy, f"{tag}_dx")
    grp = A_HEADS // A_KV_HEADS
    small = dict(norm=dg[0], sink=dsink[:, :grp, 0].reshape(A_HEADS), qnorm=dqg[0, :HEAD_DIM], knorm=dkg[0, :HEAD_DIM])
    return dx, dxb, dw_in, dw_out, small, arrived


def sgu_fwd(x, h, w_in, ln_g, ln_b, w_s, b_rows, w_out, next_g, tag):
    apre, z = matmul(h, w_in, "nn", f"{tag}_in", [BF16, BF16], epilogue=lambda acc: (acc, _gelu(acc)))
    y = sgu_mix_fwd(z, ln_g, ln_b, w_s, b_rows, f"{tag}_mix")
    out, h_next = _residual_out(y, w_out, x, next_g, f"{tag}_out")
    return out, h_next, (x, h, apre, z, y)


def sgu_bwd(dout, doutb, saved, norm_g, w_in, ln_g, ln_b, w_s, b_rows, w_out, tag):
    x, h, apre, z, y = saved
    (dy,) = matmul(doutb, w_out, "nt", f"{tag}_dy", [BF16])
    (dw_out,) = matmul(y, doutb, "tn", f"{tag}_dwout", [BF16], tk=2048)
    dapre, dlg, dlb, dws, dbs = sgu_mix_bwd(z, apre, dy, ln_g, ln_b, w_s, b_rows, f"{tag}_dmix")
    (dw_in,) = matmul(h, dapre, "tn", f"{tag}_dwin", [BF16], out_shards=True, tk=4096)
    dx, dxb, dg = matmul_nt_normbwd(dapre, w_in, x, norm_g, dout, f"{tag}_dx")
    small = dict(norm=dg[0], ln_g=dlg[0], ln_b=dlb[0], w_s=dws, b_s=dbs)
    return dx, dxb, dw_in, dw_out, small


def _square(r):
    return r * r


def mlp_fwd(x, h, w1, w2, next_g, tag):
    (r,) = matmul(h, w1, "nn", f"{tag}_up", [BF16], epilogue=lambda acc: (jnp.maximum(acc, 0.0),), tm=2048)
    y, h_next = _residual_out(r, w2, x, next_g, f"{tag}_down", a_fn=_square, tk=2048)
    return y, h_next, (x, h, r)


def mlp_bwd(dy, dyb, saved, norm_g, w1, w2, tag):
    x, h, r = saved
    (da,) = matmul(dyb, w2, "nt", f"{tag}_da", [BF16], epilogue=lambda acc, rr: (acc * (2.0 * rr.astype(F32)),), extras=(r,),
                   tm=2048)
    (dw2,) = matmul(r, dyb, "tn", f"{tag}_dw2", [BF16], a_fn=_square, tk=2048)
    (dw1,) = matmul(h, da, "tn", f"{tag}_dw1", [BF16], out_shards=True, tk=4096)
    dx, dxb, dg = matmul_nt_normbwd(da, w1, x, norm_g, dy, f"{tag}_dx")
    return dx, dxb, dw1, dw2, dg[0]


ORDER = ("att_norm", "att_w_in", "att_sink", "att_qnorm", "att_knorm", "att_w_out", "sgu_norm", "sgu_w_in", "sgu_ln_g",
         "sgu_ln_b", "sgu_w_s", "sgu_b_s", "sgu_w_out", "mlp_norm", "mlp_w1", "mlp_w2", "final_norm")
SHARDED = ("att_w_in", "att_w_out", "sgu_w_in", "sgu_w_out", "mlp_w1", "mlp_w2")
SGU_VECS = ("sgu_norm", "sgu_ln_g", "sgu_ln_b")
SMALL_EARLY = ("sgu_w_s", "sgu_b_s", "mlp_norm", "final_norm", "loss")
SMALL_LATE = ("att_norm", "att_sink", "att_qnorm", "att_knorm")
SMALL_ROWS_MULT = 8


def _flat(blocks, names):
    flat = jnp.concatenate([blocks[n].reshape(-1).astype(F32) for n in names])
    per = SMALL_ROWS_MULT * FLAT_COLS
    total = -(-flat.shape[0] // per) * per
    return jnp.pad(flat, (0, total - flat.shape[0])).reshape(1, total // FLAT_COLS, FLAT_COLS)


def _unflat(flat, like, names):
    out, off = {}, 0
    f = flat.reshape(-1)
    for n in names:
        size = like[n].size
        out[n] = f[off:off + size].reshape(like[n].shape)
        off += size
    return out


def kernel(x, att_norm, att_w_in, att_sink, att_qnorm, att_knorm, att_w_out, sgu_norm, sgu_w_in, sgu_ln_g, sgu_ln_b, sgu_w_s, sgu_b_s, sgu_w_out, mlp_norm, mlp_w1, mlp_w2, final_norm, loss_target, m_att_norm, m_att_w_in, m_att_sink, m_att_qnorm, m_att_knorm, m_att_w_out, m_sgu_norm, m_sgu_w_in, m_sgu_ln_g, m_sgu_ln_b, m_sgu_w_s, m_sgu_b_s, m_sgu_w_out, m_mlp_norm, m_mlp_w1, m_mlp_w2, m_final_norm, v_att_norm, v_att_w_in, v_att_sink, v_att_qnorm, v_att_knorm, v_att_w_out, v_sgu_norm, v_sgu_w_in, v_sgu_ln_g, v_sgu_ln_b, v_sgu_w_s, v_sgu_b_s, v_sgu_w_out, v_mlp_norm, v_mlp_w1, v_mlp_w2, v_final_norm):
    w = dict(att_norm=att_norm, att_w_in=att_w_in, att_sink=att_sink, att_qnorm=att_qnorm, att_knorm=att_knorm,
             att_w_out=att_w_out, sgu_norm=sgu_norm, sgu_w_in=sgu_w_in, sgu_ln_g=sgu_ln_g, sgu_ln_b=sgu_ln_b, sgu_w_s=sgu_w_s,
             sgu_b_s=sgu_b_s, sgu_w_out=sgu_w_out, mlp_norm=mlp_norm, mlp_w1=mlp_w1, mlp_w2=mlp_w2, final_norm=final_norm)
    m = dict(att_norm=m_att_norm, att_w_in=m_att_w_in, att_sink=m_att_sink, att_qnorm=m_att_qnorm, att_knorm=m_att_knorm,
             att_w_out=m_att_w_out, sgu_norm=m_sgu_norm, sgu_w_in=m_sgu_w_in, sgu_ln_g=m_sgu_ln_g, sgu_ln_b=m_sgu_ln_b,
             sgu_w_s=m_sgu_w_s, sgu_b_s=m_sgu_b_s, sgu_w_out=m_sgu_w_out, mlp_norm=m_mlp_norm, mlp_w1=m_mlp_w1, mlp_w2=m_mlp_w2,
             final_norm=m_final_norm)
    v = dict(att_norm=v_att_norm, att_w_in=v_att_w_in, att_sink=v_att_sink, att_qnorm=v_att_qnorm, att_knorm=v_att_knorm,
             att_w_out=v_att_w_out, sgu_norm=v_sgu_norm, sgu_w_in=v_sgu_w_in, sgu_ln_g=v_sgu_ln_g, sgu_ln_b=v_sgu_ln_b,
             sgu_w_s=v_sgu_w_s, sgu_b_s=v_sgu_b_s, sgu_w_out=v_sgu_w_out, mlp_norm=v_mlp_norm, mlp_w1=v_mlp_w1, mlp_w2=v_mlp_w2,
             final_norm=v_final_norm)
    loss, grad_x, g, d, nm, nv = train_step(x[0], loss_target[0], w, m, v)
    return (loss, grad_x[None], *[g[n] for n in ORDER], *[d[n] for n in ORDER], *[nm[n] for n in ORDER], *[nv[n] for n in ORDER])


def train_step(x, target, w, m, v):
    t, d_model = x.shape
    n_att, n_sgu, depth = w["att_w_in"].shape[0], w["sgu_w_in"].shape[0], w["mlp_w1"].shape[0]
    bf = lambda n: w[n].astype(BF16)

    vec_local = jnp.stack([w[n] for n in SGU_VECS], axis=1)
    att_in, att_out = bf("att_w_in"), bf("att_w_out")
    g_in0, g_out0, g_vec = exchange_only(Exchange(gathers=[att_in[:1], att_out[:1], vec_local]), "gather_first")
    vecs = g_vec.transpose(1, 2, 0, 3).reshape(n_sgu, len(SGU_VECS), -1)
    rest = Exchange(gathers=[att_in[1:], att_out[1:], bf("sgu_w_in"), bf("sgu_w_out"), bf("mlp_w1"), bf("mlp_w2")])
    w_s_bf = w["sgu_w_s"].astype(BF16)
    b_rows = jnp.broadcast_to(w["sgu_b_s"][:, :, :, None], w["sgu_b_s"].shape + (LANES,))
    tables = _rope_tables(t)
    full_cols = lambda g: g.transpose(1, 2, 0, 3).reshape(g.shape[1], d_model, -1)

    mixer_norm = lambda layer: w["att_norm"][layer // 2] if layer % 2 == 0 else vecs[layer // 2, 0]
    saved = []
    h = rmsnorm_fwd(x, mixer_norm(0), "att0_norm")
    for layer in range(depth):
        i = layer // 2
        if layer % 2 == 0:
            if layer == 0:
                att_w_in, att_w_out = [full_cols(g_in0)[0]], [Gathered(g_out0, "row", 0)]
            x, h, sv, arrived = attention_fwd(x, h, att_w_in[i], w["att_sink"][i], w["att_qnorm"][i], w["att_knorm"][i],
                                              att_w_out[i], tables, w["mlp_norm"][layer], f"att{i}", rest if layer == 0 else None)
            if layer == 0:
                g_in1, g_out1, g_sgu_in, g_sgu_out, g_w1, g_w2 = arrived
                att_w_in += list(full_cols(g_in1))
                att_w_out += [Gathered(g_out1, "row", l) for l in range(n_att - 1)]
        else:
            x, h, sv = sgu_fwd(x, h, Gathered(g_sgu_in, "col", i), vecs[i, 1], vecs[i, 2], w_s_bf[i], b_rows[i],
                               Gathered(g_sgu_out, "row", i), w["mlp_norm"][layer], f"sgu{i}")
        x, h, sm = mlp_fwd(x, h, Gathered(g_w1, "col", layer), Gathered(g_w2, "row", layer),
                           mixer_norm(layer + 1) if layer + 1 < depth else None, f"mlp{layer}")
        saved.append((sv, sm))
    loss_row, dh, dhb, dgf = loss_head(x, w["final_norm"], target, "loss_head")

    queue, recv = [], {}
    gs = dict(att_norm=[None] * n_att, att_sink=[None] * n_att, att_qnorm=[None] * n_att, att_knorm=[None] * n_att,
              sgu_w_s=[None] * n_sgu, sgu_b_s=[None] * n_sgu, mlp_norm=[None] * depth)

    def row_slabs(g):
        return g.reshape(N_DEV, g.shape[0] // N_DEV, g.shape[1])

    def col_slabs(g):
        return g.reshape(g.shape[0], N_DEV, g.shape[1] // N_DEV).transpose(1, 0, 2)

    def take_queue(gathers=()):
        items = list(queue)
        queue.clear()
        keys = [k for k, _ in gathers] + [k for k, _ in items]
        return Exchange(gathers=[a for _, a in gathers], scatters=[a for _, a in items]), keys

    def small_early():
        blocks = dict(sgu_w_s=jnp.stack(gs["sgu_w_s"]), sgu_b_s=jnp.stack(gs["sgu_b_s"]), mlp_norm=jnp.stack(gs["mlp_norm"]),
                      final_norm=dgf[0], loss=loss_row[0, :1])
        return _flat(blocks, SMALL_EARLY)[0]

    for layer in reversed(range(depth)):
        i = layer // 2
        sv, sm = saved[layer]
        dh, dhb, dw1, dw2, gs["mlp_norm"][layer] = mlp_bwd(
            dh, dhb, sm, w["mlp_norm"][layer], Gathered(g_w1, "col", layer), Gathered(g_w2, "row", layer), f"mlp{layer}")
        queue += [(("mlp_w1", layer), dw1), (("mlp_w2", layer), row_slabs(dw2))]
        if layer % 2 == 0:
            keys = []

            def exchange_with(dw_out, i=i, layer=layer, keys=keys):
                if layer == 0:
                    queue.append((("att_w_out", i), row_slabs(dw_out)))
                ex, got = take_queue([("small_early", small_early())] if layer == 0 else ())
                keys += got
                return ex

            dh, dhb, dw_in, dw_out, sm_g, arrived = attention_bwd(
                dh, dhb, sv, w["att_norm"][i], att_w_in[i], w["att_sink"][i], w["att_qnorm"][i], w["att_knorm"][i],
                att_w_out[i], tables, f"att{i}", exchange_with)
            recv.update(zip(keys, arrived))
            queue.append((("att_w_in", i), col_slabs(dw_in)))
            if layer != 0:
                queue.append((("att_w_out", i), row_slabs(dw_out)))
            gs["att_norm"][i], gs["att_sink"][i] = sm_g["norm"], sm_g["sink"]
            gs["att_qnorm"][i], gs["att_knorm"][i] = sm_g["qnorm"], sm_g["knorm"]
        else:
            dh, dhb, dw_in, dw_out, sm_g = sgu_bwd(
                dh, dhb, sv, vecs[i, 0], Gathered(g_sgu_in, "col", i), vecs[i, 1], vecs[i, 2], w_s_bf[i], b_rows[i],
                Gathered(g_sgu_out, "row", i), f"sgu{i}")
            dvec = jnp.stack([sm_g["norm"], sm_g["ln_g"], sm_g["ln_b"]])
            queue += [(("sgu_w_in", i), dw_in), (("sgu_w_out", i), row_slabs(dw_out)), (("sgu_vecs", i), col_slabs(dvec))]
            gs["sgu_w_s"][i], gs["sgu_b_s"][i] = sm_g["w_s"], sm_g["b_s"]
    grad_x = dh
    late = dict(att_norm=jnp.stack(gs["att_norm"]), att_sink=jnp.stack(gs["att_sink"]), att_qnorm=jnp.stack(gs["att_qnorm"]),
                att_knorm=jnp.stack(gs["att_knorm"]))
    last, keys = take_queue([("small_late", _flat(late, SMALL_LATE)[0])])
    recv.update(zip(keys, exchange_only(last, "exchange_last")))

    outs = [{}, {}, {}, {}]
    for n in SHARDED:
        res = adamw([recv[(n, l)] for l in range(w[n].shape[0])], w[n], m[n], v[n], f"adamw_{n}")
        for o, r in zip(outs, res):
            o[n] = r
    stack_vecs = lambda src: jnp.stack([src[n] for n in SGU_VECS], axis=1)
    res = adamw([recv[("sgu_vecs", i)] for i in range(n_sgu)], stack_vecs(w), stack_vecs(m), stack_vecs(v), "adamw_sgu_vecs")
    for o, r in zip(outs, res):
        o.update({n: r[:, k] for k, n in enumerate(SGU_VECS)})
    zero = {"loss": jnp.zeros((1,), F32)}
    for names, key in ((SMALL_EARLY, "small_early"), (SMALL_LATE, "small_late")):
        res = adamw([recv[key]], _flat({**w, **zero}, names), _flat({**m, **zero}, names), _flat({**v, **zero}, names), f"adamw_{key}")
        for o, r in zip(outs, res):
            o.update(_unflat(r, {**w, **zero}, names))
    loss = outs[0]["loss"][0]
    return loss, grad_x, *outs
```

```python
import functools
import math

import jax
import jax.numpy as jnp
from jax import lax
from jax.experimental import pallas as pl
from jax.experimental.pallas import tpu as pltpu

F32 = jnp.float32
BF16 = jnp.bfloat16

HEAD_DIM = 64
A_HEADS = 8
A_KV_HEADS = 2
B_HEADS = 8
B_KV_HEADS = 2
WINDOW = 128
BLOCK = 128
ROPE_THETA = 10000.0
GRID_W = 64
SGU_GROUPS = 8
SGU_CHUNK = 128
EPS = 1e-6
SCALE = HEAD_DIM ** -0.5
NEG = -1e30
LOG2E = math.log2(math.e)
LN2 = math.log(2.0)

A_Q = A_HEADS * HEAD_DIM
A_KV = A_KV_HEADS * HEAD_DIM
B_Q = B_HEADS * HEAD_DIM
B_KV = B_KV_HEADS * HEAD_DIM
OFF_QA, OFF_KA, OFF_VA = 0, A_Q, A_Q + A_KV
OFF_QB = A_Q + 2 * A_KV
OFF_KB = OFF_QB + B_Q
OFF_VB = OFF_KB + B_KV
ATT_IN = OFF_VB + B_KV

ADAM_LR = 0.001
ADAM_B1 = 0.9
ADAM_B2 = 0.999
ADAM_EPS = 1e-08
ADAM_WD = 0.01
ADAM_STEP = 10

N_DEV = 8
LANES = 128
V7X_VMEM_LIMIT = 56 * 1024 * 1024
FLAT_COLS = 1024


def _cparams(sem, vmem=V7X_VMEM_LIMIT):
    return pltpu.CompilerParams(dimension_semantics=sem, vmem_limit_bytes=vmem)


def _dot_nn(a, b):
    return lax.dot_general(a, b, (((1,), (0,)), ((), ())), preferred_element_type=F32)


def _dot_nt(a, b):
    return lax.dot_general(a, b, (((1,), (1,)), ((), ())), preferred_element_type=F32)


def _dot_tn(a, b):
    return lax.dot_general(a, b, (((0,), (0,)), ((), ())), preferred_element_type=F32)


def _bf(x):
    return x if x.dtype == BF16 else x.astype(BF16)


def _lane(shape):
    return lax.broadcasted_iota(jnp.int32, shape, len(shape) - 1)


def _seg_matrix(rows_lo, rows_hi):
    r = lax.broadcasted_iota(jnp.int32, (LANES, LANES), 0)
    return jnp.where((r >= rows_lo) & (r < rows_hi), 1.0, 0.0).astype(BF16)


def _group_matrix(width):
    r = lax.broadcasted_iota(jnp.int32, (LANES, LANES), 0)
    c = lax.broadcasted_iota(jnp.int32, (LANES, LANES), 1)
    return jnp.where((r // width) == (c // width), 1.0, 0.0).astype(BF16)


def _dot_f32_by_ones(s, ones_bf16):
    hi = s.astype(BF16)
    lo = (s - hi.astype(F32)).astype(BF16)
    return _dot_nn(hi, ones_bf16) + _dot_nn(lo, ones_bf16)


def _swap_halves(x, width):
    half = width // 2
    first = (_lane(x.shape) % width) < half
    return jnp.where(first, pltpu.roll(x, LANES - half, 1), pltpu.roll(x, half, 1))


def rmsnorm_fwd(x, g, name):
    t, d = x.shape
    tm = min(t, 512)

    def body(x_ref, g_ref, h_ref):
        xf = x_ref[...]
        r = lax.rsqrt(jnp.mean(xf * xf, axis=-1, keepdims=True) + EPS)
        h_ref[...] = (xf * r * g_ref[...]).astype(BF16)

    return pl.pallas_call(
        body, name=name, grid=(t // tm,),
        in_specs=[pl.BlockSpec((tm, d), lambda i: (i, 0)), pl.BlockSpec((1, d), lambda i: (0, 0))],
        out_specs=pl.BlockSpec((tm, d), lambda i: (i, 0)),
        out_shape=jax.ShapeDtypeStruct((t, d), BF16),
        compiler_params=_cparams(("parallel",)),
    )(x, g.reshape(1, d))


def _fit(n, want):
    t = min(n, want)
    while n % t:
        t //= 2
    return t


class Gathered:
    def __init__(self, arr, kind, layer):
        self.arr, self.kind, self.layer = arr, kind, layer
        _, _, self.rows, self.cols = arr.shape
        self.shape = (N_DEV * self.rows, self.cols) if kind == "row" else (self.rows, N_DEV * self.cols)


def _b_operand(b, mode, tn, tk, idx):
    dot = {"nn": _dot_nn, "nt": _dot_nt, "tn": _dot_tn}[mode]
    if not isinstance(b, Gathered):
        if mode == "nt":
            spec = pl.BlockSpec((tn, tk), lambda *g: idx(*g))
        else:
            spec = pl.BlockSpec((tk, tn), lambda *g: idx(*g)[::-1])
        return b, spec, lambda av, ref: dot(av, _bf(ref[...]))
    lay, rows, cols = b.layer, b.rows, b.cols
    if mode == "nn" and b.kind == "col":
        s = tn // cols
        assert s * cols == tn
        spec = pl.BlockSpec((s, None, tk, cols), lambda *g: (idx(*g)[0], lay, idx(*g)[1], 0))
        return b.arr, spec, lambda av, ref: jnp.concatenate([_dot_nn(av, ref[c]) for c in range(s)], axis=1)
    if mode == "nn" and b.kind == "row":
        s = tk // rows
        assert s * rows == tk
        spec = pl.BlockSpec((s, None, rows, tn), lambda *g: (idx(*g)[1], lay, 0, idx(*g)[0]))
        return b.arr, spec, lambda av, ref: _dot_nn(av, ref[...].reshape(s * rows, tn))
    if mode == "nt" and b.kind == "row":
        s = tn // rows
        assert s * rows == tn
        spec = pl.BlockSpec((s, None, rows, tk), lambda *g: (idx(*g)[0], lay, 0, idx(*g)[1]))
        return b.arr, spec, lambda av, ref: _dot_nt(av, ref[...].reshape(s * rows, tk))
    if mode == "nt" and b.kind == "col":
        s = tk // cols
        assert s * cols == tk
        spec = pl.BlockSpec((s, None, tn, cols), lambda *g: (idx(*g)[1], lay, idx(*g)[0], 0))

        def prod(av, ref):
            tot = _dot_nt(av[:, :cols], ref[0])
            for c in range(1, s):
                tot = tot + _dot_nt(av[:, c * cols:(c + 1) * cols], ref[c])
            return tot

        return b.arr, spec, prod
    raise NotImplementedError((mode, b.kind))


def matmul(a, b, mode, name, out_dtypes, epilogue=None, extras=(), a_fn=None, out_shards=False, tm=1024, tn=1024, tk=1024):
    (m, k) = a.shape[::-1] if mode == "tn" else a.shape
    n = b.shape[0] if mode == "nt" else b.shape[1]
    if out_shards:
        tn = n // N_DEV
    tm, tn, tk = _fit(m, tm), _fit(n, tn), _fit(k, tk)
    nk = k // tk
    n_ex, n_out = len(extras), len(out_dtypes)
    if epilogue is None:
        epilogue = lambda acc: (acc,)
    b_arr, b_spec, prod = _b_operand(b, mode, tn, tk, lambda i, j, kk: (j, kk))

    def body(*refs):
        a_ref, b_ref = refs[0], refs[1]
        ex_refs = refs[2:2 + n_ex]
        out_refs = refs[2 + n_ex:2 + n_ex + n_out]
        acc_ref = refs[2 + n_ex + n_out] if nk > 1 else None
        kk = pl.program_id(2)
        av = _bf(a_ref[...])
        if a_fn is not None:
            av = a_fn(av)
        part = prod(av, b_ref)

        def finish(acc):
            outs = epilogue(acc, *[r[...] for r in ex_refs])
            for r, o in zip(out_refs, outs):
                r[...] = o.astype(r.dtype)

        if nk == 1:
            finish(part)
            return

        @pl.when(kk == 0)
        def _():
            acc_ref[...] = part

        @pl.when(kk > 0)
        def _():
            acc_ref[...] += part

        @pl.when(kk == nk - 1)
        def _():
            finish(acc_ref[...])

    if mode == "tn":
        a_spec = pl.BlockSpec((tk, tm), lambda i, j, kk: (kk, i))
    else:
        a_spec = pl.BlockSpec((tm, tk), lambda i, j, kk: (i, kk))
    mn_spec = pl.BlockSpec((tm, tn), lambda i, j, kk: (i, j))
    row_spec = pl.BlockSpec((1, tn), lambda i, j, kk: (0, j))
    if out_shards:
        out_spec = pl.BlockSpec((None, tm, tn), lambda i, j, kk: (j, i, 0))
        out_shape = [jax.ShapeDtypeStruct((N_DEV, m, tn), dt) for dt in out_dtypes]
    else:
        out_spec = mn_spec
        out_shape = [jax.ShapeDtypeStruct((m, n), dt) for dt in out_dtypes]
    outs = pl.pallas_call(
        body, name=name, grid=(m // tm, n // tn, nk),
        in_specs=[a_spec, b_spec] + [row_spec if e.shape[0] == 1 else mn_spec for e in extras],
        out_specs=[out_spec] * n_out,
        out_shape=out_shape,
        scratch_shapes=[pltpu.VMEM((tm, tn), F32)] if nk > 1 else [],
        compiler_params=_cparams(("parallel", "parallel", "arbitrary")),
    )(a, b_arr, *extras)
    return outs


def matmul_nt_normbwd(dz, w, x, g, dres, name, tm=512):
    m, k = dz.shape
    d = w.shape[0]
    tm = _fit(m, tm)
    w_arr, w_spec, prod = _b_operand(w, "nt", d, k, lambda i: (0, 0))

    def body(dz_ref, w_ref, x_ref, g_ref, dres_ref, dx_ref, dxb_ref, dg_ref):
        @pl.when(pl.program_id(0) == 0)
        def _():
            dg_ref[...] = jnp.zeros_like(dg_ref)

        dh = prod(_bf(dz_ref[...]), w_ref)
        xf = x_ref[...]
        r = lax.rsqrt(jnp.mean(xf * xf, axis=-1, keepdims=True) + EPS)
        xhat = xf * r
        dg_ref[...] += jnp.sum(dh * xhat, axis=0, keepdims=True)
        dxh = dh * g_ref[...]
        dx = r * (dxh - xhat * jnp.mean(dxh * xhat, axis=-1, keepdims=True))
        out = dres_ref[...] + dx
        dx_ref[...] = out
        dxb_ref[...] = out.astype(BF16)

    row = pl.BlockSpec((tm, d), lambda i: (i, 0))
    vec = pl.BlockSpec((1, d), lambda i: (0, 0))
    return pl.pallas_call(
        body, name=name, grid=(m // tm,),
        in_specs=[pl.BlockSpec((tm, k), lambda i: (i, 0)), w_spec, row, vec, row],
        out_specs=[row, row, vec],
        out_shape=[jax.ShapeDtypeStruct((m, d), F32), jax.ShapeDtypeStruct((m, d), BF16), jax.ShapeDtypeStruct((1, d), F32)],
        compiler_params=_cparams(("arbitrary",)),
    )(dz, w_arr, x, g.reshape(1, d), dres)


def _rope_tables(t):
    pos = lax.broadcasted_iota(jnp.int32, (t, LANES), 0)
    dim = lax.broadcasted_iota(jnp.int32, (t, LANES), 1) % HEAD_DIM

    def table(p, width):
        i = dim % (width // 2)
        ang = p.astype(F32) * (ROPE_THETA ** (-(2 * i).astype(F32) / width))
        return jnp.cos(ang), jnp.where(dim % width < width // 2, -jnp.sin(ang), jnp.sin(ang))

    cos_a, sin_a = table(pos, HEAD_DIM)
    cos_b, sin_b = table(jnp.where(dim < HEAD_DIM // 2, pos // GRID_W, pos % GRID_W), HEAD_DIM // 2)
    return cos_a, sin_a, cos_b, sin_b


def _headnorm(xs, gmat):
    return lax.rsqrt(_dot_f32_by_ones(xs * xs, gmat) * (1.0 / HEAD_DIM) + EPS)


def qkv_post_fwd(proj, tables, qn_g, kn_g, name):
    t = proj.shape[0]
    tm = min(t, 256)
    cos_a, sin_a, cos_b, sin_b = tables
    g2 = lambda g: jnp.concatenate([g, g]).reshape(1, LANES)
    grp = B_HEADS // B_KV_HEADS

    def body(p_ref, ca_ref, sa_ref, cb_ref, sb_ref, qg_ref, kg_ref, qa_ref, ka_ref, va_ref, qb_ref, qbt_ref, kb_ref, vb_ref):
        ca, sa, cb, sb = ca_ref[...], sa_ref[...], cb_ref[...], sb_ref[...]
        gmat = _group_matrix(HEAD_DIM)

        def rope_a(xs):
            return xs * ca + _swap_halves(xs, HEAD_DIM) * sa

        def norm_rope_b(xs, g):
            y = xs * _headnorm(xs, gmat) * g
            return y * cb + _swap_halves(y, HEAD_DIM // 2) * sb

        for c in range(A_Q // LANES):
            qa_ref[:, c * LANES:(c + 1) * LANES] = rope_a(p_ref[:, OFF_QA + c * LANES:OFF_QA + (c + 1) * LANES]).astype(BF16)
        ka_ref[...] = rope_a(p_ref[:, OFF_KA:OFF_KA + LANES]).astype(BF16)
        va_ref[...] = p_ref[:, OFF_VA:OFF_VA + LANES].astype(BF16)
        for c in range(B_Q // LANES):
            y = norm_rope_b(p_ref[:, OFF_QB + c * LANES:OFF_QB + (c + 1) * LANES], qg_ref[...]) * (SCALE * LOG2E)
            for half in range(2):
                head = 2 * c + half
                placed = _head_to_half(y, half, head // grp)
                qb_ref[head] = placed.astype(BF16)
                qbt_ref[head] = placed.T.astype(BF16)
        kb_ref[...] = norm_rope_b(p_ref[:, OFF_KB:OFF_KB + LANES], kg_ref[...]).astype(BF16)
        vb_ref[...] = p_ref[:, OFF_VB:OFF_VB + LANES].astype(BF16)

    tab = pl.BlockSpec((tm, LANES), lambda i: (i, 0))
    vec = pl.BlockSpec((1, LANES), lambda i: (0, 0))
    wide = pl.BlockSpec((tm, A_Q), lambda i: (i, 0))
    return pl.pallas_call(
        body, name=name, grid=(t // tm,),
        in_specs=[pl.BlockSpec((tm, ATT_IN), lambda i: (i, 0)), tab, tab, tab, tab, vec, vec],
        out_specs=[wide, tab, tab, pl.BlockSpec((B_HEADS, tm, LANES), lambda i: (0, i, 0)),
                   pl.BlockSpec((B_HEADS, LANES, tm), lambda i: (0, 0, i)), tab, tab],
        out_shape=[jax.ShapeDtypeStruct((t, A_Q), BF16), jax.ShapeDtypeStruct((t, LANES), BF16),
                   jax.ShapeDtypeStruct((t, LANES), BF16), jax.ShapeDtypeStruct((B_HEADS, t, LANES), BF16),
                   jax.ShapeDtypeStruct((B_HEADS, LANES, t), BF16), jax.ShapeDtypeStruct((t, LANES), BF16),
                   jax.ShapeDtypeStruct((t, LANES), BF16)],
        compiler_params=_cparams(("parallel",)),
    )(proj, cos_a, sin_a, cos_b, sin_b, g2(qn_g), g2(kn_g))


def qkv_post_bwd(proj, tables, qn_g, kn_g, dqa, dka, dva, dqb, dkb, dvb, name):
    t = proj.shape[0]
    tm = min(t, 256)
    cos_a, sin_a, cos_b, sin_b = tables
    g2 = lambda g: jnp.concatenate([g, g]).reshape(1, LANES)

    def body(p_ref, ca_ref, sa_ref, cb_ref, sb_ref, qg_ref, kg_ref, dqa_ref, dka_ref, dva_ref, dqb_ref, dkb_ref, dvb_ref,
             dp_ref, dqg_ref, dkg_ref):
        ca, sa, cb, sb = ca_ref[...], sa_ref[...], cb_ref[...], sb_ref[...]
        gmat = _group_matrix(HEAD_DIM)

        @pl.when(pl.program_id(0) == 0)
        def _():
            dqg_ref[...] = jnp.zeros_like(dqg_ref)
            dkg_ref[...] = jnp.zeros_like(dkg_ref)

        def rope_a_bwd(dy):
            return dy * ca + _swap_halves(dy * sa, HEAD_DIM)

        def norm_rope_b_bwd(dout, xs, g):
            dy = dout * cb + _swap_halves(dout * sb, HEAD_DIM // 2)
            r = _headnorm(xs, gmat)
            xhat = xs * r
            dxh = dy * g
            mean = _dot_f32_by_ones(dxh * xhat, gmat) * (1.0 / HEAD_DIM)
            return r * (dxh - xhat * mean), jnp.sum(dy * xhat, axis=0, keepdims=True)

        for c in range(A_Q // LANES):
            sl = slice(c * LANES, (c + 1) * LANES)
            dp_ref[:, OFF_QA + c * LANES:OFF_QA + (c + 1) * LANES] = rope_a_bwd(dqa_ref[:, sl].astype(F32)).astype(BF16)
        dp_ref[:, OFF_KA:OFF_KA + LANES] = rope_a_bwd(dka_ref[0] + dka_ref[1]).astype(BF16)
        dp_ref[:, OFF_VA:OFF_VA + LANES] = (dva_ref[0] + dva_ref[1]).astype(BF16)
        dqg = jnp.zeros((1, LANES), F32)
        for c in range(B_Q // LANES):
            sl = slice(c * LANES, (c + 1) * LANES)
            dx, dg = norm_rope_b_bwd(dqb_ref[:, sl].astype(F32), p_ref[:, OFF_QB + c * LANES:OFF_QB + (c + 1) * LANES], qg_ref[...])
            dp_ref[:, OFF_QB + c * LANES:OFF_QB + (c + 1) * LANES] = dx.astype(BF16)
            dqg = dqg + dg
        dqg_ref[...] += dqg
        dx, dg = norm_rope_b_bwd((dkb_ref[0] + dkb_ref[1]).T, p_ref[:, OFF_KB:OFF_KB + LANES], kg_ref[...])
        dp_ref[:, OFF_KB:OFF_KB + LANES] = dx.astype(BF16)
        dkg_ref[...] += dg
        dp_ref[:, OFF_VB:OFF_VB + LANES] = (dvb_ref[0] + dvb_ref[1]).T.astype(BF16)

        @pl.when(pl.program_id(0) == t // tm - 1)
        def _():
            dqg_ref[...] = dqg_ref[...] + pltpu.roll(dqg_ref[...], HEAD_DIM, 1)
            dkg_ref[...] = dkg_ref[...] + pltpu.roll(dkg_ref[...], HEAD_DIM, 1)

    tab = pl.BlockSpec((tm, LANES), lambda i: (i, 0))
    vec = pl.BlockSpec((1, LANES), lambda i: (0, 0))
    wide = pl.BlockSpec((tm, A_Q), lambda i: (i, 0))
    slab = pl.BlockSpec((2, tm, LANES), lambda i: (0, i, 0))
    per_chunk = dkb.shape[3] // tm
    slab_t = pl.BlockSpec((2, None, LANES, tm), lambda i: (0, i // per_chunk, 0, i % per_chunk))
    full = pl.BlockSpec((tm, ATT_IN), lambda i: (i, 0))
    return pl.pallas_call(
        body, name=name, grid=(t // tm,),
        in_specs=[full, tab, tab, tab, tab, vec, vec, wide, slab, slab, wide, slab_t, slab_t],
        out_specs=[full, vec, vec],
        out_shape=[jax.ShapeDtypeStruct((t, ATT_IN), BF16), jax.ShapeDtypeStruct((1, LANES), F32),
                   jax.ShapeDtypeStruct((1, LANES), F32)],
        compiler_params=_cparams(("arbitrary",)),
    )(proj, cos_a, sin_a, cos_b, sin_b, g2(qn_g), g2(kn_g), dqa, dka, dva, dqb, dkb, dvb)


def _head_to_half(xs, head_half, kv_half):
    low = _lane(xs.shape) < HEAD_DIM
    kept = jnp.where(low if head_half == 0 else jnp.logical_not(low), xs, 0.0)
    return jnp.where(kv_half == head_half, kept, pltpu.roll(kept, HEAD_DIM, 1))


def _halves_to_heads(r0, r1, kv_half):
    low = _lane(r0.shape) < HEAD_DIM
    a = jnp.where(kv_half == 0, r0, pltpu.roll(r0, HEAD_DIM, 1))
    b = jnp.where(kv_half == 1, r1, pltpu.roll(r1, HEAD_DIM, 1))
    return jnp.where(low, a, b)


def attn_delta(o, do, name):
    t, w = o.shape
    tm = min(t, 256)
    n_heads = w // HEAD_DIM
    grp = B_HEADS // B_KV_HEADS

    def body(o_ref, do_ref, d_ref, dob_ref, dobt_ref):
        lo, hi = _seg_matrix(0, HEAD_DIM), _seg_matrix(HEAD_DIM, LANES)
        for c in range(w // LANES):
            sl = slice(c * LANES, (c + 1) * LANES)
            dof = do_ref[:, sl].astype(F32)
            s = o_ref[:, sl].astype(F32) * dof
            d_ref[2 * c] = _dot_f32_by_ones(s, lo)
            d_ref[2 * c + 1] = _dot_f32_by_ones(s, hi)
            for half in range(2):
                head = 2 * c + half - A_HEADS
                if head >= 0:
                    placed = _head_to_half(dof, half, head // grp)
                    dob_ref[head] = placed.astype(BF16)
                    dobt_ref[head] = placed.T.astype(BF16)

    blk = pl.BlockSpec((tm, w), lambda i: (i, 0))
    return pl.pallas_call(
        body, name=name, grid=(t // tm,),
        in_specs=[blk, blk],
        out_specs=[pl.BlockSpec((n_heads, tm, LANES), lambda i: (0, i, 0)), pl.BlockSpec((B_HEADS, tm, LANES), lambda i: (0, i, 0)),
                   pl.BlockSpec((B_HEADS, LANES, tm), lambda i: (0, 0, i))],
        out_shape=[jax.ShapeDtypeStruct((n_heads, t, LANES), F32), jax.ShapeDtypeStruct((B_HEADS, t, LANES), BF16),
                   jax.ShapeDtypeStruct((B_HEADS, LANES, t), BF16)],
        compiler_params=_cparams(("parallel",)),
    )(o, do)


BAND = 3 * BLOCK


def _band_offsets(rows_rep):
    qi = lax.broadcasted_iota(jnp.int32, (BLOCK, BAND), 0)
    kj = lax.broadcasted_iota(jnp.int32, (BLOCK, BAND), 1)
    return jnp.concatenate([kj - qi] * rows_rep, axis=0)


def _band(n, t, offsets):
    start = pl.multiple_of(jnp.clip((n - 1) * BLOCK, 0, t - BAND), BLOCK)
    return start, jnp.abs(offsets + (start - n * BLOCK)) <= WINDOW


def window_attn_fwd(q, k, v, sink, name, blocks_per_step=8):
    t = q.shape[0]
    assert t >= BAND
    nq = _fit(t // BLOCK, blocks_per_step)
    tq = nq * BLOCK

    def body(sink_ref, q_ref, k_ref, v_ref, o_ref, lse_ref):
        j, n0 = pl.program_id(0), pl.program_id(1)
        kvh = j // 2
        row = lax.broadcasted_iota(jnp.int32, (2 * BLOCK, 1), 0)
        sk = jnp.where(row < BLOCK, sink_ref[2 * j], sink_ref[2 * j + 1]) * LOG2E
        offsets = _band_offsets(2)
        for u in range(nq):
            rows = slice(u * BLOCK, (u + 1) * BLOCK)
            start, ok = _band(n0 * nq + u, t, offsets)
            qf = q_ref[rows, :].astype(F32) * (SCALE * LOG2E)
            qs = jnp.concatenate([_head_to_half(qf, 0, kvh), _head_to_half(qf, 1, kvh)], axis=0).astype(BF16)
            s = jnp.where(ok, _dot_nt(qs, k_ref[pl.ds(start, BAND), :]), NEG)
            m = jnp.maximum(jnp.max(s, axis=-1, keepdims=True), sk)
            p = jnp.exp2(s - m)
            denom = jnp.sum(p, axis=-1, keepdims=True) + jnp.exp2(sk - m)
            o = _dot_nn(p.astype(BF16), v_ref[pl.ds(start, BAND), :]) / denom
            o_ref[rows, :] = _halves_to_heads(o[:BLOCK], o[BLOCK:], kvh).astype(BF16)
            lse = jnp.broadcast_to(m + jnp.log2(denom), (2 * BLOCK, LANES))
            lse_ref[0, rows, :] = lse[:BLOCK]
            lse_ref[1, rows, :] = lse[BLOCK:]

    qspec = pl.BlockSpec((tq, LANES), lambda j, n: (n, j))
    whole = pl.BlockSpec((t, LANES), lambda j, n: (0, 0))
    return pl.pallas_call(
        body, name=name, grid=(A_HEADS // 2, t // tq),
        in_specs=[pl.BlockSpec(memory_space=pltpu.SMEM), qspec, whole, whole],
        out_specs=[qspec, pl.BlockSpec((2, tq, LANES), lambda j, n: (j, n, 0))],
        out_shape=[jax.ShapeDtypeStruct((t, A_Q + B_Q), BF16), jax.ShapeDtypeStruct((A_HEADS, t, LANES), F32)],
        compiler_params=_cparams(("parallel", "parallel")),
    )(sink, q, k, v)


def window_attn_bwd(q, k, v, sink, do, lse, delta, name, blocks_per_step=4):
    t = q.shape[0]
    assert t >= BAND
    nq = _fit(t // BLOCK, blocks_per_step)
    tq = nq * BLOCK
    grp = A_HEADS // A_KV_HEADS
    gw = grp * HEAD_DIM

    def body(sink_ref, q_ref, do_ref, k_ref, v_ref, lse_ref, dl_ref, dq_ref, dk_ref, dv_ref, ds_ref):
        kvh, n0 = pl.program_id(0), pl.program_id(1)

        @pl.when(n0 == 0)
        def _():
            dk_ref[...] = jnp.zeros_like(dk_ref)
            dv_ref[...] = jnp.zeros_like(dv_ref)
            ds_ref[...] = jnp.zeros_like(ds_ref)

        rid = lax.broadcasted_iota(jnp.int32, (8, LANES), 0)
        upd = jnp.zeros((8, LANES), F32)
        offsets = _band_offsets(grp)
        for u in range(nq):
            rows = slice(u * BLOCK, (u + 1) * BLOCK)
            start, ok = _band(n0 * nq + u, t, offsets)
            band = pl.ds(start, BAND)
            qparts, doparts = [], []
            for hh in range(grp):
                sl = slice((hh // 2) * LANES, (hh // 2 + 1) * LANES)
                qparts.append(_head_to_half(q_ref[rows, sl].astype(F32) * (SCALE * LOG2E), hh % 2, kvh))
                doparts.append(_head_to_half(do_ref[rows, sl].astype(F32), hh % 2, kvh))
            qs = jnp.concatenate(qparts, axis=0).astype(BF16)
            dos = jnp.concatenate(doparts, axis=0).astype(BF16)
            lse_b = jnp.concatenate([lse_ref[hh, rows, :] for hh in range(grp)], axis=0)
            dl_b = jnp.concatenate([dl_ref[hh, rows, :] for hh in range(grp)], axis=0)
            kband, vband = k_ref[band, :], v_ref[band, :]
            s = jnp.where(ok, _dot_nt(qs, kband), NEG)
            p = jnp.exp2(s - lse_b[:, :1])
            dp = _dot_nt(dos, vband)
            dsc = (p * (dp - dl_b[:, :1])).astype(BF16)
            dv_ref[0, band, :] += _dot_tn(p.astype(BF16), dos)
            dk_ref[0, band, :] += _dot_tn(dsc, qs) * LN2
            dq = _dot_nn(dsc, kband) * SCALE
            for c in range(grp // 2):
                dq_ref[rows, c * LANES:(c + 1) * LANES] = _halves_to_heads(
                    dq[2 * c * BLOCK:(2 * c + 1) * BLOCK], dq[(2 * c + 1) * BLOCK:(2 * c + 2) * BLOCK], kvh).astype(dq_ref.dtype)
            for hh in range(grp):
                rs = slice(hh * BLOCK, (hh + 1) * BLOCK)
                tot = jnp.sum(jnp.exp2(sink_ref[kvh * grp + hh] * LOG2E - lse_b[rs]) * dl_b[rs], axis=0, keepdims=True)
                upd = upd + jnp.where(rid == hh, -tot, 0.0)
        ds_ref[0] += upd

    qspec = pl.BlockSpec((tq, gw), lambda kvh, n: (n, kvh))
    whole = pl.BlockSpec((t, LANES), lambda kvh, n: (0, 0))
    stat = pl.BlockSpec((grp, tq, LANES), lambda kvh, n: (kvh, n, 0))
    slab = pl.BlockSpec((1, t, LANES), lambda kvh, n: (kvh, 0, 0))
    return pl.pallas_call(
        body, name=name, grid=(A_KV_HEADS, t // tq),
        in_specs=[pl.BlockSpec(memory_space=pltpu.SMEM), qspec, qspec, whole, whole, stat, stat],
        out_specs=[qspec, slab, slab, pl.BlockSpec((1, 8, LANES), lambda kvh, n: (kvh, 0, 0))],
        out_shape=[jax.ShapeDtypeStruct((t, A_Q), BF16), jax.ShapeDtypeStruct((A_KV_HEADS, t, LANES), F32),
                   jax.ShapeDtypeStruct((A_KV_HEADS, t, LANES), F32), jax.ShapeDtypeStruct((A_KV_HEADS, 8, LANES), F32)],
        compiler_params=_cparams(("arbitrary", "arbitrary")),
    )(sink, q, do, k, v, lse, delta)


def flash_attn_fwd(q, k, v, cat, name, exchange=None, tq=256, tk=2048):
    t = q.shape[1]
    tq, tk = _fit(t, tq), _fit(t, tk)
    nk = t // tk

    def body(q_ref, k_ref, v_ref, cat_ref, o_ref, lse_ref):
        del cat_ref
        kvh = pl.program_id(0) // 2
        qs = q_ref[...].reshape(2 * tq, LANES)
        mine = (_lane((tk, LANES)) < HEAD_DIM) == (kvh == 0)

        def scores(c):
            return _dot_nt(qs, k_ref[c * tk:(c + 1) * tk, :])

        s = scores(0)
        m = jnp.full((2 * tq, 1), NEG, F32)
        acc = jnp.zeros((2 * tq, LANES), F32)
        for c in range(nk):
            s_next = scores(c + 1) if c + 1 < nk else None
            vb = jnp.where(mine, v_ref[c * tk:(c + 1) * tk, :], jnp.ones((), BF16))
            m_new = jnp.maximum(m, jnp.max(s, axis=-1, keepdims=True))
            p = jnp.exp2(s - m_new).astype(BF16)
            acc = jnp.exp2(m - m_new) * acc + _dot_nn(p, vb)
            m, s = m_new, s_next
        other = pltpu.roll(acc, HEAD_DIM, 1)
        o = acc / other
        o_ref[...] = _halves_to_heads(o[:tq], o[tq:], kvh).astype(BF16)
        in_mine = (_lane(acc.shape) < HEAD_DIM) == (kvh == 0)
        lse = jnp.broadcast_to(m, acc.shape) + jnp.log2(jnp.where(in_mine, other, acc))
        lse_ref[0] = lse[:tq]
        lse_ref[1] = lse[tq:]

    qspec = pl.BlockSpec((2, tq, LANES), lambda j, i: (j, i, 0))
    whole = pl.BlockSpec((t, LANES), lambda j, i: (0, 0))
    nj, ni = B_HEADS // 2, t // tq
    steps = lambda: ((pl.program_id(0) == 0) & (pl.program_id(1) == 0), (pl.program_id(0) == nj - 1) & (pl.program_id(1) == ni - 1))
    body, x_in, x_out, x_shapes, x_scratch = carried(body, exchange, 4, 2, steps)
    return pl.pallas_call(
        body, name=name, grid=(nj, ni),
        in_specs=[qspec, whole, whole, _ANY] + x_in,
        out_specs=[pl.BlockSpec((tq, LANES), lambda j, i: (i, A_Q // LANES + j)),
                   pl.BlockSpec((2, tq, LANES), lambda j, i: (j, i, 0))] + x_out,
        out_shape=[jax.ShapeDtypeStruct(cat.shape, BF16), jax.ShapeDtypeStruct((B_HEADS, t, LANES), F32)] + x_shapes,
        scratch_shapes=x_scratch,
        input_output_aliases={3: 0},
        compiler_params=_cparams(("arbitrary", "arbitrary")),
    )(q, k, v, cat, *(exchange.arrays if exchange else ()))


def flash_attn_bwd(q, q_t, k, v, do, do_t, lse, delta, name, exchange=None, tq=256, tk=512):
    t = q.shape[1]
    tq, tk = _fit(t, tq), _fit(t, tk)
    nk = t // tk
    grp = B_HEADS // B_KV_HEADS
    gw = grp * HEAD_DIM

    def body(q_ref, qt_ref, do_ref, dot_ref, k_ref, v_ref, lse_ref, dl_ref, dq_ref, dk_ref, dv_ref, dq_s):
        kvh, i = pl.program_id(0), pl.program_id(1)

        @pl.when(i == 0)
        def _():
            dk_ref[...] = jnp.zeros_like(dk_ref)
            dv_ref[...] = jnp.zeros_like(dv_ref)

        qs, dos = q_ref[...].reshape(grp * tq, LANES), do_ref[...].reshape(grp * tq, LANES)
        qs_t = jnp.concatenate([qt_ref[hh] for hh in range(grp)], axis=1)
        dos_t = jnp.concatenate([dot_ref[hh] for hh in range(grp)], axis=1)
        lse = jnp.concatenate([lse_ref[hh][:, :1] for hh in range(grp)], axis=0)
        dl = jnp.concatenate([dl_ref[hh][:, :1] for hh in range(grp)], axis=0)
        dq_s[...] = jnp.zeros_like(dq_s)

        def chunk(c, carry):
            rows = pl.ds(pl.multiple_of(c * tk, tk), tk)
            kb, vb = k_ref[rows, :], v_ref[rows, :]
            p = jnp.exp2(_dot_nt(qs, kb) - lse)
            dp = _dot_nt(dos, vb)
            dsc = (p * (dp - dl)).astype(BF16)
            dv_ref[0, c] += _dot_nn(dos_t, p.astype(BF16))
            dk_ref[0, c] += _dot_nn(qs_t, dsc) * LN2
            dq_s[...] += _dot_nn(dsc, kb)
            return carry

        lax.fori_loop(0, nk, chunk, 0)
        for c in range(grp // 2):
            dq_ref[:, c * LANES:(c + 1) * LANES] = (_halves_to_heads(
                dq_s[2 * c * tq:(2 * c + 1) * tq], dq_s[(2 * c + 1) * tq:(2 * c + 2) * tq], kvh) * SCALE).astype(dq_ref.dtype)

    dqspec = pl.BlockSpec((tq, gw), lambda kvh, i: (i, kvh))
    whole = pl.BlockSpec((t, LANES), lambda kvh, i: (0, 0))
    stat = pl.BlockSpec((grp, tq, LANES), lambda kvh, i: (kvh, i, 0))
    stat_t = pl.BlockSpec((grp, LANES, tq), lambda kvh, i: (kvh, 0, i))
    dlstat = pl.BlockSpec((grp, tq, LANES), lambda kvh, i: (A_HEADS // grp + kvh, i, 0))
    slab = pl.BlockSpec((1, nk, LANES, tk), lambda kvh, i: (kvh, 0, 0, 0))
    ni = t // tq
    steps = lambda: ((pl.program_id(0) == 0) & (pl.program_id(1) == 0),
                     (pl.program_id(0) == B_KV_HEADS - 1) & (pl.program_id(1) == ni - 1))
    body, x_in, x_out, x_shapes, x_scratch = carried(body, exchange, 8, 3, steps)
    return pl.pallas_call(
        body, name=name, grid=(B_KV_HEADS, ni),
        in_specs=[stat, stat_t, stat, stat_t, whole, whole, stat, dlstat] + x_in,
        out_specs=[dqspec, slab, slab] + x_out,
        out_shape=[jax.ShapeDtypeStruct((t, B_Q), BF16), jax.ShapeDtypeStruct((B_KV_HEADS, nk, LANES, tk), F32),
                   jax.ShapeDtypeStruct((B_KV_HEADS, nk, LANES, tk), F32)] + x_shapes,
        scratch_shapes=[pltpu.VMEM((grp * tq, LANES), F32)] + x_scratch,
        compiler_params=_cparams(("arbitrary", "arbitrary")),
    )(q, q_t, do, do_t, k, v, lse, delta, *(exchange.arrays if exchange else ()))


_GELU_C = math.sqrt(2.0 / math.pi)
_GELU_A = 0.044715


def _gelu(x):
    return 0.5 * x * (1.0 + jnp.tanh(_GELU_C * (x + _GELU_A * x * x * x)))


def _gelu_grad(x):
    th = jnp.tanh(_GELU_C * (x + _GELU_A * x * x * x))
    return 0.5 * (1.0 + th) + 0.5 * x * (1.0 - th * th) * _GELU_C * (1.0 + 3.0 * _GELU_A * x * x)


def _layernorm_stats(vf):
    mu = jnp.mean(vf, axis=-1, keepdims=True)
    vc = vf - mu
    r = lax.rsqrt(jnp.mean(vc * vc, axis=-1, keepdims=True) + EPS)
    return vc * r, r


def sgu_mix_fwd(z, ln_g, ln_b, w_s, b_rows, name):
    t, w2 = z.shape
    w = w2 // 2
    dg = w // SGU_GROUPS

    def body(u_ref, v_ref, g_ref, b_ref, ws_ref, bb_ref, y_ref):
        vhat, _ = _layernorm_stats(v_ref[...].astype(F32))
        vn = (vhat * g_ref[...] + b_ref[...]).astype(BF16)
        for g in range(SGU_GROUPS):
            sl = slice(g * dg, (g + 1) * dg)
            mixed = _dot_nn(ws_ref[g], vn[:, sl]) + bb_ref[g]
            y_ref[:, sl] = (u_ref[:, sl].astype(F32) * mixed).astype(BF16)

    vec = pl.BlockSpec((1, w), lambda n: (0, 0))
    whole = pl.BlockSpec((SGU_GROUPS, SGU_CHUNK, SGU_CHUNK), lambda n: (0, 0, 0))
    return pl.pallas_call(
        body, name=name, grid=(t // SGU_CHUNK,),
        in_specs=[pl.BlockSpec((SGU_CHUNK, w), lambda n: (n, 0)), pl.BlockSpec((SGU_CHUNK, w), lambda n: (n, 1)),
                  vec, vec, whole, whole],
        out_specs=pl.BlockSpec((SGU_CHUNK, w), lambda n: (n, 0)),
        out_shape=jax.ShapeDtypeStruct((t, w), BF16),
        compiler_params=_cparams(("parallel",)),
    )(z, z, ln_g.reshape(1, w), ln_b.reshape(1, w), w_s, b_rows)


def sgu_mix_bwd(z, apre, dy, ln_g, ln_b, w_s, b_rows, name):
    t, w2 = z.shape
    w = w2 // 2
    dg = w // SGU_GROUPS

    def body(u_ref, v_ref, au_ref, av_ref, dy_ref, g_ref, b_ref, ws_ref, bb_ref, da_ref, dlg_ref, dlb_ref, dws_ref, dbs_ref):
        @pl.when(pl.program_id(0) == 0)
        def _():
            dlg_ref[...] = jnp.zeros_like(dlg_ref)
            dlb_ref[...] = jnp.zeros_like(dlb_ref)
            dws_ref[...] = jnp.zeros_like(dws_ref)
            dbs_ref[...] = jnp.zeros_like(dbs_ref)

        vhat, r = _layernorm_stats(v_ref[...].astype(F32))
        gam = g_ref[...]
        vn = (vhat * gam + b_ref[...]).astype(BF16)
        ones8 = jnp.ones((8, dg), BF16)
        rid = lax.broadcasted_iota(jnp.int32, (8, SGU_CHUNK), 0)
        dbs = jnp.zeros((8, SGU_CHUNK), F32)
        dvn_parts = []
        for g in range(SGU_GROUPS):
            sl = slice(g * dg, (g + 1) * dg)
            dyg = dy_ref[:, sl].astype(F32)
            mixed = _dot_nn(ws_ref[g], vn[:, sl]) + bb_ref[g]
            da_ref[:, sl] = (dyg * mixed * _gelu_grad(au_ref[:, sl].astype(F32))).astype(BF16)
            dmix = dyg * u_ref[:, sl].astype(F32)
            dm_hi = dmix.astype(BF16)
            dm_lo = (dmix - dm_hi.astype(F32)).astype(BF16)
            dws_ref[g] += _dot_nt(dm_hi, vn[:, sl])
            dbs = dbs + jnp.where(rid == g, _dot_nt(ones8, dm_hi) + _dot_nt(ones8, dm_lo), 0.0)
            dvn_parts.append(_dot_tn(ws_ref[g], dm_hi))
        dbs_ref[...] += dbs
        dvn = jnp.concatenate(dvn_parts, axis=1)
        dlg_ref[...] += jnp.sum(dvn * vhat, axis=0, keepdims=True)
        dlb_ref[...] += jnp.sum(dvn, axis=0, keepdims=True)
        dvh = dvn * gam
        dv = r * (dvh - jnp.mean(dvh, axis=-1, keepdims=True) - vhat * jnp.mean(dvh * vhat, axis=-1, keepdims=True))
        da_ref[:, w:] = (dv * _gelu_grad(av_ref[...].astype(F32))).astype(BF16)

    vec = pl.BlockSpec((1, w), lambda n: (0, 0))
    whole = pl.BlockSpec((SGU_GROUPS, SGU_CHUNK, SGU_CHUNK), lambda n: (0, 0, 0))
    left = pl.BlockSpec((SGU_CHUNK, w), lambda n: (n, 0))
    right = pl.BlockSpec((SGU_CHUNK, w), lambda n: (n, 1))
    return pl.pallas_call(
        body, name=name, grid=(t // SGU_CHUNK,),
        in_specs=[left, right, left, right, left, vec, vec, whole, whole],
        out_specs=[pl.BlockSpec((SGU_CHUNK, w2), lambda n: (n, 0)), vec, vec, whole,
                   pl.BlockSpec((SGU_GROUPS, SGU_CHUNK), lambda n: (0, 0))],
        out_shape=[jax.ShapeDtypeStruct((t, w2), BF16), jax.ShapeDtypeStruct((1, w), F32), jax.ShapeDtypeStruct((1, w), F32),
                   jax.ShapeDtypeStruct((SGU_GROUPS, SGU_CHUNK, SGU_CHUNK), F32),
                   jax.ShapeDtypeStruct((SGU_GROUPS, SGU_CHUNK), F32)],
        compiler_params=_cparams(("arbitrary",)),
    )(z, z, apre, apre, dy, ln_g.reshape(1, w), ln_b.reshape(1, w), w_s, b_rows)


def loss_head(h, g, target, name):
    t, d = h.shape
    tm = min(t, 512)

    def body(h_ref, g_ref, t_ref, loss_ref, dh_ref, dhb_ref, dg_ref):
        @pl.when(pl.program_id(0) == 0)
        def _():
            loss_ref[...] = jnp.zeros_like(loss_ref)
            dg_ref[...] = jnp.zeros_like(dg_ref)

        xf = h_ref[...]
        r = lax.rsqrt(jnp.mean(xf * xf, axis=-1, keepdims=True) + EPS)
        xhat = xf * r
        err = xhat * g_ref[...] - t_ref[...]
        per_tok = jnp.mean(err * err, axis=-1, keepdims=True)
        loss_ref[...] += 0.5 * jnp.sum(per_tok, axis=0, keepdims=True)
        dy = err * (1.0 / d)
        dg_ref[...] += jnp.sum(dy * xhat, axis=0, keepdims=True)
        dxh = dy * g_ref[...]
        dh = r * (dxh - xhat * jnp.mean(dxh * xhat, axis=-1, keepdims=True))
        dh_ref[...] = dh
        dhb_ref[...] = dh.astype(BF16)

    row = pl.BlockSpec((tm, d), lambda i: (i, 0))
    vec = pl.BlockSpec((1, d), lambda i: (0, 0))
    return pl.pallas_call(
        body, name=name, grid=(t // tm,),
        in_specs=[row, vec, row],
        out_specs=[pl.BlockSpec((1, LANES), lambda i: (0, 0)), row, row, vec],
        out_shape=[jax.ShapeDtypeStruct((1, LANES), F32), jax.ShapeDtypeStruct((t, d), F32), jax.ShapeDtypeStruct((t, d), BF16),
                   jax.ShapeDtypeStruct((1, d), F32)],
        compiler_params=_cparams(("arbitrary",)),
    )(h, g.reshape(1, d), target)


ADAMW_BLOCK_BYTES = 1 << 20


def adamw(parts, w, m, v, name):
    n_layers, r, c = w.shape
    row_bytes = n_layers * c * 4
    if r * row_bytes <= 2 * ADAMW_BLOCK_BYTES:
        tr = r
    else:
        tr = _fit(r, 1 << int(math.log2(max(8, ADAMW_BLOCK_BYTES // row_bytes))))
    bc1 = 1.0 - ADAM_B1 ** ADAM_STEP
    bc2 = 1.0 - ADAM_B2 ** ADAM_STEP

    def body(*refs):
        p_refs = refs[:n_layers]
        w_ref, m_ref, v_ref, g_ref, d_ref, nm_ref, nv_ref = refs[n_layers:]
        for l in range(n_layers):
            g = p_refs[l][0].astype(F32)
            for j in range(1, N_DEV):
                g = g + p_refs[l][j].astype(F32)
            nm = ADAM_B1 * m_ref[l] + (1.0 - ADAM_B1) * g
            nv = ADAM_B2 * v_ref[l] + (1.0 - ADAM_B2) * (g * g)
            g_ref[l] = g
            nm_ref[l] = nm
            nv_ref[l] = nv
            d_ref[l] = -ADAM_LR * ((nm / bc1) / (jnp.sqrt(nv / bc2) + ADAM_EPS) + ADAM_WD * w_ref[l])

    blk = pl.BlockSpec((n_layers, tr, c), lambda i: (0, i, 0))
    return pl.pallas_call(
        body, name=name, grid=(r // tr,),
        in_specs=[pl.BlockSpec((N_DEV, tr, c), lambda i: (0, i, 0))] * n_layers + [blk, blk, blk],
        out_specs=[blk] * 4,
        out_shape=[jax.ShapeDtypeStruct((n_layers, r, c), F32)] * 4,
        compiler_params=_cparams(("parallel",)),
    )(*parts, w, m, v)


_ANY = pl.BlockSpec(memory_space=pl.ANY)


def _mesh_pos():
    return lax.axis_index("x"), lax.axis_index("y"), lax.axis_index("c")


class Exchange:
    def __init__(self, gathers=(), scatters=()):
        self.items = [("gather", a) for a in gathers] + [("scatter", a) for a in scatters]
        self.arrays = [a for _, a in self.items]
        self.n = len(self.items)

    def out_shapes(self):
        return [jax.ShapeDtypeStruct(((N_DEV,) + a.shape) if kind == "gather" else a.shape, a.dtype) for kind, a in self.items]

    def scratch(self):
        return [pltpu.SemaphoreType.DMA((7 * self.n,)), pltpu.SemaphoreType.DMA((7 * self.n,)), pltpu.SemaphoreType.DMA((self.n,))]

    def _copies(self, in_refs, out_refs, send_sems, recv_sems, local_sems):
        x, y, c = _mesh_pos()
        me = 4 * x + 2 * y + c
        local, sends, arrivals = [], [], []
        for t, (kind, _) in enumerate(self.items):
            src_of = (lambda slot, r=in_refs[t]: r) if kind == "gather" else (lambda slot, r=in_refs[t]: r.at[slot])
            local.append(pltpu.make_async_copy(src_of(me), out_refs[t].at[me], local_sems.at[t]))
            for k in range(1, N_DEV):
                px = 1 - x if k & 4 else x
                py = 1 - y if k & 2 else y
                pc = 1 - c if k & 1 else c
                pid = 4 * px + 2 * py + pc
                kw = dict(send_sem=send_sems.at[7 * t + k - 1], recv_sem=recv_sems.at[7 * t + k - 1],
                          device_id=(px, py, pc), device_id_type=pl.DeviceIdType.MESH)
                sends.append(pltpu.make_async_remote_copy(src_ref=src_of(pid), dst_ref=out_refs[t].at[me], **kw))
                arrivals.append(pltpu.make_async_remote_copy(src_ref=src_of(pid), dst_ref=out_refs[t].at[pid], **kw))
        return local, sends, arrivals

    def start(self, *refs):
        local, sends, _ = self._copies(*refs)
        for cp in local + sends:
            cp.start()

    def wait(self, *refs):
        local, sends, arrivals = self._copies(*refs)
        for cp in arrivals:
            cp.wait_recv()
        for cp in sends:
            cp.wait_send()
        for cp in local:
            cp.wait()


def carried(body, exchange, n_in, n_out, first_last):
    if exchange is None:
        return body, [], [], [], []
    nx = exchange.n

    def wrapped(*refs):
        ins, xin = refs[:n_in], refs[n_in:n_in + nx]
        outs, xout = refs[n_in + nx:n_in + nx + n_out], refs[n_in + nx + n_out:n_in + 2 * nx + n_out]
        scratch, sems = refs[n_in + 2 * nx + n_out:-3], refs[-3:]
        first, last = first_last()

        @pl.when(first)
        def _():
            exchange.start(xin, xout, *sems)

        body(*ins, *outs, *scratch)

        @pl.when(last)
        def _():
            exchange.wait(xin, xout, *sems)

    return wrapped, [_ANY] * nx, [_ANY] * nx, exchange.out_shapes(), exchange.scratch()


def exchange_only(exchange, name):
    def body(*refs):
        xin, xout, sems = refs[:exchange.n], refs[exchange.n:2 * exchange.n], refs[-3:]
        exchange.start(xin, xout, *sems)
        exchange.wait(xin, xout, *sems)

    return pl.pallas_call(
        body, name=name, in_specs=[_ANY] * exchange.n, out_specs=[_ANY] * exchange.n,
        out_shape=exchange.out_shapes(), scratch_shapes=exchange.scratch(),
    )(*exchange.arrays)


def _residual_out(a, w_out, x, next_g, name, **tiles):
    if next_g is None:
        (y,) = matmul(a, w_out, "nn", name, [F32], epilogue=lambda acc, r: (r + acc,), extras=(x,), **tiles)
        return y, None

    def add_and_norm(acc, r, g):
        y = r + acc
        return y, y * lax.rsqrt(jnp.mean(y * y, axis=-1, keepdims=True) + EPS) * g

    assert w_out.shape[1] <= tiles.get("tn", 1024)
    return matmul(a, w_out, "nn", name, [F32, BF16], epilogue=add_and_norm, extras=(x, next_g.reshape(1, -1)), **tiles)


def attention_fwd(x, h, w_in, sink, qn_g, kn_g, w_out, tables, next_g, tag, exchange=None):
    (proj,) = matmul(h, w_in, "nn", f"{tag}_proj", [F32], tn=ATT_IN)
    qa, ka, va, qb, qb_t, kb, vb = qkv_post_fwd(proj, tables, qn_g, kn_g, f"{tag}_qkv")
    cat, lse_a = window_attn_fwd(qa, ka, va, sink, f"{tag}_win")
    cat, lse_b, *arrived = flash_attn_fwd(qb, kb, vb, cat, f"{tag}_flash", exchange)
    if callable(w_out):
        w_out = w_out(arrived)
    y, h_next = _residual_out(cat, w_out, x, next_g, f"{tag}_out")
    saved = (x, h, proj, qa, ka, va, qb, qb_t, kb, vb, cat, lse_a, lse_b)
    return y, h_next, saved, arrived


def attention_bwd(dy, dyb, saved, norm_g, w_in, sink, qn_g, kn_g, w_out, tables, tag, exchange_with=None):
    x, h, proj, qa, ka, va, qb, qb_t, kb, vb, cat, lse_a, lse_b = saved
    (dcat,) = matmul(dyb, w_out, "nt", f"{tag}_dcat", [BF16])
    (dw_out,) = matmul(cat, dyb, "tn", f"{tag}_dwout", [BF16], tk=2048)
    delta, dob, dob_t = attn_delta(cat, dcat, f"{tag}_delta")
    dqa, dka, dva, dsink = window_attn_bwd(qa, ka, va, sink, dcat, lse_a, delta, f"{tag}_dwin")
    exchange = exchange_with(dw_out) if exchange_with else None
    dqb, dkb, dvb, *arrived = flash_attn_bwd(qb, qb_t, kb, vb, dob, dob_t, lse_b, delta, f"{tag}_dflash", exchange)
    dproj, dqg, dkg = qkv_post_bwd(proj, tables, qn_g, kn_g, dqa, dka, dva, dqb, dkb, dvb, f"{tag}_dqkv")
    (dw_in,) = matmul(h, dproj, "tn", f"{tag}_dwin_w", [BF16], tn=ATT_IN // 2, tk=2048)
    dx, dxb, dg = matmul_nt_normbwd(dproj, w_in, x, norm_g, dy, f"{tag}_dx")
    grp = A_HEADS // A_KV_HEADS
    small = dict(norm=dg[0], sink=dsink[:, :grp, 0].reshape(A_HEADS), qnorm=dqg[0, :HEAD_DIM], knorm=dkg[0, :HEAD_DIM])
    return dx, dxb, dw_in, dw_out, small, arrived


def sgu_fwd(x, h, w_in, ln_g, ln_b, w_s, b_rows, w_out, next_g, tag):
    apre, z = matmul(h, w_in, "nn", f"{tag}_in", [BF16, BF16], epilogue=lambda acc: (acc, _gelu(acc)))
    y = sgu_mix_fwd(z, ln_g, ln_b, w_s, b_rows, f"{tag}_mix")
    out, h_next = _residual_out(y, w_out, x, next_g, f"{tag}_out")
    return out, h_next, (x, h, apre, z, y)


def sgu_bwd(dout, doutb, saved, norm_g, w_in, ln_g, ln_b, w_s, b_rows, w_out, tag):
    x, h, apre, z, y = saved
    (dy,) = matmul(doutb, w_out, "nt", f"{tag}_dy", [BF16])
    (dw_out,) = matmul(y, doutb, "tn", f"{tag}_dwout", [BF16], tk=2048)
    dapre, dlg, dlb, dws, dbs = sgu_mix_bwd(z, apre, dy, ln_g, ln_b, w_s, b_rows, f"{tag}_dmix")
    (dw_in,) = matmul(h, dapre, "tn", f"{tag}_dwin", [BF16], out_shards=True, tk=4096)
    dx, dxb, dg = matmul_nt_normbwd(dapre, w_in, x, norm_g, dout, f"{tag}_dx")
    small = dict(norm=dg[0], ln_g=dlg[0], ln_b=dlb[0], w_s=dws, b_s=dbs)
    return dx, dxb, dw_in, dw_out, small


def _square(r):
    return r * r


def mlp_fwd(x, h, w1, w2, next_g, tag):
    (r,) = matmul(h, w1, "nn", f"{tag}_up", [BF16], epilogue=lambda acc: (jnp.maximum(acc, 0.0),), tm=2048)
    y, h_next = _residual_out(r, w2, x, next_g, f"{tag}_down", a_fn=_square, tk=2048)
    return y, h_next, (x, h, r)


def mlp_bwd(dy, dyb, saved, norm_g, w1, w2, tag):
    x, h, r = saved
    (da,) = matmul(dyb, w2, "nt", f"{tag}_da", [BF16], epilogue=lambda acc, rr: (acc * (2.0 * rr.astype(F32)),), extras=(r,),
                   tm=2048)
    (dw2,) = matmul(r, dyb, "tn", f"{tag}_dw2", [BF16], a_fn=_square, tk=2048)
    (dw1,) = matmul(h, da, "tn", f"{tag}_dw1", [BF16], out_shards=True, tk=4096)
    dx, dxb, dg = matmul_nt_normbwd(da, w1, x, norm_g, dy, f"{tag}_dx")
    return dx, dxb, dw1, dw2, dg[0]


ORDER = ("att_norm", "att_w_in", "att_sink", "att_qnorm", "att_knorm", "att_w_out", "sgu_norm", "sgu_w_in", "sgu_ln_g",
         "sgu_ln_b", "sgu_w_s", "sgu_b_s", "sgu_w_out", "mlp_norm", "mlp_w1", "mlp_w2", "final_norm")
SHARDED = ("att_w_in", "att_w_out", "sgu_w_in", "sgu_w_out", "mlp_w1", "mlp_w2")
SGU_VECS = ("sgu_norm", "sgu_ln_g", "sgu_ln_b")
SMALL_EARLY = ("sgu_w_s", "sgu_b_s", "mlp_norm", "final_norm", "loss")
SMALL_LATE = ("att_norm", "att_sink", "att_qnorm", "att_knorm")
SMALL_ROWS_MULT = 8


def _flat(blocks, names):
    flat = jnp.concatenate([blocks[n].reshape(-1).astype(F32) for n in names])
    per = SMALL_ROWS_MULT * FLAT_COLS
    total = -(-flat.shape[0] // per) * per
    return jnp.pad(flat, (0, total - flat.shape[0])).reshape(1, total // FLAT_COLS, FLAT_COLS)


def _unflat(flat, like, names):
    out, off = {}, 0
    f = flat.reshape(-1)
    for n in names:
        size = like[n].size
        out[n] = f[off:off + size].reshape(like[n].shape)
        off += size
    return out


def kernel(x, att_norm, att_w_in, att_sink, att_qnorm, att_knorm, att_w_out, sgu_norm, sgu_w_in, sgu_ln_g, sgu_ln_b, sgu_w_s, sgu_b_s, sgu_w_out, mlp_norm, mlp_w1, mlp_w2, final_norm, loss_target, m_att_norm, m_att_w_in, m_att_sink, m_att_qnorm, m_att_knorm, m_att_w_out, m_sgu_norm, m_sgu_w_in, m_sgu_ln_g, m_sgu_ln_b, m_sgu_w_s, m_sgu_b_s, m_sgu_w_out, m_mlp_norm, m_mlp_w1, m_mlp_w2, m_final_norm, v_att_norm, v_att_w_in, v_att_sink, v_att_qnorm, v_att_knorm, v_att_w_out, v_sgu_norm, v_sgu_w_in, v_sgu_ln_g, v_sgu_ln_b, v_sgu_w_s, v_sgu_b_s, v_sgu_w_out, v_mlp_norm, v_mlp_w1, v_mlp_w2, v_final_norm):
    w = dict(att_norm=att_norm, att_w_in=att_w_in, att_sink=att_sink, att_qnorm=att_qnorm, att_knorm=att_knorm,
             att_w_out=att_w_out, sgu_norm=sgu_norm, sgu_w_in=sgu_w_in, sgu_ln_g=sgu_ln_g, sgu_ln_b=sgu_ln_b, sgu_w_s=sgu_w_s,
             sgu_b_s=sgu_b_s, sgu_w_out=sgu_w_out, mlp_norm=mlp_norm, mlp_w1=mlp_w1, mlp_w2=mlp_w2, final_norm=final_norm)
    m = dict(att_norm=m_att_norm, att_w_in=m_att_w_in, att_sink=m_att_sink, att_qnorm=m_att_qnorm, att_knorm=m_att_knorm,
             att_w_out=m_att_w_out, sgu_norm=m_sgu_norm, sgu_w_in=m_sgu_w_in, sgu_ln_g=m_sgu_ln_g, sgu_ln_b=m_sgu_ln_b,
             sgu_w_s=m_sgu_w_s, sgu_b_s=m_sgu_b_s, sgu_w_out=m_sgu_w_out, mlp_norm=m_mlp_norm, mlp_w1=m_mlp_w1, mlp_w2=m_mlp_w2,
             final_norm=m_final_norm)
    v = dict(att_norm=v_att_norm, att_w_in=v_att_w_in, att_sink=v_att_sink, att_qnorm=v_att_qnorm, att_knorm=v_att_knorm,
             att_w_out=v_att_w_out, sgu_norm=v_sgu_norm, sgu_w_in=v_sgu_w_in, sgu_ln_g=v_sgu_ln_g, sgu_ln_b=v_sgu_ln_b,
             sgu_w_s=v_sgu_w_s, sgu_b_s=v_sgu_b_s, sgu_w_out=v_sgu_w_out, mlp_norm=v_mlp_norm, mlp_w1=v_mlp_w1, mlp_w2=v_mlp_w2,
             final_norm=v_final_norm)
    loss, grad_x, g, d, nm, nv = train_step(x[0], loss_target[0], w, m, v)
    return (loss, grad_x[None], *[g[n] for n in ORDER], *[d[n] for n in ORDER], *[nm[n] for n in ORDER], *[nv[n] for n in ORDER])


def train_step(x, target, w, m, v):
    t, d_model = x.shape
    n_att, n_sgu, depth = w["att_w_in"].shape[0], w["sgu_w_in"].shape[0], w["mlp_w1"].shape[0]
    bf = lambda n: w[n].astype(BF16)

    vec_local = jnp.stack([w[n] for n in SGU_VECS], axis=1)
    att_in = bf("att_w_in")
    g_in0, g_vec = exchange_only(Exchange(gathers=[att_in[:1], vec_local]), "gather_first")
    vecs = g_vec.transpose(1, 2, 0, 3).reshape(n_sgu, len(SGU_VECS), -1)
    rest = Exchange(gathers=[att_in[1:], bf("att_w_out"), bf("sgu_w_in"), bf("sgu_w_out"), bf("mlp_w1"), bf("mlp_w2")])
    w_s_bf = w["sgu_w_s"].astype(BF16)
    b_rows = jnp.broadcast_to(w["sgu_b_s"][:, :, :, None], w["sgu_b_s"].shape + (LANES,))
    tables = _rope_tables(t)
    full_cols = lambda g: g.transpose(1, 2, 0, 3).reshape(g.shape[1], d_model, -1)

    mixer_norm = lambda layer: w["att_norm"][layer // 2] if layer % 2 == 0 else vecs[layer // 2, 0]
    saved = []
    h = rmsnorm_fwd(x, mixer_norm(0), "att0_norm")
    for layer in range(depth):
        i = layer // 2
        if layer % 2 == 0:
            if layer == 0:
                att_w_in = [full_cols(g_in0)[0]]
            x, h, sv, arrived = attention_fwd(x, h, att_w_in[i], w["att_sink"][i], w["att_qnorm"][i], w["att_knorm"][i],
                                              (lambda arrived: Gathered(arrived[1], "row", 0)) if layer == 0 else att_w_out[i],
                                              tables, w["mlp_norm"][layer], f"att{i}", rest if layer == 0 else None)
            if layer == 0:
                g_in1, g_out, g_sgu_in, g_sgu_out, g_w1, g_w2 = arrived
                att_w_in += list(full_cols(g_in1))
                att_w_out = [Gathered(g_out, "row", l) for l in range(n_att)]
        else:
            x, h, sv = sgu_fwd(x, h, Gathered(g_sgu_in, "col", i), vecs[i, 1], vecs[i, 2], w_s_bf[i], b_rows[i],
                               Gathered(g_sgu_out, "row", i), w["mlp_norm"][layer], f"sgu{i}")
        x, h, sm = mlp_fwd(x, h, Gathered(g_w1, "col", layer), Gathered(g_w2, "row", layer),
                           mixer_norm(layer + 1) if layer + 1 < depth else None, f"mlp{layer}")
        saved.append((sv, sm))
    loss_row, dh, dhb, dgf = loss_head(x, w["final_norm"], target, "loss_head")

    queue, recv = [], {}
    gs = dict(att_norm=[None] * n_att, att_sink=[None] * n_att, att_qnorm=[None] * n_att, att_knorm=[None] * n_att,
              sgu_w_s=[None] * n_sgu, sgu_b_s=[None] * n_sgu, mlp_norm=[None] * depth)

    def row_slabs(g):
        return g.reshape(N_DEV, g.shape[0] // N_DEV, g.shape[1])

    def col_slabs(g):
        return g.reshape(g.shape[0], N_DEV, g.shape[1] // N_DEV).transpose(1, 0, 2)

    def take_queue(gathers=()):
        items = list(queue)
        queue.clear()
        keys = [k for k, _ in gathers] + [k for k, _ in items]
        return Exchange(gathers=[a for _, a in gathers], scatters=[a for _, a in items]), keys

    def small_early():
        blocks = dict(sgu_w_s=jnp.stack(gs["sgu_w_s"]), sgu_b_s=jnp.stack(gs["sgu_b_s"]), mlp_norm=jnp.stack(gs["mlp_norm"]),
                      final_norm=dgf[0], loss=loss_row[0, :1])
        return _flat(blocks, SMALL_EARLY)[0]

    for layer in reversed(range(depth)):
        i = layer // 2
        sv, sm = saved[layer]
        dh, dhb, dw1, dw2, gs["mlp_norm"][layer] = mlp_bwd(
            dh, dhb, sm, w["mlp_norm"][layer], Gathered(g_w1, "col", layer), Gathered(g_w2, "row", layer), f"mlp{layer}")
        queue += [(("mlp_w1", layer), dw1), (("mlp_w2", layer), row_slabs(dw2))]
        if layer % 2 == 0:
            keys = []

            def exchange_with(dw_out, i=i, layer=layer, keys=keys):
                if layer == 0:
                    queue.append((("att_w_out", i), row_slabs(dw_out)))
                ex, got = take_queue([("small_early", small_early())] if layer == 0 else ())
                keys += got
                return ex

            dh, dhb, dw_in, dw_out, sm_g, arrived = attention_bwd(
                dh, dhb, sv, w["att_norm"][i], att_w_in[i], w["att_sink"][i], w["att_qnorm"][i], w["att_knorm"][i],
                att_w_out[i], tables, f"att{i}", exchange_with)
            recv.update(zip(keys, arrived))
            queue.append((("att_w_in", i), col_slabs(dw_in)))
            if layer != 0:
                queue.append((("att_w_out", i), row_slabs(dw_out)))
            gs["att_norm"][i], gs["att_sink"][i] = sm_g["norm"], sm_g["sink"]
            gs["att_qnorm"][i], gs["att_knorm"][i] = sm_g["qnorm"], sm_g["knorm"]
        else:
            dh, dhb, dw_in, dw_out, sm_g = sgu_bwd(
                dh, dhb, sv, vecs[i, 0], Gathered(g_sgu_in, "col", i), vecs[i, 1], vecs[i, 2], w_s_bf[i], b_rows[i],
                Gathered(g_sgu_out, "row", i), f"sgu{i}")
            dvec = jnp.stack([sm_g["norm"], sm_g["ln_g"], sm_g["ln_b"]])
            queue += [(("sgu_w_in", i), dw_in), (("sgu_w_out", i), row_slabs(dw_out)), (("sgu_vecs", i), col_slabs(dvec))]
            gs["sgu_w_s"][i], gs["sgu_b_s"][i] = sm_g["w_s"], sm_g["b_s"]
    grad_x = dh
    late = dict(att_norm=jnp.stack(gs["att_norm"]), att_sink=jnp.stack(gs["att_sink"]), att_qnorm=jnp.stack(gs["att_qnorm"]),
                att_knorm=jnp.stack(gs["att_knorm"]))
    last, keys = take_queue([("small_late", _flat(late, SMALL_LATE)[0])])
    recv.update(zip(keys, exchange_only(last, "exchange_last")))

    outs = [{}, {}, {}, {}]
    for n in SHARDED:
        res = adamw([recv[(n, l)] for l in range(w[n].shape[0])], w[n], m[n], v[n], f"adamw_{n}")
        for o, r in zip(outs, res):
            o[n] = r
    stack_vecs = lambda src: jnp.stack([src[n] for n in SGU_VECS], axis=1)
    res = adamw([recv[("sgu_vecs", i)] for i in range(n_sgu)], stack_vecs(w), stack_vecs(m), stack_vecs(v), "adamw_sgu_vecs")
    for o, r in zip(outs, res):
        o.update({n: r[:, k] for k, n in enumerate(SGU_VECS)})
    zero = {"loss": jnp.zeros((1,), F32)}
    for names, key in ((SMALL_EARLY, "small_early"), (SMALL_LATE, "small_late")):
        res = adamw([recv[key]], _flat({**w, **zero}, names), _flat({**m, **zero}, names), _flat({**v, **zero}, names), f"adamw_{key}")
        for o, r in zip(outs, res):
            o.update(_unflat(r, {**w, **zero}, names))
    loss = outs[0]["loss"][0]
    return loss, grad_x, *outs
```

```python
import functools
import math

import jax
import jax.numpy as jnp
from jax import lax
from jax.experimental import pallas as pl
from jax.experimental.pallas import tpu as pltpu

F32 = jnp.float32
BF16 = jnp.bfloat16

HEAD_DIM = 64
A_HEADS = 8
A_KV_HEADS = 2
B_HEADS = 8
B_KV_HEADS = 2
WINDOW = 128
BLOCK = 128
ROPE_THETA = 10000.0
GRID_W = 64
SGU_GROUPS = 8
SGU_CHUNK = 128
EPS = 1e-6
SCALE = HEAD_DIM ** -0.5
NEG = -1e30
LOG2E = math.log2(math.e)
LN2 = math.log(2.0)

A_Q = A_HEADS * HEAD_DIM
A_KV = A_KV_HEADS * HEAD_DIM
B_Q = B_HEADS * HEAD_DIM
B_KV = B_KV_HEADS * HEAD_DIM
OFF_QA, OFF_KA, OFF_VA = 0, A_Q, A_Q + A_KV
OFF_QB = A_Q + 2 * A_KV
OFF_KB = OFF_QB + B_Q
OFF_VB = OFF_KB + B_KV
ATT_IN = OFF_VB + B_KV

ADAM_LR = 0.001
ADAM_B1 = 0.9
ADAM_B2 = 0.999
ADAM_EPS = 1e-08
ADAM_WD = 0.01
ADAM_STEP = 10

N_DEV = 8
LANES = 128
V7X_VMEM_LIMIT = 56 * 1024 * 1024
FLAT_COLS = 1024


def _cparams(sem, vmem=V7X_VMEM_LIMIT):
    return pltpu.CompilerParams(dimension_semantics=sem, vmem_limit_bytes=vmem)


def _dot_nn(a, b):
    return lax.dot_general(a, b, (((1,), (0,)), ((), ())), preferred_element_type=F32)


def _dot_nt(a, b):
    return lax.dot_general(a, b, (((1,), (1,)), ((), ())), preferred_element_type=F32)


def _dot_tn(a, b):
    return lax.dot_general(a, b, (((0,), (0,)), ((), ())), preferred_element_type=F32)


def _bf(x):
    return x if x.dtype == BF16 else x.astype(BF16)


def _lane(shape):
    return lax.broadcasted_iota(jnp.int32, shape, len(shape) - 1)


def _seg_matrix(rows_lo, rows_hi):
    r = lax.broadcasted_iota(jnp.int32, (LANES, LANES), 0)
    return jnp.where((r >= rows_lo) & (r < rows_hi), 1.0, 0.0).astype(BF16)


def _group_matrix(width):
    r = lax.broadcasted_iota(jnp.int32, (LANES, LANES), 0)
    c = lax.broadcasted_iota(jnp.int32, (LANES, LANES), 1)
    return jnp.where((r // width) == (c // width), 1.0, 0.0).astype(BF16)


def _dot_f32_by_ones(s, ones_bf16):
    hi = s.astype(BF16)
    lo = (s - hi.astype(F32)).astype(BF16)
    return _dot_nn(hi, ones_bf16) + _dot_nn(lo, ones_bf16)


def _swap_halves(x, width):
    half = width // 2
    first = (_lane(x.shape) % width) < half
    return jnp.where(first, pltpu.roll(x, LANES - half, 1), pltpu.roll(x, half, 1))


def rmsnorm_fwd(x, g, name):
    t, d = x.shape
    tm = min(t, 512)

    def body(x_ref, g_ref, h_ref):
        xf = x_ref[...]
        r = lax.rsqrt(jnp.mean(xf * xf, axis=-1, keepdims=True) + EPS)
        h_ref[...] = (xf * r * g_ref[...]).astype(BF16)

    return pl.pallas_call(
        body, name=name, grid=(t // tm,),
        in_specs=[pl.BlockSpec((tm, d), lambda i: (i, 0)), pl.BlockSpec((1, d), lambda i: (0, 0))],
        out_specs=pl.BlockSpec((tm, d), lambda i: (i, 0)),
        out_shape=jax.ShapeDtypeStruct((t, d), BF16),
        compiler_params=_cparams(("parallel",)),
    )(x, g.reshape(1, d))


def _fit(n, want):
    t = min(n, want)
    while n % t:
        t //= 2
    return t


class Gathered:
    def __init__(self, arr, kind, layer):
        self.arr, self.kind, self.layer = arr, kind, layer
        _, _, self.rows, self.cols = arr.shape
        self.shape = (N_DEV * self.rows, self.cols) if kind == "row" else (self.rows, N_DEV * self.cols)


def _b_operand(b, mode, tn, tk, idx):
    dot = {"nn": _dot_nn, "nt": _dot_nt, "tn": _dot_tn}[mode]
    if not isinstance(b, Gathered):
        if mode == "nt":
            spec = pl.BlockSpec((tn, tk), lambda *g: idx(*g))
        else:
            spec = pl.BlockSpec((tk, tn), lambda *g: idx(*g)[::-1])
        return b, spec, lambda av, ref: dot(av, _bf(ref[...]))
    lay, rows, cols = b.layer, b.rows, b.cols
    if mode == "nn" and b.kind == "col":
        s = tn // cols
        assert s * cols == tn
        spec = pl.BlockSpec((s, None, tk, cols), lambda *g: (idx(*g)[0], lay, idx(*g)[1], 0))
        return b.arr, spec, lambda av, ref: jnp.concatenate([_dot_nn(av, ref[c]) for c in range(s)], axis=1)
    if mode == "nn" and b.kind == "row":
        s = tk // rows
        assert s * rows == tk
        spec = pl.BlockSpec((s, None, rows, tn), lambda *g: (idx(*g)[1], lay, 0, idx(*g)[0]))
        return b.arr, spec, lambda av, ref: _dot_nn(av, ref[...].reshape(s * rows, tn))
    if mode == "nt" and b.kind == "row":
        s = tn // rows
        assert s * rows == tn
        spec = pl.BlockSpec((s, None, rows, tk), lambda *g: (idx(*g)[0], lay, 0, idx(*g)[1]))
        return b.arr, spec, lambda av, ref: _dot_nt(av, ref[...].reshape(s * rows, tk))
    if mode == "nt" and b.kind == "col":
        s = tk // cols
        assert s * cols == tk
        spec = pl.BlockSpec((s, None, tn, cols), lambda *g: (idx(*g)[1], lay, idx(*g)[0], 0))

        def prod(av, ref):
            tot = _dot_nt(av[:, :cols], ref[0])
            for c in range(1, s):
                tot = tot + _dot_nt(av[:, c * cols:(c + 1) * cols], ref[c])
            return tot

        return b.arr, spec, prod
    raise NotImplementedError((mode, b.kind))


def matmul(a, b, mode, name, out_dtypes, epilogue=None, extras=(), a_fn=None, out_shards=False, tm=1024, tn=1024, tk=1024):
    (m, k) = a.shape[::-1] if mode == "tn" else a.shape
    n = b.shape[0] if mode == "nt" else b.shape[1]
    if out_shards:
        tn = n // N_DEV
    tm, tn, tk = _fit(m, tm), _fit(n, tn), _fit(k, tk)
    nk = k // tk
    n_ex, n_out = len(extras), len(out_dtypes)
    if epilogue is None:
        epilogue = lambda acc: (acc,)
    b_arr, b_spec, prod = _b_operand(b, mode, tn, tk, lambda i, j, kk: (j, kk))

    def body(*refs):
        a_ref, b_ref = refs[0], refs[1]
        ex_refs = refs[2:2 + n_ex]
        out_refs = refs[2 + n_ex:2 + n_ex + n_out]
        acc_ref = refs[2 + n_ex + n_out] if nk > 1 else None
        kk = pl.program_id(2)
        av = _bf(a_ref[...])
        if a_fn is not None:
            av = a_fn(av)
        part = prod(av, b_ref)

        def finish(acc):
            outs = epilogue(acc, *[r[...] for r in ex_refs])
            for r, o in zip(out_refs, outs):
                r[...] = o.astype(r.dtype)

        if nk == 1:
            finish(part)
            return

        @pl.when(kk == 0)
        def _():
            acc_ref[...] = part

        @pl.when(kk > 0)
        def _():
            acc_ref[...] += part

        @pl.when(kk == nk - 1)
        def _():
            finish(acc_ref[...])

    if mode == "tn":
        a_spec = pl.BlockSpec((tk, tm), lambda i, j, kk: (kk, i))
    else:
        a_spec = pl.BlockSpec((tm, tk), lambda i, j, kk: (i, kk))
    mn_spec = pl.BlockSpec((tm, tn), lambda i, j, kk: (i, j))
    row_spec = pl.BlockSpec((1, tn), lambda i, j, kk: (0, j))
    if out_shards:
        out_spec = pl.BlockSpec((None, tm, tn), lambda i, j, kk: (j, i, 0))
        out_shape = [jax.ShapeDtypeStruct((N_DEV, m, tn), dt) for dt in out_dtypes]
    else:
        out_spec = mn_spec
        out_shape = [jax.ShapeDtypeStruct((m, n), dt) for dt in out_dtypes]
    outs = pl.pallas_call(
        body, name=name, grid=(m // tm, n // tn, nk),
        in_specs=[a_spec, b_spec] + [row_spec if e.shape[0] == 1 else mn_spec for e in extras],
        out_specs=[out_spec] * n_out,
        out_shape=out_shape,
        scratch_shapes=[pltpu.VMEM((tm, tn), F32)] if nk > 1 else [],
        compiler_params=_cparams(("parallel", "parallel", "arbitrary")),
    )(a, b_arr, *extras)
    return outs


def matmul_nt_normbwd(dz, w, x, g, dres, name, tm=512):
    m, k = dz.shape
    d = w.shape[0]
    tm = _fit(m, tm)
    w_arr, w_spec, prod = _b_operand(w, "nt", d, k, lambda i: (0, 0))

    def body(dz_ref, w_ref, x_ref, g_ref, dres_ref, dx_ref, dxb_ref, dg_ref):
        @pl.when(pl.program_id(0) == 0)
        def _():
            dg_ref[...] = jnp.zeros_like(dg_ref)

        dh = prod(_bf(dz_ref[...]), w_ref)
        xf = x_ref[...]
        r = lax.rsqrt(jnp.mean(xf * xf, axis=-1, keepdims=True) + EPS)
        xhat = xf * r
        dg_ref[...] += jnp.sum(dh * xhat, axis=0, keepdims=True)
        dxh = dh * g_ref[...]
        dx = r * (dxh - xhat * jnp.mean(dxh * xhat, axis=-1, keepdims=True))
        out = dres_ref[...] + dx
        dx_ref[...] = out
        dxb_ref[...] = out.astype(BF16)

    row = pl.BlockSpec((tm, d), lambda i: (i, 0))
    vec = pl.BlockSpec((1, d), lambda i: (0, 0))
    return pl.pallas_call(
        body, name=name, grid=(m // tm,),
        in_specs=[pl.BlockSpec((tm, k), lambda i: (i, 0)), w_spec, row, vec, row],
        out_specs=[row, row, vec],
        out_shape=[jax.ShapeDtypeStruct((m, d), F32), jax.ShapeDtypeStruct((m, d), BF16), jax.ShapeDtypeStruct((1, d), F32)],
        compiler_params=_cparams(("arbitrary",)),
    )(dz, w_arr, x, g.reshape(1, d), dres)


def _rope_tables(t):
    pos = lax.broadcasted_iota(jnp.int32, (t, LANES), 0)
    dim = lax.broadcasted_iota(jnp.int32, (t, LANES), 1) % HEAD_DIM

    def table(p, width):
        i = dim % (width // 2)
        ang = p.astype(F32) * (ROPE_THETA ** (-(2 * i).astype(F32) / width))
        return jnp.cos(ang), jnp.where(dim % width < width // 2, -jnp.sin(ang), jnp.sin(ang))

    cos_a, sin_a = table(pos, HEAD_DIM)
    cos_b, sin_b = table(jnp.where(dim < HEAD_DIM // 2, pos // GRID_W, pos % GRID_W), HEAD_DIM // 2)
    return cos_a, sin_a, cos_b, sin_b


def _headnorm(xs, gmat):
    return lax.rsqrt(_dot_f32_by_ones(xs * xs, gmat) * (1.0 / HEAD_DIM) + EPS)


def qkv_post_fwd(proj, tables, qn_g, kn_g, name):
    t = proj.shape[0]
    tm = min(t, 256)
    cos_a, sin_a, cos_b, sin_b = tables
    g2 = lambda g: jnp.concatenate([g, g]).reshape(1, LANES)
    grp = B_HEADS // B_KV_HEADS

    def body(p_ref, ca_ref, sa_ref, cb_ref, sb_ref, qg_ref, kg_ref, qa_ref, ka_ref, va_ref, qb_ref, qbt_ref, kb_ref, vb_ref,
             vbt_ref):
        ca, sa, cb, sb = ca_ref[...], sa_ref[...], cb_ref[...], sb_ref[...]
        gmat = _group_matrix(HEAD_DIM)

        def rope_a(xs):
            return xs * ca + _swap_halves(xs, HEAD_DIM) * sa

        def norm_rope_b(xs, g):
            y = xs * _headnorm(xs, gmat) * g
            return y * cb + _swap_halves(y, HEAD_DIM // 2) * sb

        for c in range(A_Q // LANES):
            qa_ref[:, c * LANES:(c + 1) * LANES] = rope_a(p_ref[:, OFF_QA + c * LANES:OFF_QA + (c + 1) * LANES]).astype(BF16)
        ka_ref[...] = rope_a(p_ref[:, OFF_KA:OFF_KA + LANES]).astype(BF16)
        va_ref[...] = p_ref[:, OFF_VA:OFF_VA + LANES].astype(BF16)
        for c in range(B_Q // LANES):
            y = norm_rope_b(p_ref[:, OFF_QB + c * LANES:OFF_QB + (c + 1) * LANES], qg_ref[...]) * (SCALE * LOG2E)
            for half in range(2):
                head = 2 * c + half
                placed = _head_to_half(y, half, head // grp)
                qb_ref[head] = placed.astype(BF16)
                qbt_ref[head] = placed.T.astype(BF16)
        kb_ref[...] = norm_rope_b(p_ref[:, OFF_KB:OFF_KB + LANES], kg_ref[...]).astype(BF16)
        vb = p_ref[:, OFF_VB:OFF_VB + LANES]
        vb_ref[...] = vb.astype(BF16)
        vbt_ref[...] = vb.T.astype(BF16)

    tab = pl.BlockSpec((tm, LANES), lambda i: (i, 0))
    vec = pl.BlockSpec((1, LANES), lambda i: (0, 0))
    wide = pl.BlockSpec((tm, A_Q), lambda i: (i, 0))
    return pl.pallas_call(
        body, name=name, grid=(t // tm,),
        in_specs=[pl.BlockSpec((tm, ATT_IN), lambda i: (i, 0)), tab, tab, tab, tab, vec, vec],
        out_specs=[wide, tab, tab, pl.BlockSpec((B_HEADS, tm, LANES), lambda i: (0, i, 0)),
                   pl.BlockSpec((B_HEADS, LANES, tm), lambda i: (0, 0, i)), tab, tab, pl.BlockSpec((LANES, tm), lambda i: (0, i))],
        out_shape=[jax.ShapeDtypeStruct((t, A_Q), BF16), jax.ShapeDtypeStruct((t, LANES), BF16),
                   jax.ShapeDtypeStruct((t, LANES), BF16), jax.ShapeDtypeStruct((B_HEADS, t, LANES), BF16),
                   jax.ShapeDtypeStruct((B_HEADS, LANES, t), BF16), jax.ShapeDtypeStruct((t, LANES), BF16),
                   jax.ShapeDtypeStruct((t, LANES), BF16), jax.ShapeDtypeStruct((LANES, t), BF16)],
        compiler_params=_cparams(("parallel",)),
    )(proj, cos_a, sin_a, cos_b, sin_b, g2(qn_g), g2(kn_g))


def qkv_post_bwd(proj, tables, qn_g, kn_g, dqa, dka, dva, dqb, dkb, dvb, name):
    t = proj.shape[0]
    tm = min(t, 256)
    cos_a, sin_a, cos_b, sin_b = tables
    g2 = lambda g: jnp.concatenate([g, g]).reshape(1, LANES)

    def body(p_ref, ca_ref, sa_ref, cb_ref, sb_ref, qg_ref, kg_ref, dqa_ref, dka_ref, dva_ref, dqb_ref, dkb_ref, dvb_ref,
             dp_ref, dqg_ref, dkg_ref):
        ca, sa, cb, sb = ca_ref[...], sa_ref[...], cb_ref[...], sb_ref[...]
        gmat = _group_matrix(HEAD_DIM)

        @pl.when(pl.program_id(0) == 0)
        def _():
            dqg_ref[...] = jnp.zeros_like(dqg_ref)
            dkg_ref[...] = jnp.zeros_like(dkg_ref)

        def rope_a_bwd(dy):
            return dy * ca + _swap_halves(dy * sa, HEAD_DIM)

        def norm_rope_b_bwd(dout, xs, g):
            dy = dout * cb + _swap_halves(dout * sb, HEAD_DIM // 2)
            r = _headnorm(xs, gmat)
            xhat = xs * r
            dxh = dy * g
            mean = _dot_f32_by_ones(dxh * xhat, gmat) * (1.0 / HEAD_DIM)
            return r * (dxh - xhat * mean), jnp.sum(dy * xhat, axis=0, keepdims=True)

        for c in range(A_Q // LANES):
            sl = slice(c * LANES, (c + 1) * LANES)
            dp_ref[:, OFF_QA + c * LANES:OFF_QA + (c + 1) * LANES] = rope_a_bwd(dqa_ref[:, sl].astype(F32)).astype(BF16)
        dp_ref[:, OFF_KA:OFF_KA + LANES] = rope_a_bwd(dka_ref[0] + dka_ref[1]).astype(BF16)
        dp_ref[:, OFF_VA:OFF_VA + LANES] = (dva_ref[0] + dva_ref[1]).astype(BF16)
        dqg = jnp.zeros((1, LANES), F32)
        for c in range(B_Q // LANES):
            sl = slice(c * LANES, (c + 1) * LANES)
            dx, dg = norm_rope_b_bwd(dqb_ref[:, sl].astype(F32), p_ref[:, OFF_QB + c * LANES:OFF_QB + (c + 1) * LANES], qg_ref[...])
            dp_ref[:, OFF_QB + c * LANES:OFF_QB + (c + 1) * LANES] = dx.astype(BF16)
            dqg = dqg + dg
        dqg_ref[...] += dqg
        dx, dg = norm_rope_b_bwd((dkb_ref[0] + dkb_ref[1]).T, p_ref[:, OFF_KB:OFF_KB + LANES], kg_ref[...])
        dp_ref[:, OFF_KB:OFF_KB + LANES] = dx.astype(BF16)
        dkg_ref[...] += dg
        dp_ref[:, OFF_VB:OFF_VB + LANES] = (dvb_ref[0] + dvb_ref[1]).T.astype(BF16)

        @pl.when(pl.program_id(0) == t // tm - 1)
        def _():
            dqg_ref[...] = dqg_ref[...] + pltpu.roll(dqg_ref[...], HEAD_DIM, 1)
            dkg_ref[...] = dkg_ref[...] + pltpu.roll(dkg_ref[...], HEAD_DIM, 1)

    tab = pl.BlockSpec((tm, LANES), lambda i: (i, 0))
    vec = pl.BlockSpec((1, LANES), lambda i: (0, 0))
    wide = pl.BlockSpec((tm, A_Q), lambda i: (i, 0))
    slab = pl.BlockSpec((2, tm, LANES), lambda i: (0, i, 0))
    per_chunk = dkb.shape[3] // tm
    slab_t = pl.BlockSpec((2, None, LANES, tm), lambda i: (0, i // per_chunk, 0, i % per_chunk))
    full = pl.BlockSpec((tm, ATT_IN), lambda i: (i, 0))
    return pl.pallas_call(
        body, name=name, grid=(t // tm,),
        in_specs=[full, tab, tab, tab, tab, vec, vec, wide, slab, slab, wide, slab_t, slab_t],
        out_specs=[full, vec, vec],
        out_shape=[jax.ShapeDtypeStruct((t, ATT_IN), BF16), jax.ShapeDtypeStruct((1, LANES), F32),
                   jax.ShapeDtypeStruct((1, LANES), F32)],
        compiler_params=_cparams(("arbitrary",)),
    )(proj, cos_a, sin_a, cos_b, sin_b, g2(qn_g), g2(kn_g), dqa, dka, dva, dqb, dkb, dvb)


def _head_to_half(xs, head_half, kv_half):
    low = _lane(xs.shape) < HEAD_DIM
    kept = jnp.where(low if head_half == 0 else jnp.logical_not(low), xs, 0.0)
    return jnp.where(kv_half == head_half, kept, pltpu.roll(kept, HEAD_DIM, 1))


def _halves_to_heads(r0, r1, kv_half):
    low = _lane(r0.shape) < HEAD_DIM
    a = jnp.where(kv_half == 0, r0, pltpu.roll(r0, HEAD_DIM, 1))
    b = jnp.where(kv_half == 1, r1, pltpu.roll(r1, HEAD_DIM, 1))
    return jnp.where(low, a, b)


def attn_delta(o, do, name):
    t, w = o.shape
    tm = min(t, 256)
    n_heads = w // HEAD_DIM
    grp = B_HEADS // B_KV_HEADS

    def body(o_ref, do_ref, d_ref, dob_ref, dobt_ref):
        lo, hi = _seg_matrix(0, HEAD_DIM), _seg_matrix(HEAD_DIM, LANES)
        for c in range(w // LANES):
            sl = slice(c * LANES, (c + 1) * LANES)
            dof = do_ref[:, sl].astype(F32)
            s = o_ref[:, sl].astype(F32) * dof
            d_ref[2 * c] = _dot_f32_by_ones(s, lo)
            d_ref[2 * c + 1] = _dot_f32_by_ones(s, hi)
            for half in range(2):
                head = 2 * c + half - A_HEADS
                if head >= 0:
                    placed = _head_to_half(dof, half, head // grp)
                    dob_ref[head] = placed.astype(BF16)
                    dobt_ref[head] = placed.T.astype(BF16)

    blk = pl.BlockSpec((tm, w), lambda i: (i, 0))
    return pl.pallas_call(
        body, name=name, grid=(t // tm,),
        in_specs=[blk, blk],
        out_specs=[pl.BlockSpec((n_heads, tm, LANES), lambda i: (0, i, 0)), pl.BlockSpec((B_HEADS, tm, LANES), lambda i: (0, i, 0)),
                   pl.BlockSpec((B_HEADS, LANES, tm), lambda i: (0, 0, i))],
        out_shape=[jax.ShapeDtypeStruct((n_heads, t, LANES), F32), jax.ShapeDtypeStruct((B_HEADS, t, LANES), BF16),
                   jax.ShapeDtypeStruct((B_HEADS, LANES, t), BF16)],
        compiler_params=_cparams(("parallel",)),
    )(o, do)


BAND = 3 * BLOCK


def _band_offsets(rows_rep):
    qi = lax.broadcasted_iota(jnp.int32, (BLOCK, BAND), 0)
    kj = lax.broadcasted_iota(jnp.int32, (BLOCK, BAND), 1)
    return jnp.concatenate([kj - qi] * rows_rep, axis=0)


def _band(n, t, offsets):
    start = pl.multiple_of(jnp.clip((n - 1) * BLOCK, 0, t - BAND), BLOCK)
    return start, jnp.abs(offsets + (start - n * BLOCK)) <= WINDOW


def window_attn_fwd(q, k, v, sink, name, blocks_per_step=8):
    t = q.shape[0]
    assert t >= BAND
    nq = _fit(t // BLOCK, blocks_per_step)
    tq = nq * BLOCK

    def body(sink_ref, q_ref, k_ref, v_ref, o_ref, lse_ref):
        j, n0 = pl.program_id(0), pl.program_id(1)
        kvh = j // 2
        row = lax.broadcasted_iota(jnp.int32, (2 * BLOCK, 1), 0)
        sk = jnp.where(row < BLOCK, sink_ref[2 * j], sink_ref[2 * j + 1]) * LOG2E
        offsets = _band_offsets(2)
        for u in range(nq):
            rows = slice(u * BLOCK, (u + 1) * BLOCK)
            start, ok = _band(n0 * nq + u, t, offsets)
            qf = q_ref[rows, :].astype(F32) * (SCALE * LOG2E)
            qs = jnp.concatenate([_head_to_half(qf, 0, kvh), _head_to_half(qf, 1, kvh)], axis=0).astype(BF16)
            s = jnp.where(ok, _dot_nt(qs, k_ref[pl.ds(start, BAND), :]), NEG)
            m = jnp.maximum(jnp.max(s, axis=-1, keepdims=True), sk)
            p = jnp.exp2(s - m)
            denom = jnp.sum(p, axis=-1, keepdims=True) + jnp.exp2(sk - m)
            o = _dot_nn(p.astype(BF16), v_ref[pl.ds(start, BAND), :]) / denom
            o_ref[rows, :] = _halves_to_heads(o[:BLOCK], o[BLOCK:], kvh).astype(BF16)
            lse = jnp.broadcast_to(m + jnp.log2(denom), (2 * BLOCK, LANES))
            lse_ref[0, rows, :] = lse[:BLOCK]
            lse_ref[1, rows, :] = lse[BLOCK:]

    qspec = pl.BlockSpec((tq, LANES), lambda j, n: (n, j))
    whole = pl.BlockSpec((t, LANES), lambda j, n: (0, 0))
    return pl.pallas_call(
        body, name=name, grid=(A_HEADS // 2, t // tq),
        in_specs=[pl.BlockSpec(memory_space=pltpu.SMEM), qspec, whole, whole],
        out_specs=[qspec, pl.BlockSpec((2, tq, LANES), lambda j, n: (j, n, 0))],
        out_shape=[jax.ShapeDtypeStruct((t, A_Q + B_Q), BF16), jax.ShapeDtypeStruct((A_HEADS, t, LANES), F32)],
        compiler_params=_cparams(("parallel", "parallel")),
    )(sink, q, k, v)


def window_attn_bwd(q, k, v, sink, do, lse, delta, name, blocks_per_step=4):
    t = q.shape[0]
    assert t >= BAND
    nq = _fit(t // BLOCK, blocks_per_step)
    tq = nq * BLOCK
    grp = A_HEADS // A_KV_HEADS
    gw = grp * HEAD_DIM

    def body(sink_ref, q_ref, do_ref, k_ref, v_ref, lse_ref, dl_ref, dq_ref, dk_ref, dv_ref, ds_ref):
        kvh, n0 = pl.program_id(0), pl.program_id(1)

        @pl.when(n0 == 0)
        def _():
            dk_ref[...] = jnp.zeros_like(dk_ref)
            dv_ref[...] = jnp.zeros_like(dv_ref)
            ds_ref[...] = jnp.zeros_like(ds_ref)

        rid = lax.broadcasted_iota(jnp.int32, (8, LANES), 0)
        upd = jnp.zeros((8, LANES), F32)
        offsets = _band_offsets(grp)
        for u in range(nq):
            rows = slice(u * BLOCK, (u + 1) * BLOCK)
            start, ok = _band(n0 * nq + u, t, offsets)
            band = pl.ds(start, BAND)
            qparts, doparts = [], []
            for hh in range(grp):
                sl = slice((hh // 2) * LANES, (hh // 2 + 1) * LANES)
                qparts.append(_head_to_half(q_ref[rows, sl].astype(F32) * (SCALE * LOG2E), hh % 2, kvh))
                doparts.append(_head_to_half(do_ref[rows, sl].astype(F32), hh % 2, kvh))
            qs = jnp.concatenate(qparts, axis=0).astype(BF16)
            dos = jnp.concatenate(doparts, axis=0).astype(BF16)
            lse_b = jnp.concatenate([lse_ref[hh, rows, :] for hh in range(grp)], axis=0)
            dl_b = jnp.concatenate([dl_ref[hh, rows, :] for hh in range(grp)], axis=0)
            kband, vband = k_ref[band, :], v_ref[band, :]
            s = jnp.where(ok, _dot_nt(qs, kband), NEG)
            p = jnp.exp2(s - lse_b[:, :1])
            dp = _dot_nt(dos, vband)
            dsc = (p * (dp - dl_b[:, :1])).astype(BF16)
            dv_ref[0, band, :] += _dot_tn(p.astype(BF16), dos)
            dk_ref[0, band, :] += _dot_tn(dsc, qs) * LN2
            dq = _dot_nn(dsc, kband) * SCALE
            for c in range(grp // 2):
                dq_ref[rows, c * LANES:(c + 1) * LANES] = _halves_to_heads(
                    dq[2 * c * BLOCK:(2 * c + 1) * BLOCK], dq[(2 * c + 1) * BLOCK:(2 * c + 2) * BLOCK], kvh).astype(dq_ref.dtype)
            for hh in range(grp):
                rs = slice(hh * BLOCK, (hh + 1) * BLOCK)
                tot = jnp.sum(jnp.exp2(sink_ref[kvh * grp + hh] * LOG2E - lse_b[rs]) * dl_b[rs], axis=0, keepdims=True)
                upd = upd + jnp.where(rid == hh, -tot, 0.0)
        ds_ref[0] += upd

    qspec = pl.BlockSpec((tq, gw), lambda kvh, n: (n, kvh))
    whole = pl.BlockSpec((t, LANES), lambda kvh, n: (0, 0))
    stat = pl.BlockSpec((grp, tq, LANES), lambda kvh, n: (kvh, n, 0))
    slab = pl.BlockSpec((1, t, LANES), lambda kvh, n: (kvh, 0, 0))
    return pl.pallas_call(
        body, name=name, grid=(A_KV_HEADS, t // tq),
        in_specs=[pl.BlockSpec(memory_space=pltpu.SMEM), qspec, qspec, whole, whole, stat, stat],
        out_specs=[qspec, slab, slab, pl.BlockSpec((1, 8, LANES), lambda kvh, n: (kvh, 0, 0))],
        out_shape=[jax.ShapeDtypeStruct((t, A_Q), BF16), jax.ShapeDtypeStruct((A_KV_HEADS, t, LANES), F32),
                   jax.ShapeDtypeStruct((A_KV_HEADS, t, LANES), F32), jax.ShapeDtypeStruct((A_KV_HEADS, 8, LANES), F32)],
        compiler_params=_cparams(("arbitrary", "arbitrary")),
    )(sink, q, do, k, v, lse, delta)


def flash_attn_fwd(q, k, v_t, cat, name, exchange=None, tq=256, tk=2048):
    t = q.shape[1]
    tq, tk = _fit(t, tq), _fit(t, tk)
    nk = t // tk

    def body(q_ref, k_ref, vt_ref, cat_ref, o_ref, lse_ref):
        del cat_ref
        kvh = pl.program_id(0) // 2
        qs = q_ref[...].reshape(2 * tq, LANES)
        mine = (lax.broadcasted_iota(jnp.int32, (LANES, tk), 0) < HEAD_DIM) == (kvh == 0)

        def scores(c):
            return _dot_nt(k_ref[c * tk:(c + 1) * tk, :], qs)

        s = scores(0)
        m = jnp.full((1, 2 * tq), NEG, F32)
        acc = jnp.zeros((LANES, 2 * tq), F32)
        for c in range(nk):
            s_next = scores(c + 1) if c + 1 < nk else None
            vb = jnp.where(mine, vt_ref[:, c * tk:(c + 1) * tk], jnp.ones((), BF16))
            m_new = jnp.maximum(m, jnp.max(s, axis=0, keepdims=True))
            p = jnp.exp2(s - m_new).astype(BF16)
            acc = jnp.exp2(m - m_new) * acc + _dot_nn(vb, p)
            m, s = m_new, s_next
        other = pltpu.roll(acc, HEAD_DIM, 0)
        o = (acc / other).T
        o_ref[...] = _halves_to_heads(o[:tq], o[tq:], kvh).astype(BF16)
        in_mine = (lax.broadcasted_iota(jnp.int32, acc.shape, 0) < HEAD_DIM) == (kvh == 0)
        lse = (jnp.broadcast_to(m, acc.shape) + jnp.log2(jnp.where(in_mine, other, acc))).T
        lse_ref[0] = lse[:tq]
        lse_ref[1] = lse[tq:]

    qspec = pl.BlockSpec((2, tq, LANES), lambda j, i: (j, i, 0))
    whole = pl.BlockSpec((t, LANES), lambda j, i: (0, 0))
    whole_t = pl.BlockSpec((LANES, t), lambda j, i: (0, 0))
    nj, ni = B_HEADS // 2, t // tq
    steps = lambda: ((pl.program_id(0) == 0) & (pl.program_id(1) == 0), (pl.program_id(0) == nj - 1) & (pl.program_id(1) == ni - 1))
    body, x_in, x_out, x_shapes, x_scratch = carried(body, exchange, 4, 2, steps)
    return pl.pallas_call(
        body, name=name, grid=(nj, ni),
        in_specs=[qspec, whole, whole_t, _ANY] + x_in,
        out_specs=[pl.BlockSpec((tq, LANES), lambda j, i: (i, A_Q // LANES + j)),
                   pl.BlockSpec((2, tq, LANES), lambda j, i: (j, i, 0))] + x_out,
        out_shape=[jax.ShapeDtypeStruct(cat.shape, BF16), jax.ShapeDtypeStruct((B_HEADS, t, LANES), F32)] + x_shapes,
        scratch_shapes=x_scratch,
        input_output_aliases={3: 0},
        compiler_params=_cparams(("arbitrary", "arbitrary")),
    )(q, k, v_t, cat, *(exchange.arrays if exchange else ()))


def flash_attn_bwd(q, q_t, k, v, do, do_t, lse, delta, name, exchange=None, tq=256, tk=512):
    t = q.shape[1]
    tq, tk = _fit(t, tq), _fit(t, tk)
    nk = t // tk
    together = _fit(nk, 8)
    grp = B_HEADS // B_KV_HEADS
    gw = grp * HEAD_DIM

    def body(q_ref, qt_ref, do_ref, dot_ref, k_ref, v_ref, lse_ref, dl_ref, dq_ref, dk_ref, dv_ref, dq_s):
        kvh, i = pl.program_id(0), pl.program_id(1)

        @pl.when(i == 0)
        def _():
            dk_ref[...] = jnp.zeros_like(dk_ref)
            dv_ref[...] = jnp.zeros_like(dv_ref)

        qs, dos = q_ref[...].reshape(grp * tq, LANES), do_ref[...].reshape(grp * tq, LANES)
        qs_t = jnp.concatenate([qt_ref[hh] for hh in range(grp)], axis=1)
        dos_t = jnp.concatenate([dot_ref[hh] for hh in range(grp)], axis=1)
        lse = jnp.concatenate([lse_ref[hh][:, :1] for hh in range(grp)], axis=0)
        dl = jnp.concatenate([dl_ref[hh][:, :1] for hh in range(grp)], axis=0)
        dq_s[...] = jnp.zeros_like(dq_s)

        def chunks(c0, carry):
            cs = [c0 * together + u for u in range(together)]
            kbs = [k_ref[pl.ds(pl.multiple_of(c * tk, tk), tk), :] for c in cs]
            vbs = [v_ref[pl.ds(pl.multiple_of(c * tk, tk), tk), :] for c in cs]
            ss = [_dot_nt(qs, kb) for kb in kbs]
            dps = [_dot_nt(dos, vb) for vb in vbs]
            for c, kb, s, dp in zip(cs, kbs, ss, dps):
                p = jnp.exp2(s - lse)
                dsc = (p * (dp - dl)).astype(BF16)
                dv_ref[0, c] += _dot_nn(dos_t, p.astype(BF16))
                dk_ref[0, c] += _dot_nn(qs_t, dsc) * LN2
                dq_s[...] += _dot_nn(kb.T, dsc.T)
            return carry

        lax.fori_loop(0, nk // together, chunks, 0)
        dq = dq_s[...].T
        for c in range(grp // 2):
            dq_ref[:, c * LANES:(c + 1) * LANES] = (_halves_to_heads(
                dq[2 * c * tq:(2 * c + 1) * tq], dq[(2 * c + 1) * tq:(2 * c + 2) * tq], kvh) * SCALE).astype(dq_ref.dtype)

    dqspec = pl.BlockSpec((tq, gw), lambda kvh, i: (i, kvh))
    whole = pl.BlockSpec((t, LANES), lambda kvh, i: (0, 0))
    stat = pl.BlockSpec((grp, tq, LANES), lambda kvh, i: (kvh, i, 0))
    stat_t = pl.BlockSpec((grp, LANES, tq), lambda kvh, i: (kvh, 0, i))
    dlstat = pl.BlockSpec((grp, tq, LANES), lambda kvh, i: (A_HEADS // grp + kvh, i, 0))
    slab = pl.BlockSpec((1, nk, LANES, tk), lambda kvh, i: (kvh, 0, 0, 0))
    ni = t // tq
    steps = lambda: ((pl.program_id(0) == 0) & (pl.program_id(1) == 0),
                     (pl.program_id(0) == B_KV_HEADS - 1) & (pl.program_id(1) == ni - 1))
    body, x_in, x_out, x_shapes, x_scratch = carried(body, exchange, 8, 3, steps)
    return pl.pallas_call(
        body, name=name, grid=(B_KV_HEADS, ni),
        in_specs=[stat, stat_t, stat, stat_t, whole, whole, stat, dlstat] + x_in,
        out_specs=[dqspec, slab, slab] + x_out,
        out_shape=[jax.ShapeDtypeStruct((t, B_Q), BF16), jax.ShapeDtypeStruct((B_KV_HEADS, nk, LANES, tk), F32),
                   jax.ShapeDtypeStruct((B_KV_HEADS, nk, LANES, tk), F32)] + x_shapes,
        scratch_shapes=[pltpu.VMEM((LANES, grp * tq), F32)] + x_scratch,
        compiler_params=_cparams(("arbitrary", "arbitrary")),
    )(q, q_t, do, do_t, k, v, lse, delta, *(exchange.arrays if exchange else ()))


_GELU_C = math.sqrt(2.0 / math.pi)
_GELU_A = 0.044715


def _gelu(x):
    return 0.5 * x * (1.0 + jnp.tanh(_GELU_C * (x + _GELU_A * x * x * x)))


def _gelu_grad(x):
    th = jnp.tanh(_GELU_C * (x + _GELU_A * x * x * x))
    return 0.5 * (1.0 + th) + 0.5 * x * (1.0 - th * th) * _GELU_C * (1.0 + 3.0 * _GELU_A * x * x)


def _layernorm_stats(vf):
    mu = jnp.mean(vf, axis=-1, keepdims=True)
    vc = vf - mu
    r = lax.rsqrt(jnp.mean(vc * vc, axis=-1, keepdims=True) + EPS)
    return vc * r, r


def sgu_mix_fwd(z, ln_g, ln_b, w_s, b_rows, name):
    t, w2 = z.shape
    w = w2 // 2
    dg = w // SGU_GROUPS

    def body(u_ref, v_ref, g_ref, b_ref, ws_ref, bb_ref, y_ref):
        vhat, _ = _layernorm_stats(v_ref[...].astype(F32))
        vn = (vhat * g_ref[...] + b_ref[...]).astype(BF16)
        for g in range(SGU_GROUPS):
            sl = slice(g * dg, (g + 1) * dg)
            mixed = _dot_nn(ws_ref[g], vn[:, sl]) + bb_ref[g]
            y_ref[:, sl] = (u_ref[:, sl].astype(F32) * mixed).astype(BF16)

    vec = pl.BlockSpec((1, w), lambda n: (0, 0))
    whole = pl.BlockSpec((SGU_GROUPS, SGU_CHUNK, SGU_CHUNK), lambda n: (0, 0, 0))
    return pl.pallas_call(
        body, name=name, grid=(t // SGU_CHUNK,),
        in_specs=[pl.BlockSpec((SGU_CHUNK, w), lambda n: (n, 0)), pl.BlockSpec((SGU_CHUNK, w), lambda n: (n, 1)),
                  vec, vec, whole, whole],
        out_specs=pl.BlockSpec((SGU_CHUNK, w), lambda n: (n, 0)),
        out_shape=jax.ShapeDtypeStruct((t, w), BF16),
        compiler_params=_cparams(("parallel",)),
    )(z, z, ln_g.reshape(1, w), ln_b.reshape(1, w), w_s, b_rows)


def sgu_mix_bwd(z, apre, dy, ln_g, ln_b, w_s, b_rows, name):
    t, w2 = z.shape
    w = w2 // 2
    dg = w // SGU_GROUPS

    def body(u_ref, v_ref, au_ref, av_ref, dy_ref, g_ref, b_ref, ws_ref, bb_ref, da_ref, dlg_ref, dlb_ref, dws_ref, dbs_ref):
        @pl.when(pl.program_id(0) == 0)
        def _():
            dlg_ref[...] = jnp.zeros_like(dlg_ref)
            dlb_ref[...] = jnp.zeros_like(dlb_ref)
            dws_ref[...] = jnp.zeros_like(dws_ref)
            dbs_ref[...] = jnp.zeros_like(dbs_ref)

        vhat, r = _layernorm_stats(v_ref[...].astype(F32))
        gam = g_ref[...]
        vn = (vhat * gam + b_ref[...]).astype(BF16)
        ones8 = jnp.ones((8, dg), BF16)
        rid = lax.broadcasted_iota(jnp.int32, (8, SGU_CHUNK), 0)
        dbs = jnp.zeros((8, SGU_CHUNK), F32)
        dvn_parts = []
        for g in range(SGU_GROUPS):
            sl = slice(g * dg, (g + 1) * dg)
            dyg = dy_ref[:, sl].astype(F32)
            mixed = _dot_nn(ws_ref[g], vn[:, sl]) + bb_ref[g]
            da_ref[:, sl] = (dyg * mixed * _gelu_grad(au_ref[:, sl].astype(F32))).astype(BF16)
            dmix = dyg * u_ref[:, sl].astype(F32)
            dm_hi = dmix.astype(BF16)
            dm_lo = (dmix - dm_hi.astype(F32)).astype(BF16)
            dws_ref[g] += _dot_nt(dm_hi, vn[:, sl])
            dbs = dbs + jnp.where(rid == g, _dot_nt(ones8, dm_hi) + _dot_nt(ones8, dm_lo), 0.0)
            dvn_parts.append(_dot_tn(ws_ref[g], dm_hi))
        dbs_ref[...] += dbs
        dvn = jnp.concatenate(dvn_parts, axis=1)
        dlg_ref[...] += jnp.sum(dvn * vhat, axis=0, keepdims=True)
        dlb_ref[...] += jnp.sum(dvn, axis=0, keepdims=True)
        dvh = dvn * gam
        dv = r * (dvh - jnp.mean(dvh, axis=-1, keepdims=True) - vhat * jnp.mean(dvh * vhat, axis=-1, keepdims=True))
        da_ref[:, w:] = (dv * _gelu_grad(av_ref[...].astype(F32))).astype(BF16)

    vec = pl.BlockSpec((1, w), lambda n: (0, 0))
    whole = pl.BlockSpec((SGU_GROUPS, SGU_CHUNK, SGU_CHUNK), lambda n: (0, 0, 0))
    left = pl.BlockSpec((SGU_CHUNK, w), lambda n: (n, 0))
    right = pl.BlockSpec((SGU_CHUNK, w), lambda n: (n, 1))
    return pl.pallas_call(
        body, name=name, grid=(t // SGU_CHUNK,),
        in_specs=[left, right, left, right, left, vec, vec, whole, whole],
        out_specs=[pl.BlockSpec((SGU_CHUNK, w2), lambda n: (n, 0)), vec, vec, whole,
                   pl.BlockSpec((SGU_GROUPS, SGU_CHUNK), lambda n: (0, 0))],
        out_shape=[jax.ShapeDtypeStruct((t, w2), BF16), jax.ShapeDtypeStruct((1, w), F32), jax.ShapeDtypeStruct((1, w), F32),
                   jax.ShapeDtypeStruct((SGU_GROUPS, SGU_CHUNK, SGU_CHUNK), F32),
                   jax.ShapeDtypeStruct((SGU_GROUPS, SGU_CHUNK), F32)],
        compiler_params=_cparams(("arbitrary",)),
    )(z, z, apre, apre, dy, ln_g.reshape(1, w), ln_b.reshape(1, w), w_s, b_rows)


def loss_head(h, g, target, name):
    t, d = h.shape
    tm = min(t, 512)

    def body(h_ref, g_ref, t_ref, loss_ref, dh_ref, dhb_ref, dg_ref):
        @pl.when(pl.program_id(0) == 0)
        def _():
            loss_ref[...] = jnp.zeros_like(loss_ref)
            dg_ref[...] = jnp.zeros_like(dg_ref)

        xf = h_ref[...]
        r = lax.rsqrt(jnp.mean(xf * xf, axis=-1, keepdims=True) + EPS)
        xhat = xf * r
        err = xhat * g_ref[...] - t_ref[...]
        per_tok = jnp.mean(err * err, axis=-1, keepdims=True)
        loss_ref[...] += 0.5 * jnp.sum(per_tok, axis=0, keepdims=True)
        dy = err * (1.0 / d)
        dg_ref[...] += jnp.sum(dy * xhat, axis=0, keepdims=True)
        dxh = dy * g_ref[...]
        dh = r * (dxh - xhat * jnp.mean(dxh * xhat, axis=-1, keepdims=True))
        dh_ref[...] = dh
        dhb_ref[...] = dh.astype(BF16)

    row = pl.BlockSpec((tm, d), lambda i: (i, 0))
    vec = pl.BlockSpec((1, d), lambda i: (0, 0))
    return pl.pallas_call(
        body, name=name, grid=(t // tm,),
        in_specs=[row, vec, row],
        out_specs=[pl.BlockSpec((1, LANES), lambda i: (0, 0)), row, row, vec],
        out_shape=[jax.ShapeDtypeStruct((1, LANES), F32), jax.ShapeDtypeStruct((t, d), F32), jax.ShapeDtypeStruct((t, d), BF16),
                   jax.ShapeDtypeStruct((1, d), F32)],
        compiler_params=_cparams(("arbitrary",)),
    )(h, g.reshape(1, d), target)


ADAMW_BLOCK_BYTES = 1 << 20


def adamw(parts, w, m, v, name):
    n_layers, r, c = w.shape
    row_bytes = n_layers * c * 4
    if r * row_bytes <= 2 * ADAMW_BLOCK_BYTES:
        tr = r
    else:
        tr = _fit(r, 1 << int(math.log2(max(8, ADAMW_BLOCK_BYTES // row_bytes))))
    bc1 = 1.0 - ADAM_B1 ** ADAM_STEP
    bc2 = 1.0 - ADAM_B2 ** ADAM_STEP

    def body(*refs):
        p_refs = refs[:n_layers]
        w_ref, m_ref, v_ref, g_ref, d_ref, nm_ref, nv_ref = refs[n_layers:]
        for l in range(n_layers):
            g = p_refs[l][0].astype(F32)
            for j in range(1, N_DEV):
                g = g + p_refs[l][j].astype(F32)
            nm = ADAM_B1 * m_ref[l] + (1.0 - ADAM_B1) * g
            nv = ADAM_B2 * v_ref[l] + (1.0 - ADAM_B2) * (g * g)
            g_ref[l] = g
            nm_ref[l] = nm
            nv_ref[l] = nv
            d_ref[l] = -ADAM_LR * ((nm / bc1) / (jnp.sqrt(nv / bc2) + ADAM_EPS) + ADAM_WD * w_ref[l])

    blk = pl.BlockSpec((n_layers, tr, c), lambda i: (0, i, 0))
    return pl.pallas_call(
        body, name=name, grid=(r // tr,),
        in_specs=[pl.BlockSpec((N_DEV, tr, c), lambda i: (0, i, 0))] * n_layers + [blk, blk, blk],
        out_specs=[blk] * 4,
        out_shape=[jax.ShapeDtypeStruct((n_layers, r, c), F32)] * 4,
        compiler_params=_cparams(("parallel",)),
    )(*parts, w, m, v)


_ANY = pl.BlockSpec(memory_space=pl.ANY)


def _mesh_pos():
    return lax.axis_index("x"), lax.axis_index("y"), lax.axis_index("c")


class Exchange:
    def __init__(self, gathers=(), scatters=()):
        self.items = [("gather", a) for a in gathers] + [("scatter", a) for a in scatters]
        self.arrays = [a for _, a in self.items]
        self.n = len(self.items)

    def out_shapes(self):
        return [jax.ShapeDtypeStruct(((N_DEV,) + a.shape) if kind == "gather" else a.shape, a.dtype) for kind, a in self.items]

    def scratch(self):
        return [pltpu.SemaphoreType.DMA((7 * self.n,)), pltpu.SemaphoreType.DMA((7 * self.n,)), pltpu.SemaphoreType.DMA((self.n,))]

    def _copies(self, in_refs, out_refs, send_sems, recv_sems, local_sems):
        x, y, c = _mesh_pos()
        me = 4 * x + 2 * y + c
        local, sends, arrivals = [], [], []
        for t, (kind, _) in enumerate(self.items):
            src_of = (lambda slot, r=in_refs[t]: r) if kind == "gather" else (lambda slot, r=in_refs[t]: r.at[slot])
            local.append(pltpu.make_async_copy(src_of(me), out_refs[t].at[me], local_sems.at[t]))
            for k in range(1, N_DEV):
                px = 1 - x if k & 4 else x
                py = 1 - y if k & 2 else y
                pc = 1 - c if k & 1 else c
                pid = 4 * px + 2 * py + pc
                kw = dict(send_sem=send_sems.at[7 * t + k - 1], recv_sem=recv_sems.at[7 * t + k - 1],
                          device_id=(px, py, pc), device_id_type=pl.DeviceIdType.MESH)
                sends.append(pltpu.make_async_remote_copy(src_ref=src_of(pid), dst_ref=out_refs[t].at[me], **kw))
                arrivals.append(pltpu.make_async_remote_copy(src_ref=src_of(pid), dst_ref=out_refs[t].at[pid], **kw))
        return local, sends, arrivals

    def start(self, *refs):
        local, sends, _ = self._copies(*refs)
        for cp in local + sends:
            cp.start()

    def wait(self, *refs):
        local, sends, arrivals = self._copies(*refs)
        for cp in arrivals:
            cp.wait_recv()
        for cp in sends:
            cp.wait_send()
        for cp in local:
            cp.wait()


def carried(body, exchange, n_in, n_out, first_last):
    if exchange is None:
        return body, [], [], [], []
    nx = exchange.n

    def wrapped(*refs):
        ins, xin = refs[:n_in], refs[n_in:n_in + nx]
        outs, xout = refs[n_in + nx:n_in + nx + n_out], refs[n_in + nx + n_out:n_in + 2 * nx + n_out]
        scratch, sems = refs[n_in + 2 * nx + n_out:-3], refs[-3:]
        first, last = first_last()

        @pl.when(first)
        def _():
            exchange.start(xin, xout, *sems)

        body(*ins, *outs, *scratch)

        @pl.when(last)
        def _():
            exchange.wait(xin, xout, *sems)

    return wrapped, [_ANY] * nx, [_ANY] * nx, exchange.out_shapes(), exchange.scratch()


def exchange_only(exchange, name):
    def body(*refs):
        xin, xout, sems = refs[:exchange.n], refs[exchange.n:2 * exchange.n], refs[-3:]
        exchange.start(xin, xout, *sems)
        exchange.wait(xin, xout, *sems)

    return pl.pallas_call(
        body, name=name, in_specs=[_ANY] * exchange.n, out_specs=[_ANY] * exchange.n,
        out_shape=exchange.out_shapes(), scratch_shapes=exchange.scratch(),
    )(*exchange.arrays)


def _residual_out(a, w_out, x, next_g, name, **tiles):
    if next_g is None:
        (y,) = matmul(a, w_out, "nn", name, [F32], epilogue=lambda acc, r: (r + acc,), extras=(x,), **tiles)
        return y, None

    def add_and_norm(acc, r, g):
        y = r + acc
        return y, y * lax.rsqrt(jnp.mean(y * y, axis=-1, keepdims=True) + EPS) * g

    assert w_out.shape[1] <= tiles.get("tn", 1024)
    return matmul(a, w_out, "nn", name, [F32, BF16], epilogue=add_and_norm, extras=(x, next_g.reshape(1, -1)), **tiles)


def attention_fwd(x, h, w_in, sink, qn_g, kn_g, w_out, tables, next_g, tag, exchange=None):
    (proj,) = matmul(h, w_in, "nn", f"{tag}_proj", [F32], tn=ATT_IN)
    qa, ka, va, qb, qb_t, kb, vb, vb_t = qkv_post_fwd(proj, tables, qn_g, kn_g, f"{tag}_qkv")
    cat, lse_a = window_attn_fwd(qa, ka, va, sink, f"{tag}_win")
    cat, lse_b, *arrived = flash_attn_fwd(qb, kb, vb_t, cat, f"{tag}_flash", exchange)
    if callable(w_out):
        w_out = w_out(arrived)
    y, h_next = _residual_out(cat, w_out, x, next_g, f"{tag}_out")
    saved = (x, h, proj, qa, ka, va, qb, qb_t, kb, vb, cat, lse_a, lse_b)
    return y, h_next, saved, arrived


def attention_bwd(dy, dyb, saved, norm_g, w_in, sink, qn_g, kn_g, w_out, tables, tag, exchange_with=None):
    x, h, proj, qa, ka, va, qb, qb_t, kb, vb, cat, lse_a, lse_b = saved
    (dcat,) = matmul(dyb, w_out, "nt", f"{tag}_dcat", [BF16])
    (dw_out,) = matmul(cat, dyb, "tn", f"{tag}_dwout", [BF16], tk=4096)
    delta, dob, dob_t = attn_delta(cat, dcat, f"{tag}_delta")
    dqa, dka, dva, dsink = window_attn_bwd(qa, ka, va, sink, dcat, lse_a, delta, f"{tag}_dwin")
    exchange = exchange_with(dw_out) if exchange_with else None
    dqb, dkb, dvb, *arrived = flash_attn_bwd(qb, qb_t, kb, vb, dob, dob_t, lse_b, delta, f"{tag}_dflash", exchange)
    dproj, dqg, dkg = qkv_post_bwd(proj, tables, qn_g, kn_g, dqa, dka, dva, dqb, dkb, dvb, f"{tag}_dqkv")
    (dw_in,) = matmul(h, dproj, "tn", f"{tag}_dwin_w", [BF16], tn=ATT_IN // 2, tk=2048)
    dx, dxb, dg = matmul_nt_normbwd(dproj, w_in, x, norm_g, dy, f"{tag}_dx")
    grp = A_HEADS // A_KV_HEADS
    small = dict(norm=dg[0], sink=dsink[:, :grp, 0].reshape(A_HEADS), qnorm=dqg[0, :HEAD_DIM], knorm=dkg[0, :HEAD_DIM])
    return dx, dxb, dw_in, dw_out, small, arrived


def sgu_fwd(x, h, w_in, ln_g, ln_b, w_s, b_rows, w_out, next_g, tag):
    apre, z = matmul(h, w_in, "nn", f"{tag}_in", [BF16, BF16], epilogue=lambda acc: (acc, _gelu(acc)))
    y = sgu_mix_fwd(z, ln_g, ln_b, w_s, b_rows, f"{tag}_mix")
    out, h_next = _residual_out(y, w_out, x, next_g, f"{tag}_out")
    return out, h_next, (x, h, apre, z, y)


def sgu_bwd(dout, doutb, saved, norm_g, w_in, ln_g, ln_b, w_s, b_rows, w_out, tag):
    x, h, apre, z, y = saved
    (dy,) = matmul(doutb, w_out, "nt", f"{tag}_dy", [BF16])
    (dw_out,) = matmul(y, doutb, "tn", f"{tag}_dwout", [BF16], tk=4096)
    dapre, dlg, dlb, dws, dbs = sgu_mix_bwd(z, apre, dy, ln_g, ln_b, w_s, b_rows, f"{tag}_dmix")
    (dw_in,) = matmul(h, dapre, "tn", f"{tag}_dwin", [BF16], out_shards=True, tk=4096)
    dx, dxb, dg = matmul_nt_normbwd(dapre, w_in, x, norm_g, dout, f"{tag}_dx")
    small = dict(norm=dg[0], ln_g=dlg[0], ln_b=dlb[0], w_s=dws, b_s=dbs)
    return dx, dxb, dw_in, dw_out, small


def _square(r):
    return r * r


def mlp_fwd(x, h, w1, w2, next_g, tag):
    (r,) = matmul(h, w1, "nn", f"{tag}_up", [BF16], epilogue=lambda acc: (jnp.maximum(acc, 0.0),), tm=2048)
    y, h_next = _residual_out(r, w2, x, next_g, f"{tag}_down", a_fn=_square, tm=512, tk=4096)
    return y, h_next, (x, h, r)


def mlp_bwd(dy, dyb, saved, norm_g, w1, w2, tag):
    x, h, r = saved
    (da,) = matmul(dyb, w2, "nt", f"{tag}_da", [BF16], epilogue=lambda acc, rr: (acc * (2.0 * rr.astype(F32)),), extras=(r,),
                   tm=2048)
    (dw2,) = matmul(r, dyb, "tn", f"{tag}_dw2", [BF16], a_fn=_square, tk=4096)
    (dw1,) = matmul(h, da, "tn", f"{tag}_dw1", [BF16], out_shards=True, tk=4096)
    dx, dxb, dg = matmul_nt_normbwd(da, w1, x, norm_g, dy, f"{tag}_dx")
    return dx, dxb, dw1, dw2, dg[0]


ORDER = ("att_norm", "att_w_in", "att_sink", "att_qnorm", "att_knorm", "att_w_out", "sgu_norm", "sgu_w_in", "sgu_ln_g",
         "sgu_ln_b", "sgu_w_s", "sgu_b_s", "sgu_w_out", "mlp_norm", "mlp_w1", "mlp_w2", "final_norm")
SHARDED = ("att_w_in", "att_w_out", "sgu_w_in", "sgu_w_out", "mlp_w1", "mlp_w2")
SGU_VECS = ("sgu_norm", "sgu_ln_g", "sgu_ln_b")
SMALL_EARLY = ("sgu_w_s", "sgu_b_s", "mlp_norm", "final_norm", "loss")
SMALL_LATE = ("att_norm", "att_sink", "att_qnorm", "att_knorm")
SMALL_ROWS_MULT = 8


def _flat(blocks, names):
    flat = jnp.concatenate([blocks[n].reshape(-1).astype(F32) for n in names])
    per = SMALL_ROWS_MULT * FLAT_COLS
    total = -(-flat.shape[0] // per) * per
    return jnp.pad(flat, (0, total - flat.shape[0])).reshape(1, total // FLAT_COLS, FLAT_COLS)


def _unflat(flat, like, names):
    out, off = {}, 0
    f = flat.reshape(-1)
    for n in names:
        size = like[n].size
        out[n] = f[off:off + size].reshape(like[n].shape)
        off += size
    return out


def kernel(x, att_norm, att_w_in, att_sink, att_qnorm, att_knorm, att_w_out, sgu_norm, sgu_w_in, sgu_ln_g, sgu_ln_b, sgu_w_s, sgu_b_s, sgu_w_out, mlp_norm, mlp_w1, mlp_w2, final_norm, loss_target, m_att_norm, m_att_w_in, m_att_sink, m_att_qnorm, m_att_knorm, m_att_w_out, m_sgu_norm, m_sgu_w_in, m_sgu_ln_g, m_sgu_ln_b, m_sgu_w_s, m_sgu_b_s, m_sgu_w_out, m_mlp_norm, m_mlp_w1, m_mlp_w2, m_final_norm, v_att_norm, v_att_w_in, v_att_sink, v_att_qnorm, v_att_knorm, v_att_w_out, v_sgu_norm, v_sgu_w_in, v_sgu_ln_g, v_sgu_ln_b, v_sgu_w_s, v_sgu_b_s, v_sgu_w_out, v_mlp_norm, v_mlp_w1, v_mlp_w2, v_final_norm):
    w = dict(att_norm=att_norm, att_w_in=att_w_in, att_sink=att_sink, att_qnorm=att_qnorm, att_knorm=att_knorm,
             att_w_out=att_w_out, sgu_norm=sgu_norm, sgu_w_in=sgu_w_in, sgu_ln_g=sgu_ln_g, sgu_ln_b=sgu_ln_b, sgu_w_s=sgu_w_s,
             sgu_b_s=sgu_b_s, sgu_w_out=sgu_w_out, mlp_norm=mlp_norm, mlp_w1=mlp_w1, mlp_w2=mlp_w2, final_norm=final_norm)
    m = dict(att_norm=m_att_norm, att_w_in=m_att_w_in, att_sink=m_att_sink, att_qnorm=m_att_qnorm, att_knorm=m_att_knorm,
             att_w_out=m_att_w_out, sgu_norm=m_sgu_norm, sgu_w_in=m_sgu_w_in, sgu_ln_g=m_sgu_ln_g, sgu_ln_b=m_sgu_ln_b,
             sgu_w_s=m_sgu_w_s, sgu_b_s=m_sgu_b_s, sgu_w_out=m_sgu_w_out, mlp_norm=m_mlp_norm, mlp_w1=m_mlp_w1, mlp_w2=m_mlp_w2,
             final_norm=m_final_norm)
    v = dict(att_norm=v_att_norm, att_w_in=v_att_w_in, att_sink=v_att_sink, att_qnorm=v_att_qnorm, att_knorm=v_att_knorm,
             att_w_out=v_att_w_out, sgu_norm=v_sgu_norm, sgu_w_in=v_sgu_w_in, sgu_ln_g=v_sgu_ln_g, sgu_ln_b=v_sgu_ln_b,
             sgu_w_s=v_sgu_w_s, sgu_b_s=v_sgu_b_s, sgu_w_out=v_sgu_w_out, mlp_norm=v_mlp_norm, mlp_w1=v_mlp_w1, mlp_w2=v_mlp_w2,
             final_norm=v_final_norm)
    loss, grad_x, g, d, nm, nv = train_step(x[0], loss_target[0], w, m, v)
    return (loss, grad_x[None], *[g[n] for n in ORDER], *[d[n] for n in ORDER], *[nm[n] for n in ORDER], *[nv[n] for n in ORDER])


def train_step(x, target, w, m, v):
    t, d_model = x.shape
    n_att, n_sgu, depth = w["att_w_in"].shape[0], w["sgu_w_in"].shape[0], w["mlp_w1"].shape[0]
    bf = lambda n: w[n].astype(BF16)

    vec_local = jnp.stack([w[n] for n in SGU_VECS], axis=1)
    att_in = bf("att_w_in")
    g_in0, g_vec = exchange_only(Exchange(gathers=[att_in[:1], vec_local]), "gather_first")
    vecs = g_vec.transpose(1, 2, 0, 3).reshape(n_sgu, len(SGU_VECS), -1)
    rest = Exchange(gathers=[att_in[1:], bf("att_w_out"), bf("sgu_w_in"), bf("sgu_w_out"), bf("mlp_w1"), bf("mlp_w2")])
    w_s_bf = w["sgu_w_s"].astype(BF16)
    b_rows = jnp.broadcast_to(w["sgu_b_s"][:, :, :, None], w["sgu_b_s"].shape + (LANES,))
    tables = _rope_tables(t)
    full_cols = lambda g: g.transpose(1, 2, 0, 3).reshape(g.shape[1], d_model, -1)

    mixer_norm = lambda layer: w["att_norm"][layer // 2] if layer % 2 == 0 else vecs[layer // 2, 0]
    saved = []
    h = rmsnorm_fwd(x, mixer_norm(0), "att0_norm")
    for layer in range(depth):
        i = layer // 2
        if layer % 2 == 0:
            if layer == 0:
                att_w_in = [full_cols(g_in0)[0]]
            x, h, sv, arrived = attention_fwd(x, h, att_w_in[i], w["att_sink"][i], w["att_qnorm"][i], w["att_knorm"][i],
                                              (lambda arrived: Gathered(arrived[1], "row", 0)) if layer == 0 else att_w_out[i],
                                              tables, w["mlp_norm"][layer], f"att{i}", rest if layer == 0 else None)
            if layer == 0:
                g_in1, g_out, g_sgu_in, g_sgu_out, g_w1, g_w2 = arrived
                att_w_in += list(full_cols(g_in1))
                att_w_out = [Gathered(g_out, "row", l) for l in range(n_att)]
        else:
            x, h, sv = sgu_fwd(x, h, Gathered(g_sgu_in, "col", i), vecs[i, 1], vecs[i, 2], w_s_bf[i], b_rows[i],
                               Gathered(g_sgu_out, "row", i), w["mlp_norm"][layer], f"sgu{i}")
        x, h, sm = mlp_fwd(x, h, Gathered(g_w1, "col", layer), Gathered(g_w2, "row", layer),
                           mixer_norm(layer + 1) if layer + 1 < depth else None, f"mlp{layer}")
        saved.append((sv, sm))
    loss_row, dh, dhb, dgf = loss_head(x, w["final_norm"], target, "loss_head")

    queue, recv = [], {}
    gs = dict(att_norm=[None] * n_att, att_sink=[None] * n_att, att_qnorm=[None] * n_att, att_knorm=[None] * n_att,
              sgu_w_s=[None] * n_sgu, sgu_b_s=[None] * n_sgu, mlp_norm=[None] * depth)

    def row_slabs(g):
        return g.reshape(N_DEV, g.shape[0] // N_DEV, g.shape[1])

    def col_slabs(g):
        return g.reshape(g.shape[0], N_DEV, g.shape[1] // N_DEV).transpose(1, 0, 2)

    def take_queue(gathers=()):
        items = list(queue)
        queue.clear()
        keys = [k for k, _ in gathers] + [k for k, _ in items]
        return Exchange(gathers=[a for _, a in gathers], scatters=[a for _, a in items]), keys

    def small_early():
        blocks = dict(sgu_w_s=jnp.stack(gs["sgu_w_s"]), sgu_b_s=jnp.stack(gs["sgu_b_s"]), mlp_norm=jnp.stack(gs["mlp_norm"]),
                      final_norm=dgf[0], loss=loss_row[0, :1])
        return _flat(blocks, SMALL_EARLY)[0]

    for layer in reversed(range(depth)):
        i = layer // 2
        sv, sm = saved[layer]
        dh, dhb, dw1, dw2, gs["mlp_norm"][layer] = mlp_bwd(
            dh, dhb, sm, w["mlp_norm"][layer], Gathered(g_w1, "col", layer), Gathered(g_w2, "row", layer), f"mlp{layer}")
        queue += [(("mlp_w1", layer), dw1), (("mlp_w2", layer), row_slabs(dw2))]
        if layer % 2 == 0:
            keys = []

            def exchange_with(dw_out, i=i, layer=layer, keys=keys):
                if layer == 0:
                    queue.append((("att_w_out", i), row_slabs(dw_out)))
                ex, got = take_queue([("small_early", small_early())] if layer == 0 else ())
                keys += got
                return ex

            dh, dhb, dw_in, dw_out, sm_g, arrived = attention_bwd(
                dh, dhb, sv, w["att_norm"][i], att_w_in[i], w["att_sink"][i], w["att_qnorm"][i], w["att_knorm"][i],
                att_w_out[i], tables, f"att{i}", exchange_with)
            recv.update(zip(keys, arrived))
            queue.append((("att_w_in", i), col_slabs(dw_in)))
            if layer != 0:
                queue.append((("att_w_out", i), row_slabs(dw_out)))
            gs["att_norm"][i], gs["att_sink"][i] = sm_g["norm"], sm_g["sink"]
            gs["att_qnorm"][i], gs["att_knorm"][i] = sm_g["qnorm"], sm_g["knorm"]
        else:
            dh, dhb, dw_in, dw_out, sm_g = sgu_bwd(
                dh, dhb, sv, vecs[i, 0], Gathered(g_sgu_in, "col", i), vecs[i, 1], vecs[i, 2], w_s_bf[i], b_rows[i],
                Gathered(g_sgu_out, "row", i), f"sgu{i}")
            dvec = jnp.stack([sm_g["norm"], sm_g["ln_g"], sm_g["ln_b"]])
            queue += [(("sgu_w_in", i), dw_in), (("sgu_w_out", i), row_slabs(dw_out)), (("sgu_vecs", i), col_slabs(dvec))]
            gs["sgu_w_s"][i], gs["sgu_b_s"][i] = sm_g["w_s"], sm_g["b_s"]
    grad_x = dh
    late = dict(att_norm=jnp.stack(gs["att_norm"]), att_sink=jnp.stack(gs["att_sink"]), att_qnorm=jnp.stack(gs["att_qnorm"]),
                att_knorm=jnp.stack(gs["att_knorm"]))
    last, keys = take_queue([("small_late", _flat(late, SMALL_LATE)[0])])
    recv.update(zip(keys, exchange_only(last, "exchange_last")))

    outs = [{}, {}, {}, {}]
    for n in SHARDED:
        res = adamw([recv[(n, l)] for l in range(w[n].shape[0])], w[n], m[n], v[n], f"adamw_{n}")
        for o, r in zip(outs, res):
            o[n] = r
    stack_vecs = lambda src: jnp.stack([src[n] for n in SGU_VECS], axis=1)
    res = adamw([recv[("sgu_vecs", i)] for i in range(n_sgu)], stack_vecs(w), stack_vecs(m), stack_vecs(v), "adamw_sgu_vecs")
    for o, r in zip(outs, res):
        o.update({n: r[:, k] for k, n in enumerate(SGU_VECS)})
    zero = {"loss": jnp.zeros((1,), F32)}
    for names, key in ((SMALL_EARLY, "small_early"), (SMALL_LATE, "small_late")):
        res = adamw([recv[key]], _flat({**w, **zero}, names), _flat({**m, **zero}, names), _flat({**v, **zero}, names), f"adamw_{key}")
        for o, r in zip(outs, res):
            o.update(_unflat(r, {**w, **zero}, names))
    loss = outs[0]["loss"][0]
    return loss, grad_x, *outs
```

```python
import functools
import math

import jax
import jax.numpy as jnp
from jax import lax
from jax.experimental import pallas as pl
from jax.experimental.pallas import tpu as pltpu

F32 = jnp.float32
BF16 = jnp.bfloat16

HEAD_DIM = 64
A_HEADS = 8
A_KV_HEADS = 2
B_HEADS = 8
B_KV_HEADS = 2
WINDOW = 128
BLOCK = 128
ROPE_THETA = 10000.0
GRID_W = 64
SGU_GROUPS = 8
SGU_CHUNK = 128
EPS = 1e-6
SCALE = HEAD_DIM ** -0.5
NEG = -1e30
LOG2E = math.log2(math.e)
LN2 = math.log(2.0)

A_Q = A_HEADS * HEAD_DIM
A_KV = A_KV_HEADS * HEAD_DIM
B_Q = B_HEADS * HEAD_DIM
B_KV = B_KV_HEADS * HEAD_DIM
OFF_QA, OFF_KA, OFF_VA = 0, A_Q, A_Q + A_KV
OFF_QB = A_Q + 2 * A_KV
OFF_KB = OFF_QB + B_Q
OFF_VB = OFF_KB + B_KV
ATT_IN = OFF_VB + B_KV

ADAM_LR = 0.001
ADAM_B1 = 0.9
ADAM_B2 = 0.999
ADAM_EPS = 1e-08
ADAM_WD = 0.01
ADAM_STEP = 10

N_DEV = 8
LANES = 128
V7X_VMEM_LIMIT = 56 * 1024 * 1024
FLAT_COLS = 1024


def _cparams(sem, vmem=V7X_VMEM_LIMIT):
    return pltpu.CompilerParams(dimension_semantics=sem, vmem_limit_bytes=vmem)


def _dot_nn(a, b):
    return lax.dot_general(a, b, (((1,), (0,)), ((), ())), preferred_element_type=F32)


def _dot_nt(a, b):
    return lax.dot_general(a, b, (((1,), (1,)), ((), ())), preferred_element_type=F32)


def _dot_tn(a, b):
    return lax.dot_general(a, b, (((0,), (0,)), ((), ())), preferred_element_type=F32)


def _bf(x):
    return x if x.dtype == BF16 else x.astype(BF16)


def _lane(shape):
    return lax.broadcasted_iota(jnp.int32, shape, len(shape) - 1)


def _seg_matrix(rows_lo, rows_hi):
    r = lax.broadcasted_iota(jnp.int32, (LANES, LANES), 0)
    return jnp.where((r >= rows_lo) & (r < rows_hi), 1.0, 0.0).astype(BF16)


def _group_matrix(width):
    r = lax.broadcasted_iota(jnp.int32, (LANES, LANES), 0)
    c = lax.broadcasted_iota(jnp.int32, (LANES, LANES), 1)
    return jnp.where((r // width) == (c // width), 1.0, 0.0).astype(BF16)


def _dot_f32_by_ones(s, ones_bf16):
    hi = s.astype(BF16)
    lo = (s - hi.astype(F32)).astype(BF16)
    return _dot_nn(hi, ones_bf16) + _dot_nn(lo, ones_bf16)


def _swap_halves(x, width):
    half = width // 2
    first = (_lane(x.shape) % width) < half
    return jnp.where(first, pltpu.roll(x, LANES - half, 1), pltpu.roll(x, half, 1))


def rmsnorm_fwd(x, g, name):
    t, d = x.shape
    tm = min(t, 512)

    def body(x_ref, g_ref, h_ref):
        xf = x_ref[...]
        r = lax.rsqrt(jnp.mean(xf * xf, axis=-1, keepdims=True) + EPS)
        h_ref[...] = (xf * r * g_ref[...]).astype(BF16)

    return pl.pallas_call(
        body, name=name, grid=(t // tm,),
        in_specs=[pl.BlockSpec((tm, d), lambda i: (i, 0)), pl.BlockSpec((1, d), lambda i: (0, 0))],
        out_specs=pl.BlockSpec((tm, d), lambda i: (i, 0)),
        out_shape=jax.ShapeDtypeStruct((t, d), BF16),
        compiler_params=_cparams(("parallel",)),
    )(x, g.reshape(1, d))


def _fit(n, want):
    t = min(n, want)
    while n % t:
        t //= 2
    return t


class Gathered:
    def __init__(self, arr, kind, layer):
        self.arr, self.kind, self.layer = arr, kind, layer
        _, _, self.rows, self.cols = arr.shape
        self.shape = (N_DEV * self.rows, self.cols) if kind == "row" else (self.rows, N_DEV * self.cols)


def _b_operand(b, mode, tn, tk, idx):
    dot = {"nn": _dot_nn, "nt": _dot_nt, "tn": _dot_tn}[mode]
    if not isinstance(b, Gathered):
        if mode == "nt":
            spec = pl.BlockSpec((tn, tk), lambda *g: idx(*g))
        else:
            spec = pl.BlockSpec((tk, tn), lambda *g: idx(*g)[::-1])
        return b, spec, lambda av, ref: dot(av, _bf(ref[...]))
    lay, rows, cols = b.layer, b.rows, b.cols
    if mode == "nn" and b.kind == "col":
        s = tn // cols
        assert s * cols == tn
        spec = pl.BlockSpec((s, None, tk, cols), lambda *g: (idx(*g)[0], lay, idx(*g)[1], 0))
        return b.arr, spec, lambda av, ref: jnp.concatenate([_dot_nn(av, ref[c]) for c in range(s)], axis=1)
    if mode == "nn" and b.kind == "row":
        s = tk // rows
        assert s * rows == tk
        spec = pl.BlockSpec((s, None, rows, tn), lambda *g: (idx(*g)[1], lay, 0, idx(*g)[0]))
        return b.arr, spec, lambda av, ref: _dot_nn(av, ref[...].reshape(s * rows, tn))
    if mode == "nt" and b.kind == "row":
        s = tn // rows
        assert s * rows == tn
        spec = pl.BlockSpec((s, None, rows, tk), lambda *g: (idx(*g)[0], lay, 0, idx(*g)[1]))
        return b.arr, spec, lambda av, ref: _dot_nt(av, ref[...].reshape(s * rows, tk))
    if mode == "nt" and b.kind == "col":
        s = tk // cols
        assert s * cols == tk
        spec = pl.BlockSpec((s, None, tn, cols), lambda *g: (idx(*g)[1], lay, idx(*g)[0], 0))

        def prod(av, ref):
            tot = _dot_nt(av[:, :cols], ref[0])
            for c in range(1, s):
                tot = tot + _dot_nt(av[:, c * cols:(c + 1) * cols], ref[c])
            return tot

        return b.arr, spec, prod
    raise NotImplementedError((mode, b.kind))


def matmul(a, b, mode, name, out_dtypes, epilogue=None, extras=(), a_fn=None, out_shards=False, tm=1024, tn=1024, tk=1024):
    (m, k) = a.shape[::-1] if mode == "tn" else a.shape
    n = b.shape[0] if mode == "nt" else b.shape[1]
    if out_shards:
        tn = n // N_DEV
    tm, tn, tk = _fit(m, tm), _fit(n, tn), _fit(k, tk)
    nk = k // tk
    n_ex, n_out = len(extras), len(out_dtypes)
    if epilogue is None:
        epilogue = lambda acc: (acc,)
    b_arr, b_spec, prod = _b_operand(b, mode, tn, tk, lambda i, j, kk: (j, kk))

    def body(*refs):
        a_ref, b_ref = refs[0], refs[1]
        ex_refs = refs[2:2 + n_ex]
        out_refs = refs[2 + n_ex:2 + n_ex + n_out]
        acc_ref = refs[2 + n_ex + n_out] if nk > 1 else None
        kk = pl.program_id(2)
        av = _bf(a_ref[...])
        if a_fn is not None:
            av = a_fn(av)
        part = prod(av, b_ref)

        def finish(acc):
            outs = epilogue(acc, *[r[...] for r in ex_refs])
            for r, o in zip(out_refs, outs):
                r[...] = o.astype(r.dtype)

        if nk == 1:
            finish(part)
            return

        @pl.when(kk == 0)
        def _():
            acc_ref[...] = part

        @pl.when(kk > 0)
        def _():
            acc_ref[...] += part

        @pl.when(kk == nk - 1)
        def _():
            finish(acc_ref[...])

    if mode == "tn":
        a_spec = pl.BlockSpec((tk, tm), lambda i, j, kk: (kk, i))
    else:
        a_spec = pl.BlockSpec((tm, tk), lambda i, j, kk: (i, kk))
    mn_spec = pl.BlockSpec((tm, tn), lambda i, j, kk: (i, j))
    row_spec = pl.BlockSpec((1, tn), lambda i, j, kk: (0, j))
    if out_shards:
        out_spec = pl.BlockSpec((None, tm, tn), lambda i, j, kk: (j, i, 0))
        out_shape = [jax.ShapeDtypeStruct((N_DEV, m, tn), dt) for dt in out_dtypes]
    else:
        out_spec = mn_spec
        out_shape = [jax.ShapeDtypeStruct((m, n), dt) for dt in out_dtypes]
    outs = pl.pallas_call(
        body, name=name, grid=(m // tm, n // tn, nk),
        in_specs=[a_spec, b_spec] + [row_spec if e.shape[0] == 1 else mn_spec for e in extras],
        out_specs=[out_spec] * n_out,
        out_shape=out_shape,
        scratch_shapes=[pltpu.VMEM((tm, tn), F32)] if nk > 1 else [],
        compiler_params=_cparams(("parallel", "parallel", "arbitrary")),
    )(a, b_arr, *extras)
    return outs


def matmul_nt_normbwd(dz, w, x, g, dres, name, tm=512):
    m, k = dz.shape
    d = w.shape[0]
    tm = _fit(m, tm)
    w_arr, w_spec, prod = _b_operand(w, "nt", d, k, lambda i: (0, 0))

    def body(dz_ref, w_ref, x_ref, g_ref, dres_ref, dx_ref, dxb_ref, dg_ref):
        @pl.when(pl.program_id(0) == 0)
        def _():
            dg_ref[...] = jnp.zeros_like(dg_ref)

        dh = prod(_bf(dz_ref[...]), w_ref)
        xf = x_ref[...]
        r = lax.rsqrt(jnp.mean(xf * xf, axis=-1, keepdims=True) + EPS)
        xhat = xf * r
        dg_ref[...] += jnp.sum(dh * xhat, axis=0, keepdims=True)
        dxh = dh * g_ref[...]
        dx = r * (dxh - xhat * jnp.mean(dxh * xhat, axis=-1, keepdims=True))
        out = dres_ref[...] + dx
        dx_ref[...] = out
        dxb_ref[...] = out.astype(BF16)

    row = pl.BlockSpec((tm, d), lambda i: (i, 0))
    vec = pl.BlockSpec((1, d), lambda i: (0, 0))
    return pl.pallas_call(
        body, name=name, grid=(m // tm,),
        in_specs=[pl.BlockSpec((tm, k), lambda i: (i, 0)), w_spec, row, vec, row],
        out_specs=[row, row, vec],
        out_shape=[jax.ShapeDtypeStruct((m, d), F32), jax.ShapeDtypeStruct((m, d), BF16), jax.ShapeDtypeStruct((1, d), F32)],
        compiler_params=_cparams(("arbitrary",)),
    )(dz, w_arr, x, g.reshape(1, d), dres)


def _rope_tables(t):
    pos = lax.broadcasted_iota(jnp.int32, (t, LANES), 0)
    dim = lax.broadcasted_iota(jnp.int32, (t, LANES), 1) % HEAD_DIM

    def table(p, width):
        i = dim % (width // 2)
        ang = p.astype(F32) * (ROPE_THETA ** (-(2 * i).astype(F32) / width))
        return jnp.cos(ang), jnp.where(dim % width < width // 2, -jnp.sin(ang), jnp.sin(ang))

    cos_a, sin_a = table(pos, HEAD_DIM)
    cos_b, sin_b = table(jnp.where(dim < HEAD_DIM // 2, pos // GRID_W, pos % GRID_W), HEAD_DIM // 2)
    return cos_a, sin_a, cos_b, sin_b


def _headnorm(xs, gmat):
    return lax.rsqrt(_dot_f32_by_ones(xs * xs, gmat) * (1.0 / HEAD_DIM) + EPS)


def qkv_post_fwd(proj, tables, qn_g, kn_g, name):
    t = proj.shape[0]
    tm = min(t, 256)
    cos_a, sin_a, cos_b, sin_b = tables
    g2 = lambda g: jnp.concatenate([g, g]).reshape(1, LANES)
    grp = B_HEADS // B_KV_HEADS

    def body(p_ref, ca_ref, sa_ref, cb_ref, sb_ref, qg_ref, kg_ref, qa_ref, ka_ref, va_ref, qb_ref, qbt_ref, kb_ref, vb_ref):
        ca, sa, cb, sb = ca_ref[...], sa_ref[...], cb_ref[...], sb_ref[...]
        gmat = _group_matrix(HEAD_DIM)

        def rope_a(xs):
            return xs * ca + _swap_halves(xs, HEAD_DIM) * sa

        def norm_rope_b(xs, g):
            y = xs * _headnorm(xs, gmat) * g
            return y * cb + _swap_halves(y, HEAD_DIM // 2) * sb

        for c in range(A_Q // LANES):
            qa_ref[:, c * LANES:(c + 1) * LANES] = rope_a(p_ref[:, OFF_QA + c * LANES:OFF_QA + (c + 1) * LANES]).astype(BF16)
        ka_ref[...] = rope_a(p_ref[:, OFF_KA:OFF_KA + LANES]).astype(BF16)
        va_ref[...] = p_ref[:, OFF_VA:OFF_VA + LANES].astype(BF16)
        for c in range(B_Q // LANES):
            y = norm_rope_b(p_ref[:, OFF_QB + c * LANES:OFF_QB + (c + 1) * LANES], qg_ref[...]) * (SCALE * LOG2E)
            for half in range(2):
                head = 2 * c + half
                placed = _head_to_half(y, half, head // grp)
                qb_ref[head] = placed.astype(BF16)
                qbt_ref[head] = placed.T.astype(BF16)
        kb_ref[...] = norm_rope_b(p_ref[:, OFF_KB:OFF_KB + LANES], kg_ref[...]).astype(BF16)
        vb_ref[...] = p_ref[:, OFF_VB:OFF_VB + LANES].astype(BF16)

    tab = pl.BlockSpec((tm, LANES), lambda i: (i, 0))
    vec = pl.BlockSpec((1, LANES), lambda i: (0, 0))
    wide = pl.BlockSpec((tm, A_Q), lambda i: (i, 0))
    return pl.pallas_call(
        body, name=name, grid=(t // tm,),
        in_specs=[pl.BlockSpec((tm, ATT_IN), lambda i: (i, 0)), tab, tab, tab, tab, vec, vec],
        out_specs=[wide, tab, tab, pl.BlockSpec((B_HEADS, tm, LANES), lambda i: (0, i, 0)),
                   pl.BlockSpec((B_HEADS, LANES, tm), lambda i: (0, 0, i)), tab, tab],
        out_shape=[jax.ShapeDtypeStruct((t, A_Q), BF16), jax.ShapeDtypeStruct((t, LANES), BF16),
                   jax.ShapeDtypeStruct((t, LANES), BF16), jax.ShapeDtypeStruct((B_HEADS, t, LANES), BF16),
                   jax.ShapeDtypeStruct((B_HEADS, LANES, t), BF16), jax.ShapeDtypeStruct((t, LANES), BF16),
                   jax.ShapeDtypeStruct((t, LANES), BF16)],
        compiler_params=_cparams(("parallel",)),
    )(proj, cos_a, sin_a, cos_b, sin_b, g2(qn_g), g2(kn_g))


def qkv_post_bwd(proj, tables, qn_g, kn_g, dqa, dka, dva, dqb, dkb, dvb, name):
    t = proj.shape[0]
    tm = min(t, 256)
    cos_a, sin_a, cos_b, sin_b = tables
    g2 = lambda g: jnp.concatenate([g, g]).reshape(1, LANES)

    def body(p_ref, ca_ref, sa_ref, cb_ref, sb_ref, qg_ref, kg_ref, dqa_ref, dka_ref, dva_ref, dqb_ref, dkb_ref, dvb_ref,
             dp_ref, dqg_ref, dkg_ref):
        ca, sa, cb, sb = ca_ref[...], sa_ref[...], cb_ref[...], sb_ref[...]
        gmat = _group_matrix(HEAD_DIM)

        @pl.when(pl.program_id(0) == 0)
        def _():
            dqg_ref[...] = jnp.zeros_like(dqg_ref)
            dkg_ref[...] = jnp.zeros_like(dkg_ref)

        def rope_a_bwd(dy):
            return dy * ca + _swap_halves(dy * sa, HEAD_DIM)

        def norm_rope_b_bwd(dout, xs, g):
            dy = dout * cb + _swap_halves(dout * sb, HEAD_DIM // 2)
            r = _headnorm(xs, gmat)
            xhat = xs * r
            dxh = dy * g
            mean = _dot_f32_by_ones(dxh * xhat, gmat) * (1.0 / HEAD_DIM)
            return r * (dxh - xhat * mean), jnp.sum(dy * xhat, axis=0, keepdims=True)

        for c in range(A_Q // LANES):
            sl = slice(c * LANES, (c + 1) * LANES)
            dp_ref[:, OFF_QA + c * LANES:OFF_QA + (c + 1) * LANES] = rope_a_bwd(dqa_ref[:, sl].astype(F32)).astype(BF16)
        dp_ref[:, OFF_KA:OFF_KA + LANES] = rope_a_bwd(dka_ref[0] + dka_ref[1]).astype(BF16)
        dp_ref[:, OFF_VA:OFF_VA + LANES] = (dva_ref[0] + dva_ref[1]).astype(BF16)
        dqg = jnp.zeros((1, LANES), F32)
        for c in range(B_Q // LANES):
            sl = slice(c * LANES, (c + 1) * LANES)
            dx, dg = norm_rope_b_bwd(dqb_ref[:, sl].astype(F32), p_ref[:, OFF_QB + c * LANES:OFF_QB + (c + 1) * LANES], qg_ref[...])
            dp_ref[:, OFF_QB + c * LANES:OFF_QB + (c + 1) * LANES] = dx.astype(BF16)
            dqg = dqg + dg
        dqg_ref[...] += dqg
        dx, dg = norm_rope_b_bwd((dkb_ref[0] + dkb_ref[1]).T, p_ref[:, OFF_KB:OFF_KB + LANES], kg_ref[...])
        dp_ref[:, OFF_KB:OFF_KB + LANES] = dx.astype(BF16)
        dkg_ref[...] += dg
        dp_ref[:, OFF_VB:OFF_VB + LANES] = (dvb_ref[0] + dvb_ref[1]).T.astype(BF16)

        @pl.when(pl.program_id(0) == t // tm - 1)
        def _():
            dqg_ref[...] = dqg_ref[...] + pltpu.roll(dqg_ref[...], HEAD_DIM, 1)
            dkg_ref[...] = dkg_ref[...] + pltpu.roll(dkg_ref[...], HEAD_DIM, 1)

    tab = pl.BlockSpec((tm, LANES), lambda i: (i, 0))
    vec = pl.BlockSpec((1, LANES), lambda i: (0, 0))
    wide = pl.BlockSpec((tm, A_Q), lambda i: (i, 0))
    slab = pl.BlockSpec((2, tm, LANES), lambda i: (0, i, 0))
    per_chunk = dkb.shape[3] // tm
    slab_t = pl.BlockSpec((2, None, LANES, tm), lambda i: (0, i // per_chunk, 0, i % per_chunk))
    full = pl.BlockSpec((tm, ATT_IN), lambda i: (i, 0))
    return pl.pallas_call(
        body, name=name, grid=(t // tm,),
        in_specs=[full, tab, tab, tab, tab, vec, vec, wide, slab, slab, wide, slab_t, slab_t],
        out_specs=[full, vec, vec],
        out_shape=[jax.ShapeDtypeStruct((t, ATT_IN), BF16), jax.ShapeDtypeStruct((1, LANES), F32),
                   jax.ShapeDtypeStruct((1, LANES), F32)],
        compiler_params=_cparams(("arbitrary",)),
    )(proj, cos_a, sin_a, cos_b, sin_b, g2(qn_g), g2(kn_g), dqa, dka, dva, dqb, dkb, dvb)


def _head_to_half(xs, head_half, kv_half):
    low = _lane(xs.shape) < HEAD_DIM
    kept = jnp.where(low if head_half == 0 else jnp.logical_not(low), xs, 0.0)
    return jnp.where(kv_half == head_half, kept, pltpu.roll(kept, HEAD_DIM, 1))


def _halves_to_heads(r0, r1, kv_half):
    low = _lane(r0.shape) < HEAD_DIM
    a = jnp.where(kv_half == 0, r0, pltpu.roll(r0, HEAD_DIM, 1))
    b = jnp.where(kv_half == 1, r1, pltpu.roll(r1, HEAD_DIM, 1))
    return jnp.where(low, a, b)


def attn_delta(o, do, name):
    t, w = o.shape
    tm = min(t, 256)
    n_heads = w // HEAD_DIM
    grp = B_HEADS // B_KV_HEADS

    def body(o_ref, do_ref, d_ref, dob_ref, dobt_ref):
        lo, hi = _seg_matrix(0, HEAD_DIM), _seg_matrix(HEAD_DIM, LANES)
        for c in range(w // LANES):
            sl = slice(c * LANES, (c + 1) * LANES)
            dof = do_ref[:, sl].astype(F32)
            s = o_ref[:, sl].astype(F32) * dof
            d_ref[2 * c] = _dot_f32_by_ones(s, lo)
            d_ref[2 * c + 1] = _dot_f32_by_ones(s, hi)
            for half in range(2):
                head = 2 * c + half - A_HEADS
                if head >= 0:
                    placed = _head_to_half(dof, half, head // grp)
                    dob_ref[head] = placed.astype(BF16)
                    dobt_ref[head] = placed.T.astype(BF16)

    blk = pl.BlockSpec((tm, w), lambda i: (i, 0))
    return pl.pallas_call(
        body, name=name, grid=(t // tm,),
        in_specs=[blk, blk],
        out_specs=[pl.BlockSpec((n_heads, tm, LANES), lambda i: (0, i, 0)), pl.BlockSpec((B_HEADS, tm, LANES), lambda i: (0, i, 0)),
                   pl.BlockSpec((B_HEADS, LANES, tm), lambda i: (0, 0, i))],
        out_shape=[jax.ShapeDtypeStruct((n_heads, t, LANES), F32), jax.ShapeDtypeStruct((B_HEADS, t, LANES), BF16),
                   jax.ShapeDtypeStruct((B_HEADS, LANES, t), BF16)],
        compiler_params=_cparams(("parallel",)),
    )(o, do)


BAND = 3 * BLOCK


def _band_offsets(rows_rep):
    qi = lax.broadcasted_iota(jnp.int32, (BLOCK, BAND), 0)
    kj = lax.broadcasted_iota(jnp.int32, (BLOCK, BAND), 1)
    return jnp.concatenate([kj - qi] * rows_rep, axis=0)


def _band(n, t, offsets):
    start = pl.multiple_of(jnp.clip((n - 1) * BLOCK, 0, t - BAND), BLOCK)
    return start, jnp.abs(offsets + (start - n * BLOCK)) <= WINDOW


def window_attn_fwd(q, k, v, sink, name, blocks_per_step=8):
    t = q.shape[0]
    assert t >= BAND
    nq = _fit(t // BLOCK, blocks_per_step)
    tq = nq * BLOCK

    def body(sink_ref, q_ref, k_ref, v_ref, o_ref, lse_ref):
        j, n0 = pl.program_id(0), pl.program_id(1)
        kvh = j // 2
        row = lax.broadcasted_iota(jnp.int32, (2 * BLOCK, 1), 0)
        sk = jnp.where(row < BLOCK, sink_ref[2 * j], sink_ref[2 * j + 1]) * LOG2E
        offsets = _band_offsets(2)
        bands, scores = [], []
        for u in range(nq):
            start, ok = _band(n0 * nq + u, t, offsets)
            qf = q_ref[u * BLOCK:(u + 1) * BLOCK, :].astype(F32) * (SCALE * LOG2E)
            qs = jnp.concatenate([_head_to_half(qf, 0, kvh), _head_to_half(qf, 1, kvh)], axis=0).astype(BF16)
            bands.append(pl.ds(start, BAND))
            scores.append(jnp.where(ok, _dot_nt(qs, k_ref[bands[u], :]), NEG))
        soft = []
        for s in scores:
            m = jnp.maximum(jnp.max(s, axis=-1, keepdims=True), sk)
            p = jnp.exp2(s - m)
            soft.append((p.astype(BF16), jnp.sum(p, axis=-1, keepdims=True) + jnp.exp2(sk - m), m))
        for u, (p, denom, m) in enumerate(soft):
            rows = slice(u * BLOCK, (u + 1) * BLOCK)
            o = _dot_nn(p, v_ref[bands[u], :]) / denom
            o_ref[rows, :] = _halves_to_heads(o[:BLOCK], o[BLOCK:], kvh).astype(BF16)
            lse = jnp.broadcast_to(m + jnp.log2(denom), (2 * BLOCK, LANES))
            lse_ref[0, rows, :] = lse[:BLOCK]
            lse_ref[1, rows, :] = lse[BLOCK:]

    qspec = pl.BlockSpec((tq, LANES), lambda j, n: (n, j))
    whole = pl.BlockSpec((t, LANES), lambda j, n: (0, 0))
    return pl.pallas_call(
        body, name=name, grid=(A_HEADS // 2, t // tq),
        in_specs=[pl.BlockSpec(memory_space=pltpu.SMEM), qspec, whole, whole],
        out_specs=[qspec, pl.BlockSpec((2, tq, LANES), lambda j, n: (j, n, 0))],
        out_shape=[jax.ShapeDtypeStruct((t, A_Q + B_Q), BF16), jax.ShapeDtypeStruct((A_HEADS, t, LANES), F32)],
        compiler_params=_cparams(("parallel", "parallel")),
    )(sink, q, k, v)


def window_attn_bwd(q, k, v, sink, do, lse, delta, name, blocks_per_step=4):
    t = q.shape[0]
    assert t >= BAND
    nq = _fit(t // BLOCK, blocks_per_step)
    tq = nq * BLOCK
    grp = A_HEADS // A_KV_HEADS
    gw = grp * HEAD_DIM

    def body(sink_ref, q_ref, do_ref, k_ref, v_ref, lse_ref, dl_ref, dq_ref, dk_ref, dv_ref, ds_ref):
        kvh, n0 = pl.program_id(0), pl.program_id(1)

        @pl.when(n0 == 0)
        def _():
            dk_ref[...] = jnp.zeros_like(dk_ref)
            dv_ref[...] = jnp.zeros_like(dv_ref)
            ds_ref[...] = jnp.zeros_like(ds_ref)

        rid = lax.broadcasted_iota(jnp.int32, (8, LANES), 0)
        upd = jnp.zeros((8, LANES), F32)
        offsets = _band_offsets(grp)
        for u in range(nq):
            rows = slice(u * BLOCK, (u + 1) * BLOCK)
            start, ok = _band(n0 * nq + u, t, offsets)
            band = pl.ds(start, BAND)
            qparts, doparts = [], []
            for hh in range(grp):
                sl = slice((hh // 2) * LANES, (hh // 2 + 1) * LANES)
                qparts.append(_head_to_half(q_ref[rows, sl].astype(F32) * (SCALE * LOG2E), hh % 2, kvh))
                doparts.append(_head_to_half(do_ref[rows, sl].astype(F32), hh % 2, kvh))
            qs = jnp.concatenate(qparts, axis=0).astype(BF16)
            dos = jnp.concatenate(doparts, axis=0).astype(BF16)
            lse_b = jnp.concatenate([lse_ref[hh, rows, :] for hh in range(grp)], axis=0)
            dl_b = jnp.concatenate([dl_ref[hh, rows, :] for hh in range(grp)], axis=0)
            kband, vband = k_ref[band, :], v_ref[band, :]
            s = jnp.where(ok, _dot_nt(qs, kband), NEG)
            p = jnp.exp2(s - lse_b[:, :1])
            dp = _dot_nt(dos, vband)
            dsc = (p * (dp - dl_b[:, :1])).astype(BF16)
            dv_ref[0, band, :] += _dot_tn(p.astype(BF16), dos)
            dk_ref[0, band, :] += _dot_tn(dsc, qs) * LN2
            dq = _dot_nn(dsc, kband) * SCALE
            for c in range(grp // 2):
                dq_ref[rows, c * LANES:(c + 1) * LANES] = _halves_to_heads(
                    dq[2 * c * BLOCK:(2 * c + 1) * BLOCK], dq[(2 * c + 1) * BLOCK:(2 * c + 2) * BLOCK], kvh).astype(dq_ref.dtype)
            for hh in range(grp):
                rs = slice(hh * BLOCK, (hh + 1) * BLOCK)
                tot = jnp.sum(jnp.exp2(sink_ref[kvh * grp + hh] * LOG2E - lse_b[rs]) * dl_b[rs], axis=0, keepdims=True)
                upd = upd + jnp.where(rid == hh, -tot, 0.0)
        ds_ref[0] += upd

    qspec = pl.BlockSpec((tq, gw), lambda kvh, n: (n, kvh))
    whole = pl.BlockSpec((t, LANES), lambda kvh, n: (0, 0))
    stat = pl.BlockSpec((grp, tq, LANES), lambda kvh, n: (kvh, n, 0))
    slab = pl.BlockSpec((1, t, LANES), lambda kvh, n: (kvh, 0, 0))
    return pl.pallas_call(
        body, name=name, grid=(A_KV_HEADS, t // tq),
        in_specs=[pl.BlockSpec(memory_space=pltpu.SMEM), qspec, qspec, whole, whole, stat, stat],
        out_specs=[qspec, slab, slab, pl.BlockSpec((1, 8, LANES), lambda kvh, n: (kvh, 0, 0))],
        out_shape=[jax.ShapeDtypeStruct((t, A_Q), BF16), jax.ShapeDtypeStruct((A_KV_HEADS, t, LANES), F32),
                   jax.ShapeDtypeStruct((A_KV_HEADS, t, LANES), F32), jax.ShapeDtypeStruct((A_KV_HEADS, 8, LANES), F32)],
        compiler_params=_cparams(("arbitrary", "arbitrary")),
    )(sink, q, do, k, v, lse, delta)


def flash_attn_fwd(q, k, v, cat, name, exchange=None, tq=256, tk=2048, ahead=2):
    t = q.shape[1]
    tq, tk = _fit(t, tq), _fit(t, tk)
    nk = t // tk

    def body(q_ref, k_ref, v_ref, cat_ref, o_ref, lse_ref):
        del cat_ref
        kvh = pl.program_id(0) // 2
        qs = q_ref[...].reshape(2 * tq, LANES)
        mine = (_lane((tk, LANES)) < HEAD_DIM) == (kvh == 0)

        def scores(c):
            return _dot_nt(qs, k_ref[c * tk:(c + 1) * tk, :])

        s = [scores(c) for c in range(min(ahead, nk))]
        m = jnp.full((2 * tq, 1), NEG, F32)
        acc = jnp.zeros((2 * tq, LANES), F32)
        for c in range(nk):
            if c + ahead < nk:
                s.append(scores(c + ahead))
            vb = jnp.where(mine, v_ref[c * tk:(c + 1) * tk, :], jnp.ones((), BF16))
            m_new = jnp.maximum(m, jnp.max(s[c], axis=-1, keepdims=True))
            p = jnp.exp2(s[c] - m_new).astype(BF16)
            acc = jnp.exp2(m - m_new) * acc + _dot_nn(p, vb)
            m = m_new
        other = pltpu.roll(acc, HEAD_DIM, 1)
        o = acc / other
        o_ref[...] = _halves_to_heads(o[:tq], o[tq:], kvh).astype(BF16)
        in_mine = (_lane(acc.shape) < HEAD_DIM) == (kvh == 0)
        lse = jnp.broadcast_to(m, acc.shape) + jnp.log2(jnp.where(in_mine, other, acc))
        lse_ref[0] = lse[:tq]
        lse_ref[1] = lse[tq:]

    qspec = pl.BlockSpec((2, tq, LANES), lambda j, i: (j, i, 0))
    whole = pl.BlockSpec((t, LANES), lambda j, i: (0, 0))
    nj, ni = B_HEADS // 2, t // tq
    steps = lambda: ((pl.program_id(0) == 0) & (pl.program_id(1) == 0), (pl.program_id(0) == nj - 1) & (pl.program_id(1) == ni - 1))
    body, x_in, x_out, x_shapes, x_scratch = carried(body, exchange, 4, 2, steps)
    return pl.pallas_call(
        body, name=name, grid=(nj, ni),
        in_specs=[qspec, whole, whole, _ANY] + x_in,
        out_specs=[pl.BlockSpec((tq, LANES), lambda j, i: (i, A_Q // LANES + j)),
                   pl.BlockSpec((2, tq, LANES), lambda j, i: (j, i, 0))] + x_out,
        out_shape=[jax.ShapeDtypeStruct(cat.shape, BF16), jax.ShapeDtypeStruct((B_HEADS, t, LANES), F32)] + x_shapes,
        scratch_shapes=x_scratch,
        input_output_aliases={3: 0},
        compiler_params=_cparams(("arbitrary", "arbitrary")),
    )(q, k, v, cat, *(exchange.arrays if exchange else ()))


def flash_attn_bwd(q, q_t, k, v, do, do_t, lse, delta, name, exchange=None, tq=256, tk=512):
    t = q.shape[1]
    tq, tk = _fit(t, tq), _fit(t, tk)
    nk = t // tk
    together = _fit(nk, 8)
    grp = B_HEADS // B_KV_HEADS
    gw = grp * HEAD_DIM

    def body(q_ref, qt_ref, do_ref, dot_ref, k_ref, v_ref, lse_ref, dl_ref, dq_ref, dk_ref, dv_ref, dq_s):
        kvh, i = pl.program_id(0), pl.program_id(1)

        @pl.when(i == 0)
        def _():
            dk_ref[...] = jnp.zeros_like(dk_ref)
            dv_ref[...] = jnp.zeros_like(dv_ref)

        qs, dos = q_ref[...].reshape(grp * tq, LANES), do_ref[...].reshape(grp * tq, LANES)
        qs_t = jnp.concatenate([qt_ref[hh] for hh in range(grp)], axis=1)
        dos_t = jnp.concatenate([dot_ref[hh] for hh in range(grp)], axis=1)
        lse = jnp.concatenate([lse_ref[hh][:, :1] for hh in range(grp)], axis=0)
        dl = jnp.concatenate([dl_ref[hh][:, :1] for hh in range(grp)], axis=0)
        dq_s[...] = jnp.zeros_like(dq_s)

        def chunks(c0, carry):
            cs = [c0 * together + u for u in range(together)]
            kbs = [k_ref[pl.ds(pl.multiple_of(c * tk, tk), tk), :] for c in cs]
            vbs = [v_ref[pl.ds(pl.multiple_of(c * tk, tk), tk), :] for c in cs]
            ss = [_dot_nt(qs, kb) for kb in kbs]
            dps = [_dot_nt(dos, vb) for vb in vbs]
            for c, kb, s, dp in zip(cs, kbs, ss, dps):
                p = jnp.exp2(s - lse)
                dsc = (p * (dp - dl)).astype(BF16)
                dv_ref[0, c] += _dot_nn(dos_t, p.astype(BF16))
                dk_ref[0, c] += _dot_nn(qs_t, dsc) * LN2
                dq_s[...] += _dot_nn(kb.T, dsc.T)
            return carry

        lax.fori_loop(0, nk // together, chunks, 0)
        dq = dq_s[...].T
        for c in range(grp // 2):
            dq_ref[:, c * LANES:(c + 1) * LANES] = (_halves_to_heads(
                dq[2 * c * tq:(2 * c + 1) * tq], dq[(2 * c + 1) * tq:(2 * c + 2) * tq], kvh) * SCALE).astype(dq_ref.dtype)

    dqspec = pl.BlockSpec((tq, gw), lambda kvh, i: (i, kvh))
    whole = pl.BlockSpec((t, LANES), lambda kvh, i: (0, 0))
    stat = pl.BlockSpec((grp, tq, LANES), lambda kvh, i: (kvh, i, 0))
    stat_t = pl.BlockSpec((grp, LANES, tq), lambda kvh, i: (kvh, 0, i))
    dlstat = pl.BlockSpec((grp, tq, LANES), lambda kvh, i: (A_HEADS // grp + kvh, i, 0))
    slab = pl.BlockSpec((1, nk, LANES, tk), lambda kvh, i: (kvh, 0, 0, 0))
    ni = t // tq
    steps = lambda: ((pl.program_id(0) == 0) & (pl.program_id(1) == 0),
                     (pl.program_id(0) == B_KV_HEADS - 1) & (pl.program_id(1) == ni - 1))
    body, x_in, x_out, x_shapes, x_scratch = carried(body, exchange, 8, 3, steps)
    return pl.pallas_call(
        body, name=name, grid=(B_KV_HEADS, ni),
        in_specs=[stat, stat_t, stat, stat_t, whole, whole, stat, dlstat] + x_in,
        out_specs=[dqspec, slab, slab] + x_out,
        out_shape=[jax.ShapeDtypeStruct((t, B_Q), BF16), jax.ShapeDtypeStruct((B_KV_HEADS, nk, LANES, tk), F32),
                   jax.ShapeDtypeStruct((B_KV_HEADS, nk, LANES, tk), F32)] + x_shapes,
        scratch_shapes=[pltpu.VMEM((LANES, grp * tq), F32)] + x_scratch,
        compiler_params=_cparams(("arbitrary", "arbitrary")),
    )(q, q_t, do, do_t, k, v, lse, delta, *(exchange.arrays if exchange else ()))


_GELU_C = math.sqrt(2.0 / math.pi)
_GELU_A = 0.044715


def _gelu(x):
    return 0.5 * x * (1.0 + jnp.tanh(_GELU_C * (x + _GELU_A * x * x * x)))


def _gelu_grad(x):
    th = jnp.tanh(_GELU_C * (x + _GELU_A * x * x * x))
    return 0.5 * (1.0 + th) + 0.5 * x * (1.0 - th * th) * _GELU_C * (1.0 + 3.0 * _GELU_A * x * x)


def _layernorm_stats(vf):
    mu = jnp.mean(vf, axis=-1, keepdims=True)
    vc = vf - mu
    r = lax.rsqrt(jnp.mean(vc * vc, axis=-1, keepdims=True) + EPS)
    return vc * r, r


def sgu_mix_fwd(z, ln_g, ln_b, w_s, b_rows, name):
    t, w2 = z.shape
    w = w2 // 2
    dg = w // SGU_GROUPS

    def body(u_ref, v_ref, g_ref, b_ref, ws_ref, bb_ref, y_ref):
        vhat, _ = _layernorm_stats(v_ref[...].astype(F32))
        vn = (vhat * g_ref[...] + b_ref[...]).astype(BF16)
        for g in range(SGU_GROUPS):
            sl = slice(g * dg, (g + 1) * dg)
            mixed = _dot_nn(ws_ref[g], vn[:, sl]) + bb_ref[g]
            y_ref[:, sl] = (u_ref[:, sl].astype(F32) * mixed).astype(BF16)

    vec = pl.BlockSpec((1, w), lambda n: (0, 0))
    whole = pl.BlockSpec((SGU_GROUPS, SGU_CHUNK, SGU_CHUNK), lambda n: (0, 0, 0))
    return pl.pallas_call(
        body, name=name, grid=(t // SGU_CHUNK,),
        in_specs=[pl.BlockSpec((SGU_CHUNK, w), lambda n: (n, 0)), pl.BlockSpec((SGU_CHUNK, w), lambda n: (n, 1)),
                  vec, vec, whole, whole],
        out_specs=pl.BlockSpec((SGU_CHUNK, w), lambda n: (n, 0)),
        out_shape=jax.ShapeDtypeStruct((t, w), BF16),
        compiler_params=_cparams(("parallel",)),
    )(z, z, ln_g.reshape(1, w), ln_b.reshape(1, w), w_s, b_rows)


def sgu_mix_bwd(z, apre, dy, ln_g, ln_b, w_s, b_rows, name):
    t, w2 = z.shape
    w = w2 // 2
    dg = w // SGU_GROUPS

    def body(u_ref, v_ref, au_ref, av_ref, dy_ref, g_ref, b_ref, ws_ref, bb_ref, da_ref, dlg_ref, dlb_ref, dws_ref, dbs_ref):
        @pl.when(pl.program_id(0) == 0)
        def _():
            dlg_ref[...] = jnp.zeros_like(dlg_ref)
            dlb_ref[...] = jnp.zeros_like(dlb_ref)
            dws_ref[...] = jnp.zeros_like(dws_ref)
            dbs_ref[...] = jnp.zeros_like(dbs_ref)

        vhat, r = _layernorm_stats(v_ref[...].astype(F32))
        gam = g_ref[...]
        vn = (vhat * gam + b_ref[...]).astype(BF16)
        ones8 = jnp.ones((8, dg), BF16)
        rid = lax.broadcasted_iota(jnp.int32, (8, SGU_CHUNK), 0)
        dbs = jnp.zeros((8, SGU_CHUNK), F32)
        dvn_parts = []
        for g in range(SGU_GROUPS):
            sl = slice(g * dg, (g + 1) * dg)
            dyg = dy_ref[:, sl].astype(F32)
            mixed = _dot_nn(ws_ref[g], vn[:, sl]) + bb_ref[g]
            da_ref[:, sl] = (dyg * mixed * _gelu_grad(au_ref[:, sl].astype(F32))).astype(BF16)
            dmix = dyg * u_ref[:, sl].astype(F32)
            dm_hi = dmix.astype(BF16)
            dm_lo = (dmix - dm_hi.astype(F32)).astype(BF16)
            dws_ref[g] += _dot_nt(dm_hi, vn[:, sl])
            dbs = dbs + jnp.where(rid == g, _dot_nt(ones8, dm_hi) + _dot_nt(ones8, dm_lo), 0.0)
            dvn_parts.append(_dot_tn(ws_ref[g], dm_hi))
        dbs_ref[...] += dbs
        dvn = jnp.concatenate(dvn_parts, axis=1)
        dlg_ref[...] += jnp.sum(dvn * vhat, axis=0, keepdims=True)
        dlb_ref[...] += jnp.sum(dvn, axis=0, keepdims=True)
        dvh = dvn * gam
        dv = r * (dvh - jnp.mean(dvh, axis=-1, keepdims=True) - vhat * jnp.mean(dvh * vhat, axis=-1, keepdims=True))
        da_ref[:, w:] = (dv * _gelu_grad(av_ref[...].astype(F32))).astype(BF16)

    vec = pl.BlockSpec((1, w), lambda n: (0, 0))
    whole = pl.BlockSpec((SGU_GROUPS, SGU_CHUNK, SGU_CHUNK), lambda n: (0, 0, 0))
    left = pl.BlockSpec((SGU_CHUNK, w), lambda n: (n, 0))
    right = pl.BlockSpec((SGU_CHUNK, w), lambda n: (n, 1))
    return pl.pallas_call(
        body, name=name, grid=(t // SGU_CHUNK,),
        in_specs=[left, right, left, right, left, vec, vec, whole, whole],
        out_specs=[pl.BlockSpec((SGU_CHUNK, w2), lambda n: (n, 0)), vec, vec, whole,
                   pl.BlockSpec((SGU_GROUPS, SGU_CHUNK), lambda n: (0, 0))],
        out_shape=[jax.ShapeDtypeStruct((t, w2), BF16), jax.ShapeDtypeStruct((1, w), F32), jax.ShapeDtypeStruct((1, w), F32),
                   jax.ShapeDtypeStruct((SGU_GROUPS, SGU_CHUNK, SGU_CHUNK), F32),
                   jax.ShapeDtypeStruct((SGU_GROUPS, SGU_CHUNK), F32)],
        compiler_params=_cparams(("arbitrary",)),
    )(z, z, apre, apre, dy, ln_g.reshape(1, w), ln_b.reshape(1, w), w_s, b_rows)


def loss_head(h, g, target, name):
    t, d = h.shape
    tm = min(t, 512)

    def body(h_ref, g_ref, t_ref, loss_ref, dh_ref, dhb_ref, dg_ref):
        @pl.when(pl.program_id(0) == 0)
        def _():
            loss_ref[...] = jnp.zeros_like(loss_ref)
            dg_ref[...] = jnp.zeros_like(dg_ref)

        xf = h_ref[...]
        r = lax.rsqrt(jnp.mean(xf * xf, axis=-1, keepdims=True) + EPS)
        xhat = xf * r
        err = xhat * g_ref[...] - t_ref[...]
        per_tok = jnp.mean(err * err, axis=-1, keepdims=True)
        loss_ref[...] += 0.5 * jnp.sum(per_tok, axis=0, keepdims=True)
        dy = err * (1.0 / d)
        dg_ref[...] += jnp.sum(dy * xhat, axis=0, keepdims=True)
        dxh = dy * g_ref[...]
        dh = r * (dxh - xhat * jnp.mean(dxh * xhat, axis=-1, keepdims=True))
        dh_ref[...] = dh
        dhb_ref[...] = dh.astype(BF16)

    row = pl.BlockSpec((tm, d), lambda i: (i, 0))
    vec = pl.BlockSpec((1, d), lambda i: (0, 0))
    return pl.pallas_call(
        body, name=name, grid=(t // tm,),
        in_specs=[row, vec, row],
        out_specs=[pl.BlockSpec((1, LANES), lambda i: (0, 0)), row, row, vec],
        out_shape=[jax.ShapeDtypeStruct((1, LANES), F32), jax.ShapeDtypeStruct((t, d), F32), jax.ShapeDtypeStruct((t, d), BF16),
                   jax.ShapeDtypeStruct((1, d), F32)],
        compiler_params=_cparams(("arbitrary",)),
    )(h, g.reshape(1, d), target)


ADAMW_BLOCK_BYTES = 1 << 20


def adamw(parts, w, m, v, name):
    n_layers, r, c = w.shape
    row_bytes = n_layers * c * 4
    if r * row_bytes <= 2 * ADAMW_BLOCK_BYTES:
        tr = r
    else:
        tr = _fit(r, 1 << int(math.log2(max(8, ADAMW_BLOCK_BYTES // row_bytes))))
    bc1 = 1.0 - ADAM_B1 ** ADAM_STEP
    bc2 = 1.0 - ADAM_B2 ** ADAM_STEP

    def body(*refs):
        p_refs = refs[:n_layers]
        w_ref, m_ref, v_ref, g_ref, d_ref, nm_ref, nv_ref = refs[n_layers:]
        for l in range(n_layers):
            g = p_refs[l][0].astype(F32)
            for j in range(1, N_DEV):
                g = g + p_refs[l][j].astype(F32)
            nm = ADAM_B1 * m_ref[l] + (1.0 - ADAM_B1) * g
            nv = ADAM_B2 * v_ref[l] + (1.0 - ADAM_B2) * (g * g)
            g_ref[l] = g
            nm_ref[l] = nm
            nv_ref[l] = nv
            d_ref[l] = -ADAM_LR * ((nm / bc1) / (jnp.sqrt(nv / bc2) + ADAM_EPS) + ADAM_WD * w_ref[l])

    blk = pl.BlockSpec((n_layers, tr, c), lambda i: (0, i, 0))
    return pl.pallas_call(
        body, name=name, grid=(r // tr,),
        in_specs=[pl.BlockSpec((N_DEV, tr, c), lambda i: (0, i, 0))] * n_layers + [blk, blk, blk],
        out_specs=[blk] * 4,
        out_shape=[jax.ShapeDtypeStruct((n_layers, r, c), F32)] * 4,
        compiler_params=_cparams(("parallel",)),
    )(*parts, w, m, v)


_ANY = pl.BlockSpec(memory_space=pl.ANY)


def _mesh_pos():
    return lax.axis_index("x"), lax.axis_index("y"), lax.axis_index("c")


class Exchange:
    def __init__(self, gathers=(), scatters=()):
        self.items = [("gather", a) for a in gathers] + [("scatter", a) for a in scatters]
        self.arrays = [a for _, a in self.items]
        self.n = len(self.items)

    def out_shapes(self):
        return [jax.ShapeDtypeStruct(((N_DEV,) + a.shape) if kind == "gather" else a.shape, a.dtype) for kind, a in self.items]

    def scratch(self):
        return [pltpu.SemaphoreType.DMA((7 * self.n,)), pltpu.SemaphoreType.DMA((7 * self.n,)), pltpu.SemaphoreType.DMA((self.n,))]

    def _copies(self, in_refs, out_refs, send_sems, recv_sems, local_sems):
        x, y, c = _mesh_pos()
        me = 4 * x + 2 * y + c
        local, sends, arrivals = [], [], []
        for t, (kind, _) in enumerate(self.items):
            src_of = (lambda slot, r=in_refs[t]: r) if kind == "gather" else (lambda slot, r=in_refs[t]: r.at[slot])
            local.append(pltpu.make_async_copy(src_of(me), out_refs[t].at[me], local_sems.at[t]))
            for k in range(1, N_DEV):
                px = 1 - x if k & 4 else x
                py = 1 - y if k & 2 else y
                pc = 1 - c if k & 1 else c
                pid = 4 * px + 2 * py + pc
                kw = dict(send_sem=send_sems.at[7 * t + k - 1], recv_sem=recv_sems.at[7 * t + k - 1],
                          device_id=(px, py, pc), device_id_type=pl.DeviceIdType.MESH)
                sends.append(pltpu.make_async_remote_copy(src_ref=src_of(pid), dst_ref=out_refs[t].at[me], **kw))
                arrivals.append(pltpu.make_async_remote_copy(src_ref=src_of(pid), dst_ref=out_refs[t].at[pid], **kw))
        return local, sends, arrivals

    def start(self, *refs):
        local, sends, _ = self._copies(*refs)
        for cp in local + sends:
            cp.start()

    def wait(self, *refs):
        local, sends, arrivals = self._copies(*refs)
        for cp in arrivals:
            cp.wait_recv()
        for cp in sends:
            cp.wait_send()
        for cp in local:
            cp.wait()


def carried(body, exchange, n_in, n_out, first_last):
    if exchange is None:
        return body, [], [], [], []
    nx = exchange.n

    def wrapped(*refs):
        ins, xin = refs[:n_in], refs[n_in:n_in + nx]
        outs, xout = refs[n_in + nx:n_in + nx + n_out], refs[n_in + nx + n_out:n_in + 2 * nx + n_out]
        scratch, sems = refs[n_in + 2 * nx + n_out:-3], refs[-3:]
        first, last = first_last()

        @pl.when(first)
        def _():
            exchange.start(xin, xout, *sems)

        body(*ins, *outs, *scratch)

        @pl.when(last)
        def _():
            exchange.wait(xin, xout, *sems)

    return wrapped, [_ANY] * nx, [_ANY] * nx, exchange.out_shapes(), exchange.scratch()


def exchange_only(exchange, name):
    def body(*refs):
        xin, xout, sems = refs[:exchange.n], refs[exchange.n:2 * exchange.n], refs[-3:]
        exchange.start(xin, xout, *sems)
        exchange.wait(xin, xout, *sems)

    return pl.pallas_call(
        body, name=name, in_specs=[_ANY] * exchange.n, out_specs=[_ANY] * exchange.n,
        out_shape=exchange.out_shapes(), scratch_shapes=exchange.scratch(),
    )(*exchange.arrays)


def _residual_out(a, w_out, x, next_g, name, **tiles):
    if next_g is None:
        (y,) = matmul(a, w_out, "nn", name, [F32], epilogue=lambda acc, r: (r + acc,), extras=(x,), **tiles)
        return y, None

    def add_and_norm(acc, r, g):
        y = r + acc
        return y, y * lax.rsqrt(jnp.mean(y * y, axis=-1, keepdims=True) + EPS) * g

    assert w_out.shape[1] <= tiles.get("tn", 1024)
    return matmul(a, w_out, "nn", name, [F32, BF16], epilogue=add_and_norm, extras=(x, next_g.reshape(1, -1)), **tiles)


def attention_fwd(x, h, w_in, sink, qn_g, kn_g, w_out, tables, next_g, tag, exchange=None):
    (proj,) = matmul(h, w_in, "nn", f"{tag}_proj", [F32], tn=ATT_IN)
    qa, ka, va, qb, qb_t, kb, vb = qkv_post_fwd(proj, tables, qn_g, kn_g, f"{tag}_qkv")
    cat, lse_a = window_attn_fwd(qa, ka, va, sink, f"{tag}_win")
    cat, lse_b, *arrived = flash_attn_fwd(qb, kb, vb, cat, f"{tag}_flash", exchange)
    if callable(w_out):
        w_out = w_out(arrived)
    y, h_next = _residual_out(cat, w_out, x, next_g, f"{tag}_out")
    saved = (x, h, proj, qa, ka, va, qb, qb_t, kb, vb, cat, lse_a, lse_b)
    return y, h_next, saved, arrived


def attention_bwd(dy, dyb, saved, norm_g, w_in, sink, qn_g, kn_g, w_out, tables, tag, exchange_with=None):
    x, h, proj, qa, ka, va, qb, qb_t, kb, vb, cat, lse_a, lse_b = saved
    (dcat,) = matmul(dyb, w_out, "nt", f"{tag}_dcat", [BF16])
    (dw_out,) = matmul(cat, dyb, "tn", f"{tag}_dwout", [BF16], tk=4096)
    delta, dob, dob_t = attn_delta(cat, dcat, f"{tag}_delta")
    dqa, dka, dva, dsink = window_attn_bwd(qa, ka, va, sink, dcat, lse_a, delta, f"{tag}_dwin")
    exchange = exchange_with(dw_out) if exchange_with else None
    dqb, dkb, dvb, *arrived = flash_attn_bwd(qb, qb_t, kb, vb, dob, dob_t, lse_b, delta, f"{tag}_dflash", exchange)
    dproj, dqg, dkg = qkv_post_bwd(proj, tables, qn_g, kn_g, dqa, dka, dva, dqb, dkb, dvb, f"{tag}_dqkv")
    (dw_in,) = matmul(h, dproj, "tn", f"{tag}_dwin_w", [BF16], tn=ATT_IN // 2, tk=2048)
    dx, dxb, dg = matmul_nt_normbwd(dproj, w_in, x, norm_g, dy, f"{tag}_dx")
    grp = A_HEADS // A_KV_HEADS
    small = dict(norm=dg[0], sink=dsink[:, :grp, 0].reshape(A_HEADS), qnorm=dqg[0, :HEAD_DIM], knorm=dkg[0, :HEAD_DIM])
    return dx, dxb, dw_in, dw_out, small, arrived


def sgu_fwd(x, h, w_in, ln_g, ln_b, w_s, b_rows, w_out, next_g, tag):
    apre, z = matmul(h, w_in, "nn", f"{tag}_in", [BF16, BF16], epilogue=lambda acc: (acc, _gelu(acc)))
    y = sgu_mix_fwd(z, ln_g, ln_b, w_s, b_rows, f"{tag}_mix")
    out, h_next = _residual_out(y, w_out, x, next_g, f"{tag}_out")
    return out, h_next, (x, h, apre, z, y)


def sgu_bwd(dout, doutb, saved, norm_g, w_in, ln_g, ln_b, w_s, b_rows, w_out, tag):
    x, h, apre, z, y = saved
    (dy,) = matmul(doutb, w_out, "nt", f"{tag}_dy", [BF16])
    (dw_out,) = matmul(y, doutb, "tn", f"{tag}_dwout", [BF16], tk=4096)
    dapre, dlg, dlb, dws, dbs = sgu_mix_bwd(z, apre, dy, ln_g, ln_b, w_s, b_rows, f"{tag}_dmix")
    (dw_in,) = matmul(h, dapre, "tn", f"{tag}_dwin", [BF16], out_shards=True, tk=4096)
    dx, dxb, dg = matmul_nt_normbwd(dapre, w_in, x, norm_g, dout, f"{tag}_dx")
    small = dict(norm=dg[0], ln_g=dlg[0], ln_b=dlb[0], w_s=dws, b_s=dbs)
    return dx, dxb, dw_in, dw_out, small


def _square(r):
    return r * r


def mlp_fwd(x, h, w1, w2, next_g, tag):
    (r,) = matmul(h, w1, "nn", f"{tag}_up", [BF16], epilogue=lambda acc: (jnp.maximum(acc, 0.0),), tm=2048)
    y, h_next = _residual_out(r, w2, x, next_g, f"{tag}_down", a_fn=_square, tm=512, tk=4096)
    return y, h_next, (x, h, r)


def mlp_bwd(dy, dyb, saved, norm_g, w1, w2, tag):
    x, h, r = saved
    (da,) = matmul(dyb, w2, "nt", f"{tag}_da", [BF16], epilogue=lambda acc, rr: (acc * (2.0 * rr.astype(F32)),), extras=(r,),
                   tm=2048)
    (dw2,) = matmul(r, dyb, "tn", f"{tag}_dw2", [BF16], a_fn=_square, tk=4096)
    (dw1,) = matmul(h, da, "tn", f"{tag}_dw1", [BF16], out_shards=True, tk=4096)
    dx, dxb, dg = matmul_nt_normbwd(da, w1, x, norm_g, dy, f"{tag}_dx")
    return dx, dxb, dw1, dw2, dg[0]


ORDER = ("att_norm", "att_w_in", "att_sink", "att_qnorm", "att_knorm", "att_w_out", "sgu_norm", "sgu_w_in", "sgu_ln_g",
         "sgu_ln_b", "sgu_w_s", "sgu_b_s", "sgu_w_out", "mlp_norm", "mlp_w1", "mlp_w2", "final_norm")
SHARDED = ("att_w_in", "att_w_out", "sgu_w_in", "sgu_w_out", "mlp_w1", "mlp_w2")
SGU_VECS = ("sgu_norm", "sgu_ln_g", "sgu_ln_b")
SMALL_EARLY = ("sgu_w_s", "sgu_b_s", "mlp_norm", "final_norm", "loss")
SMALL_LATE = ("att_norm", "att_sink", "att_qnorm", "att_knorm")
SMALL_ROWS_MULT = 8


def _flat(blocks, names):
    flat = jnp.concatenate([blocks[n].reshape(-1).astype(F32) for n in names])
    per = SMALL_ROWS_MULT * FLAT_COLS
    total = -(-flat.shape[0] // per) * per
    return jnp.pad(flat, (0, total - flat.shape[0])).reshape(1, total // FLAT_COLS, FLAT_COLS)


def _unflat(flat, like, names):
    out, off = {}, 0
    f = flat.reshape(-1)
    for n in names:
        size = like[n].size
        out[n] = f[off:off + size].reshape(like[n].shape)
        off += size
    return out


def kernel(x, att_norm, att_w_in, att_sink, att_qnorm, att_knorm, att_w_out, sgu_norm, sgu_w_in, sgu_ln_g, sgu_ln_b, sgu_w_s, sgu_b_s, sgu_w_out, mlp_norm, mlp_w1, mlp_w2, final_norm, loss_target, m_att_norm, m_att_w_in, m_att_sink, m_att_qnorm, m_att_knorm, m_att_w_out, m_sgu_norm, m_sgu_w_in, m_sgu_ln_g, m_sgu_ln_b, m_sgu_w_s, m_sgu_b_s, m_sgu_w_out, m_mlp_norm, m_mlp_w1, m_mlp_w2, m_final_norm, v_att_norm, v_att_w_in, v_att_sink, v_att_qnorm, v_att_knorm, v_att_w_out, v_sgu_norm, v_sgu_w_in, v_sgu_ln_g, v_sgu_ln_b, v_sgu_w_s, v_sgu_b_s, v_sgu_w_out, v_mlp_norm, v_mlp_w1, v_mlp_w2, v_final_norm):
    w = dict(att_norm=att_norm, att_w_in=att_w_in, att_sink=att_sink, att_qnorm=att_qnorm, att_knorm=att_knorm,
             att_w_out=att_w_out, sgu_norm=sgu_norm, sgu_w_in=sgu_w_in, sgu_ln_g=sgu_ln_g, sgu_ln_b=sgu_ln_b, sgu_w_s=sgu_w_s,
             sgu_b_s=sgu_b_s, sgu_w_out=sgu_w_out, mlp_norm=mlp_norm, mlp_w1=mlp_w1, mlp_w2=mlp_w2, final_norm=final_norm)
    m = dict(att_norm=m_att_norm, att_w_in=m_att_w_in, att_sink=m_att_sink, att_qnorm=m_att_qnorm, att_knorm=m_att_knorm,
             att_w_out=m_att_w_out, sgu_norm=m_sgu_norm, sgu_w_in=m_sgu_w_in, sgu_ln_g=m_sgu_ln_g, sgu_ln_b=m_sgu_ln_b,
             sgu_w_s=m_sgu_w_s, sgu_b_s=m_sgu_b_s, sgu_w_out=m_sgu_w_out, mlp_norm=m_mlp_norm, mlp_w1=m_mlp_w1, mlp_w2=m_mlp_w2,
             final_norm=m_final_norm)
    v = dict(att_norm=v_att_norm, att_w_in=v_att_w_in, att_sink=v_att_sink, att_qnorm=v_att_qnorm, att_knorm=v_att_knorm,
             att_w_out=v_att_w_out, sgu_norm=v_sgu_norm, sgu_w_in=v_sgu_w_in, sgu_ln_g=v_sgu_ln_g, sgu_ln_b=v_sgu_ln_b,
             sgu_w_s=v_sgu_w_s, sgu_b_s=v_sgu_b_s, sgu_w_out=v_sgu_w_out, mlp_norm=v_mlp_norm, mlp_w1=v_mlp_w1, mlp_w2=v_mlp_w2,
             final_norm=v_final_norm)
    loss, grad_x, g, d, nm, nv = train_step(x[0], loss_target[0], w, m, v)
    return (loss, grad_x[None], *[g[n] for n in ORDER], *[d[n] for n in ORDER], *[nm[n] for n in ORDER], *[nv[n] for n in ORDER])


def train_step(x, target, w, m, v):
    t, d_model = x.shape
    n_att, n_sgu, depth = w["att_w_in"].shape[0], w["sgu_w_in"].shape[0], w["mlp_w1"].shape[0]
    bf = lambda n: w[n].astype(BF16)

    vec_local = jnp.stack([w[n] for n in SGU_VECS], axis=1)
    att_in = bf("att_w_in")
    g_in0, g_vec = exchange_only(Exchange(gathers=[att_in[:1], vec_local]), "gather_first")
    vecs = g_vec.transpose(1, 2, 0, 3).reshape(n_sgu, len(SGU_VECS), -1)
    rest = Exchange(gathers=[att_in[1:], bf("att_w_out"), bf("sgu_w_in"), bf("sgu_w_out"), bf("mlp_w1"), bf("mlp_w2")])
    w_s_bf = w["sgu_w_s"].astype(BF16)
    b_rows = jnp.broadcast_to(w["sgu_b_s"][:, :, :, None], w["sgu_b_s"].shape + (LANES,))
    tables = _rope_tables(t)
    full_cols = lambda g: g.transpose(1, 2, 0, 3).reshape(g.shape[1], d_model, -1)

    mixer_norm = lambda layer: w["att_norm"][layer // 2] if layer % 2 == 0 else vecs[layer // 2, 0]
    saved = []
    h = rmsnorm_fwd(x, mixer_norm(0), "att0_norm")
    for layer in range(depth):
        i = layer // 2
        if layer % 2 == 0:
            if layer == 0:
                att_w_in = [full_cols(g_in0)[0]]
            x, h, sv, arrived = attention_fwd(x, h, att_w_in[i], w["att_sink"][i], w["att_qnorm"][i], w["att_knorm"][i],
                                              (lambda arrived: Gathered(arrived[1], "row", 0)) if layer == 0 else att_w_out[i],
                                              tables, w["mlp_norm"][layer], f"att{i}", rest if layer == 0 else None)
            if layer == 0:
                g_in1, g_out, g_sgu_in, g_sgu_out, g_w1, g_w2 = arrived
                att_w_in += list(full_cols(g_in1))
                att_w_out = [Gathered(g_out, "row", l) for l in range(n_att)]
        else:
            x, h, sv = sgu_fwd(x, h, Gathered(g_sgu_in, "col", i), vecs[i, 1], vecs[i, 2], w_s_bf[i], b_rows[i],
                               Gathered(g_sgu_out, "row", i), w["mlp_norm"][layer], f"sgu{i}")
        x, h, sm = mlp_fwd(x, h, Gathered(g_w1, "col", layer), Gathered(g_w2, "row", layer),
                           mixer_norm(layer + 1) if layer + 1 < depth else None, f"mlp{layer}")
        saved.append((sv, sm))
    loss_row, dh, dhb, dgf = loss_head(x, w["final_norm"], target, "loss_head")

    queue, recv = [], {}
    gs = dict(att_norm=[None] * n_att, att_sink=[None] * n_att, att_qnorm=[None] * n_att, att_knorm=[None] * n_att,
              sgu_w_s=[None] * n_sgu, sgu_b_s=[None] * n_sgu, mlp_norm=[None] * depth)

    def row_slabs(g):
        return g.reshape(N_DEV, g.shape[0] // N_DEV, g.shape[1])

    def col_slabs(g):
        return g.reshape(g.shape[0], N_DEV, g.shape[1] // N_DEV).transpose(1, 0, 2)

    def take_queue(gathers=()):
        items = list(queue)
        queue.clear()
        keys = [k for k, _ in gathers] + [k for k, _ in items]
        return Exchange(gathers=[a for _, a in gathers], scatters=[a for _, a in items]), keys

    def small_early():
        blocks = dict(sgu_w_s=jnp.stack(gs["sgu_w_s"]), sgu_b_s=jnp.stack(gs["sgu_b_s"]), mlp_norm=jnp.stack(gs["mlp_norm"]),
                      final_norm=dgf[0], loss=loss_row[0, :1])
        return _flat(blocks, SMALL_EARLY)[0]

    for layer in reversed(range(depth)):
        i = layer // 2
        sv, sm = saved[layer]
        dh, dhb, dw1, dw2, gs["mlp_norm"][layer] = mlp_bwd(
            dh, dhb, sm, w["mlp_norm"][layer], Gathered(g_w1, "col", layer), Gathered(g_w2, "row", layer), f"mlp{layer}")
        queue += [(("mlp_w1", layer), dw1), (("mlp_w2", layer), row_slabs(dw2))]
        if layer % 2 == 0:
            keys = []

            def exchange_with(dw_out, i=i, layer=layer, keys=keys):
                if layer == 0:
                    queue.append((("att_w_out", i), row_slabs(dw_out)))
                ex, got = take_queue([("small_early", small_early())] if layer == 0 else ())
                keys += got
                return ex

            dh, dhb, dw_in, dw_out, sm_g, arrived = attention_bwd(
                dh, dhb, sv, w["att_norm"][i], att_w_in[i], w["att_sink"][i], w["att_qnorm"][i], w["att_knorm"][i],
                att_w_out[i], tables, f"att{i}", exchange_with)
            recv.update(zip(keys, arrived))
            queue.append((("att_w_in", i), col_slabs(dw_in)))
            if layer != 0:
                queue.append((("att_w_out", i), row_slabs(dw_out)))
            gs["att_norm"][i], gs["att_sink"][i] = sm_g["norm"], sm_g["sink"]
            gs["att_qnorm"][i], gs["att_knorm"][i] = sm_g["qnorm"], sm_g["knorm"]
        else:
            dh, dhb, dw_in, dw_out, sm_g = sgu_bwd(
                dh, dhb, sv, vecs[i, 0], Gathered(g_sgu_in, "col", i), vecs[i, 1], vecs[i, 2], w_s_bf[i], b_rows[i],
                Gathered(g_sgu_out, "row", i), f"sgu{i}")
            dvec = jnp.stack([sm_g["norm"], sm_g["ln_g"], sm_g["ln_b"]])
            queue += [(("sgu_w_in", i), dw_in), (("sgu_w_out", i), row_slabs(dw_out)), (("sgu_vecs", i), col_slabs(dvec))]
            gs["sgu_w_s"][i], gs["sgu_b_s"][i] = sm_g["w_s"], sm_g["b_s"]
    grad_x = dh
    late = dict(att_norm=jnp.stack(gs["att_norm"]), att_sink=jnp.stack(gs["att_sink"]), att_qnorm=jnp.stack(gs["att_qnorm"]),
                att_knorm=jnp.stack(gs["att_knorm"]))
    last, keys = take_queue([("small_late", _flat(late, SMALL_LATE)[0])])
    recv.update(zip(keys, exchange_only(last, "exchange_last")))

    outs = [{}, {}, {}, {}]
    for n in SHARDED:
        res = adamw([recv[(n, l)] for l in range(w[n].shape[0])], w[n], m[n], v[n], f"adamw_{n}")
        for o, r in zip(outs, res):
            o[n] = r
    stack_vecs = lambda src: jnp.stack([src[n] for n in SGU_VECS], axis=1)
    res = adamw([recv[("sgu_vecs", i)] for i in range(n_sgu)], stack_vecs(w), stack_vecs(m), stack_vecs(v), "adamw_sgu_vecs")
    for o, r in zip(outs, res):
        o.update({n: r[:, k] for k, n in enumerate(SGU_VECS)})
    zero = {"loss": jnp.zeros((1,), F32)}
    for names, key in ((SMALL_EARLY, "small_early"), (SMALL_LATE, "small_late")):
        res = adamw([recv[key]], _flat({**w, **zero}, names), _flat({**m, **zero}, names), _flat({**v, **zero}, names), f"adamw_{key}")
        for o, r in zip(outs, res):
            o.update(_unflat(r, {**w, **zero}, names))
    loss = outs[0]["loss"][0]
    return loss, grad_x, *outs
```

```python
import functools
import math

import jax
import jax.numpy as jnp
from jax import lax
from jax.experimental import pallas as pl
from jax.experimental.pallas import tpu as pltpu

F32 = jnp.float32
BF16 = jnp.bfloat16

HEAD_DIM = 64
A_HEADS = 8
A_KV_HEADS = 2
B_HEADS = 8
B_KV_HEADS = 2
WINDOW = 128
BLOCK = 128
ROPE_THETA = 10000.0
GRID_W = 64
SGU_GROUPS = 8
SGU_CHUNK = 128
EPS = 1e-6
SCALE = HEAD_DIM ** -0.5
NEG = -1e30
LOG2E = math.log2(math.e)
LN2 = math.log(2.0)

A_Q = A_HEADS * HEAD_DIM
A_KV = A_KV_HEADS * HEAD_DIM
B_Q = B_HEADS * HEAD_DIM
B_KV = B_KV_HEADS * HEAD_DIM
OFF_QA, OFF_KA, OFF_VA = 0, A_Q, A_Q + A_KV
OFF_QB = A_Q + 2 * A_KV
OFF_KB = OFF_QB + B_Q
OFF_VB = OFF_KB + B_KV
ATT_IN = OFF_VB + B_KV

ADAM_LR = 0.001
ADAM_B1 = 0.9
ADAM_B2 = 0.999
ADAM_EPS = 1e-08
ADAM_WD = 0.01
ADAM_STEP = 10

N_DEV = 8
LANES = 128
V7X_VMEM_LIMIT = 56 * 1024 * 1024
FLAT_COLS = 1024


def _cparams(sem, vmem=V7X_VMEM_LIMIT):
    return pltpu.CompilerParams(dimension_semantics=sem, vmem_limit_bytes=vmem)


def _dot_nn(a, b):
    return lax.dot_general(a, b, (((1,), (0,)), ((), ())), preferred_element_type=F32)


def _dot_nt(a, b):
    return lax.dot_general(a, b, (((1,), (1,)), ((), ())), preferred_element_type=F32)


def _dot_tn(a, b):
    return lax.dot_general(a, b, (((0,), (0,)), ((), ())), preferred_element_type=F32)


def _bf(x):
    return x if x.dtype == BF16 else x.astype(BF16)


def _lane(shape):
    return lax.broadcasted_iota(jnp.int32, shape, len(shape) - 1)


def _seg_matrix(rows_lo, rows_hi):
    r = lax.broadcasted_iota(jnp.int32, (LANES, LANES), 0)
    return jnp.where((r >= rows_lo) & (r < rows_hi), 1.0, 0.0).astype(BF16)


def _group_matrix(width):
    r = lax.broadcasted_iota(jnp.int32, (LANES, LANES), 0)
    c = lax.broadcasted_iota(jnp.int32, (LANES, LANES), 1)
    return jnp.where((r // width) == (c // width), 1.0, 0.0).astype(BF16)


def _dot_f32_by_ones(s, ones_bf16):
    hi = s.astype(BF16)
    lo = (s - hi.astype(F32)).astype(BF16)
    return _dot_nn(hi, ones_bf16) + _dot_nn(lo, ones_bf16)


def _swap_halves(x, width):
    half = width // 2
    first = (_lane(x.shape) % width) < half
    return jnp.where(first, pltpu.roll(x, LANES - half, 1), pltpu.roll(x, half, 1))


def rmsnorm_fwd(x, g, name):
    t, d = x.shape
    tm = min(t, 512)

    def body(x_ref, g_ref, h_ref):
        xf = x_ref[...]
        r = lax.rsqrt(jnp.mean(xf * xf, axis=-1, keepdims=True) + EPS)
        h_ref[...] = (xf * r * g_ref[...]).astype(BF16)

    return pl.pallas_call(
        body, name=name, grid=(t // tm,),
        in_specs=[pl.BlockSpec((tm, d), lambda i: (i, 0)), pl.BlockSpec((1, d), lambda i: (0, 0))],
        out_specs=pl.BlockSpec((tm, d), lambda i: (i, 0)),
        out_shape=jax.ShapeDtypeStruct((t, d), BF16),
        compiler_params=_cparams(("parallel",)),
    )(x, g.reshape(1, d))


def _fit(n, want):
    t = min(n, want)
    while n % t:
        t //= 2
    return t


class Gathered:
    def __init__(self, arr, kind, layer):
        self.arr, self.kind, self.layer = arr, kind, layer
        _, _, self.rows, self.cols = arr.shape
        self.shape = (N_DEV * self.rows, self.cols) if kind == "row" else (self.rows, N_DEV * self.cols)


def _b_operand(b, mode, tn, tk, idx):
    dot = {"nn": _dot_nn, "nt": _dot_nt, "tn": _dot_tn}[mode]
    if not isinstance(b, Gathered):
        if mode == "nt":
            spec = pl.BlockSpec((tn, tk), lambda *g: idx(*g))
        else:
            spec = pl.BlockSpec((tk, tn), lambda *g: idx(*g)[::-1])
        return b, spec, lambda av, ref: dot(av, _bf(ref[...]))
    lay, rows, cols = b.layer, b.rows, b.cols
    if mode == "nn" and b.kind == "col":
        s = tn // cols
        assert s * cols == tn
        spec = pl.BlockSpec((s, None, tk, cols), lambda *g: (idx(*g)[0], lay, idx(*g)[1], 0))
        return b.arr, spec, lambda av, ref: jnp.concatenate([_dot_nn(av, ref[c]) for c in range(s)], axis=1)
    if mode == "nn" and b.kind == "row":
        s = tk // rows
        assert s * rows == tk
        spec = pl.BlockSpec((s, None, rows, tn), lambda *g: (idx(*g)[1], lay, 0, idx(*g)[0]))
        return b.arr, spec, lambda av, ref: _dot_nn(av, ref[...].reshape(s * rows, tn))
    if mode == "nt" and b.kind == "row":
        s = tn // rows
        assert s * rows == tn
        spec = pl.BlockSpec((s, None, rows, tk), lambda *g: (idx(*g)[0], lay, 0, idx(*g)[1]))
        return b.arr, spec, lambda av, ref: _dot_nt(av, ref[...].reshape(s * rows, tk))
    if mode == "nt" and b.kind == "col":
        s = tk // cols
        assert s * cols == tk
        spec = pl.BlockSpec((s, None, tn, cols), lambda *g: (idx(*g)[1], lay, idx(*g)[0], 0))

        def prod(av, ref):
            tot = _dot_nt(av[:, :cols], ref[0])
            for c in range(1, s):
                tot = tot + _dot_nt(av[:, c * cols:(c + 1) * cols], ref[c])
            return tot

        return b.arr, spec, prod
    raise NotImplementedError((mode, b.kind))


def matmul(a, b, mode, name, out_dtypes, epilogue=None, extras=(), a_fn=None, out_shards=False, tm=1024, tn=1024, tk=1024):
    (m, k) = a.shape[::-1] if mode == "tn" else a.shape
    n = b.shape[0] if mode == "nt" else b.shape[1]
    if out_shards:
        tn = n // N_DEV
    tm, tn, tk = _fit(m, tm), _fit(n, tn), _fit(k, tk)
    nk = k // tk
    n_ex, n_out = len(extras), len(out_dtypes)
    if epilogue is None:
        epilogue = lambda acc: (acc,)
    b_arr, b_spec, prod = _b_operand(b, mode, tn, tk, lambda i, j, kk: (j, kk))

    def body(*refs):
        a_ref, b_ref = refs[0], refs[1]
        ex_refs = refs[2:2 + n_ex]
        out_refs = refs[2 + n_ex:2 + n_ex + n_out]
        acc_ref = refs[2 + n_ex + n_out] if nk > 1 else None
        kk = pl.program_id(2)
        av = _bf(a_ref[...])
        if a_fn is not None:
            av = a_fn(av)
        part = prod(av, b_ref)

        def finish(acc):
            outs = epilogue(acc, *[r[...] for r in ex_refs])
            for r, o in zip(out_refs, outs):
                r[...] = o.astype(r.dtype)

        if nk == 1:
            finish(part)
            return

        @pl.when(kk == 0)
        def _():
            acc_ref[...] = part

        @pl.when(kk > 0)
        def _():
            acc_ref[...] += part

        @pl.when(kk == nk - 1)
        def _():
            finish(acc_ref[...])

    if mode == "tn":
        a_spec = pl.BlockSpec((tk, tm), lambda i, j, kk: (kk, i))
    else:
        a_spec = pl.BlockSpec((tm, tk), lambda i, j, kk: (i, kk))
    mn_spec = pl.BlockSpec((tm, tn), lambda i, j, kk: (i, j))
    row_spec = pl.BlockSpec((1, tn), lambda i, j, kk: (0, j))
    if out_shards:
        out_spec = pl.BlockSpec((None, tm, tn), lambda i, j, kk: (j, i, 0))
        out_shape = [jax.ShapeDtypeStruct((N_DEV, m, tn), dt) for dt in out_dtypes]
    else:
        out_spec = mn_spec
        out_shape = [jax.ShapeDtypeStruct((m, n), dt) for dt in out_dtypes]
    outs = pl.pallas_call(
        body, name=name, grid=(m // tm, n // tn, nk),
        in_specs=[a_spec, b_spec] + [row_spec if e.shape[0] == 1 else mn_spec for e in extras],
        out_specs=[out_spec] * n_out,
        out_shape=out_shape,
        scratch_shapes=[pltpu.VMEM((tm, tn), F32)] if nk > 1 else [],
        compiler_params=_cparams(("parallel", "parallel", "arbitrary")),
    )(a, b_arr, *extras)
    return outs


def matmul_nt_normbwd(dz, w, x, g, dres, name, tm=512):
    m, k = dz.shape
    d = w.shape[0]
    tm = _fit(m, tm)
    w_arr, w_spec, prod = _b_operand(w, "nt", d, k, lambda i: (0, 0))

    def body(dz_ref, w_ref, x_ref, g_ref, dres_ref, dx_ref, dxb_ref, dg_ref):
        @pl.when(pl.program_id(0) == 0)
        def _():
            dg_ref[...] = jnp.zeros_like(dg_ref)

        dh = prod(_bf(dz_ref[...]), w_ref)
        xf = x_ref[...]
        r = lax.rsqrt(jnp.mean(xf * xf, axis=-1, keepdims=True) + EPS)
        xhat = xf * r
        dg_ref[...] += jnp.sum(dh * xhat, axis=0, keepdims=True)
        dxh = dh * g_ref[...]
        dx = r * (dxh - xhat * jnp.mean(dxh * xhat, axis=-1, keepdims=True))
        out = dres_ref[...] + dx
        dx_ref[...] = out
        dxb_ref[...] = out.astype(BF16)

    row = pl.BlockSpec((tm, d), lambda i: (i, 0))
    vec = pl.BlockSpec((1, d), lambda i: (0, 0))
    return pl.pallas_call(
        body, name=name, grid=(m // tm,),
        in_specs=[pl.BlockSpec((tm, k), lambda i: (i, 0)), w_spec, row, vec, row],
        out_specs=[row, row, vec],
        out_shape=[jax.ShapeDtypeStruct((m, d), F32), jax.ShapeDtypeStruct((m, d), BF16), jax.ShapeDtypeStruct((1, d), F32)],
        compiler_params=_cparams(("arbitrary",)),
    )(dz, w_arr, x, g.reshape(1, d), dres)


def _rope_tables(t):
    pos = lax.broadcasted_iota(jnp.int32, (t, LANES), 0)
    dim = lax.broadcasted_iota(jnp.int32, (t, LANES), 1) % HEAD_DIM

    def table(p, width):
        i = dim % (width // 2)
        ang = p.astype(F32) * (ROPE_THETA ** (-(2 * i).astype(F32) / width))
        return jnp.cos(ang), jnp.where(dim % width < width // 2, -jnp.sin(ang), jnp.sin(ang))

    cos_a, sin_a = table(pos, HEAD_DIM)
    cos_b, sin_b = table(jnp.where(dim < HEAD_DIM // 2, pos // GRID_W, pos % GRID_W), HEAD_DIM // 2)
    return cos_a, sin_a, cos_b, sin_b


def _headnorm(xs, gmat):
    return lax.rsqrt(_dot_f32_by_ones(xs * xs, gmat) * (1.0 / HEAD_DIM) + EPS)


def qkv_post_fwd(proj, tables, qn_g, kn_g, name):
    t = proj.shape[0]
    tm = min(t, 256)
    cos_a, sin_a, cos_b, sin_b = tables
    g2 = lambda g: jnp.concatenate([g, g]).reshape(1, LANES)

    def body(p_ref, ca_ref, sa_ref, cb_ref, sb_ref, qg_ref, kg_ref, qa_ref, ka_ref, va_ref, qb_ref, kb_ref, vb_ref):
        ca, sa, cb, sb = ca_ref[...], sa_ref[...], cb_ref[...], sb_ref[...]
        gmat = _group_matrix(HEAD_DIM)

        def rope_a(xs):
            return xs * ca + _swap_halves(xs, HEAD_DIM) * sa

        def norm_rope_b(xs, g):
            y = xs * _headnorm(xs, gmat) * g
            return y * cb + _swap_halves(y, HEAD_DIM // 2) * sb

        for c in range(A_Q // LANES):
            qa_ref[:, c * LANES:(c + 1) * LANES] = rope_a(p_ref[:, OFF_QA + c * LANES:OFF_QA + (c + 1) * LANES]).astype(BF16)
        ka_ref[...] = rope_a(p_ref[:, OFF_KA:OFF_KA + LANES]).astype(BF16)
        va_ref[...] = p_ref[:, OFF_VA:OFF_VA + LANES].astype(BF16)
        for c in range(B_Q // LANES):
            qb_ref[:, c * LANES:(c + 1) * LANES] = norm_rope_b(
                p_ref[:, OFF_QB + c * LANES:OFF_QB + (c + 1) * LANES], qg_ref[...]).astype(BF16)
        kb_ref[...] = norm_rope_b(p_ref[:, OFF_KB:OFF_KB + LANES], kg_ref[...]).astype(BF16)
        vb_ref[...] = p_ref[:, OFF_VB:OFF_VB + LANES].astype(BF16)

    tab = pl.BlockSpec((tm, LANES), lambda i: (i, 0))
    vec = pl.BlockSpec((1, LANES), lambda i: (0, 0))
    wide = pl.BlockSpec((tm, A_Q), lambda i: (i, 0))
    return pl.pallas_call(
        body, name=name, grid=(t // tm,),
        in_specs=[pl.BlockSpec((tm, ATT_IN), lambda i: (i, 0)), tab, tab, tab, tab, vec, vec],
        out_specs=[wide, tab, tab, wide, tab, tab],
        out_shape=[jax.ShapeDtypeStruct((t, A_Q), BF16), jax.ShapeDtypeStruct((t, LANES), BF16),
                   jax.ShapeDtypeStruct((t, LANES), BF16), jax.ShapeDtypeStruct((t, B_Q), BF16),
                   jax.ShapeDtypeStruct((t, LANES), BF16), jax.ShapeDtypeStruct((t, LANES), BF16)],
        compiler_params=_cparams(("parallel",)),
    )(proj, cos_a, sin_a, cos_b, sin_b, g2(qn_g), g2(kn_g))


def qkv_post_bwd(proj, tables, qn_g, kn_g, dqa, dka, dva, dqb, dkb, dvb, name):
    t = proj.shape[0]
    tm = min(t, 256)
    cos_a, sin_a, cos_b, sin_b = tables
    g2 = lambda g: jnp.concatenate([g, g]).reshape(1, LANES)

    def body(p_ref, ca_ref, sa_ref, cb_ref, sb_ref, qg_ref, kg_ref, dqa_ref, dka_ref, dva_ref, dqb_ref, dkb_ref, dvb_ref,
             dp_ref, dqg_ref, dkg_ref):
        ca, sa, cb, sb = ca_ref[...], sa_ref[...], cb_ref[...], sb_ref[...]
        gmat = _group_matrix(HEAD_DIM)

        @pl.when(pl.program_id(0) == 0)
        def _():
            dqg_ref[...] = jnp.zeros_like(dqg_ref)
            dkg_ref[...] = jnp.zeros_like(dkg_ref)

        def rope_a_bwd(dy):
            return dy * ca + _swap_halves(dy * sa, HEAD_DIM)

        def norm_rope_b_bwd(dout, xs, g):
            dy = dout * cb + _swap_halves(dout * sb, HEAD_DIM // 2)
            r = _headnorm(xs, gmat)
            xhat = xs * r
            dxh = dy * g
            mean = _dot_f32_by_ones(dxh * xhat, gmat) * (1.0 / HEAD_DIM)
            return r * (dxh - xhat * mean), jnp.sum(dy * xhat, axis=0, keepdims=True)

        for c in range(A_Q // LANES):
            sl = slice(c * LANES, (c + 1) * LANES)
            dp_ref[:, OFF_QA + c * LANES:OFF_QA + (c + 1) * LANES] = rope_a_bwd(dqa_ref[:, sl].astype(F32)).astype(BF16)
        dp_ref[:, OFF_KA:OFF_KA + LANES] = rope_a_bwd(dka_ref[0] + dka_ref[1]).astype(BF16)
        dp_ref[:, OFF_VA:OFF_VA + LANES] = (dva_ref[0] + dva_ref[1]).astype(BF16)
        dqg = jnp.zeros((1, LANES), F32)
        for c in range(B_Q // LANES):
            sl = slice(c * LANES, (c + 1) * LANES)
            dx, dg = norm_rope_b_bwd(dqb_ref[:, sl].astype(F32), p_ref[:, OFF_QB + c * LANES:OFF_QB + (c + 1) * LANES], qg_ref[...])
            dp_ref[:, OFF_QB + c * LANES:OFF_QB + (c + 1) * LANES] = dx.astype(BF16)
            dqg = dqg + dg
        dqg_ref[...] += dqg
        dx, dg = norm_rope_b_bwd((dkb_ref[0] + dkb_ref[1]).T, p_ref[:, OFF_KB:OFF_KB + LANES], kg_ref[...])
        dp_ref[:, OFF_KB:OFF_KB + LANES] = dx.astype(BF16)
        dkg_ref[...] += dg
        dp_ref[:, OFF_VB:OFF_VB + LANES] = (dvb_ref[0] + dvb_ref[1]).T.astype(BF16)

        @pl.when(pl.program_id(0) == t // tm - 1)
        def _():
            dqg_ref[...] = dqg_ref[...] + pltpu.roll(dqg_ref[...], HEAD_DIM, 1)
            dkg_ref[...] = dkg_ref[...] + pltpu.roll(dkg_ref[...], HEAD_DIM, 1)

    tab = pl.BlockSpec((tm, LANES), lambda i: (i, 0))
    vec = pl.BlockSpec((1, LANES), lambda i: (0, 0))
    wide = pl.BlockSpec((tm, A_Q), lambda i: (i, 0))
    slab = pl.BlockSpec((2, tm, LANES), lambda i: (0, i, 0))
    per_chunk = dkb.shape[3] // tm
    slab_t = pl.BlockSpec((2, None, LANES, tm), lambda i: (0, i // per_chunk, 0, i % per_chunk))
    full = pl.BlockSpec((tm, ATT_IN), lambda i: (i, 0))
    return pl.pallas_call(
        body, name=name, grid=(t // tm,),
        in_specs=[full, tab, tab, tab, tab, vec, vec, wide, slab, slab, wide, slab_t, slab_t],
        out_specs=[full, vec, vec],
        out_shape=[jax.ShapeDtypeStruct((t, ATT_IN), BF16), jax.ShapeDtypeStruct((1, LANES), F32),
                   jax.ShapeDtypeStruct((1, LANES), F32)],
        compiler_params=_cparams(("arbitrary",)),
    )(proj, cos_a, sin_a, cos_b, sin_b, g2(qn_g), g2(kn_g), dqa, dka, dva, dqb, dkb, dvb)


def _head_to_half(xs, head_half, kv_half):
    low = _lane(xs.shape) < HEAD_DIM
    kept = jnp.where(low if head_half == 0 else jnp.logical_not(low), xs, 0.0)
    return jnp.where(kv_half == head_half, kept, pltpu.roll(kept, HEAD_DIM, 1))


def _halves_to_heads(r0, r1, kv_half):
    low = _lane(r0.shape) < HEAD_DIM
    a = jnp.where(kv_half == 0, r0, pltpu.roll(r0, HEAD_DIM, 1))
    b = jnp.where(kv_half == 1, r1, pltpu.roll(r1, HEAD_DIM, 1))
    return jnp.where(low, a, b)


def attn_delta(o, do, name):
    t, w = o.shape
    tm = min(t, 512)
    n_heads = w // HEAD_DIM

    def body(o_ref, do_ref, d_ref):
        lo, hi = _seg_matrix(0, HEAD_DIM), _seg_matrix(HEAD_DIM, LANES)
        for c in range(w // LANES):
            sl = slice(c * LANES, (c + 1) * LANES)
            s = o_ref[:, sl].astype(F32) * do_ref[:, sl].astype(F32)
            d_ref[2 * c] = _dot_f32_by_ones(s, lo)
            d_ref[2 * c + 1] = _dot_f32_by_ones(s, hi)

    blk = pl.BlockSpec((tm, w), lambda i: (i, 0))
    return pl.pallas_call(
        body, name=name, grid=(t // tm,),
        in_specs=[blk, blk],
        out_specs=pl.BlockSpec((n_heads, tm, LANES), lambda i: (0, i, 0)),
        out_shape=jax.ShapeDtypeStruct((n_heads, t, LANES), F32),
        compiler_params=_cparams(("parallel",)),
    )(o, do)


BAND = 3 * BLOCK


def _band_offsets(rows_rep):
    qi = lax.broadcasted_iota(jnp.int32, (BLOCK, BAND), 0)
    kj = lax.broadcasted_iota(jnp.int32, (BLOCK, BAND), 1)
    return jnp.concatenate([kj - qi] * rows_rep, axis=0)


def _band(n, t, offsets):
    start = pl.multiple_of(jnp.clip((n - 1) * BLOCK, 0, t - BAND), BLOCK)
    return start, jnp.abs(offsets + (start - n * BLOCK)) <= WINDOW


def window_attn_fwd(q, k, v, sink, name, blocks_per_step=8):
    t = q.shape[0]
    assert t >= BAND
    nq = _fit(t // BLOCK, blocks_per_step)
    tq = nq * BLOCK

    def body(sink_ref, q_ref, k_ref, v_ref, o_ref, lse_ref):
        j, n0 = pl.program_id(0), pl.program_id(1)
        kvh = j // 2
        row = lax.broadcasted_iota(jnp.int32, (2 * BLOCK, 1), 0)
        sk = jnp.where(row < BLOCK, sink_ref[2 * j], sink_ref[2 * j + 1]) * LOG2E
        offsets = _band_offsets(2)
        bands, scores = [], []
        for u in range(nq):
            start, ok = _band(n0 * nq + u, t, offsets)
            qf = q_ref[u * BLOCK:(u + 1) * BLOCK, :].astype(F32) * (SCALE * LOG2E)
            qs = jnp.concatenate([_head_to_half(qf, 0, kvh), _head_to_half(qf, 1, kvh)], axis=0).astype(BF16)
            bands.append(pl.ds(start, BAND))
            scores.append(jnp.where(ok, _dot_nt(qs, k_ref[bands[u], :]), NEG))
        soft = []
        for s in scores:
            m = jnp.maximum(jnp.max(s, axis=-1, keepdims=True), sk)
            p = jnp.exp2(s - m)
            soft.append((p.astype(BF16), jnp.sum(p, axis=-1, keepdims=True) + jnp.exp2(sk - m), m))
        for u, (p, denom, m) in enumerate(soft):
            rows = slice(u * BLOCK, (u + 1) * BLOCK)
            o = _dot_nn(p, v_ref[bands[u], :]) / denom
            o_ref[rows, :] = _halves_to_heads(o[:BLOCK], o[BLOCK:], kvh).astype(BF16)
            lse = jnp.broadcast_to(m + jnp.log2(denom), (2 * BLOCK, LANES))
            lse_ref[0, rows, :] = lse[:BLOCK]
            lse_ref[1, rows, :] = lse[BLOCK:]

    qspec = pl.BlockSpec((tq, LANES), lambda j, n: (n, j))
    whole = pl.BlockSpec((t, LANES), lambda j, n: (0, 0))
    return pl.pallas_call(
        body, name=name, grid=(A_HEADS // 2, t // tq),
        in_specs=[pl.BlockSpec(memory_space=pltpu.SMEM), qspec, whole, whole],
        out_specs=[qspec, pl.BlockSpec((2, tq, LANES), lambda j, n: (j, n, 0))],
        out_shape=[jax.ShapeDtypeStruct((t, A_Q + B_Q), BF16), jax.ShapeDtypeStruct((A_HEADS, t, LANES), F32)],
        compiler_params=_cparams(("parallel", "parallel")),
    )(sink, q, k, v)


def window_attn_bwd(q, k, v, sink, do, lse, delta, name, blocks_per_step=4):
    t = q.shape[0]
    assert t >= BAND
    nq = _fit(t // BLOCK, blocks_per_step)
    tq = nq * BLOCK
    grp = A_HEADS // A_KV_HEADS
    gw = grp * HEAD_DIM

    def body(sink_ref, q_ref, do_ref, k_ref, v_ref, lse_ref, dl_ref, dq_ref, dk_ref, dv_ref, ds_ref):
        kvh, n0 = pl.program_id(0), pl.program_id(1)

        @pl.when(n0 == 0)
        def _():
            dk_ref[...] = jnp.zeros_like(dk_ref)
            dv_ref[...] = jnp.zeros_like(dv_ref)
            ds_ref[...] = jnp.zeros_like(ds_ref)

        rid = lax.broadcasted_iota(jnp.int32, (8, LANES), 0)
        upd = jnp.zeros((8, LANES), F32)
        offsets = _band_offsets(grp)
        for u in range(nq):
            rows = slice(u * BLOCK, (u + 1) * BLOCK)
            start, ok = _band(n0 * nq + u, t, offsets)
            band = pl.ds(start, BAND)
            qparts, doparts = [], []
            for hh in range(grp):
                sl = slice((hh // 2) * LANES, (hh // 2 + 1) * LANES)
                qparts.append(_head_to_half(q_ref[rows, sl].astype(F32) * (SCALE * LOG2E), hh % 2, kvh))
                doparts.append(_head_to_half(do_ref[rows, sl].astype(F32), hh % 2, kvh))
            qs = jnp.concatenate(qparts, axis=0).astype(BF16)
            dos = jnp.concatenate(doparts, axis=0).astype(BF16)
            lse_b = jnp.concatenate([lse_ref[hh, rows, :] for hh in range(grp)], axis=0)
            dl_b = jnp.concatenate([dl_ref[hh, rows, :] for hh in range(grp)], axis=0)
            kband, vband = k_ref[band, :], v_ref[band, :]
            s = jnp.where(ok, _dot_nt(qs, kband), NEG)
            p = jnp.exp2(s - lse_b[:, :1])
            dp = _dot_nt(dos, vband)
            dsc = (p * (dp - dl_b[:, :1])).astype(BF16)
            dv_ref[0, band, :] += _dot_tn(p.astype(BF16), dos)
            dk_ref[0, band, :] += _dot_tn(dsc, qs) * LN2
            dq = _dot_nn(dsc, kband) * SCALE
            for c in range(grp // 2):
                dq_ref[rows, c * LANES:(c + 1) * LANES] = _halves_to_heads(
                    dq[2 * c * BLOCK:(2 * c + 1) * BLOCK], dq[(2 * c + 1) * BLOCK:(2 * c + 2) * BLOCK], kvh).astype(dq_ref.dtype)
            for hh in range(grp):
                rs = slice(hh * BLOCK, (hh + 1) * BLOCK)
                tot = jnp.sum(jnp.exp2(sink_ref[kvh * grp + hh] * LOG2E - lse_b[rs]) * dl_b[rs], axis=0, keepdims=True)
                upd = upd + jnp.where(rid == hh, -tot, 0.0)
        ds_ref[0] += upd

    qspec = pl.BlockSpec((tq, gw), lambda kvh, n: (n, kvh))
    whole = pl.BlockSpec((t, LANES), lambda kvh, n: (0, 0))
    stat = pl.BlockSpec((grp, tq, LANES), lambda kvh, n: (kvh, n, 0))
    slab = pl.BlockSpec((1, t, LANES), lambda kvh, n: (kvh, 0, 0))
    return pl.pallas_call(
        body, name=name, grid=(A_KV_HEADS, t // tq),
        in_specs=[pl.BlockSpec(memory_space=pltpu.SMEM), qspec, qspec, whole, whole, stat, stat],
        out_specs=[qspec, slab, slab, pl.BlockSpec((1, 8, LANES), lambda kvh, n: (kvh, 0, 0))],
        out_shape=[jax.ShapeDtypeStruct((t, A_Q), BF16), jax.ShapeDtypeStruct((A_KV_HEADS, t, LANES), F32),
                   jax.ShapeDtypeStruct((A_KV_HEADS, t, LANES), F32), jax.ShapeDtypeStruct((A_KV_HEADS, 8, LANES), F32)],
        compiler_params=_cparams(("arbitrary", "arbitrary")),
    )(sink, q, do, k, v, lse, delta)


def flash_attn_fwd(q, k, v, cat, name, exchange=None, tq=256, tk=2048, ahead=2):
    t = q.shape[0]
    tq, tk = _fit(t, tq), _fit(t, tk)
    nk = t // tk

    def body(q_ref, k_ref, v_ref, cat_ref, o_ref, lse_ref):
        del cat_ref
        kvh = pl.program_id(0) // 2
        qf = q_ref[...].astype(F32) * (SCALE * LOG2E)
        qs = jnp.concatenate([_head_to_half(qf, 0, kvh), _head_to_half(qf, 1, kvh)], axis=0).astype(BF16)
        mine = (_lane((tk, LANES)) < HEAD_DIM) == (kvh == 0)

        def scores(c):
            return _dot_nt(qs, k_ref[c * tk:(c + 1) * tk, :])

        s = [scores(c) for c in range(min(ahead, nk))]
        m = jnp.full((2 * tq, 1), NEG, F32)
        acc = jnp.zeros((2 * tq, LANES), F32)
        for c in range(nk):
            if c + ahead < nk:
                s.append(scores(c + ahead))
            vb = jnp.where(mine, v_ref[c * tk:(c + 1) * tk, :], jnp.ones((), BF16))
            m_new = jnp.maximum(m, jnp.max(s[c], axis=-1, keepdims=True))
            p = jnp.exp2(s[c] - m_new).astype(BF16)
            acc = jnp.exp2(m - m_new) * acc + _dot_nn(p, vb)
            m = m_new
        other = pltpu.roll(acc, HEAD_DIM, 1)
        o = acc / other
        o_ref[...] = _halves_to_heads(o[:tq], o[tq:], kvh).astype(BF16)
        in_mine = (_lane(acc.shape) < HEAD_DIM) == (kvh == 0)
        lse = jnp.broadcast_to(m, acc.shape) + jnp.log2(jnp.where(in_mine, other, acc))
        lse_ref[0] = lse[:tq]
        lse_ref[1] = lse[tq:]

    qspec = pl.BlockSpec((tq, LANES), lambda j, i: (i, j))
    whole = pl.BlockSpec((t, LANES), lambda j, i: (0, 0))
    nj, ni = B_HEADS // 2, t // tq
    steps = lambda: ((pl.program_id(0) == 0) & (pl.program_id(1) == 0), (pl.program_id(0) == nj - 1) & (pl.program_id(1) == ni - 1))
    body, x_in, x_out, x_shapes, x_scratch = carried(body, exchange, 4, 2, steps)
    return pl.pallas_call(
        body, name=name, grid=(nj, ni),
        in_specs=[qspec, whole, whole, _ANY] + x_in,
        out_specs=[pl.BlockSpec((tq, LANES), lambda j, i: (i, A_Q // LANES + j)),
                   pl.BlockSpec((2, tq, LANES), lambda j, i: (j, i, 0))] + x_out,
        out_shape=[jax.ShapeDtypeStruct(cat.shape, BF16), jax.ShapeDtypeStruct((B_HEADS, t, LANES), F32)] + x_shapes,
        scratch_shapes=x_scratch,
        input_output_aliases={3: 0},
        compiler_params=_cparams(("arbitrary", "arbitrary")),
    )(q, k, v, cat, *(exchange.arrays if exchange else ()))


def flash_attn_bwd(q, k, v, do, lse, delta, name, exchange=None, tq=256, tk=512):
    t = q.shape[0]
    tq, tk = _fit(t, tq), _fit(t, tk)
    nk = t // tk
    together = _fit(nk, 8)
    grp = B_HEADS // B_KV_HEADS
    gw = grp * HEAD_DIM

    def body(q_ref, do_ref, k_ref, v_ref, lse_ref, dl_ref, dq_ref, dk_ref, dv_ref, dq_s):
        kvh, i = pl.program_id(0), pl.program_id(1)

        @pl.when(i == 0)
        def _():
            dk_ref[...] = jnp.zeros_like(dk_ref)
            dv_ref[...] = jnp.zeros_like(dv_ref)

        qparts, doparts = [], []
        for hh in range(grp):
            sl = slice((hh // 2) * LANES, (hh // 2 + 1) * LANES)
            qparts.append(_head_to_half(q_ref[:, sl].astype(F32) * (SCALE * LOG2E), hh % 2, kvh))
            doparts.append(_head_to_half(do_ref[:, sl].astype(F32), hh % 2, kvh))
        qf, dof = jnp.concatenate(qparts, axis=0), jnp.concatenate(doparts, axis=0)
        qs, dos = qf.astype(BF16), dof.astype(BF16)
        qs_t, dos_t = qf.T.astype(BF16), dof.T.astype(BF16)
        lse = jnp.tile(jnp.concatenate([lse_ref[hh] for hh in range(grp)], axis=0), (1, tk // LANES))
        dl = jnp.tile(jnp.concatenate([dl_ref[hh] for hh in range(grp)], axis=0), (1, tk // LANES))
        dq_s[...] = jnp.zeros_like(dq_s)

        def chunks(c0, carry):
            cs = [c0 * together + u for u in range(together)]
            kbs = [k_ref[pl.ds(pl.multiple_of(c * tk, tk), tk), :] for c in cs]
            vbs = [v_ref[pl.ds(pl.multiple_of(c * tk, tk), tk), :] for c in cs]
            ss = [_dot_nt(qs, kb) for kb in kbs]
            dps = [_dot_nt(dos, vb) for vb in vbs]
            for c, kb, s, dp in zip(cs, kbs, ss, dps):
                p = jnp.exp2(s - lse)
                dsc = (p * (dp - dl)).astype(BF16)
                dv_ref[0, c] += _dot_nn(dos_t, p.astype(BF16))
                dk_ref[0, c] += _dot_nn(qs_t, dsc) * LN2
                dq_s[...] += _dot_nn(kb.T, dsc.T)
            return carry

        lax.fori_loop(0, nk // together, chunks, 0)
        dq = dq_s[...].T
        for c in range(grp // 2):
            dq_ref[:, c * LANES:(c + 1) * LANES] = (_halves_to_heads(
                dq[2 * c * tq:(2 * c + 1) * tq], dq[(2 * c + 1) * tq:(2 * c + 2) * tq], kvh) * SCALE).astype(dq_ref.dtype)

    qspec = pl.BlockSpec((tq, gw), lambda kvh, i: (i, kvh))
    dospec = pl.BlockSpec((tq, gw), lambda kvh, i: (i, A_Q // gw + kvh))
    whole = pl.BlockSpec((t, LANES), lambda kvh, i: (0, 0))
    stat = pl.BlockSpec((grp, tq, LANES), lambda kvh, i: (kvh, i, 0))
    dlstat = pl.BlockSpec((grp, tq, LANES), lambda kvh, i: (A_HEADS // grp + kvh, i, 0))
    slab = pl.BlockSpec((1, nk, LANES, tk), lambda kvh, i: (kvh, 0, 0, 0))
    ni = t // tq
    steps = lambda: ((pl.program_id(0) == 0) & (pl.program_id(1) == 0),
                     (pl.program_id(0) == B_KV_HEADS - 1) & (pl.program_id(1) == ni - 1))
    body, x_in, x_out, x_shapes, x_scratch = carried(body, exchange, 6, 3, steps)
    return pl.pallas_call(
        body, name=name, grid=(B_KV_HEADS, ni),
        in_specs=[qspec, dospec, whole, whole, stat, dlstat] + x_in,
        out_specs=[qspec, slab, slab] + x_out,
        out_shape=[jax.ShapeDtypeStruct((t, B_Q), BF16), jax.ShapeDtypeStruct((B_KV_HEADS, nk, LANES, tk), F32),
                   jax.ShapeDtypeStruct((B_KV_HEADS, nk, LANES, tk), F32)] + x_shapes,
        scratch_shapes=[pltpu.VMEM((LANES, grp * tq), F32)] + x_scratch,
        compiler_params=_cparams(("arbitrary", "arbitrary")),
    )(q, do, k, v, lse, delta, *(exchange.arrays if exchange else ()))


_GELU_C = math.sqrt(2.0 / math.pi)
_GELU_A = 0.044715


def _gelu(x):
    return 0.5 * x * (1.0 + jnp.tanh(_GELU_C * (x + _GELU_A * x * x * x)))


def _gelu_grad(x):
    th = jnp.tanh(_GELU_C * (x + _GELU_A * x * x * x))
    return 0.5 * (1.0 + th) + 0.5 * x * (1.0 - th * th) * _GELU_C * (1.0 + 3.0 * _GELU_A * x * x)


def _layernorm_stats(vf):
    mu = jnp.mean(vf, axis=-1, keepdims=True)
    vc = vf - mu
    r = lax.rsqrt(jnp.mean(vc * vc, axis=-1, keepdims=True) + EPS)
    return vc * r, r


def sgu_mix_fwd(z, ln_g, ln_b, w_s, b_rows, name):
    t, w2 = z.shape
    w = w2 // 2
    dg = w // SGU_GROUPS

    def body(u_ref, v_ref, g_ref, b_ref, ws_ref, bb_ref, y_ref):
        vhat, _ = _layernorm_stats(v_ref[...].astype(F32))
        vn = (vhat * g_ref[...] + b_ref[...]).astype(BF16)
        for g in range(SGU_GROUPS):
            sl = slice(g * dg, (g + 1) * dg)
            mixed = _dot_nn(ws_ref[g], vn[:, sl]) + bb_ref[g]
            y_ref[:, sl] = (u_ref[:, sl].astype(F32) * mixed).astype(BF16)

    vec = pl.BlockSpec((1, w), lambda n: (0, 0))
    whole = pl.BlockSpec((SGU_GROUPS, SGU_CHUNK, SGU_CHUNK), lambda n: (0, 0, 0))
    return pl.pallas_call(
        body, name=name, grid=(t // SGU_CHUNK,),
        in_specs=[pl.BlockSpec((SGU_CHUNK, w), lambda n: (n, 0)), pl.BlockSpec((SGU_CHUNK, w), lambda n: (n, 1)),
                  vec, vec, whole, whole],
        out_specs=pl.BlockSpec((SGU_CHUNK, w), lambda n: (n, 0)),
        out_shape=jax.ShapeDtypeStruct((t, w), BF16),
        compiler_params=_cparams(("parallel",)),
    )(z, z, ln_g.reshape(1, w), ln_b.reshape(1, w), w_s, b_rows)


def sgu_mix_bwd(z, apre, dy, ln_g, ln_b, w_s, b_rows, name):
    t, w2 = z.shape
    w = w2 // 2
    dg = w // SGU_GROUPS

    def body(u_ref, v_ref, au_ref, av_ref, dy_ref, g_ref, b_ref, ws_ref, bb_ref, da_ref, dlg_ref, dlb_ref, dws_ref, dbs_ref):
        @pl.when(pl.program_id(0) == 0)
        def _():
            dlg_ref[...] = jnp.zeros_like(dlg_ref)
            dlb_ref[...] = jnp.zeros_like(dlb_ref)
            dws_ref[...] = jnp.zeros_like(dws_ref)
            dbs_ref[...] = jnp.zeros_like(dbs_ref)

        vhat, r = _layernorm_stats(v_ref[...].astype(F32))
        gam = g_ref[...]
        vn = (vhat * gam + b_ref[...]).astype(BF16)
        ones8 = jnp.ones((8, dg), BF16)
        rid = lax.broadcasted_iota(jnp.int32, (8, SGU_CHUNK), 0)
        dbs = jnp.zeros((8, SGU_CHUNK), F32)
        dvn_parts = []
        for g in range(SGU_GROUPS):
            sl = slice(g * dg, (g + 1) * dg)
            dyg = dy_ref[:, sl].astype(F32)
            mixed = _dot_nn(ws_ref[g], vn[:, sl]) + bb_ref[g]
            da_ref[:, sl] = (dyg * mixed * _gelu_grad(au_ref[:, sl].astype(F32))).astype(BF16)
            dmix = dyg * u_ref[:, sl].astype(F32)
            dm_hi = dmix.astype(BF16)
            dm_lo = (dmix - dm_hi.astype(F32)).astype(BF16)
            dws_ref[g] += _dot_nt(dm_hi, vn[:, sl])
            dbs = dbs + jnp.where(rid == g, _dot_nt(ones8, dm_hi) + _dot_nt(ones8, dm_lo), 0.0)
            dvn_parts.append(_dot_tn(ws_ref[g], dm_hi))
        dbs_ref[...] += dbs
        dvn = jnp.concatenate(dvn_parts, axis=1)
        dlg_ref[...] += jnp.sum(dvn * vhat, axis=0, keepdims=True)
        dlb_ref[...] += jnp.sum(dvn, axis=0, keepdims=True)
        dvh = dvn * gam
        dv = r * (dvh - jnp.mean(dvh, axis=-1, keepdims=True) - vhat * jnp.mean(dvh * vhat, axis=-1, keepdims=True))
        da_ref[:, w:] = (dv * _gelu_grad(av_ref[...].astype(F32))).astype(BF16)

    vec = pl.BlockSpec((1, w), lambda n: (0, 0))
    whole = pl.BlockSpec((SGU_GROUPS, SGU_CHUNK, SGU_CHUNK), lambda n: (0, 0, 0))
    left = pl.BlockSpec((SGU_CHUNK, w), lambda n: (n, 0))
    right = pl.BlockSpec((SGU_CHUNK, w), lambda n: (n, 1))
    return pl.pallas_call(
        body, name=name, grid=(t // SGU_CHUNK,),
        in_specs=[left, right, left, right, left, vec, vec, whole, whole],
        out_specs=[pl.BlockSpec((SGU_CHUNK, w2), lambda n: (n, 0)), vec, vec, whole,
                   pl.BlockSpec((SGU_GROUPS, SGU_CHUNK), lambda n: (0, 0))],
        out_shape=[jax.ShapeDtypeStruct((t, w2), BF16), jax.ShapeDtypeStruct((1, w), F32), jax.ShapeDtypeStruct((1, w), F32),
                   jax.ShapeDtypeStruct((SGU_GROUPS, SGU_CHUNK, SGU_CHUNK), F32),
                   jax.ShapeDtypeStruct((SGU_GROUPS, SGU_CHUNK), F32)],
        compiler_params=_cparams(("arbitrary",)),
    )(z, z, apre, apre, dy, ln_g.reshape(1, w), ln_b.reshape(1, w), w_s, b_rows)


def loss_head(h, g, target, name):
    t, d = h.shape
    tm = min(t, 512)

    def body(h_ref, g_ref, t_ref, loss_ref, dh_ref, dhb_ref, dg_ref):
        @pl.when(pl.program_id(0) == 0)
        def _():
            loss_ref[...] = jnp.zeros_like(loss_ref)
            dg_ref[...] = jnp.zeros_like(dg_ref)

        xf = h_ref[...]
        r = lax.rsqrt(jnp.mean(xf * xf, axis=-1, keepdims=True) + EPS)
        xhat = xf * r
        err = xhat * g_ref[...] - t_ref[...]
        per_tok = jnp.mean(err * err, axis=-1, keepdims=True)
        loss_ref[...] += 0.5 * jnp.sum(per_tok, axis=0, keepdims=True)
        dy = err * (1.0 / d)
        dg_ref[...] += jnp.sum(dy * xhat, axis=0, keepdims=True)
        dxh = dy * g_ref[...]
        dh = r * (dxh - xhat * jnp.mean(dxh * xhat, axis=-1, keepdims=True))
        dh_ref[...] = dh
        dhb_ref[...] = dh.astype(BF16)

    row = pl.BlockSpec((tm, d), lambda i: (i, 0))
    vec = pl.BlockSpec((1, d), lambda i: (0, 0))
    return pl.pallas_call(
        body, name=name, grid=(t // tm,),
        in_specs=[row, vec, row],
        out_specs=[pl.BlockSpec((1, LANES), lambda i: (0, 0)), row, row, vec],
        out_shape=[jax.ShapeDtypeStruct((1, LANES), F32), jax.ShapeDtypeStruct((t, d), F32), jax.ShapeDtypeStruct((t, d), BF16),
                   jax.ShapeDtypeStruct((1, d), F32)],
        compiler_params=_cparams(("arbitrary",)),
    )(h, g.reshape(1, d), target)


ADAMW_BLOCK_BYTES = 1 << 20


def adamw(parts, w, m, v, name):
    n_layers, r, c = w.shape
    row_bytes = n_layers * c * 4
    if r * row_bytes <= 2 * ADAMW_BLOCK_BYTES:
        tr = r
    else:
        tr = _fit(r, 1 << int(math.log2(max(8, ADAMW_BLOCK_BYTES // row_bytes))))
    bc1 = 1.0 - ADAM_B1 ** ADAM_STEP
    bc2 = 1.0 - ADAM_B2 ** ADAM_STEP

    def body(*refs):
        p_refs = refs[:n_layers]
        w_ref, m_ref, v_ref, g_ref, d_ref, nm_ref, nv_ref = refs[n_layers:]
        for l in range(n_layers):
            g = p_refs[l][0].astype(F32)
            for j in range(1, N_DEV):
                g = g + p_refs[l][j].astype(F32)
            nm = ADAM_B1 * m_ref[l] + (1.0 - ADAM_B1) * g
            nv = ADAM_B2 * v_ref[l] + (1.0 - ADAM_B2) * (g * g)
            g_ref[l] = g
            nm_ref[l] = nm
            nv_ref[l] = nv
            d_ref[l] = -ADAM_LR * ((nm / bc1) / (jnp.sqrt(nv / bc2) + ADAM_EPS) + ADAM_WD * w_ref[l])

    blk = pl.BlockSpec((n_layers, tr, c), lambda i: (0, i, 0))
    return pl.pallas_call(
        body, name=name, grid=(r // tr,),
        in_specs=[pl.BlockSpec((N_DEV, tr, c), lambda i: (0, i, 0))] * n_layers + [blk, blk, blk],
        out_specs=[blk] * 4,
        out_shape=[jax.ShapeDtypeStruct((n_layers, r, c), F32)] * 4,
        compiler_params=_cparams(("parallel",)),
    )(*parts, w, m, v)


_ANY = pl.BlockSpec(memory_space=pl.ANY)


def _mesh_pos():
    return lax.axis_index("x"), lax.axis_index("y"), lax.axis_index("c")


class Exchange:
    def __init__(self, gathers=(), scatters=()):
        self.items = [("gather", a) for a in gathers] + [("scatter", a) for a in scatters]
        self.arrays = [a for _, a in self.items]
        self.n = len(self.items)

    def out_shapes(self):
        return [jax.ShapeDtypeStruct(((N_DEV,) + a.shape) if kind == "gather" else a.shape, a.dtype) for kind, a in self.items]

    def scratch(self):
        return [pltpu.SemaphoreType.DMA((7 * self.n,)), pltpu.SemaphoreType.DMA((7 * self.n,)), pltpu.SemaphoreType.DMA((self.n,))]

    def _copies(self, in_refs, out_refs, send_sems, recv_sems, local_sems):
        x, y, c = _mesh_pos()
        me = 4 * x + 2 * y + c
        local, sends, arrivals = [], [], []
        for t, (kind, _) in enumerate(self.items):
            src_of = (lambda slot, r=in_refs[t]: r) if kind == "gather" else (lambda slot, r=in_refs[t]: r.at[slot])
            local.append(pltpu.make_async_copy(src_of(me), out_refs[t].at[me], local_sems.at[t]))
            for k in range(1, N_DEV):
                px = 1 - x if k & 4 else x
                py = 1 - y if k & 2 else y
                pc = 1 - c if k & 1 else c
                pid = 4 * px + 2 * py + pc
                kw = dict(send_sem=send_sems.at[7 * t + k - 1], recv_sem=recv_sems.at[7 * t + k - 1],
                          device_id=(px, py, pc), device_id_type=pl.DeviceIdType.MESH)
                sends.append(pltpu.make_async_remote_copy(src_ref=src_of(pid), dst_ref=out_refs[t].at[me], **kw))
                arrivals.append(pltpu.make_async_remote_copy(src_ref=src_of(pid), dst_ref=out_refs[t].at[pid], **kw))
        return local, sends, arrivals

    def start(self, *refs):
        local, sends, _ = self._copies(*refs)
        for cp in local + sends:
            cp.start()

    def wait(self, *refs):
        local, sends, arrivals = self._copies(*refs)
        for cp in arrivals:
            cp.wait_recv()
        for cp in sends:
            cp.wait_send()
        for cp in local:
            cp.wait()


def carried(body, exchange, n_in, n_out, first_last):
    if exchange is None:
        return body, [], [], [], []
    nx = exchange.n

    def wrapped(*refs):
        ins, xin = refs[:n_in], refs[n_in:n_in + nx]
        outs, xout = refs[n_in + nx:n_in + nx + n_out], refs[n_in + nx + n_out:n_in + 2 * nx + n_out]
        scratch, sems = refs[n_in + 2 * nx + n_out:-3], refs[-3:]
        first, last = first_last()

        @pl.when(first)
        def _():
            exchange.start(xin, xout, *sems)

        body(*ins, *outs, *scratch)

        @pl.when(last)
        def _():
            exchange.wait(xin, xout, *sems)

    return wrapped, [_ANY] * nx, [_ANY] * nx, exchange.out_shapes(), exchange.scratch()


def exchange_only(exchange, name):
    def body(*refs):
        xin, xout, sems = refs[:exchange.n], refs[exchange.n:2 * exchange.n], refs[-3:]
        exchange.start(xin, xout, *sems)
        exchange.wait(xin, xout, *sems)

    return pl.pallas_call(
        body, name=name, in_specs=[_ANY] * exchange.n, out_specs=[_ANY] * exchange.n,
        out_shape=exchange.out_shapes(), scratch_shapes=exchange.scratch(),
    )(*exchange.arrays)


def _residual_out(a, w_out, x, next_g, name, **tiles):
    if next_g is None:
        (y,) = matmul(a, w_out, "nn", name, [F32], epilogue=lambda acc, r: (r + acc,), extras=(x,), **tiles)
        return y, None

    def add_and_norm(acc, r, g):
        y = r + acc
        return y, y * lax.rsqrt(jnp.mean(y * y, axis=-1, keepdims=True) + EPS) * g

    assert w_out.shape[1] <= tiles.get("tn", 1024)
    return matmul(a, w_out, "nn", name, [F32, BF16], epilogue=add_and_norm, extras=(x, next_g.reshape(1, -1)), **tiles)


def attention_fwd(x, h, w_in, sink, qn_g, kn_g, w_out, tables, next_g, tag, exchange=None):
    (proj,) = matmul(h, w_in, "nn", f"{tag}_proj", [F32], tn=ATT_IN)
    qa, ka, va, qb, kb, vb = qkv_post_fwd(proj, tables, qn_g, kn_g, f"{tag}_qkv")
    cat, lse_a = window_attn_fwd(qa, ka, va, sink, f"{tag}_win")
    cat, lse_b, *arrived = flash_attn_fwd(qb, kb, vb, cat, f"{tag}_flash", exchange)
    if callable(w_out):
        w_out = w_out(arrived)
    y, h_next = _residual_out(cat, w_out, x, next_g, f"{tag}_out")
    saved = (x, h, proj, qa, ka, va, qb, kb, vb, cat, lse_a, lse_b)
    return y, h_next, saved, arrived


def attention_bwd(dy, dyb, saved, norm_g, w_in, sink, qn_g, kn_g, w_out, tables, tag, exchange_with=None):
    x, h, proj, qa, ka, va, qb, kb, vb, cat, lse_a, lse_b = saved
    (dcat,) = matmul(dyb, w_out, "nt", f"{tag}_dcat", [BF16])
    (dw_out,) = matmul(cat, dyb, "tn", f"{tag}_dwout", [BF16], tk=4096)
    delta = attn_delta(cat, dcat, f"{tag}_delta")
    dqa, dka, dva, dsink = window_attn_bwd(qa, ka, va, sink, dcat, lse_a, delta, f"{tag}_dwin")
    exchange = exchange_with(dw_out) if exchange_with else None
    dqb, dkb, dvb, *arrived = flash_attn_bwd(qb, kb, vb, dcat, lse_b, delta, f"{tag}_dflash", exchange)
    dproj, dqg, dkg = qkv_post_bwd(proj, tables, qn_g, kn_g, dqa, dka, dva, dqb, dkb, dvb, f"{tag}_dqkv")
    (dw_in,) = matmul(h, dproj, "tn", f"{tag}_dwin_w", [BF16], tn=ATT_IN // 2, tk=2048)
    dx, dxb, dg = matmul_nt_normbwd(dproj, w_in, x, norm_g, dy, f"{tag}_dx")
    grp = A_HEADS // A_KV_HEADS
    small = dict(norm=dg[0], sink=dsink[:, :grp, 0].reshape(A_HEADS), qnorm=dqg[0, :HEAD_DIM], knorm=dkg[0, :HEAD_DIM])
    return dx, dxb, dw_in, dw_out, small, arrived


def sgu_fwd(x, h, w_in, ln_g, ln_b, w_s, b_rows, w_out, next_g, tag):
    apre, z = matmul(h, w_in, "nn", f"{tag}_in", [BF16, BF16], epilogue=lambda acc: (acc, _gelu(acc)))
    y = sgu_mix_fwd(z, ln_g, ln_b, w_s, b_rows, f"{tag}_mix")
    out, h_next = _residual_out(y, w_out, x, next_g, f"{tag}_out")
    return out, h_next, (x, h, apre, z, y)


def sgu_bwd(dout, doutb, saved, norm_g, w_in, ln_g, ln_b, w_s, b_rows, w_out, tag):
    x, h, apre, z, y = saved
    (dy,) = matmul(doutb, w_out, "nt", f"{tag}_dy", [BF16])
    (dw_out,) = matmul(y, doutb, "tn", f"{tag}_dwout", [BF16], tk=4096)
    dapre, dlg, dlb, dws, dbs = sgu_mix_bwd(z, apre, dy, ln_g, ln_b, w_s, b_rows, f"{tag}_dmix")
    (dw_in,) = matmul(h, dapre, "tn", f"{tag}_dwin", [BF16], out_shards=True, tk=4096)
    dx, dxb, dg = matmul_nt_normbwd(dapre, w_in, x, norm_g, dout, f"{tag}_dx")
    small = dict(norm=dg[0], ln_g=dlg[0], ln_b=dlb[0], w_s=dws, b_s=dbs)
    return dx, dxb, dw_in, dw_out, small


def _square(r):
    return r * r


def mlp_fwd(x, h, w1, w2, next_g, tag):
    (r,) = matmul(h, w1, "nn", f"{tag}_up", [BF16], epilogue=lambda acc: (jnp.maximum(acc, 0.0),), tm=2048)
    y, h_next = _residual_out(r, w2, x, next_g, f"{tag}_down", a_fn=_square, tm=512, tk=4096)
    return y, h_next, (x, h, r)


def mlp_bwd(dy, dyb, saved, norm_g, w1, w2, tag):
    x, h, r = saved
    (da,) = matmul(dyb, w2, "nt", f"{tag}_da", [BF16], epilogue=lambda acc, rr: (acc * (2.0 * rr.astype(F32)),), extras=(r,),
                   tm=2048)
    (dw2,) = matmul(r, dyb, "tn", f"{tag}_dw2", [BF16], a_fn=_square, tk=4096)
    (dw1,) = matmul(h, da, "tn", f"{tag}_dw1", [BF16], out_shards=True, tk=4096)
    dx, dxb, dg = matmul_nt_normbwd(da, w1, x, norm_g, dy, f"{tag}_dx")
    return dx, dxb, dw1, dw2, dg[0]


ORDER = ("att_norm", "att_w_in", "att_sink", "att_qnorm", "att_knorm", "att_w_out", "sgu_norm", "sgu_w_in", "sgu_ln_g",
         "sgu_ln_b", "sgu_w_s", "sgu_b_s", "sgu_w_out", "mlp_norm", "mlp_w1", "mlp_w2", "final_norm")
SHARDED = ("att_w_in", "att_w_out", "sgu_w_in", "sgu_w_out", "mlp_w1", "mlp_w2")
SGU_VECS = ("sgu_norm", "sgu_ln_g", "sgu_ln_b")
SMALL_EARLY = ("sgu_w_s", "sgu_b_s", "mlp_norm", "final_norm", "loss")
SMALL_LATE = ("att_norm", "att_sink", "att_qnorm", "att_knorm")
SMALL_ROWS_MULT = 8


def _flat(blocks, names):
    flat = jnp.concatenate([blocks[n].reshape(-1).astype(F32) for n in names])
    per = SMALL_ROWS_MULT * FLAT_COLS
    total = -(-flat.shape[0] // per) * per
    return jnp.pad(flat, (0, total - flat.shape[0])).reshape(1, total // FLAT_COLS, FLAT_COLS)


def _unflat(flat, like, names):
    out, off = {}, 0
    f = flat.reshape(-1)
    for n in names:
        size = like[n].size
        out[n] = f[off:off + size].reshape(like[n].shape)
        off += size
    return out


def kernel(x, att_norm, att_w_in, att_sink, att_qnorm, att_knorm, att_w_out, sgu_norm, sgu_w_in, sgu_ln_g, sgu_ln_b, sgu_w_s, sgu_b_s, sgu_w_out, mlp_norm, mlp_w1, mlp_w2, final_norm, loss_target, m_att_norm, m_att_w_in, m_att_sink, m_att_qnorm, m_att_knorm, m_att_w_out, m_sgu_norm, m_sgu_w_in, m_sgu_ln_g, m_sgu_ln_b, m_sgu_w_s, m_sgu_b_s, m_sgu_w_out, m_mlp_norm, m_mlp_w1, m_mlp_w2, m_final_norm, v_att_norm, v_att_w_in, v_att_sink, v_att_qnorm, v_att_knorm, v_att_w_out, v_sgu_norm, v_sgu_w_in, v_sgu_ln_g, v_sgu_ln_b, v_sgu_w_s, v_sgu_b_s, v_sgu_w_out, v_mlp_norm, v_mlp_w1, v_mlp_w2, v_final_norm):
    w = dict(att_norm=att_norm, att_w_in=att_w_in, att_sink=att_sink, att_qnorm=att_qnorm, att_knorm=att_knorm,
             att_w_out=att_w_out, sgu_norm=sgu_norm, sgu_w_in=sgu_w_in, sgu_ln_g=sgu_ln_g, sgu_ln_b=sgu_ln_b, sgu_w_s=sgu_w_s,
             sgu_b_s=sgu_b_s, sgu_w_out=sgu_w_out, mlp_norm=mlp_norm, mlp_w1=mlp_w1, mlp_w2=mlp_w2, final_norm=final_norm)
    m = dict(att_norm=m_att_norm, att_w_in=m_att_w_in, att_sink=m_att_sink, att_qnorm=m_att_qnorm, att_knorm=m_att_knorm,
             att_w_out=m_att_w_out, sgu_norm=m_sgu_norm, sgu_w_in=m_sgu_w_in, sgu_ln_g=m_sgu_ln_g, sgu_ln_b=m_sgu_ln_b,
             sgu_w_s=m_sgu_w_s, sgu_b_s=m_sgu_b_s, sgu_w_out=m_sgu_w_out, mlp_norm=m_mlp_norm, mlp_w1=m_mlp_w1, mlp_w2=m_mlp_w2,
             final_norm=m_final_norm)
    v = dict(att_norm=v_att_norm, att_w_in=v_att_w_in, att_sink=v_att_sink, att_qnorm=v_att_qnorm, att_knorm=v_att_knorm,
             att_w_out=v_att_w_out, sgu_norm=v_sgu_norm, sgu_w_in=v_sgu_w_in, sgu_ln_g=v_sgu_ln_g, sgu_ln_b=v_sgu_ln_b,
             sgu_w_s=v_sgu_w_s, sgu_b_s=v_sgu_b_s, sgu_w_out=v_sgu_w_out, mlp_norm=v_mlp_norm, mlp_w1=v_mlp_w1, mlp_w2=v_mlp_w2,
             final_norm=v_final_norm)
    loss, grad_x, g, d, nm, nv = train_step(x[0], loss_target[0], w, m, v)
    return (loss, grad_x[None], *[g[n] for n in ORDER], *[d[n] for n in ORDER], *[nm[n] for n in ORDER], *[nv[n] for n in ORDER])


def train_step(x, target, w, m, v):
    t, d_model = x.shape
    n_att, n_sgu, depth = w["att_w_in"].shape[0], w["sgu_w_in"].shape[0], w["mlp_w1"].shape[0]
    bf = lambda n: w[n].astype(BF16)

    vec_local = jnp.stack([w[n] for n in SGU_VECS], axis=1)
    att_in = bf("att_w_in")
    g_in0, g_vec = exchange_only(Exchange(gathers=[att_in[:1], vec_local]), "gather_first")
    vecs = g_vec.transpose(1, 2, 0, 3).reshape(n_sgu, len(SGU_VECS), -1)
    rest = Exchange(gathers=[att_in[1:], bf("att_w_out"), bf("sgu_w_in"), bf("sgu_w_out"), bf("mlp_w1"), bf("mlp_w2")])
    w_s_bf = w["sgu_w_s"].astype(BF16)
    b_rows = jnp.broadcast_to(w["sgu_b_s"][:, :, :, None], w["sgu_b_s"].shape + (LANES,))
    tables = _rope_tables(t)
    full_cols = lambda g: g.transpose(1, 2, 0, 3).reshape(g.shape[1], d_model, -1)

    mixer_norm = lambda layer: w["att_norm"][layer // 2] if layer % 2 == 0 else vecs[layer // 2, 0]
    saved = []
    h = rmsnorm_fwd(x, mixer_norm(0), "att0_norm")
    for layer in range(depth):
        i = layer // 2
        if layer % 2 == 0:
            if layer == 0:
                att_w_in = [full_cols(g_in0)[0]]
            x, h, sv, arrived = attention_fwd(x, h, att_w_in[i], w["att_sink"][i], w["att_qnorm"][i], w["att_knorm"][i],
                                              (lambda arrived: Gathered(arrived[1], "row", 0)) if layer == 0 else att_w_out[i],
                                              tables, w["mlp_norm"][layer], f"att{i}", rest if layer == 0 else None)
            if layer == 0:
                g_in1, g_out, g_sgu_in, g_sgu_out, g_w1, g_w2 = arrived
                att_w_in += list(full_cols(g_in1))
                att_w_out = [Gathered(g_out, "row", l) for l in range(n_att)]
        else:
            x, h, sv = sgu_fwd(x, h, Gathered(g_sgu_in, "col", i), vecs[i, 1], vecs[i, 2], w_s_bf[i], b_rows[i],
                               Gathered(g_sgu_out, "row", i), w["mlp_norm"][layer], f"sgu{i}")
        x, h, sm = mlp_fwd(x, h, Gathered(g_w1, "col", layer), Gathered(g_w2, "row", layer),
                           mixer_norm(layer + 1) if layer + 1 < depth else None, f"mlp{layer}")
        saved.append((sv, sm))
    loss_row, dh, dhb, dgf = loss_head(x, w["final_norm"], target, "loss_head")

    queue, recv = [], {}
    gs = dict(att_norm=[None] * n_att, att_sink=[None] * n_att, att_qnorm=[None] * n_att, att_knorm=[None] * n_att,
              sgu_w_s=[None] * n_sgu, sgu_b_s=[None] * n_sgu, mlp_norm=[None] * depth)

    def row_slabs(g):
        return g.reshape(N_DEV, g.shape[0] // N_DEV, g.shape[1])

    def col_slabs(g):
        return g.reshape(g.shape[0], N_DEV, g.shape[1] // N_DEV).transpose(1, 0, 2)

    def take_queue(gathers=()):
        items = list(queue)
        queue.clear()
        keys = [k for k, _ in gathers] + [k for k, _ in items]
        return Exchange(gathers=[a for _, a in gathers], scatters=[a for _, a in items]), keys

    def small_early():
        blocks = dict(sgu_w_s=jnp.stack(gs["sgu_w_s"]), sgu_b_s=jnp.stack(gs["sgu_b_s"]), mlp_norm=jnp.stack(gs["mlp_norm"]),
                      final_norm=dgf[0], loss=loss_row[0, :1])
        return _flat(blocks, SMALL_EARLY)[0]

    for layer in reversed(range(depth)):
        i = layer // 2
        sv, sm = saved[layer]
        dh, dhb, dw1, dw2, gs["mlp_norm"][layer] = mlp_bwd(
            dh, dhb, sm, w["mlp_norm"][layer], Gathered(g_w1, "col", layer), Gathered(g_w2, "row", layer), f"mlp{layer}")
        queue += [(("mlp_w1", layer), dw1), (("mlp_w2", layer), row_slabs(dw2))]
        if layer % 2 == 0:
            keys = []

            def exchange_with(dw_out, i=i, layer=layer, keys=keys):
                if layer == 0:
                    queue.append((("att_w_out", i), row_slabs(dw_out)))
                ex, got = take_queue([("small_early", small_early())] if layer == 0 else ())
                keys += got
                return ex

            dh, dhb, dw_in, dw_out, sm_g, arrived = attention_bwd(
                dh, dhb, sv, w["att_norm"][i], att_w_in[i], w["att_sink"][i], w["att_qnorm"][i], w["att_knorm"][i],
                att_w_out[i], tables, f"att{i}", exchange_with)
            recv.update(zip(keys, arrived))
            queue.append((("att_w_in", i), col_slabs(dw_in)))
            if layer != 0:
                queue.append((("att_w_out", i), row_slabs(dw_out)))
            gs["att_norm"][i], gs["att_sink"][i] = sm_g["norm"], sm_g["sink"]
            gs["att_qnorm"][i], gs["att_knorm"][i] = sm_g["qnorm"], sm_g["knorm"]
        else:
            dh, dhb, dw_in, dw_out, sm_g = sgu_bwd(
                dh, dhb, sv, vecs[i, 0], Gathered(g_sgu_in, "col", i), vecs[i, 1], vecs[i, 2], w_s_bf[i], b_rows[i],
                Gathered(g_sgu_out, "row", i), f"sgu{i}")
            dvec = jnp.stack([sm_g["norm"], sm_g["ln_g"], sm_g["ln_b"]])
            queue += [(("sgu_w_in", i), dw_in), (("sgu_w_out", i), row_slabs(dw_out)), (("sgu_vecs", i), col_slabs(dvec))]
            gs["sgu_w_s"][i], gs["sgu_b_s"][i] = sm_g["w_s"], sm_g["b_s"]
    grad_x = dh
    late = dict(att_norm=jnp.stack(gs["att_norm"]), att_sink=jnp.stack(gs["att_sink"]), att_qnorm=jnp.stack(gs["att_qnorm"]),
                att_knorm=jnp.stack(gs["att_knorm"]))
    last, keys = take_queue([("small_late", _flat(late, SMALL_LATE)[0])])
    recv.update(zip(keys, exchange_only(last, "exchange_last")))

    outs = [{}, {}, {}, {}]
    for n in SHARDED:
        res = adamw([recv[(n, l)] for l in range(w[n].shape[0])], w[n], m[n], v[n], f"adamw_{n}")
        for o, r in zip(outs, res):
            o[n] = r
    stack_vecs = lambda src: jnp.stack([src[n] for n in SGU_VECS], axis=1)
    res = adamw([recv[("sgu_vecs", i)] for i in range(n_sgu)], stack_vecs(w), stack_vecs(m), stack_vecs(v), "adamw_sgu_vecs")
    for o, r in zip(outs, res):
        o.update({n: r[:, k] for k, n in enumerate(SGU_VECS)})
    zero = {"loss": jnp.zeros((1,), F32)}
    for names, key in ((SMALL_EARLY, "small_early"), (SMALL_LATE, "small_late")):
        res = adamw([recv[key]], _flat({**w, **zero}, names), _flat({**m, **zero}, names), _flat({**v, **zero}, names), f"adamw_{key}")
        for o, r in zip(outs, res):
            o.update(_unflat(r, {**w, **zero}, names))
    loss = outs[0]["loss"][0]
    return loss, grad_x, *outs
```

```python
import functools
import math

import jax
import jax.numpy as jnp
from jax import lax
from jax.experimental import pallas as pl
from jax.experimental.pallas import tpu as pltpu

F32 = jnp.float32
BF16 = jnp.bfloat16

HEAD_DIM = 64
A_HEADS = 8
A_KV_HEADS = 2
B_HEADS = 8
B_KV_HEADS = 2
WINDOW = 128
BLOCK = 128
ROPE_THETA = 10000.0
GRID_W = 64
SGU_GROUPS = 8
SGU_CHUNK = 128
EPS = 1e-6
SCALE = HEAD_DIM ** -0.5
NEG = -1e30
LOG2E = math.log2(math.e)
LN2 = math.log(2.0)

A_Q = A_HEADS * HEAD_DIM
A_KV = A_KV_HEADS * HEAD_DIM
B_Q = B_HEADS * HEAD_DIM
B_KV = B_KV_HEADS * HEAD_DIM
OFF_QA, OFF_KA, OFF_VA = 0, A_Q, A_Q + A_KV
OFF_QB = A_Q + 2 * A_KV
OFF_KB = OFF_QB + B_Q
OFF_VB = OFF_KB + B_KV
ATT_IN = OFF_VB + B_KV

ADAM_LR = 0.001
ADAM_B1 = 0.9
ADAM_B2 = 0.999
ADAM_EPS = 1e-08
ADAM_WD = 0.01
ADAM_STEP = 10

N_DEV = 8
LANES = 128
V7X_VMEM_LIMIT = 56 * 1024 * 1024
FLAT_COLS = 1024


def _cparams(sem, vmem=V7X_VMEM_LIMIT):
    return pltpu.CompilerParams(dimension_semantics=sem, vmem_limit_bytes=vmem)


def _dot_nn(a, b):
    return lax.dot_general(a, b, (((1,), (0,)), ((), ())), preferred_element_type=F32)


def _dot_nt(a, b):
    return lax.dot_general(a, b, (((1,), (1,)), ((), ())), preferred_element_type=F32)


def _dot_tn(a, b):
    return lax.dot_general(a, b, (((0,), (0,)), ((), ())), preferred_element_type=F32)


def _bf(x):
    return x if x.dtype == BF16 else x.astype(BF16)


def _lane(shape):
    return lax.broadcasted_iota(jnp.int32, shape, len(shape) - 1)


def _seg_matrix(rows_lo, rows_hi):
    r = lax.broadcasted_iota(jnp.int32, (LANES, LANES), 0)
    return jnp.where((r >= rows_lo) & (r < rows_hi), 1.0, 0.0).astype(BF16)


def _group_matrix(width):
    r = lax.broadcasted_iota(jnp.int32, (LANES, LANES), 0)
    c = lax.broadcasted_iota(jnp.int32, (LANES, LANES), 1)
    return jnp.where((r // width) == (c // width), 1.0, 0.0).astype(BF16)


def _dot_f32_by_ones(s, ones_bf16):
    hi = s.astype(BF16)
    lo = (s - hi.astype(F32)).astype(BF16)
    return _dot_nn(hi, ones_bf16) + _dot_nn(lo, ones_bf16)


def _swap_halves(x, width):
    half = width // 2
    first = (_lane(x.shape) % width) < half
    return jnp.where(first, pltpu.roll(x, LANES - half, 1), pltpu.roll(x, half, 1))


def rmsnorm_fwd(x, g, name):
    t, d = x.shape
    tm = min(t, 512)

    def body(x_ref, g_ref, h_ref):
        xf = x_ref[...]
        r = lax.rsqrt(jnp.mean(xf * xf, axis=-1, keepdims=True) + EPS)
        h_ref[...] = (xf * r * g_ref[...]).astype(BF16)

    return pl.pallas_call(
        body, name=name, grid=(t // tm,),
        in_specs=[pl.BlockSpec((tm, d), lambda i: (i, 0)), pl.BlockSpec((1, d), lambda i: (0, 0))],
        out_specs=pl.BlockSpec((tm, d), lambda i: (i, 0)),
        out_shape=jax.ShapeDtypeStruct((t, d), BF16),
        compiler_params=_cparams(("parallel",)),
    )(x, g.reshape(1, d))


def _fit(n, want):
    t = min(n, want)
    while n % t:
        t //= 2
    return t


class Gathered:
    def __init__(self, arr, kind, layer):
        self.arr, self.kind, self.layer = arr, kind, layer
        _, _, self.rows, self.cols = arr.shape
        self.shape = (N_DEV * self.rows, self.cols) if kind == "row" else (self.rows, N_DEV * self.cols)


def _b_operand(b, mode, tn, tk, idx):
    dot = {"nn": _dot_nn, "nt": _dot_nt, "tn": _dot_tn}[mode]
    if not isinstance(b, Gathered):
        if mode == "nt":
            spec = pl.BlockSpec((tn, tk), lambda *g: idx(*g))
        else:
            spec = pl.BlockSpec((tk, tn), lambda *g: idx(*g)[::-1])
        return b, spec, lambda av, ref: dot(av, _bf(ref[...]))
    lay, rows, cols = b.layer, b.rows, b.cols
    if mode == "nn" and b.kind == "col":
        s = tn // cols
        assert s * cols == tn
        spec = pl.BlockSpec((s, None, tk, cols), lambda *g: (idx(*g)[0], lay, idx(*g)[1], 0))
        return b.arr, spec, lambda av, ref: jnp.concatenate([_dot_nn(av, ref[c]) for c in range(s)], axis=1)
    if mode == "nn" and b.kind == "row":
        s = tk // rows
        assert s * rows == tk
        spec = pl.BlockSpec((s, None, rows, tn), lambda *g: (idx(*g)[1], lay, 0, idx(*g)[0]))
        return b.arr, spec, lambda av, ref: _dot_nn(av, ref[...].reshape(s * rows, tn))
    if mode == "nt" and b.kind == "row":
        s = tn // rows
        assert s * rows == tn
        spec = pl.BlockSpec((s, None, rows, tk), lambda *g: (idx(*g)[0], lay, 0, idx(*g)[1]))
        return b.arr, spec, lambda av, ref: _dot_nt(av, ref[...].reshape(s * rows, tk))
    if mode == "nt" and b.kind == "col":
        s = tk // cols
        assert s * cols == tk
        spec = pl.BlockSpec((s, None, tn, cols), lambda *g: (idx(*g)[1], lay, idx(*g)[0], 0))

        def prod(av, ref):
            tot = _dot_nt(av[:, :cols], ref[0])
            for c in range(1, s):
                tot = tot + _dot_nt(av[:, c * cols:(c + 1) * cols], ref[c])
            return tot

        return b.arr, spec, prod
    raise NotImplementedError((mode, b.kind))


def matmul(a, b, mode, name, out_dtypes, epilogue=None, extras=(), a_fn=None, out_shards=False, tm=1024, tn=1024, tk=1024):
    (m, k) = a.shape[::-1] if mode == "tn" else a.shape
    n = b.shape[0] if mode == "nt" else b.shape[1]
    if out_shards:
        tn = n // N_DEV
    tm, tn, tk = _fit(m, tm), _fit(n, tn), _fit(k, tk)
    nk = k // tk
    n_ex, n_out = len(extras), len(out_dtypes)
    if epilogue is None:
        epilogue = lambda acc: (acc,)
    b_arr, b_spec, prod = _b_operand(b, mode, tn, tk, lambda i, j, kk: (j, kk))

    def body(*refs):
        a_ref, b_ref = refs[0], refs[1]
        ex_refs = refs[2:2 + n_ex]
        out_refs = refs[2 + n_ex:2 + n_ex + n_out]
        acc_ref = refs[2 + n_ex + n_out] if nk > 1 else None
        kk = pl.program_id(2)
        av = _bf(a_ref[...])
        if a_fn is not None:
            av = a_fn(av)
        part = prod(av, b_ref)

        def finish(acc):
            outs = epilogue(acc, *[r[...] for r in ex_refs])
            for r, o in zip(out_refs, outs):
                r[...] = o.astype(r.dtype)

        if nk == 1:
            finish(part)
            return

        @pl.when(kk == 0)
        def _():
            acc_ref[...] = part

        @pl.when(kk > 0)
        def _():
            acc_ref[...] += part

        @pl.when(kk == nk - 1)
        def _():
            finish(acc_ref[...])

    if mode == "tn":
        a_spec = pl.BlockSpec((tk, tm), lambda i, j, kk: (kk, i))
    else:
        a_spec = pl.BlockSpec((tm, tk), lambda i, j, kk: (i, kk))
    mn_spec = pl.BlockSpec((tm, tn), lambda i, j, kk: (i, j))
    row_spec = pl.BlockSpec((1, tn), lambda i, j, kk: (0, j))
    if out_shards:
        out_spec = pl.BlockSpec((None, tm, tn), lambda i, j, kk: (j, i, 0))
        out_shape = [jax.ShapeDtypeStruct((N_DEV, m, tn), dt) for dt in out_dtypes]
    else:
        out_spec = mn_spec
        out_shape = [jax.ShapeDtypeStruct((m, n), dt) for dt in out_dtypes]
    outs = pl.pallas_call(
        body, name=name, grid=(m // tm, n // tn, nk),
        in_specs=[a_spec, b_spec] + [row_spec if e.shape[0] == 1 else mn_spec for e in extras],
        out_specs=[out_spec] * n_out,
        out_shape=out_shape,
        scratch_shapes=[pltpu.VMEM((tm, tn), F32)] if nk > 1 else [],
        compiler_params=_cparams(("parallel", "parallel", "arbitrary")),
    )(a, b_arr, *extras)
    return outs


def matmul_nt_normbwd(dz, w, x, g, dres, name, tm=512):
    m, k = dz.shape
    d = w.shape[0]
    tm = _fit(m, tm)
    w_arr, w_spec, prod = _b_operand(w, "nt", d, k, lambda i: (0, 0))

    def body(dz_ref, w_ref, x_ref, g_ref, dres_ref, dx_ref, dxb_ref, dg_ref):
        @pl.when(pl.program_id(0) == 0)
        def _():
            dg_ref[...] = jnp.zeros_like(dg_ref)

        dh = prod(_bf(dz_ref[...]), w_ref)
        xf = x_ref[...]
        r = lax.rsqrt(jnp.mean(xf * xf, axis=-1, keepdims=True) + EPS)
        xhat = xf * r
        dg_ref[...] += jnp.sum(dh * xhat, axis=0, keepdims=True)
        dxh = dh * g_ref[...]
        dx = r * (dxh - xhat * jnp.mean(dxh * xhat, axis=-1, keepdims=True))
        out = dres_ref[...] + dx
        dx_ref[...] = out
        dxb_ref[...] = out.astype(BF16)

    row = pl.BlockSpec((tm, d), lambda i: (i, 0))
    vec = pl.BlockSpec((1, d), lambda i: (0, 0))
    return pl.pallas_call(
        body, name=name, grid=(m // tm,),
        in_specs=[pl.BlockSpec((tm, k), lambda i: (i, 0)), w_spec, row, vec, row],
        out_specs=[row, row, vec],
        out_shape=[jax.ShapeDtypeStruct((m, d), F32), jax.ShapeDtypeStruct((m, d), BF16), jax.ShapeDtypeStruct((1, d), F32)],
        compiler_params=_cparams(("arbitrary",)),
    )(dz, w_arr, x, g.reshape(1, d), dres)


def _rope_tables(t):
    pos = lax.broadcasted_iota(jnp.int32, (t, LANES), 0)
    dim = lax.broadcasted_iota(jnp.int32, (t, LANES), 1) % HEAD_DIM

    def table(p, width):
        i = dim % (width // 2)
        ang = p.astype(F32) * (ROPE_THETA ** (-(2 * i).astype(F32) / width))
        return jnp.cos(ang), jnp.where(dim % width < width // 2, -jnp.sin(ang), jnp.sin(ang))

    cos_a, sin_a = table(pos, HEAD_DIM)
    cos_b, sin_b = table(jnp.where(dim < HEAD_DIM // 2, pos // GRID_W, pos % GRID_W), HEAD_DIM // 2)
    return cos_a, sin_a, cos_b, sin_b


def _headnorm(xs, gmat):
    return lax.rsqrt(_dot_f32_by_ones(xs * xs, gmat) * (1.0 / HEAD_DIM) + EPS)


def qkv_post_fwd(proj, tables, qn_g, kn_g, name):
    t = proj.shape[0]
    tm = min(t, 256)
    cos_a, sin_a, cos_b, sin_b = tables
    g2 = lambda g: jnp.concatenate([g, g]).reshape(1, LANES)

    def body(p_ref, ca_ref, sa_ref, cb_ref, sb_ref, qg_ref, kg_ref, qa_ref, ka_ref, va_ref, qb_ref, kb_ref, vb_ref):
        ca, sa, cb, sb = ca_ref[...], sa_ref[...], cb_ref[...], sb_ref[...]
        gmat = _group_matrix(HEAD_DIM)

        def rope_a(xs):
            return xs * ca + _swap_halves(xs, HEAD_DIM) * sa

        def norm_rope_b(xs, g):
            y = xs * _headnorm(xs, gmat) * g
            return y * cb + _swap_halves(y, HEAD_DIM // 2) * sb

        for c in range(A_Q // LANES):
            qa_ref[:, c * LANES:(c + 1) * LANES] = rope_a(p_ref[:, OFF_QA + c * LANES:OFF_QA + (c + 1) * LANES]).astype(BF16)
        ka_ref[...] = rope_a(p_ref[:, OFF_KA:OFF_KA + LANES]).astype(BF16)
        va_ref[...] = p_ref[:, OFF_VA:OFF_VA + LANES].astype(BF16)
        for c in range(B_Q // LANES):
            qb_ref[:, c * LANES:(c + 1) * LANES] = norm_rope_b(
                p_ref[:, OFF_QB + c * LANES:OFF_QB + (c + 1) * LANES], qg_ref[...]).astype(BF16)
        kb_ref[...] = norm_rope_b(p_ref[:, OFF_KB:OFF_KB + LANES], kg_ref[...]).astype(BF16)
        vb_ref[...] = p_ref[:, OFF_VB:OFF_VB + LANES].astype(BF16)

    tab = pl.BlockSpec((tm, LANES), lambda i: (i, 0))
    vec = pl.BlockSpec((1, LANES), lambda i: (0, 0))
    wide = pl.BlockSpec((tm, A_Q), lambda i: (i, 0))
    return pl.pallas_call(
        body, name=name, grid=(t // tm,),
        in_specs=[pl.BlockSpec((tm, ATT_IN), lambda i: (i, 0)), tab, tab, tab, tab, vec, vec],
        out_specs=[wide, tab, tab, wide, tab, tab],
        out_shape=[jax.ShapeDtypeStruct((t, A_Q), BF16), jax.ShapeDtypeStruct((t, LANES), BF16),
                   jax.ShapeDtypeStruct((t, LANES), BF16), jax.ShapeDtypeStruct((t, B_Q), BF16),
                   jax.ShapeDtypeStruct((t, LANES), BF16), jax.ShapeDtypeStruct((t, LANES), BF16)],
        compiler_params=_cparams(("parallel",)),
    )(proj, cos_a, sin_a, cos_b, sin_b, g2(qn_g), g2(kn_g))


def qkv_post_bwd(proj, tables, qn_g, kn_g, dqa, dka, dva, dqb, dkb, dvb, name):
    t = proj.shape[0]
    tm = min(t, 256)
    cos_a, sin_a, cos_b, sin_b = tables
    g2 = lambda g: jnp.concatenate([g, g]).reshape(1, LANES)

    def body(p_ref, ca_ref, sa_ref, cb_ref, sb_ref, qg_ref, kg_ref, dqa_ref, dka_ref, dva_ref, dqb_ref, dkb_ref, dvb_ref,
             dp_ref, dqg_ref, dkg_ref):
        ca, sa, cb, sb = ca_ref[...], sa_ref[...], cb_ref[...], sb_ref[...]
        gmat = _group_matrix(HEAD_DIM)

        @pl.when(pl.program_id(0) == 0)
        def _():
            dqg_ref[...] = jnp.zeros_like(dqg_ref)
            dkg_ref[...] = jnp.zeros_like(dkg_ref)

        def rope_a_bwd(dy):
            return dy * ca + _swap_halves(dy * sa, HEAD_DIM)

        def norm_rope_b_bwd(dout, xs, g):
            dy = dout * cb + _swap_halves(dout * sb, HEAD_DIM // 2)
            r = _headnorm(xs, gmat)
            xhat = xs * r
            dxh = dy * g
            mean = _dot_f32_by_ones(dxh * xhat, gmat) * (1.0 / HEAD_DIM)
            return r * (dxh - xhat * mean), jnp.sum(dy * xhat, axis=0, keepdims=True)

        for c in range(A_Q // LANES):
            sl = slice(c * LANES, (c + 1) * LANES)
            dp_ref[:, OFF_QA + c * LANES:OFF_QA + (c + 1) * LANES] = rope_a_bwd(dqa_ref[:, sl].astype(F32)).astype(BF16)
        dp_ref[:, OFF_KA:OFF_KA + LANES] = rope_a_bwd(dka_ref[0] + dka_ref[1]).astype(BF16)
        dp_ref[:, OFF_VA:OFF_VA + LANES] = (dva_ref[0] + dva_ref[1]).astype(BF16)
        dqg = jnp.zeros((1, LANES), F32)
        for c in range(B_Q // LANES):
            sl = slice(c * LANES, (c + 1) * LANES)
            dx, dg = norm_rope_b_bwd(dqb_ref[:, sl].astype(F32), p_ref[:, OFF_QB + c * LANES:OFF_QB + (c + 1) * LANES], qg_ref[...])
            dp_ref[:, OFF_QB + c * LANES:OFF_QB + (c + 1) * LANES] = dx.astype(BF16)
            dqg = dqg + dg
        dqg_ref[...] += dqg
        dx, dg = norm_rope_b_bwd((dkb_ref[0] + dkb_ref[1]).T, p_ref[:, OFF_KB:OFF_KB + LANES], kg_ref[...])
        dp_ref[:, OFF_KB:OFF_KB + LANES] = dx.astype(BF16)
        dkg_ref[...] += dg
        dp_ref[:, OFF_VB:OFF_VB + LANES] = (dvb_ref[0] + dvb_ref[1]).T.astype(BF16)

        @pl.when(pl.program_id(0) == t // tm - 1)
        def _():
            dqg_ref[...] = dqg_ref[...] + pltpu.roll(dqg_ref[...], HEAD_DIM, 1)
            dkg_ref[...] = dkg_ref[...] + pltpu.roll(dkg_ref[...], HEAD_DIM, 1)

    tab = pl.BlockSpec((tm, LANES), lambda i: (i, 0))
    vec = pl.BlockSpec((1, LANES), lambda i: (0, 0))
    wide = pl.BlockSpec((tm, A_Q), lambda i: (i, 0))
    slab = pl.BlockSpec((2, tm, LANES), lambda i: (0, i, 0))
    per_chunk = dkb.shape[3] // tm
    slab_t = pl.BlockSpec((2, None, LANES, tm), lambda i: (0, i // per_chunk, 0, i % per_chunk))
    full = pl.BlockSpec((tm, ATT_IN), lambda i: (i, 0))
    return pl.pallas_call(
        body, name=name, grid=(t // tm,),
        in_specs=[full, tab, tab, tab, tab, vec, vec, wide, slab, slab, wide, slab_t, slab_t],
        out_specs=[full, vec, vec],
        out_shape=[jax.ShapeDtypeStruct((t, ATT_IN), BF16), jax.ShapeDtypeStruct((1, LANES), F32),
                   jax.ShapeDtypeStruct((1, LANES), F32)],
        compiler_params=_cparams(("arbitrary",)),
    )(proj, cos_a, sin_a, cos_b, sin_b, g2(qn_g), g2(kn_g), dqa, dka, dva, dqb, dkb, dvb)


def _head_to_half(xs, head_half, kv_half):
    low = _lane(xs.shape) < HEAD_DIM
    kept = jnp.where(low if head_half == 0 else jnp.logical_not(low), xs, 0.0)
    return jnp.where(kv_half == head_half, kept, pltpu.roll(kept, HEAD_DIM, 1))


def _halves_to_heads(r0, r1, kv_half):
    low = _lane(r0.shape) < HEAD_DIM
    a = jnp.where(kv_half == 0, r0, pltpu.roll(r0, HEAD_DIM, 1))
    b = jnp.where(kv_half == 1, r1, pltpu.roll(r1, HEAD_DIM, 1))
    return jnp.where(low, a, b)


def attn_delta(o, do, name):
    t, w = o.shape
    tm = min(t, 512)
    n_heads = w // HEAD_DIM

    def body(o_ref, do_ref, d_ref):
        lo, hi = _seg_matrix(0, HEAD_DIM), _seg_matrix(HEAD_DIM, LANES)
        for c in range(w // LANES):
            sl = slice(c * LANES, (c + 1) * LANES)
            s = o_ref[:, sl].astype(F32) * do_ref[:, sl].astype(F32)
            d_ref[2 * c] = _dot_f32_by_ones(s, lo)
            d_ref[2 * c + 1] = _dot_f32_by_ones(s, hi)

    blk = pl.BlockSpec((tm, w), lambda i: (i, 0))
    return pl.pallas_call(
        body, name=name, grid=(t // tm,),
        in_specs=[blk, blk],
        out_specs=pl.BlockSpec((n_heads, tm, LANES), lambda i: (0, i, 0)),
        out_shape=jax.ShapeDtypeStruct((n_heads, t, LANES), F32),
        compiler_params=_cparams(("parallel",)),
    )(o, do)


BAND = 3 * BLOCK


def _band_offsets(rows_rep):
    qi = lax.broadcasted_iota(jnp.int32, (BLOCK, BAND), 0)
    kj = lax.broadcasted_iota(jnp.int32, (BLOCK, BAND), 1)
    return jnp.concatenate([kj - qi] * rows_rep, axis=0)


def _band(n, t, offsets):
    start = pl.multiple_of(jnp.clip((n - 1) * BLOCK, 0, t - BAND), BLOCK)
    return start, jnp.abs(offsets + (start - n * BLOCK)) <= WINDOW


def window_attn_fwd(q, k, v, sink, name, blocks_per_step=8):
    t = q.shape[0]
    assert t >= BAND
    nq = _fit(t // BLOCK, blocks_per_step)
    tq = nq * BLOCK

    def body(sink_ref, q_ref, k_ref, v_ref, o_ref, lse_ref):
        j, n0 = pl.program_id(0), pl.program_id(1)
        kvh = j // 2
        row = lax.broadcasted_iota(jnp.int32, (2 * BLOCK, 1), 0)
        sk = jnp.where(row < BLOCK, sink_ref[2 * j], sink_ref[2 * j + 1]) * LOG2E
        offsets = _band_offsets(2)
        bands, scores = [], []
        for u in range(nq):
            start, ok = _band(n0 * nq + u, t, offsets)
            qf = q_ref[u * BLOCK:(u + 1) * BLOCK, :].astype(F32) * (SCALE * LOG2E)
            qs = jnp.concatenate([_head_to_half(qf, 0, kvh), _head_to_half(qf, 1, kvh)], axis=0).astype(BF16)
            bands.append(pl.ds(start, BAND))
            scores.append(jnp.where(ok, _dot_nt(qs, k_ref[bands[u], :]), NEG))
        soft = []
        for s in scores:
            m = jnp.maximum(jnp.max(s, axis=-1, keepdims=True), sk)
            p = jnp.exp2(s - m)
            soft.append((p.astype(BF16), jnp.sum(p, axis=-1, keepdims=True) + jnp.exp2(sk - m), m))
        for u, (p, denom, m) in enumerate(soft):
            rows = slice(u * BLOCK, (u + 1) * BLOCK)
            o = _dot_nn(p, v_ref[bands[u], :]) / denom
            o_ref[rows, :] = _halves_to_heads(o[:BLOCK], o[BLOCK:], kvh).astype(BF16)
            lse = jnp.broadcast_to(m + jnp.log2(denom), (2 * BLOCK, LANES))
            lse_ref[0, rows, :] = lse[:BLOCK]
            lse_ref[1, rows, :] = lse[BLOCK:]

    qspec = pl.BlockSpec((tq, LANES), lambda j, n: (n, j))
    whole = pl.BlockSpec((t, LANES), lambda j, n: (0, 0))
    return pl.pallas_call(
        body, name=name, grid=(A_HEADS // 2, t // tq),
        in_specs=[pl.BlockSpec(memory_space=pltpu.SMEM), qspec, whole, whole],
        out_specs=[qspec, pl.BlockSpec((2, tq, LANES), lambda j, n: (j, n, 0))],
        out_shape=[jax.ShapeDtypeStruct((t, A_Q + B_Q), BF16), jax.ShapeDtypeStruct((A_HEADS, t, LANES), F32)],
        compiler_params=_cparams(("parallel", "parallel")),
    )(sink, q, k, v)


def window_attn_bwd(q, k, v, sink, do, lse, delta, name, blocks_per_step=4):
    t = q.shape[0]
    assert t >= BAND
    nq = _fit(t // BLOCK, blocks_per_step)
    tq = nq * BLOCK
    grp = A_HEADS // A_KV_HEADS
    gw = grp * HEAD_DIM

    def body(sink_ref, q_ref, do_ref, k_ref, v_ref, lse_ref, dl_ref, dq_ref, dk_ref, dv_ref, ds_ref):
        kvh, n0 = pl.program_id(0), pl.program_id(1)

        @pl.when(n0 == 0)
        def _():
            dk_ref[...] = jnp.zeros_like(dk_ref)
            dv_ref[...] = jnp.zeros_like(dv_ref)
            ds_ref[...] = jnp.zeros_like(ds_ref)

        rid = lax.broadcasted_iota(jnp.int32, (8, LANES), 0)
        upd = jnp.zeros((8, LANES), F32)
        offsets = _band_offsets(grp)
        for u in range(nq):
            rows = slice(u * BLOCK, (u + 1) * BLOCK)
            start, ok = _band(n0 * nq + u, t, offsets)
            band = pl.ds(start, BAND)
            qparts, doparts = [], []
            for hh in range(grp):
                sl = slice((hh // 2) * LANES, (hh // 2 + 1) * LANES)
                qparts.append(_head_to_half(q_ref[rows, sl].astype(F32) * (SCALE * LOG2E), hh % 2, kvh))
                doparts.append(_head_to_half(do_ref[rows, sl].astype(F32), hh % 2, kvh))
            qs = jnp.concatenate(qparts, axis=0).astype(BF16)
            dos = jnp.concatenate(doparts, axis=0).astype(BF16)
            lse_b = jnp.concatenate([lse_ref[hh, rows, :] for hh in range(grp)], axis=0)
            dl_b = jnp.concatenate([dl_ref[hh, rows, :] for hh in range(grp)], axis=0)
            kband, vband = k_ref[band, :], v_ref[band, :]
            s = jnp.where(ok, _dot_nt(qs, kband), NEG)
            p = jnp.exp2(s - lse_b[:, :1])
            dp = _dot_nt(dos, vband)
            dsc = (p * (dp - dl_b[:, :1])).astype(BF16)
            dv_ref[0, band, :] += _dot_tn(p.astype(BF16), dos)
            dk_ref[0, band, :] += _dot_tn(dsc, qs) * LN2
            dq = _dot_nn(dsc, kband) * SCALE
            for c in range(grp // 2):
                dq_ref[rows, c * LANES:(c + 1) * LANES] = _halves_to_heads(
                    dq[2 * c * BLOCK:(2 * c + 1) * BLOCK], dq[(2 * c + 1) * BLOCK:(2 * c + 2) * BLOCK], kvh).astype(dq_ref.dtype)
            for hh in range(grp):
                rs = slice(hh * BLOCK, (hh + 1) * BLOCK)
                tot = jnp.sum(jnp.exp2(sink_ref[kvh * grp + hh] * LOG2E - lse_b[rs]) * dl_b[rs], axis=0, keepdims=True)
                upd = upd + jnp.where(rid == hh, -tot, 0.0)
        ds_ref[0] += upd

    qspec = pl.BlockSpec((tq, gw), lambda kvh, n: (n, kvh))
    whole = pl.BlockSpec((t, LANES), lambda kvh, n: (0, 0))
    stat = pl.BlockSpec((grp, tq, LANES), lambda kvh, n: (kvh, n, 0))
    slab = pl.BlockSpec((1, t, LANES), lambda kvh, n: (kvh, 0, 0))
    return pl.pallas_call(
        body, name=name, grid=(A_KV_HEADS, t // tq),
        in_specs=[pl.BlockSpec(memory_space=pltpu.SMEM), qspec, qspec, whole, whole, stat, stat],
        out_specs=[qspec, slab, slab, pl.BlockSpec((1, 8, LANES), lambda kvh, n: (kvh, 0, 0))],
        out_shape=[jax.ShapeDtypeStruct((t, A_Q), BF16), jax.ShapeDtypeStruct((A_KV_HEADS, t, LANES), F32),
                   jax.ShapeDtypeStruct((A_KV_HEADS, t, LANES), F32), jax.ShapeDtypeStruct((A_KV_HEADS, 8, LANES), F32)],
        compiler_params=_cparams(("arbitrary", "arbitrary")),
    )(sink, q, do, k, v, lse, delta)


def flash_attn_fwd(q, k, v, cat, name, exchange=None, tq=1024, tk=512, ahead=2):
    t = q.shape[0]
    tq, tk = _fit(t, tq), _fit(t, tk)
    nk = t // tk

    def body(q_ref, k_ref, v_ref, cat_ref, o_ref, lse_ref):
        del cat_ref
        kvh = pl.program_id(0) // 2
        qf = q_ref[...].astype(F32) * (SCALE * LOG2E)
        qs = jnp.concatenate([_head_to_half(qf, 0, kvh), _head_to_half(qf, 1, kvh)], axis=0).astype(BF16)
        mine = (_lane((tk, LANES)) < HEAD_DIM) == (kvh == 0)

        def scores(c):
            return _dot_nt(qs, k_ref[c * tk:(c + 1) * tk, :])

        s = [scores(c) for c in range(min(ahead, nk))]
        m = jnp.full((2 * tq, 1), NEG, F32)
        acc = jnp.zeros((2 * tq, LANES), F32)
        for c in range(nk):
            if c + ahead < nk:
                s.append(scores(c + ahead))
            vb = jnp.where(mine, v_ref[c * tk:(c + 1) * tk, :], jnp.ones((), BF16))
            m_new = jnp.maximum(m, jnp.max(s[c], axis=-1, keepdims=True))
            p = jnp.exp2(s[c] - m_new).astype(BF16)
            acc = jnp.exp2(m - m_new) * acc + _dot_nn(p, vb)
            m = m_new
        other = pltpu.roll(acc, HEAD_DIM, 1)
        o = acc / other
        o_ref[...] = _halves_to_heads(o[:tq], o[tq:], kvh).astype(BF16)
        in_mine = (_lane(acc.shape) < HEAD_DIM) == (kvh == 0)
        lse = jnp.broadcast_to(m, acc.shape) + jnp.log2(jnp.where(in_mine, other, acc))
        lse_ref[0] = lse[:tq]
        lse_ref[1] = lse[tq:]

    qspec = pl.BlockSpec((tq, LANES), lambda j, i: (i, j))
    whole = pl.BlockSpec((t, LANES), lambda j, i: (0, 0))
    nj, ni = B_HEADS // 2, t // tq
    steps = lambda: ((pl.program_id(0) == 0) & (pl.program_id(1) == 0), (pl.program_id(0) == nj - 1) & (pl.program_id(1) == ni - 1))
    body, x_in, x_out, x_shapes, x_scratch = carried(body, exchange, 4, 2, steps)
    return pl.pallas_call(
        body, name=name, grid=(nj, ni),
        in_specs=[qspec, whole, whole, _ANY] + x_in,
        out_specs=[pl.BlockSpec((tq, LANES), lambda j, i: (i, A_Q // LANES + j)),
                   pl.BlockSpec((2, tq, LANES), lambda j, i: (j, i, 0))] + x_out,
        out_shape=[jax.ShapeDtypeStruct(cat.shape, BF16), jax.ShapeDtypeStruct((B_HEADS, t, LANES), F32)] + x_shapes,
        scratch_shapes=x_scratch,
        input_output_aliases={3: 0},
        compiler_params=_cparams(("arbitrary", "arbitrary")),
    )(q, k, v, cat, *(exchange.arrays if exchange else ()))


def flash_attn_bwd(q, k, v, do, lse, delta, name, exchange=None, tq=512, tk=512, together=4):
    t = q.shape[0]
    tq, tk = _fit(t, tq), _fit(t, tk)
    nk = t // tk
    together = _fit(nk, together)
    grp = B_HEADS // B_KV_HEADS
    gw = grp * HEAD_DIM

    def body(q_ref, do_ref, k_ref, v_ref, lse_ref, dl_ref, dq_ref, dk_ref, dv_ref, dq_s):
        kvh, i = pl.program_id(0), pl.program_id(1)

        @pl.when(i == 0)
        def _():
            dk_ref[...] = jnp.zeros_like(dk_ref)
            dv_ref[...] = jnp.zeros_like(dv_ref)

        qparts, doparts = [], []
        for hh in range(grp):
            sl = slice((hh // 2) * LANES, (hh // 2 + 1) * LANES)
            qparts.append(_head_to_half(q_ref[:, sl].astype(F32) * (SCALE * LOG2E), hh % 2, kvh))
            doparts.append(_head_to_half(do_ref[:, sl].astype(F32), hh % 2, kvh))
        qf, dof = jnp.concatenate(qparts, axis=0), jnp.concatenate(doparts, axis=0)
        qs, dos = qf.astype(BF16), dof.astype(BF16)
        qs_t, dos_t = qf.T.astype(BF16), dof.T.astype(BF16)
        lse = jnp.tile(jnp.concatenate([lse_ref[hh] for hh in range(grp)], axis=0), (1, tk // LANES))
        dl = jnp.tile(jnp.concatenate([dl_ref[hh] for hh in range(grp)], axis=0), (1, tk // LANES))
        dq_s[...] = jnp.zeros_like(dq_s)

        def chunks(c0, carry):
            cs = [c0 * together + u for u in range(together)]
            kbs = [k_ref[pl.ds(pl.multiple_of(c * tk, tk), tk), :] for c in cs]
            vbs = [v_ref[pl.ds(pl.multiple_of(c * tk, tk), tk), :] for c in cs]
            ss = [_dot_nt(qs, kb) for kb in kbs]
            dps = [_dot_nt(dos, vb) for vb in vbs]
            for c, kb, s, dp in zip(cs, kbs, ss, dps):
                p = jnp.exp2(s - lse)
                dsc = (p * (dp - dl)).astype(BF16)
                dv_ref[0, c] += _dot_nn(dos_t, p.astype(BF16))
                dk_ref[0, c] += _dot_nn(qs_t, dsc) * LN2
                dq_s[...] += _dot_nn(kb.T, dsc.T)
            return carry

        lax.fori_loop(0, nk // together, chunks, 0)
        dq = dq_s[...].T
        for c in range(grp // 2):
            dq_ref[:, c * LANES:(c + 1) * LANES] = (_halves_to_heads(
                dq[2 * c * tq:(2 * c + 1) * tq], dq[(2 * c + 1) * tq:(2 * c + 2) * tq], kvh) * SCALE).astype(dq_ref.dtype)

    qspec = pl.BlockSpec((tq, gw), lambda kvh, i: (i, kvh))
    dospec = pl.BlockSpec((tq, gw), lambda kvh, i: (i, A_Q // gw + kvh))
    whole = pl.BlockSpec((t, LANES), lambda kvh, i: (0, 0))
    stat = pl.BlockSpec((grp, tq, LANES), lambda kvh, i: (kvh, i, 0))
    dlstat = pl.BlockSpec((grp, tq, LANES), lambda kvh, i: (A_HEADS // grp + kvh, i, 0))
    slab = pl.BlockSpec((1, nk, LANES, tk), lambda kvh, i: (kvh, 0, 0, 0))
    ni = t // tq
    steps = lambda: ((pl.program_id(0) == 0) & (pl.program_id(1) == 0),
                     (pl.program_id(0) == B_KV_HEADS - 1) & (pl.program_id(1) == ni - 1))
    body, x_in, x_out, x_shapes, x_scratch = carried(body, exchange, 6, 3, steps)
    return pl.pallas_call(
        body, name=name, grid=(B_KV_HEADS, ni),
        in_specs=[qspec, dospec, whole, whole, stat, dlstat] + x_in,
        out_specs=[qspec, slab, slab] + x_out,
        out_shape=[jax.ShapeDtypeStruct((t, B_Q), BF16), jax.ShapeDtypeStruct((B_KV_HEADS, nk, LANES, tk), F32),
                   jax.ShapeDtypeStruct((B_KV_HEADS, nk, LANES, tk), F32)] + x_shapes,
        scratch_shapes=[pltpu.VMEM((LANES, grp * tq), F32)] + x_scratch,
        compiler_params=_cparams(("arbitrary", "arbitrary")),
    )(q, do, k, v, lse, delta, *(exchange.arrays if exchange else ()))


_GELU_C = math.sqrt(2.0 / math.pi)
_GELU_A = 0.044715


def _gelu(x):
    return 0.5 * x * (1.0 + jnp.tanh(_GELU_C * (x + _GELU_A * x * x * x)))


def _gelu_grad(x):
    th = jnp.tanh(_GELU_C * (x + _GELU_A * x * x * x))
    return 0.5 * (1.0 + th) + 0.5 * x * (1.0 - th * th) * _GELU_C * (1.0 + 3.0 * _GELU_A * x * x)


def _layernorm_stats(vf):
    mu = jnp.mean(vf, axis=-1, keepdims=True)
    vc = vf - mu
    r = lax.rsqrt(jnp.mean(vc * vc, axis=-1, keepdims=True) + EPS)
    return vc * r, r


def sgu_mix_fwd(z, ln_g, ln_b, w_s, b_rows, name):
    t, w2 = z.shape
    w = w2 // 2
    dg = w // SGU_GROUPS

    def body(u_ref, v_ref, g_ref, b_ref, ws_ref, bb_ref, y_ref):
        vhat, _ = _layernorm_stats(v_ref[...].astype(F32))
        vn = (vhat * g_ref[...] + b_ref[...]).astype(BF16)
        for g in range(SGU_GROUPS):
            sl = slice(g * dg, (g + 1) * dg)
            mixed = _dot_nn(ws_ref[g], vn[:, sl]) + bb_ref[g]
            y_ref[:, sl] = (u_ref[:, sl].astype(F32) * mixed).astype(BF16)

    vec = pl.BlockSpec((1, w), lambda n: (0, 0))
    whole = pl.BlockSpec((SGU_GROUPS, SGU_CHUNK, SGU_CHUNK), lambda n: (0, 0, 0))
    return pl.pallas_call(
        body, name=name, grid=(t // SGU_CHUNK,),
        in_specs=[pl.BlockSpec((SGU_CHUNK, w), lambda n: (n, 0)), pl.BlockSpec((SGU_CHUNK, w), lambda n: (n, 1)),
                  vec, vec, whole, whole],
        out_specs=pl.BlockSpec((SGU_CHUNK, w), lambda n: (n, 0)),
        out_shape=jax.ShapeDtypeStruct((t, w), BF16),
        compiler_params=_cparams(("parallel",)),
    )(z, z, ln_g.reshape(1, w), ln_b.reshape(1, w), w_s, b_rows)


def sgu_mix_bwd(z, apre, dy, ln_g, ln_b, w_s, b_rows, name):
    t, w2 = z.shape
    w = w2 // 2
    dg = w // SGU_GROUPS

    def body(u_ref, v_ref, au_ref, av_ref, dy_ref, g_ref, b_ref, ws_ref, bb_ref, da_ref, dlg_ref, dlb_ref, dws_ref, dbs_ref):
        @pl.when(pl.program_id(0) == 0)
        def _():
            dlg_ref[...] = jnp.zeros_like(dlg_ref)
            dlb_ref[...] = jnp.zeros_like(dlb_ref)
            dws_ref[...] = jnp.zeros_like(dws_ref)
            dbs_ref[...] = jnp.zeros_like(dbs_ref)

        vhat, r = _layernorm_stats(v_ref[...].astype(F32))
        gam = g_ref[...]
        vn = (vhat * gam + b_ref[...]).astype(BF16)
        ones8 = jnp.ones((8, dg), BF16)
        rid = lax.broadcasted_iota(jnp.int32, (8, SGU_CHUNK), 0)
        dbs = jnp.zeros((8, SGU_CHUNK), F32)
        dvn_parts = []
        for g in range(SGU_GROUPS):
            sl = slice(g * dg, (g + 1) * dg)
            dyg = dy_ref[:, sl].astype(F32)
            mixed = _dot_nn(ws_ref[g], vn[:, sl]) + bb_ref[g]
            da_ref[:, sl] = (dyg * mixed * _gelu_grad(au_ref[:, sl].astype(F32))).astype(BF16)
            dmix = dyg * u_ref[:, sl].astype(F32)
            dm_hi = dmix.astype(BF16)
            dm_lo = (dmix - dm_hi.astype(F32)).astype(BF16)
            dws_ref[g] += _dot_nt(dm_hi, vn[:, sl])
            dbs = dbs + jnp.where(rid == g, _dot_nt(ones8, dm_hi) + _dot_nt(ones8, dm_lo), 0.0)
            dvn_parts.append(_dot_tn(ws_ref[g], dm_hi))
        dbs_ref[...] += dbs
        dvn = jnp.concatenate(dvn_parts, axis=1)
        dlg_ref[...] += jnp.sum(dvn * vhat, axis=0, keepdims=True)
        dlb_ref[...] += jnp.sum(dvn, axis=0, keepdims=True)
        dvh = dvn * gam
        dv = r * (dvh - jnp.mean(dvh, axis=-1, keepdims=True) - vhat * jnp.mean(dvh * vhat, axis=-1, keepdims=True))
        da_ref[:, w:] = (dv * _gelu_grad(av_ref[...].astype(F32))).astype(BF16)

    vec = pl.BlockSpec((1, w), lambda n: (0, 0))
    whole = pl.BlockSpec((SGU_GROUPS, SGU_CHUNK, SGU_CHUNK), lambda n: (0, 0, 0))
    left = pl.BlockSpec((SGU_CHUNK, w), lambda n: (n, 0))
    right = pl.BlockSpec((SGU_CHUNK, w), lambda n: (n, 1))
    return pl.pallas_call(
        body, name=name, grid=(t // SGU_CHUNK,),
        in_specs=[left, right, left, right, left, vec, vec, whole, whole],
        out_specs=[pl.BlockSpec((SGU_CHUNK, w2), lambda n: (n, 0)), vec, vec, whole,
                   pl.BlockSpec((SGU_GROUPS, SGU_CHUNK), lambda n: (0, 0))],
        out_shape=[jax.ShapeDtypeStruct((t, w2), BF16), jax.ShapeDtypeStruct((1, w), F32), jax.ShapeDtypeStruct((1, w), F32),
                   jax.ShapeDtypeStruct((SGU_GROUPS, SGU_CHUNK, SGU_CHUNK), F32),
                   jax.ShapeDtypeStruct((SGU_GROUPS, SGU_CHUNK), F32)],
        compiler_params=_cparams(("arbitrary",)),
    )(z, z, apre, apre, dy, ln_g.reshape(1, w), ln_b.reshape(1, w), w_s, b_rows)


def loss_head(h, g, target, name):
    t, d = h.shape
    tm = min(t, 512)

    def body(h_ref, g_ref, t_ref, loss_ref, dh_ref, dhb_ref, dg_ref):
        @pl.when(pl.program_id(0) == 0)
        def _():
            loss_ref[...] = jnp.zeros_like(loss_ref)
            dg_ref[...] = jnp.zeros_like(dg_ref)

        xf = h_ref[...]
        r = lax.rsqrt(jnp.mean(xf * xf, axis=-1, keepdims=True) + EPS)
        xhat = xf * r
        err = xhat * g_ref[...] - t_ref[...]
        per_tok = jnp.mean(err * err, axis=-1, keepdims=True)
        loss_ref[...] += 0.5 * jnp.sum(per_tok, axis=0, keepdims=True)
        dy = err * (1.0 / d)
        dg_ref[...] += jnp.sum(dy * xhat, axis=0, keepdims=True)
        dxh = dy * g_ref[...]
        dh = r * (dxh - xhat * jnp.mean(dxh * xhat, axis=-1, keepdims=True))
        dh_ref[...] = dh
        dhb_ref[...] = dh.astype(BF16)

    row = pl.BlockSpec((tm, d), lambda i: (i, 0))
    vec = pl.BlockSpec((1, d), lambda i: (0, 0))
    return pl.pallas_call(
        body, name=name, grid=(t // tm,),
        in_specs=[row, vec, row],
        out_specs=[pl.BlockSpec((1, LANES), lambda i: (0, 0)), row, row, vec],
        out_shape=[jax.ShapeDtypeStruct((1, LANES), F32), jax.ShapeDtypeStruct((t, d), F32), jax.ShapeDtypeStruct((t, d), BF16),
                   jax.ShapeDtypeStruct((1, d), F32)],
        compiler_params=_cparams(("arbitrary",)),
    )(h, g.reshape(1, d), target)


ADAMW_BLOCK_BYTES = 1 << 20


def adamw(parts, w, m, v, name):
    n_layers, r, c = w.shape
    row_bytes = n_layers * c * 4
    if r * row_bytes <= 2 * ADAMW_BLOCK_BYTES:
        tr = r
    else:
        tr = _fit(r, 1 << int(math.log2(max(8, ADAMW_BLOCK_BYTES // row_bytes))))
    bc1 = 1.0 - ADAM_B1 ** ADAM_STEP
    bc2 = 1.0 - ADAM_B2 ** ADAM_STEP

    def body(*refs):
        p_refs = refs[:n_layers]
        w_ref, m_ref, v_ref, g_ref, d_ref, nm_ref, nv_ref = refs[n_layers:]
        for l in range(n_layers):
            g = p_refs[l][0].astype(F32)
            for j in range(1, N_DEV):
                g = g + p_refs[l][j].astype(F32)
            nm = ADAM_B1 * m_ref[l] + (1.0 - ADAM_B1) * g
            nv = ADAM_B2 * v_ref[l] + (1.0 - ADAM_B2) * (g * g)
            g_ref[l] = g
            nm_ref[l] = nm
            nv_ref[l] = nv
            d_ref[l] = -ADAM_LR * ((nm / bc1) / (jnp.sqrt(nv / bc2) + ADAM_EPS) + ADAM_WD * w_ref[l])

    blk = pl.BlockSpec((n_layers, tr, c), lambda i: (0, i, 0))
    return pl.pallas_call(
        body, name=name, grid=(r // tr,),
        in_specs=[pl.BlockSpec((N_DEV, tr, c), lambda i: (0, i, 0))] * n_layers + [blk, blk, blk],
        out_specs=[blk] * 4,
        out_shape=[jax.ShapeDtypeStruct((n_layers, r, c), F32)] * 4,
        compiler_params=_cparams(("parallel",)),
    )(*parts, w, m, v)


_ANY = pl.BlockSpec(memory_space=pl.ANY)


def _mesh_pos():
    return lax.axis_index("x"), lax.axis_index("y"), lax.axis_index("c")


class Exchange:
    def __init__(self, gathers=(), scatters=()):
        self.items = [("gather", a) for a in gathers] + [("scatter", a) for a in scatters]
        self.arrays = [a for _, a in self.items]
        self.n = len(self.items)

    def out_shapes(self):
        return [jax.ShapeDtypeStruct(((N_DEV,) + a.shape) if kind == "gather" else a.shape, a.dtype) for kind, a in self.items]

    def scratch(self):
        return [pltpu.SemaphoreType.DMA((7 * self.n,)), pltpu.SemaphoreType.DMA((7 * self.n,)), pltpu.SemaphoreType.DMA((self.n,))]

    def _copies(self, in_refs, out_refs, send_sems, recv_sems, local_sems):
        x, y, c = _mesh_pos()
        me = 4 * x + 2 * y + c
        local, sends, arrivals = [], [], []
        for t, (kind, _) in enumerate(self.items):
            src_of = (lambda slot, r=in_refs[t]: r) if kind == "gather" else (lambda slot, r=in_refs[t]: r.at[slot])
            local.append(pltpu.make_async_copy(src_of(me), out_refs[t].at[me], local_sems.at[t]))
            for k in range(1, N_DEV):
                px = 1 - x if k & 4 else x
                py = 1 - y if k & 2 else y
                pc = 1 - c if k & 1 else c
                pid = 4 * px + 2 * py + pc
                kw = dict(send_sem=send_sems.at[7 * t + k - 1], recv_sem=recv_sems.at[7 * t + k - 1],
                          device_id=(px, py, pc), device_id_type=pl.DeviceIdType.MESH)
                sends.append(pltpu.make_async_remote_copy(src_ref=src_of(pid), dst_ref=out_refs[t].at[me], **kw))
                arrivals.append(pltpu.make_async_remote_copy(src_ref=src_of(pid), dst_ref=out_refs[t].at[pid], **kw))
        return local, sends, arrivals

    def start(self, *refs):
        local, sends, _ = self._copies(*refs)
        for cp in local + sends:
            cp.start()

    def wait(self, *refs):
        local, sends, arrivals = self._copies(*refs)
        for cp in arrivals:
            cp.wait_recv()
        for cp in sends:
            cp.wait_send()
        for cp in local:
            cp.wait()


def carried(body, exchange, n_in, n_out, first_last):
    if exchange is None:
        return body, [], [], [], []
    nx = exchange.n

    def wrapped(*refs):
        ins, xin = refs[:n_in], refs[n_in:n_in + nx]
        outs, xout = refs[n_in + nx:n_in + nx + n_out], refs[n_in + nx + n_out:n_in + 2 * nx + n_out]
        scratch, sems = refs[n_in + 2 * nx + n_out:-3], refs[-3:]
        first, last = first_last()

        @pl.when(first)
        def _():
            exchange.start(xin, xout, *sems)

        body(*ins, *outs, *scratch)

        @pl.when(last)
        def _():
            exchange.wait(xin, xout, *sems)

    return wrapped, [_ANY] * nx, [_ANY] * nx, exchange.out_shapes(), exchange.scratch()


def exchange_only(exchange, name):
    def body(*refs):
        xin, xout, sems = refs[:exchange.n], refs[exchange.n:2 * exchange.n], refs[-3:]
        exchange.start(xin, xout, *sems)
        exchange.wait(xin, xout, *sems)

    return pl.pallas_call(
        body, name=name, in_specs=[_ANY] * exchange.n, out_specs=[_ANY] * exchange.n,
        out_shape=exchange.out_shapes(), scratch_shapes=exchange.scratch(),
    )(*exchange.arrays)


def _residual_out(a, w_out, x, next_g, name, **tiles):
    if next_g is None:
        (y,) = matmul(a, w_out, "nn", name, [F32], epilogue=lambda acc, r: (r + acc,), extras=(x,), **tiles)
        return y, None

    def add_and_norm(acc, r, g):
        y = r + acc
        return y, y * lax.rsqrt(jnp.mean(y * y, axis=-1, keepdims=True) + EPS) * g

    assert w_out.shape[1] <= tiles.get("tn", 1024)
    return matmul(a, w_out, "nn", name, [F32, BF16], epilogue=add_and_norm, extras=(x, next_g.reshape(1, -1)), **tiles)


def attention_fwd(x, h, w_in, sink, qn_g, kn_g, w_out, tables, next_g, tag, exchange=None):
    (proj,) = matmul(h, w_in, "nn", f"{tag}_proj", [F32], tn=ATT_IN)
    qa, ka, va, qb, kb, vb = qkv_post_fwd(proj, tables, qn_g, kn_g, f"{tag}_qkv")
    cat, lse_a = window_attn_fwd(qa, ka, va, sink, f"{tag}_win")
    cat, lse_b, *arrived = flash_attn_fwd(qb, kb, vb, cat, f"{tag}_flash", exchange)
    if callable(w_out):
        w_out = w_out(arrived)
    y, h_next = _residual_out(cat, w_out, x, next_g, f"{tag}_out")
    saved = (x, h, proj, qa, ka, va, qb, kb, vb, cat, lse_a, lse_b)
    return y, h_next, saved, arrived


def attention_bwd(dy, dyb, saved, norm_g, w_in, sink, qn_g, kn_g, w_out, tables, tag, exchange_with=None):
    x, h, proj, qa, ka, va, qb, kb, vb, cat, lse_a, lse_b = saved
    (dcat,) = matmul(dyb, w_out, "nt", f"{tag}_dcat", [BF16])
    (dw_out,) = matmul(cat, dyb, "tn", f"{tag}_dwout", [BF16], tk=4096)
    delta = attn_delta(cat, dcat, f"{tag}_delta")
    dqa, dka, dva, dsink = window_attn_bwd(qa, ka, va, sink, dcat, lse_a, delta, f"{tag}_dwin")
    exchange = exchange_with(dw_out) if exchange_with else None
    dqb, dkb, dvb, *arrived = flash_attn_bwd(qb, kb, vb, dcat, lse_b, delta, f"{tag}_dflash", exchange)
    dproj, dqg, dkg = qkv_post_bwd(proj, tables, qn_g, kn_g, dqa, dka, dva, dqb, dkb, dvb, f"{tag}_dqkv")
    (dw_in,) = matmul(h, dproj, "tn", f"{tag}_dwin_w", [BF16], tn=ATT_IN // 2, tk=2048)
    dx, dxb, dg = matmul_nt_normbwd(dproj, w_in, x, norm_g, dy, f"{tag}_dx")
    grp = A_HEADS // A_KV_HEADS
    small = dict(norm=dg[0], sink=dsink[:, :grp, 0].reshape(A_HEADS), qnorm=dqg[0, :HEAD_DIM], knorm=dkg[0, :HEAD_DIM])
    return dx, dxb, dw_in, dw_out, small, arrived


def sgu_fwd(x, h, w_in, ln_g, ln_b, w_s, b_rows, w_out, next_g, tag):
    apre, z = matmul(h, w_in, "nn", f"{tag}_in", [BF16, BF16], epilogue=lambda acc: (acc, _gelu(acc)))
    y = sgu_mix_fwd(z, ln_g, ln_b, w_s, b_rows, f"{tag}_mix")
    out, h_next = _residual_out(y, w_out, x, next_g, f"{tag}_out")
    return out, h_next, (x, h, apre, z, y)


def sgu_bwd(dout, doutb, saved, norm_g, w_in, ln_g, ln_b, w_s, b_rows, w_out, tag):
    x, h, apre, z, y = saved
    (dy,) = matmul(doutb, w_out, "nt", f"{tag}_dy", [BF16])
    (dw_out,) = matmul(y, doutb, "tn", f"{tag}_dwout", [BF16], tk=4096)
    dapre, dlg, dlb, dws, dbs = sgu_mix_bwd(z, apre, dy, ln_g, ln_b, w_s, b_rows, f"{tag}_dmix")
    (dw_in,) = matmul(h, dapre, "tn", f"{tag}_dwin", [BF16], out_shards=True, tk=4096)
    dx, dxb, dg = matmul_nt_normbwd(dapre, w_in, x, norm_g, dout, f"{tag}_dx")
    small = dict(norm=dg[0], ln_g=dlg[0], ln_b=dlb[0], w_s=dws, b_s=dbs)
    return dx, dxb, dw_in, dw_out, small


def _square(r):
    return r * r


def mlp_fwd(x, h, w1, w2, next_g, tag):
    (r,) = matmul(h, w1, "nn", f"{tag}_up", [BF16], epilogue=lambda acc: (jnp.maximum(acc, 0.0),), tm=2048)
    y, h_next = _residual_out(r, w2, x, next_g, f"{tag}_down", a_fn=_square, tm=512, tk=4096)
    return y, h_next, (x, h, r)


def mlp_bwd(dy, dyb, saved, norm_g, w1, w2, tag):
    x, h, r = saved
    (da,) = matmul(dyb, w2, "nt", f"{tag}_da", [BF16], epilogue=lambda acc, rr: (acc * (2.0 * rr.astype(F32)),), extras=(r,),
                   tm=2048)
    (dw2,) = matmul(r, dyb, "tn", f"{tag}_dw2", [BF16], a_fn=_square, tk=4096)
    (dw1,) = matmul(h, da, "tn", f"{tag}_dw1", [BF16], out_shards=True, tk=4096)
    dx, dxb, dg = matmul_nt_normbwd(da, w1, x, norm_g, dy, f"{tag}_dx")
    return dx, dxb, dw1, dw2, dg[0]


ORDER = ("att_norm", "att_w_in", "att_sink", "att_qnorm", "att_knorm", "att_w_out", "sgu_norm", "sgu_w_in", "sgu_ln_g",
         "sgu_ln_b", "sgu_w_s", "sgu_b_s", "sgu_w_out", "mlp_norm", "mlp_w1", "mlp_w2", "final_norm")
SHARDED = ("att_w_in", "att_w_out", "sgu_w_in", "sgu_w_out", "mlp_w1", "mlp_w2")
SGU_VECS = ("sgu_norm", "sgu_ln_g", "sgu_ln_b")
SMALL_EARLY = ("sgu_w_s", "sgu_b_s", "mlp_norm", "final_norm", "loss")
SMALL_LATE = ("att_norm", "att_sink", "att_qnorm", "att_knorm")
SMALL_ROWS_MULT = 8


def _flat(blocks, names):
    flat = jnp.concatenate([blocks[n].reshape(-1).astype(F32) for n in names])
    per = SMALL_ROWS_MULT * FLAT_COLS
    total = -(-flat.shape[0] // per) * per
    return jnp.pad(flat, (0, total - flat.shape[0])).reshape(1, total // FLAT_COLS, FLAT_COLS)


def _unflat(flat, like, names):
    out, off = {}, 0
    f = flat.reshape(-1)
    for n in names:
        size = like[n].size
        out[n] = f[off:off + size].reshape(like[n].shape)
        off += size
    return out


def kernel(x, att_norm, att_w_in, att_sink, att_qnorm, att_knorm, att_w_out, sgu_norm, sgu_w_in, sgu_ln_g, sgu_ln_b, sgu_w_s, sgu_b_s, sgu_w_out, mlp_norm, mlp_w1, mlp_w2, final_norm, loss_target, m_att_norm, m_att_w_in, m_att_sink, m_att_qnorm, m_att_knorm, m_att_w_out, m_sgu_norm, m_sgu_w_in, m_sgu_ln_g, m_sgu_ln_b, m_sgu_w_s, m_sgu_b_s, m_sgu_w_out, m_mlp_norm, m_mlp_w1, m_mlp_w2, m_final_norm, v_att_norm, v_att_w_in, v_att_sink, v_att_qnorm, v_att_knorm, v_att_w_out, v_sgu_norm, v_sgu_w_in, v_sgu_ln_g, v_sgu_ln_b, v_sgu_w_s, v_sgu_b_s, v_sgu_w_out, v_mlp_norm, v_mlp_w1, v_mlp_w2, v_final_norm):
    w = dict(att_norm=att_norm, att_w_in=att_w_in, att_sink=att_sink, att_qnorm=att_qnorm, att_knorm=att_knorm,
             att_w_out=att_w_out, sgu_norm=sgu_norm, sgu_w_in=sgu_w_in, sgu_ln_g=sgu_ln_g, sgu_ln_b=sgu_ln_b, sgu_w_s=sgu_w_s,
             sgu_b_s=sgu_b_s, sgu_w_out=sgu_w_out, mlp_norm=mlp_norm, mlp_w1=mlp_w1, mlp_w2=mlp_w2, final_norm=final_norm)
    m = dict(att_norm=m_att_norm, att_w_in=m_att_w_in, att_sink=m_att_sink, att_qnorm=m_att_qnorm, att_knorm=m_att_knorm,
             att_w_out=m_att_w_out, sgu_norm=m_sgu_norm, sgu_w_in=m_sgu_w_in, sgu_ln_g=m_sgu_ln_g, sgu_ln_b=m_sgu_ln_b,
             sgu_w_s=m_sgu_w_s, sgu_b_s=m_sgu_b_s, sgu_w_out=m_sgu_w_out, mlp_norm=m_mlp_norm, mlp_w1=m_mlp_w1, mlp_w2=m_mlp_w2,
             final_norm=m_final_norm)
    v = dict(att_norm=v_att_norm, att_w_in=v_att_w_in, att_sink=v_att_sink, att_qnorm=v_att_qnorm, att_knorm=v_att_knorm,
             att_w_out=v_att_w_out, sgu_norm=v_sgu_norm, sgu_w_in=v_sgu_w_in, sgu_ln_g=v_sgu_ln_g, sgu_ln_b=v_sgu_ln_b,
             sgu_w_s=v_sgu_w_s, sgu_b_s=v_sgu_b_s, sgu_w_out=v_sgu_w_out, mlp_norm=v_mlp_norm, mlp_w1=v_mlp_w1, mlp_w2=v_mlp_w2,
             final_norm=v_final_norm)
    loss, grad_x, g, d, nm, nv = train_step(x[0], loss_target[0], w, m, v)
    return (loss, grad_x[None], *[g[n] for n in ORDER], *[d[n] for n in ORDER], *[nm[n] for n in ORDER], *[nv[n] for n in ORDER])


def train_step(x, target, w, m, v):
    t, d_model = x.shape
    n_att, n_sgu, depth = w["att_w_in"].shape[0], w["sgu_w_in"].shape[0], w["mlp_w1"].shape[0]
    bf = lambda n: w[n].astype(BF16)

    vec_local = jnp.stack([w[n] for n in SGU_VECS], axis=1)
    att_in = bf("att_w_in")
    g_in0, g_vec = exchange_only(Exchange(gathers=[att_in[:1], vec_local]), "gather_first")
    vecs = g_vec.transpose(1, 2, 0, 3).reshape(n_sgu, len(SGU_VECS), -1)
    rest = Exchange(gathers=[att_in[1:], bf("att_w_out"), bf("sgu_w_in"), bf("sgu_w_out"), bf("mlp_w1"), bf("mlp_w2")])
    w_s_bf = w["sgu_w_s"].astype(BF16)
    b_rows = jnp.broadcast_to(w["sgu_b_s"][:, :, :, None], w["sgu_b_s"].shape + (LANES,))
    tables = _rope_tables(t)
    full_cols = lambda g: g.transpose(1, 2, 0, 3).reshape(g.shape[1], d_model, -1)

    mixer_norm = lambda layer: w["att_norm"][layer // 2] if layer % 2 == 0 else vecs[layer // 2, 0]
    saved = []
    h = rmsnorm_fwd(x, mixer_norm(0), "att0_norm")
    for layer in range(depth):
        i = layer // 2
        if layer % 2 == 0:
            if layer == 0:
                att_w_in = [full_cols(g_in0)[0]]
            x, h, sv, arrived = attention_fwd(x, h, att_w_in[i], w["att_sink"][i], w["att_qnorm"][i], w["att_knorm"][i],
                                              (lambda arrived: Gathered(arrived[1], "row", 0)) if layer == 0 else att_w_out[i],
                                              tables, w["mlp_norm"][layer], f"att{i}", rest if layer == 0 else None)
            if layer == 0:
                g_in1, g_out, g_sgu_in, g_sgu_out, g_w1, g_w2 = arrived
                att_w_in += list(full_cols(g_in1))
                att_w_out = [Gathered(g_out, "row", l) for l in range(n_att)]
        else:
            x, h, sv = sgu_fwd(x, h, Gathered(g_sgu_in, "col", i), vecs[i, 1], vecs[i, 2], w_s_bf[i], b_rows[i],
                               Gathered(g_sgu_out, "row", i), w["mlp_norm"][layer], f"sgu{i}")
        x, h, sm = mlp_fwd(x, h, Gathered(g_w1, "col", layer), Gathered(g_w2, "row", layer),
                           mixer_norm(layer + 1) if layer + 1 < depth else None, f"mlp{layer}")
        saved.append((sv, sm))
    loss_row, dh, dhb, dgf = loss_head(x, w["final_norm"], target, "loss_head")

    queue, recv = [], {}
    gs = dict(att_norm=[None] * n_att, att_sink=[None] * n_att, att_qnorm=[None] * n_att, att_knorm=[None] * n_att,
              sgu_w_s=[None] * n_sgu, sgu_b_s=[None] * n_sgu, mlp_norm=[None] * depth)

    def row_slabs(g):
        return g.reshape(N_DEV, g.shape[0] // N_DEV, g.shape[1])

    def col_slabs(g):
        return g.reshape(g.shape[0], N_DEV, g.shape[1] // N_DEV).transpose(1, 0, 2)

    def take_queue(gathers=()):
        items = list(queue)
        queue.clear()
        keys = [k for k, _ in gathers] + [k for k, _ in items]
        return Exchange(gathers=[a for _, a in gathers], scatters=[a for _, a in items]), keys

    def small_early():
        blocks = dict(sgu_w_s=jnp.stack(gs["sgu_w_s"]), sgu_b_s=jnp.stack(gs["sgu_b_s"]), mlp_norm=jnp.stack(gs["mlp_norm"]),
                      final_norm=dgf[0], loss=loss_row[0, :1])
        return _flat(blocks, SMALL_EARLY)[0]

    for layer in reversed(range(depth)):
        i = layer // 2
        sv, sm = saved[layer]
        dh, dhb, dw1, dw2, gs["mlp_norm"][layer] = mlp_bwd(
            dh, dhb, sm, w["mlp_norm"][layer], Gathered(g_w1, "col", layer), Gathered(g_w2, "row", layer), f"mlp{layer}")
        queue += [(("mlp_w1", layer), dw1), (("mlp_w2", layer), row_slabs(dw2))]
        if layer % 2 == 0:
            keys = []

            def exchange_with(dw_out, i=i, layer=layer, keys=keys):
                if layer == 0:
                    queue.append((("att_w_out", i), row_slabs(dw_out)))
                ex, got = take_queue([("small_early", small_early())] if layer == 0 else ())
                keys += got
                return ex

            dh, dhb, dw_in, dw_out, sm_g, arrived = attention_bwd(
                dh, dhb, sv, w["att_norm"][i], att_w_in[i], w["att_sink"][i], w["att_qnorm"][i], w["att_knorm"][i],
                att_w_out[i], tables, f"att{i}", exchange_with)
            recv.update(zip(keys, arrived))
            queue.append((("att_w_in", i), col_slabs(dw_in)))
            if layer != 0:
                queue.append((("att_w_out", i), row_slabs(dw_out)))
            gs["att_norm"][i], gs["att_sink"][i] = sm_g["norm"], sm_g["sink"]
            gs["att_qnorm"][i], gs["att_knorm"][i] = sm_g["qnorm"], sm_g["knorm"]
        else:
            dh, dhb, dw_in, dw_out, sm_g = sgu_bwd(
                dh, dhb, sv, vecs[i, 0], Gathered(g_sgu_in, "col", i), vecs[i, 1], vecs[i, 2], w_s_bf[i], b_rows[i],
                Gathered(g_sgu_out, "row", i), f"sgu{i}")
            dvec = jnp.stack([sm_g["norm"], sm_g["ln_g"], sm_g["ln_b"]])
            queue += [(("sgu_w_in", i), dw_in), (("sgu_w_out", i), row_slabs(dw_out)), (("sgu_vecs", i), col_slabs(dvec))]
            gs["sgu_w_s"][i], gs["sgu_b_s"][i] = sm_g["w_s"], sm_g["b_s"]
    grad_x = dh
    late = dict(att_norm=jnp.stack(gs["att_norm"]), att_sink=jnp.stack(gs["att_sink"]), att_qnorm=jnp.stack(gs["att_qnorm"]),
                att_knorm=jnp.stack(gs["att_knorm"]))
    last, keys = take_queue([("small_late", _flat(late, SMALL_LATE)[0])])
    recv.update(zip(keys, exchange_only(last, "exchange_last")))

    outs = [{}, {}, {}, {}]
    for n in SHARDED:
        res = adamw([recv[(n, l)] for l in range(w[n].shape[0])], w[n], m[n], v[n], f"adamw_{n}")
        for o, r in zip(outs, res):
            o[n] = r
    stack_vecs = lambda src: jnp.stack([src[n] for n in SGU_VECS], axis=1)
    res = adamw([recv[("sgu_vecs", i)] for i in range(n_sgu)], stack_vecs(w), stack_vecs(m), stack_vecs(v), "adamw_sgu_vecs")
    for o, r in zip(outs, res):
        o.update({n: r[:, k] for k, n in enumerate(SGU_VECS)})
    zero = {"loss": jnp.zeros((1,), F32)}
    for names, key in ((SMALL_EARLY, "small_early"), (SMALL_LATE, "small_late")):
        res = adamw([recv[key]], _flat({**w, **zero}, names), _flat({**m, **zero}, names), _flat({**v, **zero}, names), f"adamw_{key}")
        for o, r in zip(outs, res):
            o.update(_unflat(r, {**w, **zero}, names))
    loss = outs[0]["loss"][0]
    return loss, grad_x, *outs
```

```python
import functools
import math

import jax
import jax.numpy as jnp
from jax import lax
from jax.experimental import pallas as pl
from jax.experimental.pallas import tpu as pltpu

F32 = jnp.float32
BF16 = jnp.bfloat16

HEAD_DIM = 64
A_HEADS = 8
A_KV_HEADS = 2
B_HEADS = 8
B_KV_HEADS = 2
WINDOW = 128
BLOCK = 128
ROPE_THETA = 10000.0
GRID_W = 64
SGU_GROUPS = 8
SGU_CHUNK = 128
EPS = 1e-6
SCALE = HEAD_DIM ** -0.5
NEG = -1e30
LOG2E = math.log2(math.e)
LN2 = math.log(2.0)

A_Q = A_HEADS * HEAD_DIM
A_KV = A_KV_HEADS * HEAD_DIM
B_Q = B_HEADS * HEAD_DIM
B_KV = B_KV_HEADS * HEAD_DIM
OFF_QA, OFF_KA, OFF_VA = 0, A_Q, A_Q + A_KV
OFF_QB = A_Q + 2 * A_KV
OFF_KB = OFF_QB + B_Q
OFF_VB = OFF_KB + B_KV
ATT_IN = OFF_VB + B_KV

ADAM_LR = 0.001
ADAM_B1 = 0.9
ADAM_B2 = 0.999
ADAM_EPS = 1e-08
ADAM_WD = 0.01
ADAM_STEP = 10

N_DEV = 8
LANES = 128
V7X_VMEM_LIMIT = 56 * 1024 * 1024
FLAT_COLS = 1024


def _cparams(sem, vmem=V7X_VMEM_LIMIT):
    return pltpu.CompilerParams(dimension_semantics=sem, vmem_limit_bytes=vmem)


def _dot_nn(a, b):
    return lax.dot_general(a, b, (((1,), (0,)), ((), ())), preferred_element_type=F32)


def _dot_nt(a, b):
    return lax.dot_general(a, b, (((1,), (1,)), ((), ())), preferred_element_type=F32)


def _dot_tn(a, b):
    return lax.dot_general(a, b, (((0,), (0,)), ((), ())), preferred_element_type=F32)


def _bf(x):
    return x if x.dtype == BF16 else x.astype(BF16)


def _lane(shape):
    return lax.broadcasted_iota(jnp.int32, shape, len(shape) - 1)


def _seg_matrix(rows_lo, rows_hi):
    r = lax.broadcasted_iota(jnp.int32, (LANES, LANES), 0)
    return jnp.where((r >= rows_lo) & (r < rows_hi), 1.0, 0.0).astype(BF16)


def _group_matrix(width):
    r = lax.broadcasted_iota(jnp.int32, (LANES, LANES), 0)
    c = lax.broadcasted_iota(jnp.int32, (LANES, LANES), 1)
    return jnp.where((r // width) == (c // width), 1.0, 0.0).astype(BF16)


def _dot_f32_by_ones(s, ones_bf16):
    hi = s.astype(BF16)
    lo = (s - hi.astype(F32)).astype(BF16)
    return _dot_nn(hi, ones_bf16) + _dot_nn(lo, ones_bf16)


def _swap_halves(x, width):
    half = width // 2
    first = (_lane(x.shape) % width) < half
    return jnp.where(first, pltpu.roll(x, LANES - half, 1), pltpu.roll(x, half, 1))


def rmsnorm_fwd(x, g, name):
    t, d = x.shape
    tm = min(t, 512)

    def body(x_ref, g_ref, h_ref):
        xf = x_ref[...]
        r = lax.rsqrt(jnp.mean(xf * xf, axis=-1, keepdims=True) + EPS)
        h_ref[...] = (xf * r * g_ref[...]).astype(BF16)

    return pl.pallas_call(
        body, name=name, grid=(t // tm,),
        in_specs=[pl.BlockSpec((tm, d), lambda i: (i, 0)), pl.BlockSpec((1, d), lambda i: (0, 0))],
        out_specs=pl.BlockSpec((tm, d), lambda i: (i, 0)),
        out_shape=jax.ShapeDtypeStruct((t, d), BF16),
        compiler_params=_cparams(("parallel",)),
    )(x, g.reshape(1, d))


def _fit(n, want):
    t = min(n, want)
    while n % t:
        t //= 2
    return t


class Gathered:
    def __init__(self, arr, kind, layer):
        self.arr, self.kind, self.layer = arr, kind, layer
        _, _, self.rows, self.cols = arr.shape
        self.shape = (N_DEV * self.rows, self.cols) if kind == "row" else (self.rows, N_DEV * self.cols)


def _b_operand(b, mode, tn, tk, idx):
    dot = {"nn": _dot_nn, "nt": _dot_nt, "tn": _dot_tn}[mode]
    if not isinstance(b, Gathered):
        if mode == "nt":
            spec = pl.BlockSpec((tn, tk), lambda *g: idx(*g))
        else:
            spec = pl.BlockSpec((tk, tn), lambda *g: idx(*g)[::-1])
        return b, spec, lambda av, ref: dot(av, _bf(ref[...]))
    lay, rows, cols = b.layer, b.rows, b.cols
    if mode == "nn" and b.kind == "col":
        s = tn // cols
        assert s * cols == tn
        spec = pl.BlockSpec((s, None, tk, cols), lambda *g: (idx(*g)[0], lay, idx(*g)[1], 0))
        return b.arr, spec, lambda av, ref: jnp.concatenate([_dot_nn(av, ref[c]) for c in range(s)], axis=1)
    if mode == "nn" and b.kind == "row":
        s = tk // rows
        assert s * rows == tk
        spec = pl.BlockSpec((s, None, rows, tn), lambda *g: (idx(*g)[1], lay, 0, idx(*g)[0]))
        return b.arr, spec, lambda av, ref: _dot_nn(av, ref[...].reshape(s * rows, tn))
    if mode == "nt" and b.kind == "row":
        s = tn // rows
        assert s * rows == tn
        spec = pl.BlockSpec((s, None, rows, tk), lambda *g: (idx(*g)[0], lay, 0, idx(*g)[1]))
        return b.arr, spec, lambda av, ref: _dot_nt(av, ref[...].reshape(s * rows, tk))
    if mode == "nt" and b.kind == "col":
        s = tk // cols
        assert s * cols == tk
        spec = pl.BlockSpec((s, None, tn, cols), lambda *g: (idx(*g)[1], lay, idx(*g)[0], 0))

        def prod(av, ref):
            tot = _dot_nt(av[:, :cols], ref[0])
            for c in range(1, s):
                tot = tot + _dot_nt(av[:, c * cols:(c + 1) * cols], ref[c])
            return tot

        return b.arr, spec, prod
    raise NotImplementedError((mode, b.kind))


def matmul(a, b, mode, name, out_dtypes, epilogue=None, extras=(), a_fn=None, out_shards=False, tm=1024, tn=1024, tk=1024):
    (m, k) = a.shape[::-1] if mode == "tn" else a.shape
    n = b.shape[0] if mode == "nt" else b.shape[1]
    if out_shards:
        tn = n // N_DEV
    tm, tn, tk = _fit(m, tm), _fit(n, tn), _fit(k, tk)
    nk = k // tk
    n_ex, n_out = len(extras), len(out_dtypes)
    if epilogue is None:
        epilogue = lambda acc: (acc,)
    b_arr, b_spec, prod = _b_operand(b, mode, tn, tk, lambda i, j, kk: (j, kk))

    def body(*refs):
        a_ref, b_ref = refs[0], refs[1]
        ex_refs = refs[2:2 + n_ex]
        out_refs = refs[2 + n_ex:2 + n_ex + n_out]
        acc_ref = refs[2 + n_ex + n_out] if nk > 1 else None
        kk = pl.program_id(2)
        av = _bf(a_ref[...])
        if a_fn is not None:
            av = a_fn(av)
        part = prod(av, b_ref)

        def finish(acc):
            outs = epilogue(acc, *[r[...] for r in ex_refs])
            for r, o in zip(out_refs, outs):
                r[...] = o.astype(r.dtype)

        if nk == 1:
            finish(part)
            return

        @pl.when(kk == 0)
        def _():
            acc_ref[...] = part

        @pl.when(kk > 0)
        def _():
            acc_ref[...] += part

        @pl.when(kk == nk - 1)
        def _():
            finish(acc_ref[...])

    if mode == "tn":
        a_spec = pl.BlockSpec((tk, tm), lambda i, j, kk: (kk, i))
    else:
        a_spec = pl.BlockSpec((tm, tk), lambda i, j, kk: (i, kk))
    mn_spec = pl.BlockSpec((tm, tn), lambda i, j, kk: (i, j))
    row_spec = pl.BlockSpec((1, tn), lambda i, j, kk: (0, j))
    if out_shards:
        out_spec = pl.BlockSpec((None, tm, tn), lambda i, j, kk: (j, i, 0))
        out_shape = [jax.ShapeDtypeStruct((N_DEV, m, tn), dt) for dt in out_dtypes]
    else:
        out_spec = mn_spec
        out_shape = [jax.ShapeDtypeStruct((m, n), dt) for dt in out_dtypes]
    outs = pl.pallas_call(
        body, name=name, grid=(m // tm, n // tn, nk),
        in_specs=[a_spec, b_spec] + [row_spec if e.shape[0] == 1 else mn_spec for e in extras],
        out_specs=[out_spec] * n_out,
        out_shape=out_shape,
        scratch_shapes=[pltpu.VMEM((tm, tn), F32)] if nk > 1 else [],
        compiler_params=_cparams(("parallel", "parallel", "arbitrary")),
    )(a, b_arr, *extras)
    return outs


def matmul_nt_normbwd(dz, w, x, g, dres, name, tm=512):
    m, k = dz.shape
    d = w.shape[0]
    tm = _fit(m, tm)
    w_arr, w_spec, prod = _b_operand(w, "nt", d, k, lambda i: (0, 0))

    def body(dz_ref, w_ref, x_ref, g_ref, dres_ref, dx_ref, dxb_ref, dg_ref):
        @pl.when(pl.program_id(0) == 0)
        def _():
            dg_ref[...] = jnp.zeros_like(dg_ref)

        dh = prod(_bf(dz_ref[...]), w_ref)
        xf = x_ref[...]
        r = lax.rsqrt(jnp.mean(xf * xf, axis=-1, keepdims=True) + EPS)
        xhat = xf * r
        dg_ref[...] += jnp.sum(dh * xhat, axis=0, keepdims=True)
        dxh = dh * g_ref[...]
        dx = r * (dxh - xhat * jnp.mean(dxh * xhat, axis=-1, keepdims=True))
        out = dres_ref[...] + dx
        dx_ref[...] = out
        dxb_ref[...] = out.astype(BF16)

    row = pl.BlockSpec((tm, d), lambda i: (i, 0))
    vec = pl.BlockSpec((1, d), lambda i: (0, 0))
    return pl.pallas_call(
        body, name=name, grid=(m // tm,),
        in_specs=[pl.BlockSpec((tm, k), lambda i: (i, 0)), w_spec, row, vec, row],
        out_specs=[row, row, vec],
        out_shape=[jax.ShapeDtypeStruct((m, d), F32), jax.ShapeDtypeStruct((m, d), BF16), jax.ShapeDtypeStruct((1, d), F32)],
        compiler_params=_cparams(("arbitrary",)),
    )(dz, w_arr, x, g.reshape(1, d), dres)


def _rope_tables(t):
    pos = lax.broadcasted_iota(jnp.int32, (t, LANES), 0)
    dim = lax.broadcasted_iota(jnp.int32, (t, LANES), 1) % HEAD_DIM

    def table(p, width):
        i = dim % (width // 2)
        ang = p.astype(F32) * (ROPE_THETA ** (-(2 * i).astype(F32) / width))
        return jnp.cos(ang), jnp.where(dim % width < width // 2, -jnp.sin(ang), jnp.sin(ang))

    cos_a, sin_a = table(pos, HEAD_DIM)
    cos_b, sin_b = table(jnp.where(dim < HEAD_DIM // 2, pos // GRID_W, pos % GRID_W), HEAD_DIM // 2)
    return cos_a, sin_a, cos_b, sin_b


def _headnorm(xs, gmat):
    return lax.rsqrt(_dot_f32_by_ones(xs * xs, gmat) * (1.0 / HEAD_DIM) + EPS)


def qkv_post_fwd(proj, tables, qn_g, kn_g, name):
    t = proj.shape[0]
    tm = min(t, 256)
    cos_a, sin_a, cos_b, sin_b = tables
    g2 = lambda g: jnp.concatenate([g, g]).reshape(1, LANES)

    def body(p_ref, ca_ref, sa_ref, cb_ref, sb_ref, qg_ref, kg_ref, qa_ref, ka_ref, va_ref, qb_ref, kb_ref, vb_ref):
        ca, sa, cb, sb = ca_ref[...], sa_ref[...], cb_ref[...], sb_ref[...]
        gmat = _group_matrix(HEAD_DIM)

        def rope_a(xs):
            return xs * ca + _swap_halves(xs, HEAD_DIM) * sa

        def norm_rope_b(xs, g):
            y = xs * _headnorm(xs, gmat) * g
            return y * cb + _swap_halves(y, HEAD_DIM // 2) * sb

        for c in range(A_Q // LANES):
            qa_ref[:, c * LANES:(c + 1) * LANES] = rope_a(p_ref[:, OFF_QA + c * LANES:OFF_QA + (c + 1) * LANES]).astype(BF16)
        ka_ref[...] = rope_a(p_ref[:, OFF_KA:OFF_KA + LANES]).astype(BF16)
        va_ref[...] = p_ref[:, OFF_VA:OFF_VA + LANES].astype(BF16)
        for c in range(B_Q // LANES):
            qb_ref[:, c * LANES:(c + 1) * LANES] = norm_rope_b(
                p_ref[:, OFF_QB + c * LANES:OFF_QB + (c + 1) * LANES], qg_ref[...]).astype(BF16)
        kb_ref[...] = norm_rope_b(p_ref[:, OFF_KB:OFF_KB + LANES], kg_ref[...]).astype(BF16)
        vb_ref[...] = p_ref[:, OFF_VB:OFF_VB + LANES].astype(BF16)

    tab = pl.BlockSpec((tm, LANES), lambda i: (i, 0))
    vec = pl.BlockSpec((1, LANES), lambda i: (0, 0))
    wide = pl.BlockSpec((tm, A_Q), lambda i: (i, 0))
    return pl.pallas_call(
        body, name=name, grid=(t // tm,),
        in_specs=[pl.BlockSpec((tm, ATT_IN), lambda i: (i, 0)), tab, tab, tab, tab, vec, vec],
        out_specs=[wide, tab, tab, wide, tab, tab],
        out_shape=[jax.ShapeDtypeStruct((t, A_Q), BF16), jax.ShapeDtypeStruct((t, LANES), BF16),
                   jax.ShapeDtypeStruct((t, LANES), BF16), jax.ShapeDtypeStruct((t, B_Q), BF16),
                   jax.ShapeDtypeStruct((t, LANES), BF16), jax.ShapeDtypeStruct((t, LANES), BF16)],
        compiler_params=_cparams(("parallel",)),
    )(proj, cos_a, sin_a, cos_b, sin_b, g2(qn_g), g2(kn_g))


def qkv_post_bwd(proj, tables, qn_g, kn_g, dqa, dka, dva, dqb, dkb, dvb, name):
    t = proj.shape[0]
    tm = min(t, 256)
    cos_a, sin_a, cos_b, sin_b = tables
    g2 = lambda g: jnp.concatenate([g, g]).reshape(1, LANES)

    def body(p_ref, ca_ref, sa_ref, cb_ref, sb_ref, qg_ref, kg_ref, dqa_ref, dka_ref, dva_ref, dqb_ref, dkb_ref, dvb_ref,
             dp_ref, dqg_ref, dkg_ref):
        ca, sa, cb, sb = ca_ref[...], sa_ref[...], cb_ref[...], sb_ref[...]
        gmat = _group_matrix(HEAD_DIM)

        @pl.when(pl.program_id(0) == 0)
        def _():
            dqg_ref[...] = jnp.zeros_like(dqg_ref)
            dkg_ref[...] = jnp.zeros_like(dkg_ref)

        def rope_a_bwd(dy):
            return dy * ca + _swap_halves(dy * sa, HEAD_DIM)

        def norm_rope_b_bwd(dout, xs, g):
            dy = dout * cb + _swap_halves(dout * sb, HEAD_DIM // 2)
            r = _headnorm(xs, gmat)
            xhat = xs * r
            dxh = dy * g
            mean = _dot_f32_by_ones(dxh * xhat, gmat) * (1.0 / HEAD_DIM)
            return r * (dxh - xhat * mean), jnp.sum(dy * xhat, axis=0, keepdims=True)

        for c in range(A_Q // LANES):
            sl = slice(c * LANES, (c + 1) * LANES)
            dp_ref[:, OFF_QA + c * LANES:OFF_QA + (c + 1) * LANES] = rope_a_bwd(dqa_ref[:, sl].astype(F32)).astype(BF16)
        dp_ref[:, OFF_KA:OFF_KA + LANES] = rope_a_bwd(dka_ref[0] + dka_ref[1]).astype(BF16)
        dp_ref[:, OFF_VA:OFF_VA + LANES] = (dva_ref[0] + dva_ref[1]).astype(BF16)
        dqg = jnp.zeros((1, LANES), F32)
        for c in range(B_Q // LANES):
            sl = slice(c * LANES, (c + 1) * LANES)
            dx, dg = norm_rope_b_bwd(dqb_ref[:, sl].astype(F32), p_ref[:, OFF_QB + c * LANES:OFF_QB + (c + 1) * LANES], qg_ref[...])
            dp_ref[:, OFF_QB + c * LANES:OFF_QB + (c + 1) * LANES] = dx.astype(BF16)
            dqg = dqg + dg
        dqg_ref[...] += dqg
        dx, dg = norm_rope_b_bwd((dkb_ref[0] + dkb_ref[1]).T, p_ref[:, OFF_KB:OFF_KB + LANES], kg_ref[...])
        dp_ref[:, OFF_KB:OFF_KB + LANES] = dx.astype(BF16)
        dkg_ref[...] += dg
        dp_ref[:, OFF_VB:OFF_VB + LANES] = (dvb_ref[0] + dvb_ref[1]).T.astype(BF16)

        @pl.when(pl.program_id(0) == t // tm - 1)
        def _():
            dqg_ref[...] = dqg_ref[...] + pltpu.roll(dqg_ref[...], HEAD_DIM, 1)
            dkg_ref[...] = dkg_ref[...] + pltpu.roll(dkg_ref[...], HEAD_DIM, 1)

    tab = pl.BlockSpec((tm, LANES), lambda i: (i, 0))
    vec = pl.BlockSpec((1, LANES), lambda i: (0, 0))
    wide = pl.BlockSpec((tm, A_Q), lambda i: (i, 0))
    slab = pl.BlockSpec((2, tm, LANES), lambda i: (0, i, 0))
    per_chunk = dkb.shape[3] // tm
    slab_t = pl.BlockSpec((2, None, LANES, tm), lambda i: (0, i // per_chunk, 0, i % per_chunk))
    full = pl.BlockSpec((tm, ATT_IN), lambda i: (i, 0))
    return pl.pallas_call(
        body, name=name, grid=(t // tm,),
        in_specs=[full, tab, tab, tab, tab, vec, vec, wide, slab, slab, wide, slab_t, slab_t],
        out_specs=[full, vec, vec],
        out_shape=[jax.ShapeDtypeStruct((t, ATT_IN), BF16), jax.ShapeDtypeStruct((1, LANES), F32),
                   jax.ShapeDtypeStruct((1, LANES), F32)],
        compiler_params=_cparams(("arbitrary",)),
    )(proj, cos_a, sin_a, cos_b, sin_b, g2(qn_g), g2(kn_g), dqa, dka, dva, dqb, dkb, dvb)


def _head_to_half(xs, head_half, kv_half):
    low = _lane(xs.shape) < HEAD_DIM
    kept = jnp.where(low if head_half == 0 else jnp.logical_not(low), xs, 0.0)
    return jnp.where(kv_half == head_half, kept, pltpu.roll(kept, HEAD_DIM, 1))


def _halves_to_heads(r0, r1, kv_half):
    low = _lane(r0.shape) < HEAD_DIM
    a = jnp.where(kv_half == 0, r0, pltpu.roll(r0, HEAD_DIM, 1))
    b = jnp.where(kv_half == 1, r1, pltpu.roll(r1, HEAD_DIM, 1))
    return jnp.where(low, a, b)


def attn_delta(o, do, name):
    t, w = o.shape
    tm = min(t, 512)
    n_heads = w // HEAD_DIM

    def body(o_ref, do_ref, d_ref):
        lo, hi = _seg_matrix(0, HEAD_DIM), _seg_matrix(HEAD_DIM, LANES)
        for c in range(w // LANES):
            sl = slice(c * LANES, (c + 1) * LANES)
            s = o_ref[:, sl].astype(F32) * do_ref[:, sl].astype(F32)
            d_ref[2 * c] = _dot_f32_by_ones(s, lo)
            d_ref[2 * c + 1] = _dot_f32_by_ones(s, hi)

    blk = pl.BlockSpec((tm, w), lambda i: (i, 0))
    return pl.pallas_call(
        body, name=name, grid=(t // tm,),
        in_specs=[blk, blk],
        out_specs=pl.BlockSpec((n_heads, tm, LANES), lambda i: (0, i, 0)),
        out_shape=jax.ShapeDtypeStruct((n_heads, t, LANES), F32),
        compiler_params=_cparams(("parallel",)),
    )(o, do)


BAND = 3 * BLOCK


def _band_offsets(rows_rep):
    qi = lax.broadcasted_iota(jnp.int32, (BLOCK, BAND), 0)
    kj = lax.broadcasted_iota(jnp.int32, (BLOCK, BAND), 1)
    return jnp.concatenate([kj - qi] * rows_rep, axis=0)


def _band(n, t, offsets):
    start = pl.multiple_of(jnp.clip((n - 1) * BLOCK, 0, t - BAND), BLOCK)
    return start, jnp.abs(offsets + (start - n * BLOCK)) <= WINDOW


def window_attn_fwd(q, k, v, sink, name, blocks_per_step=8):
    t = q.shape[0]
    assert t >= BAND
    nq = _fit(t // BLOCK, blocks_per_step)
    tq = nq * BLOCK

    def body(sink_ref, q_ref, k_ref, v_ref, o_ref, lse_ref):
        j, n0 = pl.program_id(0), pl.program_id(1)
        kvh = j // 2
        row = lax.broadcasted_iota(jnp.int32, (2 * BLOCK, 1), 0)
        sk = jnp.where(row < BLOCK, sink_ref[2 * j], sink_ref[2 * j + 1]) * LOG2E
        offsets = _band_offsets(2)
        bands, scores = [], []
        for u in range(nq):
            start, ok = _band(n0 * nq + u, t, offsets)
            qf = q_ref[u * BLOCK:(u + 1) * BLOCK, :].astype(F32) * (SCALE * LOG2E)
            qs = jnp.concatenate([_head_to_half(qf, 0, kvh), _head_to_half(qf, 1, kvh)], axis=0).astype(BF16)
            bands.append(pl.ds(start, BAND))
            scores.append(jnp.where(ok, _dot_nt(qs, k_ref[bands[u], :]), NEG))
        soft = []
        for s in scores:
            m = jnp.maximum(jnp.max(s, axis=-1, keepdims=True), sk)
            p = jnp.exp2(s - m)
            soft.append((p.astype(BF16), jnp.sum(p, axis=-1, keepdims=True) + jnp.exp2(sk - m), m))
        for u, (p, denom, m) in enumerate(soft):
            rows = slice(u * BLOCK, (u + 1) * BLOCK)
            o = _dot_nn(p, v_ref[bands[u], :]) / denom
            o_ref[rows, :] = _halves_to_heads(o[:BLOCK], o[BLOCK:], kvh).astype(BF16)
            lse = jnp.broadcast_to(m + jnp.log2(denom), (2 * BLOCK, LANES))
            lse_ref[0, rows, :] = lse[:BLOCK]
            lse_ref[1, rows, :] = lse[BLOCK:]

    qspec = pl.BlockSpec((tq, LANES), lambda j, n: (n, j))
    whole = pl.BlockSpec((t, LANES), lambda j, n: (0, 0))
    return pl.pallas_call(
        body, name=name, grid=(A_HEADS // 2, t // tq),
        in_specs=[pl.BlockSpec(memory_space=pltpu.SMEM), qspec, whole, whole],
        out_specs=[qspec, pl.BlockSpec((2, tq, LANES), lambda j, n: (j, n, 0))],
        out_shape=[jax.ShapeDtypeStruct((t, A_Q + B_Q), BF16), jax.ShapeDtypeStruct((A_HEADS, t, LANES), F32)],
        compiler_params=_cparams(("parallel", "parallel")),
    )(sink, q, k, v)


def window_attn_bwd(q, k, v, sink, do, lse, delta, name, blocks_per_step=4):
    t = q.shape[0]
    assert t >= BAND
    nq = _fit(t // BLOCK, blocks_per_step)
    tq = nq * BLOCK
    grp = A_HEADS // A_KV_HEADS
    gw = grp * HEAD_DIM

    def body(sink_ref, q_ref, do_ref, k_ref, v_ref, lse_ref, dl_ref, dq_ref, dk_ref, dv_ref, ds_ref):
        kvh, n0 = pl.program_id(0), pl.program_id(1)

        @pl.when(n0 == 0)
        def _():
            dk_ref[...] = jnp.zeros_like(dk_ref)
            dv_ref[...] = jnp.zeros_like(dv_ref)
            ds_ref[...] = jnp.zeros_like(ds_ref)

        rid = lax.broadcasted_iota(jnp.int32, (8, LANES), 0)
        upd = jnp.zeros((8, LANES), F32)
        offsets = _band_offsets(grp)
        for u in range(nq):
            rows = slice(u * BLOCK, (u + 1) * BLOCK)
            start, ok = _band(n0 * nq + u, t, offsets)
            band = pl.ds(start, BAND)
            qparts, doparts = [], []
            for hh in range(grp):
                sl = slice((hh // 2) * LANES, (hh // 2 + 1) * LANES)
                qparts.append(_head_to_half(q_ref[rows, sl].astype(F32) * (SCALE * LOG2E), hh % 2, kvh))
                doparts.append(_head_to_half(do_ref[rows, sl].astype(F32), hh % 2, kvh))
            qs = jnp.concatenate(qparts, axis=0).astype(BF16)
            dos = jnp.concatenate(doparts, axis=0).astype(BF16)
            lse_b = jnp.concatenate([lse_ref[hh, rows, :] for hh in range(grp)], axis=0)
            dl_b = jnp.concatenate([dl_ref[hh, rows, :] for hh in range(grp)], axis=0)
            kband, vband = k_ref[band, :], v_ref[band, :]
            s = jnp.where(ok, _dot_nt(qs, kband), NEG)
            p = jnp.exp2(s - lse_b[:, :1])
            dp = _dot_nt(dos, vband)
            dsc = (p * (dp - dl_b[:, :1])).astype(BF16)
            dv_ref[0, band, :] += _dot_tn(p.astype(BF16), dos)
            dk_ref[0, band, :] += _dot_tn(dsc, qs) * LN2
            dq = _dot_nn(dsc, kband) * SCALE
            for c in range(grp // 2):
                dq_ref[rows, c * LANES:(c + 1) * LANES] = _halves_to_heads(
                    dq[2 * c * BLOCK:(2 * c + 1) * BLOCK], dq[(2 * c + 1) * BLOCK:(2 * c + 2) * BLOCK], kvh).astype(dq_ref.dtype)
            for hh in range(grp):
                rs = slice(hh * BLOCK, (hh + 1) * BLOCK)
                tot = jnp.sum(jnp.exp2(sink_ref[kvh * grp + hh] * LOG2E - lse_b[rs]) * dl_b[rs], axis=0, keepdims=True)
                upd = upd + jnp.where(rid == hh, -tot, 0.0)
        ds_ref[0] += upd

    qspec = pl.BlockSpec((tq, gw), lambda kvh, n: (n, kvh))
    whole = pl.BlockSpec((t, LANES), lambda kvh, n: (0, 0))
    stat = pl.BlockSpec((grp, tq, LANES), lambda kvh, n: (kvh, n, 0))
    slab = pl.BlockSpec((1, t, LANES), lambda kvh, n: (kvh, 0, 0))
    return pl.pallas_call(
        body, name=name, grid=(A_KV_HEADS, t // tq),
        in_specs=[pl.BlockSpec(memory_space=pltpu.SMEM), qspec, qspec, whole, whole, stat, stat],
        out_specs=[qspec, slab, slab, pl.BlockSpec((1, 8, LANES), lambda kvh, n: (kvh, 0, 0))],
        out_shape=[jax.ShapeDtypeStruct((t, A_Q), BF16), jax.ShapeDtypeStruct((A_KV_HEADS, t, LANES), F32),
                   jax.ShapeDtypeStruct((A_KV_HEADS, t, LANES), F32), jax.ShapeDtypeStruct((A_KV_HEADS, 8, LANES), F32)],
        compiler_params=_cparams(("arbitrary", "arbitrary")),
    )(sink, q, do, k, v, lse, delta)


def flash_attn_fwd(q, k, v, cat, name, exchange=None, tq=1024, tk=512, ahead=2):
    t = q.shape[0]
    tq, tk = _fit(t, tq), _fit(t, tk)
    nk = t // tk

    def body(q_ref, k_ref, v_ref, cat_ref, o_ref, lse_ref):
        del cat_ref
        kvh = pl.program_id(0) // 2
        qf = q_ref[...].astype(F32) * (SCALE * LOG2E)
        qs = jnp.concatenate([_head_to_half(qf, 0, kvh), _head_to_half(qf, 1, kvh)], axis=0).astype(BF16)
        mine = (_lane((tk, LANES)) < HEAD_DIM) == (kvh == 0)

        def scores(c):
            return _dot_nt(qs, k_ref[c * tk:(c + 1) * tk, :])

        s = [scores(c) for c in range(min(ahead, nk))]
        m = jnp.full((2 * tq, 1), NEG, F32)
        acc = jnp.zeros((2 * tq, LANES), F32)
        for c in range(nk):
            if c + ahead < nk:
                s.append(scores(c + ahead))
            vb = jnp.where(mine, v_ref[c * tk:(c + 1) * tk, :], jnp.ones((), BF16))
            m_new = jnp.maximum(m, jnp.max(s[c], axis=-1, keepdims=True))
            p = jnp.exp2(s[c] - m_new).astype(BF16)
            acc = jnp.exp2(m - m_new) * acc + _dot_nn(p, vb)
            m = m_new
        other = pltpu.roll(acc, HEAD_DIM, 1)
        o = acc / other
        o_ref[...] = _halves_to_heads(o[:tq], o[tq:], kvh).astype(BF16)
        in_mine = (_lane(acc.shape) < HEAD_DIM) == (kvh == 0)
        lse = jnp.broadcast_to(m, acc.shape) + jnp.log2(jnp.where(in_mine, other, acc))
        lse_ref[0] = lse[:tq]
        lse_ref[1] = lse[tq:]

    qspec = pl.BlockSpec((tq, LANES), lambda j, i: (i, j))
    whole = pl.BlockSpec((t, LANES), lambda j, i: (0, 0))
    nj, ni = B_HEADS // 2, t // tq
    steps = lambda: ((pl.program_id(0) == 0) & (pl.program_id(1) == 0), (pl.program_id(0) == nj - 1) & (pl.program_id(1) == ni - 1))
    body, x_in, x_out, x_shapes, x_scratch = carried(body, exchange, 4, 2, steps)
    return pl.pallas_call(
        body, name=name, grid=(nj, ni),
        in_specs=[qspec, whole, whole, _ANY] + x_in,
        out_specs=[pl.BlockSpec((tq, LANES), lambda j, i: (i, A_Q // LANES + j)),
                   pl.BlockSpec((2, tq, LANES), lambda j, i: (j, i, 0))] + x_out,
        out_shape=[jax.ShapeDtypeStruct(cat.shape, BF16), jax.ShapeDtypeStruct((B_HEADS, t, LANES), F32)] + x_shapes,
        scratch_shapes=x_scratch,
        input_output_aliases={3: 0},
        compiler_params=_cparams(("arbitrary", "arbitrary")),
    )(q, k, v, cat, *(exchange.arrays if exchange else ()))


def flash_attn_bwd(q, k, v, do, lse, delta, name, exchange=None, tq=512, tk=512, together=4):
    t = q.shape[0]
    tq, tk = _fit(t, tq), _fit(t, tk)
    nk = t // tk
    together = _fit(nk, together)
    grp = B_HEADS // B_KV_HEADS
    gw = grp * HEAD_DIM

    def body(q_ref, do_ref, k_ref, v_ref, lse_ref, dl_ref, dq_ref, dk_ref, dv_ref, dq_s):
        kvh, i = pl.program_id(0), pl.program_id(1)

        @pl.when(i == 0)
        def _():
            dk_ref[...] = jnp.zeros_like(dk_ref)
            dv_ref[...] = jnp.zeros_like(dv_ref)

        qparts, doparts = [], []
        for hh in range(grp):
            sl = slice((hh // 2) * LANES, (hh // 2 + 1) * LANES)
            qparts.append(_head_to_half(q_ref[:, sl].astype(F32) * (SCALE * LOG2E), hh % 2, kvh))
            doparts.append(_head_to_half(do_ref[:, sl].astype(F32), hh % 2, kvh))
        qf, dof = jnp.concatenate(qparts, axis=0), jnp.concatenate(doparts, axis=0)
        qs, dos = qf.astype(BF16), dof.astype(BF16)
        qs_t, dos_t = qf.T.astype(BF16), dof.T.astype(BF16)
        lse = jnp.tile(jnp.concatenate([lse_ref[hh] for hh in range(grp)], axis=0), (1, tk // LANES))
        dl = jnp.tile(jnp.concatenate([dl_ref[hh] for hh in range(grp)], axis=0), (1, tk // LANES))
        dq_s[...] = jnp.zeros_like(dq_s)

        def chunks(c0, carry):
            cs = [c0 * together + u for u in range(together)]
            kbs = [k_ref[pl.ds(pl.multiple_of(c * tk, tk), tk), :] for c in cs]
            vbs = [v_ref[pl.ds(pl.multiple_of(c * tk, tk), tk), :] for c in cs]
            ss = [_dot_nt(qs, kb) for kb in kbs]
            dps = [_dot_nt(dos, vb) for vb in vbs]
            for c, kb, s, dp in zip(cs, kbs, ss, dps):
                p = jnp.exp2(s - lse)
                dsc = (p * (dp - dl)).astype(BF16)
                dv_ref[0, c] += _dot_nn(dos_t, p.astype(BF16))
                dk_ref[0, c] += _dot_nn(qs_t, dsc) * LN2
                dq_s[...] += _dot_nn(kb.T, dsc.T)
            return carry

        lax.fori_loop(0, nk // together, chunks, 0)
        dq = dq_s[...].T
        for c in range(grp // 2):
            dq_ref[:, c * LANES:(c + 1) * LANES] = (_halves_to_heads(
                dq[2 * c * tq:(2 * c + 1) * tq], dq[(2 * c + 1) * tq:(2 * c + 2) * tq], kvh) * SCALE).astype(dq_ref.dtype)

    qspec = pl.BlockSpec((tq, gw), lambda kvh, i: (i, kvh))
    dospec = pl.BlockSpec((tq, gw), lambda kvh, i: (i, A_Q // gw + kvh))
    whole = pl.BlockSpec((t, LANES), lambda kvh, i: (0, 0))
    stat = pl.BlockSpec((grp, tq, LANES), lambda kvh, i: (kvh, i, 0))
    dlstat = pl.BlockSpec((grp, tq, LANES), lambda kvh, i: (A_HEADS // grp + kvh, i, 0))
    slab = pl.BlockSpec((1, nk, LANES, tk), lambda kvh, i: (kvh, 0, 0, 0))
    ni = t // tq
    steps = lambda: ((pl.program_id(0) == 0) & (pl.program_id(1) == 0),
                     (pl.program_id(0) == B_KV_HEADS - 1) & (pl.program_id(1) == ni - 1))
    body, x_in, x_out, x_shapes, x_scratch = carried(body, exchange, 6, 3, steps)
    return pl.pallas_call(
        body, name=name, grid=(B_KV_HEADS, ni),
        in_specs=[qspec, dospec, whole, whole, stat, dlstat] + x_in,
        out_specs=[qspec, slab, slab] + x_out,
        out_shape=[jax.ShapeDtypeStruct((t, B_Q), BF16), jax.ShapeDtypeStruct((B_KV_HEADS, nk, LANES, tk), F32),
                   jax.ShapeDtypeStruct((B_KV_HEADS, nk, LANES, tk), F32)] + x_shapes,
        scratch_shapes=[pltpu.VMEM((LANES, grp * tq), F32)] + x_scratch,
        compiler_params=_cparams(("arbitrary", "arbitrary")),
    )(q, do, k, v, lse, delta, *(exchange.arrays if exchange else ()))


_GELU_C = math.sqrt(2.0 / math.pi)
_GELU_A = 0.044715


def _gelu(x):
    return 0.5 * x * (1.0 + jnp.tanh(_GELU_C * (x + _GELU_A * x * x * x)))


def _gelu_grad(x):
    th = jnp.tanh(_GELU_C * (x + _GELU_A * x * x * x))
    return 0.5 * (1.0 + th) + 0.5 * x * (1.0 - th * th) * _GELU_C * (1.0 + 3.0 * _GELU_A * x * x)


def _layernorm_stats(vf):
    mu = jnp.mean(vf, axis=-1, keepdims=True)
    vc = vf - mu
    r = lax.rsqrt(jnp.mean(vc * vc, axis=-1, keepdims=True) + EPS)
    return vc * r, r


def sgu_mix_fwd(z, ln_g, ln_b, w_s, b_rows, name):
    t, w2 = z.shape
    w = w2 // 2
    dg = w // SGU_GROUPS

    def body(u_ref, v_ref, g_ref, b_ref, ws_ref, bb_ref, y_ref):
        vhat, _ = _layernorm_stats(v_ref[...].astype(F32))
        vn = (vhat * g_ref[...] + b_ref[...]).astype(BF16)
        for g in range(SGU_GROUPS):
            sl = slice(g * dg, (g + 1) * dg)
            mixed = _dot_nn(ws_ref[g], vn[:, sl]) + bb_ref[g]
            y_ref[:, sl] = (u_ref[:, sl].astype(F32) * mixed).astype(BF16)

    vec = pl.BlockSpec((1, w), lambda n: (0, 0))
    whole = pl.BlockSpec((SGU_GROUPS, SGU_CHUNK, SGU_CHUNK), lambda n: (0, 0, 0))
    return pl.pallas_call(
        body, name=name, grid=(t // SGU_CHUNK,),
        in_specs=[pl.BlockSpec((SGU_CHUNK, w), lambda n: (n, 0)), pl.BlockSpec((SGU_CHUNK, w), lambda n: (n, 1)),
                  vec, vec, whole, whole],
        out_specs=pl.BlockSpec((SGU_CHUNK, w), lambda n: (n, 0)),
        out_shape=jax.ShapeDtypeStruct((t, w), BF16),
        compiler_params=_cparams(("parallel",)),
    )(z, z, ln_g.reshape(1, w), ln_b.reshape(1, w), w_s, b_rows)


def sgu_mix_bwd(z, apre, dy, ln_g, ln_b, w_s, b_rows, name):
    t, w2 = z.shape
    w = w2 // 2
    dg = w // SGU_GROUPS

    def body(u_ref, v_ref, au_ref, av_ref, dy_ref, g_ref, b_ref, ws_ref, bb_ref, da_ref, dlg_ref, dlb_ref, dws_ref, dbs_ref):
        @pl.when(pl.program_id(0) == 0)
        def _():
            dlg_ref[...] = jnp.zeros_like(dlg_ref)
            dlb_ref[...] = jnp.zeros_like(dlb_ref)
            dws_ref[...] = jnp.zeros_like(dws_ref)
            dbs_ref[...] = jnp.zeros_like(dbs_ref)

        vhat, r = _layernorm_stats(v_ref[...].astype(F32))
        gam = g_ref[...]
        vn = (vhat * gam + b_ref[...]).astype(BF16)
        ones8 = jnp.ones((8, dg), BF16)
        rid = lax.broadcasted_iota(jnp.int32, (8, SGU_CHUNK), 0)
        dbs = jnp.zeros((8, SGU_CHUNK), F32)
        dvn_parts = []
        for g in range(SGU_GROUPS):
            sl = slice(g * dg, (g + 1) * dg)
            dyg = dy_ref[:, sl].astype(F32)
            mixed = _dot_nn(ws_ref[g], vn[:, sl]) + bb_ref[g]
            da_ref[:, sl] = (dyg * mixed * _gelu_grad(au_ref[:, sl].astype(F32))).astype(BF16)
            dmix = dyg * u_ref[:, sl].astype(F32)
            dm_hi = dmix.astype(BF16)
            dm_lo = (dmix - dm_hi.astype(F32)).astype(BF16)
            dws_ref[g] += _dot_nt(dm_hi, vn[:, sl])
            dbs = dbs + jnp.where(rid == g, _dot_nt(ones8, dm_hi) + _dot_nt(ones8, dm_lo), 0.0)
            dvn_parts.append(_dot_tn(ws_ref[g], dm_hi))
        dbs_ref[...] += dbs
        dvn = jnp.concatenate(dvn_parts, axis=1)
        dlg_ref[...] += jnp.sum(dvn * vhat, axis=0, keepdims=True)
        dlb_ref[...] += jnp.sum(dvn, axis=0, keepdims=True)
        dvh = dvn * gam
        dv = r * (dvh - jnp.mean(dvh, axis=-1, keepdims=True) - vhat * jnp.mean(dvh * vhat, axis=-1, keepdims=True))
        da_ref[:, w:] = (dv * _gelu_grad(av_ref[...].astype(F32))).astype(BF16)

    vec = pl.BlockSpec((1, w), lambda n: (0, 0))
    whole = pl.BlockSpec((SGU_GROUPS, SGU_CHUNK, SGU_CHUNK), lambda n: (0, 0, 0))
    left = pl.BlockSpec((SGU_CHUNK, w), lambda n: (n, 0))
    right = pl.BlockSpec((SGU_CHUNK, w), lambda n: (n, 1))
    return pl.pallas_call(
        body, name=name, grid=(t // SGU_CHUNK,),
        in_specs=[left, right, left, right, left, vec, vec, whole, whole],
        out_specs=[pl.BlockSpec((SGU_CHUNK, w2), lambda n: (n, 0)), vec, vec, whole,
                   pl.BlockSpec((SGU_GROUPS, SGU_CHUNK), lambda n: (0, 0))],
        out_shape=[jax.ShapeDtypeStruct((t, w2), BF16), jax.ShapeDtypeStruct((1, w), F32), jax.ShapeDtypeStruct((1, w), F32),
                   jax.ShapeDtypeStruct((SGU_GROUPS, SGU_CHUNK, SGU_CHUNK), F32),
                   jax.ShapeDtypeStruct((SGU_GROUPS, SGU_CHUNK), F32)],
        compiler_params=_cparams(("arbitrary",)),
    )(z, z, apre, apre, dy, ln_g.reshape(1, w), ln_b.reshape(1, w), w_s, b_rows)


def loss_head(h, g, target, name):
    t, d = h.shape
    tm = min(t, 512)

    def body(h_ref, g_ref, t_ref, loss_ref, dh_ref, dhb_ref, dg_ref):
        @pl.when(pl.program_id(0) == 0)
        def _():
            loss_ref[...] = jnp.zeros_like(loss_ref)
            dg_ref[...] = jnp.zeros_like(dg_ref)

        xf = h_ref[...]
        r = lax.rsqrt(jnp.mean(xf * xf, axis=-1, keepdims=True) + EPS)
        xhat = xf * r
        err = xhat * g_ref[...] - t_ref[...]
        per_tok = jnp.mean(err * err, axis=-1, keepdims=True)
        loss_ref[...] += 0.5 * jnp.sum(per_tok, axis=0, keepdims=True)
        dy = err * (1.0 / d)
        dg_ref[...] += jnp.sum(dy * xhat, axis=0, keepdims=True)
        dxh = dy * g_ref[...]
        dh = r * (dxh - xhat * jnp.mean(dxh * xhat, axis=-1, keepdims=True))
        dh_ref[...] = dh
        dhb_ref[...] = dh.astype(BF16)

    row = pl.BlockSpec((tm, d), lambda i: (i, 0))
    vec = pl.BlockSpec((1, d), lambda i: (0, 0))
    return pl.pallas_call(
        body, name=name, grid=(t // tm,),
        in_specs=[row, vec, row],
        out_specs=[pl.BlockSpec((1, LANES), lambda i: (0, 0)), row, row, vec],
        out_shape=[jax.ShapeDtypeStruct((1, LANES), F32), jax.ShapeDtypeStruct((t, d), F32), jax.ShapeDtypeStruct((t, d), BF16),
                   jax.ShapeDtypeStruct((1, d), F32)],
        compiler_params=_cparams(("arbitrary",)),
    )(h, g.reshape(1, d), target)


ADAMW_BLOCK_BYTES = 1 << 20


def adamw(parts, w, m, v, name):
    n_layers, r, c = w.shape
    row_bytes = n_layers * c * 4
    if r * row_bytes <= 2 * ADAMW_BLOCK_BYTES:
        tr = r
    else:
        tr = _fit(r, 1 << int(math.log2(max(8, ADAMW_BLOCK_BYTES // row_bytes))))
    bc1 = 1.0 - ADAM_B1 ** ADAM_STEP
    bc2 = 1.0 - ADAM_B2 ** ADAM_STEP

    def body(*refs):
        p_refs = refs[:n_layers]
        w_ref, m_ref, v_ref, g_ref, d_ref, nm_ref, nv_ref = refs[n_layers:]
        for l in range(n_layers):
            g = p_refs[l][0].astype(F32)
            for j in range(1, N_DEV):
                g = g + p_refs[l][j].astype(F32)
            nm = ADAM_B1 * m_ref[l] + (1.0 - ADAM_B1) * g
            nv = ADAM_B2 * v_ref[l] + (1.0 - ADAM_B2) * (g * g)
            g_ref[l] = g
            nm_ref[l] = nm
            nv_ref[l] = nv
            d_ref[l] = -ADAM_LR * ((nm / bc1) / (jnp.sqrt(nv / bc2) + ADAM_EPS) + ADAM_WD * w_ref[l])

    blk = pl.BlockSpec((n_layers, tr, c), lambda i: (0, i, 0))
    return pl.pallas_call(
        body, name=name, grid=(r // tr,),
        in_specs=[pl.BlockSpec((N_DEV, tr, c), lambda i: (0, i, 0))] * n_layers + [blk, blk, blk],
        out_specs=[blk] * 4,
        out_shape=[jax.ShapeDtypeStruct((n_layers, r, c), F32)] * 4,
        compiler_params=_cparams(("parallel",)),
    )(*parts, w, m, v)


_ANY = pl.BlockSpec(memory_space=pl.ANY)


def _mesh_pos():
    return lax.axis_index("x"), lax.axis_index("y"), lax.axis_index("c")


class Exchange:
    def __init__(self, gathers=(), scatters=()):
        self.items = [("gather", a) for a in gathers] + [("scatter", a) for a in scatters]
        self.arrays = [a for _, a in self.items]
        self.n = len(self.items)

    def out_shapes(self):
        return [jax.ShapeDtypeStruct(((N_DEV,) + a.shape) if kind == "gather" else a.shape, a.dtype) for kind, a in self.items]

    def scratch(self):
        return [pltpu.SemaphoreType.DMA((7 * self.n,)), pltpu.SemaphoreType.DMA((7 * self.n,)), pltpu.SemaphoreType.DMA((self.n,))]

    def _copies(self, in_refs, out_refs, send_sems, recv_sems, local_sems):
        x, y, c = _mesh_pos()
        me = 4 * x + 2 * y + c
        local, sends, arrivals = [], [], []
        for t, (kind, _) in enumerate(self.items):
            src_of = (lambda slot, r=in_refs[t]: r) if kind == "gather" else (lambda slot, r=in_refs[t]: r.at[slot])
            local.append(pltpu.make_async_copy(src_of(me), out_refs[t].at[me], local_sems.at[t]))
            for k in range(1, N_DEV):
                px = 1 - x if k & 4 else x
                py = 1 - y if k & 2 else y
                pc = 1 - c if k & 1 else c
                pid = 4 * px + 2 * py + pc
                kw = dict(send_sem=send_sems.at[7 * t + k - 1], recv_sem=recv_sems.at[7 * t + k - 1],
                          device_id=(px, py, pc), device_id_type=pl.DeviceIdType.MESH)
                sends.append(pltpu.make_async_remote_copy(src_ref=src_of(pid), dst_ref=out_refs[t].at[me], **kw))
                arrivals.append(pltpu.make_async_remote_copy(src_ref=src_of(pid), dst_ref=out_refs[t].at[pid], **kw))
        return local, sends, arrivals

    def start(self, *refs):
        local, sends, _ = self._copies(*refs)
        for cp in local + sends:
            cp.start()

    def wait(self, *refs):
        local, sends, arrivals = self._copies(*refs)
        for cp in arrivals:
            cp.wait_recv()
        for cp in sends:
            cp.wait_send()
        for cp in local:
            cp.wait()


def carried(body, exchange, n_in, n_out, first_last):
    if exchange is None:
        return body, [], [], [], []
    nx = exchange.n

    def wrapped(*refs):
        ins, xin = refs[:n_in], refs[n_in:n_in + nx]
        outs, xout = refs[n_in + nx:n_in + nx + n_out], refs[n_in + nx + n_out:n_in + 2 * nx + n_out]
        scratch, sems = refs[n_in + 2 * nx + n_out:-3], refs[-3:]
        first, last = first_last()

        @pl.when(first)
        def _():
            exchange.start(xin, xout, *sems)

        body(*ins, *outs, *scratch)

        @pl.when(last)
        def _():
            exchange.wait(xin, xout, *sems)

    return wrapped, [_ANY] * nx, [_ANY] * nx, exchange.out_shapes(), exchange.scratch()


def exchange_only(exchange, name):
    def body(*refs):
        xin, xout, sems = refs[:exchange.n], refs[exchange.n:2 * exchange.n], refs[-3:]
        exchange.start(xin, xout, *sems)
        exchange.wait(xin, xout, *sems)

    return pl.pallas_call(
        body, name=name, in_specs=[_ANY] * exchange.n, out_specs=[_ANY] * exchange.n,
        out_shape=exchange.out_shapes(), scratch_shapes=exchange.scratch(),
    )(*exchange.arrays)


def _residual_out(a, w_out, x, next_g, name, **tiles):
    if next_g is None:
        (y,) = matmul(a, w_out, "nn", name, [F32], epilogue=lambda acc, r: (r + acc,), extras=(x,), **tiles)
        return y, None

    def add_and_norm(acc, r, g):
        y = r + acc
        return y, y * lax.rsqrt(jnp.mean(y * y, axis=-1, keepdims=True) + EPS) * g

    assert w_out.shape[1] <= tiles.get("tn", 1024)
    return matmul(a, w_out, "nn", name, [F32, BF16], epilogue=add_and_norm, extras=(x, next_g.reshape(1, -1)), **tiles)


def attention_fwd(x, h, w_in, sink, qn_g, kn_g, w_out, tables, next_g, tag, exchange=None):
    (proj,) = matmul(h, w_in, "nn", f"{tag}_proj", [F32], tn=ATT_IN)
    qa, ka, va, qb, kb, vb = qkv_post_fwd(proj, tables, qn_g, kn_g, f"{tag}_qkv")
    cat, lse_a = window_attn_fwd(qa, ka, va, sink, f"{tag}_win")
    cat, lse_b, *arrived = flash_attn_fwd(qb, kb, vb, cat, f"{tag}_flash", exchange)
    if callable(w_out):
        w_out = w_out(arrived)
    y, h_next = _residual_out(cat, w_out, x, next_g, f"{tag}_out")
    saved = (x, h, proj, qa, ka, va, qb, kb, vb, cat, lse_a, lse_b)
    return y, h_next, saved, arrived


def attention_bwd(dy, dyb, saved, norm_g, w_in, sink, qn_g, kn_g, w_out, tables, tag, exchange_with=None):
    x, h, proj, qa, ka, va, qb, kb, vb, cat, lse_a, lse_b = saved
    (dcat,) = matmul(dyb, w_out, "nt", f"{tag}_dcat", [BF16])
    (dw_out,) = matmul(cat, dyb, "tn", f"{tag}_dwout", [BF16], tk=4096)
    delta = attn_delta(cat, dcat, f"{tag}_delta")
    dqa, dka, dva, dsink = window_attn_bwd(qa, ka, va, sink, dcat, lse_a, delta, f"{tag}_dwin")
    exchange = exchange_with(dw_out) if exchange_with else None
    dqb, dkb, dvb, *arrived = flash_attn_bwd(qb, kb, vb, dcat, lse_b, delta, f"{tag}_dflash", exchange)
    dproj, dqg, dkg = qkv_post_bwd(proj, tables, qn_g, kn_g, dqa, dka, dva, dqb, dkb, dvb, f"{tag}_dqkv")
    (dw_in,) = matmul(h, dproj, "tn", f"{tag}_dwin_w", [BF16], tn=ATT_IN // 2, tk=2048)
    dx, dxb, dg = matmul_nt_normbwd(dproj, w_in, x, norm_g, dy, f"{tag}_dx")
    grp = A_HEADS // A_KV_HEADS
    small = dict(norm=dg[0], sink=dsink[:, :grp, 0].reshape(A_HEADS), qnorm=dqg[0, :HEAD_DIM], knorm=dkg[0, :HEAD_DIM])
    return dx, dxb, dw_in, dw_out, small, arrived


def sgu_fwd(x, h, w_in, ln_g, ln_b, w_s, b_rows, w_out, next_g, tag):
    apre, z = matmul(h, w_in, "nn", f"{tag}_in", [BF16, BF16], epilogue=lambda acc: (acc, _gelu(acc)))
    y = sgu_mix_fwd(z, ln_g, ln_b, w_s, b_rows, f"{tag}_mix")
    out, h_next = _residual_out(y, w_out, x, next_g, f"{tag}_out")
    return out, h_next, (x, h, apre, z, y)


def sgu_bwd(dout, doutb, saved, norm_g, w_in, ln_g, ln_b, w_s, b_rows, w_out, tag):
    x, h, apre, z, y = saved
    (dy,) = matmul(doutb, w_out, "nt", f"{tag}_dy", [BF16])
    (dw_out,) = matmul(y, doutb, "tn", f"{tag}_dwout", [BF16], tk=4096)
    dapre, dlg, dlb, dws, dbs = sgu_mix_bwd(z, apre, dy, ln_g, ln_b, w_s, b_rows, f"{tag}_dmix")
    (dw_in,) = matmul(h, dapre, "tn", f"{tag}_dwin", [BF16], out_shards=True, tk=4096)
    dx, dxb, dg = matmul_nt_normbwd(dapre, w_in, x, norm_g, dout, f"{tag}_dx")
    small = dict(norm=dg[0], ln_g=dlg[0], ln_b=dlb[0], w_s=dws, b_s=dbs)
    return dx, dxb, dw_in, dw_out, small


def _square(r):
    return r * r


def mlp_fwd(x, h, w1, w2, next_g, tag):
    (r,) = matmul(h, w1, "nn", f"{tag}_up", [BF16], epilogue=lambda acc: (jnp.maximum(acc, 0.0),), tm=2048)
    y, h_next = _residual_out(r, w2, x, next_g, f"{tag}_down", a_fn=_square, tm=512, tk=4096)
    return y, h_next, (x, h, r)


def mlp_bwd(dy, dyb, saved, norm_g, w1, w2, tag):
    x, h, r = saved
    (da,) = matmul(dyb, w2, "nt", f"{tag}_da", [BF16], epilogue=lambda acc, rr: (acc * (2.0 * rr.astype(F32)),), extras=(r,),
                   tm=2048)
    (dw2,) = matmul(r, dyb, "tn", f"{tag}_dw2", [BF16], a_fn=_square, tk=4096)
    (dw1,) = matmul(h, da, "tn", f"{tag}_dw1", [BF16], out_shards=True, tk=4096)
    dx, dxb, dg = matmul_nt_normbwd(da, w1, x, norm_g, dy, f"{tag}_dx")
    return dx, dxb, dw1, dw2, dg[0]


ORDER = ("att_norm", "att_w_in", "att_sink", "att_qnorm", "att_knorm", "att_w_out", "sgu_norm", "sgu_w_in", "sgu_ln_g",
         "sgu_ln_b", "sgu_w_s", "sgu_b_s", "sgu_w_out", "mlp_norm", "mlp_w1", "mlp_w2", "final_norm")
SHARDED = ("att_w_in", "att_w_out", "sgu_w_in", "sgu_w_out", "mlp_w1", "mlp_w2")
SGU_VECS = ("sgu_norm", "sgu_ln_g", "sgu_ln_b")
SMALL_EARLY = ("sgu_w_s", "sgu_b_s", "mlp_norm", "final_norm", "loss")
SMALL_LATE = ("att_norm", "att_sink", "att_qnorm", "att_knorm")
SMALL_ROWS_MULT = 8


def _flat(blocks, names):
    flat = jnp.concatenate([blocks[n].reshape(-1).astype(F32) for n in names])
    per = SMALL_ROWS_MULT * FLAT_COLS
    total = -(-flat.shape[0] // per) * per
    return jnp.pad(flat, (0, total - flat.shape[0])).reshape(1, total // FLAT_COLS, FLAT_COLS)


def _unflat(flat, like, names):
    out, off = {}, 0
    f = flat.reshape(-1)
    for n in names:
        size = like[n].size
        out[n] = f[off:off + size].reshape(like[n].shape)
        off += size
    return out


def kernel(x, att_norm, att_w_in, att_sink, att_qnorm, att_knorm, att_w_out, sgu_norm, sgu_w_in, sgu_ln_g, sgu_ln_b, sgu_w_s, sgu_b_s, sgu_w_out, mlp_norm, mlp_w1, mlp_w2, final_norm, loss_target, m_att_norm, m_att_w_in, m_att_sink, m_att_qnorm, m_att_knorm, m_att_w_out, m_sgu_norm, m_sgu_w_in, m_sgu_ln_g, m_sgu_ln_b, m_sgu_w_s, m_sgu_b_s, m_sgu_w_out, m_mlp_norm, m_mlp_w1, m_mlp_w2, m_final_norm, v_att_norm, v_att_w_in, v_att_sink, v_att_qnorm, v_att_knorm, v_att_w_out, v_sgu_norm, v_sgu_w_in, v_sgu_ln_g, v_sgu_ln_b, v_sgu_w_s, v_sgu_b_s, v_sgu_w_out, v_mlp_norm, v_mlp_w1, v_mlp_w2, v_final_norm):
    w = dict(att_norm=att_norm, att_w_in=att_w_in, att_sink=att_sink, att_qnorm=att_qnorm, att_knorm=att_knorm,
             att_w_out=att_w_out, sgu_norm=sgu_norm, sgu_w_in=sgu_w_in, sgu_ln_g=sgu_ln_g, sgu_ln_b=sgu_ln_b, sgu_w_s=sgu_w_s,
             sgu_b_s=sgu_b_s, sgu_w_out=sgu_w_out, mlp_norm=mlp_norm, mlp_w1=mlp_w1, mlp_w2=mlp_w2, final_norm=final_norm)
    m = dict(att_norm=m_att_norm, att_w_in=m_att_w_in, att_sink=m_att_sink, att_qnorm=m_att_qnorm, att_knorm=m_att_knorm,
             att_w_out=m_att_w_out, sgu_norm=m_sgu_norm, sgu_w_in=m_sgu_w_in, sgu_ln_g=m_sgu_ln_g, sgu_ln_b=m_sgu_ln_b,
             sgu_w_s=m_sgu_w_s, sgu_b_s=m_sgu_b_s, sgu_w_out=m_sgu_w_out, mlp_norm=m_mlp_norm, mlp_w1=m_mlp_w1, mlp_w2=m_mlp_w2,
             final_norm=m_final_norm)
    v = dict(att_norm=v_att_norm, att_w_in=v_att_w_in, att_sink=v_att_sink, att_qnorm=v_att_qnorm, att_knorm=v_att_knorm,
             att_w_out=v_att_w_out, sgu_norm=v_sgu_norm, sgu_w_in=v_sgu_w_in, sgu_ln_g=v_sgu_ln_g, sgu_ln_b=v_sgu_ln_b,
             sgu_w_s=v_sgu_w_s, sgu_b_s=v_sgu_b_s, sgu_w_out=v_sgu_w_out, mlp_norm=v_mlp_norm, mlp_w1=v_mlp_w1, mlp_w2=v_mlp_w2,
             final_norm=v_final_norm)
    loss, grad_x, g, d, nm, nv = train_step(x[0], loss_target[0], w, m, v)
    return (loss, grad_x[None], *[g[n] for n in ORDER], *[d[n] for n in ORDER], *[nm[n] for n in ORDER], *[nv[n] for n in ORDER])


def train_step(x, target, w, m, v):
    t, d_model = x.shape
    n_att, n_sgu, depth = w["att_w_in"].shape[0], w["sgu_w_in"].shape[0], w["mlp_w1"].shape[0]
    bf = lambda n: w[n].astype(BF16)

    assert n_sgu == n_att and depth == 2 * n_att
    vec_local = jnp.stack([w[n] for n in SGU_VECS], axis=1)
    att_in = bf("att_w_in")
    g_in0, g_vec = exchange_only(Exchange(gathers=[att_in[:1], vec_local]), "gather_first")
    vecs = g_vec.transpose(1, 2, 0, 3).reshape(n_sgu, len(SGU_VECS), -1)
    group_names = ("att_w_out", "sgu_w_in", "sgu_w_out", "mlp_w1", "mlp_w2")
    per = {n: w[n].shape[0] // n_att for n in group_names}
    rest = [Exchange(gathers=([att_in[1:]] if gi == 0 else []) + [bf(n)[gi * per[n]:(gi + 1) * per[n]] for n in group_names])
            for gi in range(n_att)]
    gathered = {}

    def weight(name, kind, layer):
        return Gathered(gathered[name][layer // per[name]], kind, layer % per[name])

    w_s_bf = w["sgu_w_s"].astype(BF16)
    b_rows = jnp.broadcast_to(w["sgu_b_s"][:, :, :, None], w["sgu_b_s"].shape + (LANES,))
    tables = _rope_tables(t)
    full_cols = lambda g: g.transpose(1, 2, 0, 3).reshape(g.shape[1], d_model, -1)

    mixer_norm = lambda layer: w["att_norm"][layer // 2] if layer % 2 == 0 else vecs[layer // 2, 0]
    saved = []
    h = rmsnorm_fwd(x, mixer_norm(0), "att0_norm")
    for layer in range(depth):
        i = layer // 2
        if layer % 2 == 0:
            if layer == 0:
                att_w_in = [full_cols(g_in0)[0]]
            first = 1 if layer == 0 else 0
            x, h, sv, arrived = attention_fwd(x, h, att_w_in[i], w["att_sink"][i], w["att_qnorm"][i], w["att_knorm"][i],
                                              lambda arrived, first=first: Gathered(arrived[first], "row", 0),
                                              tables, w["mlp_norm"][layer], f"att{i}", rest[i])
            if layer == 0:
                att_w_in += list(full_cols(arrived[0]))
            for n, g in zip(group_names, arrived[first:]):
                gathered.setdefault(n, []).append(g)
        else:
            x, h, sv = sgu_fwd(x, h, weight("sgu_w_in", "col", i), vecs[i, 1], vecs[i, 2], w_s_bf[i], b_rows[i],
                               weight("sgu_w_out", "row", i), w["mlp_norm"][layer], f"sgu{i}")
        x, h, sm = mlp_fwd(x, h, weight("mlp_w1", "col", layer), weight("mlp_w2", "row", layer),
                           mixer_norm(layer + 1) if layer + 1 < depth else None, f"mlp{layer}")
        saved.append((sv, sm))
    loss_row, dh, dhb, dgf = loss_head(x, w["final_norm"], target, "loss_head")

    queue, recv = [], {}
    gs = dict(att_norm=[None] * n_att, att_sink=[None] * n_att, att_qnorm=[None] * n_att, att_knorm=[None] * n_att,
              sgu_w_s=[None] * n_sgu, sgu_b_s=[None] * n_sgu, mlp_norm=[None] * depth)

    def row_slabs(g):
        return g.reshape(N_DEV, g.shape[0] // N_DEV, g.shape[1])

    def col_slabs(g):
        return g.reshape(g.shape[0], N_DEV, g.shape[1] // N_DEV).transpose(1, 0, 2)

    def take_queue(gathers=()):
        items = list(queue)
        queue.clear()
        keys = [k for k, _ in gathers] + [k for k, _ in items]
        return Exchange(gathers=[a for _, a in gathers], scatters=[a for _, a in items]), keys

    def small_early():
        blocks = dict(sgu_w_s=jnp.stack(gs["sgu_w_s"]), sgu_b_s=jnp.stack(gs["sgu_b_s"]), mlp_norm=jnp.stack(gs["mlp_norm"]),
                      final_norm=dgf[0], loss=loss_row[0, :1])
        return _flat(blocks, SMALL_EARLY)[0]

    for layer in reversed(range(depth)):
        i = layer // 2
        sv, sm = saved[layer]
        dh, dhb, dw1, dw2, gs["mlp_norm"][layer] = mlp_bwd(
            dh, dhb, sm, w["mlp_norm"][layer], weight("mlp_w1", "col", layer), weight("mlp_w2", "row", layer), f"mlp{layer}")
        queue += [(("mlp_w1", layer), dw1), (("mlp_w2", layer), row_slabs(dw2))]
        if layer % 2 == 0:
            keys = []

            def exchange_with(dw_out, i=i, layer=layer, keys=keys):
                if layer == 0:
                    queue.append((("att_w_out", i), row_slabs(dw_out)))
                ex, got = take_queue([("small_early", small_early())] if layer == 0 else ())
                keys += got
                return ex

            dh, dhb, dw_in, dw_out, sm_g, arrived = attention_bwd(
                dh, dhb, sv, w["att_norm"][i], att_w_in[i], w["att_sink"][i], w["att_qnorm"][i], w["att_knorm"][i],
                weight("att_w_out", "row", i), tables, f"att{i}", exchange_with)
            recv.update(zip(keys, arrived))
            queue.append((("att_w_in", i), col_slabs(dw_in)))
            if layer != 0:
                queue.append((("att_w_out", i), row_slabs(dw_out)))
            gs["att_norm"][i], gs["att_sink"][i] = sm_g["norm"], sm_g["sink"]
            gs["att_qnorm"][i], gs["att_knorm"][i] = sm_g["qnorm"], sm_g["knorm"]
        else:
            dh, dhb, dw_in, dw_out, sm_g = sgu_bwd(
                dh, dhb, sv, vecs[i, 0], weight("sgu_w_in", "col", i), vecs[i, 1], vecs[i, 2], w_s_bf[i], b_rows[i],
                weight("sgu_w_out", "row", i), f"sgu{i}")
            dvec = jnp.stack([sm_g["norm"], sm_g["ln_g"], sm_g["ln_b"]])
            queue += [(("sgu_w_in", i), dw_in), (("sgu_w_out", i), row_slabs(dw_out)), (("sgu_vecs", i), col_slabs(dvec))]
            gs["sgu_w_s"][i], gs["sgu_b_s"][i] = sm_g["w_s"], sm_g["b_s"]
    grad_x = dh
    late = dict(att_norm=jnp.stack(gs["att_norm"]), att_sink=jnp.stack(gs["att_sink"]), att_qnorm=jnp.stack(gs["att_qnorm"]),
                att_knorm=jnp.stack(gs["att_knorm"]))
    last, keys = take_queue([("small_late", _flat(late, SMALL_LATE)[0])])
    recv.update(zip(keys, exchange_only(last, "exchange_last")))

    outs = [{}, {}, {}, {}]
    for n in SHARDED:
        res = adamw([recv[(n, l)] for l in range(w[n].shape[0])], w[n], m[n], v[n], f"adamw_{n}")
        for o, r in zip(outs, res):
            o[n] = r
    stack_vecs = lambda src: jnp.stack([src[n] for n in SGU_VECS], axis=1)
    res = adamw([recv[("sgu_vecs", i)] for i in range(n_sgu)], stack_vecs(w), stack_vecs(m), stack_vecs(v), "adamw_sgu_vecs")
    for o, r in zip(outs, res):
        o.update({n: r[:, k] for k, n in enumerate(SGU_VECS)})
    zero = {"loss": jnp.zeros((1,), F32)}
    for names, key in ((SMALL_EARLY, "small_early"), (SMALL_LATE, "small_late")):
        res = adamw([recv[key]], _flat({**w, **zero}, names), _flat({**m, **zero}, names), _flat({**v, **zero}, names), f"adamw_{key}")
        for o, r in zip(outs, res):
            o.update(_unflat(r, {**w, **zero}, names))
    loss = outs[0]["loss"][0]
    return loss, grad_x, *outs
```

```python
import math

import jax
import jax.numpy as jnp
from jax import lax
from jax.experimental import pallas as pl
from jax.experimental.pallas import tpu as pltpu

F32 = jnp.float32
BF16 = jnp.bfloat16

HEAD_DIM = 64
A_HEADS = 8
A_KV_HEADS = 2
B_HEADS = 8
B_KV_HEADS = 2
WINDOW = 128
BLOCK = 128
ROPE_THETA = 10000.0
GRID_W = 64
SGU_GROUPS = 8
SGU_CHUNK = 128
EPS = 1e-6
SCALE = HEAD_DIM ** -0.5
NEG = -1e30
LOG2E = math.log2(math.e)
LN2 = math.log(2.0)

A_Q = A_HEADS * HEAD_DIM
A_KV = A_KV_HEADS * HEAD_DIM
B_Q = B_HEADS * HEAD_DIM
B_KV = B_KV_HEADS * HEAD_DIM
OFF_QA, OFF_KA, OFF_VA = 0, A_Q, A_Q + A_KV
OFF_QB = A_Q + 2 * A_KV
OFF_KB = OFF_QB + B_Q
OFF_VB = OFF_KB + B_KV
ATT_IN = OFF_VB + B_KV

ADAM_LR = 0.001
ADAM_B1 = 0.9
ADAM_B2 = 0.999
ADAM_EPS = 1e-08
ADAM_WD = 0.01
ADAM_STEP = 10

N_DEV = 8
LANES = 128
V7X_VMEM_LIMIT = 56 * 1024 * 1024
FLAT_COLS = 1024


def _cparams(sem, vmem=V7X_VMEM_LIMIT):
    return pltpu.CompilerParams(dimension_semantics=sem, vmem_limit_bytes=vmem)


def _dot_nn(a, b):
    return lax.dot_general(a, b, (((1,), (0,)), ((), ())), preferred_element_type=F32)


def _dot_nt(a, b):
    return lax.dot_general(a, b, (((1,), (1,)), ((), ())), preferred_element_type=F32)


def _dot_tn(a, b):
    return lax.dot_general(a, b, (((0,), (0,)), ((), ())), preferred_element_type=F32)


def _bf(x):
    return x if x.dtype == BF16 else x.astype(BF16)


def _lane(shape):
    return lax.broadcasted_iota(jnp.int32, shape, len(shape) - 1)


def _seg_matrix(rows_lo, rows_hi):
    r = lax.broadcasted_iota(jnp.int32, (LANES, LANES), 0)
    return jnp.where((r >= rows_lo) & (r < rows_hi), 1.0, 0.0).astype(BF16)


def _group_matrix(width):
    r = lax.broadcasted_iota(jnp.int32, (LANES, LANES), 0)
    c = lax.broadcasted_iota(jnp.int32, (LANES, LANES), 1)
    return jnp.where((r // width) == (c // width), 1.0, 0.0).astype(BF16)


def _dot_f32_by_ones(s, ones_bf16):
    hi = s.astype(BF16)
    lo = (s - hi.astype(F32)).astype(BF16)
    return _dot_nn(hi, ones_bf16) + _dot_nn(lo, ones_bf16)


def _swap_halves(x, width):
    half = width // 2
    first = (_lane(x.shape) % width) < half
    return jnp.where(first, pltpu.roll(x, LANES - half, 1), pltpu.roll(x, half, 1))


def rmsnorm_fwd(x, g, name, exchange=None):
    t, d = x.shape
    tm = min(t, 512)
    n = t // tm

    def body(x_ref, g_ref, h_ref):
        xf = x_ref[...]
        r = lax.rsqrt(jnp.mean(xf * xf, axis=-1, keepdims=True) + EPS)
        h_ref[...] = (xf * r * g_ref[...]).astype(BF16)

    body, x_in, x_out, x_shapes, x_scratch = carried(
        body, exchange, 2, 1, lambda: (pl.program_id(0) == 0, pl.program_id(0) == n - 1))
    return pl.pallas_call(
        body, name=name, grid=(n,),
        in_specs=[pl.BlockSpec((tm, d), lambda i: (i, 0)), pl.BlockSpec((1, d), lambda i: (0, 0))] + x_in,
        out_specs=[pl.BlockSpec((tm, d), lambda i: (i, 0))] + x_out,
        out_shape=[jax.ShapeDtypeStruct((t, d), BF16)] + x_shapes,
        scratch_shapes=x_scratch,
        compiler_params=_cparams(("arbitrary",)),
    )(x, g.reshape(1, d), *(exchange.arrays if exchange else ()))


def _fit(n, want):
    t = min(n, want)
    while n % t:
        t //= 2
    return t


class Gathered:
    def __init__(self, arr, kind, layer):
        self.arr, self.kind, self.layer = arr, kind, layer
        _, _, self.rows, self.cols = arr.shape
        self.shape = (N_DEV * self.rows, self.cols) if kind == "row" else (self.rows, N_DEV * self.cols)


def _b_operand(b, mode, tn, tk, idx):
    dot = {"nn": _dot_nn, "nt": _dot_nt, "tn": _dot_tn}[mode]
    if not isinstance(b, Gathered):
        if mode == "nt":
            spec = pl.BlockSpec((tn, tk), lambda *g: idx(*g))
        else:
            spec = pl.BlockSpec((tk, tn), lambda *g: idx(*g)[::-1])
        return b, spec, lambda av, ref: dot(av, _bf(ref[...]))
    lay, rows, cols = b.layer, b.rows, b.cols
    if mode == "nn" and b.kind == "col":
        s = tn // cols
        assert s * cols == tn
        spec = pl.BlockSpec((s, None, tk, cols), lambda *g: (idx(*g)[0], lay, idx(*g)[1], 0))
        return b.arr, spec, lambda av, ref: jnp.concatenate([_dot_nn(av, ref[c]) for c in range(s)], axis=1)
    if mode == "nn" and b.kind == "row":
        s = tk // rows
        assert s * rows == tk
        spec = pl.BlockSpec((s, None, rows, tn), lambda *g: (idx(*g)[1], lay, 0, idx(*g)[0]))
        return b.arr, spec, lambda av, ref: _dot_nn(av, ref[...].reshape(s * rows, tn))
    if mode == "nt" and b.kind == "row":
        s = tn // rows
        assert s * rows == tn
        spec = pl.BlockSpec((s, None, rows, tk), lambda *g: (idx(*g)[0], lay, 0, idx(*g)[1]))
        return b.arr, spec, lambda av, ref: _dot_nt(av, ref[...].reshape(s * rows, tk))
    if mode == "nt" and b.kind == "col":
        s = tk // cols
        assert s * cols == tk
        spec = pl.BlockSpec((s, None, tn, cols), lambda *g: (idx(*g)[1], lay, idx(*g)[0], 0))

        def prod(av, ref):
            tot = _dot_nt(av[:, :cols], ref[0])
            for c in range(1, s):
                tot = tot + _dot_nt(av[:, c * cols:(c + 1) * cols], ref[c])
            return tot

        return b.arr, spec, prod
    raise NotImplementedError((mode, b.kind))


def matmul(a, b, mode, name, out_dtypes, epilogue=None, extras=(), a_fn=None, out_shards=False, tm=1024, tn=1024, tk=1024):
    (m, k) = a.shape[::-1] if mode == "tn" else a.shape
    n = b.shape[0] if mode == "nt" else b.shape[1]
    if out_shards:
        tn = n // N_DEV
    tm, tn, tk = _fit(m, tm), _fit(n, tn), _fit(k, tk)
    nk = k // tk
    n_ex, n_out = len(extras), len(out_dtypes)
    if epilogue is None:
        epilogue = lambda acc: (acc,)
    b_arr, b_spec, prod = _b_operand(b, mode, tn, tk, lambda i, j, kk: (j, kk))

    def body(*refs):
        a_ref, b_ref = refs[0], refs[1]
        ex_refs = refs[2:2 + n_ex]
        out_refs = refs[2 + n_ex:2 + n_ex + n_out]
        acc_ref = refs[2 + n_ex + n_out] if nk > 1 else None
        kk = pl.program_id(2)
        av = _bf(a_ref[...])
        if a_fn is not None:
            av = a_fn(av)
        part = prod(av, b_ref)

        def finish(acc):
            outs = epilogue(acc, *[r[...] for r in ex_refs])
            for r, o in zip(out_refs, outs):
                r[...] = o.astype(r.dtype)

        if nk == 1:
            finish(part)
            return

        @pl.when(kk == 0)
        def _():
            acc_ref[...] = part

        @pl.when(kk > 0)
        def _():
            acc_ref[...] += part

        @pl.when(kk == nk - 1)
        def _():
            finish(acc_ref[...])

    if mode == "tn":
        a_spec = pl.BlockSpec((tk, tm), lambda i, j, kk: (kk, i))
    else:
        a_spec = pl.BlockSpec((tm, tk), lambda i, j, kk: (i, kk))
    mn_spec = pl.BlockSpec((tm, tn), lambda i, j, kk: (i, j))
    row_spec = pl.BlockSpec((1, tn), lambda i, j, kk: (0, j))
    if out_shards:
        out_spec = pl.BlockSpec((None, tm, tn), lambda i, j, kk: (j, i, 0))
        out_shape = [jax.ShapeDtypeStruct((N_DEV, m, tn), dt) for dt in out_dtypes]
    else:
        out_spec = mn_spec
        out_shape = [jax.ShapeDtypeStruct((m, n), dt) for dt in out_dtypes]
    outs = pl.pallas_call(
        body, name=name, grid=(m // tm, n // tn, nk),
        in_specs=[a_spec, b_spec] + [row_spec if e.shape[0] == 1 else mn_spec for e in extras],
        out_specs=[out_spec] * n_out,
        out_shape=out_shape,
        scratch_shapes=[pltpu.VMEM((tm, tn), F32)] if nk > 1 else [],
        compiler_params=_cparams(("parallel", "parallel", "arbitrary")),
    )(a, b_arr, *extras)
    return outs


def matmul_nt_normbwd(dz, w, x, g, dres, name, tm=512):
    m, k = dz.shape
    d = w.shape[0]
    tm = _fit(m, tm)
    w_arr, w_spec, prod = _b_operand(w, "nt", d, k, lambda i: (0, 0))

    def body(dz_ref, w_ref, x_ref, g_ref, dres_ref, dx_ref, dxb_ref, dg_ref):
        @pl.when(pl.program_id(0) == 0)
        def _():
            dg_ref[...] = jnp.zeros_like(dg_ref)

        dh = prod(_bf(dz_ref[...]), w_ref)
        xf = x_ref[...]
        r = lax.rsqrt(jnp.mean(xf * xf, axis=-1, keepdims=True) + EPS)
        xhat = xf * r
        dg_ref[...] += jnp.sum(dh * xhat, axis=0, keepdims=True)
        dxh = dh * g_ref[...]
        dx = r * (dxh - xhat * jnp.mean(dxh * xhat, axis=-1, keepdims=True))
        out = dres_ref[...] + dx
        dx_ref[...] = out
        dxb_ref[...] = out.astype(BF16)

    row = pl.BlockSpec((tm, d), lambda i: (i, 0))
    vec = pl.BlockSpec((1, d), lambda i: (0, 0))
    return pl.pallas_call(
        body, name=name, grid=(m // tm,),
        in_specs=[pl.BlockSpec((tm, k), lambda i: (i, 0)), w_spec, row, vec, row],
        out_specs=[row, row, vec],
        out_shape=[jax.ShapeDtypeStruct((m, d), F32), jax.ShapeDtypeStruct((m, d), BF16), jax.ShapeDtypeStruct((1, d), F32)],
        compiler_params=_cparams(("arbitrary",)),
    )(dz, w_arr, x, g.reshape(1, d), dres)


def _rope_tables(t):
    pos = lax.broadcasted_iota(jnp.int32, (t, LANES), 0)
    dim = lax.broadcasted_iota(jnp.int32, (t, LANES), 1) % HEAD_DIM

    def table(p, width):
        i = dim % (width // 2)
        ang = p.astype(F32) * (ROPE_THETA ** (-(2 * i).astype(F32) / width))
        return jnp.cos(ang), jnp.where(dim % width < width // 2, -jnp.sin(ang), jnp.sin(ang))

    cos_a, sin_a = table(pos, HEAD_DIM)
    cos_b, sin_b = table(jnp.where(dim < HEAD_DIM // 2, pos // GRID_W, pos % GRID_W), HEAD_DIM // 2)
    return cos_a, sin_a, cos_b, sin_b


def _headnorm(xs, gmat):
    return lax.rsqrt(_dot_f32_by_ones(xs * xs, gmat) * (1.0 / HEAD_DIM) + EPS)


def qkv_post_fwd(proj, tables, qn_g, kn_g, name):
    t = proj.shape[0]
    tm = min(t, 256)
    cos_a, sin_a, cos_b, sin_b = tables
    g2 = lambda g: jnp.concatenate([g, g]).reshape(1, LANES)

    def body(p_ref, ca_ref, sa_ref, cb_ref, sb_ref, qg_ref, kg_ref, qa_ref, ka_ref, va_ref, qb_ref, kb_ref, vb_ref):
        ca, sa, cb, sb = ca_ref[...], sa_ref[...], cb_ref[...], sb_ref[...]
        gmat = _group_matrix(HEAD_DIM)

        def rope_a(xs):
            return xs * ca + _swap_halves(xs, HEAD_DIM) * sa

        def norm_rope_b(xs, g):
            y = xs * _headnorm(xs, gmat) * g
            return y * cb + _swap_halves(y, HEAD_DIM // 2) * sb

        for c in range(A_Q // LANES):
            qa_ref[:, c * LANES:(c + 1) * LANES] = rope_a(p_ref[:, OFF_QA + c * LANES:OFF_QA + (c + 1) * LANES]).astype(BF16)
        ka_ref[...] = rope_a(p_ref[:, OFF_KA:OFF_KA + LANES]).astype(BF16)
        va_ref[...] = p_ref[:, OFF_VA:OFF_VA + LANES].astype(BF16)
        for c in range(B_Q // LANES):
            qb_ref[:, c * LANES:(c + 1) * LANES] = norm_rope_b(
                p_ref[:, OFF_QB + c * LANES:OFF_QB + (c + 1) * LANES], qg_ref[...]).astype(BF16)
        kb_ref[...] = norm_rope_b(p_ref[:, OFF_KB:OFF_KB + LANES], kg_ref[...]).astype(BF16)
        vb_ref[...] = p_ref[:, OFF_VB:OFF_VB + LANES].astype(BF16)

    tab = pl.BlockSpec((tm, LANES), lambda i: (i, 0))
    vec = pl.BlockSpec((1, LANES), lambda i: (0, 0))
    wide = pl.BlockSpec((tm, A_Q), lambda i: (i, 0))
    return pl.pallas_call(
        body, name=name, grid=(t // tm,),
        in_specs=[pl.BlockSpec((tm, ATT_IN), lambda i: (i, 0)), tab, tab, tab, tab, vec, vec],
        out_specs=[wide, tab, tab, wide, tab, tab],
        out_shape=[jax.ShapeDtypeStruct((t, A_Q), BF16), jax.ShapeDtypeStruct((t, LANES), BF16),
                   jax.ShapeDtypeStruct((t, LANES), BF16), jax.ShapeDtypeStruct((t, B_Q), BF16),
                   jax.ShapeDtypeStruct((t, LANES), BF16), jax.ShapeDtypeStruct((t, LANES), BF16)],
        compiler_params=_cparams(("parallel",)),
    )(proj, cos_a, sin_a, cos_b, sin_b, g2(qn_g), g2(kn_g))


def qkv_post_bwd(proj, tables, qn_g, kn_g, dqa, dka, dva, dqb, dkb, dvb, name):
    t = proj.shape[0]
    tm = min(t, 256)
    cos_a, sin_a, cos_b, sin_b = tables
    g2 = lambda g: jnp.concatenate([g, g]).reshape(1, LANES)

    def body(p_ref, ca_ref, sa_ref, cb_ref, sb_ref, qg_ref, kg_ref, dqa_ref, dka_ref, dva_ref, dqb_ref, dkb_ref, dvb_ref,
             dp_ref, dqg_ref, dkg_ref):
        ca, sa, cb, sb = ca_ref[...], sa_ref[...], cb_ref[...], sb_ref[...]
        gmat = _group_matrix(HEAD_DIM)

        @pl.when(pl.program_id(0) == 0)
        def _():
            dqg_ref[...] = jnp.zeros_like(dqg_ref)
            dkg_ref[...] = jnp.zeros_like(dkg_ref)

        def rope_a_bwd(dy):
            return dy * ca + _swap_halves(dy * sa, HEAD_DIM)

        def norm_rope_b_bwd(dout, xs, g):
            dy = dout * cb + _swap_halves(dout * sb, HEAD_DIM // 2)
            r = _headnorm(xs, gmat)
            xhat = xs * r
            dxh = dy * g
            mean = _dot_f32_by_ones(dxh * xhat, gmat) * (1.0 / HEAD_DIM)
            return r * (dxh - xhat * mean), jnp.sum(dy * xhat, axis=0, keepdims=True)

        for c in range(A_Q // LANES):
            sl = slice(c * LANES, (c + 1) * LANES)
            dp_ref[:, OFF_QA + c * LANES:OFF_QA + (c + 1) * LANES] = rope_a_bwd(dqa_ref[:, sl].astype(F32)).astype(BF16)
        dp_ref[:, OFF_KA:OFF_KA + LANES] = rope_a_bwd(dka_ref[0] + dka_ref[1]).astype(BF16)
        dp_ref[:, OFF_VA:OFF_VA + LANES] = (dva_ref[0] + dva_ref[1]).astype(BF16)
        dqg = jnp.zeros((1, LANES), F32)
        for c in range(B_Q // LANES):
            sl = slice(c * LANES, (c + 1) * LANES)
            dx, dg = norm_rope_b_bwd(dqb_ref[:, sl].astype(F32), p_ref[:, OFF_QB + c * LANES:OFF_QB + (c + 1) * LANES], qg_ref[...])
            dp_ref[:, OFF_QB + c * LANES:OFF_QB + (c + 1) * LANES] = dx.astype(BF16)
            dqg = dqg + dg
        dqg_ref[...] += dqg
        dx, dg = norm_rope_b_bwd((dkb_ref[0] + dkb_ref[1]).T, p_ref[:, OFF_KB:OFF_KB + LANES], kg_ref[...])
        dp_ref[:, OFF_KB:OFF_KB + LANES] = dx.astype(BF16)
        dkg_ref[...] += dg
        dp_ref[:, OFF_VB:OFF_VB + LANES] = (dvb_ref[0] + dvb_ref[1]).T.astype(BF16)

        @pl.when(pl.program_id(0) == t // tm - 1)
        def _():
            dqg_ref[...] = dqg_ref[...] + pltpu.roll(dqg_ref[...], HEAD_DIM, 1)
            dkg_ref[...] = dkg_ref[...] + pltpu.roll(dkg_ref[...], HEAD_DIM, 1)

    tab = pl.BlockSpec((tm, LANES), lambda i: (i, 0))
    vec = pl.BlockSpec((1, LANES), lambda i: (0, 0))
    wide = pl.BlockSpec((tm, A_Q), lambda i: (i, 0))
    slab = pl.BlockSpec((2, tm, LANES), lambda i: (0, i, 0))
    per_chunk = dkb.shape[3] // tm
    slab_t = pl.BlockSpec((2, None, LANES, tm), lambda i: (0, i // per_chunk, 0, i % per_chunk))
    full = pl.BlockSpec((tm, ATT_IN), lambda i: (i, 0))
    return pl.pallas_call(
        body, name=name, grid=(t // tm,),
        in_specs=[full, tab, tab, tab, tab, vec, vec, wide, slab, slab, wide, slab_t, slab_t],
        out_specs=[full, vec, vec],
        out_shape=[jax.ShapeDtypeStruct((t, ATT_IN), BF16), jax.ShapeDtypeStruct((1, LANES), F32),
                   jax.ShapeDtypeStruct((1, LANES), F32)],
        compiler_params=_cparams(("arbitrary",)),
    )(proj, cos_a, sin_a, cos_b, sin_b, g2(qn_g), g2(kn_g), dqa, dka, dva, dqb, dkb, dvb)


def _head_to_half(xs, head_half, kv_half):
    low = _lane(xs.shape) < HEAD_DIM
    kept = jnp.where(low if head_half == 0 else jnp.logical_not(low), xs, 0.0)
    return jnp.where(kv_half == head_half, kept, pltpu.roll(kept, HEAD_DIM, 1))


def _halves_to_heads(r0, r1, kv_half):
    low = _lane(r0.shape) < HEAD_DIM
    a = jnp.where(kv_half == 0, r0, pltpu.roll(r0, HEAD_DIM, 1))
    b = jnp.where(kv_half == 1, r1, pltpu.roll(r1, HEAD_DIM, 1))
    return jnp.where(low, a, b)


def attn_delta(o, do, name):
    t, w = o.shape
    tm = min(t, 512)
    n_heads = w // HEAD_DIM

    def body(o_ref, do_ref, d_ref):
        lo, hi = _seg_matrix(0, HEAD_DIM), _seg_matrix(HEAD_DIM, LANES)
        for c in range(w // LANES):
            sl = slice(c * LANES, (c + 1) * LANES)
            s = o_ref[:, sl].astype(F32) * do_ref[:, sl].astype(F32)
            d_ref[2 * c] = _dot_f32_by_ones(s, lo)
            d_ref[2 * c + 1] = _dot_f32_by_ones(s, hi)

    blk = pl.BlockSpec((tm, w), lambda i: (i, 0))
    return pl.pallas_call(
        body, name=name, grid=(t // tm,),
        in_specs=[blk, blk],
        out_specs=pl.BlockSpec((n_heads, tm, LANES), lambda i: (0, i, 0)),
        out_shape=jax.ShapeDtypeStruct((n_heads, t, LANES), F32),
        compiler_params=_cparams(("parallel",)),
    )(o, do)


BAND = 3 * BLOCK


def _band_offsets(rows_rep):
    qi = lax.broadcasted_iota(jnp.int32, (BLOCK, BAND), 0)
    kj = lax.broadcasted_iota(jnp.int32, (BLOCK, BAND), 1)
    return jnp.concatenate([kj - qi] * rows_rep, axis=0)


def _band(n, t, offsets):
    start = pl.multiple_of(jnp.clip((n - 1) * BLOCK, 0, t - BAND), BLOCK)
    return start, jnp.abs(offsets + (start - n * BLOCK)) <= WINDOW


def window_attn_fwd(q, k, v, sink, name, blocks_per_step=8):
    t = q.shape[0]
    assert t >= BAND
    nq = _fit(t // BLOCK, blocks_per_step)
    tq = nq * BLOCK

    def body(sink_ref, q_ref, k_ref, v_ref, o_ref, lse_ref):
        j, n0 = pl.program_id(0), pl.program_id(1)
        kvh = j // 2
        row = lax.broadcasted_iota(jnp.int32, (2 * BLOCK, 1), 0)
        sk = jnp.where(row < BLOCK, sink_ref[2 * j], sink_ref[2 * j + 1]) * LOG2E
        offsets = _band_offsets(2)
        bands, scores = [], []
        for u in range(nq):
            start, ok = _band(n0 * nq + u, t, offsets)
            qf = q_ref[u * BLOCK:(u + 1) * BLOCK, :].astype(F32) * (SCALE * LOG2E)
            qs = jnp.concatenate([_head_to_half(qf, 0, kvh), _head_to_half(qf, 1, kvh)], axis=0).astype(BF16)
            bands.append(pl.ds(start, BAND))
            scores.append(jnp.where(ok, _dot_nt(qs, k_ref[bands[u], :]), NEG))
        soft = []
        for s in scores:
            m = jnp.maximum(jnp.max(s, axis=-1, keepdims=True), sk)
            p = jnp.exp2(s - m)
            soft.append((p.astype(BF16), jnp.sum(p, axis=-1, keepdims=True) + jnp.exp2(sk - m), m))
        for u, (p, denom, m) in enumerate(soft):
            rows = slice(u * BLOCK, (u + 1) * BLOCK)
            o = _dot_nn(p, v_ref[bands[u], :]) / denom
            o_ref[rows, :] = _halves_to_heads(o[:BLOCK], o[BLOCK:], kvh).astype(BF16)
            lse = jnp.broadcast_to(m + jnp.log2(denom), (2 * BLOCK, LANES))
            lse_ref[0, rows, :] = lse[:BLOCK]
            lse_ref[1, rows, :] = lse[BLOCK:]

    qspec = pl.BlockSpec((tq, LANES), lambda j, n: (n, j))
    whole = pl.BlockSpec((t, LANES), lambda j, n: (0, 0))
    return pl.pallas_call(
        body, name=name, grid=(A_HEADS // 2, t // tq),
        in_specs=[pl.BlockSpec(memory_space=pltpu.SMEM), qspec, whole, whole],
        out_specs=[qspec, pl.BlockSpec((2, tq, LANES), lambda j, n: (j, n, 0))],
        out_shape=[jax.ShapeDtypeStruct((t, A_Q + B_Q), BF16), jax.ShapeDtypeStruct((A_HEADS, t, LANES), F32)],
        compiler_params=_cparams(("parallel", "parallel")),
    )(sink, q, k, v)


def window_attn_bwd(q, k, v, sink, do, lse, delta, name, blocks_per_step=4):
    t = q.shape[0]
    assert t >= BAND
    nq = _fit(t // BLOCK, blocks_per_step)
    tq = nq * BLOCK
    grp = A_HEADS // A_KV_HEADS
    gw = grp * HEAD_DIM

    def body(sink_ref, q_ref, do_ref, k_ref, v_ref, lse_ref, dl_ref, dq_ref, dk_ref, dv_ref, ds_ref):
        kvh, n0 = pl.program_id(0), pl.program_id(1)

        @pl.when(n0 == 0)
        def _():
            dk_ref[...] = jnp.zeros_like(dk_ref)
            dv_ref[...] = jnp.zeros_like(dv_ref)
            ds_ref[...] = jnp.zeros_like(ds_ref)

        rid = lax.broadcasted_iota(jnp.int32, (8, LANES), 0)
        upd = jnp.zeros((8, LANES), F32)
        offsets = _band_offsets(grp)
        for u in range(nq):
            rows = slice(u * BLOCK, (u + 1) * BLOCK)
            start, ok = _band(n0 * nq + u, t, offsets)
            band = pl.ds(start, BAND)
            qparts, doparts = [], []
            for hh in range(grp):
                sl = slice((hh // 2) * LANES, (hh // 2 + 1) * LANES)
                qparts.append(_head_to_half(q_ref[rows, sl].astype(F32) * (SCALE * LOG2E), hh % 2, kvh))
                doparts.append(_head_to_half(do_ref[rows, sl].astype(F32), hh % 2, kvh))
            qs = jnp.concatenate(qparts, axis=0).astype(BF16)
            dos = jnp.concatenate(doparts, axis=0).astype(BF16)
            lse_b = jnp.concatenate([lse_ref[hh, rows, :] for hh in range(grp)], axis=0)
            dl_b = jnp.concatenate([dl_ref[hh, rows, :] for hh in range(grp)], axis=0)
            kband, vband = k_ref[band, :], v_ref[band, :]
            s = jnp.where(ok, _dot_nt(qs, kband), NEG)
            p = jnp.exp2(s - lse_b[:, :1])
            dp = _dot_nt(dos, vband)
            dsc = (p * (dp - dl_b[:, :1])).astype(BF16)
            dv_ref[0, band, :] += _dot_tn(p.astype(BF16), dos)
            dk_ref[0, band, :] += _dot_tn(dsc, qs) * LN2
            dq = _dot_nn(dsc, kband) * SCALE
            for c in range(grp // 2):
                dq_ref[rows, c * LANES:(c + 1) * LANES] = _halves_to_heads(
                    dq[2 * c * BLOCK:(2 * c + 1) * BLOCK], dq[(2 * c + 1) * BLOCK:(2 * c + 2) * BLOCK], kvh).astype(dq_ref.dtype)
            for hh in range(grp):
                rs = slice(hh * BLOCK, (hh + 1) * BLOCK)
                tot = jnp.sum(jnp.exp2(sink_ref[kvh * grp + hh] * LOG2E - lse_b[rs]) * dl_b[rs], axis=0, keepdims=True)
                upd = upd + jnp.where(rid == hh, -tot, 0.0)
        ds_ref[0] += upd

    qspec = pl.BlockSpec((tq, gw), lambda kvh, n: (n, kvh))
    whole = pl.BlockSpec((t, LANES), lambda kvh, n: (0, 0))
    stat = pl.BlockSpec((grp, tq, LANES), lambda kvh, n: (kvh, n, 0))
    slab = pl.BlockSpec((1, t, LANES), lambda kvh, n: (kvh, 0, 0))
    return pl.pallas_call(
        body, name=name, grid=(A_KV_HEADS, t // tq),
        in_specs=[pl.BlockSpec(memory_space=pltpu.SMEM), qspec, qspec, whole, whole, stat, stat],
        out_specs=[qspec, slab, slab, pl.BlockSpec((1, 8, LANES), lambda kvh, n: (kvh, 0, 0))],
        out_shape=[jax.ShapeDtypeStruct((t, A_Q), BF16), jax.ShapeDtypeStruct((A_KV_HEADS, t, LANES), F32),
                   jax.ShapeDtypeStruct((A_KV_HEADS, t, LANES), F32), jax.ShapeDtypeStruct((A_KV_HEADS, 8, LANES), F32)],
        compiler_params=_cparams(("arbitrary", "arbitrary")),
    )(sink, q, do, k, v, lse, delta)


def flash_attn_fwd(q, k, v, cat, name, exchange=None, tq=1024, tk=512, ahead=2):
    t = q.shape[0]
    tq, tk = _fit(t, tq), _fit(t, tk)
    nk = t // tk

    def body(q_ref, k_ref, v_ref, cat_ref, o_ref, lse_ref):
        del cat_ref
        kvh = pl.program_id(0) // 2
        qf = q_ref[...].astype(F32) * (SCALE * LOG2E)
        qs = jnp.concatenate([_head_to_half(qf, 0, kvh), _head_to_half(qf, 1, kvh)], axis=0).astype(BF16)
        mine = (_lane((tk, LANES)) < HEAD_DIM) == (kvh == 0)

        def scores(c):
            return _dot_nt(qs, k_ref[c * tk:(c + 1) * tk, :])

        s = [scores(c) for c in range(min(ahead, nk))]
        m = jnp.full((2 * tq, 1), NEG, F32)
        acc = jnp.zeros((2 * tq, LANES), F32)
        for c in range(nk):
            if c + ahead < nk:
                s.append(scores(c + ahead))
            vb = jnp.where(mine, v_ref[c * tk:(c + 1) * tk, :], jnp.ones((), BF16))
            m_new = jnp.maximum(m, jnp.max(s[c], axis=-1, keepdims=True))
            p = jnp.exp2(s[c] - m_new).astype(BF16)
            acc = jnp.exp2(m - m_new) * acc + _dot_nn(p, vb)
            m = m_new
        other = pltpu.roll(acc, HEAD_DIM, 1)
        o = acc / other
        o_ref[...] = _halves_to_heads(o[:tq], o[tq:], kvh).astype(BF16)
        in_mine = (_lane(acc.shape) < HEAD_DIM) == (kvh == 0)
        lse = jnp.broadcast_to(m, acc.shape) + jnp.log2(jnp.where(in_mine, other, acc))
        lse_ref[0] = lse[:tq]
        lse_ref[1] = lse[tq:]

    qspec = pl.BlockSpec((tq, LANES), lambda j, i: (i, j))
    whole = pl.BlockSpec((t, LANES), lambda j, i: (0, 0))
    nj, ni = B_HEADS // 2, t // tq
    steps = lambda: ((pl.program_id(0) == 0) & (pl.program_id(1) == 0), (pl.program_id(0) == nj - 1) & (pl.program_id(1) == ni - 1))
    body, x_in, x_out, x_shapes, x_scratch = carried(body, exchange, 4, 2, steps)
    return pl.pallas_call(
        body, name=name, grid=(nj, ni),
        in_specs=[qspec, whole, whole, _ANY] + x_in,
        out_specs=[pl.BlockSpec((tq, LANES), lambda j, i: (i, A_Q // LANES + j)),
                   pl.BlockSpec((2, tq, LANES), lambda j, i: (j, i, 0))] + x_out,
        out_shape=[jax.ShapeDtypeStruct(cat.shape, BF16), jax.ShapeDtypeStruct((B_HEADS, t, LANES), F32)] + x_shapes,
        scratch_shapes=x_scratch,
        input_output_aliases={3: 0},
        compiler_params=_cparams(("arbitrary", "arbitrary")),
    )(q, k, v, cat, *(exchange.arrays if exchange else ()))


def flash_attn_bwd(q, k, v, do, lse, delta, name, exchange=None, tq=512, tk=512, together=4):
    t = q.shape[0]
    tq, tk = _fit(t, tq), _fit(t, tk)
    nk = t // tk
    together = _fit(nk, together)
    grp = B_HEADS // B_KV_HEADS
    gw = grp * HEAD_DIM

    def body(q_ref, do_ref, k_ref, v_ref, lse_ref, dl_ref, dq_ref, dk_ref, dv_ref, dq_s):
        kvh, i = pl.program_id(0), pl.program_id(1)

        @pl.when(i == 0)
        def _():
            dk_ref[...] = jnp.zeros_like(dk_ref)
            dv_ref[...] = jnp.zeros_like(dv_ref)

        qparts, doparts = [], []
        for hh in range(grp):
            sl = slice((hh // 2) * LANES, (hh // 2 + 1) * LANES)
            qparts.append(_head_to_half(q_ref[:, sl].astype(F32) * (SCALE * LOG2E), hh % 2, kvh))
            doparts.append(_head_to_half(do_ref[:, sl].astype(F32), hh % 2, kvh))
        qf, dof = jnp.concatenate(qparts, axis=0), jnp.concatenate(doparts, axis=0)
        qs, dos = qf.astype(BF16), dof.astype(BF16)
        qs_t, dos_t = qf.T.astype(BF16), dof.T.astype(BF16)
        lse = jnp.tile(jnp.concatenate([lse_ref[hh] for hh in range(grp)], axis=0), (1, tk // LANES))
        dl = jnp.tile(jnp.concatenate([dl_ref[hh] for hh in range(grp)], axis=0), (1, tk // LANES))
        dq_s[...] = jnp.zeros_like(dq_s)

        def chunks(c0, carry):
            cs = [c0 * together + u for u in range(together)]
            kbs = [k_ref[pl.ds(pl.multiple_of(c * tk, tk), tk), :] for c in cs]
            vbs = [v_ref[pl.ds(pl.multiple_of(c * tk, tk), tk), :] for c in cs]
            ss = [_dot_nt(qs, kb) for kb in kbs]
            dps = [_dot_nt(dos, vb) for vb in vbs]
            for c, kb, s, dp in zip(cs, kbs, ss, dps):
                p = jnp.exp2(s - lse)
                dsc = (p * (dp - dl)).astype(BF16)
                dv_ref[0, c] += _dot_nn(dos_t, p.astype(BF16))
                dk_ref[0, c] += _dot_nn(qs_t, dsc) * LN2
                dq_s[...] += _dot_nn(kb.T, dsc.T)
            return carry

        lax.fori_loop(0, nk // together, chunks, 0)
        dq = dq_s[...].T
        for c in range(grp // 2):
            dq_ref[:, c * LANES:(c + 1) * LANES] = (_halves_to_heads(
                dq[2 * c * tq:(2 * c + 1) * tq], dq[(2 * c + 1) * tq:(2 * c + 2) * tq], kvh) * SCALE).astype(dq_ref.dtype)

    qspec = pl.BlockSpec((tq, gw), lambda kvh, i: (i, kvh))
    dospec = pl.BlockSpec((tq, gw), lambda kvh, i: (i, A_Q // gw + kvh))
    whole = pl.BlockSpec((t, LANES), lambda kvh, i: (0, 0))
    stat = pl.BlockSpec((grp, tq, LANES), lambda kvh, i: (kvh, i, 0))
    dlstat = pl.BlockSpec((grp, tq, LANES), lambda kvh, i: (A_HEADS // grp + kvh, i, 0))
    slab = pl.BlockSpec((1, nk, LANES, tk), lambda kvh, i: (kvh, 0, 0, 0))
    ni = t // tq
    steps = lambda: ((pl.program_id(0) == 0) & (pl.program_id(1) == 0),
                     (pl.program_id(0) == B_KV_HEADS - 1) & (pl.program_id(1) == ni - 1))
    body, x_in, x_out, x_shapes, x_scratch = carried(body, exchange, 6, 3, steps)
    return pl.pallas_call(
        body, name=name, grid=(B_KV_HEADS, ni),
        in_specs=[qspec, dospec, whole, whole, stat, dlstat] + x_in,
        out_specs=[qspec, slab, slab] + x_out,
        out_shape=[jax.ShapeDtypeStruct((t, B_Q), BF16), jax.ShapeDtypeStruct((B_KV_HEADS, nk, LANES, tk), F32),
                   jax.ShapeDtypeStruct((B_KV_HEADS, nk, LANES, tk), F32)] + x_shapes,
        scratch_shapes=[pltpu.VMEM((LANES, grp * tq), F32)] + x_scratch,
        compiler_params=_cparams(("arbitrary", "arbitrary")),
    )(q, do, k, v, lse, delta, *(exchange.arrays if exchange else ()))


_GELU_C = math.sqrt(2.0 / math.pi)
_GELU_A = 0.044715


def _gelu(x):
    return 0.5 * x * (1.0 + jnp.tanh(_GELU_C * (x + _GELU_A * x * x * x)))


def _gelu_grad(x):
    th = jnp.tanh(_GELU_C * (x + _GELU_A * x * x * x))
    return 0.5 * (1.0 + th) + 0.5 * x * (1.0 - th * th) * _GELU_C * (1.0 + 3.0 * _GELU_A * x * x)


def _layernorm_stats(vf):
    mu = jnp.mean(vf, axis=-1, keepdims=True)
    vc = vf - mu
    r = lax.rsqrt(jnp.mean(vc * vc, axis=-1, keepdims=True) + EPS)
    return vc * r, r


def sgu_mix_fwd(z, ln_g, ln_b, w_s, b_rows, name):
    t, w2 = z.shape
    w = w2 // 2
    dg = w // SGU_GROUPS

    def body(u_ref, v_ref, g_ref, b_ref, ws_ref, bb_ref, y_ref):
        vhat, _ = _layernorm_stats(v_ref[...].astype(F32))
        vn = (vhat * g_ref[...] + b_ref[...]).astype(BF16)
        for g in range(SGU_GROUPS):
            sl = slice(g * dg, (g + 1) * dg)
            mixed = _dot_nn(ws_ref[g], vn[:, sl]) + bb_ref[g]
            y_ref[:, sl] = (u_ref[:, sl].astype(F32) * mixed).astype(BF16)

    vec = pl.BlockSpec((1, w), lambda n: (0, 0))
    whole = pl.BlockSpec((SGU_GROUPS, SGU_CHUNK, SGU_CHUNK), lambda n: (0, 0, 0))
    return pl.pallas_call(
        body, name=name, grid=(t // SGU_CHUNK,),
        in_specs=[pl.BlockSpec((SGU_CHUNK, w), lambda n: (n, 0)), pl.BlockSpec((SGU_CHUNK, w), lambda n: (n, 1)),
                  vec, vec, whole, whole],
        out_specs=pl.BlockSpec((SGU_CHUNK, w), lambda n: (n, 0)),
        out_shape=jax.ShapeDtypeStruct((t, w), BF16),
        compiler_params=_cparams(("parallel",)),
    )(z, z, ln_g.reshape(1, w), ln_b.reshape(1, w), w_s, b_rows)


def sgu_mix_bwd(z, apre, dy, ln_g, ln_b, w_s, b_rows, name):
    t, w2 = z.shape
    w = w2 // 2
    dg = w // SGU_GROUPS

    def body(u_ref, v_ref, au_ref, av_ref, dy_ref, g_ref, b_ref, ws_ref, bb_ref, da_ref, dlg_ref, dlb_ref, dws_ref, dbs_ref):
        @pl.when(pl.program_id(0) == 0)
        def _():
            dlg_ref[...] = jnp.zeros_like(dlg_ref)
            dlb_ref[...] = jnp.zeros_like(dlb_ref)
            dws_ref[...] = jnp.zeros_like(dws_ref)
            dbs_ref[...] = jnp.zeros_like(dbs_ref)

        vhat, r = _layernorm_stats(v_ref[...].astype(F32))
        gam = g_ref[...]
        vn = (vhat * gam + b_ref[...]).astype(BF16)
        ones8 = jnp.ones((8, dg), BF16)
        rid = lax.broadcasted_iota(jnp.int32, (8, SGU_CHUNK), 0)
        dbs = jnp.zeros((8, SGU_CHUNK), F32)
        dvn_parts = []
        for g in range(SGU_GROUPS):
            sl = slice(g * dg, (g + 1) * dg)
            dyg = dy_ref[:, sl].astype(F32)
            mixed = _dot_nn(ws_ref[g], vn[:, sl]) + bb_ref[g]
            da_ref[:, sl] = (dyg * mixed * _gelu_grad(au_ref[:, sl].astype(F32))).astype(BF16)
            dmix = dyg * u_ref[:, sl].astype(F32)
            dm_hi = dmix.astype(BF16)
            dm_lo = (dmix - dm_hi.astype(F32)).astype(BF16)
            dws_ref[g] += _dot_nt(dm_hi, vn[:, sl])
            dbs = dbs + jnp.where(rid == g, _dot_nt(ones8, dm_hi) + _dot_nt(ones8, dm_lo), 0.0)
            dvn_parts.append(_dot_tn(ws_ref[g], dm_hi))
        dbs_ref[...] += dbs
        dvn = jnp.concatenate(dvn_parts, axis=1)
        dlg_ref[...] += jnp.sum(dvn * vhat, axis=0, keepdims=True)
        dlb_ref[...] += jnp.sum(dvn, axis=0, keepdims=True)
        dvh = dvn * gam
        dv = r * (dvh - jnp.mean(dvh, axis=-1, keepdims=True) - vhat * jnp.mean(dvh * vhat, axis=-1, keepdims=True))
        da_ref[:, w:] = (dv * _gelu_grad(av_ref[...].astype(F32))).astype(BF16)

    vec = pl.BlockSpec((1, w), lambda n: (0, 0))
    whole = pl.BlockSpec((SGU_GROUPS, SGU_CHUNK, SGU_CHUNK), lambda n: (0, 0, 0))
    left = pl.BlockSpec((SGU_CHUNK, w), lambda n: (n, 0))
    right = pl.BlockSpec((SGU_CHUNK, w), lambda n: (n, 1))
    return pl.pallas_call(
        body, name=name, grid=(t // SGU_CHUNK,),
        in_specs=[left, right, left, right, left, vec, vec, whole, whole],
        out_specs=[pl.BlockSpec((SGU_CHUNK, w2), lambda n: (n, 0)), vec, vec, whole,
                   pl.BlockSpec((SGU_GROUPS, SGU_CHUNK), lambda n: (0, 0))],
        out_shape=[jax.ShapeDtypeStruct((t, w2), BF16), jax.ShapeDtypeStruct((1, w), F32), jax.ShapeDtypeStruct((1, w), F32),
                   jax.ShapeDtypeStruct((SGU_GROUPS, SGU_CHUNK, SGU_CHUNK), F32),
                   jax.ShapeDtypeStruct((SGU_GROUPS, SGU_CHUNK), F32)],
        compiler_params=_cparams(("arbitrary",)),
    )(z, z, apre, apre, dy, ln_g.reshape(1, w), ln_b.reshape(1, w), w_s, b_rows)


def loss_head(h, g, target, name):
    t, d = h.shape
    tm = min(t, 512)

    def body(h_ref, g_ref, t_ref, loss_ref, dh_ref, dhb_ref, dg_ref):
        @pl.when(pl.program_id(0) == 0)
        def _():
            loss_ref[...] = jnp.zeros_like(loss_ref)
            dg_ref[...] = jnp.zeros_like(dg_ref)

        xf = h_ref[...]
        r = lax.rsqrt(jnp.mean(xf * xf, axis=-1, keepdims=True) + EPS)
        xhat = xf * r
        err = xhat * g_ref[...] - t_ref[...]
        per_tok = jnp.mean(err * err, axis=-1, keepdims=True)
        loss_ref[...] += 0.5 * jnp.sum(per_tok, axis=0, keepdims=True)
        dy = err * (1.0 / d)
        dg_ref[...] += jnp.sum(dy * xhat, axis=0, keepdims=True)
        dxh = dy * g_ref[...]
        dh = r * (dxh - xhat * jnp.mean(dxh * xhat, axis=-1, keepdims=True))
        dh_ref[...] = dh
        dhb_ref[...] = dh.astype(BF16)

    row = pl.BlockSpec((tm, d), lambda i: (i, 0))
    vec = pl.BlockSpec((1, d), lambda i: (0, 0))
    return pl.pallas_call(
        body, name=name, grid=(t // tm,),
        in_specs=[row, vec, row],
        out_specs=[pl.BlockSpec((1, LANES), lambda i: (0, 0)), row, row, vec],
        out_shape=[jax.ShapeDtypeStruct((1, LANES), F32), jax.ShapeDtypeStruct((t, d), F32), jax.ShapeDtypeStruct((t, d), BF16),
                   jax.ShapeDtypeStruct((1, d), F32)],
        compiler_params=_cparams(("arbitrary",)),
    )(h, g.reshape(1, d), target)


ADAMW_BLOCK_BYTES = 1 << 20


def adamw(parts, w, m, v, name):
    n_layers, r, c = w.shape
    row_bytes = n_layers * c * 4
    if r * row_bytes <= 2 * ADAMW_BLOCK_BYTES:
        tr = r
    else:
        tr = _fit(r, 1 << int(math.log2(max(8, ADAMW_BLOCK_BYTES // row_bytes))))
    bc1 = 1.0 - ADAM_B1 ** ADAM_STEP
    bc2 = 1.0 - ADAM_B2 ** ADAM_STEP

    def body(*refs):
        p_refs = refs[:n_layers]
        w_ref, m_ref, v_ref, g_ref, d_ref, nm_ref, nv_ref = refs[n_layers:]
        for l in range(n_layers):
            g = p_refs[l][0].astype(F32)
            for j in range(1, N_DEV):
                g = g + p_refs[l][j].astype(F32)
            nm = ADAM_B1 * m_ref[l] + (1.0 - ADAM_B1) * g
            nv = ADAM_B2 * v_ref[l] + (1.0 - ADAM_B2) * (g * g)
            g_ref[l] = g
            nm_ref[l] = nm
            nv_ref[l] = nv
            d_ref[l] = -ADAM_LR * ((nm / bc1) / (jnp.sqrt(nv / bc2) + ADAM_EPS) + ADAM_WD * w_ref[l])

    blk = pl.BlockSpec((n_layers, tr, c), lambda i: (0, i, 0))
    return pl.pallas_call(
        body, name=name, grid=(r // tr,),
        in_specs=[pl.BlockSpec((N_DEV, tr, c), lambda i: (0, i, 0))] * n_layers + [blk, blk, blk],
        out_specs=[blk] * 4,
        out_shape=[jax.ShapeDtypeStruct((n_layers, r, c), F32)] * 4,
        compiler_params=_cparams(("parallel",)),
    )(*parts, w, m, v)


_ANY = pl.BlockSpec(memory_space=pl.ANY)


def _mesh_pos():
    return lax.axis_index("x"), lax.axis_index("y"), lax.axis_index("c")


class Exchange:
    def __init__(self, gathers=(), scatters=()):
        self.items = [("gather", a) for a in gathers] + [("scatter", a) for a in scatters]
        self.arrays = [a for _, a in self.items]
        self.n = len(self.items)

    def out_shapes(self):
        return [jax.ShapeDtypeStruct(((N_DEV,) + a.shape) if kind == "gather" else a.shape, a.dtype) for kind, a in self.items]

    def scratch(self):
        return [pltpu.SemaphoreType.DMA((7 * self.n,)), pltpu.SemaphoreType.DMA((7 * self.n,)), pltpu.SemaphoreType.DMA((self.n,))]

    def _copies(self, in_refs, out_refs, send_sems, recv_sems, local_sems):
        x, y, c = _mesh_pos()
        me = 4 * x + 2 * y + c
        local, sends, arrivals = [], [], []
        for t, (kind, _) in enumerate(self.items):
            src_of = (lambda slot, r=in_refs[t]: r) if kind == "gather" else (lambda slot, r=in_refs[t]: r.at[slot])
            local.append(pltpu.make_async_copy(src_of(me), out_refs[t].at[me], local_sems.at[t]))
            for k in range(1, N_DEV):
                px = 1 - x if k & 4 else x
                py = 1 - y if k & 2 else y
                pc = 1 - c if k & 1 else c
                pid = 4 * px + 2 * py + pc
                kw = dict(send_sem=send_sems.at[7 * t + k - 1], recv_sem=recv_sems.at[7 * t + k - 1],
                          device_id=(px, py, pc), device_id_type=pl.DeviceIdType.MESH)
                sends.append(pltpu.make_async_remote_copy(src_ref=src_of(pid), dst_ref=out_refs[t].at[me], **kw))
                arrivals.append(pltpu.make_async_remote_copy(src_ref=src_of(pid), dst_ref=out_refs[t].at[pid], **kw))
        return local, sends, arrivals

    def start(self, *refs):
        local, sends, _ = self._copies(*refs)
        for cp in local + sends:
            cp.start()

    def wait(self, *refs):
        local, sends, arrivals = self._copies(*refs)
        for cp in arrivals:
            cp.wait_recv()
        for cp in sends:
            cp.wait_send()
        for cp in local:
            cp.wait()


def carried(body, exchange, n_in, n_out, first_last):
    if exchange is None:
        return body, [], [], [], []
    nx = exchange.n

    def wrapped(*refs):
        ins, xin = refs[:n_in], refs[n_in:n_in + nx]
        outs, xout = refs[n_in + nx:n_in + nx + n_out], refs[n_in + nx + n_out:n_in + 2 * nx + n_out]
        scratch, sems = refs[n_in + 2 * nx + n_out:-3], refs[-3:]
        first, last = first_last()

        @pl.when(first)
        def _():
            exchange.start(xin, xout, *sems)

        body(*ins, *outs, *scratch)

        @pl.when(last)
        def _():
            exchange.wait(xin, xout, *sems)

    return wrapped, [_ANY] * nx, [_ANY] * nx, exchange.out_shapes(), exchange.scratch()


def exchange_only(exchange, name):
    def body(*refs):
        xin, xout, sems = refs[:exchange.n], refs[exchange.n:2 * exchange.n], refs[-3:]
        exchange.start(xin, xout, *sems)
        exchange.wait(xin, xout, *sems)

    return pl.pallas_call(
        body, name=name, in_specs=[_ANY] * exchange.n, out_specs=[_ANY] * exchange.n,
        out_shape=exchange.out_shapes(), scratch_shapes=exchange.scratch(),
    )(*exchange.arrays)


def _residual_out(a, w_out, x, next_g, name, **tiles):
    if next_g is None:
        (y,) = matmul(a, w_out, "nn", name, [F32], epilogue=lambda acc, r: (r + acc,), extras=(x,), **tiles)
        return y, None

    def add_and_norm(acc, r, g):
        y = r + acc
        return y, y * lax.rsqrt(jnp.mean(y * y, axis=-1, keepdims=True) + EPS) * g

    assert w_out.shape[1] <= tiles.get("tn", 1024)
    return matmul(a, w_out, "nn", name, [F32, BF16], epilogue=add_and_norm, extras=(x, next_g.reshape(1, -1)), **tiles)


def attention_fwd(x, h, w_in, sink, qn_g, kn_g, w_out, tables, next_g, tag, exchange=None):
    (proj,) = matmul(h, w_in, "nn", f"{tag}_proj", [F32], tn=ATT_IN)
    qa, ka, va, qb, kb, vb = qkv_post_fwd(proj, tables, qn_g, kn_g, f"{tag}_qkv")
    cat, lse_a = window_attn_fwd(qa, ka, va, sink, f"{tag}_win")
    cat, lse_b, *arrived = flash_attn_fwd(qb, kb, vb, cat, f"{tag}_flash", exchange)
    if callable(w_out):
        w_out = w_out(arrived)
    y, h_next = _residual_out(cat, w_out, x, next_g, f"{tag}_out")
    saved = (x, h, proj, qa, ka, va, qb, kb, vb, cat, lse_a, lse_b)
    return y, h_next, saved, arrived


def attention_bwd(dy, dyb, saved, norm_g, w_in, sink, qn_g, kn_g, w_out, tables, tag, exchange_with=None):
    x, h, proj, qa, ka, va, qb, kb, vb, cat, lse_a, lse_b = saved
    (dcat,) = matmul(dyb, w_out, "nt", f"{tag}_dcat", [BF16])
    (dw_out,) = matmul(cat, dyb, "tn", f"{tag}_dwout", [BF16], tk=4096)
    delta = attn_delta(cat, dcat, f"{tag}_delta")
    dqa, dka, dva, dsink = window_attn_bwd(qa, ka, va, sink, dcat, lse_a, delta, f"{tag}_dwin")
    exchange = exchange_with(dw_out) if exchange_with else None
    dqb, dkb, dvb, *arrived = flash_attn_bwd(qb, kb, vb, dcat, lse_b, delta, f"{tag}_dflash", exchange)
    dproj, dqg, dkg = qkv_post_bwd(proj, tables, qn_g, kn_g, dqa, dka, dva, dqb, dkb, dvb, f"{tag}_dqkv")
    (dw_in,) = matmul(h, dproj, "tn", f"{tag}_dwin_w", [BF16], tn=ATT_IN // 2, tk=2048)
    dx, dxb, dg = matmul_nt_normbwd(dproj, w_in, x, norm_g, dy, f"{tag}_dx")
    grp = A_HEADS // A_KV_HEADS
    small = dict(norm=dg[0], sink=dsink[:, :grp, 0].reshape(A_HEADS), qnorm=dqg[0, :HEAD_DIM], knorm=dkg[0, :HEAD_DIM])
    return dx, dxb, dw_in, dw_out, small, arrived


def sgu_fwd(x, h, w_in, ln_g, ln_b, w_s, b_rows, w_out, next_g, tag):
    apre, z = matmul(h, w_in, "nn", f"{tag}_in", [BF16, BF16], epilogue=lambda acc: (acc, _gelu(acc)))
    y = sgu_mix_fwd(z, ln_g, ln_b, w_s, b_rows, f"{tag}_mix")
    out, h_next = _residual_out(y, w_out, x, next_g, f"{tag}_out")
    return out, h_next, (x, h, apre, z, y)


def sgu_bwd(dout, doutb, saved, norm_g, w_in, ln_g, ln_b, w_s, b_rows, w_out, tag):
    x, h, apre, z, y = saved
    (dy,) = matmul(doutb, w_out, "nt", f"{tag}_dy", [BF16])
    (dw_out,) = matmul(y, doutb, "tn", f"{tag}_dwout", [BF16], tk=4096)
    dapre, dlg, dlb, dws, dbs = sgu_mix_bwd(z, apre, dy, ln_g, ln_b, w_s, b_rows, f"{tag}_dmix")
    (dw_in,) = matmul(h, dapre, "tn", f"{tag}_dwin", [BF16], out_shards=True, tk=4096)
    dx, dxb, dg = matmul_nt_normbwd(dapre, w_in, x, norm_g, dout, f"{tag}_dx")
    small = dict(norm=dg[0], ln_g=dlg[0], ln_b=dlb[0], w_s=dws, b_s=dbs)
    return dx, dxb, dw_in, dw_out, small


def _square(r):
    return r * r


def mlp_fwd(x, h, w1, w2, next_g, tag):
    (r,) = matmul(h, w1, "nn", f"{tag}_up", [BF16], epilogue=lambda acc: (jnp.maximum(acc, 0.0),), tm=2048)
    y, h_next = _residual_out(r, w2, x, next_g, f"{tag}_down", a_fn=_square, tm=512, tk=4096)
    return y, h_next, (x, h, r)


def mlp_bwd(dy, dyb, saved, norm_g, w1, w2, tag):
    x, h, r = saved
    (da,) = matmul(dyb, w2, "nt", f"{tag}_da", [BF16], epilogue=lambda acc, rr: (acc * (2.0 * rr.astype(F32)),), extras=(r,),
                   tm=2048)
    (dw2,) = matmul(r, dyb, "tn", f"{tag}_dw2", [BF16], a_fn=_square, tk=4096)
    (dw1,) = matmul(h, da, "tn", f"{tag}_dw1", [BF16], out_shards=True, tk=4096)
    dx, dxb, dg = matmul_nt_normbwd(da, w1, x, norm_g, dy, f"{tag}_dx")
    return dx, dxb, dw1, dw2, dg[0]


ORDER = ("att_norm", "att_w_in", "att_sink", "att_qnorm", "att_knorm", "att_w_out", "sgu_norm", "sgu_w_in", "sgu_ln_g",
         "sgu_ln_b", "sgu_w_s", "sgu_b_s", "sgu_w_out", "mlp_norm", "mlp_w1", "mlp_w2", "final_norm")
SHARDED = ("att_w_in", "att_w_out", "sgu_w_in", "sgu_w_out", "mlp_w1", "mlp_w2")
SGU_VECS = ("sgu_norm", "sgu_ln_g", "sgu_ln_b")
SMALL_EARLY = ("sgu_w_s", "sgu_b_s", "mlp_norm", "final_norm", "loss")
SMALL_LATE = ("att_norm", "att_sink", "att_qnorm", "att_knorm")
SMALL_ROWS_MULT = 8


def _flat(blocks, names):
    flat = jnp.concatenate([blocks[n].reshape(-1).astype(F32) for n in names])
    per = SMALL_ROWS_MULT * FLAT_COLS
    total = -(-flat.shape[0] // per) * per
    return jnp.pad(flat, (0, total - flat.shape[0])).reshape(1, total // FLAT_COLS, FLAT_COLS)


def _unflat(flat, like, names):
    out, off = {}, 0
    f = flat.reshape(-1)
    for n in names:
        size = like[n].size
        out[n] = f[off:off + size].reshape(like[n].shape)
        off += size
    return out


def kernel(x, att_norm, att_w_in, att_sink, att_qnorm, att_knorm, att_w_out, sgu_norm, sgu_w_in, sgu_ln_g, sgu_ln_b, sgu_w_s, sgu_b_s, sgu_w_out, mlp_norm, mlp_w1, mlp_w2, final_norm, loss_target, m_att_norm, m_att_w_in, m_att_sink, m_att_qnorm, m_att_knorm, m_att_w_out, m_sgu_norm, m_sgu_w_in, m_sgu_ln_g, m_sgu_ln_b, m_sgu_w_s, m_sgu_b_s, m_sgu_w_out, m_mlp_norm, m_mlp_w1, m_mlp_w2, m_final_norm, v_att_norm, v_att_w_in, v_att_sink, v_att_qnorm, v_att_knorm, v_att_w_out, v_sgu_norm, v_sgu_w_in, v_sgu_ln_g, v_sgu_ln_b, v_sgu_w_s, v_sgu_b_s, v_sgu_w_out, v_mlp_norm, v_mlp_w1, v_mlp_w2, v_final_norm):
    w = dict(att_norm=att_norm, att_w_in=att_w_in, att_sink=att_sink, att_qnorm=att_qnorm, att_knorm=att_knorm,
             att_w_out=att_w_out, sgu_norm=sgu_norm, sgu_w_in=sgu_w_in, sgu_ln_g=sgu_ln_g, sgu_ln_b=sgu_ln_b, sgu_w_s=sgu_w_s,
             sgu_b_s=sgu_b_s, sgu_w_out=sgu_w_out, mlp_norm=mlp_norm, mlp_w1=mlp_w1, mlp_w2=mlp_w2, final_norm=final_norm)
    m = dict(att_norm=m_att_norm, att_w_in=m_att_w_in, att_sink=m_att_sink, att_qnorm=m_att_qnorm, att_knorm=m_att_knorm,
             att_w_out=m_att_w_out, sgu_norm=m_sgu_norm, sgu_w_in=m_sgu_w_in, sgu_ln_g=m_sgu_ln_g, sgu_ln_b=m_sgu_ln_b,
             sgu_w_s=m_sgu_w_s, sgu_b_s=m_sgu_b_s, sgu_w_out=m_sgu_w_out, mlp_norm=m_mlp_norm, mlp_w1=m_mlp_w1, mlp_w2=m_mlp_w2,
             final_norm=m_final_norm)
    v = dict(att_norm=v_att_norm, att_w_in=v_att_w_in, att_sink=v_att_sink, att_qnorm=v_att_qnorm, att_knorm=v_att_knorm,
             att_w_out=v_att_w_out, sgu_norm=v_sgu_norm, sgu_w_in=v_sgu_w_in, sgu_ln_g=v_sgu_ln_g, sgu_ln_b=v_sgu_ln_b,
             sgu_w_s=v_sgu_w_s, sgu_b_s=v_sgu_b_s, sgu_w_out=v_sgu_w_out, mlp_norm=v_mlp_norm, mlp_w1=v_mlp_w1, mlp_w2=v_mlp_w2,
             final_norm=v_final_norm)
    loss, grad_x, g, d, nm, nv = train_step(x[0], loss_target[0], w, m, v)
    return (loss, grad_x[None], *[g[n] for n in ORDER], *[d[n] for n in ORDER], *[nm[n] for n in ORDER], *[nv[n] for n in ORDER])


def train_step(x, target, w, m, v):
    t, d_model = x.shape
    n_att, n_sgu, depth = w["att_w_in"].shape[0], w["sgu_w_in"].shape[0], w["mlp_w1"].shape[0]
    bf = lambda n: w[n].astype(BF16)

    assert n_sgu == n_att and depth == 2 * n_att
    vec_local = jnp.stack([w[n] for n in SGU_VECS], axis=1)
    att_in = bf("att_w_in")
    h, g_in0, g_vec = rmsnorm_fwd(x, w["att_norm"][0], "att0_norm", Exchange(gathers=[att_in[:1], vec_local]))
    vecs = g_vec.transpose(1, 2, 0, 3).reshape(n_sgu, len(SGU_VECS), -1)
    group_names = ("att_w_out", "sgu_w_in", "sgu_w_out", "mlp_w1", "mlp_w2")
    per = {n: w[n].shape[0] // n_att for n in group_names}
    rest = [Exchange(gathers=([att_in[1:]] if gi == 0 else []) + [bf(n)[gi * per[n]:(gi + 1) * per[n]] for n in group_names])
            for gi in range(n_att)]
    gathered = {}

    def weight(name, kind, layer):
        return Gathered(gathered[name][layer // per[name]], kind, layer % per[name])

    w_s_bf = w["sgu_w_s"].astype(BF16)
    b_rows = jnp.broadcast_to(w["sgu_b_s"][:, :, :, None], w["sgu_b_s"].shape + (LANES,))
    tables = _rope_tables(t)
    full_cols = lambda g: g.transpose(1, 2, 0, 3).reshape(g.shape[1], d_model, -1)

    mixer_norm = lambda layer: w["att_norm"][layer // 2] if layer % 2 == 0 else vecs[layer // 2, 0]
    saved = []
    for layer in range(depth):
        i = layer // 2
        if layer % 2 == 0:
            if layer == 0:
                att_w_in = [full_cols(g_in0)[0]]
            first = 1 if layer == 0 else 0
            x, h, sv, arrived = attention_fwd(x, h, att_w_in[i], w["att_sink"][i], w["att_qnorm"][i], w["att_knorm"][i],
                                              lambda arrived, first=first: Gathered(arrived[first], "row", 0),
                                              tables, w["mlp_norm"][layer], f"att{i}", rest[i])
            if layer == 0:
                att_w_in += list(full_cols(arrived[0]))
            for n, g in zip(group_names, arrived[first:]):
                gathered.setdefault(n, []).append(g)
        else:
            x, h, sv = sgu_fwd(x, h, weight("sgu_w_in", "col", i), vecs[i, 1], vecs[i, 2], w_s_bf[i], b_rows[i],
                               weight("sgu_w_out", "row", i), w["mlp_norm"][layer], f"sgu{i}")
        x, h, sm = mlp_fwd(x, h, weight("mlp_w1", "col", layer), weight("mlp_w2", "row", layer),
                           mixer_norm(layer + 1) if layer + 1 < depth else None, f"mlp{layer}")
        saved.append((sv, sm))
    loss_row, dh, dhb, dgf = loss_head(x, w["final_norm"], target, "loss_head")

    queue, recv = [], {}
    gs = dict(att_norm=[None] * n_att, att_sink=[None] * n_att, att_qnorm=[None] * n_att, att_knorm=[None] * n_att,
              sgu_w_s=[None] * n_sgu, sgu_b_s=[None] * n_sgu, mlp_norm=[None] * depth)

    def row_slabs(g):
        return g.reshape(N_DEV, g.shape[0] // N_DEV, g.shape[1])

    def col_slabs(g):
        return g.reshape(g.shape[0], N_DEV, g.shape[1] // N_DEV).transpose(1, 0, 2)

    def take_queue(gathers=()):
        items = list(queue)
        queue.clear()
        keys = [k for k, _ in gathers] + [k for k, _ in items]
        return Exchange(gathers=[a for _, a in gathers], scatters=[a for _, a in items]), keys

    def small_early():
        blocks = dict(sgu_w_s=jnp.stack(gs["sgu_w_s"]), sgu_b_s=jnp.stack(gs["sgu_b_s"]), mlp_norm=jnp.stack(gs["mlp_norm"]),
                      final_norm=dgf[0], loss=loss_row[0, :1])
        return _flat(blocks, SMALL_EARLY)[0]

    for layer in reversed(range(depth)):
        i = layer // 2
        sv, sm = saved[layer]
        dh, dhb, dw1, dw2, gs["mlp_norm"][layer] = mlp_bwd(
            dh, dhb, sm, w["mlp_norm"][layer], weight("mlp_w1", "col", layer), weight("mlp_w2", "row", layer), f"mlp{layer}")
        queue += [(("mlp_w1", layer), dw1), (("mlp_w2", layer), row_slabs(dw2))]
        if layer % 2 == 0:
            keys = []

            def exchange_with(dw_out, i=i, layer=layer, keys=keys):
                if layer == 0:
                    queue.append((("att_w_out", i), row_slabs(dw_out)))
                ex, got = take_queue([("small_early", small_early())] if layer == 0 else ())
                keys += got
                return ex

            dh, dhb, dw_in, dw_out, sm_g, arrived = attention_bwd(
                dh, dhb, sv, w["att_norm"][i], att_w_in[i], w["att_sink"][i], w["att_qnorm"][i], w["att_knorm"][i],
                weight("att_w_out", "row", i), tables, f"att{i}", exchange_with)
            recv.update(zip(keys, arrived))
            queue.append((("att_w_in", i), col_slabs(dw_in)))
            if layer != 0:
                queue.append((("att_w_out", i), row_slabs(dw_out)))
            gs["att_norm"][i], gs["att_sink"][i] = sm_g["norm"], sm_g["sink"]
            gs["att_qnorm"][i], gs["att_knorm"][i] = sm_g["qnorm"], sm_g["knorm"]
        else:
            dh, dhb, dw_in, dw_out, sm_g = sgu_bwd(
                dh, dhb, sv, vecs[i, 0], weight("sgu_w_in", "col", i), vecs[i, 1], vecs[i, 2], w_s_bf[i], b_rows[i],
                weight("sgu_w_out", "row", i), f"sgu{i}")
            dvec = jnp.stack([sm_g["norm"], sm_g["ln_g"], sm_g["ln_b"]])
            queue += [(("sgu_w_in", i), dw_in), (("sgu_w_out", i), row_slabs(dw_out)), (("sgu_vecs", i), col_slabs(dvec))]
            gs["sgu_w_s"][i], gs["sgu_b_s"][i] = sm_g["w_s"], sm_g["b_s"]
    grad_x = dh
    late = dict(att_norm=jnp.stack(gs["att_norm"]), att_sink=jnp.stack(gs["att_sink"]), att_qnorm=jnp.stack(gs["att_qnorm"]),
                att_knorm=jnp.stack(gs["att_knorm"]))
    last, keys = take_queue([("small_late", _flat(late, SMALL_LATE)[0])])
    recv.update(zip(keys, exchange_only(last, "exchange_last")))

    outs = [{}, {}, {}, {}]
    for n in SHARDED:
        res = adamw([recv[(n, l)] for l in range(w[n].shape[0])], w[n], m[n], v[n], f"adamw_{n}")
        for o, r in zip(outs, res):
            o[n] = r
    stack_vecs = lambda src: jnp.stack([src[n] for n in SGU_VECS], axis=1)
    res = adamw([recv[("sgu_vecs", i)] for i in range(n_sgu)], stack_vecs(w), stack_vecs(m), stack_vecs(v), "adamw_sgu_vecs")
    for o, r in zip(outs, res):
        o.update({n: r[:, k] for k, n in enumerate(SGU_VECS)})
    zero = {"loss": jnp.zeros((1,), F32)}
    for names, key in ((SMALL_EARLY, "small_early"), (SMALL_LATE, "small_late")):
        res = adamw([recv[key]], _flat({**w, **zero}, names), _flat({**m, **zero}, names), _flat({**v, **zero}, names), f"adamw_{key}")
        for o, r in zip(outs, res):
            o.update(_unflat(r, {**w, **zero}, names))
    loss = outs[0]["loss"][0]
    return loss, grad_x, *outs
```

```python
import math

import jax
import jax.numpy as jnp
from jax import lax
from jax.experimental import pallas as pl
from jax.experimental.pallas import tpu as pltpu

F32 = jnp.float32
BF16 = jnp.bfloat16

HEAD_DIM = 64
A_HEADS = 8
A_KV_HEADS = 2
B_HEADS = 8
B_KV_HEADS = 2
WINDOW = 128
BLOCK = 128
ROPE_THETA = 10000.0
GRID_W = 64
SGU_GROUPS = 8
SGU_CHUNK = 128
EPS = 1e-6
SCALE = HEAD_DIM ** -0.5
NEG = -1e30
LOG2E = math.log2(math.e)
LN2 = math.log(2.0)

A_Q = A_HEADS * HEAD_DIM
A_KV = A_KV_HEADS * HEAD_DIM
B_Q = B_HEADS * HEAD_DIM
B_KV = B_KV_HEADS * HEAD_DIM
OFF_QA, OFF_KA, OFF_VA = 0, A_Q, A_Q + A_KV
OFF_QB = A_Q + 2 * A_KV
OFF_KB = OFF_QB + B_Q
OFF_VB = OFF_KB + B_KV
ATT_IN = OFF_VB + B_KV

ADAM_LR = 0.001
ADAM_B1 = 0.9
ADAM_B2 = 0.999
ADAM_EPS = 1e-08
ADAM_WD = 0.01
ADAM_STEP = 10

N_DEV = 8
LANES = 128
V7X_VMEM_LIMIT = 56 * 1024 * 1024
FLAT_COLS = 1024


def _cparams(sem, vmem=V7X_VMEM_LIMIT):
    return pltpu.CompilerParams(dimension_semantics=sem, vmem_limit_bytes=vmem)


def _dot_nn(a, b):
    return lax.dot_general(a, b, (((1,), (0,)), ((), ())), preferred_element_type=F32)


def _dot_nt(a, b):
    return lax.dot_general(a, b, (((1,), (1,)), ((), ())), preferred_element_type=F32)


def _dot_tn(a, b):
    return lax.dot_general(a, b, (((0,), (0,)), ((), ())), preferred_element_type=F32)


def _bf(x):
    return x if x.dtype == BF16 else x.astype(BF16)


def _lane(shape):
    return lax.broadcasted_iota(jnp.int32, shape, len(shape) - 1)


def _seg_matrix(rows_lo, rows_hi):
    r = lax.broadcasted_iota(jnp.int32, (LANES, LANES), 0)
    return jnp.where((r >= rows_lo) & (r < rows_hi), 1.0, 0.0).astype(BF16)


def _group_matrix(width):
    r = lax.broadcasted_iota(jnp.int32, (LANES, LANES), 0)
    c = lax.broadcasted_iota(jnp.int32, (LANES, LANES), 1)
    return jnp.where((r // width) == (c // width), 1.0, 0.0).astype(BF16)


def _dot_f32_by_ones(s, ones_bf16):
    hi = s.astype(BF16)
    lo = (s - hi.astype(F32)).astype(BF16)
    return _dot_nn(hi, ones_bf16) + _dot_nn(lo, ones_bf16)


def _swap_halves(x, width):
    half = width // 2
    first = (_lane(x.shape) % width) < half
    return jnp.where(first, pltpu.roll(x, LANES - half, 1), pltpu.roll(x, half, 1))


def rmsnorm_fwd(x, g, name, exchange=None):
    t, d = x.shape
    tm = min(t, 512)
    n = t // tm

    def body(x_ref, g_ref, h_ref):
        xf = x_ref[...]
        r = lax.rsqrt(jnp.mean(xf * xf, axis=-1, keepdims=True) + EPS)
        h_ref[...] = (xf * r * g_ref[...]).astype(BF16)

    body, x_in, x_out, x_shapes, x_scratch = carried(
        body, exchange, 2, 1, lambda: (pl.program_id(0) == 0, pl.program_id(0) == n - 1))
    return pl.pallas_call(
        body, name=name, grid=(n,),
        in_specs=[pl.BlockSpec((tm, d), lambda i: (i, 0)), pl.BlockSpec((1, d), lambda i: (0, 0))] + x_in,
        out_specs=[pl.BlockSpec((tm, d), lambda i: (i, 0))] + x_out,
        out_shape=[jax.ShapeDtypeStruct((t, d), BF16)] + x_shapes,
        scratch_shapes=x_scratch,
        compiler_params=_cparams(("arbitrary",)),
    )(x, g.reshape(1, d), *(exchange.arrays if exchange else ()))


def _fit(n, want):
    t = min(n, want)
    while n % t:
        t //= 2
    return t


class Gathered:
    def __init__(self, arr, kind, layer):
        self.arr, self.kind, self.layer = arr, kind, layer
        _, _, self.rows, self.cols = arr.shape
        self.shape = (N_DEV * self.rows, self.cols) if kind == "row" else (self.rows, N_DEV * self.cols)


def _b_operand(b, mode, tn, tk, idx):
    dot = {"nn": _dot_nn, "nt": _dot_nt, "tn": _dot_tn}[mode]
    if not isinstance(b, Gathered):
        if mode == "nt":
            spec = pl.BlockSpec((tn, tk), lambda *g: idx(*g))
        else:
            spec = pl.BlockSpec((tk, tn), lambda *g: idx(*g)[::-1])
        return b, spec, lambda av, ref: dot(av, _bf(ref[...]))
    lay, rows, cols = b.layer, b.rows, b.cols
    if mode == "nn" and b.kind == "col":
        s = tn // cols
        assert s * cols == tn
        spec = pl.BlockSpec((s, None, tk, cols), lambda *g: (idx(*g)[0], lay, idx(*g)[1], 0))
        return b.arr, spec, lambda av, ref: jnp.concatenate([_dot_nn(av, ref[c]) for c in range(s)], axis=1)
    if mode == "nn" and b.kind == "row":
        s = tk // rows
        assert s * rows == tk
        spec = pl.BlockSpec((s, None, rows, tn), lambda *g: (idx(*g)[1], lay, 0, idx(*g)[0]))
        return b.arr, spec, lambda av, ref: _dot_nn(av, ref[...].reshape(s * rows, tn))
    if mode == "nt" and b.kind == "row":
        s = tn // rows
        assert s * rows == tn
        spec = pl.BlockSpec((s, None, rows, tk), lambda *g: (idx(*g)[0], lay, 0, idx(*g)[1]))
        return b.arr, spec, lambda av, ref: _dot_nt(av, ref[...].reshape(s * rows, tk))
    if mode == "nt" and b.kind == "col":
        s = tk // cols
        assert s * cols == tk
        spec = pl.BlockSpec((s, None, tn, cols), lambda *g: (idx(*g)[1], lay, idx(*g)[0], 0))

        def prod(av, ref):
            tot = _dot_nt(av[:, :cols], ref[0])
            for c in range(1, s):
                tot = tot + _dot_nt(av[:, c * cols:(c + 1) * cols], ref[c])
            return tot

        return b.arr, spec, prod
    raise NotImplementedError((mode, b.kind))


def matmul(a, b, mode, name, out_dtypes, epilogue=None, extras=(), a_fn=None, out_shards=False, tm=1024, tn=1024, tk=1024):
    (m, k) = a.shape[::-1] if mode == "tn" else a.shape
    n = b.shape[0] if mode == "nt" else b.shape[1]
    if out_shards:
        tn = n // N_DEV
    tm, tn, tk = _fit(m, tm), _fit(n, tn), _fit(k, tk)
    nk = k // tk
    n_ex, n_out = len(extras), len(out_dtypes)
    if epilogue is None:
        epilogue = lambda acc: (acc,)
    b_arr, b_spec, prod = _b_operand(b, mode, tn, tk, lambda i, j, kk: (j, kk))
    by_shard = isinstance(b, Gathered) and mode == "nn" and b.kind == "col" and nk == 1 and not extras and not out_shards
    shard_cols = b.cols if by_shard and tn > b.cols else 0

    def body(*refs):
        a_ref, b_ref = refs[0], refs[1]
        ex_refs = refs[2:2 + n_ex]
        out_refs = refs[2 + n_ex:2 + n_ex + n_out]
        acc_ref = refs[2 + n_ex + n_out] if nk > 1 else None
        kk = pl.program_id(2)
        av = _bf(a_ref[...])
        if a_fn is not None:
            av = a_fn(av)
        if shard_cols:
            for c in range(tn // shard_cols):
                outs = epilogue(_dot_nn(av, b_ref[c]))
                for r, o in zip(out_refs, outs):
                    r[:, c * shard_cols:(c + 1) * shard_cols] = o.astype(r.dtype)
            return
        part = prod(av, b_ref)

        def finish(acc):
            outs = epilogue(acc, *[r[...] for r in ex_refs])
            for r, o in zip(out_refs, outs):
                r[...] = o.astype(r.dtype)

        if nk == 1:
            finish(part)
            return

        @pl.when(kk == 0)
        def _():
            acc_ref[...] = part

        @pl.when(kk > 0)
        def _():
            acc_ref[...] += part

        @pl.when(kk == nk - 1)
        def _():
            finish(acc_ref[...])

    if mode == "tn":
        a_spec = pl.BlockSpec((tk, tm), lambda i, j, kk: (kk, i))
    else:
        a_spec = pl.BlockSpec((tm, tk), lambda i, j, kk: (i, kk))
    mn_spec = pl.BlockSpec((tm, tn), lambda i, j, kk: (i, j))
    row_spec = pl.BlockSpec((1, tn), lambda i, j, kk: (0, j))
    if out_shards:
        out_spec = pl.BlockSpec((None, tm, tn), lambda i, j, kk: (j, i, 0))
        out_shape = [jax.ShapeDtypeStruct((N_DEV, m, tn), dt) for dt in out_dtypes]
    else:
        out_spec = mn_spec
        out_shape = [jax.ShapeDtypeStruct((m, n), dt) for dt in out_dtypes]
    outs = pl.pallas_call(
        body, name=name, grid=(m // tm, n // tn, nk),
        in_specs=[a_spec, b_spec] + [row_spec if e.shape[0] == 1 else mn_spec for e in extras],
        out_specs=[out_spec] * n_out,
        out_shape=out_shape,
        scratch_shapes=[pltpu.VMEM((tm, tn), F32)] if nk > 1 else [],
        compiler_params=_cparams(("parallel", "parallel", "arbitrary")),
    )(a, b_arr, *extras)
    return outs


def matmul_nt_normbwd(dz, w, x, g, dres, name, tm=512):
    m, k = dz.shape
    d = w.shape[0]
    tm = _fit(m, tm)
    w_arr, w_spec, prod = _b_operand(w, "nt", d, k, lambda i: (0, 0))

    def body(dz_ref, w_ref, x_ref, g_ref, dres_ref, dx_ref, dxb_ref, dg_ref):
        @pl.when(pl.program_id(0) == 0)
        def _():
            dg_ref[...] = jnp.zeros_like(dg_ref)

        dh = prod(_bf(dz_ref[...]), w_ref)
        xf = x_ref[...]
        r = lax.rsqrt(jnp.mean(xf * xf, axis=-1, keepdims=True) + EPS)
        xhat = xf * r
        dg_ref[...] += jnp.sum(dh * xhat, axis=0, keepdims=True)
        dxh = dh * g_ref[...]
        dx = r * (dxh - xhat * jnp.mean(dxh * xhat, axis=-1, keepdims=True))
        out = dres_ref[...] + dx
        dx_ref[...] = out
        dxb_ref[...] = out.astype(BF16)

    row = pl.BlockSpec((tm, d), lambda i: (i, 0))
    vec = pl.BlockSpec((1, d), lambda i: (0, 0))
    return pl.pallas_call(
        body, name=name, grid=(m // tm,),
        in_specs=[pl.BlockSpec((tm, k), lambda i: (i, 0)), w_spec, row, vec, row],
        out_specs=[row, row, vec],
        out_shape=[jax.ShapeDtypeStruct((m, d), F32), jax.ShapeDtypeStruct((m, d), BF16), jax.ShapeDtypeStruct((1, d), F32)],
        compiler_params=_cparams(("arbitrary",)),
    )(dz, w_arr, x, g.reshape(1, d), dres)


def _rope_tables(t):
    pos = lax.broadcasted_iota(jnp.int32, (t, LANES), 0)
    dim = lax.broadcasted_iota(jnp.int32, (t, LANES), 1) % HEAD_DIM

    def table(p, width):
        i = dim % (width // 2)
        ang = p.astype(F32) * (ROPE_THETA ** (-(2 * i).astype(F32) / width))
        return jnp.cos(ang), jnp.where(dim % width < width // 2, -jnp.sin(ang), jnp.sin(ang))

    cos_a, sin_a = table(pos, HEAD_DIM)
    cos_b, sin_b = table(jnp.where(dim < HEAD_DIM // 2, pos // GRID_W, pos % GRID_W), HEAD_DIM // 2)
    return cos_a, sin_a, cos_b, sin_b


def _headnorm(xs, gmat):
    return lax.rsqrt(_dot_f32_by_ones(xs * xs, gmat) * (1.0 / HEAD_DIM) + EPS)


def qkv_post_fwd(proj, tables, qn_g, kn_g, name):
    t = proj.shape[0]
    tm = min(t, 256)
    cos_a, sin_a, cos_b, sin_b = tables
    g2 = lambda g: jnp.concatenate([g, g]).reshape(1, LANES)

    def body(p_ref, ca_ref, sa_ref, cb_ref, sb_ref, qg_ref, kg_ref, qa_ref, ka_ref, va_ref, qb_ref, kb_ref, vb_ref):
        ca, sa, cb, sb = ca_ref[...], sa_ref[...], cb_ref[...], sb_ref[...]
        gmat = _group_matrix(HEAD_DIM)

        def rope_a(xs):
            return xs * ca + _swap_halves(xs, HEAD_DIM) * sa

        def norm_rope_b(xs, g):
            y = xs * _headnorm(xs, gmat) * g
            return y * cb + _swap_halves(y, HEAD_DIM // 2) * sb

        for c in range(A_Q // LANES):
            qa_ref[:, c * LANES:(c + 1) * LANES] = rope_a(p_ref[:, OFF_QA + c * LANES:OFF_QA + (c + 1) * LANES]).astype(BF16)
        ka_ref[...] = rope_a(p_ref[:, OFF_KA:OFF_KA + LANES]).astype(BF16)
        va_ref[...] = p_ref[:, OFF_VA:OFF_VA + LANES].astype(BF16)
        for c in range(B_Q // LANES):
            qb_ref[:, c * LANES:(c + 1) * LANES] = norm_rope_b(
                p_ref[:, OFF_QB + c * LANES:OFF_QB + (c + 1) * LANES], qg_ref[...]).astype(BF16)
        kb_ref[...] = norm_rope_b(p_ref[:, OFF_KB:OFF_KB + LANES], kg_ref[...]).astype(BF16)
        vb_ref[...] = p_ref[:, OFF_VB:OFF_VB + LANES].astype(BF16)

    tab = pl.BlockSpec((tm, LANES), lambda i: (i, 0))
    vec = pl.BlockSpec((1, LANES), lambda i: (0, 0))
    wide = pl.BlockSpec((tm, A_Q), lambda i: (i, 0))
    return pl.pallas_call(
        body, name=name, grid=(t // tm,),
        in_specs=[pl.BlockSpec((tm, ATT_IN), lambda i: (i, 0)), tab, tab, tab, tab, vec, vec],
        out_specs=[wide, tab, tab, wide, tab, tab],
        out_shape=[jax.ShapeDtypeStruct((t, A_Q), BF16), jax.ShapeDtypeStruct((t, LANES), BF16),
                   jax.ShapeDtypeStruct((t, LANES), BF16), jax.ShapeDtypeStruct((t, B_Q), BF16),
                   jax.ShapeDtypeStruct((t, LANES), BF16), jax.ShapeDtypeStruct((t, LANES), BF16)],
        compiler_params=_cparams(("parallel",)),
    )(proj, cos_a, sin_a, cos_b, sin_b, g2(qn_g), g2(kn_g))


def qkv_post_bwd(proj, tables, qn_g, kn_g, dqa, dka, dva, dqb, dkb, dvb, name):
    t = proj.shape[0]
    tm = min(t, 256)
    cos_a, sin_a, cos_b, sin_b = tables
    g2 = lambda g: jnp.concatenate([g, g]).reshape(1, LANES)

    def body(p_ref, ca_ref, sa_ref, cb_ref, sb_ref, qg_ref, kg_ref, dqa_ref, dka_ref, dva_ref, dqb_ref, dkb_ref, dvb_ref,
             dp_ref, dqg_ref, dkg_ref):
        ca, sa, cb, sb = ca_ref[...], sa_ref[...], cb_ref[...], sb_ref[...]
        gmat = _group_matrix(HEAD_DIM)

        @pl.when(pl.program_id(0) == 0)
        def _():
            dqg_ref[...] = jnp.zeros_like(dqg_ref)
            dkg_ref[...] = jnp.zeros_like(dkg_ref)

        def rope_a_bwd(dy):
            return dy * ca + _swap_halves(dy * sa, HEAD_DIM)

        def norm_rope_b_bwd(dout, xs, g):
            dy = dout * cb + _swap_halves(dout * sb, HEAD_DIM // 2)
            r = _headnorm(xs, gmat)
            xhat = xs * r
            dxh = dy * g
            mean = _dot_f32_by_ones(dxh * xhat, gmat) * (1.0 / HEAD_DIM)
            return r * (dxh - xhat * mean), jnp.sum(dy * xhat, axis=0, keepdims=True)

        for c in range(A_Q // LANES):
            sl = slice(c * LANES, (c + 1) * LANES)
            dp_ref[:, OFF_QA + c * LANES:OFF_QA + (c + 1) * LANES] = rope_a_bwd(dqa_ref[:, sl].astype(F32)).astype(BF16)
        dp_ref[:, OFF_KA:OFF_KA + LANES] = rope_a_bwd(dka_ref[0] + dka_ref[1]).astype(BF16)
        dp_ref[:, OFF_VA:OFF_VA + LANES] = (dva_ref[0] + dva_ref[1]).astype(BF16)
        dqg = jnp.zeros((1, LANES), F32)
        for c in range(B_Q // LANES):
            sl = slice(c * LANES, (c + 1) * LANES)
            dx, dg = norm_rope_b_bwd(dqb_ref[:, sl].astype(F32), p_ref[:, OFF_QB + c * LANES:OFF_QB + (c + 1) * LANES], qg_ref[...])
            dp_ref[:, OFF_QB + c * LANES:OFF_QB + (c + 1) * LANES] = dx.astype(BF16)
            dqg = dqg + dg
        dqg_ref[...] += dqg
        dx, dg = norm_rope_b_bwd((dkb_ref[0] + dkb_ref[1]).T, p_ref[:, OFF_KB:OFF_KB + LANES], kg_ref[...])
        dp_ref[:, OFF_KB:OFF_KB + LANES] = dx.astype(BF16)
        dkg_ref[...] += dg
        dp_ref[:, OFF_VB:OFF_VB + LANES] = (dvb_ref[0] + dvb_ref[1]).T.astype(BF16)

        @pl.when(pl.program_id(0) == t // tm - 1)
        def _():
            dqg_ref[...] = dqg_ref[...] + pltpu.roll(dqg_ref[...], HEAD_DIM, 1)
            dkg_ref[...] = dkg_ref[...] + pltpu.roll(dkg_ref[...], HEAD_DIM, 1)

    tab = pl.BlockSpec((tm, LANES), lambda i: (i, 0))
    vec = pl.BlockSpec((1, LANES), lambda i: (0, 0))
    wide = pl.BlockSpec((tm, A_Q), lambda i: (i, 0))
    slab = pl.BlockSpec((2, tm, LANES), lambda i: (0, i, 0))
    per_chunk = dkb.shape[3] // tm
    slab_t = pl.BlockSpec((2, None, LANES, tm), lambda i: (0, i // per_chunk, 0, i % per_chunk))
    full = pl.BlockSpec((tm, ATT_IN), lambda i: (i, 0))
    return pl.pallas_call(
        body, name=name, grid=(t // tm,),
        in_specs=[full, tab, tab, tab, tab, vec, vec, wide, slab, slab, wide, slab_t, slab_t],
        out_specs=[full, vec, vec],
        out_shape=[jax.ShapeDtypeStruct((t, ATT_IN), BF16), jax.ShapeDtypeStruct((1, LANES), F32),
                   jax.ShapeDtypeStruct((1, LANES), F32)],
        compiler_params=_cparams(("arbitrary",)),
    )(proj, cos_a, sin_a, cos_b, sin_b, g2(qn_g), g2(kn_g), dqa, dka, dva, dqb, dkb, dvb)


def _head_to_half(xs, head_half, kv_half):
    low = _lane(xs.shape) < HEAD_DIM
    kept = jnp.where(low if head_half == 0 else jnp.logical_not(low), xs, 0.0)
    return jnp.where(kv_half == head_half, kept, pltpu.roll(kept, HEAD_DIM, 1))


def _halves_to_heads(r0, r1, kv_half):
    low = _lane(r0.shape) < HEAD_DIM
    a = jnp.where(kv_half == 0, r0, pltpu.roll(r0, HEAD_DIM, 1))
    b = jnp.where(kv_half == 1, r1, pltpu.roll(r1, HEAD_DIM, 1))
    return jnp.where(low, a, b)


def attn_delta(o, do, name):
    t, w = o.shape
    tm = min(t, 512)
    n_heads = w // HEAD_DIM

    def body(o_ref, do_ref, d_ref):
        lo, hi = _seg_matrix(0, HEAD_DIM), _seg_matrix(HEAD_DIM, LANES)
        for c in range(w // LANES):
            sl = slice(c * LANES, (c + 1) * LANES)
            s = o_ref[:, sl].astype(F32) * do_ref[:, sl].astype(F32)
            d_ref[2 * c] = _dot_f32_by_ones(s, lo)
            d_ref[2 * c + 1] = _dot_f32_by_ones(s, hi)

    blk = pl.BlockSpec((tm, w), lambda i: (i, 0))
    return pl.pallas_call(
        body, name=name, grid=(t // tm,),
        in_specs=[blk, blk],
        out_specs=pl.BlockSpec((n_heads, tm, LANES), lambda i: (0, i, 0)),
        out_shape=jax.ShapeDtypeStruct((n_heads, t, LANES), F32),
        compiler_params=_cparams(("parallel",)),
    )(o, do)


BAND = 3 * BLOCK


def _band_offsets(rows_rep):
    qi = lax.broadcasted_iota(jnp.int32, (BLOCK, BAND), 0)
    kj = lax.broadcasted_iota(jnp.int32, (BLOCK, BAND), 1)
    return jnp.concatenate([kj - qi] * rows_rep, axis=0)


def _band(n, t, offsets):
    start = pl.multiple_of(jnp.clip((n - 1) * BLOCK, 0, t - BAND), BLOCK)
    return start, jnp.abs(offsets + (start - n * BLOCK)) <= WINDOW


def window_attn_fwd(q, k, v, sink, name, blocks_per_step=8):
    t = q.shape[0]
    assert t >= BAND
    nq = _fit(t // BLOCK, blocks_per_step)
    tq = nq * BLOCK

    def body(sink_ref, q_ref, k_ref, v_ref, o_ref, lse_ref):
        j, n0 = pl.program_id(0), pl.program_id(1)
        kvh = j // 2
        row = lax.broadcasted_iota(jnp.int32, (2 * BLOCK, 1), 0)
        sk = jnp.where(row < BLOCK, sink_ref[2 * j], sink_ref[2 * j + 1]) * LOG2E
        offsets = _band_offsets(2)
        bands, scores = [], []
        for u in range(nq):
            start, ok = _band(n0 * nq + u, t, offsets)
            qf = q_ref[u * BLOCK:(u + 1) * BLOCK, :].astype(F32) * (SCALE * LOG2E)
            qs = jnp.concatenate([_head_to_half(qf, 0, kvh), _head_to_half(qf, 1, kvh)], axis=0).astype(BF16)
            bands.append(pl.ds(start, BAND))
            scores.append(jnp.where(ok, _dot_nt(qs, k_ref[bands[u], :]), NEG))
        soft = []
        for s in scores:
            m = jnp.maximum(jnp.max(s, axis=-1, keepdims=True), sk)
            p = jnp.exp2(s - m)
            soft.append((p.astype(BF16), jnp.sum(p, axis=-1, keepdims=True) + jnp.exp2(sk - m), m))
        for u, (p, denom, m) in enumerate(soft):
            rows = slice(u * BLOCK, (u + 1) * BLOCK)
            o = _dot_nn(p, v_ref[bands[u], :]) / denom
            o_ref[rows, :] = _halves_to_heads(o[:BLOCK], o[BLOCK:], kvh).astype(BF16)
            lse = jnp.broadcast_to(m + jnp.log2(denom), (2 * BLOCK, LANES))
            lse_ref[0, rows, :] = lse[:BLOCK]
            lse_ref[1, rows, :] = lse[BLOCK:]

    qspec = pl.BlockSpec((tq, LANES), lambda j, n: (n, j))
    whole = pl.BlockSpec((t, LANES), lambda j, n: (0, 0))
    return pl.pallas_call(
        body, name=name, grid=(A_HEADS // 2, t // tq),
        in_specs=[pl.BlockSpec(memory_space=pltpu.SMEM), qspec, whole, whole],
        out_specs=[qspec, pl.BlockSpec((2, tq, LANES), lambda j, n: (j, n, 0))],
        out_shape=[jax.ShapeDtypeStruct((t, A_Q + B_Q), BF16), jax.ShapeDtypeStruct((A_HEADS, t, LANES), F32)],
        compiler_params=_cparams(("parallel", "parallel")),
    )(sink, q, k, v)


def window_attn_bwd(q, k, v, sink, do, lse, delta, name, blocks_per_step=4):
    t = q.shape[0]
    assert t >= BAND
    nq = _fit(t // BLOCK, blocks_per_step)
    tq = nq * BLOCK
    grp = A_HEADS // A_KV_HEADS
    gw = grp * HEAD_DIM

    def body(sink_ref, q_ref, do_ref, k_ref, v_ref, lse_ref, dl_ref, dq_ref, dk_ref, dv_ref, ds_ref):
        kvh, n0 = pl.program_id(0), pl.program_id(1)

        @pl.when(n0 == 0)
        def _():
            dk_ref[...] = jnp.zeros_like(dk_ref)
            dv_ref[...] = jnp.zeros_like(dv_ref)
            ds_ref[...] = jnp.zeros_like(ds_ref)

        rid = lax.broadcasted_iota(jnp.int32, (8, LANES), 0)
        upd = jnp.zeros((8, LANES), F32)
        offsets = _band_offsets(grp)
        for u in range(nq):
            rows = slice(u * BLOCK, (u + 1) * BLOCK)
            start, ok = _band(n0 * nq + u, t, offsets)
            band = pl.ds(start, BAND)
            qparts, doparts = [], []
            for hh in range(grp):
                sl = slice((hh // 2) * LANES, (hh // 2 + 1) * LANES)
                qparts.append(_head_to_half(q_ref[rows, sl].astype(F32) * (SCALE * LOG2E), hh % 2, kvh))
                doparts.append(_head_to_half(do_ref[rows, sl].astype(F32), hh % 2, kvh))
            qs = jnp.concatenate(qparts, axis=0).astype(BF16)
            dos = jnp.concatenate(doparts, axis=0).astype(BF16)
            lse_b = jnp.concatenate([lse_ref[hh, rows, :] for hh in range(grp)], axis=0)
            dl_b = jnp.concatenate([dl_ref[hh, rows, :] for hh in range(grp)], axis=0)
            kband, vband = k_ref[band, :], v_ref[band, :]
            s = jnp.where(ok, _dot_nt(qs, kband), NEG)
            p = jnp.exp2(s - lse_b[:, :1])
            dp = _dot_nt(dos, vband)
            dsc = (p * (dp - dl_b[:, :1])).astype(BF16)
            dv_ref[0, band, :] += _dot_tn(p.astype(BF16), dos)
            dk_ref[0, band, :] += _dot_tn(dsc, qs) * LN2
            dq = _dot_nn(dsc, kband) * SCALE
            for c in range(grp // 2):
                dq_ref[rows, c * LANES:(c + 1) * LANES] = _halves_to_heads(
                    dq[2 * c * BLOCK:(2 * c + 1) * BLOCK], dq[(2 * c + 1) * BLOCK:(2 * c + 2) * BLOCK], kvh).astype(dq_ref.dtype)
            for hh in range(grp):
                rs = slice(hh * BLOCK, (hh + 1) * BLOCK)
                tot = jnp.sum(jnp.exp2(sink_ref[kvh * grp + hh] * LOG2E - lse_b[rs]) * dl_b[rs], axis=0, keepdims=True)
                upd = upd + jnp.where(rid == hh, -tot, 0.0)
        ds_ref[0] += upd

    qspec = pl.BlockSpec((tq, gw), lambda kvh, n: (n, kvh))
    whole = pl.BlockSpec((t, LANES), lambda kvh, n: (0, 0))
    stat = pl.BlockSpec((grp, tq, LANES), lambda kvh, n: (kvh, n, 0))
    slab = pl.BlockSpec((1, t, LANES), lambda kvh, n: (kvh, 0, 0))
    return pl.pallas_call(
        body, name=name, grid=(A_KV_HEADS, t // tq),
        in_specs=[pl.BlockSpec(memory_space=pltpu.SMEM), qspec, qspec, whole, whole, stat, stat],
        out_specs=[qspec, slab, slab, pl.BlockSpec((1, 8, LANES), lambda kvh, n: (kvh, 0, 0))],
        out_shape=[jax.ShapeDtypeStruct((t, A_Q), BF16), jax.ShapeDtypeStruct((A_KV_HEADS, t, LANES), F32),
                   jax.ShapeDtypeStruct((A_KV_HEADS, t, LANES), F32), jax.ShapeDtypeStruct((A_KV_HEADS, 8, LANES), F32)],
        compiler_params=_cparams(("arbitrary", "arbitrary")),
    )(sink, q, do, k, v, lse, delta)


def flash_attn_fwd(q, k, v, cat, name, exchange=None, tq=1024, tk=512, ahead=2):
    t = q.shape[0]
    tq, tk = _fit(t, tq), _fit(t, tk)
    nk = t // tk

    def body(q_ref, k_ref, v_ref, cat_ref, o_ref, lse_ref):
        del cat_ref
        kvh = pl.program_id(0) // 2
        qf = q_ref[...].astype(F32) * (SCALE * LOG2E)
        qs = jnp.concatenate([_head_to_half(qf, 0, kvh), _head_to_half(qf, 1, kvh)], axis=0).astype(BF16)
        mine = (_lane((tk, LANES)) < HEAD_DIM) == (kvh == 0)

        def scores(c):
            return _dot_nt(qs, k_ref[c * tk:(c + 1) * tk, :])

        s = [scores(c) for c in range(min(ahead, nk))]
        m = jnp.full((2 * tq, 1), NEG, F32)
        acc = jnp.zeros((2 * tq, LANES), F32)
        for c in range(nk):
            if c + ahead < nk:
                s.append(scores(c + ahead))
            vb = jnp.where(mine, v_ref[c * tk:(c + 1) * tk, :], jnp.ones((), BF16))
            m_new = jnp.maximum(m, jnp.max(s[c], axis=-1, keepdims=True))
            p = jnp.exp2(s[c] - m_new).astype(BF16)
            acc = jnp.exp2(m - m_new) * acc + _dot_nn(p, vb)
            m = m_new
        other = pltpu.roll(acc, HEAD_DIM, 1)
        o = acc / other
        o_ref[...] = _halves_to_heads(o[:tq], o[tq:], kvh).astype(BF16)
        in_mine = (_lane(acc.shape) < HEAD_DIM) == (kvh == 0)
        lse = jnp.broadcast_to(m, acc.shape) + jnp.log2(jnp.where(in_mine, other, acc))
        lse_ref[0] = lse[:tq]
        lse_ref[1] = lse[tq:]

    qspec = pl.BlockSpec((tq, LANES), lambda j, i: (i, j))
    whole = pl.BlockSpec((t, LANES), lambda j, i: (0, 0))
    nj, ni = B_HEADS // 2, t // tq
    steps = lambda: ((pl.program_id(0) == 0) & (pl.program_id(1) == 0), (pl.program_id(0) == nj - 1) & (pl.program_id(1) == ni - 1))
    body, x_in, x_out, x_shapes, x_scratch = carried(body, exchange, 4, 2, steps)
    return pl.pallas_call(
        body, name=name, grid=(nj, ni),
        in_specs=[qspec, whole, whole, _ANY] + x_in,
        out_specs=[pl.BlockSpec((tq, LANES), lambda j, i: (i, A_Q // LANES + j)),
                   pl.BlockSpec((2, tq, LANES), lambda j, i: (j, i, 0))] + x_out,
        out_shape=[jax.ShapeDtypeStruct(cat.shape, BF16), jax.ShapeDtypeStruct((B_HEADS, t, LANES), F32)] + x_shapes,
        scratch_shapes=x_scratch,
        input_output_aliases={3: 0},
        compiler_params=_cparams(("arbitrary", "arbitrary")),
    )(q, k, v, cat, *(exchange.arrays if exchange else ()))


def flash_attn_bwd(q, k, v, do, lse, delta, name, exchange=None, tq=512, tk=512, together=4):
    t = q.shape[0]
    tq, tk = _fit(t, tq), _fit(t, tk)
    nk = t // tk
    together = _fit(nk, together)
    grp = B_HEADS // B_KV_HEADS
    gw = grp * HEAD_DIM

    def body(q_ref, do_ref, k_ref, v_ref, lse_ref, dl_ref, dq_ref, dk_ref, dv_ref, dq_s):
        kvh, i = pl.program_id(0), pl.program_id(1)

        @pl.when(i == 0)
        def _():
            dk_ref[...] = jnp.zeros_like(dk_ref)
            dv_ref[...] = jnp.zeros_like(dv_ref)

        qparts, doparts = [], []
        for hh in range(grp):
            sl = slice((hh // 2) * LANES, (hh // 2 + 1) * LANES)
            qparts.append(_head_to_half(q_ref[:, sl].astype(F32) * (SCALE * LOG2E), hh % 2, kvh))
            doparts.append(_head_to_half(do_ref[:, sl].astype(F32), hh % 2, kvh))
        qf, dof = jnp.concatenate(qparts, axis=0), jnp.concatenate(doparts, axis=0)
        qs, dos = qf.astype(BF16), dof.astype(BF16)
        qs_t, dos_t = qf.T.astype(BF16), dof.T.astype(BF16)
        lse = jnp.tile(jnp.concatenate([lse_ref[hh] for hh in range(grp)], axis=0), (1, tk // LANES))
        dl = jnp.tile(jnp.concatenate([dl_ref[hh] for hh in range(grp)], axis=0), (1, tk // LANES))
        dq_s[...] = jnp.zeros_like(dq_s)

        def chunks(c0, carry):
            cs = [c0 * together + u for u in range(together)]
            kbs = [k_ref[pl.ds(pl.multiple_of(c * tk, tk), tk), :] for c in cs]
            vbs = [v_ref[pl.ds(pl.multiple_of(c * tk, tk), tk), :] for c in cs]
            ss = [_dot_nt(qs, kb) for kb in kbs]
            dps = [_dot_nt(dos, vb) for vb in vbs]
            for c, kb, s, dp in zip(cs, kbs, ss, dps):
                p = jnp.exp2(s - lse)
                dsc = (p * (dp - dl)).astype(BF16)
                dv_ref[0, c] += _dot_nn(dos_t, p.astype(BF16))
                dk_ref[0, c] += _dot_nn(qs_t, dsc) * LN2
                dq_s[...] += _dot_nn(kb.T, dsc.T)
            return carry

        lax.fori_loop(0, nk // together, chunks, 0)
        dq = dq_s[...].T
        for c in range(grp // 2):
            dq_ref[:, c * LANES:(c + 1) * LANES] = (_halves_to_heads(
                dq[2 * c * tq:(2 * c + 1) * tq], dq[(2 * c + 1) * tq:(2 * c + 2) * tq], kvh) * SCALE).astype(dq_ref.dtype)

    qspec = pl.BlockSpec((tq, gw), lambda kvh, i: (i, kvh))
    dospec = pl.BlockSpec((tq, gw), lambda kvh, i: (i, A_Q // gw + kvh))
    whole = pl.BlockSpec((t, LANES), lambda kvh, i: (0, 0))
    stat = pl.BlockSpec((grp, tq, LANES), lambda kvh, i: (kvh, i, 0))
    dlstat = pl.BlockSpec((grp, tq, LANES), lambda kvh, i: (A_HEADS // grp + kvh, i, 0))
    slab = pl.BlockSpec((1, nk, LANES, tk), lambda kvh, i: (kvh, 0, 0, 0))
    ni = t // tq
    steps = lambda: ((pl.program_id(0) == 0) & (pl.program_id(1) == 0),
                     (pl.program_id(0) == B_KV_HEADS - 1) & (pl.program_id(1) == ni - 1))
    body, x_in, x_out, x_shapes, x_scratch = carried(body, exchange, 6, 3, steps)
    return pl.pallas_call(
        body, name=name, grid=(B_KV_HEADS, ni),
        in_specs=[qspec, dospec, whole, whole, stat, dlstat] + x_in,
        out_specs=[qspec, slab, slab] + x_out,
        out_shape=[jax.ShapeDtypeStruct((t, B_Q), BF16), jax.ShapeDtypeStruct((B_KV_HEADS, nk, LANES, tk), F32),
                   jax.ShapeDtypeStruct((B_KV_HEADS, nk, LANES, tk), F32)] + x_shapes,
        scratch_shapes=[pltpu.VMEM((LANES, grp * tq), F32)] + x_scratch,
        compiler_params=_cparams(("arbitrary", "arbitrary")),
    )(q, do, k, v, lse, delta, *(exchange.arrays if exchange else ()))


_GELU_C = math.sqrt(2.0 / math.pi)
_GELU_A = 0.044715


def _gelu(x):
    return 0.5 * x * (1.0 + jnp.tanh(_GELU_C * (x + _GELU_A * x * x * x)))


def _gelu_grad(x):
    th = jnp.tanh(_GELU_C * (x + _GELU_A * x * x * x))
    return 0.5 * (1.0 + th) + 0.5 * x * (1.0 - th * th) * _GELU_C * (1.0 + 3.0 * _GELU_A * x * x)


def _layernorm_stats(vf):
    mu = jnp.mean(vf, axis=-1, keepdims=True)
    vc = vf - mu
    r = lax.rsqrt(jnp.mean(vc * vc, axis=-1, keepdims=True) + EPS)
    return vc * r, r


def sgu_mix_fwd(z, ln_g, ln_b, w_s, b_rows, name):
    t, w2 = z.shape
    w = w2 // 2
    dg = w // SGU_GROUPS

    def body(u_ref, v_ref, g_ref, b_ref, ws_ref, bb_ref, y_ref):
        vhat, _ = _layernorm_stats(v_ref[...].astype(F32))
        vn = (vhat * g_ref[...] + b_ref[...]).astype(BF16)
        for g in range(SGU_GROUPS):
            sl = slice(g * dg, (g + 1) * dg)
            mixed = _dot_nn(ws_ref[g], vn[:, sl]) + bb_ref[g]
            y_ref[:, sl] = (u_ref[:, sl].astype(F32) * mixed).astype(BF16)

    vec = pl.BlockSpec((1, w), lambda n: (0, 0))
    whole = pl.BlockSpec((SGU_GROUPS, SGU_CHUNK, SGU_CHUNK), lambda n: (0, 0, 0))
    return pl.pallas_call(
        body, name=name, grid=(t // SGU_CHUNK,),
        in_specs=[pl.BlockSpec((SGU_CHUNK, w), lambda n: (n, 0)), pl.BlockSpec((SGU_CHUNK, w), lambda n: (n, 1)),
                  vec, vec, whole, whole],
        out_specs=pl.BlockSpec((SGU_CHUNK, w), lambda n: (n, 0)),
        out_shape=jax.ShapeDtypeStruct((t, w), BF16),
        compiler_params=_cparams(("parallel",)),
    )(z, z, ln_g.reshape(1, w), ln_b.reshape(1, w), w_s, b_rows)


def sgu_mix_bwd(z, apre, dy, ln_g, ln_b, w_s, b_rows, name):
    t, w2 = z.shape
    w = w2 // 2
    dg = w // SGU_GROUPS

    def body(u_ref, v_ref, au_ref, av_ref, dy_ref, g_ref, b_ref, ws_ref, bb_ref, da_ref, dlg_ref, dlb_ref, dws_ref, dbs_ref):
        @pl.when(pl.program_id(0) == 0)
        def _():
            dlg_ref[...] = jnp.zeros_like(dlg_ref)
            dlb_ref[...] = jnp.zeros_like(dlb_ref)
            dws_ref[...] = jnp.zeros_like(dws_ref)
            dbs_ref[...] = jnp.zeros_like(dbs_ref)

        vhat, r = _layernorm_stats(v_ref[...].astype(F32))
        gam = g_ref[...]
        vn = (vhat * gam + b_ref[...]).astype(BF16)
        ones8 = jnp.ones((8, dg), BF16)
        rid = lax.broadcasted_iota(jnp.int32, (8, SGU_CHUNK), 0)
        dbs = jnp.zeros((8, SGU_CHUNK), F32)
        dvn_parts = []
        for g in range(SGU_GROUPS):
            sl = slice(g * dg, (g + 1) * dg)
            dyg = dy_ref[:, sl].astype(F32)
            mixed = _dot_nn(ws_ref[g], vn[:, sl]) + bb_ref[g]
            da_ref[:, sl] = (dyg * mixed * _gelu_grad(au_ref[:, sl].astype(F32))).astype(BF16)
            dmix = dyg * u_ref[:, sl].astype(F32)
            dm_hi = dmix.astype(BF16)
            dm_lo = (dmix - dm_hi.astype(F32)).astype(BF16)
            dws_ref[g] += _dot_nt(dm_hi, vn[:, sl])
            dbs = dbs + jnp.where(rid == g, _dot_nt(ones8, dm_hi) + _dot_nt(ones8, dm_lo), 0.0)
            dvn_parts.append(_dot_tn(ws_ref[g], dm_hi))
        dbs_ref[...] += dbs
        dvn = jnp.concatenate(dvn_parts, axis=1)
        dlg_ref[...] += jnp.sum(dvn * vhat, axis=0, keepdims=True)
        dlb_ref[...] += jnp.sum(dvn, axis=0, keepdims=True)
        dvh = dvn * gam
        dv = r * (dvh - jnp.mean(dvh, axis=-1, keepdims=True) - vhat * jnp.mean(dvh * vhat, axis=-1, keepdims=True))
        da_ref[:, w:] = (dv * _gelu_grad(av_ref[...].astype(F32))).astype(BF16)

    vec = pl.BlockSpec((1, w), lambda n: (0, 0))
    whole = pl.BlockSpec((SGU_GROUPS, SGU_CHUNK, SGU_CHUNK), lambda n: (0, 0, 0))
    left = pl.BlockSpec((SGU_CHUNK, w), lambda n: (n, 0))
    right = pl.BlockSpec((SGU_CHUNK, w), lambda n: (n, 1))
    return pl.pallas_call(
        body, name=name, grid=(t // SGU_CHUNK,),
        in_specs=[left, right, left, right, left, vec, vec, whole, whole],
        out_specs=[pl.BlockSpec((SGU_CHUNK, w2), lambda n: (n, 0)), vec, vec, whole,
                   pl.BlockSpec((SGU_GROUPS, SGU_CHUNK), lambda n: (0, 0))],
        out_shape=[jax.ShapeDtypeStruct((t, w2), BF16), jax.ShapeDtypeStruct((1, w), F32), jax.ShapeDtypeStruct((1, w), F32),
                   jax.ShapeDtypeStruct((SGU_GROUPS, SGU_CHUNK, SGU_CHUNK), F32),
                   jax.ShapeDtypeStruct((SGU_GROUPS, SGU_CHUNK), F32)],
        compiler_params=_cparams(("arbitrary",)),
    )(z, z, apre, apre, dy, ln_g.reshape(1, w), ln_b.reshape(1, w), w_s, b_rows)


def loss_head(h, g, target, name):
    t, d = h.shape
    tm = min(t, 512)

    def body(h_ref, g_ref, t_ref, loss_ref, dh_ref, dhb_ref, dg_ref):
        @pl.when(pl.program_id(0) == 0)
        def _():
            loss_ref[...] = jnp.zeros_like(loss_ref)
            dg_ref[...] = jnp.zeros_like(dg_ref)

        xf = h_ref[...]
        r = lax.rsqrt(jnp.mean(xf * xf, axis=-1, keepdims=True) + EPS)
        xhat = xf * r
        err = xhat * g_ref[...] - t_ref[...]
        per_tok = jnp.mean(err * err, axis=-1, keepdims=True)
        loss_ref[...] += 0.5 * jnp.sum(per_tok, axis=0, keepdims=True)
        dy = err * (1.0 / d)
        dg_ref[...] += jnp.sum(dy * xhat, axis=0, keepdims=True)
        dxh = dy * g_ref[...]
        dh = r * (dxh - xhat * jnp.mean(dxh * xhat, axis=-1, keepdims=True))
        dh_ref[...] = dh
        dhb_ref[...] = dh.astype(BF16)

    row = pl.BlockSpec((tm, d), lambda i: (i, 0))
    vec = pl.BlockSpec((1, d), lambda i: (0, 0))
    return pl.pallas_call(
        body, name=name, grid=(t // tm,),
        in_specs=[row, vec, row],
        out_specs=[pl.BlockSpec((1, LANES), lambda i: (0, 0)), row, row, vec],
        out_shape=[jax.ShapeDtypeStruct((1, LANES), F32), jax.ShapeDtypeStruct((t, d), F32), jax.ShapeDtypeStruct((t, d), BF16),
                   jax.ShapeDtypeStruct((1, d), F32)],
        compiler_params=_cparams(("arbitrary",)),
    )(h, g.reshape(1, d), target)


ADAMW_BLOCK_BYTES = 1 << 20


def adamw(parts, w, m, v, name):
    n_layers, r, c = w.shape
    row_bytes = n_layers * c * 4
    if r * row_bytes <= 2 * ADAMW_BLOCK_BYTES:
        tr = r
    else:
        tr = _fit(r, 1 << int(math.log2(max(8, ADAMW_BLOCK_BYTES // row_bytes))))
    bc1 = 1.0 - ADAM_B1 ** ADAM_STEP
    bc2 = 1.0 - ADAM_B2 ** ADAM_STEP

    def body(*refs):
        p_refs = refs[:n_layers]
        w_ref, m_ref, v_ref, g_ref, d_ref, nm_ref, nv_ref = refs[n_layers:]
        for l in range(n_layers):
            g = p_refs[l][0].astype(F32)
            for j in range(1, N_DEV):
                g = g + p_refs[l][j].astype(F32)
            nm = ADAM_B1 * m_ref[l] + (1.0 - ADAM_B1) * g
            nv = ADAM_B2 * v_ref[l] + (1.0 - ADAM_B2) * (g * g)
            g_ref[l] = g
            nm_ref[l] = nm
            nv_ref[l] = nv
            d_ref[l] = -ADAM_LR * ((nm / bc1) / (jnp.sqrt(nv / bc2) + ADAM_EPS) + ADAM_WD * w_ref[l])

    blk = pl.BlockSpec((n_layers, tr, c), lambda i: (0, i, 0))
    return pl.pallas_call(
        body, name=name, grid=(r // tr,),
        in_specs=[pl.BlockSpec((N_DEV, tr, c), lambda i: (0, i, 0))] * n_layers + [blk, blk, blk],
        out_specs=[blk] * 4,
        out_shape=[jax.ShapeDtypeStruct((n_layers, r, c), F32)] * 4,
        compiler_params=_cparams(("parallel",)),
    )(*parts, w, m, v)


_ANY = pl.BlockSpec(memory_space=pl.ANY)


def _mesh_pos():
    return lax.axis_index("x"), lax.axis_index("y"), lax.axis_index("c")


class Exchange:
    def __init__(self, gathers=(), scatters=()):
        self.items = [("gather", a) for a in gathers] + [("scatter", a) for a in scatters]
        self.arrays = [a for _, a in self.items]
        self.n = len(self.items)

    def out_shapes(self):
        return [jax.ShapeDtypeStruct(((N_DEV,) + a.shape) if kind == "gather" else a.shape, a.dtype) for kind, a in self.items]

    def scratch(self):
        return [pltpu.SemaphoreType.DMA((7 * self.n,)), pltpu.SemaphoreType.DMA((7 * self.n,)), pltpu.SemaphoreType.DMA((self.n,))]

    def _copies(self, in_refs, out_refs, send_sems, recv_sems, local_sems):
        x, y, c = _mesh_pos()
        me = 4 * x + 2 * y + c
        local, sends, arrivals = [], [], []
        for t, (kind, _) in enumerate(self.items):
            src_of = (lambda slot, r=in_refs[t]: r) if kind == "gather" else (lambda slot, r=in_refs[t]: r.at[slot])
            local.append(pltpu.make_async_copy(src_of(me), out_refs[t].at[me], local_sems.at[t]))
            for k in range(1, N_DEV):
                px = 1 - x if k & 4 else x
                py = 1 - y if k & 2 else y
                pc = 1 - c if k & 1 else c
                pid = 4 * px + 2 * py + pc
                kw = dict(send_sem=send_sems.at[7 * t + k - 1], recv_sem=recv_sems.at[7 * t + k - 1],
                          device_id=(px, py, pc), device_id_type=pl.DeviceIdType.MESH)
                sends.append(pltpu.make_async_remote_copy(src_ref=src_of(pid), dst_ref=out_refs[t].at[me], **kw))
                arrivals.append(pltpu.make_async_remote_copy(src_ref=src_of(pid), dst_ref=out_refs[t].at[pid], **kw))
        return local, sends, arrivals

    def start(self, *refs):
        local, sends, _ = self._copies(*refs)
        for cp in local + sends:
            cp.start()

    def wait(self, *refs):
        local, sends, arrivals = self._copies(*refs)
        for cp in arrivals:
            cp.wait_recv()
        for cp in sends:
            cp.wait_send()
        for cp in local:
            cp.wait()


def carried(body, exchange, n_in, n_out, first_last):
    if exchange is None:
        return body, [], [], [], []
    nx = exchange.n

    def wrapped(*refs):
        ins, xin = refs[:n_in], refs[n_in:n_in + nx]
        outs, xout = refs[n_in + nx:n_in + nx + n_out], refs[n_in + nx + n_out:n_in + 2 * nx + n_out]
        scratch, sems = refs[n_in + 2 * nx + n_out:-3], refs[-3:]
        first, last = first_last()

        @pl.when(first)
        def _():
            exchange.start(xin, xout, *sems)

        body(*ins, *outs, *scratch)

        @pl.when(last)
        def _():
            exchange.wait(xin, xout, *sems)

    return wrapped, [_ANY] * nx, [_ANY] * nx, exchange.out_shapes(), exchange.scratch()


def exchange_only(exchange, name):
    def body(*refs):
        xin, xout, sems = refs[:exchange.n], refs[exchange.n:2 * exchange.n], refs[-3:]
        exchange.start(xin, xout, *sems)
        exchange.wait(xin, xout, *sems)

    return pl.pallas_call(
        body, name=name, in_specs=[_ANY] * exchange.n, out_specs=[_ANY] * exchange.n,
        out_shape=exchange.out_shapes(), scratch_shapes=exchange.scratch(),
    )(*exchange.arrays)


def _residual_out(a, w_out, x, next_g, name, **tiles):
    if next_g is None:
        (y,) = matmul(a, w_out, "nn", name, [F32], epilogue=lambda acc, r: (r + acc,), extras=(x,), **tiles)
        return y, None

    def add_and_norm(acc, r, g):
        y = r + acc
        return y, y * lax.rsqrt(jnp.mean(y * y, axis=-1, keepdims=True) + EPS) * g

    assert w_out.shape[1] <= tiles.get("tn", 1024)
    return matmul(a, w_out, "nn", name, [F32, BF16], epilogue=add_and_norm, extras=(x, next_g.reshape(1, -1)), **tiles)


def attention_fwd(x, h, w_in, sink, qn_g, kn_g, w_out, tables, next_g, tag, exchange=None):
    (proj,) = matmul(h, w_in, "nn", f"{tag}_proj", [F32], tn=ATT_IN)
    qa, ka, va, qb, kb, vb = qkv_post_fwd(proj, tables, qn_g, kn_g, f"{tag}_qkv")
    cat, lse_a = window_attn_fwd(qa, ka, va, sink, f"{tag}_win")
    cat, lse_b, *arrived = flash_attn_fwd(qb, kb, vb, cat, f"{tag}_flash", exchange)
    if callable(w_out):
        w_out = w_out(arrived)
    y, h_next = _residual_out(cat, w_out, x, next_g, f"{tag}_out")
    saved = (x, h, proj, qa, ka, va, qb, kb, vb, cat, lse_a, lse_b)
    return y, h_next, saved, arrived


def attention_bwd(dy, dyb, saved, norm_g, w_in, sink, qn_g, kn_g, w_out, tables, tag, exchange_with=None):
    x, h, proj, qa, ka, va, qb, kb, vb, cat, lse_a, lse_b = saved
    (dcat,) = matmul(dyb, w_out, "nt", f"{tag}_dcat", [BF16])
    (dw_out,) = matmul(cat, dyb, "tn", f"{tag}_dwout", [BF16], tk=4096)
    delta = attn_delta(cat, dcat, f"{tag}_delta")
    dqa, dka, dva, dsink = window_attn_bwd(qa, ka, va, sink, dcat, lse_a, delta, f"{tag}_dwin")
    exchange = exchange_with(dw_out) if exchange_with else None
    dqb, dkb, dvb, *arrived = flash_attn_bwd(qb, kb, vb, dcat, lse_b, delta, f"{tag}_dflash", exchange)
    dproj, dqg, dkg = qkv_post_bwd(proj, tables, qn_g, kn_g, dqa, dka, dva, dqb, dkb, dvb, f"{tag}_dqkv")
    (dw_in,) = matmul(h, dproj, "tn", f"{tag}_dwin_w", [BF16], tn=ATT_IN // 2, tk=2048)
    dx, dxb, dg = matmul_nt_normbwd(dproj, w_in, x, norm_g, dy, f"{tag}_dx")
    grp = A_HEADS // A_KV_HEADS
    small = dict(norm=dg[0], sink=dsink[:, :grp, 0].reshape(A_HEADS), qnorm=dqg[0, :HEAD_DIM], knorm=dkg[0, :HEAD_DIM])
    return dx, dxb, dw_in, dw_out, small, arrived


def sgu_fwd(x, h, w_in, ln_g, ln_b, w_s, b_rows, w_out, next_g, tag):
    apre, z = matmul(h, w_in, "nn", f"{tag}_in", [BF16, BF16], epilogue=lambda acc: (acc, _gelu(acc)))
    y = sgu_mix_fwd(z, ln_g, ln_b, w_s, b_rows, f"{tag}_mix")
    out, h_next = _residual_out(y, w_out, x, next_g, f"{tag}_out")
    return out, h_next, (x, h, apre, z, y)


def sgu_bwd(dout, doutb, saved, norm_g, w_in, ln_g, ln_b, w_s, b_rows, w_out, tag):
    x, h, apre, z, y = saved
    (dy,) = matmul(doutb, w_out, "nt", f"{tag}_dy", [BF16])
    (dw_out,) = matmul(y, doutb, "tn", f"{tag}_dwout", [BF16], tk=4096)
    dapre, dlg, dlb, dws, dbs = sgu_mix_bwd(z, apre, dy, ln_g, ln_b, w_s, b_rows, f"{tag}_dmix")
    (dw_in,) = matmul(h, dapre, "tn", f"{tag}_dwin", [BF16], out_shards=True, tk=4096)
    dx, dxb, dg = matmul_nt_normbwd(dapre, w_in, x, norm_g, dout, f"{tag}_dx")
    small = dict(norm=dg[0], ln_g=dlg[0], ln_b=dlb[0], w_s=dws, b_s=dbs)
    return dx, dxb, dw_in, dw_out, small


def _square(r):
    return r * r


def mlp_fwd(x, h, w1, w2, next_g, tag):
    (r,) = matmul(h, w1, "nn", f"{tag}_up", [BF16], epilogue=lambda acc: (jnp.maximum(acc, 0.0),), tm=2048)
    y, h_next = _residual_out(r, w2, x, next_g, f"{tag}_down", a_fn=_square, tm=512, tk=4096)
    return y, h_next, (x, h, r)


def mlp_bwd(dy, dyb, saved, norm_g, w1, w2, tag):
    x, h, r = saved
    (da,) = matmul(dyb, w2, "nt", f"{tag}_da", [BF16], epilogue=lambda acc, rr: (acc * (2.0 * rr.astype(F32)),), extras=(r,),
                   tm=2048)
    (dw2,) = matmul(r, dyb, "tn", f"{tag}_dw2", [BF16], a_fn=_square, tk=4096)
    (dw1,) = matmul(h, da, "tn", f"{tag}_dw1", [BF16], out_shards=True, tk=4096)
    dx, dxb, dg = matmul_nt_normbwd(da, w1, x, norm_g, dy, f"{tag}_dx")
    return dx, dxb, dw1, dw2, dg[0]


ORDER = ("att_norm", "att_w_in", "att_sink", "att_qnorm", "att_knorm", "att_w_out", "sgu_norm", "sgu_w_in", "sgu_ln_g",
         "sgu_ln_b", "sgu_w_s", "sgu_b_s", "sgu_w_out", "mlp_norm", "mlp_w1", "mlp_w2", "final_norm")
SHARDED = ("att_w_in", "att_w_out", "sgu_w_in", "sgu_w_out", "mlp_w1", "mlp_w2")
SGU_VECS = ("sgu_norm", "sgu_ln_g", "sgu_ln_b")
SMALL_EARLY = ("sgu_w_s", "sgu_b_s", "mlp_norm", "final_norm", "loss")
SMALL_LATE = ("att_norm", "att_sink", "att_qnorm", "att_knorm")
SMALL_ROWS_MULT = 8


def _flat(blocks, names):
    flat = jnp.concatenate([blocks[n].reshape(-1).astype(F32) for n in names])
    per = SMALL_ROWS_MULT * FLAT_COLS
    total = -(-flat.shape[0] // per) * per
    return jnp.pad(flat, (0, total - flat.shape[0])).reshape(1, total // FLAT_COLS, FLAT_COLS)


def _unflat(flat, like, names):
    out, off = {}, 0
    f = flat.reshape(-1)
    for n in names:
        size = like[n].size
        out[n] = f[off:off + size].reshape(like[n].shape)
        off += size
    return out


def kernel(x, att_norm, att_w_in, att_sink, att_qnorm, att_knorm, att_w_out, sgu_norm, sgu_w_in, sgu_ln_g, sgu_ln_b, sgu_w_s, sgu_b_s, sgu_w_out, mlp_norm, mlp_w1, mlp_w2, final_norm, loss_target, m_att_norm, m_att_w_in, m_att_sink, m_att_qnorm, m_att_knorm, m_att_w_out, m_sgu_norm, m_sgu_w_in, m_sgu_ln_g, m_sgu_ln_b, m_sgu_w_s, m_sgu_b_s, m_sgu_w_out, m_mlp_norm, m_mlp_w1, m_mlp_w2, m_final_norm, v_att_norm, v_att_w_in, v_att_sink, v_att_qnorm, v_att_knorm, v_att_w_out, v_sgu_norm, v_sgu_w_in, v_sgu_ln_g, v_sgu_ln_b, v_sgu_w_s, v_sgu_b_s, v_sgu_w_out, v_mlp_norm, v_mlp_w1, v_mlp_w2, v_final_norm):
    w = dict(att_norm=att_norm, att_w_in=att_w_in, att_sink=att_sink, att_qnorm=att_qnorm, att_knorm=att_knorm,
             att_w_out=att_w_out, sgu_norm=sgu_norm, sgu_w_in=sgu_w_in, sgu_ln_g=sgu_ln_g, sgu_ln_b=sgu_ln_b, sgu_w_s=sgu_w_s,
             sgu_b_s=sgu_b_s, sgu_w_out=sgu_w_out, mlp_norm=mlp_norm, mlp_w1=mlp_w1, mlp_w2=mlp_w2, final_norm=final_norm)
    m = dict(att_norm=m_att_norm, att_w_in=m_att_w_in, att_sink=m_att_sink, att_qnorm=m_att_qnorm, att_knorm=m_att_knorm,
             att_w_out=m_att_w_out, sgu_norm=m_sgu_norm, sgu_w_in=m_sgu_w_in, sgu_ln_g=m_sgu_ln_g, sgu_ln_b=m_sgu_ln_b,
             sgu_w_s=m_sgu_w_s, sgu_b_s=m_sgu_b_s, sgu_w_out=m_sgu_w_out, mlp_norm=m_mlp_norm, mlp_w1=m_mlp_w1, mlp_w2=m_mlp_w2,
             final_norm=m_final_norm)
    v = dict(att_norm=v_att_norm, att_w_in=v_att_w_in, att_sink=v_att_sink, att_qnorm=v_att_qnorm, att_knorm=v_att_knorm,
             att_w_out=v_att_w_out, sgu_norm=v_sgu_norm, sgu_w_in=v_sgu_w_in, sgu_ln_g=v_sgu_ln_g, sgu_ln_b=v_sgu_ln_b,
             sgu_w_s=v_sgu_w_s, sgu_b_s=v_sgu_b_s, sgu_w_out=v_sgu_w_out, mlp_norm=v_mlp_norm, mlp_w1=v_mlp_w1, mlp_w2=v_mlp_w2,
             final_norm=v_final_norm)
    loss, grad_x, g, d, nm, nv = train_step(x[0], loss_target[0], w, m, v)
    return (loss, grad_x[None], *[g[n] for n in ORDER], *[d[n] for n in ORDER], *[nm[n] for n in ORDER], *[nv[n] for n in ORDER])


def train_step(x, target, w, m, v):
    t, d_model = x.shape
    n_att, n_sgu, depth = w["att_w_in"].shape[0], w["sgu_w_in"].shape[0], w["mlp_w1"].shape[0]
    bf = lambda n: w[n].astype(BF16)

    assert n_sgu == n_att and depth == 2 * n_att
    vec_local = jnp.stack([w[n] for n in SGU_VECS], axis=1)
    att_in = bf("att_w_in")
    h, g_in0, g_vec = rmsnorm_fwd(x, w["att_norm"][0], "att0_norm", Exchange(gathers=[att_in[:1], vec_local]))
    vecs = g_vec.transpose(1, 2, 0, 3).reshape(n_sgu, len(SGU_VECS), -1)
    group_names = ("att_w_out", "sgu_w_in", "sgu_w_out", "mlp_w1", "mlp_w2")
    per = {n: w[n].shape[0] // n_att for n in group_names}
    rest = [Exchange(gathers=([att_in[1:]] if gi == 0 else []) + [bf(n)[gi * per[n]:(gi + 1) * per[n]] for n in group_names])
            for gi in range(n_att)]
    gathered = {}

    def weight(name, kind, layer):
        return Gathered(gathered[name][layer // per[name]], kind, layer % per[name])

    w_s_bf = w["sgu_w_s"].astype(BF16)
    b_rows = jnp.broadcast_to(w["sgu_b_s"][:, :, :, None], w["sgu_b_s"].shape + (LANES,))
    tables = _rope_tables(t)
    full_cols = lambda g: g.transpose(1, 2, 0, 3).reshape(g.shape[1], d_model, -1)

    mixer_norm = lambda layer: w["att_norm"][layer // 2] if layer % 2 == 0 else vecs[layer // 2, 0]
    saved = []
    for layer in range(depth):
        i = layer // 2
        if layer % 2 == 0:
            if layer == 0:
                att_w_in = [full_cols(g_in0)[0]]
            first = 1 if layer == 0 else 0
            x, h, sv, arrived = attention_fwd(x, h, att_w_in[i], w["att_sink"][i], w["att_qnorm"][i], w["att_knorm"][i],
                                              lambda arrived, first=first: Gathered(arrived[first], "row", 0),
                                              tables, w["mlp_norm"][layer], f"att{i}", rest[i])
            if layer == 0:
                att_w_in += list(full_cols(arrived[0]))
            for n, g in zip(group_names, arrived[first:]):
                gathered.setdefault(n, []).append(g)
        else:
            x, h, sv = sgu_fwd(x, h, weight("sgu_w_in", "col", i), vecs[i, 1], vecs[i, 2], w_s_bf[i], b_rows[i],
                               weight("sgu_w_out", "row", i), w["mlp_norm"][layer], f"sgu{i}")
        x, h, sm = mlp_fwd(x, h, weight("mlp_w1", "col", layer), weight("mlp_w2", "row", layer),
                           mixer_norm(layer + 1) if layer + 1 < depth else None, f"mlp{layer}")
        saved.append((sv, sm))
    loss_row, dh, dhb, dgf = loss_head(x, w["final_norm"], target, "loss_head")

    queue, recv = [], {}
    gs = dict(att_norm=[None] * n_att, att_sink=[None] * n_att, att_qnorm=[None] * n_att, att_knorm=[None] * n_att,
              sgu_w_s=[None] * n_sgu, sgu_b_s=[None] * n_sgu, mlp_norm=[None] * depth)

    def row_slabs(g):
        return g.reshape(N_DEV, g.shape[0] // N_DEV, g.shape[1])

    def col_slabs(g):
        return g.reshape(g.shape[0], N_DEV, g.shape[1] // N_DEV).transpose(1, 0, 2)

    def take_queue(gathers=()):
        items = list(queue)
        queue.clear()
        keys = [k for k, _ in gathers] + [k for k, _ in items]
        return Exchange(gathers=[a for _, a in gathers], scatters=[a for _, a in items]), keys

    def small_early():
        blocks = dict(sgu_w_s=jnp.stack(gs["sgu_w_s"]), sgu_b_s=jnp.stack(gs["sgu_b_s"]), mlp_norm=jnp.stack(gs["mlp_norm"]),
                      final_norm=dgf[0], loss=loss_row[0, :1])
        return _flat(blocks, SMALL_EARLY)[0]

    for layer in reversed(range(depth)):
        i = layer // 2
        sv, sm = saved[layer]
        dh, dhb, dw1, dw2, gs["mlp_norm"][layer] = mlp_bwd(
            dh, dhb, sm, w["mlp_norm"][layer], weight("mlp_w1", "col", layer), weight("mlp_w2", "row", layer), f"mlp{layer}")
        queue += [(("mlp_w1", layer), dw1), (("mlp_w2", layer), row_slabs(dw2))]
        if layer % 2 == 0:
            keys = []

            def exchange_with(dw_out, i=i, layer=layer, keys=keys):
                if layer == 0:
                    queue.append((("att_w_out", i), row_slabs(dw_out)))
                ex, got = take_queue([("small_early", small_early())] if layer == 0 else ())
                keys += got
                return ex

            dh, dhb, dw_in, dw_out, sm_g, arrived = attention_bwd(
                dh, dhb, sv, w["att_norm"][i], att_w_in[i], w["att_sink"][i], w["att_qnorm"][i], w["att_knorm"][i],
                weight("att_w_out", "row", i), tables, f"att{i}", exchange_with)
            recv.update(zip(keys, arrived))
            queue.append((("att_w_in", i), col_slabs(dw_in)))
            if layer != 0:
                queue.append((("att_w_out", i), row_slabs(dw_out)))
            gs["att_norm"][i], gs["att_sink"][i] = sm_g["norm"], sm_g["sink"]
            gs["att_qnorm"][i], gs["att_knorm"][i] = sm_g["qnorm"], sm_g["knorm"]
        else:
            dh, dhb, dw_in, dw_out, sm_g = sgu_bwd(
                dh, dhb, sv, vecs[i, 0], weight("sgu_w_in", "col", i), vecs[i, 1], vecs[i, 2], w_s_bf[i], b_rows[i],
                weight("sgu_w_out", "row", i), f"sgu{i}")
            dvec = jnp.stack([sm_g["norm"], sm_g["ln_g"], sm_g["ln_b"]])
            queue += [(("sgu_w_in", i), dw_in), (("sgu_w_out", i), row_slabs(dw_out)), (("sgu_vecs", i), col_slabs(dvec))]
            gs["sgu_w_s"][i], gs["sgu_b_s"][i] = sm_g["w_s"], sm_g["b_s"]
    grad_x = dh
    late = dict(att_norm=jnp.stack(gs["att_norm"]), att_sink=jnp.stack(gs["att_sink"]), att_qnorm=jnp.stack(gs["att_qnorm"]),
                att_knorm=jnp.stack(gs["att_knorm"]))
    last, keys = take_queue([("small_late", _flat(late, SMALL_LATE)[0])])
    recv.update(zip(keys, exchange_only(last, "exchange_last")))

    outs = [{}, {}, {}, {}]
    for n in SHARDED:
        res = adamw([recv[(n, l)] for l in range(w[n].shape[0])], w[n], m[n], v[n], f"adamw_{n}")
        for o, r in zip(outs, res):
            o[n] = r
    stack_vecs = lambda src: jnp.stack([src[n] for n in SGU_VECS], axis=1)
    res = adamw([recv[("sgu_vecs", i)] for i in range(n_sgu)], stack_vecs(w), stack_vecs(m), stack_vecs(v), "adamw_sgu_vecs")
    for o, r in zip(outs, res):
        o.update({n: r[:, k] for k, n in enumerate(SGU_VECS)})
    zero = {"loss": jnp.zeros((1,), F32)}
    for names, key in ((SMALL_EARLY, "small_early"), (SMALL_LATE, "small_late")):
        res = adamw([recv[key]], _flat({**w, **zero}, names), _flat({**m, **zero}, names), _flat({**v, **zero}, names), f"adamw_{key}")
        for o, r in zip(outs, res):
            o.update(_unflat(r, {**w, **zero}, names))
    loss = outs[0]["loss"][0]
    return loss, grad_x, *outs
```

```python
import math

import jax
import jax.numpy as jnp
from jax import lax
from jax.experimental import pallas as pl
from jax.experimental.pallas import tpu as pltpu

F32 = jnp.float32
BF16 = jnp.bfloat16

HEAD_DIM = 64
A_HEADS = 8
A_KV_HEADS = 2
B_HEADS = 8
B_KV_HEADS = 2
WINDOW = 128
BLOCK = 128
ROPE_THETA = 10000.0
GRID_W = 64
SGU_GROUPS = 8
SGU_CHUNK = 128
EPS = 1e-6
SCALE = HEAD_DIM ** -0.5
NEG = -1e30
LOG2E = math.log2(math.e)
LN2 = math.log(2.0)

A_Q = A_HEADS * HEAD_DIM
A_KV = A_KV_HEADS * HEAD_DIM
B_Q = B_HEADS * HEAD_DIM
B_KV = B_KV_HEADS * HEAD_DIM
OFF_QA, OFF_KA, OFF_VA = 0, A_Q, A_Q + A_KV
OFF_QB = A_Q + 2 * A_KV
OFF_KB = OFF_QB + B_Q
OFF_VB = OFF_KB + B_KV
ATT_IN = OFF_VB + B_KV

ADAM_LR = 0.001
ADAM_B1 = 0.9
ADAM_B2 = 0.999
ADAM_EPS = 1e-08
ADAM_WD = 0.01
ADAM_STEP = 10

N_DEV = 8
LANES = 128
V7X_VMEM_LIMIT = 56 * 1024 * 1024
FLAT_COLS = 1024


def _cparams(sem, vmem=V7X_VMEM_LIMIT):
    return pltpu.CompilerParams(dimension_semantics=sem, vmem_limit_bytes=vmem)


def _dot_nn(a, b):
    return lax.dot_general(a, b, (((1,), (0,)), ((), ())), preferred_element_type=F32)


def _dot_nt(a, b):
    return lax.dot_general(a, b, (((1,), (1,)), ((), ())), preferred_element_type=F32)


def _dot_tn(a, b):
    return lax.dot_general(a, b, (((0,), (0,)), ((), ())), preferred_element_type=F32)


def _bf(x):
    return x if x.dtype == BF16 else x.astype(BF16)


def _lane(shape):
    return lax.broadcasted_iota(jnp.int32, shape, len(shape) - 1)


def _seg_matrix(rows_lo, rows_hi):
    r = lax.broadcasted_iota(jnp.int32, (LANES, LANES), 0)
    return jnp.where((r >= rows_lo) & (r < rows_hi), 1.0, 0.0).astype(BF16)


def _group_matrix(width):
    r = lax.broadcasted_iota(jnp.int32, (LANES, LANES), 0)
    c = lax.broadcasted_iota(jnp.int32, (LANES, LANES), 1)
    return jnp.where((r // width) == (c // width), 1.0, 0.0).astype(BF16)


def _dot_f32_by_ones(s, ones_bf16):
    hi = s.astype(BF16)
    lo = (s - hi.astype(F32)).astype(BF16)
    return _dot_nn(hi, ones_bf16) + _dot_nn(lo, ones_bf16)


def _swap_halves(x, width):
    half = width // 2
    first = (_lane(x.shape) % width) < half
    return jnp.where(first, pltpu.roll(x, LANES - half, 1), pltpu.roll(x, half, 1))


def rmsnorm_fwd(x, g, name, exchange=None):
    t, d = x.shape
    tm = min(t, 512)
    n = t // tm

    def body(x_ref, g_ref, h_ref):
        xf = x_ref[...]
        r = lax.rsqrt(jnp.mean(xf * xf, axis=-1, keepdims=True) + EPS)
        h_ref[...] = (xf * r * g_ref[...]).astype(BF16)

    body, x_in, x_out, x_shapes, x_scratch = carried(
        body, exchange, 2, 1, lambda: (pl.program_id(0) == 0, pl.program_id(0) == n - 1))
    return pl.pallas_call(
        body, name=name, grid=(n,),
        in_specs=[pl.BlockSpec((tm, d), lambda i: (i, 0)), pl.BlockSpec((1, d), lambda i: (0, 0))] + x_in,
        out_specs=[pl.BlockSpec((tm, d), lambda i: (i, 0))] + x_out,
        out_shape=[jax.ShapeDtypeStruct((t, d), BF16)] + x_shapes,
        scratch_shapes=x_scratch,
        compiler_params=_cparams(("arbitrary",)),
    )(x, g.reshape(1, d), *(exchange.arrays if exchange else ()))


def _fit(n, want):
    t = min(n, want)
    while n % t:
        t //= 2
    return t


class Gathered:
    def __init__(self, arr, kind, layer):
        self.arr, self.kind, self.layer = arr, kind, layer
        _, _, self.rows, self.cols = arr.shape
        self.shape = (N_DEV * self.rows, self.cols) if kind == "row" else (self.rows, N_DEV * self.cols)


def _b_operand(b, mode, tn, tk, idx):
    dot = {"nn": _dot_nn, "nt": _dot_nt, "tn": _dot_tn}[mode]
    if not isinstance(b, Gathered):
        if mode == "nt":
            spec = pl.BlockSpec((tn, tk), lambda *g: idx(*g))
        else:
            spec = pl.BlockSpec((tk, tn), lambda *g: idx(*g)[::-1])
        return b, spec, lambda av, ref: dot(av, _bf(ref[...]))
    lay, rows, cols = b.layer, b.rows, b.cols
    if mode == "nn" and b.kind == "col":
        s = tn // cols
        assert s * cols == tn
        spec = pl.BlockSpec((s, None, tk, cols), lambda *g: (idx(*g)[0], lay, idx(*g)[1], 0))
        return b.arr, spec, lambda av, ref: jnp.concatenate([_dot_nn(av, ref[c]) for c in range(s)], axis=1)
    if mode == "nn" and b.kind == "row":
        s = tk // rows
        assert s * rows == tk
        spec = pl.BlockSpec((s, None, rows, tn), lambda *g: (idx(*g)[1], lay, 0, idx(*g)[0]))
        return b.arr, spec, lambda av, ref: _dot_nn(av, ref[...].reshape(s * rows, tn))
    if mode == "nt" and b.kind == "row":
        s = tn // rows
        assert s * rows == tn
        spec = pl.BlockSpec((s, None, rows, tk), lambda *g: (idx(*g)[0], lay, 0, idx(*g)[1]))
        return b.arr, spec, lambda av, ref: _dot_nt(av, ref[...].reshape(s * rows, tk))
    if mode == "nt" and b.kind == "col":
        s = tk // cols
        assert s * cols == tk
        spec = pl.BlockSpec((s, None, tn, cols), lambda *g: (idx(*g)[1], lay, idx(*g)[0], 0))

        def prod(av, ref):
            tot = _dot_nt(av[:, :cols], ref[0])
            for c in range(1, s):
                tot = tot + _dot_nt(av[:, c * cols:(c + 1) * cols], ref[c])
            return tot

        return b.arr, spec, prod
    raise NotImplementedError((mode, b.kind))


def matmul(a, b, mode, name, out_dtypes, epilogue=None, extras=(), a_fn=None, out_shards=False, tm=1024, tn=1024, tk=1024):
    (m, k) = a.shape[::-1] if mode == "tn" else a.shape
    n = b.shape[0] if mode == "nt" else b.shape[1]
    if out_shards:
        tn = n // N_DEV
    tm, tn, tk = _fit(m, tm), _fit(n, tn), _fit(k, tk)
    nk = k // tk
    n_ex, n_out = len(extras), len(out_dtypes)
    if epilogue is None:
        epilogue = lambda acc: (acc,)
    b_arr, b_spec, prod = _b_operand(b, mode, tn, tk, lambda i, j, kk: (j, kk))

    def body(*refs):
        a_ref, b_ref = refs[0], refs[1]
        ex_refs = refs[2:2 + n_ex]
        out_refs = refs[2 + n_ex:2 + n_ex + n_out]
        acc_ref = refs[2 + n_ex + n_out] if nk > 1 else None
        kk = pl.program_id(2)
        av = _bf(a_ref[...])
        if a_fn is not None:
            av = a_fn(av)
        part = prod(av, b_ref)

        def finish(acc):
            outs = epilogue(acc, *[r[...] for r in ex_refs])
            for r, o in zip(out_refs, outs):
                r[...] = o.astype(r.dtype)

        if nk == 1:
            finish(part)
            return

        @pl.when(kk == 0)
        def _():
            acc_ref[...] = part

        @pl.when(kk > 0)
        def _():
            acc_ref[...] += part

        @pl.when(kk == nk - 1)
        def _():
            finish(acc_ref[...])

    if mode == "tn":
        a_spec = pl.BlockSpec((tk, tm), lambda i, j, kk: (kk, i))
    else:
        a_spec = pl.BlockSpec((tm, tk), lambda i, j, kk: (i, kk))
    mn_spec = pl.BlockSpec((tm, tn), lambda i, j, kk: (i, j))
    row_spec = pl.BlockSpec((1, tn), lambda i, j, kk: (0, j))
    if out_shards:
        out_spec = pl.BlockSpec((None, tm, tn), lambda i, j, kk: (j, i, 0))
        out_shape = [jax.ShapeDtypeStruct((N_DEV, m, tn), dt) for dt in out_dtypes]
    else:
        out_spec = mn_spec
        out_shape = [jax.ShapeDtypeStruct((m, n), dt) for dt in out_dtypes]
    outs = pl.pallas_call(
        body, name=name, grid=(m // tm, n // tn, nk),
        in_specs=[a_spec, b_spec] + [row_spec if e.shape[0] == 1 else mn_spec for e in extras],
        out_specs=[out_spec] * n_out,
        out_shape=out_shape,
        scratch_shapes=[pltpu.VMEM((tm, tn), F32)] if nk > 1 else [],
        compiler_params=_cparams(("parallel", "parallel", "arbitrary")),
    )(a, b_arr, *extras)
    return outs


def matmul_nt_normbwd(dz, w, x, g, dres, name, tm=512):
    m, k = dz.shape
    d = w.shape[0]
    tm = _fit(m, tm)
    w_arr, w_spec, prod = _b_operand(w, "nt", d, k, lambda i: (0, 0))

    def body(dz_ref, w_ref, x_ref, g_ref, dres_ref, dx_ref, dxb_ref, dg_ref):
        @pl.when(pl.program_id(0) == 0)
        def _():
            dg_ref[...] = jnp.zeros_like(dg_ref)

        dh = prod(_bf(dz_ref[...]), w_ref)
        xf = x_ref[...]
        r = lax.rsqrt(jnp.mean(xf * xf, axis=-1, keepdims=True) + EPS)
        xhat = xf * r
        dg_ref[...] += jnp.sum(dh * xhat, axis=0, keepdims=True)
        dxh = dh * g_ref[...]
        dx = r * (dxh - xhat * jnp.mean(dxh * xhat, axis=-1, keepdims=True))
        out = dres_ref[...] + dx
        dx_ref[...] = out
        dxb_ref[...] = out.astype(BF16)

    row = pl.BlockSpec((tm, d), lambda i: (i, 0))
    vec = pl.BlockSpec((1, d), lambda i: (0, 0))
    return pl.pallas_call(
        body, name=name, grid=(m // tm,),
        in_specs=[pl.BlockSpec((tm, k), lambda i: (i, 0)), w_spec, row, vec, row],
        out_specs=[row, row, vec],
        out_shape=[jax.ShapeDtypeStruct((m, d), F32), jax.ShapeDtypeStruct((m, d), BF16), jax.ShapeDtypeStruct((1, d), F32)],
        compiler_params=_cparams(("arbitrary",)),
    )(dz, w_arr, x, g.reshape(1, d), dres)


def _rope_tables(t):
    pos = lax.broadcasted_iota(jnp.int32, (t, LANES), 0)
    dim = lax.broadcasted_iota(jnp.int32, (t, LANES), 1) % HEAD_DIM

    def table(p, width):
        i = dim % (width // 2)
        ang = p.astype(F32) * (ROPE_THETA ** (-(2 * i).astype(F32) / width))
        return jnp.cos(ang), jnp.where(dim % width < width // 2, -jnp.sin(ang), jnp.sin(ang))

    cos_a, sin_a = table(pos, HEAD_DIM)
    cos_b, sin_b = table(jnp.where(dim < HEAD_DIM // 2, pos // GRID_W, pos % GRID_W), HEAD_DIM // 2)
    return cos_a, sin_a, cos_b, sin_b


def _headnorm(xs, gmat):
    return lax.rsqrt(_dot_f32_by_ones(xs * xs, gmat) * (1.0 / HEAD_DIM) + EPS)


def qkv_post_fwd(proj, tables, qn_g, kn_g, name):
    t = proj.shape[0]
    tm = min(t, 256)
    cos_a, sin_a, cos_b, sin_b = tables
    g2 = lambda g: jnp.concatenate([g, g]).reshape(1, LANES)

    def body(p_ref, ca_ref, sa_ref, cb_ref, sb_ref, qg_ref, kg_ref, qa_ref, ka_ref, va_ref, qb_ref, kb_ref, vb_ref):
        ca, sa, cb, sb = ca_ref[...], sa_ref[...], cb_ref[...], sb_ref[...]
        gmat = _group_matrix(HEAD_DIM)

        def rope_a(xs):
            return xs * ca + _swap_halves(xs, HEAD_DIM) * sa

        def norm_rope_b(xs, g):
            y = xs * _headnorm(xs, gmat) * g
            return y * cb + _swap_halves(y, HEAD_DIM // 2) * sb

        for c in range(A_Q // LANES):
            qa_ref[:, c * LANES:(c + 1) * LANES] = rope_a(p_ref[:, OFF_QA + c * LANES:OFF_QA + (c + 1) * LANES]).astype(BF16)
        ka_ref[...] = rope_a(p_ref[:, OFF_KA:OFF_KA + LANES]).astype(BF16)
        va_ref[...] = p_ref[:, OFF_VA:OFF_VA + LANES].astype(BF16)
        for c in range(B_Q // LANES):
            qb_ref[:, c * LANES:(c + 1) * LANES] = norm_rope_b(
                p_ref[:, OFF_QB + c * LANES:OFF_QB + (c + 1) * LANES], qg_ref[...]).astype(BF16)
        kb_ref[...] = norm_rope_b(p_ref[:, OFF_KB:OFF_KB + LANES], kg_ref[...]).astype(BF16)
        vb_ref[...] = p_ref[:, OFF_VB:OFF_VB + LANES].astype(BF16)

    tab = pl.BlockSpec((tm, LANES), lambda i: (i, 0))
    vec = pl.BlockSpec((1, LANES), lambda i: (0, 0))
    wide = pl.BlockSpec((tm, A_Q), lambda i: (i, 0))
    return pl.pallas_call(
        body, name=name, grid=(t // tm,),
        in_specs=[pl.BlockSpec((tm, ATT_IN), lambda i: (i, 0)), tab, tab, tab, tab, vec, vec],
        out_specs=[wide, tab, tab, wide, tab, tab],
        out_shape=[jax.ShapeDtypeStruct((t, A_Q), BF16), jax.ShapeDtypeStruct((t, LANES), BF16),
                   jax.ShapeDtypeStruct((t, LANES), BF16), jax.ShapeDtypeStruct((t, B_Q), BF16),
                   jax.ShapeDtypeStruct((t, LANES), BF16), jax.ShapeDtypeStruct((t, LANES), BF16)],
        compiler_params=_cparams(("parallel",)),
    )(proj, cos_a, sin_a, cos_b, sin_b, g2(qn_g), g2(kn_g))


def qkv_post_bwd(proj, tables, qn_g, kn_g, dqa, dka, dva, dqb, dkb, dvb, name):
    t = proj.shape[0]
    tm = min(t, 256)
    cos_a, sin_a, cos_b, sin_b = tables
    g2 = lambda g: jnp.concatenate([g, g]).reshape(1, LANES)

    def body(p_ref, ca_ref, sa_ref, cb_ref, sb_ref, qg_ref, kg_ref, dqa_ref, dka_ref, dva_ref, dqb_ref, dkb_ref, dvb_ref,
             dp_ref, dqg_ref, dkg_ref):
        ca, sa, cb, sb = ca_ref[...], sa_ref[...], cb_ref[...], sb_ref[...]
        gmat = _group_matrix(HEAD_DIM)

        @pl.when(pl.program_id(0) == 0)
        def _():
            dqg_ref[...] = jnp.zeros_like(dqg_ref)
            dkg_ref[...] = jnp.zeros_like(dkg_ref)

        def rope_a_bwd(dy):
            return dy * ca + _swap_halves(dy * sa, HEAD_DIM)

        def norm_rope_b_bwd(dout, xs, g):
            dy = dout * cb + _swap_halves(dout * sb, HEAD_DIM // 2)
            r = _headnorm(xs, gmat)
            xhat = xs * r
            dxh = dy * g
            mean = _dot_f32_by_ones(dxh * xhat, gmat) * (1.0 / HEAD_DIM)
            return r * (dxh - xhat * mean), jnp.sum(dy * xhat, axis=0, keepdims=True)

        for c in range(A_Q // LANES):
            sl = slice(c * LANES, (c + 1) * LANES)
            dp_ref[:, OFF_QA + c * LANES:OFF_QA + (c + 1) * LANES] = rope_a_bwd(dqa_ref[:, sl].astype(F32)).astype(BF16)
        dp_ref[:, OFF_KA:OFF_KA + LANES] = rope_a_bwd(dka_ref[0] + dka_ref[1]).astype(BF16)
        dp_ref[:, OFF_VA:OFF_VA + LANES] = (dva_ref[0] + dva_ref[1]).astype(BF16)
        dqg = jnp.zeros((1, LANES), F32)
        for c in range(B_Q // LANES):
            sl = slice(c * LANES, (c + 1) * LANES)
            dx, dg = norm_rope_b_bwd(dqb_ref[:, sl].astype(F32), p_ref[:, OFF_QB + c * LANES:OFF_QB + (c + 1) * LANES], qg_ref[...])
            dp_ref[:, OFF_QB + c * LANES:OFF_QB + (c + 1) * LANES] = dx.astype(BF16)
            dqg = dqg + dg
        dqg_ref[...] += dqg
        dx, dg = norm_rope_b_bwd((dkb_ref[0] + dkb_ref[1]).T, p_ref[:, OFF_KB:OFF_KB + LANES], kg_ref[...])
        dp_ref[:, OFF_KB:OFF_KB + LANES] = dx.astype(BF16)
        dkg_ref[...] += dg
        dp_ref[:, OFF_VB:OFF_VB + LANES] = (dvb_ref[0] + dvb_ref[1]).T.astype(BF16)

        @pl.when(pl.program_id(0) == t // tm - 1)
        def _():
            dqg_ref[...] = dqg_ref[...] + pltpu.roll(dqg_ref[...], HEAD_DIM, 1)
            dkg_ref[...] = dkg_ref[...] + pltpu.roll(dkg_ref[...], HEAD_DIM, 1)

    tab = pl.BlockSpec((tm, LANES), lambda i: (i, 0))
    vec = pl.BlockSpec((1, LANES), lambda i: (0, 0))
    wide = pl.BlockSpec((tm, A_Q), lambda i: (i, 0))
    slab = pl.BlockSpec((2, tm, LANES), lambda i: (0, i, 0))
    per_chunk = dkb.shape[3] // tm
    slab_t = pl.BlockSpec((2, None, LANES, tm), lambda i: (0, i // per_chunk, 0, i % per_chunk))
    full = pl.BlockSpec((tm, ATT_IN), lambda i: (i, 0))
    return pl.pallas_call(
        body, name=name, grid=(t // tm,),
        in_specs=[full, tab, tab, tab, tab, vec, vec, wide, slab, slab, wide, slab_t, slab_t],
        out_specs=[full, vec, vec],
        out_shape=[jax.ShapeDtypeStruct((t, ATT_IN), BF16), jax.ShapeDtypeStruct((1, LANES), F32),
                   jax.ShapeDtypeStruct((1, LANES), F32)],
        compiler_params=_cparams(("arbitrary",)),
    )(proj, cos_a, sin_a, cos_b, sin_b, g2(qn_g), g2(kn_g), dqa, dka, dva, dqb, dkb, dvb)


def _head_to_half(xs, head_half, kv_half):
    low = _lane(xs.shape) < HEAD_DIM
    kept = jnp.where(low if head_half == 0 else jnp.logical_not(low), xs, 0.0)
    return jnp.where(kv_half == head_half, kept, pltpu.roll(kept, HEAD_DIM, 1))


def _halves_to_heads(r0, r1, kv_half):
    low = _lane(r0.shape) < HEAD_DIM
    a = jnp.where(kv_half == 0, r0, pltpu.roll(r0, HEAD_DIM, 1))
    b = jnp.where(kv_half == 1, r1, pltpu.roll(r1, HEAD_DIM, 1))
    return jnp.where(low, a, b)


def _head_deltas(o2, do2):
    s = o2.astype(F32) * do2.astype(F32)
    return _dot_f32_by_ones(s, _seg_matrix(0, HEAD_DIM)), _dot_f32_by_ones(s, _seg_matrix(HEAD_DIM, LANES))


def attn_delta(o, do, name):
    t, w = o.shape
    tm = min(t, 512)
    n_heads = w // HEAD_DIM

    def body(o_ref, do_ref, d_ref):
        lo, hi = _seg_matrix(0, HEAD_DIM), _seg_matrix(HEAD_DIM, LANES)
        for c in range(w // LANES):
            sl = slice(c * LANES, (c + 1) * LANES)
            s = o_ref[:, sl].astype(F32) * do_ref[:, sl].astype(F32)
            d_ref[2 * c] = _dot_f32_by_ones(s, lo)
            d_ref[2 * c + 1] = _dot_f32_by_ones(s, hi)

    blk = pl.BlockSpec((tm, w), lambda i: (i, 0))
    return pl.pallas_call(
        body, name=name, grid=(t // tm,),
        in_specs=[blk, blk],
        out_specs=pl.BlockSpec((n_heads, tm, LANES), lambda i: (0, i, 0)),
        out_shape=jax.ShapeDtypeStruct((n_heads, t, LANES), F32),
        compiler_params=_cparams(("parallel",)),
    )(o, do)


BAND = 3 * BLOCK


def _band_offsets(rows_rep):
    qi = lax.broadcasted_iota(jnp.int32, (BLOCK, BAND), 0)
    kj = lax.broadcasted_iota(jnp.int32, (BLOCK, BAND), 1)
    return jnp.concatenate([kj - qi] * rows_rep, axis=0)


def _band(n, t, offsets):
    start = pl.multiple_of(jnp.clip((n - 1) * BLOCK, 0, t - BAND), BLOCK)
    return start, jnp.abs(offsets + (start - n * BLOCK)) <= WINDOW


def window_attn_fwd(q, k, v, sink, name, blocks_per_step=8):
    t = q.shape[0]
    assert t >= BAND
    nq = _fit(t // BLOCK, blocks_per_step)
    tq = nq * BLOCK

    def body(sink_ref, q_ref, k_ref, v_ref, o_ref, lse_ref):
        j, n0 = pl.program_id(0), pl.program_id(1)
        kvh = j // 2
        row = lax.broadcasted_iota(jnp.int32, (2 * BLOCK, 1), 0)
        sk = jnp.where(row < BLOCK, sink_ref[2 * j], sink_ref[2 * j + 1]) * LOG2E
        offsets = _band_offsets(2)
        bands, scores = [], []
        for u in range(nq):
            start, ok = _band(n0 * nq + u, t, offsets)
            qf = q_ref[u * BLOCK:(u + 1) * BLOCK, :].astype(F32) * (SCALE * LOG2E)
            qs = jnp.concatenate([_head_to_half(qf, 0, kvh), _head_to_half(qf, 1, kvh)], axis=0).astype(BF16)
            bands.append(pl.ds(start, BAND))
            scores.append(jnp.where(ok, _dot_nt(qs, k_ref[bands[u], :]), NEG))
        soft = []
        for s in scores:
            m = jnp.maximum(jnp.max(s, axis=-1, keepdims=True), sk)
            p = jnp.exp2(s - m)
            soft.append((p.astype(BF16), jnp.sum(p, axis=-1, keepdims=True) + jnp.exp2(sk - m), m))
        for u, (p, denom, m) in enumerate(soft):
            rows = slice(u * BLOCK, (u + 1) * BLOCK)
            o = _dot_nn(p, v_ref[bands[u], :]) / denom
            o_ref[rows, :] = _halves_to_heads(o[:BLOCK], o[BLOCK:], kvh).astype(BF16)
            lse = jnp.broadcast_to(m + jnp.log2(denom), (2 * BLOCK, LANES))
            lse_ref[0, rows, :] = lse[:BLOCK]
            lse_ref[1, rows, :] = lse[BLOCK:]

    qspec = pl.BlockSpec((tq, LANES), lambda j, n: (n, j))
    whole = pl.BlockSpec((t, LANES), lambda j, n: (0, 0))
    return pl.pallas_call(
        body, name=name, grid=(A_HEADS // 2, t // tq),
        in_specs=[pl.BlockSpec(memory_space=pltpu.SMEM), qspec, whole, whole],
        out_specs=[qspec, pl.BlockSpec((2, tq, LANES), lambda j, n: (j, n, 0))],
        out_shape=[jax.ShapeDtypeStruct((t, A_Q + B_Q), BF16), jax.ShapeDtypeStruct((A_HEADS, t, LANES), F32)],
        compiler_params=_cparams(("parallel", "parallel")),
    )(sink, q, k, v)


def window_attn_bwd(q, k, v, sink, do, o, lse, name, blocks_per_step=4):
    t = q.shape[0]
    assert t >= BAND
    nq = _fit(t // BLOCK, blocks_per_step)
    tq = nq * BLOCK
    grp = A_HEADS // A_KV_HEADS
    gw = grp * HEAD_DIM

    def body(sink_ref, q_ref, do_ref, o_ref, k_ref, v_ref, lse_ref, dq_ref, dk_ref, dv_ref, ds_ref):
        kvh, n0 = pl.program_id(0), pl.program_id(1)

        @pl.when(n0 == 0)
        def _():
            dk_ref[...] = jnp.zeros_like(dk_ref)
            dv_ref[...] = jnp.zeros_like(dv_ref)
            ds_ref[...] = jnp.zeros_like(ds_ref)

        rid = lax.broadcasted_iota(jnp.int32, (8, LANES), 0)
        upd = jnp.zeros((8, LANES), F32)
        offsets = _band_offsets(grp)
        for u in range(nq):
            rows = slice(u * BLOCK, (u + 1) * BLOCK)
            start, ok = _band(n0 * nq + u, t, offsets)
            band = pl.ds(start, BAND)
            qparts, doparts = [], []
            for hh in range(grp):
                sl = slice((hh // 2) * LANES, (hh // 2 + 1) * LANES)
                qparts.append(_head_to_half(q_ref[rows, sl].astype(F32) * (SCALE * LOG2E), hh % 2, kvh))
                doparts.append(_head_to_half(do_ref[rows, sl].astype(F32), hh % 2, kvh))
            qs = jnp.concatenate(qparts, axis=0).astype(BF16)
            dos = jnp.concatenate(doparts, axis=0).astype(BF16)
            lse_b = jnp.concatenate([lse_ref[hh, rows, :] for hh in range(grp)], axis=0)
            dl_b = jnp.concatenate([d for c in range(grp // 2) for d in _head_deltas(
                o_ref[rows, c * LANES:(c + 1) * LANES], do_ref[rows, c * LANES:(c + 1) * LANES])], axis=0)
            kband, vband = k_ref[band, :], v_ref[band, :]
            s = jnp.where(ok, _dot_nt(qs, kband), NEG)
            p = jnp.exp2(s - lse_b[:, :1])
            dp = _dot_nt(dos, vband)
            dsc = (p * (dp - dl_b[:, :1])).astype(BF16)
            dv_ref[0, band, :] += _dot_tn(p.astype(BF16), dos)
            dk_ref[0, band, :] += _dot_tn(dsc, qs) * LN2
            dq = _dot_nn(dsc, kband) * SCALE
            for c in range(grp // 2):
                dq_ref[rows, c * LANES:(c + 1) * LANES] = _halves_to_heads(
                    dq[2 * c * BLOCK:(2 * c + 1) * BLOCK], dq[(2 * c + 1) * BLOCK:(2 * c + 2) * BLOCK], kvh).astype(dq_ref.dtype)
            for hh in range(grp):
                rs = slice(hh * BLOCK, (hh + 1) * BLOCK)
                tot = jnp.sum(jnp.exp2(sink_ref[kvh * grp + hh] * LOG2E - lse_b[rs]) * dl_b[rs], axis=0, keepdims=True)
                upd = upd + jnp.where(rid == hh, -tot, 0.0)
        ds_ref[0] += upd

    qspec = pl.BlockSpec((tq, gw), lambda kvh, n: (n, kvh))
    whole = pl.BlockSpec((t, LANES), lambda kvh, n: (0, 0))
    stat = pl.BlockSpec((grp, tq, LANES), lambda kvh, n: (kvh, n, 0))
    slab = pl.BlockSpec((1, t, LANES), lambda kvh, n: (kvh, 0, 0))
    return pl.pallas_call(
        body, name=name, grid=(A_KV_HEADS, t // tq),
        in_specs=[pl.BlockSpec(memory_space=pltpu.SMEM), qspec, qspec, qspec, whole, whole, stat],
        out_specs=[qspec, slab, slab, pl.BlockSpec((1, 8, LANES), lambda kvh, n: (kvh, 0, 0))],
        out_shape=[jax.ShapeDtypeStruct((t, A_Q), BF16), jax.ShapeDtypeStruct((A_KV_HEADS, t, LANES), F32),
                   jax.ShapeDtypeStruct((A_KV_HEADS, t, LANES), F32), jax.ShapeDtypeStruct((A_KV_HEADS, 8, LANES), F32)],
        compiler_params=_cparams(("arbitrary", "arbitrary")),
    )(sink, q, do, o, k, v, lse)


def flash_attn_fwd(q, k, v, cat, name, exchange=None, tq=1024, tk=512, ahead=2):
    t = q.shape[0]
    tq, tk = _fit(t, tq), _fit(t, tk)
    nk = t // tk

    def body(q_ref, k_ref, v_ref, cat_ref, o_ref, lse_ref):
        del cat_ref
        kvh = pl.program_id(0) // 2
        qf = q_ref[...].astype(F32) * (SCALE * LOG2E)
        qs = jnp.concatenate([_head_to_half(qf, 0, kvh), _head_to_half(qf, 1, kvh)], axis=0).astype(BF16)
        mine = (_lane((tk, LANES)) < HEAD_DIM) == (kvh == 0)

        def scores(c):
            return _dot_nt(qs, k_ref[c * tk:(c + 1) * tk, :])

        s = [scores(c) for c in range(min(ahead, nk))]
        m = jnp.full((2 * tq, 1), NEG, F32)
        acc = jnp.zeros((2 * tq, LANES), F32)
        for c in range(nk):
            if c + ahead < nk:
                s.append(scores(c + ahead))
            vb = jnp.where(mine, v_ref[c * tk:(c + 1) * tk, :], jnp.ones((), BF16))
            m_new = jnp.maximum(m, jnp.max(s[c], axis=-1, keepdims=True))
            p = jnp.exp2(s[c] - m_new).astype(BF16)
            acc = jnp.exp2(m - m_new) * acc + _dot_nn(p, vb)
            m = m_new
        other = pltpu.roll(acc, HEAD_DIM, 1)
        o = acc / other
        o_ref[...] = _halves_to_heads(o[:tq], o[tq:], kvh).astype(BF16)
        in_mine = (_lane(acc.shape) < HEAD_DIM) == (kvh == 0)
        lse = jnp.broadcast_to(m, acc.shape) + jnp.log2(jnp.where(in_mine, other, acc))
        lse_ref[0] = lse[:tq]
        lse_ref[1] = lse[tq:]

    qspec = pl.BlockSpec((tq, LANES), lambda j, i: (i, j))
    whole = pl.BlockSpec((t, LANES), lambda j, i: (0, 0))
    nj, ni = B_HEADS // 2, t // tq
    steps = lambda: ((pl.program_id(0) == 0) & (pl.program_id(1) == 0), (pl.program_id(0) == nj - 1) & (pl.program_id(1) == ni - 1))
    body, x_in, x_out, x_shapes, x_scratch = carried(body, exchange, 4, 2, steps)
    return pl.pallas_call(
        body, name=name, grid=(nj, ni),
        in_specs=[qspec, whole, whole, _ANY] + x_in,
        out_specs=[pl.BlockSpec((tq, LANES), lambda j, i: (i, A_Q // LANES + j)),
                   pl.BlockSpec((2, tq, LANES), lambda j, i: (j, i, 0))] + x_out,
        out_shape=[jax.ShapeDtypeStruct(cat.shape, BF16), jax.ShapeDtypeStruct((B_HEADS, t, LANES), F32)] + x_shapes,
        scratch_shapes=x_scratch,
        input_output_aliases={3: 0},
        compiler_params=_cparams(("arbitrary", "arbitrary")),
    )(q, k, v, cat, *(exchange.arrays if exchange else ()))


def flash_attn_bwd(q, k, v, do, o, lse, name, exchange=None, tq=512, tk=512, together=4):
    t = q.shape[0]
    tq, tk = _fit(t, tq), _fit(t, tk)
    nk = t // tk
    together = _fit(nk, together)
    grp = B_HEADS // B_KV_HEADS
    gw = grp * HEAD_DIM

    def body(q_ref, do_ref, o_ref, k_ref, v_ref, lse_ref, dq_ref, dk_ref, dv_ref, dq_s):
        kvh, i = pl.program_id(0), pl.program_id(1)

        @pl.when(i == 0)
        def _():
            dk_ref[...] = jnp.zeros_like(dk_ref)
            dv_ref[...] = jnp.zeros_like(dv_ref)

        qparts, doparts = [], []
        for hh in range(grp):
            sl = slice((hh // 2) * LANES, (hh // 2 + 1) * LANES)
            qparts.append(_head_to_half(q_ref[:, sl].astype(F32) * (SCALE * LOG2E), hh % 2, kvh))
            doparts.append(_head_to_half(do_ref[:, sl].astype(F32), hh % 2, kvh))
        qf, dof = jnp.concatenate(qparts, axis=0), jnp.concatenate(doparts, axis=0)
        qs, dos = qf.astype(BF16), dof.astype(BF16)
        qs_t, dos_t = qf.T.astype(BF16), dof.T.astype(BF16)
        lse = jnp.tile(jnp.concatenate([lse_ref[hh] for hh in range(grp)], axis=0), (1, tk // LANES))
        dl = jnp.tile(jnp.concatenate([d for c in range(grp // 2) for d in _head_deltas(
            o_ref[:, c * LANES:(c + 1) * LANES], do_ref[:, c * LANES:(c + 1) * LANES])], axis=0), (1, tk // LANES))
        dq_s[...] = jnp.zeros_like(dq_s)

        def chunks(c0, carry):
            cs = [c0 * together + u for u in range(together)]
            kbs = [k_ref[pl.ds(pl.multiple_of(c * tk, tk), tk), :] for c in cs]
            vbs = [v_ref[pl.ds(pl.multiple_of(c * tk, tk), tk), :] for c in cs]
            ss = [_dot_nt(qs, kb) for kb in kbs]
            dps = [_dot_nt(dos, vb) for vb in vbs]
            for c, kb, s, dp in zip(cs, kbs, ss, dps):
                p = jnp.exp2(s - lse)
                dsc = (p * (dp - dl)).astype(BF16)
                dv_ref[0, c] += _dot_nn(dos_t, p.astype(BF16))
                dk_ref[0, c] += _dot_nn(qs_t, dsc) * LN2
                dq_s[...] += _dot_nn(kb.T, dsc.T)
            return carry

        lax.fori_loop(0, nk // together, chunks, 0)
        dq = dq_s[...].T
        for c in range(grp // 2):
            dq_ref[:, c * LANES:(c + 1) * LANES] = (_halves_to_heads(
                dq[2 * c * tq:(2 * c + 1) * tq], dq[(2 * c + 1) * tq:(2 * c + 2) * tq], kvh) * SCALE).astype(dq_ref.dtype)

    qspec = pl.BlockSpec((tq, gw), lambda kvh, i: (i, kvh))
    dospec = pl.BlockSpec((tq, gw), lambda kvh, i: (i, A_Q // gw + kvh))
    whole = pl.BlockSpec((t, LANES), lambda kvh, i: (0, 0))
    stat = pl.BlockSpec((grp, tq, LANES), lambda kvh, i: (kvh, i, 0))
    slab = pl.BlockSpec((1, nk, LANES, tk), lambda kvh, i: (kvh, 0, 0, 0))
    ni = t // tq
    steps = lambda: ((pl.program_id(0) == 0) & (pl.program_id(1) == 0),
                     (pl.program_id(0) == B_KV_HEADS - 1) & (pl.program_id(1) == ni - 1))
    body, x_in, x_out, x_shapes, x_scratch = carried(body, exchange, 6, 3, steps)
    return pl.pallas_call(
        body, name=name, grid=(B_KV_HEADS, ni),
        in_specs=[qspec, dospec, dospec, whole, whole, stat] + x_in,
        out_specs=[qspec, slab, slab] + x_out,
        out_shape=[jax.ShapeDtypeStruct((t, B_Q), BF16), jax.ShapeDtypeStruct((B_KV_HEADS, nk, LANES, tk), F32),
                   jax.ShapeDtypeStruct((B_KV_HEADS, nk, LANES, tk), F32)] + x_shapes,
        scratch_shapes=[pltpu.VMEM((LANES, grp * tq), F32)] + x_scratch,
        compiler_params=_cparams(("arbitrary", "arbitrary")),
    )(q, do, o, k, v, lse, *(exchange.arrays if exchange else ()))


_GELU_C = math.sqrt(2.0 / math.pi)
_GELU_A = 0.044715


def _gelu(x):
    return 0.5 * x * (1.0 + jnp.tanh(_GELU_C * (x + _GELU_A * x * x * x)))


def _gelu_grad(x):
    th = jnp.tanh(_GELU_C * (x + _GELU_A * x * x * x))
    return 0.5 * (1.0 + th) + 0.5 * x * (1.0 - th * th) * _GELU_C * (1.0 + 3.0 * _GELU_A * x * x)


def _layernorm_stats(vf):
    mu = jnp.mean(vf, axis=-1, keepdims=True)
    vc = vf - mu
    r = lax.rsqrt(jnp.mean(vc * vc, axis=-1, keepdims=True) + EPS)
    return vc * r, r


def sgu_mix_fwd(z, ln_g, ln_b, w_s, b_rows, name):
    t, w2 = z.shape
    w = w2 // 2
    dg = w // SGU_GROUPS

    def body(u_ref, v_ref, g_ref, b_ref, ws_ref, bb_ref, y_ref):
        vhat, _ = _layernorm_stats(v_ref[...].astype(F32))
        vn = (vhat * g_ref[...] + b_ref[...]).astype(BF16)
        for g in range(SGU_GROUPS):
            sl = slice(g * dg, (g + 1) * dg)
            mixed = _dot_nn(ws_ref[g], vn[:, sl]) + bb_ref[g]
            y_ref[:, sl] = (u_ref[:, sl].astype(F32) * mixed).astype(BF16)

    vec = pl.BlockSpec((1, w), lambda n: (0, 0))
    whole = pl.BlockSpec((SGU_GROUPS, SGU_CHUNK, SGU_CHUNK), lambda n: (0, 0, 0))
    return pl.pallas_call(
        body, name=name, grid=(t // SGU_CHUNK,),
        in_specs=[pl.BlockSpec((SGU_CHUNK, w), lambda n: (n, 0)), pl.BlockSpec((SGU_CHUNK, w), lambda n: (n, 1)),
                  vec, vec, whole, whole],
        out_specs=pl.BlockSpec((SGU_CHUNK, w), lambda n: (n, 0)),
        out_shape=jax.ShapeDtypeStruct((t, w), BF16),
        compiler_params=_cparams(("parallel",)),
    )(z, z, ln_g.reshape(1, w), ln_b.reshape(1, w), w_s, b_rows)


def sgu_mix_bwd(z, apre, dy, ln_g, ln_b, w_s, b_rows, name):
    t, w2 = z.shape
    w = w2 // 2
    dg = w // SGU_GROUPS

    def body(u_ref, v_ref, au_ref, av_ref, dy_ref, g_ref, b_ref, ws_ref, bb_ref, da_ref, dlg_ref, dlb_ref, dws_ref, dbs_ref):
        @pl.when(pl.program_id(0) == 0)
        def _():
            dlg_ref[...] = jnp.zeros_like(dlg_ref)
            dlb_ref[...] = jnp.zeros_like(dlb_ref)
            dws_ref[...] = jnp.zeros_like(dws_ref)
            dbs_ref[...] = jnp.zeros_like(dbs_ref)

        vhat, r = _layernorm_stats(v_ref[...].astype(F32))
        gam = g_ref[...]
        vn = (vhat * gam + b_ref[...]).astype(BF16)
        ones8 = jnp.ones((8, dg), BF16)
        rid = lax.broadcasted_iota(jnp.int32, (8, SGU_CHUNK), 0)
        dbs = jnp.zeros((8, SGU_CHUNK), F32)
        dvn_parts = []
        for g in range(SGU_GROUPS):
            sl = slice(g * dg, (g + 1) * dg)
            dyg = dy_ref[:, sl].astype(F32)
            mixed = _dot_nn(ws_ref[g], vn[:, sl]) + bb_ref[g]
            da_ref[:, sl] = (dyg * mixed * _gelu_grad(au_ref[:, sl].astype(F32))).astype(BF16)
            dmix = dyg * u_ref[:, sl].astype(F32)
            dm_hi = dmix.astype(BF16)
            dm_lo = (dmix - dm_hi.astype(F32)).astype(BF16)
            dws_ref[g] += _dot_nt(dm_hi, vn[:, sl])
            dbs = dbs + jnp.where(rid == g, _dot_nt(ones8, dm_hi) + _dot_nt(ones8, dm_lo), 0.0)
            dvn_parts.append(_dot_tn(ws_ref[g], dm_hi))
        dbs_ref[...] += dbs
        dvn = jnp.concatenate(dvn_parts, axis=1)
        dlg_ref[...] += jnp.sum(dvn * vhat, axis=0, keepdims=True)
        dlb_ref[...] += jnp.sum(dvn, axis=0, keepdims=True)
        dvh = dvn * gam
        dv = r * (dvh - jnp.mean(dvh, axis=-1, keepdims=True) - vhat * jnp.mean(dvh * vhat, axis=-1, keepdims=True))
        da_ref[:, w:] = (dv * _gelu_grad(av_ref[...].astype(F32))).astype(BF16)

    vec = pl.BlockSpec((1, w), lambda n: (0, 0))
    whole = pl.BlockSpec((SGU_GROUPS, SGU_CHUNK, SGU_CHUNK), lambda n: (0, 0, 0))
    left = pl.BlockSpec((SGU_CHUNK, w), lambda n: (n, 0))
    right = pl.BlockSpec((SGU_CHUNK, w), lambda n: (n, 1))
    return pl.pallas_call(
        body, name=name, grid=(t // SGU_CHUNK,),
        in_specs=[left, right, left, right, left, vec, vec, whole, whole],
        out_specs=[pl.BlockSpec((SGU_CHUNK, w2), lambda n: (n, 0)), vec, vec, whole,
                   pl.BlockSpec((SGU_GROUPS, SGU_CHUNK), lambda n: (0, 0))],
        out_shape=[jax.ShapeDtypeStruct((t, w2), BF16), jax.ShapeDtypeStruct((1, w), F32), jax.ShapeDtypeStruct((1, w), F32),
                   jax.ShapeDtypeStruct((SGU_GROUPS, SGU_CHUNK, SGU_CHUNK), F32),
                   jax.ShapeDtypeStruct((SGU_GROUPS, SGU_CHUNK), F32)],
        compiler_params=_cparams(("arbitrary",)),
    )(z, z, apre, apre, dy, ln_g.reshape(1, w), ln_b.reshape(1, w), w_s, b_rows)


def loss_head(h, g, target, name):
    t, d = h.shape
    tm = min(t, 512)

    def body(h_ref, g_ref, t_ref, loss_ref, dh_ref, dhb_ref, dg_ref):
        @pl.when(pl.program_id(0) == 0)
        def _():
            loss_ref[...] = jnp.zeros_like(loss_ref)
            dg_ref[...] = jnp.zeros_like(dg_ref)

        xf = h_ref[...]
        r = lax.rsqrt(jnp.mean(xf * xf, axis=-1, keepdims=True) + EPS)
        xhat = xf * r
        err = xhat * g_ref[...] - t_ref[...]
        per_tok = jnp.mean(err * err, axis=-1, keepdims=True)
        loss_ref[...] += 0.5 * jnp.sum(per_tok, axis=0, keepdims=True)
        dy = err * (1.0 / d)
        dg_ref[...] += jnp.sum(dy * xhat, axis=0, keepdims=True)
        dxh = dy * g_ref[...]
        dh = r * (dxh - xhat * jnp.mean(dxh * xhat, axis=-1, keepdims=True))
        dh_ref[...] = dh
        dhb_ref[...] = dh.astype(BF16)

    row = pl.BlockSpec((tm, d), lambda i: (i, 0))
    vec = pl.BlockSpec((1, d), lambda i: (0, 0))
    return pl.pallas_call(
        body, name=name, grid=(t // tm,),
        in_specs=[row, vec, row],
        out_specs=[pl.BlockSpec((1, LANES), lambda i: (0, 0)), row, row, vec],
        out_shape=[jax.ShapeDtypeStruct((1, LANES), F32), jax.ShapeDtypeStruct((t, d), F32), jax.ShapeDtypeStruct((t, d), BF16),
                   jax.ShapeDtypeStruct((1, d), F32)],
        compiler_params=_cparams(("arbitrary",)),
    )(h, g.reshape(1, d), target)


ADAMW_BLOCK_BYTES = 1 << 20


def adamw(parts, w, m, v, name):
    n_layers, r, c = w.shape
    row_bytes = n_layers * c * 4
    if r * row_bytes <= 2 * ADAMW_BLOCK_BYTES:
        tr = r
    else:
        tr = _fit(r, 1 << int(math.log2(max(8, ADAMW_BLOCK_BYTES // row_bytes))))
    bc1 = 1.0 - ADAM_B1 ** ADAM_STEP
    bc2 = 1.0 - ADAM_B2 ** ADAM_STEP

    def body(*refs):
        p_refs = refs[:n_layers]
        w_ref, m_ref, v_ref, g_ref, d_ref, nm_ref, nv_ref = refs[n_layers:]
        for l in range(n_layers):
            g = p_refs[l][0].astype(F32)
            for j in range(1, N_DEV):
                g = g + p_refs[l][j].astype(F32)
            nm = ADAM_B1 * m_ref[l] + (1.0 - ADAM_B1) * g
            nv = ADAM_B2 * v_ref[l] + (1.0 - ADAM_B2) * (g * g)
            g_ref[l] = g
            nm_ref[l] = nm
            nv_ref[l] = nv
            d_ref[l] = -ADAM_LR * ((nm / bc1) / (jnp.sqrt(nv / bc2) + ADAM_EPS) + ADAM_WD * w_ref[l])

    blk = pl.BlockSpec((n_layers, tr, c), lambda i: (0, i, 0))
    return pl.pallas_call(
        body, name=name, grid=(r // tr,),
        in_specs=[pl.BlockSpec((N_DEV, tr, c), lambda i: (0, i, 0))] * n_layers + [blk, blk, blk],
        out_specs=[blk] * 4,
        out_shape=[jax.ShapeDtypeStruct((n_layers, r, c), F32)] * 4,
        compiler_params=_cparams(("parallel",)),
    )(*parts, w, m, v)


_ANY = pl.BlockSpec(memory_space=pl.ANY)


def _mesh_pos():
    return lax.axis_index("x"), lax.axis_index("y"), lax.axis_index("c")


class Exchange:
    def __init__(self, gathers=(), scatters=()):
        self.items = [("gather", a) for a in gathers] + [("scatter", a) for a in scatters]
        self.arrays = [a for _, a in self.items]
        self.n = len(self.items)

    def out_shapes(self):
        return [jax.ShapeDtypeStruct(((N_DEV,) + a.shape) if kind == "gather" else a.shape, a.dtype) for kind, a in self.items]

    def scratch(self):
        return [pltpu.SemaphoreType.DMA((7 * self.n,)), pltpu.SemaphoreType.DMA((7 * self.n,)), pltpu.SemaphoreType.DMA((self.n,))]

    def _copies(self, in_refs, out_refs, send_sems, recv_sems, local_sems):
        x, y, c = _mesh_pos()
        me = 4 * x + 2 * y + c
        local, sends, arrivals = [], [], []
        for t, (kind, _) in enumerate(self.items):
            src_of = (lambda slot, r=in_refs[t]: r) if kind == "gather" else (lambda slot, r=in_refs[t]: r.at[slot])
            local.append(pltpu.make_async_copy(src_of(me), out_refs[t].at[me], local_sems.at[t]))
            for k in range(1, N_DEV):
                px = 1 - x if k & 4 else x
                py = 1 - y if k & 2 else y
                pc = 1 - c if k & 1 else c
                pid = 4 * px + 2 * py + pc
                kw = dict(send_sem=send_sems.at[7 * t + k - 1], recv_sem=recv_sems.at[7 * t + k - 1],
                          device_id=(px, py, pc), device_id_type=pl.DeviceIdType.MESH)
                sends.append(pltpu.make_async_remote_copy(src_ref=src_of(pid), dst_ref=out_refs[t].at[me], **kw))
                arrivals.append(pltpu.make_async_remote_copy(src_ref=src_of(pid), dst_ref=out_refs[t].at[pid], **kw))
        return local, sends, arrivals

    def start(self, *refs):
        local, sends, _ = self._copies(*refs)
        for cp in local + sends:
            cp.start()

    def wait(self, *refs):
        local, sends, arrivals = self._copies(*refs)
        for cp in arrivals:
            cp.wait_recv()
        for cp in sends:
            cp.wait_send()
        for cp in local:
            cp.wait()


def carried(body, exchange, n_in, n_out, first_last):
    if exchange is None:
        return body, [], [], [], []
    nx = exchange.n

    def wrapped(*refs):
        ins, xin = refs[:n_in], refs[n_in:n_in + nx]
        outs, xout = refs[n_in + nx:n_in + nx + n_out], refs[n_in + nx + n_out:n_in + 2 * nx + n_out]
        scratch, sems = refs[n_in + 2 * nx + n_out:-3], refs[-3:]
        first, last = first_last()

        @pl.when(first)
        def _():
            exchange.start(xin, xout, *sems)

        body(*ins, *outs, *scratch)

        @pl.when(last)
        def _():
            exchange.wait(xin, xout, *sems)

    return wrapped, [_ANY] * nx, [_ANY] * nx, exchange.out_shapes(), exchange.scratch()


def exchange_only(exchange, name):
    def body(*refs):
        xin, xout, sems = refs[:exchange.n], refs[exchange.n:2 * exchange.n], refs[-3:]
        exchange.start(xin, xout, *sems)
        exchange.wait(xin, xout, *sems)

    return pl.pallas_call(
        body, name=name, in_specs=[_ANY] * exchange.n, out_specs=[_ANY] * exchange.n,
        out_shape=exchange.out_shapes(), scratch_shapes=exchange.scratch(),
    )(*exchange.arrays)


def _residual_out(a, w_out, x, next_g, name, **tiles):
    if next_g is None:
        (y,) = matmul(a, w_out, "nn", name, [F32], epilogue=lambda acc, r: (r + acc,), extras=(x,), **tiles)
        return y, None

    def add_and_norm(acc, r, g):
        y = r + acc
        return y, y * lax.rsqrt(jnp.mean(y * y, axis=-1, keepdims=True) + EPS) * g

    assert w_out.shape[1] <= tiles.get("tn", 1024)
    return matmul(a, w_out, "nn", name, [F32, BF16], epilogue=add_and_norm, extras=(x, next_g.reshape(1, -1)), **tiles)


def attention_fwd(x, h, w_in, sink, qn_g, kn_g, w_out, tables, next_g, tag, exchange=None):
    (proj,) = matmul(h, w_in, "nn", f"{tag}_proj", [F32], tn=ATT_IN)
    qa, ka, va, qb, kb, vb = qkv_post_fwd(proj, tables, qn_g, kn_g, f"{tag}_qkv")
    cat, lse_a = window_attn_fwd(qa, ka, va, sink, f"{tag}_win")
    cat, lse_b, *arrived = flash_attn_fwd(qb, kb, vb, cat, f"{tag}_flash", exchange)
    if callable(w_out):
        w_out = w_out(arrived)
    y, h_next = _residual_out(cat, w_out, x, next_g, f"{tag}_out")
    saved = (x, h, proj, qa, ka, va, qb, kb, vb, cat, lse_a, lse_b)
    return y, h_next, saved, arrived


def attention_bwd(dy, dyb, saved, norm_g, w_in, sink, qn_g, kn_g, w_out, tables, tag, exchange_with=None):
    x, h, proj, qa, ka, va, qb, kb, vb, cat, lse_a, lse_b = saved
    (dcat,) = matmul(dyb, w_out, "nt", f"{tag}_dcat", [BF16])
    (dw_out,) = matmul(cat, dyb, "tn", f"{tag}_dwout", [BF16], tk=4096)
    dqa, dka, dva, dsink = window_attn_bwd(qa, ka, va, sink, dcat, cat, lse_a, f"{tag}_dwin")
    exchange = exchange_with(dw_out) if exchange_with else None
    dqb, dkb, dvb, *arrived = flash_attn_bwd(qb, kb, vb, dcat, cat, lse_b, f"{tag}_dflash", exchange)
    dproj, dqg, dkg = qkv_post_bwd(proj, tables, qn_g, kn_g, dqa, dka, dva, dqb, dkb, dvb, f"{tag}_dqkv")
    (dw_in,) = matmul(h, dproj, "tn", f"{tag}_dwin_w", [BF16], tn=ATT_IN // 2, tk=2048)
    dx, dxb, dg = matmul_nt_normbwd(dproj, w_in, x, norm_g, dy, f"{tag}_dx")
    grp = A_HEADS // A_KV_HEADS
    small = dict(norm=dg[0], sink=dsink[:, :grp, 0].reshape(A_HEADS), qnorm=dqg[0, :HEAD_DIM], knorm=dkg[0, :HEAD_DIM])
    return dx, dxb, dw_in, dw_out, small, arrived


def sgu_fwd(x, h, w_in, ln_g, ln_b, w_s, b_rows, w_out, next_g, tag):
    apre, z = matmul(h, w_in, "nn", f"{tag}_in", [BF16, BF16], epilogue=lambda acc: (acc, _gelu(acc)))
    y = sgu_mix_fwd(z, ln_g, ln_b, w_s, b_rows, f"{tag}_mix")
    out, h_next = _residual_out(y, w_out, x, next_g, f"{tag}_out")
    return out, h_next, (x, h, apre, z, y)


def sgu_bwd(dout, doutb, saved, norm_g, w_in, ln_g, ln_b, w_s, b_rows, w_out, tag):
    x, h, apre, z, y = saved
    (dy,) = matmul(doutb, w_out, "nt", f"{tag}_dy", [BF16])
    (dw_out,) = matmul(y, doutb, "tn", f"{tag}_dwout", [BF16], tk=4096)
    dapre, dlg, dlb, dws, dbs = sgu_mix_bwd(z, apre, dy, ln_g, ln_b, w_s, b_rows, f"{tag}_dmix")
    (dw_in,) = matmul(h, dapre, "tn", f"{tag}_dwin", [BF16], out_shards=True, tk=4096)
    dx, dxb, dg = matmul_nt_normbwd(dapre, w_in, x, norm_g, dout, f"{tag}_dx")
    small = dict(norm=dg[0], ln_g=dlg[0], ln_b=dlb[0], w_s=dws, b_s=dbs)
    return dx, dxb, dw_in, dw_out, small


def _square(r):
    return r * r


def mlp_fwd(x, h, w1, w2, next_g, tag):
    (r,) = matmul(h, w1, "nn", f"{tag}_up", [BF16], epilogue=lambda acc: (jnp.maximum(acc, 0.0),), tm=2048)
    y, h_next = _residual_out(r, w2, x, next_g, f"{tag}_down", a_fn=_square, tm=512, tk=4096)
    return y, h_next, (x, h, r)


def mlp_bwd(dy, dyb, saved, norm_g, w1, w2, tag):
    x, h, r = saved
    (da,) = matmul(dyb, w2, "nt", f"{tag}_da", [BF16], epilogue=lambda acc, rr: (acc * (2.0 * rr.astype(F32)),), extras=(r,),
                   tm=2048)
    (dw2,) = matmul(r, dyb, "tn", f"{tag}_dw2", [BF16], a_fn=_square, tk=4096)
    (dw1,) = matmul(h, da, "tn", f"{tag}_dw1", [BF16], out_shards=True, tk=4096)
    dx, dxb, dg = matmul_nt_normbwd(da, w1, x, norm_g, dy, f"{tag}_dx")
    return dx, dxb, dw1, dw2, dg[0]


ORDER = ("att_norm", "att_w_in", "att_sink", "att_qnorm", "att_knorm", "att_w_out", "sgu_norm", "sgu_w_in", "sgu_ln_g",
         "sgu_ln_b", "sgu_w_s", "sgu_b_s", "sgu_w_out", "mlp_norm", "mlp_w1", "mlp_w2", "final_norm")
SHARDED = ("att_w_in", "att_w_out", "sgu_w_in", "sgu_w_out", "mlp_w1", "mlp_w2")
SGU_VECS = ("sgu_norm", "sgu_ln_g", "sgu_ln_b")
SMALL_EARLY = ("sgu_w_s", "sgu_b_s", "mlp_norm", "final_norm", "loss")
SMALL_LATE = ("att_norm", "att_sink", "att_qnorm", "att_knorm")
SMALL_ROWS_MULT = 8


def _flat(blocks, names):
    flat = jnp.concatenate([blocks[n].reshape(-1).astype(F32) for n in names])
    per = SMALL_ROWS_MULT * FLAT_COLS
    total = -(-flat.shape[0] // per) * per
    return jnp.pad(flat, (0, total - flat.shape[0])).reshape(1, total // FLAT_COLS, FLAT_COLS)


def _unflat(flat, like, names):
    out, off = {}, 0
    f = flat.reshape(-1)
    for n in names:
        size = like[n].size
        out[n] = f[off:off + size].reshape(like[n].shape)
        off += size
    return out


def kernel(x, att_norm, att_w_in, att_sink, att_qnorm, att_knorm, att_w_out, sgu_norm, sgu_w_in, sgu_ln_g, sgu_ln_b, sgu_w_s, sgu_b_s, sgu_w_out, mlp_norm, mlp_w1, mlp_w2, final_norm, loss_target, m_att_norm, m_att_w_in, m_att_sink, m_att_qnorm, m_att_knorm, m_att_w_out, m_sgu_norm, m_sgu_w_in, m_sgu_ln_g, m_sgu_ln_b, m_sgu_w_s, m_sgu_b_s, m_sgu_w_out, m_mlp_norm, m_mlp_w1, m_mlp_w2, m_final_norm, v_att_norm, v_att_w_in, v_att_sink, v_att_qnorm, v_att_knorm, v_att_w_out, v_sgu_norm, v_sgu_w_in, v_sgu_ln_g, v_sgu_ln_b, v_sgu_w_s, v_sgu_b_s, v_sgu_w_out, v_mlp_norm, v_mlp_w1, v_mlp_w2, v_final_norm):
    w = dict(att_norm=att_norm, att_w_in=att_w_in, att_sink=att_sink, att_qnorm=att_qnorm, att_knorm=att_knorm,
             att_w_out=att_w_out, sgu_norm=sgu_norm, sgu_w_in=sgu_w_in, sgu_ln_g=sgu_ln_g, sgu_ln_b=sgu_ln_b, sgu_w_s=sgu_w_s,
             sgu_b_s=sgu_b_s, sgu_w_out=sgu_w_out, mlp_norm=mlp_norm, mlp_w1=mlp_w1, mlp_w2=mlp_w2, final_norm=final_norm)
    m = dict(att_norm=m_att_norm, att_w_in=m_att_w_in, att_sink=m_att_sink, att_qnorm=m_att_qnorm, att_knorm=m_att_knorm,
             att_w_out=m_att_w_out, sgu_norm=m_sgu_norm, sgu_w_in=m_sgu_w_in, sgu_ln_g=m_sgu_ln_g, sgu_ln_b=m_sgu_ln_b,
             sgu_w_s=m_sgu_w_s, sgu_b_s=m_sgu_b_s, sgu_w_out=m_sgu_w_out, mlp_norm=m_mlp_norm, mlp_w1=m_mlp_w1, mlp_w2=m_mlp_w2,
             final_norm=m_final_norm)
    v = dict(att_norm=v_att_norm, att_w_in=v_att_w_in, att_sink=v_att_sink, att_qnorm=v_att_qnorm, att_knorm=v_att_knorm,
             att_w_out=v_att_w_out, sgu_norm=v_sgu_norm, sgu_w_in=v_sgu_w_in, sgu_ln_g=v_sgu_ln_g, sgu_ln_b=v_sgu_ln_b,
             sgu_w_s=v_sgu_w_s, sgu_b_s=v_sgu_b_s, sgu_w_out=v_sgu_w_out, mlp_norm=v_mlp_norm, mlp_w1=v_mlp_w1, mlp_w2=v_mlp_w2,
             final_norm=v_final_norm)
    loss, grad_x, g, d, nm, nv = train_step(x[0], loss_target[0], w, m, v)
    return (loss, grad_x[None], *[g[n] for n in ORDER], *[d[n] for n in ORDER], *[nm[n] for n in ORDER], *[nv[n] for n in ORDER])


def train_step(x, target, w, m, v):
    t, d_model = x.shape
    n_att, n_sgu, depth = w["att_w_in"].shape[0], w["sgu_w_in"].shape[0], w["mlp_w1"].shape[0]
    bf = lambda n: w[n].astype(BF16)

    assert n_sgu == n_att and depth == 2 * n_att
    vec_local = jnp.stack([w[n] for n in SGU_VECS], axis=1)
    att_in = bf("att_w_in")
    h, g_in0, g_vec = rmsnorm_fwd(x, w["att_norm"][0], "att0_norm", Exchange(gathers=[att_in[:1], vec_local]))
    vecs = g_vec.transpose(1, 2, 0, 3).reshape(n_sgu, len(SGU_VECS), -1)
    group_names = ("att_w_out", "sgu_w_in", "sgu_w_out", "mlp_w1", "mlp_w2")
    per = {n: w[n].shape[0] // n_att for n in group_names}
    rest = [Exchange(gathers=([att_in[1:]] if gi == 0 else []) + [bf(n)[gi * per[n]:(gi + 1) * per[n]] for n in group_names])
            for gi in range(n_att)]
    gathered = {}

    def weight(name, kind, layer):
        return Gathered(gathered[name][layer // per[name]], kind, layer % per[name])

    w_s_bf = w["sgu_w_s"].astype(BF16)
    b_rows = jnp.broadcast_to(w["sgu_b_s"][:, :, :, None], w["sgu_b_s"].shape + (LANES,))
    tables = _rope_tables(t)
    full_cols = lambda g: g.transpose(1, 2, 0, 3).reshape(g.shape[1], d_model, -1)

    mixer_norm = lambda layer: w["att_norm"][layer // 2] if layer % 2 == 0 else vecs[layer // 2, 0]
    saved = []
    for layer in range(depth):
        i = layer // 2
        if layer % 2 == 0:
            if layer == 0:
                att_w_in = [full_cols(g_in0)[0]]
            first = 1 if layer == 0 else 0
            x, h, sv, arrived = attention_fwd(x, h, att_w_in[i], w["att_sink"][i], w["att_qnorm"][i], w["att_knorm"][i],
                                              lambda arrived, first=first: Gathered(arrived[first], "row", 0),
                                              tables, w["mlp_norm"][layer], f"att{i}", rest[i])
            if layer == 0:
                att_w_in += list(full_cols(arrived[0]))
            for n, g in zip(group_names, arrived[first:]):
                gathered.setdefault(n, []).append(g)
        else:
            x, h, sv = sgu_fwd(x, h, weight("sgu_w_in", "col", i), vecs[i, 1], vecs[i, 2], w_s_bf[i], b_rows[i],
                               weight("sgu_w_out", "row", i), w["mlp_norm"][layer], f"sgu{i}")
        x, h, sm = mlp_fwd(x, h, weight("mlp_w1", "col", layer), weight("mlp_w2", "row", layer),
                           mixer_norm(layer + 1) if layer + 1 < depth else None, f"mlp{layer}")
        saved.append((sv, sm))
    loss_row, dh, dhb, dgf = loss_head(x, w["final_norm"], target, "loss_head")

    queue, recv = [], {}
    gs = dict(att_norm=[None] * n_att, att_sink=[None] * n_att, att_qnorm=[None] * n_att, att_knorm=[None] * n_att,
              sgu_w_s=[None] * n_sgu, sgu_b_s=[None] * n_sgu, mlp_norm=[None] * depth)

    def row_slabs(g):
        return g.reshape(N_DEV, g.shape[0] // N_DEV, g.shape[1])

    def col_slabs(g):
        return g.reshape(g.shape[0], N_DEV, g.shape[1] // N_DEV).transpose(1, 0, 2)

    def take_queue(gathers=()):
        items = list(queue)
        queue.clear()
        keys = [k for k, _ in gathers] + [k for k, _ in items]
        return Exchange(gathers=[a for _, a in gathers], scatters=[a for _, a in items]), keys

    def small_early():
        blocks = dict(sgu_w_s=jnp.stack(gs["sgu_w_s"]), sgu_b_s=jnp.stack(gs["sgu_b_s"]), mlp_norm=jnp.stack(gs["mlp_norm"]),
                      final_norm=dgf[0], loss=loss_row[0, :1])
        return _flat(blocks, SMALL_EARLY)[0]

    for layer in reversed(range(depth)):
        i = layer // 2
        sv, sm = saved[layer]
        dh, dhb, dw1, dw2, gs["mlp_norm"][layer] = mlp_bwd(
            dh, dhb, sm, w["mlp_norm"][layer], weight("mlp_w1", "col", layer), weight("mlp_w2", "row", layer), f"mlp{layer}")
        queue += [(("mlp_w1", layer), dw1), (("mlp_w2", layer), row_slabs(dw2))]
        if layer % 2 == 0:
            keys = []

            def exchange_with(dw_out, i=i, layer=layer, keys=keys):
                if layer == 0:
                    queue.append((("att_w_out", i), row_slabs(dw_out)))
                ex, got = take_queue([("small_early", small_early())] if layer == 0 else ())
                keys += got
                return ex

            dh, dhb, dw_in, dw_out, sm_g, arrived = attention_bwd(
                dh, dhb, sv, w["att_norm"][i], att_w_in[i], w["att_sink"][i], w["att_qnorm"][i], w["att_knorm"][i],
                weight("att_w_out", "row", i), tables, f"att{i}", exchange_with)
            recv.update(zip(keys, arrived))
            queue.append((("att_w_in", i), col_slabs(dw_in)))
            if layer != 0:
                queue.append((("att_w_out", i), row_slabs(dw_out)))
            gs["att_norm"][i], gs["att_sink"][i] = sm_g["norm"], sm_g["sink"]
            gs["att_qnorm"][i], gs["att_knorm"][i] = sm_g["qnorm"], sm_g["knorm"]
        else:
            dh, dhb, dw_in, dw_out, sm_g = sgu_bwd(
                dh, dhb, sv, vecs[i, 0], weight("sgu_w_in", "col", i), vecs[i, 1], vecs[i, 2], w_s_bf[i], b_rows[i],
                weight("sgu_w_out", "row", i), f"sgu{i}")
            dvec = jnp.stack([sm_g["norm"], sm_g["ln_g"], sm_g["ln_b"]])
            queue += [(("sgu_w_in", i), dw_in), (("sgu_w_out", i), row_slabs(dw_out)), (("sgu_vecs", i), col_slabs(dvec))]
            gs["sgu_w_s"][i], gs["sgu_b_s"][i] = sm_g["w_s"], sm_g["b_s"]
    grad_x = dh
    late = dict(att_norm=jnp.stack(gs["att_norm"]), att_sink=jnp.stack(gs["att_sink"]), att_qnorm=jnp.stack(gs["att_qnorm"]),
                att_knorm=jnp.stack(gs["att_knorm"]))
    last, keys = take_queue([("small_late", _flat(late, SMALL_LATE)[0])])
    recv.update(zip(keys, exchange_only(last, "exchange_last")))

    outs = [{}, {}, {}, {}]
    for n in SHARDED:
        res = adamw([recv[(n, l)] for l in range(w[n].shape[0])], w[n], m[n], v[n], f"adamw_{n}")
        for o, r in zip(outs, res):
            o[n] = r
    stack_vecs = lambda src: jnp.stack([src[n] for n in SGU_VECS], axis=1)
    res = adamw([recv[("sgu_vecs", i)] for i in range(n_sgu)], stack_vecs(w), stack_vecs(m), stack_vecs(v), "adamw_sgu_vecs")
    for o, r in zip(outs, res):
        o.update({n: r[:, k] for k, n in enumerate(SGU_VECS)})
    zero = {"loss": jnp.zeros((1,), F32)}
    for names, key in ((SMALL_EARLY, "small_early"), (SMALL_LATE, "small_late")):
        res = adamw([recv[key]], _flat({**w, **zero}, names), _flat({**m, **zero}, names), _flat({**v, **zero}, names), f"adamw_{key}")
        for o, r in zip(outs, res):
            o.update(_unflat(r, {**w, **zero}, names))
    loss = outs[0]["loss"][0]
    return loss, grad_x, *outs
```
